```python
import math
import jax
import jax.numpy as jnp
from jax import lax
import numpy as np

D_MODEL = 1024
BATCH = 8
SEQ = 8192
DEPTH = 2

NORM_EPS = 1e-6
POOL_WINDOWS = (2, 4, 8, 16)
POOL_GROUPS = 4
POOL_WIDTH = D_MODEL // 2
POOL_GROUP_DIM = POOL_WIDTH // POOL_GROUPS
ATTN_HEADS = 8
ATTN_HEAD_DIM = 64
ATTN_WIDTH = ATTN_HEADS * ATTN_HEAD_DIM
Q_BLOCK = 128
SSD_HEAD_DIM = 64
SSD_WIDTH = D_MODEL
SSD_HEADS = SSD_WIDTH // SSD_HEAD_DIM
SSD_GROUPS = 2
SSD_STATE = 128
SSD_CONV = 4
SSD_CHUNK = 128
SSD_CONV_CH = SSD_WIDTH + 2 * SSD_GROUPS * SSD_STATE
N_BRANCH = 3
FFN_DIM = 2816
FFN_CONV = 3
IN_SPLITS = (POOL_WIDTH, ATTN_WIDTH, ATTN_WIDTH, ATTN_WIDTH, ATTN_HEADS,
             SSD_WIDTH, SSD_CONV_CH, SSD_HEADS, N_BRANCH * D_MODEL)
IN_TOTAL = sum(IN_SPLITS)

kernel_name = "hybrid_pool_fox_ssd_block"


def rmsnorm(x, w):
    xf = x.astype(jnp.float32)
    y = xf * lax.rsqrt(jnp.mean(xf * xf, axis=-1, keepdims=True) + NORM_EPS)
    return (y * w.astype(jnp.float32)).astype(x.dtype)


def split_columns(proj):
    offsets, acc = [], 0
    for n in IN_SPLITS[:-1]:
        acc += n
        offsets.append(acc)
    return jnp.split(proj, offsets, axis=-1)


def causal_dwconv(x, w, b):
    K, C = w.shape
    y = lax.conv_general_dilated(
        x, w[:, None, :].astype(x.dtype), window_strides=(1,),
        padding=[(K - 1, 0)], dimension_numbers=('NWC', 'WIO', 'NWC'),
        feature_group_count=C)
    return y + b.astype(x.dtype)


def pool_mixer(v, mix_w, scale):
    B_, S, _ = v.shape
    vf = v.astype(jnp.float32).reshape(B_, S, POOL_GROUPS, POOL_GROUP_DIM)
    cs = jnp.cumsum(vf, axis=1)
    t = jnp.arange(S)
    pooled = []
    for g, w in enumerate(POOL_WINDOWS):
        c = cs[:, :, g]
        lag = jnp.pad(c, ((0, 0), (w, 0), (0, 0)))[:, :S]
        cnt = jnp.minimum(t + 1, w).astype(jnp.float32)[None, :, None]
        pooled.append((c - lag) / cnt)
    d = (jnp.stack(pooled, axis=2) - vf).astype(v.dtype)
    y = jnp.einsum('bsgc,gcd->bsgd', d, mix_w)
    return y.reshape(B_, S, POOL_WIDTH) * scale


def forgetting_attention(q, k, v, f_logit, f_bias):
    B_, S, _ = q.shape
    H, Dh = ATTN_HEADS, ATTN_HEAD_DIM
    q = q.reshape(B_, S, H, Dh).transpose(0, 2, 1, 3)
    k = k.reshape(B_, S, H, Dh).transpose(0, 2, 1, 3)
    v = v.reshape(B_, S, H, Dh).transpose(0, 2, 1, 3)
    logf = jax.nn.log_sigmoid((f_logit + f_bias).astype(jnp.float32))
    c = jnp.cumsum(logf, axis=1).transpose(0, 2, 1)
    nb = S // Q_BLOCK
    qb = q.reshape(B_, H, nb, Q_BLOCK, Dh).transpose(2, 0, 1, 3, 4)
    cb = c.reshape(B_, H, nb, Q_BLOCK).transpose(2, 0, 1, 3)
    kpos = jnp.arange(S)
    scale = Dh ** -0.5

    def block(args):
        qi, ci, i = args
        qpos = i * Q_BLOCK + jnp.arange(Q_BLOCK)
        s = jnp.einsum('bhqd,bhkd->bhqk', qi, k).astype(jnp.float32) * scale
        s = s + (ci[..., :, None] - c[:, :, None, :])
        s = jnp.where(kpos[None, :] <= qpos[:, None], s, -jnp.inf)
        p = jax.nn.softmax(s, axis=-1).astype(v.dtype)
        return jnp.einsum('bhqk,bhkd->bhqd', p, v)

    o = lax.map(block, (qb, cb, jnp.arange(nb)))
    return o.transpose(1, 0, 3, 2, 4).reshape(B_, S, H * Dh)


def segsum(a):
    L = a.shape[-1]
    cs = jnp.cumsum(a, axis=-1)
    seg = cs[..., :, None] - cs[..., None, :]
    mask = jnp.tril(jnp.ones((L, L), dtype=bool))
    return jnp.where(mask, seg, -jnp.inf)


def ssd_scan(x, dt, A, Bm, Cm):
    B_, S = x.shape[:2]
    L, G, N, P = SSD_CHUNK, SSD_GROUPS, SSD_STATE, SSD_HEAD_DIM
    E = SSD_HEADS // G
    nc = S // L
    xd = (x * dt[..., None]).reshape(B_, nc, L, G, E, P)
    a = (dt * A).reshape(B_, nc, L, G, E).transpose(0, 3, 4, 1, 2)
    Bc = Bm.reshape(B_, nc, L, G, N)
    Cc = Cm.reshape(B_, nc, L, G, N)
    a_cs = jnp.cumsum(a, axis=-1)
    Lmat = jnp.exp(segsum(a))
    cb = jnp.einsum('bclgn,bcsgn->bgcls', Cc, Bc)
    y_diag = jnp.einsum('bgcls,bgecls,bcsgep->bclgep', cb, Lmat, xd)
    decay = jnp.exp(a_cs[..., -1:] - a_cs)
    states = jnp.einsum('bclgn,bgecl,bclgep->cbgepn', Bc, decay, xd)
    chunk_decay = jnp.exp(a_cs[..., -1]).transpose(3, 0, 1, 2)

    def step(h, inp):
        s_c, d_c = inp
        return d_c[..., None, None] * h + s_c, h

    h0 = jnp.zeros(states.shape[1:], x.dtype)
    _, prev = lax.scan(step, h0, (states, chunk_decay))
    y_off = jnp.einsum('bclgn,cbgepn,bgecl->bclgep', Cc, prev, jnp.exp(a_cs))
    return (y_diag + y_off).reshape(B_, S, SSD_HEADS, P)


def ssd_mixer(z, xbc, dt_raw, conv_w, conv_b, dt_bias, a_log, d_skip, norm_w):
    B_, S, _ = z.shape
    f32 = jnp.float32
    xbc = jax.nn.silu(causal_dwconv(xbc, conv_w, conv_b))
    xs, Bm, Cm = jnp.split(xbc, [SSD_WIDTH, SSD_WIDTH + SSD_GROUPS * SSD_STATE], axis=-1)
    dt = jax.nn.softplus(dt_raw.astype(f32) + dt_bias.astype(f32))
    A = -jnp.exp(a_log.astype(f32))
    x = xs.astype(f32).reshape(B_, S, SSD_HEADS, SSD_HEAD_DIM)
    y = ssd_scan(x, dt, A,
                 Bm.astype(f32).reshape(B_, S, SSD_GROUPS, SSD_STATE),
                 Cm.astype(f32).reshape(B_, S, SSD_GROUPS, SSD_STATE))
    y = y + d_skip.astype(f32)[:, None] * x
    y = y.reshape(B_, S, SSD_WIDTH) * jax.nn.silu(z.astype(f32))
    yg = y.reshape(B_, S, SSD_GROUPS, SSD_WIDTH // SSD_GROUPS)
    yg = yg * lax.rsqrt(jnp.mean(yg * yg, axis=-1, keepdims=True) + NORM_EPS)
    return (yg.reshape(B_, S, SSD_WIDTH) * norm_w.astype(f32)).astype(z.dtype)


def conv_ffn(u, w_up, conv_w, conv_b, w_down):
    h = causal_dwconv(u @ w_up, conv_w, conv_b)
    g, val = jnp.split(h, 2, axis=-1)
    return (jax.nn.silu(g) * val) @ w_down


def _fwd_setup_inputs(seed: int = 0) -> dict:
    key = jax.random.key(seed)
    ks = jax.random.split(key, 24)
    f32 = jnp.float32
    L = DEPTH

    def nrm(k, shape, scale):
        return jax.random.normal(k, shape, f32) * scale

    def gain(k, shape):
        return 1.0 + 0.02 * jax.random.normal(k, shape, f32)

    dt0 = jnp.exp(jax.random.uniform(ks[9], (L, SSD_HEADS), f32,
                                     math.log(1e-3), math.log(1e-1)))
    return {
        'x': jax.random.normal(ks[0], (BATCH, SEQ, D_MODEL), f32),
        'norm_mix': gain(ks[1], (L, D_MODEL)),
        'w_in': nrm(ks[2], (L, D_MODEL, IN_TOTAL), D_MODEL ** -0.5),
        'pool_mix': nrm(ks[3], (L, POOL_GROUPS, POOL_GROUP_DIM, POOL_GROUP_DIM), POOL_GROUP_DIM ** -0.5),
        'pool_scale': 1.0 + 0.1 * jax.random.normal(ks[4], (L, POOL_WIDTH), f32),
        'f_bias': jax.random.uniform(ks[5], (L, ATTN_HEADS), f32, 1.0, 4.0),
        'ssd_conv_w': nrm(ks[6], (L, SSD_CONV, SSD_CONV_CH), SSD_CONV ** -0.5),
        'ssd_conv_b': nrm(ks[7], (L, SSD_CONV_CH), 0.02),
        'ssd_dt_bias': dt0 + jnp.log(-jnp.expm1(-dt0)),
        'ssd_a_log': jnp.log(jax.random.uniform(ks[10], (L, SSD_HEADS), f32, 1.0, 16.0)),
        'ssd_d': 1.0 + 0.1 * jax.random.normal(ks[11], (L, SSD_HEADS), f32),
        'ssd_norm': gain(ks[12], (L, SSD_WIDTH)),
        'p_pool': nrm(ks[13], (L, POOL_WIDTH, D_MODEL), POOL_WIDTH ** -0.5),
        'p_attn': nrm(ks[14], (L, ATTN_WIDTH, D_MODEL), ATTN_WIDTH ** -0.5),
        'p_ssd': nrm(ks[15], (L, SSD_WIDTH, D_MODEL), SSD_WIDTH ** -0.5),
        'w_out': nrm(ks[16], (L, D_MODEL, D_MODEL), D_MODEL ** -0.5),
        'norm_ffn': gain(ks[17], (L, D_MODEL)),
        'ffn_up': nrm(ks[18], (L, D_MODEL, 2 * FFN_DIM), D_MODEL ** -0.5),
        'ffn_conv_w': nrm(ks[19], (L, FFN_CONV, 2 * FFN_DIM), FFN_CONV ** -0.5),
        'ffn_conv_b': nrm(ks[20], (L, 2 * FFN_DIM), 0.02),
        'ffn_down': nrm(ks[21], (L, FFN_DIM, D_MODEL), FFN_DIM ** -0.5),
        'norm_final': gain(ks[22], (D_MODEL,)),
    }


def _fwd_reference(x, norm_mix, w_in, pool_mix, pool_scale, f_bias, ssd_conv_w, ssd_conv_b,
              ssd_dt_bias, ssd_a_log, ssd_d, ssd_norm, p_pool, p_attn, p_ssd, w_out,
              norm_ffn, ffn_up, ffn_conv_w, ffn_conv_b, ffn_down, norm_final):
    B_, S, D = x.shape
    for l in range(DEPTH):
        u = rmsnorm(x, norm_mix[l])
        (pool_v, q, k, v, f_logit, z, xbc, dt_raw, gate_logits) = split_columns(u @ w_in[l])
        y_pool = pool_mixer(pool_v, pool_mix[l], pool_scale[l]) @ p_pool[l]
        y_attn = forgetting_attention(q, k, v, f_logit, f_bias[l]) @ p_attn[l]
        y_ssd = ssd_mixer(z, xbc, dt_raw, ssd_conv_w[l], ssd_conv_b[l], ssd_dt_bias[l],
                          ssd_a_log[l], ssd_d[l], ssd_norm[l]) @ p_ssd[l]
        gates = jax.nn.sigmoid(gate_logits.astype(jnp.float32)).astype(x.dtype)
        gates = gates.reshape(B_, S, N_BRANCH, D)
        merged = gates[:, :, 0] * y_pool + gates[:, :, 1] * y_attn + gates[:, :, 2] * y_ssd
        x = x + merged @ w_out[l]
        x = x + conv_ffn(rmsnorm(x, norm_ffn[l]), ffn_up[l], ffn_conv_w[l], ffn_conv_b[l], ffn_down[l])
    return rmsnorm(x, norm_final)


import jax as _jax
import jax.numpy as _jnp

TWIN_FORMAT = 'train_step'
FWD_PARAMS = ['x', 'norm_mix', 'w_in', 'pool_mix', 'pool_scale', 'f_bias', 'ssd_conv_w', 'ssd_conv_b', 'ssd_dt_bias', 'ssd_a_log', 'ssd_d', 'ssd_norm', 'p_pool', 'p_attn', 'p_ssd', 'w_out', 'norm_ffn', 'ffn_up', 'ffn_conv_w', 'ffn_conv_b', 'ffn_down', 'norm_final']
TWIN_WEIGHTS = ['norm_mix', 'w_in', 'pool_mix', 'pool_scale', 'f_bias', 'ssd_conv_w', 'ssd_conv_b', 'ssd_dt_bias', 'ssd_a_log', 'ssd_d', 'ssd_norm', 'p_pool', 'p_attn', 'p_ssd', 'w_out', 'norm_ffn', 'ffn_up', 'ffn_conv_w', 'ffn_conv_b', 'ffn_down', 'norm_final']
TWIN_DIFF_INPUT = 'x'
TWIN_INPUTS = ['x', 'norm_mix', 'w_in', 'pool_mix', 'pool_scale', 'f_bias', 'ssd_conv_w', 'ssd_conv_b', 'ssd_dt_bias', 'ssd_a_log', 'ssd_d', 'ssd_norm', 'p_pool', 'p_attn', 'p_ssd', 'w_out', 'norm_ffn', 'ffn_up', 'ffn_conv_w', 'ffn_conv_b', 'ffn_down', 'norm_final', 'loss_target', 'm_norm_mix', 'm_w_in', 'm_pool_mix', 'm_pool_scale', 'm_f_bias', 'm_ssd_conv_w', 'm_ssd_conv_b', 'm_ssd_dt_bias', 'm_ssd_a_log', 'm_ssd_d', 'm_ssd_norm', 'm_p_pool', 'm_p_attn', 'm_p_ssd', 'm_w_out', 'm_norm_ffn', 'm_ffn_up', 'm_ffn_conv_w', 'm_ffn_conv_b', 'm_ffn_down', 'm_norm_final', 'v_norm_mix', 'v_w_in', 'v_pool_mix', 'v_pool_scale', 'v_f_bias', 'v_ssd_conv_w', 'v_ssd_conv_b', 'v_ssd_dt_bias', 'v_ssd_a_log', 'v_ssd_d', 'v_ssd_norm', 'v_p_pool', 'v_p_attn', 'v_p_ssd', 'v_w_out', 'v_norm_ffn', 'v_ffn_up', 'v_ffn_conv_w', 'v_ffn_conv_b', 'v_ffn_down', 'v_norm_final']
TWIN_OUTPUTS = ['loss', 'grad_x', 'grad_norm_mix', 'grad_w_in', 'grad_pool_mix', 'grad_pool_scale', 'grad_f_bias', 'grad_ssd_conv_w', 'grad_ssd_conv_b', 'grad_ssd_dt_bias', 'grad_ssd_a_log', 'grad_ssd_d', 'grad_ssd_norm', 'grad_p_pool', 'grad_p_attn', 'grad_p_ssd', 'grad_w_out', 'grad_norm_ffn', 'grad_ffn_up', 'grad_ffn_conv_w', 'grad_ffn_conv_b', 'grad_ffn_down', 'grad_norm_final', 'delta_norm_mix', 'delta_w_in', 'delta_pool_mix', 'delta_pool_scale', 'delta_f_bias', 'delta_ssd_conv_w', 'delta_ssd_conv_b', 'delta_ssd_dt_bias', 'delta_ssd_a_log', 'delta_ssd_d', 'delta_ssd_norm', 'delta_p_pool', 'delta_p_attn', 'delta_p_ssd', 'delta_w_out', 'delta_norm_ffn', 'delta_ffn_up', 'delta_ffn_conv_w', 'delta_ffn_conv_b', 'delta_ffn_down', 'delta_norm_final', 'new_m_norm_mix', 'new_m_w_in', 'new_m_pool_mix', 'new_m_pool_scale', 'new_m_f_bias', 'new_m_ssd_conv_w', 'new_m_ssd_conv_b', 'new_m_ssd_dt_bias', 'new_m_ssd_a_log', 'new_m_ssd_d', 'new_m_ssd_norm', 'new_m_p_pool', 'new_m_p_attn', 'new_m_p_ssd', 'new_m_w_out', 'new_m_norm_ffn', 'new_m_ffn_up', 'new_m_ffn_conv_w', 'new_m_ffn_conv_b', 'new_m_ffn_down', 'new_m_norm_final', 'new_v_norm_mix', 'new_v_w_in', 'new_v_pool_mix', 'new_v_pool_scale', 'new_v_f_bias', 'new_v_ssd_conv_w', 'new_v_ssd_conv_b', 'new_v_ssd_dt_bias', 'new_v_ssd_a_log', 'new_v_ssd_d', 'new_v_ssd_norm', 'new_v_p_pool', 'new_v_p_attn', 'new_v_p_ssd', 'new_v_w_out', 'new_v_norm_ffn', 'new_v_ffn_up', 'new_v_ffn_conv_w', 'new_v_ffn_conv_b', 'new_v_ffn_down', 'new_v_norm_final']
TWIN_LEAF_KINDS = {'loss': 'loss', 'grad_x': 'grad_x', 'grad_norm_mix': 'grad_w', 'grad_w_in': 'grad_w', 'grad_pool_mix': 'grad_w', 'grad_pool_scale': 'grad_w', 'grad_f_bias': 'grad_w', 'grad_ssd_conv_w': 'grad_w', 'grad_ssd_conv_b': 'grad_w', 'grad_ssd_dt_bias': 'grad_w', 'grad_ssd_a_log': 'grad_w', 'grad_ssd_d': 'grad_w', 'grad_ssd_norm': 'grad_w', 'grad_p_pool': 'grad_w', 'grad_p_attn': 'grad_w', 'grad_p_ssd': 'grad_w', 'grad_w_out': 'grad_w', 'grad_norm_ffn': 'grad_w', 'grad_ffn_up': 'grad_w', 'grad_ffn_conv_w': 'grad_w', 'grad_ffn_conv_b': 'grad_w', 'grad_ffn_down': 'grad_w', 'grad_norm_final': 'grad_w', 'delta_norm_mix': 'delta_w', 'delta_w_in': 'delta_w', 'delta_pool_mix': 'delta_w', 'delta_pool_scale': 'delta_w', 'delta_f_bias': 'delta_w', 'delta_ssd_conv_w': 'delta_w', 'delta_ssd_conv_b': 'delta_w', 'delta_ssd_dt_bias': 'delta_w', 'delta_ssd_a_log': 'delta_w', 'delta_ssd_d': 'delta_w', 'delta_ssd_norm': 'delta_w', 'delta_p_pool': 'delta_w', 'delta_p_attn': 'delta_w', 'delta_p_ssd': 'delta_w', 'delta_w_out': 'delta_w', 'delta_norm_ffn': 'delta_w', 'delta_ffn_up': 'delta_w', 'delta_ffn_conv_w': 'delta_w', 'delta_ffn_conv_b': 'delta_w', 'delta_ffn_down': 'delta_w', 'delta_norm_final': 'delta_w', 'new_m_norm_mix': 'new_m', 'new_m_w_in': 'new_m', 'new_m_pool_mix': 'new_m', 'new_m_pool_scale': 'new_m', 'new_m_f_bias': 'new_m', 'new_m_ssd_conv_w': 'new_m', 'new_m_ssd_conv_b': 'new_m', 'new_m_ssd_dt_bias': 'new_m', 'new_m_ssd_a_log': 'new_m', 'new_m_ssd_d': 'new_m', 'new_m_ssd_norm': 'new_m', 'new_m_p_pool': 'new_m', 'new_m_p_attn': 'new_m', 'new_m_p_ssd': 'new_m', 'new_m_w_out': 'new_m', 'new_m_norm_ffn': 'new_m', 'new_m_ffn_up': 'new_m', 'new_m_ffn_conv_w': 'new_m', 'new_m_ffn_conv_b': 'new_m', 'new_m_ffn_down': 'new_m', 'new_m_norm_final': 'new_m', 'new_v_norm_mix': 'new_v', 'new_v_w_in': 'new_v', 'new_v_pool_mix': 'new_v', 'new_v_pool_scale': 'new_v', 'new_v_f_bias': 'new_v', 'new_v_ssd_conv_w': 'new_v', 'new_v_ssd_conv_b': 'new_v', 'new_v_ssd_dt_bias': 'new_v', 'new_v_ssd_a_log': 'new_v', 'new_v_ssd_d': 'new_v', 'new_v_ssd_norm': 'new_v', 'new_v_p_pool': 'new_v', 'new_v_p_attn': 'new_v', 'new_v_p_ssd': 'new_v', 'new_v_w_out': 'new_v', 'new_v_norm_ffn': 'new_v', 'new_v_ffn_up': 'new_v', 'new_v_ffn_conv_w': 'new_v', 'new_v_ffn_conv_b': 'new_v', 'new_v_ffn_down': 'new_v', 'new_v_norm_final': 'new_v'}


def _forward(args):
    return _fwd_reference(*[args[k] for k in FWD_PARAMS])


def _output_shape():
    def fwd():
        inp = _fwd_setup_inputs(0)
        return _fwd_reference(*[inp[k] for k in FWD_PARAMS])
    out = _jax.eval_shape(fwd)
    return out.shape, out.dtype

N_MICROBATCH = 1
ADAM_LR = 0.001
ADAM_B1 = 0.9
ADAM_B2 = 0.999
ADAM_EPS = 1e-08
ADAM_WD = 0.01
ADAM_STEP = 10
PER_EXAMPLE_BATCH_AXIS = {'x': 0, 'loss_target': 0}
SHARED_INPUTS = []
_WEIGHT_DTYPES = {'norm_mix': _jnp.float32, 'w_in': _jnp.float32, 'pool_mix': _jnp.float32, 'pool_scale': _jnp.float32, 'f_bias': _jnp.float32, 'ssd_conv_w': _jnp.float32, 'ssd_conv_b': _jnp.float32, 'ssd_dt_bias': _jnp.float32, 'ssd_a_log': _jnp.float32, 'ssd_d': _jnp.float32, 'ssd_norm': _jnp.float32, 'p_pool': _jnp.float32, 'p_attn': _jnp.float32, 'p_ssd': _jnp.float32, 'w_out': _jnp.float32, 'norm_ffn': _jnp.float32, 'ffn_up': _jnp.float32, 'ffn_conv_w': _jnp.float32, 'ffn_conv_b': _jnp.float32, 'ffn_down': _jnp.float32, 'norm_final': _jnp.float32}
MOMENT_SCALE = {'norm_mix': 2.191662e-01, 'w_in': 8.146440e-02, 'pool_mix': 1.494849e-01, 'pool_scale': 1.658261e-01, 'f_bias': 2.734767e-01, 'ssd_conv_w': 1.042730e-01, 'ssd_conv_b': 1.517563e-01, 'ssd_dt_bias': 2.190574e-01, 'ssd_a_log': 6.256871e-01, 'ssd_d': 5.050593e-01, 'ssd_norm': 1.259686e-01, 'p_pool': 1.041357e-01, 'p_attn': 4.035779e-02, 'p_ssd': 1.199622e-01, 'w_out': 1.623561e-01, 'norm_ffn': 1.642015e-01, 'ffn_up': 6.839964e-02, 'ffn_conv_w': 6.784783e-02, 'ffn_conv_b': 6.834123e-02, 'ffn_down': 1.120348e-01, 'norm_final': 6.407512e+01}


def _to_microbatches(a, axis):
    t = _jnp.moveaxis(a, axis, 0)
    t = t.reshape((N_MICROBATCH, t.shape[0] // N_MICROBATCH) + t.shape[1:])
    return _jnp.moveaxis(t, 1, axis + 1)


def setup_inputs(seed: int = 0) -> dict:
    inp = _fwd_setup_inputs(seed)
    key = _jax.random.fold_in(_jax.random.key(seed), 7919)
    shape, _ = _output_shape()
    out = dict(inp)
    out["loss_target"] = _jax.random.normal(_jax.random.fold_in(key, 0), shape, _jnp.float32)
    for i, name in enumerate(TWIN_WEIGHTS):
        w = inp[name].astype(_jnp.float32)
        if MOMENT_SCALE is None:
            s = _jnp.sqrt(_jnp.mean(_jnp.square(w)) + 1e-30)
        else:
            s = MOMENT_SCALE[name]
        km, kv = _jax.random.split(_jax.random.fold_in(key, i + 1))
        out[name] = w
        out["m_" + name] = s * _jax.random.normal(km, w.shape, _jnp.float32)
        out["v_" + name] = (s * s) * _jax.random.uniform(kv, w.shape, _jnp.float32, 0.5, 1.5)
    if N_MICROBATCH > 1:
        for name, axis in PER_EXAMPLE_BATCH_AXIS.items():
            out[name] = _to_microbatches(out[name], axis)
    return {'x': out['x'], 'norm_mix': out['norm_mix'], 'w_in': out['w_in'], 'pool_mix': out['pool_mix'], 'pool_scale': out['pool_scale'], 'f_bias': out['f_bias'], 'ssd_conv_w': out['ssd_conv_w'], 'ssd_conv_b': out['ssd_conv_b'], 'ssd_dt_bias': out['ssd_dt_bias'], 'ssd_a_log': out['ssd_a_log'], 'ssd_d': out['ssd_d'], 'ssd_norm': out['ssd_norm'], 'p_pool': out['p_pool'], 'p_attn': out['p_attn'], 'p_ssd': out['p_ssd'], 'w_out': out['w_out'], 'norm_ffn': out['norm_ffn'], 'ffn_up': out['ffn_up'], 'ffn_conv_w': out['ffn_conv_w'], 'ffn_conv_b': out['ffn_conv_b'], 'ffn_down': out['ffn_down'], 'norm_final': out['norm_final'], 'loss_target': out['loss_target'], 'm_norm_mix': out['m_norm_mix'], 'm_w_in': out['m_w_in'], 'm_pool_mix': out['m_pool_mix'], 'm_pool_scale': out['m_pool_scale'], 'm_f_bias': out['m_f_bias'], 'm_ssd_conv_w': out['m_ssd_conv_w'], 'm_ssd_conv_b': out['m_ssd_conv_b'], 'm_ssd_dt_bias': out['m_ssd_dt_bias'], 'm_ssd_a_log': out['m_ssd_a_log'], 'm_ssd_d': out['m_ssd_d'], 'm_ssd_norm': out['m_ssd_norm'], 'm_p_pool': out['m_p_pool'], 'm_p_attn': out['m_p_attn'], 'm_p_ssd': out['m_p_ssd'], 'm_w_out': out['m_w_out'], 'm_norm_ffn': out['m_norm_ffn'], 'm_ffn_up': out['m_ffn_up'], 'm_ffn_conv_w': out['m_ffn_conv_w'], 'm_ffn_conv_b': out['m_ffn_conv_b'], 'm_ffn_down': out['m_ffn_down'], 'm_norm_final': out['m_norm_final'], 'v_norm_mix': out['v_norm_mix'], 'v_w_in': out['v_w_in'], 'v_pool_mix': out['v_pool_mix'], 'v_pool_scale': out['v_pool_scale'], 'v_f_bias': out['v_f_bias'], 'v_ssd_conv_w': out['v_ssd_conv_w'], 'v_ssd_conv_b': out['v_ssd_conv_b'], 'v_ssd_dt_bias': out['v_ssd_dt_bias'], 'v_ssd_a_log': out['v_ssd_a_log'], 'v_ssd_d': out['v_ssd_d'], 'v_ssd_norm': out['v_ssd_norm'], 'v_p_pool': out['v_p_pool'], 'v_p_attn': out['v_p_attn'], 'v_p_ssd': out['v_p_ssd'], 'v_w_out': out['v_w_out'], 'v_norm_ffn': out['v_norm_ffn'], 'v_ffn_up': out['v_ffn_up'], 'v_ffn_conv_w': out['v_ffn_conv_w'], 'v_ffn_conv_b': out['v_ffn_conv_b'], 'v_ffn_down': out['v_ffn_down'], 'v_norm_final': out['v_norm_final']}


def _loss(weights, diff, rest, loss_target):
    with _jax.named_scope("forward"):
        args = {**rest, TWIN_DIFF_INPUT: diff, **{k: w.astype(_WEIGHT_DTYPES[k]) for k, w in weights.items()}}
        y = _forward(args)
    with _jax.named_scope("loss_head"):
        err = _jnp.square(y.astype(_jnp.float32) - loss_target)
        return 0.5 * _jnp.sum(_jnp.mean(err, axis=-1)) if err.ndim else 0.5 * err


def _adamw(w, g, m, v):
    m = ADAM_B1 * m + (1.0 - ADAM_B1) * g
    v = ADAM_B2 * v + (1.0 - ADAM_B2) * _jnp.square(g)
    m_hat = m / (1.0 - ADAM_B1 ** ADAM_STEP)
    v_hat = v / (1.0 - ADAM_B2 ** ADAM_STEP)
    delta = -ADAM_LR * (m_hat / (_jnp.sqrt(v_hat) + ADAM_EPS) + ADAM_WD * w)
    return delta, m, v


def reference(x, norm_mix, w_in, pool_mix, pool_scale, f_bias, ssd_conv_w, ssd_conv_b, ssd_dt_bias, ssd_a_log, ssd_d, ssd_norm, p_pool, p_attn, p_ssd, w_out, norm_ffn, ffn_up, ffn_conv_w, ffn_conv_b, ffn_down, norm_final, loss_target, m_norm_mix, m_w_in, m_pool_mix, m_pool_scale, m_f_bias, m_ssd_conv_w, m_ssd_conv_b, m_ssd_dt_bias, m_ssd_a_log, m_ssd_d, m_ssd_norm, m_p_pool, m_p_attn, m_p_ssd, m_w_out, m_norm_ffn, m_ffn_up, m_ffn_conv_w, m_ffn_conv_b, m_ffn_down, m_norm_final, v_norm_mix, v_w_in, v_pool_mix, v_pool_scale, v_f_bias, v_ssd_conv_w, v_ssd_conv_b, v_ssd_dt_bias, v_ssd_a_log, v_ssd_d, v_ssd_norm, v_p_pool, v_p_attn, v_p_ssd, v_w_out, v_norm_ffn, v_ffn_up, v_ffn_conv_w, v_ffn_conv_b, v_ffn_down, v_norm_final):
    given = dict(x=x, norm_mix=norm_mix, w_in=w_in, pool_mix=pool_mix, pool_scale=pool_scale, f_bias=f_bias, ssd_conv_w=ssd_conv_w, ssd_conv_b=ssd_conv_b, ssd_dt_bias=ssd_dt_bias, ssd_a_log=ssd_a_log, ssd_d=ssd_d, ssd_norm=ssd_norm, p_pool=p_pool, p_attn=p_attn, p_ssd=p_ssd, w_out=w_out, norm_ffn=norm_ffn, ffn_up=ffn_up, ffn_conv_w=ffn_conv_w, ffn_conv_b=ffn_conv_b, ffn_down=ffn_down, norm_final=norm_final, loss_target=loss_target, m_norm_mix=m_norm_mix, m_w_in=m_w_in, m_pool_mix=m_pool_mix, m_pool_scale=m_pool_scale, m_f_bias=m_f_bias, m_ssd_conv_w=m_ssd_conv_w, m_ssd_conv_b=m_ssd_conv_b, m_ssd_dt_bias=m_ssd_dt_bias, m_ssd_a_log=m_ssd_a_log, m_ssd_d=m_ssd_d, m_ssd_norm=m_ssd_norm, m_p_pool=m_p_pool, m_p_attn=m_p_attn, m_p_ssd=m_p_ssd, m_w_out=m_w_out, m_norm_ffn=m_norm_ffn, m_ffn_up=m_ffn_up, m_ffn_conv_w=m_ffn_conv_w, m_ffn_conv_b=m_ffn_conv_b, m_ffn_down=m_ffn_down, m_norm_final=m_norm_final, v_norm_mix=v_norm_mix, v_w_in=v_w_in, v_pool_mix=v_pool_mix, v_pool_scale=v_pool_scale, v_f_bias=v_f_bias, v_ssd_conv_w=v_ssd_conv_w, v_ssd_conv_b=v_ssd_conv_b, v_ssd_dt_bias=v_ssd_dt_bias, v_ssd_a_log=v_ssd_a_log, v_ssd_d=v_ssd_d, v_ssd_norm=v_ssd_norm, v_p_pool=v_p_pool, v_p_attn=v_p_attn, v_p_ssd=v_p_ssd, v_w_out=v_w_out, v_norm_ffn=v_norm_ffn, v_ffn_up=v_ffn_up, v_ffn_conv_w=v_ffn_conv_w, v_ffn_conv_b=v_ffn_conv_b, v_ffn_down=v_ffn_down, v_norm_final=v_norm_final)
    weights = {n: given[n] for n in TWIN_WEIGHTS}
    shared = {n: given[n] for n in SHARED_INPUTS}
    per_example = {n: given[n] for n in ['x']}
    grad_fn = _jax.value_and_grad(_loss, argnums=(0, 1))

    def one_microbatch(ex, loss_target):
        ex = dict(ex)
        diff = ex.pop(TWIN_DIFF_INPUT)
        return grad_fn(weights, diff, {**shared, **ex}, loss_target)

    if N_MICROBATCH == 1:
        loss, (grad_w, grad_x) = one_microbatch(per_example, given["loss_target"])
    else:
        def body(carry, xs):
            loss_sum, grad_sum = carry
            l_k, (gw_k, gx_k) = one_microbatch(xs[0], xs[1])
            with _jax.named_scope("update"):
                return (loss_sum + l_k, _jax.tree.map(_jnp.add, grad_sum, gw_k)), gx_k

        init = (_jnp.zeros((), _jnp.float32), _jax.tree.map(_jnp.zeros_like, weights))
        (loss, grad_w), grad_x = _jax.lax.scan(body, init, (per_example, given["loss_target"]))
    with _jax.named_scope("update"):
        delta_w, new_m, new_v = {}, {}, {}
        for n in TWIN_WEIGHTS:
            delta_w[n], new_m[n], new_v[n] = _adamw(weights[n], grad_w[n], given["m_" + n], given["v_" + n])
    return (loss, grad_x, *[grad_w[n] for n in TWIN_WEIGHTS], *[delta_w[n] for n in TWIN_WEIGHTS],
            *[new_m[n] for n in TWIN_WEIGHTS], *[new_v[n] for n in TWIN_WEIGHTS])
```

```python
import functools
import math

import jax
import jax.numpy as jnp
from jax import lax
from jax.experimental import pallas as pl
from jax.experimental.pallas import tpu as pltpu

F32 = jnp.float32
BF16 = jnp.bfloat16

LANE = 128
SUBLANE = 8
VMEM_LIMIT = 56 * 1024 * 1024

NORM_EPS = 1e-6
POOL_WINDOWS = (2, 4, 8, 16)
POOL_GROUP_DIM = 128
ATTN_HEADS = 8
ATTN_HEAD_DIM = 64
SSD_HEAD_DIM = 64
SSD_HEADS = 16
SSD_GROUPS = 2
SSD_STATE = 128
SSD_CHUNK = 128

ADAM_LR = 0.001
ADAM_B1 = 0.9
ADAM_B2 = 0.999
ADAM_EPS = 1e-08
ADAM_WD = 0.01
ADAM_STEP = 10


def _cparams(*sem):
    return pltpu.CompilerParams(dimension_semantics=tuple(sem), vmem_limit_bytes=VMEM_LIMIT)


def _tile(n, pref):
    if n <= pref:
        return n
    assert n % LANE == 0, n
    q = n // LANE
    best = 1
    for d in range(1, pref // LANE + 1):
        if q % d == 0:
            best = d
    return best * LANE


def _mm(a, b, *, ta=False, tb=False, add=None, out_dtype=F32, name):
    assert a.dtype == BF16 and b.dtype == BF16, (a.dtype, b.dtype)
    if ta:
        K, M = a.shape
    else:
        M, K = a.shape
    if tb:
        N, K2 = b.shape
    else:
        K2, N = b.shape
    assert K == K2, (a.shape, b.shape, ta, tb)
    tm, tn, tk = _tile(M, 512), _tile(N, 1024), _tile(K, 2048)
    nk = K // tk

    def body(*refs):
        if add is None:
            a_ref, b_ref, o_ref, acc_ref = refs
        else:
            a_ref, b_ref, add_ref, o_ref, acc_ref = refs
        k = pl.program_id(2)

        @pl.when(k == 0)
        def _():
            acc_ref[...] = jnp.zeros_like(acc_ref)

        dims = (((0 if ta else 1,), (1 if tb else 0,)), ((), ()))
        acc_ref[...] += lax.dot_general(a_ref[...], b_ref[...], dims, preferred_element_type=F32)

        @pl.when(k == nk - 1)
        def _():
            r = acc_ref[...]
            if add is not None:
                r = r + add_ref[...]
            o_ref[...] = r.astype(o_ref.dtype)

    a_spec = pl.BlockSpec((tk, tm), lambda i, j, k: (k, i)) if ta else pl.BlockSpec((tm, tk), lambda i, j, k: (i, k))
    b_spec = pl.BlockSpec((tn, tk), lambda i, j, k: (j, k)) if tb else pl.BlockSpec((tk, tn), lambda i, j, k: (k, j))
    in_specs = [a_spec, b_spec]
    args = [a, b]
    if add is not None:
        in_specs.append(pl.BlockSpec((tm, tn), lambda i, j, k: (i, j)))
        args.append(add)
    return pl.pallas_call(
        body,
        name=name,
        grid=(M // tm, N // tn, nk),
        in_specs=in_specs,
        out_specs=pl.BlockSpec((tm, tn), lambda i, j, k: (i, j)),
        out_shape=jax.ShapeDtypeStruct((M, N), out_dtype),
        scratch_shapes=[pltpu.VMEM((tm, tn), F32)],
        compiler_params=_cparams("parallel", "parallel", "arbitrary"),
    )(*args)


ROW_TILE = 512
HALO = 8
POOL_HALO = 16


def _row_tile(S):
    return min(ROW_TILE, S)


def _acc_out(ref, val, first):
    @pl.when(first)
    def _():
        ref[...] = val

    @pl.when(jnp.logical_not(first))
    def _():
        ref[...] += val


def _rms_fwd(x, w, *, name):
    S, D = x.shape
    ts = _row_tile(S)

    def body(x_ref, w_ref, u_ref):
        xv = x_ref[...]
        r = lax.rsqrt(jnp.mean(xv * xv, axis=-1, keepdims=True) + NORM_EPS)
        u_ref[...] = (xv * r * w_ref[...]).astype(BF16)

    return pl.pallas_call(
        body, name=name, grid=(S // ts,),
        in_specs=[pl.BlockSpec((ts, D), lambda i: (i, 0)), pl.BlockSpec((1, D), lambda i: (0, 0))],
        out_specs=pl.BlockSpec((ts, D), lambda i: (i, 0)),
        out_shape=jax.ShapeDtypeStruct((S, D), BF16),
        compiler_params=_cparams("parallel"),
    )(x, w)


def _rms_bwd(x, w, du, g, *, name):
    S, D = x.shape
    ts = _row_tile(S)

    def body(x_ref, w_ref, du_ref, g_ref, dx_ref, dxb_ref, dw_ref):
        xv = x_ref[...]
        r = lax.rsqrt(jnp.mean(xv * xv, axis=-1, keepdims=True) + NORM_EPS)
        n = xv * r
        duv = du_ref[...]
        dn = duv * w_ref[...]
        dx = g_ref[...] + r * (dn - n * jnp.mean(dn * n, axis=-1, keepdims=True))
        dx_ref[...] = dx
        dxb_ref[...] = dx.astype(BF16)
        _acc_out(dw_ref, jnp.sum(duv * n, axis=0, keepdims=True), pl.program_id(0) == 0)

    row = pl.BlockSpec((ts, D), lambda i: (i, 0))
    vec = pl.BlockSpec((1, D), lambda i: (0, 0))
    return pl.pallas_call(
        body, name=name, grid=(S // ts,),
        in_specs=[row, vec, row, row],
        out_specs=[row, row, vec],
        out_shape=[jax.ShapeDtypeStruct((S, D), F32), jax.ShapeDtypeStruct((S, D), BF16),
                   jax.ShapeDtypeStruct((1, D), F32)],
        compiler_params=_cparams("arbitrary"),
    )(x, w, du, g)


def _loss_head(x, w, target, *, name):
    S, D = x.shape
    ts = _row_tile(S)

    def body(x_ref, w_ref, t_ref, loss_ref, dx_ref, dxb_ref, dw_ref):
        xv = x_ref[...]
        wv = w_ref[...]
        r = lax.rsqrt(jnp.mean(xv * xv, axis=-1, keepdims=True) + NORM_EPS)
        n = xv * r
        err = n * wv - t_ref[...]
        part = 0.5 * jnp.sum(jnp.mean(err * err, axis=-1, keepdims=True), axis=0, keepdims=True)
        dy = err * (1.0 / D)
        dn = dy * wv
        dx = r * (dn - n * jnp.mean(dn * n, axis=-1, keepdims=True))
        dx_ref[...] = dx
        dxb_ref[...] = dx.astype(BF16)
        first = pl.program_id(0) == 0
        _acc_out(dw_ref, jnp.sum(dy * n, axis=0, keepdims=True), first)
        _acc_out(loss_ref, jnp.broadcast_to(part, loss_ref.shape), first)

    row = pl.BlockSpec((ts, D), lambda i: (i, 0))
    vec = pl.BlockSpec((1, D), lambda i: (0, 0))
    return pl.pallas_call(
        body, name=name, grid=(S // ts,),
        in_specs=[row, vec, row],
        out_specs=[pl.BlockSpec((SUBLANE, LANE), lambda i: (0, 0)), row, row, vec],
        out_shape=[jax.ShapeDtypeStruct((SUBLANE, LANE), F32), jax.ShapeDtypeStruct((S, D), F32),
                   jax.ShapeDtypeStruct((S, D), BF16), jax.ShapeDtypeStruct((1, D), F32)],
        compiler_params=_cparams("arbitrary"),
    )(x, w, target)


def _sigmoid(x):
    return 1.0 / (1.0 + jnp.exp(-x))


def _merge_fwd(gl, ya, yb, yc, *, name):
    S, D = ya.shape
    ts = _row_tile(S)

    def body(ga_ref, gb_ref, gc_ref, ya_ref, yb_ref, yc_ref, o_ref):
        m = (_sigmoid(ga_ref[...]) * ya_ref[...] + _sigmoid(gb_ref[...]) * yb_ref[...]
             + _sigmoid(gc_ref[...]) * yc_ref[...])
        o_ref[...] = m.astype(BF16)

    row = pl.BlockSpec((ts, D), lambda i: (i, 0))
    gspec = [pl.BlockSpec((ts, D), functools.partial(lambda b, i: (i, b), b)) for b in range(3)]
    return pl.pallas_call(
        body, name=name, grid=(S // ts,),
        in_specs=gspec + [row, row, row],
        out_specs=row,
        out_shape=jax.ShapeDtypeStruct((S, D), BF16),
        compiler_params=_cparams("parallel"),
    )(gl, gl, gl, ya, yb, yc)


def _merge_bwd(gl, ya, yb, yc, dm, *, name):
    S, D = ya.shape
    ts = _row_tile(S)

    def body(ga_ref, gb_ref, gc_ref, ya_ref, yb_ref, yc_ref, dm_ref, dgl_ref, da_ref, db_ref, dc_ref):
        dmv = dm_ref[...]
        for b, (g_ref, y_ref, dy_ref) in enumerate(((ga_ref, ya_ref, da_ref), (gb_ref, yb_ref, db_ref),
                                                     (gc_ref, yc_ref, dc_ref))):
            s = _sigmoid(g_ref[...])
            dy_ref[...] = (dmv * s).astype(BF16)
            dgl_ref[:, b * D:(b + 1) * D] = (dmv * y_ref[...] * s * (1.0 - s)).astype(BF16)

    row = pl.BlockSpec((ts, D), lambda i: (i, 0))
    gspec = [pl.BlockSpec((ts, D), functools.partial(lambda b, i: (i, b), b)) for b in range(3)]
    return pl.pallas_call(
        body, name=name, grid=(S // ts,),
        in_specs=gspec + [row, row, row, row],
        out_specs=[pl.BlockSpec((ts, 3 * D), lambda i: (i, 0)), row, row, row],
        out_shape=[jax.ShapeDtypeStruct((S, 3 * D), BF16)] + [jax.ShapeDtypeStruct((S, D), BF16)] * 3,
        compiler_params=_cparams("parallel"),
    )(gl, gl, gl, ya, yb, yc, dm)


POOL_WIDTH = 512


def _pool_cnt(t, w):
    return jnp.minimum(t + 1, w).astype(F32)


def _pool_fwd(rest, col, mix, scale, *, name):
    S = rest.shape[0]
    W, G, H = POOL_WIDTH, POOL_GROUP_DIM, POOL_HALO
    ts = _row_tile(S)

    def body(v_ref, h_ref, mix_ref, sc_ref, d_ref, y_ref, ext_ref):
        i = pl.program_id(0)
        cur = v_ref[...]
        ext_ref[0:H, :] = jnp.where(i > 0, h_ref[...], 0.0)
        ext_ref[H:, :] = cur
        t = i * ts + lax.broadcasted_iota(jnp.int32, (ts, 1), 0)
        for g, w in enumerate(POOL_WINDOWS):
            cols = slice(g * G, (g + 1) * G)
            acc = cur[:, cols]
            for k in range(1, w):
                acc = acc + ext_ref[pl.ds(H - k, ts), cols]
            db = (acc / _pool_cnt(t, w) - cur[:, cols]).astype(BF16)
            d_ref[:, cols] = db
            y = jnp.dot(db, mix_ref[g], preferred_element_type=F32) * sc_ref[:, cols]
            y_ref[:, cols] = y.astype(BF16)

    row = pl.BlockSpec((ts, W), lambda i: (i, 0))
    return pl.pallas_call(
        body, name=name, grid=(S // ts,),
        in_specs=[pl.BlockSpec((ts, W), lambda i: (i, col)),
                  pl.BlockSpec((H, W), lambda i: (jnp.maximum(i * (ts // H) - 1, 0), col)),
                  pl.BlockSpec((len(POOL_WINDOWS), G, G), lambda i: (0, 0, 0)),
                  pl.BlockSpec((1, W), lambda i: (0, 0))],
        out_specs=[row, row],
        out_shape=[jax.ShapeDtypeStruct((S, W), BF16)] * 2,
        scratch_shapes=[pltpu.VMEM((H + ts, W), F32)],
        compiler_params=_cparams("parallel"),
    )(rest, rest, mix, scale)


def _pool_bwd(dy, d, mix, scale, *, name):
    S = dy.shape[0]
    W, G, H = POOL_WIDTH, POOL_GROUP_DIM, POOL_HALO
    ts = _row_tile(S)
    n = S // ts
    NT = (((1,), (1,)), ((), ()))
    TN = (((0,), (0,)), ((), ()))

    def body(dy_ref, dyn_ref, d_ref, mix_ref, sc_ref, dv_ref, dmix_ref, dsc_ref, ext_ref):
        i = pl.program_id(0)

        @pl.when(i == 0)
        def _():
            dmix_ref[...] = jnp.zeros_like(dmix_ref)
            dsc_ref[...] = jnp.zeros_like(dsc_ref)

        t = i * ts + lax.broadcasted_iota(jnp.int32, (ts, 1), 0)
        tn = (i + 1) * ts + lax.broadcasted_iota(jnp.int32, (H, 1), 0)
        for g, w in enumerate(POOL_WINDOWS):
            cols = slice(g * G, (g + 1) * G)
            sc = sc_ref[:, cols]
            dyv = dy_ref[:, cols]
            db = d_ref[:, cols]
            mg = mix_ref[g]
            yp = jnp.dot(db, mg, preferred_element_type=F32)
            dsc_ref[:, cols] += jnp.sum(dyv * yp, axis=0, keepdims=True)
            dyp = (dyv * sc).astype(BF16)
            dd = lax.dot_general(dyp, mg, NT, preferred_element_type=F32)
            dmix_ref[g] += lax.dot_general(db, dyp, TN, preferred_element_type=F32)
            dyn = (jnp.where(i < n - 1, dyn_ref[:, cols], 0.0) * sc).astype(BF16)
            ddn = lax.dot_general(dyn, mg, NT, preferred_element_type=F32)
            e = dd / _pool_cnt(t, w)
            ext_ref[0:ts, cols] = e
            ext_ref[ts:, cols] = ddn / _pool_cnt(tn, w)
            acc = e
            for k in range(1, w):
                acc = acc + ext_ref[pl.ds(k, ts), cols]
            dv_ref[:, cols] = (acc - dd).astype(BF16)

    row = pl.BlockSpec((ts, W), lambda i: (i, 0))
    return pl.pallas_call(
        body, name=name, grid=(n,),
        in_specs=[row,
                  pl.BlockSpec((H, W), lambda i: (jnp.minimum((i + 1) * (ts // H), S // H - 1), 0)),
                  row,
                  pl.BlockSpec((len(POOL_WINDOWS), G, G), lambda i: (0, 0, 0)),
                  pl.BlockSpec((1, W), lambda i: (0, 0))],
        out_specs=[row, pl.BlockSpec((len(POOL_WINDOWS), G, G), lambda i: (0, 0, 0)),
                   pl.BlockSpec((1, W), lambda i: (0, 0))],
        out_shape=[jax.ShapeDtypeStruct((S, W), BF16),
                   jax.ShapeDtypeStruct((len(POOL_WINDOWS), G, G), F32),
                   jax.ShapeDtypeStruct((1, W), F32)],
        scratch_shapes=[pltpu.VMEM((ts + H, W), F32)],
        compiler_params=_cparams("arbitrary"),
    )(dy, dy, d, mix, scale)


def _conv_taps(ext_ref, w_ref, b_ref, rows, first):
    K = w_ref.shape[0]
    pre = b_ref[...] + w_ref[K - 1:K, :] * ext_ref[pl.ds(first, rows), :]
    for k in range(K - 1):
        pre = pre + w_ref[k:k + 1, :] * ext_ref[pl.ds(first - (K - 1) + k, rows), :]
    return pre


def _prev_halo_spec(ts, tc, colfn):
    return pl.BlockSpec((HALO, tc), lambda *g: (jnp.maximum(g[-1] * (ts // HALO) - 1, 0), colfn(*g)))


def _conv_silu_fwd(rest, col0, C, w, b, *, name):
    S = rest.shape[0]
    ts, tc = _row_tile(S), 512
    assert col0 % tc == 0 and C % tc == 0
    cb = col0 // tc

    def body(x_ref, h_ref, w_ref, b_ref, o_ref, ext_ref):
        i = pl.program_id(1)
        ext_ref[0:HALO, :] = jnp.where(i > 0, h_ref[...], 0.0)
        ext_ref[HALO:, :] = x_ref[...]
        pre = _conv_taps(ext_ref, w_ref, b_ref, ts, HALO)
        o_ref[...] = pre * _sigmoid(pre)

    K = w.shape[0]
    return pl.pallas_call(
        body, name=name, grid=(C // tc, S // ts),
        in_specs=[pl.BlockSpec((ts, tc), lambda j, i: (i, cb + j)),
                  _prev_halo_spec(ts, tc, lambda j, i: cb + j),
                  pl.BlockSpec((K, tc), lambda j, i: (0, j)),
                  pl.BlockSpec((1, tc), lambda j, i: (0, j))],
        out_specs=pl.BlockSpec((ts, tc), lambda j, i: (i, j)),
        out_shape=jax.ShapeDtypeStruct((S, C), F32),
        scratch_shapes=[pltpu.VMEM((HALO + ts, tc), F32)],
        compiler_params=_cparams("parallel", "parallel"),
    )(rest, rest, w, b)


def _dsilu(pre):
    s = _sigmoid(pre)
    return s, s * (1.0 + pre * (1.0 - s))


def _conv_silu_bwd(rest, col0, C, w, b, dy, *, name):
    S = rest.shape[0]
    ts, tc = _row_tile(S), 512
    cb = col0 // tc
    n = S // ts
    K = w.shape[0]
    R = ts + HALO

    def body(x_ref, hp_ref, hn_ref, w_ref, b_ref, dy_ref, dyn_ref, dx_ref, dw_ref, db_ref, ext_ref, g_ref):
        i = pl.program_id(1)
        last = i == n - 1
        ext_ref[0:HALO, :] = jnp.where(i > 0, hp_ref[...], 0.0)
        ext_ref[HALO:HALO + ts, :] = x_ref[...]
        ext_ref[HALO + ts:, :] = jnp.where(last, 0.0, hn_ref[...])
        g_ref[0:ts, :] = dy_ref[...]
        g_ref[ts:, :] = jnp.where(last, 0.0, dyn_ref[...])
        pre = _conv_taps(ext_ref, w_ref, b_ref, R, HALO)
        g_ref[...] = g_ref[...] * _dsilu(pre)[1]
        dx = w_ref[K - 1:K, :] * g_ref[pl.ds(0, ts), :]
        for k in range(K - 1):
            dx = dx + w_ref[k:k + 1, :] * g_ref[pl.ds(K - 1 - k, ts), :]
        dx_ref[...] = dx.astype(BF16)
        gc = g_ref[pl.ds(0, ts), :]
        first = i == 0
        _acc_out(db_ref, jnp.sum(gc, axis=0, keepdims=True), first)
        dws = [jnp.sum(gc * ext_ref[pl.ds(HALO - (K - 1) + k, ts), :], axis=0, keepdims=True) for k in range(K)]
        _acc_out(dw_ref, jnp.concatenate(dws, axis=0), first)

    nxt = lambda cf: pl.BlockSpec((HALO, tc), lambda j, i: (jnp.minimum((i + 1) * (ts // HALO), S // HALO - 1), cf(j)))
    return pl.pallas_call(
        body, name=name, grid=(C // tc, n),
        in_specs=[pl.BlockSpec((ts, tc), lambda j, i: (i, cb + j)),
                  _prev_halo_spec(ts, tc, lambda j, i: cb + j),
                  nxt(lambda j: cb + j),
                  pl.BlockSpec((K, tc), lambda j, i: (0, j)),
                  pl.BlockSpec((1, tc), lambda j, i: (0, j)),
                  pl.BlockSpec((ts, tc), lambda j, i: (i, j)),
                  nxt(lambda j: j)],
        out_specs=[pl.BlockSpec((ts, tc), lambda j, i: (i, j)),
                   pl.BlockSpec((K, tc), lambda j, i: (0, j)),
                   pl.BlockSpec((1, tc), lambda j, i: (0, j))],
        out_shape=[jax.ShapeDtypeStruct((S, C), BF16), jax.ShapeDtypeStruct((K, C), F32),
                   jax.ShapeDtypeStruct((1, C), F32)],
        scratch_shapes=[pltpu.VMEM((HALO + R, tc), F32), pltpu.VMEM((R, tc), F32)],
        compiler_params=_cparams("parallel", "arbitrary"),
    )(rest, rest, rest, w, b, dy, dy)


FFN_TC = 256


def _ffn_act_fwd(h0, w, b, *, name):
    S, F2 = h0.shape
    F = F2 // 2
    ts, tc = _row_tile(S), FFN_TC
    nb = F // tc
    K = w.shape[0]

    def body(xg_ref, hg_ref, xv_ref, hv_ref, wg_ref, wv_ref, bg_ref, bv_ref, o_ref, eg_ref, ev_ref):
        i = pl.program_id(1)
        for x_ref, h_ref, e_ref in ((xg_ref, hg_ref, eg_ref), (xv_ref, hv_ref, ev_ref)):
            e_ref[0:HALO, :] = jnp.where(i > 0, h_ref[...], 0.0)
            e_ref[HALO:, :] = x_ref[...]
        pg = _conv_taps(eg_ref, wg_ref, bg_ref, ts, HALO)
        pv = _conv_taps(ev_ref, wv_ref, bv_ref, ts, HALO)
        o_ref[...] = (pg * _sigmoid(pg) * pv).astype(BF16)

    def half(off):
        return [pl.BlockSpec((ts, tc), lambda j, i: (i, off + j)), _prev_halo_spec(ts, tc, lambda j, i: off + j)]

    wspec = lambda off: pl.BlockSpec((K, tc), lambda j, i: (0, off + j))
    bspec = lambda off: pl.BlockSpec((1, tc), lambda j, i: (0, off + j))
    return pl.pallas_call(
        body, name=name, grid=(nb, S // ts),
        in_specs=half(0) + half(nb) + [wspec(0), wspec(nb), bspec(0), bspec(nb)],
        out_specs=pl.BlockSpec((ts, tc), lambda j, i: (i, j)),
        out_shape=jax.ShapeDtypeStruct((S, F), BF16),
        scratch_shapes=[pltpu.VMEM((HALO + ts, tc), F32)] * 2,
        compiler_params=_cparams("parallel", "parallel"),
    )(h0, h0, h0, h0, w, w, b, b)


def _ffn_act_bwd(h0, w, b, da, *, name):
    S, F2 = h0.shape
    F = F2 // 2
    ts, tc = _row_tile(S), FFN_TC
    nb = F // tc
    n = S // ts
    K = w.shape[0]
    R = ts + HALO

    def body(xg_ref, pg_ref, ng_ref, xv_ref, pv_ref, nv_ref, wg_ref, wv_ref, bg_ref, bv_ref, da_ref, dan_ref,
             dxg_ref, dxv_ref, dw_ref, db_ref, eg_ref, ev_ref, gg_ref, gv_ref):
        i = pl.program_id(1)
        dx_refs = (dxg_ref, dxv_ref)
        last = i == n - 1
        for x_ref, p_ref, n_ref, e_ref in ((xg_ref, pg_ref, ng_ref, eg_ref), (xv_ref, pv_ref, nv_ref, ev_ref)):
            e_ref[0:HALO, :] = jnp.where(i > 0, p_ref[...], 0.0)
            e_ref[HALO:HALO + ts, :] = x_ref[...]
            e_ref[HALO + ts:, :] = jnp.where(last, 0.0, n_ref[...])
        gg_ref[0:ts, :] = da_ref[...]
        gg_ref[ts:, :] = jnp.where(last, 0.0, dan_ref[...])
        dav = gg_ref[...]
        pg = _conv_taps(eg_ref, wg_ref, bg_ref, R, HALO)
        pv = _conv_taps(ev_ref, wv_ref, bv_ref, R, HALO)
        s, ds = _dsilu(pg)
        gg_ref[...] = dav * pv * ds
        gv_ref[...] = dav * pg * s
        first = i == 0
        for h, (g_ref, e_ref, w_ref) in enumerate(((gg_ref, eg_ref, wg_ref), (gv_ref, ev_ref, wv_ref))):
            dx = w_ref[K - 1:K, :] * g_ref[pl.ds(0, ts), :]
            for k in range(K - 1):
                dx = dx + w_ref[k:k + 1, :] * g_ref[pl.ds(K - 1 - k, ts), :]
            dx_refs[h][...] = dx.astype(BF16)
            gc = g_ref[pl.ds(0, ts), :]
            dws = [jnp.sum(gc * e_ref[pl.ds(HALO - (K - 1) + k, ts), :], axis=0, keepdims=True) for k in range(K)]
            _acc_out(db_ref.at[h], jnp.sum(gc, axis=0, keepdims=True), first)
            _acc_out(dw_ref.at[h], jnp.concatenate(dws, axis=0), first)

    nxt = lambda off: pl.BlockSpec((HALO, tc), lambda j, i: (jnp.minimum((i + 1) * (ts // HALO), S // HALO - 1), off + j))

    def half(off):
        return [pl.BlockSpec((ts, tc), lambda j, i: (i, off + j)), _prev_halo_spec(ts, tc, lambda j, i: off + j), nxt(off)]

    wspec = lambda off: pl.BlockSpec((K, tc), lambda j, i: (0, off + j))
    bspec = lambda off: pl.BlockSpec((1, tc), lambda j, i: (0, off + j))
    return pl.pallas_call(
        body, name=name, grid=(nb, n),
        in_specs=half(0) + half(nb) + [wspec(0), wspec(nb), bspec(0), bspec(nb),
                                       pl.BlockSpec((ts, tc), lambda j, i: (i, j)), nxt(0)],
        out_specs=[pl.BlockSpec((ts, tc), lambda j, i: (i, j)),
                   pl.BlockSpec((ts, tc), lambda j, i: (i, j)),
                   pl.BlockSpec((2, K, tc), lambda j, i: (0, 0, j)),
                   pl.BlockSpec((2, 1, tc), lambda j, i: (0, 0, j))],
        out_shape=[jax.ShapeDtypeStruct((S, F), BF16), jax.ShapeDtypeStruct((S, F), BF16),
                   jax.ShapeDtypeStruct((2, K, F), F32), jax.ShapeDtypeStruct((2, 1, F), F32)],
        scratch_shapes=[pltpu.VMEM((HALO + R, tc), F32)] * 2 + [pltpu.VMEM((R, tc), F32)] * 2,
        compiler_params=_cparams("parallel", "arbitrary"),
    )(h0, h0, h0, h0, h0, h0, w, w, b, b, da, da)


CUM_TILE = 256
ATT_TILE = 512
NEG = -1e30
N_PAIR = ATTN_HEADS // 2
NT_DIMS = (((1,), (1,)), ((), ()))


def _split3(x):
    a = x.astype(BF16)
    r = x - a.astype(F32)
    b = r.astype(BF16)
    c = (r - b.astype(F32)).astype(BF16)
    return a, b, c


def _tri_dot(tri, x):
    return sum(jnp.dot(tri, p, preferred_element_type=F32) for p in _split3(x))


def _log_sigmoid(x):
    return jnp.minimum(x, 0.0) - jnp.log(1.0 + jnp.exp(-jnp.abs(x)))


def _col(v, idx, lane):
    return jnp.sum(jnp.where(lane == idx, v, 0.0), axis=1, keepdims=True)


def _fox_pre(rest, col, fb, *, name):
    S = rest.shape[0]
    ts = min(CUM_TILE, S)

    def body(f_ref, fb_ref, ccol_ref, crow_ref, carry_ref):
        i = pl.program_id(0)

        @pl.when(i == 0)
        def _():
            carry_ref[...] = jnp.zeros_like(carry_ref)

        lane = lax.broadcasted_iota(jnp.int32, (1, LANE), 1)
        logf = jnp.where(lane < ATTN_HEADS, _log_sigmoid(f_ref[...] + fb_ref[...]), 0.0)
        r = lax.broadcasted_iota(jnp.int32, (ts, ts), 0)
        c = lax.broadcasted_iota(jnp.int32, (ts, ts), 1)
        tri = jnp.where(c <= r, 1.0, 0.0).astype(BF16)
        cs = _tri_dot(tri, logf) + carry_ref[...]
        ccol_ref[...] = cs
        carry_ref[...] = cs[ts - 1:ts, :]
        crow_ref[...] = cs.T[0:SUBLANE, :]

    return pl.pallas_call(
        body, name=name, grid=(S // ts,),
        in_specs=[pl.BlockSpec((ts, LANE), lambda i: (i, col)), pl.BlockSpec((1, LANE), lambda i: (0, 0))],
        out_specs=[pl.BlockSpec((ts, LANE), lambda i: (i, 0)), pl.BlockSpec((SUBLANE, ts), lambda i: (0, i))],
        out_shape=[jax.ShapeDtypeStruct((S, LANE), F32), jax.ShapeDtypeStruct((SUBLANE, S), F32)],
        scratch_shapes=[pltpu.VMEM((1, LANE), F32)],
        compiler_params=_cparams("arbitrary"),
    )(rest, fb)


def _fox_post(dcq, dck, rest, col, fb, ddt, *, name):
    S = rest.shape[0]
    ts = min(CUM_TILE, S)
    n = S // ts

    def body(dcq_ref, dck_ref, f_ref, fb_ref, ddt_ref, o_ref, db_ref, carry_ref):
        i = pl.program_id(0)

        @pl.when(i == 0)
        def _():
            carry_ref[...] = jnp.zeros_like(carry_ref)

        lane = lax.broadcasted_iota(jnp.int32, (1, LANE), 1)
        dc = jnp.zeros((ts, LANE), F32)
        for h in range(ATTN_HEADS):
            d = dcq_ref[h // 2] - dck_ref[h // 2]
            dc = jnp.where(lane == h, _col(d, h % 2, lane), dc)
        r = lax.broadcasted_iota(jnp.int32, (ts, ts), 0)
        c = lax.broadcasted_iota(jnp.int32, (ts, ts), 1)
        tri = jnp.where(c >= r, 1.0, 0.0).astype(BF16)
        rc = _tri_dot(tri, dc) + carry_ref[...]
        carry_ref[...] = rc[0:1, :]
        df = rc * _sigmoid(-(f_ref[...] + fb_ref[...]))
        out = jnp.where(lane < ATTN_HEADS, df, ddt_ref[...])
        o_ref[...] = out.astype(BF16)
        _acc_out(db_ref, jnp.sum(out, axis=0, keepdims=True), i == 0)

    rev = lambda i: n - 1 - i
    return pl.pallas_call(
        body, name=name, grid=(n,),
        in_specs=[pl.BlockSpec((N_PAIR, ts, LANE), lambda i: (0, rev(i), 0)),
                  pl.BlockSpec((N_PAIR, ts, LANE), lambda i: (0, rev(i), 0)),
                  pl.BlockSpec((ts, LANE), lambda i: (rev(i), col)),
                  pl.BlockSpec((1, LANE), lambda i: (0, 0)),
                  pl.BlockSpec((ts, LANE), lambda i: (rev(i), 0))],
        out_specs=[pl.BlockSpec((ts, LANE), lambda i: (rev(i), 0)), pl.BlockSpec((1, LANE), lambda i: (0, 0))],
        out_shape=[jax.ShapeDtypeStruct((S, LANE), BF16), jax.ShapeDtypeStruct((1, LANE), F32)],
        scratch_shapes=[pltpu.VMEM((1, LANE), F32)],
        compiler_params=_cparams("arbitrary"),
    )(dcq, dck, rest, fb, ddt)


def _head_masks():
    lane = lax.broadcasted_iota(jnp.int32, (1, LANE), 1)
    return lane, (lane < ATTN_HEAD_DIM, lane >= ATTN_HEAD_DIM)


def _causal_mask(t):
    r = lax.broadcasted_iota(jnp.int32, (t, t), 0)
    c = lax.broadcasted_iota(jnp.int32, (t, t), 1)
    return r, c


def _attn_fwd(qkv, ccol, crow, *, name):
    S = qkv.shape[0]
    t = min(ATT_TILE, S)
    scale = ATTN_HEAD_DIM ** -0.5

    def body(q_ref, k_ref, v_ref, cc_ref, cr_ref, o_ref, lse_ref, m_ref, l_ref, acc_ref):
        j, i = pl.program_id(0), pl.program_id(1)
        lane, masks = _head_masks()
        q2 = q_ref[...] * scale
        qh = [jnp.where(mk, q2, 0.0).astype(BF16) for mk in masks]
        ccv = cc_ref[...]
        cq = [_col(ccv, 2 * j + hh, lane) for hh in range(2)]
        m_ref[...] = jnp.full(m_ref.shape, NEG, F32)
        l_ref[...] = jnp.zeros_like(l_ref)
        acc_ref[...] = jnp.zeros_like(acc_ref)

        def step(kb, masked):
            off = pl.multiple_of(kb * t, t)
            kblk = k_ref[pl.ds(off, t), :]
            vblk = v_ref[pl.ds(off, t), :]
            for hh in range(2):
                s = lax.dot_general(qh[hh], kblk, NT_DIMS, preferred_element_type=F32)
                s = s + (cq[hh] - cr_ref[hh, :, pl.ds(off, t)])
                if masked:
                    r, c = _causal_mask(t)
                    s = jnp.where(c <= r, s, NEG)
                m_old = m_ref[hh]
                m_new = jnp.maximum(m_old, jnp.max(s, axis=1, keepdims=True))
                p = jnp.exp(s - m_new)
                alpha = jnp.exp(m_old - m_new)
                l_ref[hh] = alpha * l_ref[hh] + jnp.sum(p, axis=1, keepdims=True)
                acc_ref[hh] = alpha * acc_ref[hh] + jnp.dot(p.astype(BF16), vblk, preferred_element_type=F32)
                m_ref[hh] = m_new

        def loop_body(kb, carry):
            step(kb, False)
            return carry

        lax.fori_loop(0, i, loop_body, 0)
        step(i, True)
        o = [acc_ref[hh] / l_ref[hh] for hh in range(2)]
        o_ref[...] = jnp.where(masks[0], o[0], o[1]).astype(BF16)
        lse = [m_ref[hh] + jnp.log(l_ref[hh]) for hh in range(2)]
        lse_ref[0] = jnp.where(lane == 0, lse[0], jnp.where(lane == 1, lse[1], 0.0))

    return pl.pallas_call(
        body, name=name, grid=(N_PAIR, S // t),
        in_specs=[pl.BlockSpec((t, LANE), lambda j, i: (i, j)),
                  pl.BlockSpec((S, LANE), lambda j, i: (0, N_PAIR + j)),
                  pl.BlockSpec((S, LANE), lambda j, i: (0, 2 * N_PAIR + j)),
                  pl.BlockSpec((t, LANE), lambda j, i: (i, 0)),
                  pl.BlockSpec((2, 1, S), lambda j, i: (j, 0, 0))],
        out_specs=[pl.BlockSpec((t, LANE), lambda j, i: (i, j)),
                   pl.BlockSpec((1, t, LANE), lambda j, i: (j, i, 0))],
        out_shape=[jax.ShapeDtypeStruct((S, N_PAIR * LANE), BF16), jax.ShapeDtypeStruct((N_PAIR, S, LANE), F32)],
        scratch_shapes=[pltpu.VMEM((2, t, 1), F32), pltpu.VMEM((2, t, 1), F32), pltpu.VMEM((2, t, LANE), F32)],
        compiler_params=_cparams("parallel", "parallel"),
    )(qkv, qkv, qkv, ccol, crow)


def _attn_bwd_dq(qkv, o, do, lse, ccol, crow, *, name):
    S = qkv.shape[0]
    t = min(ATT_TILE, S)
    scale = ATTN_HEAD_DIM ** -0.5

    def body(q_ref, k_ref, v_ref, o_ref, do_ref, lse_ref, cc_ref, cr_ref,
             dq_ref, dcq_ref, lrow_ref, drow_ref, acc_ref, rs_ref, tr_ref):
        j, i = pl.program_id(0), pl.program_id(1)
        lane, masks = _head_masks()
        q2 = q_ref[...] * scale
        do2 = do_ref[...]
        qh = [jnp.where(mk, q2, 0.0).astype(BF16) for mk in masks]
        doh = [jnp.where(mk, do2, 0.0).astype(BF16) for mk in masks]
        of = o_ref[...].astype(F32)
        delta = [jnp.sum(doh[hh].astype(F32) * of, axis=1, keepdims=True) for hh in range(2)]
        ccv = cc_ref[...]
        cq = [_col(ccv, 2 * j + hh, lane) for hh in range(2)]
        lsev = lse_ref[0]
        lse = [_col(lsev, hh, lane) for hh in range(2)]
        acc_ref[...] = jnp.zeros_like(acc_ref)
        rs_ref[...] = jnp.zeros_like(rs_ref)

        def step(kb, masked):
            off = pl.multiple_of(kb * t, t)
            kblk = k_ref[pl.ds(off, t), :]
            vblk = v_ref[pl.ds(off, t), :]
            for hh in range(2):
                s = lax.dot_general(qh[hh], kblk, NT_DIMS, preferred_element_type=F32)
                s = s + (cq[hh] - cr_ref[hh, :, pl.ds(off, t)])
                if masked:
                    r, c = _causal_mask(t)
                    s = jnp.where(c <= r, s, NEG)
                p = jnp.exp(s - lse[hh])
                dp = lax.dot_general(doh[hh], vblk, NT_DIMS, preferred_element_type=F32)
                ds = p * (dp - delta[hh])
                rs_ref[hh] += jnp.sum(ds, axis=1, keepdims=True)
                acc_ref[hh] += jnp.dot(ds.astype(BF16), kblk, preferred_element_type=F32)

        def loop_body(kb, carry):
            step(kb, False)
            return carry

        lax.fori_loop(0, i, loop_body, 0)
        step(i, True)
        dq_ref[...] = (jnp.where(masks[0], acc_ref[0], acc_ref[1]) * scale).astype(BF16)
        dcq_ref[0] = jnp.where(lane == 0, rs_ref[0], jnp.where(lane == 1, rs_ref[1], 0.0))
        stats = jnp.where(lane == 0, lse[0], jnp.where(lane == 1, lse[1],
                          jnp.where(lane == 2, delta[0], jnp.where(lane == 3, delta[1], 0.0))))
        tr_ref[...] = stats.T[0:SUBLANE, :]
        for hh in range(2):
            lrow_ref[hh] = tr_ref[hh:hh + 1, :]
            drow_ref[hh] = tr_ref[2 + hh:3 + hh, :]

    qspec = pl.BlockSpec((t, LANE), lambda j, i: (i, j))
    stat = pl.BlockSpec((1, t, LANE), lambda j, i: (j, i, 0))
    rowstat = pl.BlockSpec((2, 1, t), lambda j, i: (j, 0, i))
    return pl.pallas_call(
        body, name=name, grid=(N_PAIR, S // t),
        in_specs=[qspec,
                  pl.BlockSpec((S, LANE), lambda j, i: (0, N_PAIR + j)),
                  pl.BlockSpec((S, LANE), lambda j, i: (0, 2 * N_PAIR + j)),
                  qspec, qspec, stat,
                  pl.BlockSpec((t, LANE), lambda j, i: (i, 0)),
                  pl.BlockSpec((2, 1, S), lambda j, i: (j, 0, 0))],
        out_specs=[qspec, stat, rowstat, rowstat],
        out_shape=[jax.ShapeDtypeStruct((S, N_PAIR * LANE), BF16), jax.ShapeDtypeStruct((N_PAIR, S, LANE), F32),
                   jax.ShapeDtypeStruct((ATTN_HEADS, 1, S), F32), jax.ShapeDtypeStruct((ATTN_HEADS, 1, S), F32)],
        scratch_shapes=[pltpu.VMEM((2, t, LANE), F32), pltpu.VMEM((2, t, 1), F32), pltpu.VMEM((SUBLANE, t), F32)],
        compiler_params=_cparams("parallel", "parallel"),
    )(qkv, qkv, qkv, o, do, lse, ccol, crow)


def _attn_bwd_dkv(qkv, do, lrow, drow, ccol, crow, *, name):
    S = qkv.shape[0]
    t = min(ATT_TILE, S)
    n = S // t
    scale = ATTN_HEAD_DIM ** -0.5

    def body(q_ref, k_ref, v_ref, do_ref, lrow_ref, drow_ref, cc_ref, cr_ref,
             dk_ref, dv_ref, dck_ref, dka_ref, dva_ref, cs_ref):
        j, kb = pl.program_id(0), pl.program_id(1)
        lane, masks = _head_masks()
        k2 = k_ref[...]
        v2 = v_ref[...]
        kh = [jnp.where(mk, k2, 0.0).astype(BF16) for mk in masks]
        vh = [jnp.where(mk, v2, 0.0).astype(BF16) for mk in masks]
        ccv = cc_ref[...]
        ck = [_col(ccv, 2 * j + hh, lane) for hh in range(2)]
        dka_ref[...] = jnp.zeros_like(dka_ref)
        dva_ref[...] = jnp.zeros_like(dva_ref)
        cs_ref[...] = jnp.zeros_like(cs_ref)

        def step(qb, masked):
            off = pl.multiple_of(qb * t, t)
            qblk = (q_ref[pl.ds(off, t), :] * scale).astype(BF16)
            doblk = do_ref[pl.ds(off, t), :]
            for hh in range(2):
                st = lax.dot_general(kh[hh], qblk, NT_DIMS, preferred_element_type=F32)
                st = st + (cr_ref[hh, :, pl.ds(off, t)] - ck[hh])
                if masked:
                    r, c = _causal_mask(t)
                    st = jnp.where(r <= c, st, NEG)
                pt = jnp.exp(st - lrow_ref[hh, :, pl.ds(off, t)])
                dpt = lax.dot_general(vh[hh], doblk, NT_DIMS, preferred_element_type=F32)
                dst = pt * (dpt - drow_ref[hh, :, pl.ds(off, t)])
                cs_ref[hh] += jnp.sum(dst, axis=1, keepdims=True)
                dva_ref[hh] += jnp.dot(pt.astype(BF16), doblk, preferred_element_type=F32)
                dka_ref[hh] += jnp.dot(dst.astype(BF16), qblk, preferred_element_type=F32)

        step(kb, True)

        def loop_body(qb, carry):
            step(qb, False)
            return carry

        lax.fori_loop(kb + 1, n, loop_body, 0)
        dk_ref[...] = jnp.where(masks[0], dka_ref[0], dka_ref[1]).astype(BF16)
        dv_ref[...] = jnp.where(masks[0], dva_ref[0], dva_ref[1]).astype(BF16)
        dck_ref[0] = jnp.where(lane == 0, cs_ref[0], jnp.where(lane == 1, cs_ref[1], 0.0))

    full = lambda cb: pl.BlockSpec((S, LANE), lambda j, kb: (0, cb(j)))
    kspec = lambda base: pl.BlockSpec((t, LANE), lambda j, kb: (kb, base + j))
    rows = pl.BlockSpec((2, 1, S), lambda j, kb: (j, 0, 0))
    oblk = pl.BlockSpec((t, LANE), lambda j, kb: (kb, j))
    return pl.pallas_call(
        body, name=name, grid=(N_PAIR, n),
        in_specs=[full(lambda j: j), kspec(N_PAIR), kspec(2 * N_PAIR), full(lambda j: j), rows, rows,
                  pl.BlockSpec((t, LANE), lambda j, kb: (kb, 0)), rows],
        out_specs=[oblk, oblk, pl.BlockSpec((1, t, LANE), lambda j, kb: (j, kb, 0))],
        out_shape=[jax.ShapeDtypeStruct((S, N_PAIR * LANE), BF16)] * 2 + [jax.ShapeDtypeStruct((N_PAIR, S, LANE), F32)],
        scratch_shapes=[pltpu.VMEM((2, t, LANE), F32), pltpu.VMEM((2, t, LANE), F32), pltpu.VMEM((2, t, 1), F32)],
        compiler_params=_cparams("parallel", "parallel"),
    )(qkv, qkv, qkv, do, lrow, drow, ccol, crow)


DT_LANE0 = ATTN_HEADS
SSD_PAIRS = SSD_HEADS // 2
SSD_X = SSD_HEADS * SSD_HEAD_DIM
SSD_B0 = SSD_X
SSD_C0 = SSD_X + SSD_GROUPS * SSD_STATE
SSD_CH = SSD_X + 2 * SSD_GROUPS * SSD_STATE
TN_DIMS = (((0,), (0,)), ((), ()))


def _softplus(x):
    return jnp.maximum(x, 0.0) + jnp.log(1.0 + jnp.exp(-jnp.abs(x)))


def _ssd_prep(fdt, dtb, alog):
    L = fdt.shape[0]
    lane = lax.broadcasted_iota(jnp.int32, (1, LANE), 1)
    hl = jnp.logical_and(lane >= DT_LANE0, lane < DT_LANE0 + SSD_HEADS)
    dtv = jnp.where(hl, _softplus(fdt + dtb), 0.0)
    A = jnp.where(hl, -jnp.exp(alog), 0.0)
    r = lax.broadcasted_iota(jnp.int32, (L, L), 0)
    c = lax.broadcasted_iota(jnp.int32, (L, L), 1)
    cs = _tri_dot(jnp.where(c <= r, 1.0, 0.0).astype(BF16), dtv * A)
    return lane, hl, dtv, A, cs, r, c


def _halves(lane, v0, v1):
    return jnp.where(lane < SSD_HEAD_DIM, v0, v1)


def _ssd_fwd(xc, rest, col, dtb, alog, dskip, *, name):
    S = xc.shape[0]
    L = SSD_CHUNK
    nc = S // L

    def body(xc_ref, f_ref, dtb_ref, al_ref, dk_ref, y_ref, hs_ref, h_ref, cst_ref):
        i = pl.program_id(0)

        @pl.when(i == 0)
        def _():
            h_ref[...] = jnp.zeros_like(h_ref)

        lane, hl, dtv, A, cs, r, c = _ssd_prep(f_ref[...], dtb_ref[...], al_ref[...])
        cst_ref[...] = cs.T
        cs_last = cs[L - 1:L, :]
        ecs = jnp.exp(cs)
        dec = jnp.exp(cs_last - cs)
        cd = jnp.exp(cs_last)
        dkv = dk_ref[...]
        prow = lax.broadcasted_iota(jnp.int32, (LANE, 1), 0)
        for g in range(SSD_GROUPS):
            Bg = xc_ref[:, SSD_B0 + g * SSD_STATE:SSD_B0 + (g + 1) * SSD_STATE].astype(BF16)
            Cg = xc_ref[:, SSD_C0 + g * SSD_STATE:SSD_C0 + (g + 1) * SSD_STATE].astype(BF16)
            CB = lax.dot_general(Cg, Bg, NT_DIMS, preferred_element_type=F32)
            for pp in range(SSD_PAIRS // SSD_GROUPS):
                pi = g * (SSD_PAIRS // SSD_GROUPS) + pp
                hl0 = DT_LANE0 + 2 * pi
                x2 = xc_ref[:, pi * LANE:(pi + 1) * LANE]
                xd = x2 * _halves(lane, _col(dtv, hl0, lane), _col(dtv, hl0 + 1, lane))
                xdb = xd.astype(BF16)
                yd = jnp.zeros((L, LANE), F32)
                for hh in range(2):
                    seg = _col(cs, hl0 + hh, lane) - cst_ref[hl0 + hh:hl0 + hh + 1, :]
                    M = CB * jnp.exp(jnp.where(c <= r, seg, NEG))
                    yh = jnp.dot(M.astype(BF16), xdb, preferred_element_type=F32)
                    yd = jnp.where((lane >= SSD_HEAD_DIM) if hh else (lane < SSD_HEAD_DIM), yh, yd)
                hp = h_ref[pi]
                hs_ref[0, pi] = hp
                yo = lax.dot_general(Cg, hp.astype(BF16), NT_DIMS, preferred_element_type=F32)
                yo = yo * _halves(lane, _col(ecs, hl0, lane), _col(ecs, hl0 + 1, lane))
                dsk = _halves(lane, _col(dkv, hl0, lane), _col(dkv, hl0 + 1, lane))
                y_ref[:, pi * LANE:(pi + 1) * LANE] = yd + yo + dsk * x2
                xw = (xd * _halves(lane, _col(dec, hl0, lane), _col(dec, hl0 + 1, lane))).astype(BF16)
                st = lax.dot_general(xw, Bg, TN_DIMS, preferred_element_type=F32)
                cdp = jnp.where(prow < SSD_HEAD_DIM, _col(cd, hl0, lane), _col(cd, hl0 + 1, lane))
                h_ref[pi] = cdp * hp + st

    vec = pl.BlockSpec((1, LANE), lambda i: (0, 0))
    return pl.pallas_call(
        body, name=name, grid=(nc,),
        in_specs=[pl.BlockSpec((L, SSD_CH), lambda i: (i, 0)), pl.BlockSpec((L, LANE), lambda i: (i, col)),
                  vec, vec, vec],
        out_specs=[pl.BlockSpec((L, SSD_X), lambda i: (i, 0)),
                   pl.BlockSpec((1, SSD_PAIRS, LANE, SSD_STATE), lambda i: (i, 0, 0, 0))],
        out_shape=[jax.ShapeDtypeStruct((S, SSD_X), F32),
                   jax.ShapeDtypeStruct((nc, SSD_PAIRS, LANE, SSD_STATE), F32)],
        scratch_shapes=[pltpu.VMEM((SSD_PAIRS, LANE, SSD_STATE), F32), pltpu.VMEM((LANE, L), F32)],
        compiler_params=_cparams("arbitrary"),
    )(xc, rest, dtb, alog, dskip)


def _pair_sums(lane, v):
    lo = jnp.sum(jnp.where(lane < SSD_HEAD_DIM, v, 0.0), axis=1, keepdims=True)
    return lo, jnp.sum(v, axis=1, keepdims=True) - lo


def _ssd_bwd(xc, rest, col, dtb, alog, dskip, hs, dy, *, name):
    S = xc.shape[0]
    L = SSD_CHUNK
    nc = S // L
    PG = SSD_PAIRS // SSD_GROUPS

    def body(xc_ref, f_ref, dtb_ref, al_ref, dk_ref, hs_ref, dy_ref, dxc_ref, ddt_ref, dp_ref, dh_ref, cst_ref):
        i = pl.program_id(0)

        @pl.when(i == 0)
        def _():
            dh_ref[...] = jnp.zeros_like(dh_ref)
            dp_ref[...] = jnp.zeros_like(dp_ref)

        fv = f_ref[...] + dtb_ref[...]
        lane, hl, dtv, A, cs, r, c = _ssd_prep(f_ref[...], dtb_ref[...], al_ref[...])
        cst_ref[...] = cs.T
        cs_last = cs[L - 1:L, :]
        ecs = jnp.exp(cs)
        dec = jnp.exp(cs_last - cs)
        cd = jnp.exp(cs_last)
        dkv = dk_ref[...]
        prow = lax.broadcasted_iota(jnp.int32, (LANE, 1), 0)
        lrow = lax.broadcasted_iota(jnp.int32, (L, 1), 0)
        is_last = lrow == L - 1
        causal = c <= r
        dcs = jnp.zeros((L, LANE), F32)
        ddt = jnp.zeros((L, LANE), F32)
        dD = jnp.zeros((1, LANE), F32)
        for g in range(SSD_GROUPS):
            Bg = xc_ref[:, SSD_B0 + g * SSD_STATE:SSD_B0 + (g + 1) * SSD_STATE].astype(BF16)
            Cg = xc_ref[:, SSD_C0 + g * SSD_STATE:SSD_C0 + (g + 1) * SSD_STATE].astype(BF16)
            CB = lax.dot_general(Cg, Bg, NT_DIMS, preferred_element_type=F32)
            dCB = jnp.zeros((L, L), F32)
            dB = jnp.zeros((L, SSD_STATE), F32)
            dC = jnp.zeros((L, SSD_STATE), F32)
            for pp in range(PG):
                pi = g * PG + pp
                hl0 = DT_LANE0 + 2 * pi
                x2 = xc_ref[:, pi * LANE:(pi + 1) * LANE]
                dy2 = dy_ref[:, pi * LANE:(pi + 1) * LANE]
                dtp = _halves(lane, _col(dtv, hl0, lane), _col(dtv, hl0 + 1, lane))
                xd = x2 * dtp
                xdb = xd.astype(BF16)
                dsk = _halves(lane, _col(dkv, hl0, lane), _col(dkv, hl0 + 1, lane))
                dx2 = dsk * dy2
                sD = _pair_sums(lane, dy2 * x2)
                hp = hs_ref[0, pi]
                hpb = hp.astype(BF16)
                ecsp = _halves(lane, _col(ecs, hl0, lane), _col(ecs, hl0 + 1, lane))
                yo = lax.dot_general(Cg, hpb, NT_DIMS, preferred_element_type=F32) * ecsp
                dW = (ecsp * dy2).astype(BF16)
                dC = dC + jnp.dot(dW, hpb, preferred_element_type=F32)
                dhp = lax.dot_general(dW, Cg, TN_DIMS, preferred_element_type=F32)
                sYo = _pair_sums(lane, dy2 * yo)
                dhn = dh_ref[pi]
                cdh = (_col(cd, hl0, lane), _col(cd, hl0 + 1, lane))
                dhp = dhp + jnp.where(prow < SSD_HEAD_DIM, cdh[0], cdh[1]) * dhn
                rs = jnp.sum(dhn * hp, axis=1, keepdims=True)
                lo = jnp.sum(jnp.where(prow < SSD_HEAD_DIM, rs, 0.0), axis=0, keepdims=True)
                dcd = (lo, jnp.sum(rs, axis=0, keepdims=True) - lo)
                dhnb = dhn.astype(BF16)
                decp = _halves(lane, _col(dec, hl0, lane), _col(dec, hl0 + 1, lane))
                G = lax.dot_general(Bg, dhnb, NT_DIMS, preferred_element_type=F32)
                dxd = decp * G
                sdec = _pair_sums(lane, xd * G)
                dB = dB + jnp.dot((xd * decp).astype(BF16), dhnb, preferred_element_type=F32)
                dh_ref[pi] = dhp
                for hh in range(2):
                    hmask = (lane >= SSD_HEAD_DIM) if hh else (lane < SSD_HEAD_DIM)
                    seg = _col(cs, hl0 + hh, lane) - cst_ref[hl0 + hh:hl0 + hh + 1, :]
                    Lm = jnp.exp(jnp.where(causal, seg, NEG))
                    M = CB * Lm
                    dyh = jnp.where(hmask, dy2, 0.0).astype(BF16)
                    dM = lax.dot_general(dyh, xdb, NT_DIMS, preferred_element_type=F32)
                    dxd = dxd + lax.dot_general(M.astype(BF16), dyh, TN_DIMS, preferred_element_type=F32)
                    Q = dM * M
                    dCB = dCB + dM * Lm
                    dech = _col(dec, hl0 + hh, lane)
                    dd = sdec[hh] * dech
                    end = dcd[hh] * cdh[hh] + jnp.sum(dd, axis=0, keepdims=True)
                    dcs_h = (sYo[hh] - dd + jnp.sum(Q, axis=1, keepdims=True)
                             - jnp.sum(Q.T, axis=1, keepdims=True) + jnp.where(is_last, end, 0.0))
                    dcs = jnp.where(lane == hl0 + hh, dcs_h, dcs)
                    dD = jnp.where(lane == hl0 + hh, jnp.sum(sD[hh], axis=0, keepdims=True), dD)
                sdt = _pair_sums(lane, dxd * x2)
                ddt = jnp.where(lane == hl0, sdt[0], jnp.where(lane == hl0 + 1, sdt[1], ddt))
                dxc_ref[:, pi * LANE:(pi + 1) * LANE] = dx2 + dxd * dtp
            dCBb = dCB.astype(BF16)
            dC = dC + jnp.dot(dCBb, Bg, preferred_element_type=F32)
            dB = dB + lax.dot_general(dCBb, Cg, TN_DIMS, preferred_element_type=F32)
            dxc_ref[:, SSD_B0 + g * SSD_STATE:SSD_B0 + (g + 1) * SSD_STATE] = dB
            dxc_ref[:, SSD_C0 + g * SSD_STATE:SSD_C0 + (g + 1) * SSD_STATE] = dC
        da = _tri_dot(jnp.where(c >= r, 1.0, 0.0).astype(BF16), dcs)
        ddtv = ddt + da * A
        ddt_ref[...] = jnp.where(hl, ddtv * _sigmoid(fv), 0.0)
        dal = jnp.sum(da * dtv, axis=0, keepdims=True) * A
        dp_ref[0:1, :] += dal
        dp_ref[1:2, :] += dD

    rev = lambda i: nc - 1 - i
    vec = pl.BlockSpec((1, LANE), lambda i: (0, 0))
    return pl.pallas_call(
        body, name=name, grid=(nc,),
        in_specs=[pl.BlockSpec((L, SSD_CH), lambda i: (rev(i), 0)), pl.BlockSpec((L, LANE), lambda i: (rev(i), col)),
                  vec, vec, vec,
                  pl.BlockSpec((1, SSD_PAIRS, LANE, SSD_STATE), lambda i: (rev(i), 0, 0, 0)),
                  pl.BlockSpec((L, SSD_X), lambda i: (rev(i), 0))],
        out_specs=[pl.BlockSpec((L, SSD_CH), lambda i: (rev(i), 0)), pl.BlockSpec((L, LANE), lambda i: (rev(i), 0)),
                   pl.BlockSpec((SUBLANE, LANE), lambda i: (0, 0))],
        out_shape=[jax.ShapeDtypeStruct((S, SSD_CH), F32), jax.ShapeDtypeStruct((S, LANE), F32),
                   jax.ShapeDtypeStruct((SUBLANE, LANE), F32)],
        scratch_shapes=[pltpu.VMEM((SSD_PAIRS, LANE, SSD_STATE), F32), pltpu.VMEM((LANE, L), F32)],
        compiler_params=_cparams("arbitrary"),
    )(xc, rest, dtb, alog, dskip, hs, dy)


def _gate_norm_fwd(y, rest, zcol, nw, *, name):
    S, W = y.shape
    ts = _row_tile(S)
    GW = W // SSD_GROUPS

    def body(y_ref, z_ref, nw_ref, o_ref):
        z = z_ref[...]
        t = y_ref[...] * (z * _sigmoid(z))
        for g in range(SSD_GROUPS):
            cols = slice(g * GW, (g + 1) * GW)
            tg = t[:, cols]
            rr = lax.rsqrt(jnp.mean(tg * tg, axis=-1, keepdims=True) + NORM_EPS)
            o_ref[:, cols] = (tg * rr * nw_ref[:, cols]).astype(BF16)

    row = pl.BlockSpec((ts, W), lambda i: (i, 0))
    return pl.pallas_call(
        body, name=name, grid=(S // ts,),
        in_specs=[row, pl.BlockSpec((ts, W), lambda i: (i, zcol)), pl.BlockSpec((1, W), lambda i: (0, 0))],
        out_specs=row,
        out_shape=jax.ShapeDtypeStruct((S, W), BF16),
        compiler_params=_cparams("parallel"),
    )(y, rest, nw)


def _gate_norm_bwd(y, rest, zcol, nw, do, *, name):
    S, W = y.shape
    ts = _row_tile(S)
    GW = W // SSD_GROUPS

    def body(y_ref, z_ref, nw_ref, do_ref, dy_ref, dz_ref, dnw_ref):
        z = z_ref[...]
        yv = y_ref[...]
        s, ds = _dsilu(z)
        sz = z * s
        t = yv * sz
        dov = do_ref[...]
        parts = []
        for g in range(SSD_GROUPS):
            cols = slice(g * GW, (g + 1) * GW)
            tg = t[:, cols]
            rr = lax.rsqrt(jnp.mean(tg * tg, axis=-1, keepdims=True) + NORM_EPS)
            n = tg * rr
            dog = dov[:, cols]
            dn = dog * nw_ref[:, cols]
            dt = rr * (dn - n * jnp.mean(dn * n, axis=-1, keepdims=True))
            dy_ref[:, cols] = dt * sz[:, cols]
            dz_ref[:, cols] = (dt * yv[:, cols] * ds[:, cols]).astype(BF16)
            parts.append(jnp.sum(dog * n, axis=0, keepdims=True))
        _acc_out(dnw_ref, jnp.concatenate(parts, axis=1), pl.program_id(0) == 0)

    row = pl.BlockSpec((ts, W), lambda i: (i, 0))
    vec = pl.BlockSpec((1, W), lambda i: (0, 0))
    return pl.pallas_call(
        body, name=name, grid=(S // ts,),
        in_specs=[row, pl.BlockSpec((ts, W), lambda i: (i, zcol)), vec, row],
        out_specs=[row, row, vec],
        out_shape=[jax.ShapeDtypeStruct((S, W), F32), jax.ShapeDtypeStruct((S, W), BF16),
                   jax.ShapeDtypeStruct((1, W), F32)],
        compiler_params=_cparams("arbitrary"),
    )(y, rest, nw, do)


N_DEV = 8
MESH = pl.DeviceIdType.MESH
ANY = pl.BlockSpec(memory_space=pl.ANY)


def _all_gather(x, *, name):
    m_per, n = x.shape

    def body(x_ref, out_ref, send_sems, recv_sems, local_sem):
        px, py, pc = lax.axis_index("x"), lax.axis_index("y"), lax.axis_index("c")
        me, sibling = (px, py, pc), (px, py, 1 - pc)
        chips = [(1 - px, py), (px, 1 - py), (1 - px, 1 - py)]

        def rows(bx, by, bc):
            return out_ref.at[pl.ds((4 * bx + 2 * by + bc) * m_per, m_per), :]

        def copy(k, block, to, src=None):
            return pltpu.make_async_remote_copy(
                src_ref=rows(*block) if src is None else src, dst_ref=rows(*block),
                send_sem=send_sems.at[k], recv_sem=recv_sems.at[k], device_id=to, device_id_type=MESH)

        mine = pltpu.make_async_copy(x_ref, rows(*me), local_sem)
        mine.start()
        first = [copy(0, me, sibling, src=x_ref)]
        first += [copy(1 + j, me, (*chip, pc), src=x_ref) for j, chip in enumerate(chips)]
        for cp in first:
            cp.start()
        passed = [copy(4 + j, (*chip, pc), sibling) for j, chip in enumerate(chips)]
        for j, chip in enumerate(chips):
            copy(1 + j, (*chip, pc), me).wait_recv()
            passed[j].start()
        copy(0, sibling, me).wait_recv()
        for j, chip in enumerate(chips):
            copy(4 + j, (*chip, 1 - pc), me).wait_recv()
        for cp in first + passed:
            cp.wait_send()
        mine.wait()

    return pl.pallas_call(
        body, name=name,
        out_shape=jax.ShapeDtypeStruct((N_DEV * m_per, n), x.dtype),
        in_specs=[ANY], out_specs=ANY,
        scratch_shapes=[pltpu.SemaphoreType.DMA((N_DEV - 1,)), pltpu.SemaphoreType.DMA((N_DEV - 1,)),
                        pltpu.SemaphoreType.DMA],
    )(x)


def _exchange(send, *, name):
    _, R, C = send.shape

    def body(send_ref, recv_ref, send_sems, recv_sems, local_sem):
        px, py, pc = lax.axis_index("x"), lax.axis_index("y"), lax.axis_index("c")
        me = 4 * px + 2 * py + pc
        local = pltpu.make_async_copy(send_ref.at[me], recv_ref.at[me], local_sem)
        local.start()
        copies = []
        for k in range(1, N_DEV):
            qx = 1 - px if k & 4 else px
            qy = 1 - py if k & 2 else py
            qc = 1 - pc if k & 1 else pc
            peer = 4 * qx + 2 * qy + qc
            cp = pltpu.make_async_remote_copy(
                src_ref=send_ref.at[peer], dst_ref=recv_ref.at[me],
                send_sem=send_sems.at[k - 1], recv_sem=recv_sems.at[k - 1],
                device_id=(qx, qy, qc), device_id_type=MESH)
            cp.start()
            copies.append((cp, pltpu.make_async_remote_copy(
                src_ref=send_ref.at[peer], dst_ref=recv_ref.at[peer],
                send_sem=send_sems.at[k - 1], recv_sem=recv_sems.at[k - 1],
                device_id=(qx, qy, qc), device_id_type=MESH)))
        for cp, landing in copies:
            landing.wait_recv()
        for cp, landing in copies:
            cp.wait_send()
        local.wait()

    return pl.pallas_call(
        body, name=name,
        out_shape=jax.ShapeDtypeStruct(send.shape, send.dtype),
        in_specs=[ANY], out_specs=ANY,
        scratch_shapes=[pltpu.SemaphoreType.DMA((N_DEV - 1,)), pltpu.SemaphoreType.DMA((N_DEV - 1,)),
                        pltpu.SemaphoreType.DMA],
    )(send)


FLAT_W = 1024
ADAM_ROWS = 64


def _adam_math(w, g, m, v):
    m2 = ADAM_B1 * m + (1.0 - ADAM_B1) * g
    v2 = ADAM_B2 * v + (1.0 - ADAM_B2) * (g * g)
    m_hat = m2 / (1.0 - ADAM_B1 ** ADAM_STEP)
    v_hat = v2 / (1.0 - ADAM_B2 ** ADAM_STEP)
    delta = -ADAM_LR * (m_hat / (jnp.sqrt(v_hat) + ADAM_EPS) + ADAM_WD * w)
    return delta, m2, v2


def _sum_adam(parts, w, m, v, *, name):
    _, R, C = parts.shape
    tr = ADAM_ROWS

    def body(p_ref, w_ref, m_ref, v_ref, g_ref, d_ref, m2_ref, v2_ref):
        g = p_ref[0]
        for s in range(1, N_DEV):
            g = g + p_ref[s]
        delta, m2, v2 = _adam_math(w_ref[...], g, m_ref[...], v_ref[...])
        g_ref[...] = g
        d_ref[...] = delta
        m2_ref[...] = m2
        v2_ref[...] = v2

    row = pl.BlockSpec((tr, C), lambda i: (i, 0))
    return pl.pallas_call(
        body, name=name, grid=(R // tr,),
        in_specs=[pl.BlockSpec((N_DEV, tr, C), lambda i: (0, i, 0)), row, row, row],
        out_specs=[row] * 4,
        out_shape=[jax.ShapeDtypeStruct((R, C), F32)] * 4,
        compiler_params=_cparams("parallel"),
    )(parts, w, m, v)


def _adam(g, w, m, v, *, name):
    R, C = g.shape
    tr = _tile_rows(R)

    def body(g_ref, w_ref, m_ref, v_ref, d_ref, m2_ref, v2_ref):
        delta, m2, v2 = _adam_math(w_ref[...], g_ref[...], m_ref[...], v_ref[...])
        d_ref[...] = delta
        m2_ref[...] = m2
        v2_ref[...] = v2

    row = pl.BlockSpec((tr, C), lambda i: (i, 0))
    return pl.pallas_call(
        body, name=name, grid=(R // tr,),
        in_specs=[row] * 4, out_specs=[row] * 3,
        out_shape=[jax.ShapeDtypeStruct((R, C), F32)] * 3,
        compiler_params=_cparams("parallel"),
    )(g, w, m, v)


def _tile_rows(R):
    return ADAM_ROWS if R % ADAM_ROWS == 0 else R


IN_POOL, IN_QKV, IN_F, IN_Z, IN_XBC, IN_DT, IN_GATE = 0, 512, 2048, 2056, 3080, 4616, 4632
IN_TOTAL = 7704
REST_Z_BLK, REST_POOL_BLK, REST_XBC0, REST_FDT_BLK = 3, 8, 4608, 48
REST_W = 6272

SHARDED = ("w_in", "p_pool", "p_attn", "p_ssd", "w_out", "ffn_up", "ffn_down", "ssd_conv_w", "ffn_conv_w")
ROW_SHARDED = ("p_ssd", "w_out", "ffn_down")
MATMUL_W = ("w_in", "p_pool", "p_attn", "p_ssd", "w_out", "ffn_up", "ffn_down")
REPLICATED = ("norm_mix", "pool_mix", "pool_scale", "f_bias", "ssd_conv_b", "ssd_dt_bias", "ssd_a_log", "ssd_d",
              "ssd_norm", "norm_ffn", "ffn_conv_b", "norm_final")
WEIGHTS = ("norm_mix", "w_in", "pool_mix", "pool_scale", "f_bias", "ssd_conv_w", "ssd_conv_b", "ssd_dt_bias",
           "ssd_a_log", "ssd_d", "ssd_norm", "p_pool", "p_attn", "p_ssd", "w_out", "norm_ffn", "ffn_up",
           "ffn_conv_w", "ffn_conv_b", "ffn_down", "norm_final")


def _round_up(n, k):
    return -(-n // k) * k


def _flat_rows(parts, rows=None, width=FLAT_W):
    flat = jnp.concatenate([p.reshape(-1) for p in parts])
    if rows is None:
        rows = _round_up(-(-flat.shape[0] // width), SUBLANE)
    return jnp.pad(flat, (0, rows * width - flat.shape[0])).reshape(rows, width)


def _unshard(g, name, shape):
    blocks = g.reshape((N_DEV,) + shape)
    if name in ROW_SHARDED:
        return jnp.moveaxis(blocks, 0, 1).reshape(shape[0], N_DEV * shape[1], shape[2])
    return jnp.moveaxis(blocks, 0, -2).reshape(shape[:-1] + (N_DEV * shape[-1],))


def _shard_rows(full, name):
    if name in ROW_SHARDED:
        L, Rr, Cc = full.shape
        return jnp.moveaxis(full.reshape(L, N_DEV, Rr // N_DEV, Cc), 1, 0).reshape(N_DEV, -1)
    lead, Cc = full.shape[:-1], full.shape[-1]
    return jnp.moveaxis(full.reshape(lead + (N_DEV, Cc // N_DEV)), -2, 0).reshape(N_DEV, -1)


def _lanes(v, lane0):
    return jnp.zeros((1, LANE), F32).at[0, lane0:lane0 + v.shape[0]].set(v)


def _layer_params(full, rep, l):
    w_in = full["w_in"][l]
    w_rest = jnp.concatenate(
        [w_in[:, IN_GATE:], w_in[:, IN_Z:IN_XBC], w_in[:, IN_POOL:IN_QKV], w_in[:, IN_XBC:IN_DT],
         w_in[:, IN_F:IN_Z], w_in[:, IN_DT:IN_GATE],
         jnp.zeros((w_in.shape[0], LANE - (IN_Z - IN_F) - (IN_GATE - IN_DT)), BF16)], axis=1)
    F = full["ffn_down"].shape[1]
    return dict(
        nm=rep["norm_mix"][l][None], w_qkv=w_in[:, IN_QKV:IN_F], w_rest=w_rest,
        mix=rep["pool_mix"][l].astype(BF16), pscale=rep["pool_scale"][l][None],
        fb=_lanes(rep["f_bias"][l], 0),
        cw=full["ssd_conv_w"][l], cb=rep["ssd_conv_b"][l][None],
        dtb=_lanes(rep["ssd_dt_bias"][l], DT_LANE0), alog=_lanes(rep["ssd_a_log"][l], DT_LANE0),
        dsk=_lanes(rep["ssd_d"][l], DT_LANE0), snw=rep["ssd_norm"][l][None],
        p_pool=full["p_pool"][l], p_attn=full["p_attn"][l], p_ssd=full["p_ssd"][l], w_out=full["w_out"][l],
        nf=rep["norm_ffn"][l][None], w_up=full["ffn_up"][l], w_up_g=full["ffn_up"][l][:, :F],
        w_up_v=full["ffn_up"][l][:, F:], fcw=full["ffn_conv_w"][l], fcb=rep["ffn_conv_b"][l][None],
        w_down=full["ffn_down"][l])


def _layer_fwd(x, P, l):
    S = x.shape[0]
    nm = lambda s: f"l{l}_{s}"
    u = _rms_fwd(x, P["nm"], name=nm("rms_mix"))
    qkv = _mm(u, P["w_qkv"], out_dtype=BF16, name=nm("proj_qkv"))
    rest = _mm(u, P["w_rest"], name=nm("proj_rest"))
    d, ya0 = _pool_fwd(rest, REST_POOL_BLK, P["mix"], P["pscale"], name=nm("pool"))
    ya = _mm(ya0, P["p_pool"], name=nm("p_pool"))
    ccol, crow = _fox_pre(rest, REST_FDT_BLK, P["fb"], name=nm("fox_pre"))
    crow = crow.reshape(ATTN_HEADS, 1, S)
    o, lse = _attn_fwd(qkv, ccol, crow, name=nm("attn"))
    yb = _mm(o, P["p_attn"], name=nm("p_attn"))
    xc = _conv_silu_fwd(rest, REST_XBC0, SSD_CH, P["cw"], P["cb"], name=nm("ssd_conv"))
    y, hs = _ssd_fwd(xc, rest, REST_FDT_BLK, P["dtb"], P["alog"], P["dsk"], name=nm("ssd"))
    yc0 = _gate_norm_fwd(y, rest, REST_Z_BLK, P["snw"], name=nm("ssd_norm"))
    yc = _mm(yc0, P["p_ssd"], name=nm("p_ssd"))
    merged = _merge_fwd(rest, ya, yb, yc, name=nm("merge"))
    x1 = _mm(merged, P["w_out"], add=x, name=nm("w_out"))
    u2 = _rms_fwd(x1, P["nf"], name=nm("rms_ffn"))
    h0 = _mm(u2, P["w_up"], name=nm("ffn_up"))
    a = _ffn_act_fwd(h0, P["fcw"], P["fcb"], name=nm("ffn_act"))
    x2 = _mm(a, P["w_down"], add=x1, name=nm("ffn_down"))
    saved = dict(x=x, u=u, qkv=qkv, rest=rest, d=d, ya0=ya0, ya=ya, ccol=ccol, crow=crow, o=o, lse=lse, yb=yb,
                 xc=xc, y=y, hs=hs, yc0=yc0, yc=yc, merged=merged, x1=x1, u2=u2, h0=h0, a=a)
    return x2, saved


def _layer_bwd(g, gb, sv, P, l):
    nm = lambda s: f"l{l}_{s}_bwd"
    rest = sv["rest"]
    G = {}
    da = _mm(gb, P["w_down"], tb=True, name=nm("ffn_down_dx"))
    G["ffn_down"] = _mm(sv["a"], gb, ta=True, name=nm("ffn_down_dw"))
    dhg, dhv, dfcw, dfcb = _ffn_act_bwd(sv["h0"], P["fcw"], P["fcb"], da, name=nm("ffn_act"))
    du2 = _mm(dhg, P["w_up_g"], tb=True, name=nm("ffn_up_dx_g"))
    du2 = _mm(dhv, P["w_up_v"], tb=True, add=du2, name=nm("ffn_up_dx_v"))
    G["ffn_up"] = jnp.concatenate([_mm(sv["u2"], dhg, ta=True, name=nm("ffn_up_dw_g")),
                                   _mm(sv["u2"], dhv, ta=True, name=nm("ffn_up_dw_v"))], axis=1)
    G["ffn_conv_w"] = jnp.concatenate([dfcw[0], dfcw[1]], axis=1)
    G["ffn_conv_b"] = jnp.concatenate([dfcb[0, 0], dfcb[1, 0]])
    dx1, dx1b, dnf = _rms_bwd(sv["x1"], P["nf"], du2, g, name=nm("rms_ffn"))
    G["norm_ffn"] = dnf[0]
    dmerged = _mm(dx1b, P["w_out"], tb=True, name=nm("w_out_dx"))
    G["w_out"] = _mm(sv["merged"], dx1b, ta=True, name=nm("w_out_dw"))
    dgl, dya, dyb, dyc = _merge_bwd(rest, sv["ya"], sv["yb"], sv["yc"], dmerged, name=nm("merge"))
    G["p_pool"] = _mm(sv["ya0"], dya, ta=True, name=nm("p_pool_dw"))
    dya0 = _mm(dya, P["p_pool"], tb=True, name=nm("p_pool_dx"))
    G["p_attn"] = _mm(sv["o"], dyb, ta=True, name=nm("p_attn_dw"))
    do = _mm(dyb, P["p_attn"], tb=True, out_dtype=BF16, name=nm("p_attn_dx"))
    G["p_ssd"] = _mm(sv["yc0"], dyc, ta=True, name=nm("p_ssd_dw"))
    dyc0 = _mm(dyc, P["p_ssd"], tb=True, name=nm("p_ssd_dx"))
    dpv, dmix, dpsc = _pool_bwd(dya0, sv["d"], P["mix"], P["pscale"], name=nm("pool"))
    G["pool_mix"], G["pool_scale"] = dmix, dpsc[0]
    dq, dcq, lrow, drow = _attn_bwd_dq(sv["qkv"], sv["o"], do, sv["lse"], sv["ccol"], sv["crow"], name=nm("attn_dq"))
    dk, dv, dck = _attn_bwd_dkv(sv["qkv"], do, lrow, drow, sv["ccol"], sv["crow"], name=nm("attn_dkv"))
    dy, dz, dsnw = _gate_norm_bwd(sv["y"], rest, REST_Z_BLK, P["snw"], dyc0, name=nm("ssd_norm"))
    G["ssd_norm"] = dsnw[0]
    dxc, ddt, dpar = _ssd_bwd(sv["xc"], rest, REST_FDT_BLK, P["dtb"], P["alog"], P["dsk"], sv["hs"], dy, name=nm("ssd"))
    G["ssd_a_log"] = dpar[0, DT_LANE0:DT_LANE0 + SSD_HEADS]
    G["ssd_d"] = dpar[1, DT_LANE0:DT_LANE0 + SSD_HEADS]
    dxbc, dcw, dcb = _conv_silu_bwd(rest, REST_XBC0, SSD_CH, P["cw"], P["cb"], dxc, name=nm("ssd_conv"))
    G["ssd_conv_w"], G["ssd_conv_b"] = dcw, dcb[0]
    dfdt, dfdtb = _fox_post(dcq, dck, rest, REST_FDT_BLK, P["fb"], ddt, name=nm("fox_post"))
    G["f_bias"] = dfdtb[0, :ATTN_HEADS]
    G["ssd_dt_bias"] = dfdtb[0, DT_LANE0:DT_LANE0 + SSD_HEADS]
    dqkv = jnp.concatenate([dq, dk, dv], axis=1)
    drest = jnp.concatenate([dgl, dz, dpv, dxbc, dfdt], axis=1)
    du = _mm(dqkv, P["w_qkv"], tb=True, name=nm("proj_qkv_dx"))
    du = _mm(drest, P["w_rest"], tb=True, add=du, name=nm("proj_rest_dx"))
    dwq = _mm(sv["u"], dqkv, ta=True, name=nm("proj_qkv_dw"))
    dwr = _mm(sv["u"], drest, ta=True, name=nm("proj_rest_dw"))
    nf_, ndt = IN_Z - IN_F, IN_GATE - IN_DT
    G["w_in"] = jnp.concatenate(
        [dwr[:, 4096:4608], dwq, dwr[:, 6144:6144 + nf_], dwr[:, 3072:4096], dwr[:, 4608:6144],
         dwr[:, 6144 + nf_:6144 + nf_ + ndt], dwr[:, 0:3072]], axis=1)
    dx, dxb, dnm = _rms_bwd(sv["x"], P["nm"], du, dx1, name=nm("rms_mix"))
    G["norm_mix"] = dnm[0]
    return dx, dxb, G


def kernel(x, norm_mix, w_in, pool_mix, pool_scale, f_bias, ssd_conv_w, ssd_conv_b, ssd_dt_bias, ssd_a_log, ssd_d, ssd_norm, p_pool, p_attn, p_ssd, w_out, norm_ffn, ffn_up, ffn_conv_w, ffn_conv_b, ffn_down, norm_final, loss_target, m_norm_mix, m_w_in, m_pool_mix, m_pool_scale, m_f_bias, m_ssd_conv_w, m_ssd_conv_b, m_ssd_dt_bias, m_ssd_a_log, m_ssd_d, m_ssd_norm, m_p_pool, m_p_attn, m_p_ssd, m_w_out, m_norm_ffn, m_ffn_up, m_ffn_conv_w, m_ffn_conv_b, m_ffn_down, m_norm_final, v_norm_mix, v_w_in, v_pool_mix, v_pool_scale, v_f_bias, v_ssd_conv_w, v_ssd_conv_b, v_ssd_dt_bias, v_ssd_a_log, v_ssd_d, v_ssd_norm, v_p_pool, v_p_attn, v_p_ssd, v_w_out, v_norm_ffn, v_ffn_up, v_ffn_conv_w, v_ffn_conv_b, v_ffn_down, v_norm_final):
    args = dict(locals())
    W = {n: args[n] for n in WEIGHTS}
    M = {n: args["m_" + n] for n in WEIGHTS}
    V = {n: args["v_" + n] for n in WEIGHTS}
    depth = norm_mix.shape[0]

    payload = [W[n].astype(BF16) for n in MATMUL_W]
    payload += [lax.bitcast_convert_type(W[n], BF16) for n in SHARDED if n not in MATMUL_W]
    mine = _flat_rows(payload, rows=None)
    mine = jnp.pad(mine, ((0, _round_up(mine.shape[0], 2 * SUBLANE) - mine.shape[0]), (0, 0)))
    gathered = _all_gather(mine, name="gather_weights").reshape(N_DEV, -1)
    full, off = {}, 0
    for n in MATMUL_W:
        size = W[n].size
        full[n] = _unshard(gathered[:, off:off + size], n, W[n].shape)
        off += size
    for n in SHARDED:
        if n not in MATMUL_W:
            size = 2 * W[n].size
            bits = gathered[:, off:off + size].reshape(N_DEV, W[n].size, 2)
            full[n] = _unshard(lax.bitcast_convert_type(bits, F32), n, W[n].shape)
            off += size
    rep = {n: W[n] for n in REPLICATED}

    params = [_layer_params(full, rep, l) for l in range(depth)]
    h, saves = x[0], []
    for l in range(depth):
        h, sv = _layer_fwd(h, params[l], l)
        saves.append(sv)
    loss_part, g, gb, dnfin = _loss_head(h, norm_final[None], loss_target[0], name="loss_head")
    layer_grads = [None] * depth
    for l in reversed(range(depth)):
        g, gb, layer_grads[l] = _layer_bwd(g, gb, saves[l], params[l], l)
    grads = {n: jnp.stack([layer_grads[l][n] for l in range(depth)]) for n in WEIGHTS if n != "norm_final"}
    grads["norm_final"] = dnfin[0]

    shard_part = jnp.concatenate([_shard_rows(grads[n], n) for n in SHARDED], axis=1)
    rep_flat = jnp.concatenate([grads[n].reshape(-1) for n in REPLICATED])
    rep_rows = _round_up(-(-rep_flat.shape[0] // (N_DEV * FLAT_W)), SUBLANE)
    rep_part = jnp.pad(rep_flat, (0, N_DEV * rep_rows * FLAT_W - rep_flat.shape[0])).reshape(N_DEV, rep_rows * FLAT_W)
    shard_rows = -(-shard_part.shape[1] // FLAT_W)
    total_rows = _round_up(shard_rows + rep_rows, ADAM_ROWS)
    send = jnp.concatenate(
        [jnp.pad(shard_part, ((0, 0), (0, shard_rows * FLAT_W - shard_part.shape[1]))), rep_part,
         jnp.zeros((N_DEV, (total_rows - shard_rows - rep_rows) * FLAT_W), F32)], axis=1)
    recv = _exchange(send.reshape(N_DEV, total_rows, FLAT_W), name="exchange_grads")

    flat = lambda D: _flat_rows([D[n] for n in SHARDED], rows=total_rows)
    g_sh, d_sh, m_sh, v_sh = _sum_adam(recv, flat(W), flat(M), flat(V), name="sum_adam_sharded")
    out = {}
    off = 0
    for n in SHARDED:
        size = W[n].size
        for key, arr in (("grad_", g_sh), ("delta_", d_sh), ("new_m_", m_sh), ("new_v_", v_sh)):
            out[key + n] = arr.reshape(-1)[off:off + size].reshape(W[n].shape)
        off += size

    rep_g = _all_gather(g_sh[shard_rows:shard_rows + rep_rows], name="gather_replicated_grads")
    flat_rep = lambda D: _flat_rows([D[n] for n in REPLICATED], rows=N_DEV * rep_rows)
    d_rp, m_rp, v_rp = _adam(rep_g, flat_rep(W), flat_rep(M), flat_rep(V), name="adam_replicated")
    off = 0
    for n in REPLICATED:
        size = W[n].size
        for key, arr in (("grad_", rep_g), ("delta_", d_rp), ("new_m_", m_rp), ("new_v_", v_rp)):
            out[key + n] = arr.reshape(-1)[off:off + size].reshape(W[n].shape)
        off += size

    loss = lax.psum(loss_part[0, 0], ("x", "y", "c"))
    return (loss, g[None], *[out["grad_" + n] for n in WEIGHTS], *[out["delta_" + n] for n in WEIGHTS],
            *[out["new_m_" + n] for n in WEIGHTS], *[out["new_v_" + n] for n in WEIGHTS])
```

```python
import functools

import jax
import jax.numpy as jnp
from jax import lax
from jax.experimental import pallas as pl
from jax.experimental.pallas import tpu as pltpu

F32 = jnp.float32
BF16 = jnp.bfloat16

LANE = 128
SUBLANE = 8
VMEM_LIMIT = 56 * 1024 * 1024

NORM_EPS = 1e-6
POOL_WINDOWS = (2, 4, 8, 16)
POOL_GROUP_DIM = 128
ATTN_HEADS = 8
ATTN_HEAD_DIM = 64
SSD_HEAD_DIM = 64
SSD_HEADS = 16
SSD_GROUPS = 2
SSD_STATE = 128
SSD_CHUNK = 128

ADAM_LR = 0.001
ADAM_B1 = 0.9
ADAM_B2 = 0.999
ADAM_EPS = 1e-08
ADAM_WD = 0.01
ADAM_STEP = 10


def _cparams(*sem):
    return pltpu.CompilerParams(dimension_semantics=tuple(sem), vmem_limit_bytes=VMEM_LIMIT)


def _tile(n, pref):
    if n <= pref:
        return n
    assert n % LANE == 0, n
    q = n // LANE
    best = 1
    for d in range(1, pref // LANE + 1):
        if q % d == 0:
            best = d
    return best * LANE


def _mm_call(args, in_specs, out_spec, out_shape, grid, *, ta, tb, nk, acc_shape, has_add, keep, name):
    dims = (((0 if ta else 1,), (1 if tb else 0,)), ((), ()))

    def body(*refs):
        a_ref, b_ref = refs[0], refs[1]
        o_ref, acc_ref = refs[-2], refs[-1]
        k = pl.program_id(2)

        @pl.when(k == 0)
        def _():
            acc_ref[...] = jnp.zeros_like(acc_ref)

        acc_ref[...] += lax.dot_general(a_ref[...], b_ref[...], dims, preferred_element_type=F32)

        @pl.when(k == nk - 1)
        def _():
            r = acc_ref[...]
            if has_add:
                r = r + refs[2][...]
            if keep is not None:
                r = r[:, :keep]
            o_ref[...] = r.astype(o_ref.dtype)

    return pl.pallas_call(
        body, name=name, grid=grid, in_specs=in_specs, out_specs=out_spec, out_shape=out_shape,
        scratch_shapes=[pltpu.VMEM(acc_shape, F32)],
        compiler_params=_cparams("parallel", "parallel", "arbitrary"),
    )(*args)


def _mm(a, b, *, ta=False, tb=False, add=None, out_dtype=F32, name):
    assert a.dtype == BF16 and b.dtype == BF16, (a.dtype, b.dtype)
    if ta:
        K, M = a.shape
    else:
        M, K = a.shape
    if tb:
        N, K2 = b.shape
    else:
        K2, N = b.shape
    assert K == K2, (a.shape, b.shape, ta, tb)
    tm, tn, tk = _tile(M, 512), _tile(N, 1024), _tile(K, 2048)
    a_spec = pl.BlockSpec((tk, tm), lambda i, j, k: (k, i)) if ta else pl.BlockSpec((tm, tk), lambda i, j, k: (i, k))
    b_spec = pl.BlockSpec((tn, tk), lambda i, j, k: (j, k)) if tb else pl.BlockSpec((tk, tn), lambda i, j, k: (k, j))
    in_specs, args = [a_spec, b_spec], [a, b]
    if add is not None:
        in_specs.append(pl.BlockSpec((tm, tn), lambda i, j, k: (i, j)))
        args.append(add)
    return _mm_call(args, in_specs, pl.BlockSpec((tm, tn), lambda i, j, k: (i, j)),
                    jax.ShapeDtypeStruct((M, N), out_dtype), (M // tm, N // tn, K // tk),
                    ta=ta, tb=tb, nk=K // tk, acc_shape=(tm, tn), has_add=add is not None, keep=None, name=name)


def _mm_tn_blocks(a, b, nblk, *, name):
    K, M = a.shape
    N = b.shape[1]
    bw = N // nblk
    tm, tk = _tile(M, 512), _tile(K, 2048)
    return _mm_call([a, b],
                    [pl.BlockSpec((tk, tm), lambda i, j, k: (k, i)), pl.BlockSpec((tk, bw), lambda i, j, k: (k, j))],
                    pl.BlockSpec((None, tm, bw), lambda i, j, k: (j, i, 0)),
                    jax.ShapeDtypeStruct((nblk, M, bw), F32), (M // tm, nblk, K // tk),
                    ta=True, tb=False, nk=K // tk, acc_shape=(tm, bw), has_add=False, keep=None, name=name)


def _half_of(b, half):
    hi = jnp.where(b >= half, 1, 0)
    return hi, b - half * hi


def _mm_up_fwd(u, g_up, l, *, name):
    S, K = u.shape
    nblk, _, _, bw = g_up.shape
    tm = _tile(S, 512)
    return _mm_call([u, g_up],
                    [pl.BlockSpec((tm, K), lambda i, j, k: (i, 0)),
                     pl.BlockSpec((None, None, K, bw), lambda i, j, k: (j, l, 0, 0))],
                    pl.BlockSpec((tm, bw), lambda i, j, k: (i, j)),
                    jax.ShapeDtypeStruct((S, nblk * bw), F32), (S // tm, nblk, 1),
                    ta=False, tb=False, nk=1, acc_shape=(tm, bw), has_add=False, keep=None, name=name)


def _mm_up_dx(dh, g_up, l, *, name):
    _, S, _ = dh.shape
    nblk, _, N, bw = g_up.shape
    half = nblk // 2
    tm = _tile(S, 512)
    return _mm_call([dh, g_up],
                    [pl.BlockSpec((None, tm, bw), lambda i, j, k: (_half_of(k, half)[0], i, _half_of(k, half)[1])),
                     pl.BlockSpec((None, None, N, bw), lambda i, j, k: (k, l, 0, 0))],
                    pl.BlockSpec((tm, N), lambda i, j, k: (i, 0)),
                    jax.ShapeDtypeStruct((S, N), F32), (S // tm, 1, nblk),
                    ta=False, tb=True, nk=nblk, acc_shape=(tm, N), has_add=False, keep=None, name=name)


def _mm_up_dw(u, dh, nblk, keep, *, name):
    S, M = u.shape
    half = nblk // 2
    bw = dh.shape[2] // half
    tm, tk = _tile(M, 512), _tile(S, 2048)
    return _mm_call([u, dh],
                    [pl.BlockSpec((tk, tm), lambda i, j, k: (k, i)),
                     pl.BlockSpec((None, tk, bw), lambda i, j, k: (_half_of(j, half)[0], k, _half_of(j, half)[1]))],
                    pl.BlockSpec((None, tm, keep), lambda i, j, k: (j, i, 0)),
                    jax.ShapeDtypeStruct((nblk, M, keep), F32), (M // tm, nblk, S // tk),
                    ta=True, tb=False, nk=S // tk, acc_shape=(tm, bw), has_add=False, keep=keep, name=name)


ROW_TILE = 512
HALO = 8
POOL_HALO = 16


def _row_tile(S):
    return min(ROW_TILE, S)


def _acc_out(ref, val, first):
    @pl.when(first)
    def _():
        ref[...] = val

    @pl.when(jnp.logical_not(first))
    def _():
        ref[...] += val


def _rms_fwd(x, w, *, name):
    S, D = x.shape
    ts = _row_tile(S)

    def body(x_ref, w_ref, u_ref):
        xv = x_ref[...]
        r = lax.rsqrt(jnp.mean(xv * xv, axis=-1, keepdims=True) + NORM_EPS)
        u_ref[...] = (xv * r * w_ref[...]).astype(BF16)

    return pl.pallas_call(
        body, name=name, grid=(S // ts,),
        in_specs=[pl.BlockSpec((ts, D), lambda i: (i, 0)), pl.BlockSpec((1, D), lambda i: (0, 0))],
        out_specs=pl.BlockSpec((ts, D), lambda i: (i, 0)),
        out_shape=jax.ShapeDtypeStruct((S, D), BF16),
        compiler_params=_cparams("parallel"),
    )(x, w)


def _rms_bwd(x, w, du, g, *, name):
    S, D = x.shape
    ts = _row_tile(S)

    def body(x_ref, w_ref, du_ref, g_ref, dx_ref, dxb_ref, dw_ref):
        xv = x_ref[...]
        r = lax.rsqrt(jnp.mean(xv * xv, axis=-1, keepdims=True) + NORM_EPS)
        n = xv * r
        duv = du_ref[...]
        dn = duv * w_ref[...]
        dx = g_ref[...] + r * (dn - n * jnp.mean(dn * n, axis=-1, keepdims=True))
        dx_ref[...] = dx
        dxb_ref[...] = dx.astype(BF16)
        _acc_out(dw_ref, jnp.sum(duv * n, axis=0, keepdims=True), pl.program_id(0) == 0)

    row = pl.BlockSpec((ts, D), lambda i: (i, 0))
    vec = pl.BlockSpec((1, D), lambda i: (0, 0))
    return pl.pallas_call(
        body, name=name, grid=(S // ts,),
        in_specs=[row, vec, row, row],
        out_specs=[row, row, vec],
        out_shape=[jax.ShapeDtypeStruct((S, D), F32), jax.ShapeDtypeStruct((S, D), BF16),
                   jax.ShapeDtypeStruct((1, D), F32)],
        compiler_params=_cparams("arbitrary"),
    )(x, w, du, g)


def _loss_head(x, w, target, *, name):
    S, D = x.shape
    ts = _row_tile(S)

    def body(x_ref, w_ref, t_ref, loss_ref, dx_ref, dxb_ref, dw_ref):
        xv = x_ref[...]
        wv = w_ref[...]
        r = lax.rsqrt(jnp.mean(xv * xv, axis=-1, keepdims=True) + NORM_EPS)
        n = xv * r
        err = n * wv - t_ref[...]
        part = 0.5 * jnp.sum(jnp.mean(err * err, axis=-1, keepdims=True), axis=0, keepdims=True)
        dy = err * (1.0 / D)
        dn = dy * wv
        dx = r * (dn - n * jnp.mean(dn * n, axis=-1, keepdims=True))
        dx_ref[...] = dx
        dxb_ref[...] = dx.astype(BF16)
        first = pl.program_id(0) == 0
        _acc_out(dw_ref, jnp.sum(dy * n, axis=0, keepdims=True), first)
        _acc_out(loss_ref, jnp.broadcast_to(part, loss_ref.shape), first)

    row = pl.BlockSpec((ts, D), lambda i: (i, 0))
    vec = pl.BlockSpec((1, D), lambda i: (0, 0))
    return pl.pallas_call(
        body, name=name, grid=(S // ts,),
        in_specs=[row, vec, row],
        out_specs=[pl.BlockSpec((SUBLANE, LANE), lambda i: (0, 0)), row, row, vec],
        out_shape=[jax.ShapeDtypeStruct((SUBLANE, LANE), F32), jax.ShapeDtypeStruct((S, D), F32),
                   jax.ShapeDtypeStruct((S, D), BF16), jax.ShapeDtypeStruct((1, D), F32)],
        compiler_params=_cparams("arbitrary"),
    )(x, w, target)


def _sigmoid(x):
    return 1.0 / (1.0 + jnp.exp(-x))


def _merge_fwd(gl, ya, yb, yc, *, name):
    S, D = ya.shape
    ts = _row_tile(S)

    def body(ga_ref, gb_ref, gc_ref, ya_ref, yb_ref, yc_ref, o_ref):
        m = (_sigmoid(ga_ref[...]) * ya_ref[...] + _sigmoid(gb_ref[...]) * yb_ref[...]
             + _sigmoid(gc_ref[...]) * yc_ref[...])
        o_ref[...] = m.astype(BF16)

    row = pl.BlockSpec((ts, D), lambda i: (i, 0))
    gspec = [pl.BlockSpec((ts, D), functools.partial(lambda b, i: (i, b), b)) for b in range(3)]
    return pl.pallas_call(
        body, name=name, grid=(S // ts,),
        in_specs=gspec + [row, row, row],
        out_specs=row,
        out_shape=jax.ShapeDtypeStruct((S, D), BF16),
        compiler_params=_cparams("parallel"),
    )(gl, gl, gl, ya, yb, yc)


def _merge_bwd(gl, ya, yb, yc, dm, *, name):
    S, D = ya.shape
    ts = _row_tile(S)

    def body(ga_ref, gb_ref, gc_ref, ya_ref, yb_ref, yc_ref, dm_ref, dgl_ref, da_ref, db_ref, dc_ref):
        dmv = dm_ref[...]
        for b, (g_ref, y_ref, dy_ref) in enumerate(((ga_ref, ya_ref, da_ref), (gb_ref, yb_ref, db_ref),
                                                     (gc_ref, yc_ref, dc_ref))):
            s = _sigmoid(g_ref[...])
            dy_ref[...] = (dmv * s).astype(BF16)
            dgl_ref[:, b * D:(b + 1) * D] = (dmv * y_ref[...] * s * (1.0 - s)).astype(BF16)

    row = pl.BlockSpec((ts, D), lambda i: (i, 0))
    gspec = [pl.BlockSpec((ts, D), functools.partial(lambda b, i: (i, b), b)) for b in range(3)]
    return pl.pallas_call(
        body, name=name, grid=(S // ts,),
        in_specs=gspec + [row, row, row, row],
        out_specs=[pl.BlockSpec((ts, 3 * D), lambda i: (i, 0)), row, row, row],
        out_shape=[jax.ShapeDtypeStruct((S, 3 * D), BF16)] + [jax.ShapeDtypeStruct((S, D), BF16)] * 3,
        compiler_params=_cparams("parallel"),
    )(gl, gl, gl, ya, yb, yc, dm)


POOL_WIDTH = 512


def _pool_cnt(t, w):
    return jnp.minimum(t + 1, w).astype(F32)


def _pool_fwd(rest, col, mix, scale, *, name):
    S = rest.shape[0]
    W, G, H = POOL_WIDTH, POOL_GROUP_DIM, POOL_HALO
    ts = _row_tile(S)

    def body(v_ref, h_ref, mix_ref, sc_ref, d_ref, y_ref, ext_ref):
        i = pl.program_id(0)
        cur = v_ref[...]
        ext_ref[0:H, :] = jnp.where(i > 0, h_ref[...], 0.0)
        ext_ref[H:, :] = cur
        t = i * ts + lax.broadcasted_iota(jnp.int32, (ts, 1), 0)
        for g, w in enumerate(POOL_WINDOWS):
            cols = slice(g * G, (g + 1) * G)
            acc = cur[:, cols]
            for k in range(1, w):
                acc = acc + ext_ref[pl.ds(H - k, ts), cols]
            db = (acc / _pool_cnt(t, w) - cur[:, cols]).astype(BF16)
            d_ref[:, cols] = db
            y = jnp.dot(db, mix_ref[g], preferred_element_type=F32) * sc_ref[:, cols]
            y_ref[:, cols] = y.astype(BF16)

    row = pl.BlockSpec((ts, W), lambda i: (i, 0))
    return pl.pallas_call(
        body, name=name, grid=(S // ts,),
        in_specs=[pl.BlockSpec((ts, W), lambda i: (i, col)),
                  pl.BlockSpec((H, W), lambda i: (jnp.maximum(i * (ts // H) - 1, 0), col)),
                  pl.BlockSpec((len(POOL_WINDOWS), G, G), lambda i: (0, 0, 0)),
                  pl.BlockSpec((1, W), lambda i: (0, 0))],
        out_specs=[row, row],
        out_shape=[jax.ShapeDtypeStruct((S, W), BF16)] * 2,
        scratch_shapes=[pltpu.VMEM((H + ts, W), F32)],
        compiler_params=_cparams("parallel"),
    )(rest, rest, mix, scale)


def _pool_bwd(dy, d, mix, scale, *, name):
    S = dy.shape[0]
    W, G, H = POOL_WIDTH, POOL_GROUP_DIM, POOL_HALO
    ts = _row_tile(S)
    n = S // ts
    NT = (((1,), (1,)), ((), ()))
    TN = (((0,), (0,)), ((), ()))

    def body(dy_ref, dyn_ref, d_ref, mix_ref, sc_ref, dv_ref, dmix_ref, dsc_ref, ext_ref):
        i = pl.program_id(0)

        @pl.when(i == 0)
        def _():
            dmix_ref[...] = jnp.zeros_like(dmix_ref)
            dsc_ref[...] = jnp.zeros_like(dsc_ref)

        t = i * ts + lax.broadcasted_iota(jnp.int32, (ts, 1), 0)
        tn = (i + 1) * ts + lax.broadcasted_iota(jnp.int32, (H, 1), 0)
        for g, w in enumerate(POOL_WINDOWS):
            cols = slice(g * G, (g + 1) * G)
            sc = sc_ref[:, cols]
            dyv = dy_ref[:, cols]
            db = d_ref[:, cols]
            mg = mix_ref[g]
            yp = jnp.dot(db, mg, preferred_element_type=F32)
            dsc_ref[:, cols] += jnp.sum(dyv * yp, axis=0, keepdims=True)
            dyp = (dyv * sc).astype(BF16)
            dd = lax.dot_general(dyp, mg, NT, preferred_element_type=F32)
            dmix_ref[g] += lax.dot_general(db, dyp, TN, preferred_element_type=F32)
            dyn = (jnp.where(i < n - 1, dyn_ref[:, cols], 0.0) * sc).astype(BF16)
            ddn = lax.dot_general(dyn, mg, NT, preferred_element_type=F32)
            e = dd / _pool_cnt(t, w)
            ext_ref[0:ts, cols] = e
            ext_ref[ts:, cols] = ddn / _pool_cnt(tn, w)
            acc = e
            for k in range(1, w):
                acc = acc + ext_ref[pl.ds(k, ts), cols]
            dv_ref[:, cols] = (acc - dd).astype(BF16)

    row = pl.BlockSpec((ts, W), lambda i: (i, 0))
    return pl.pallas_call(
        body, name=name, grid=(n,),
        in_specs=[row,
                  pl.BlockSpec((H, W), lambda i: (jnp.minimum((i + 1) * (ts // H), S // H - 1), 0)),
                  row,
                  pl.BlockSpec((len(POOL_WINDOWS), G, G), lambda i: (0, 0, 0)),
                  pl.BlockSpec((1, W), lambda i: (0, 0))],
        out_specs=[row, pl.BlockSpec((len(POOL_WINDOWS), G, G), lambda i: (0, 0, 0)),
                   pl.BlockSpec((1, W), lambda i: (0, 0))],
        out_shape=[jax.ShapeDtypeStruct((S, W), BF16),
                   jax.ShapeDtypeStruct((len(POOL_WINDOWS), G, G), F32),
                   jax.ShapeDtypeStruct((1, W), F32)],
        scratch_shapes=[pltpu.VMEM((ts + H, W), F32)],
        compiler_params=_cparams("arbitrary"),
    )(dy, dy, d, mix, scale)


def _conv_taps(ext_ref, w_ref, b_ref, rows, first):
    K = w_ref.shape[0]
    pre = b_ref[...] + w_ref[K - 1:K, :] * ext_ref[pl.ds(first, rows), :]
    for k in range(K - 1):
        pre = pre + w_ref[k:k + 1, :] * ext_ref[pl.ds(first - (K - 1) + k, rows), :]
    return pre


def _prev_halo_spec(ts, tc, colfn):
    return pl.BlockSpec((HALO, tc), lambda *g: (jnp.maximum(g[-1] * (ts // HALO) - 1, 0), colfn(*g)))


def _conv_silu_fwd(rest, col0, C, w, b, *, name):
    S = rest.shape[0]
    ts, tc = _row_tile(S), 512
    assert col0 % tc == 0 and C % tc == 0
    cb = col0 // tc

    def body(x_ref, h_ref, w_ref, b_ref, o_ref, ext_ref):
        i = pl.program_id(1)
        ext_ref[0:HALO, :] = jnp.where(i > 0, h_ref[...], 0.0)
        ext_ref[HALO:, :] = x_ref[...]
        pre = _conv_taps(ext_ref, w_ref, b_ref, ts, HALO)
        o_ref[...] = pre * _sigmoid(pre)

    K = w.shape[0]
    return pl.pallas_call(
        body, name=name, grid=(C // tc, S // ts),
        in_specs=[pl.BlockSpec((ts, tc), lambda j, i: (i, cb + j)),
                  _prev_halo_spec(ts, tc, lambda j, i: cb + j),
                  pl.BlockSpec((K, tc), lambda j, i: (0, j)),
                  pl.BlockSpec((1, tc), lambda j, i: (0, j))],
        out_specs=pl.BlockSpec((ts, tc), lambda j, i: (i, j)),
        out_shape=jax.ShapeDtypeStruct((S, C), F32),
        scratch_shapes=[pltpu.VMEM((HALO + ts, tc), F32)],
        compiler_params=_cparams("parallel", "parallel"),
    )(rest, rest, w, b)


def _dsilu(pre):
    s = _sigmoid(pre)
    return s, s * (1.0 + pre * (1.0 - s))


def _conv_silu_bwd(rest, col0, C, w, b, dy, *, name):
    S = rest.shape[0]
    ts, tc = _row_tile(S), 512
    cb = col0 // tc
    n = S // ts
    K = w.shape[0]
    R = ts + HALO

    def body(x_ref, hp_ref, hn_ref, w_ref, b_ref, dy_ref, dyn_ref, dx_ref, dw_ref, db_ref, ext_ref, g_ref):
        i = pl.program_id(1)
        last = i == n - 1
        ext_ref[0:HALO, :] = jnp.where(i > 0, hp_ref[...], 0.0)
        ext_ref[HALO:HALO + ts, :] = x_ref[...]
        ext_ref[HALO + ts:, :] = jnp.where(last, 0.0, hn_ref[...])
        g_ref[0:ts, :] = dy_ref[...]
        g_ref[ts:, :] = jnp.where(last, 0.0, dyn_ref[...])
        pre = _conv_taps(ext_ref, w_ref, b_ref, R, HALO)
        g_ref[...] = g_ref[...] * _dsilu(pre)[1]
        dx = w_ref[K - 1:K, :] * g_ref[pl.ds(0, ts), :]
        for k in range(K - 1):
            dx = dx + w_ref[k:k + 1, :] * g_ref[pl.ds(K - 1 - k, ts), :]
        dx_ref[...] = dx.astype(BF16)
        gc = g_ref[pl.ds(0, ts), :]
        first = i == 0
        _acc_out(db_ref, jnp.sum(gc, axis=0, keepdims=True), first)
        dws = [jnp.sum(gc * ext_ref[pl.ds(HALO - (K - 1) + k, ts), :], axis=0, keepdims=True) for k in range(K)]
        _acc_out(dw_ref, jnp.concatenate(dws, axis=0), first)

    nxt = lambda cf: pl.BlockSpec((HALO, tc), lambda j, i: (jnp.minimum((i + 1) * (ts // HALO), S // HALO - 1), cf(j)))
    return pl.pallas_call(
        body, name=name, grid=(C // tc, n),
        in_specs=[pl.BlockSpec((ts, tc), lambda j, i: (i, cb + j)),
                  _prev_halo_spec(ts, tc, lambda j, i: cb + j),
                  nxt(lambda j: cb + j),
                  pl.BlockSpec((K, tc), lambda j, i: (0, j)),
                  pl.BlockSpec((1, tc), lambda j, i: (0, j)),
                  pl.BlockSpec((ts, tc), lambda j, i: (i, j)),
                  nxt(lambda j: j)],
        out_specs=[pl.BlockSpec((ts, tc), lambda j, i: (i, j)),
                   pl.BlockSpec((K, tc), lambda j, i: (0, j)),
                   pl.BlockSpec((1, tc), lambda j, i: (0, j))],
        out_shape=[jax.ShapeDtypeStruct((S, C), BF16), jax.ShapeDtypeStruct((K, C), F32),
                   jax.ShapeDtypeStruct((1, C), F32)],
        scratch_shapes=[pltpu.VMEM((HALO + R, tc), F32), pltpu.VMEM((R, tc), F32)],
        compiler_params=_cparams("parallel", "arbitrary"),
    )(rest, rest, rest, w, b, dy, dy)


FFN_TC = 256


def _ffn_act_fwd(h0, w, b, *, name):
    S, F2 = h0.shape
    F = F2 // 2
    ts, tc = _row_tile(S), FFN_TC
    nb = F // tc
    K = w.shape[0]

    def body(xg_ref, hg_ref, xv_ref, hv_ref, wg_ref, wv_ref, bg_ref, bv_ref, o_ref, eg_ref, ev_ref):
        i = pl.program_id(1)
        for x_ref, h_ref, e_ref in ((xg_ref, hg_ref, eg_ref), (xv_ref, hv_ref, ev_ref)):
            e_ref[0:HALO, :] = jnp.where(i > 0, h_ref[...], 0.0)
            e_ref[HALO:, :] = x_ref[...]
        pg = _conv_taps(eg_ref, wg_ref, bg_ref, ts, HALO)
        pv = _conv_taps(ev_ref, wv_ref, bv_ref, ts, HALO)
        o_ref[...] = (pg * _sigmoid(pg) * pv).astype(BF16)

    def half(off):
        return [pl.BlockSpec((ts, tc), lambda j, i: (i, off + j)), _prev_halo_spec(ts, tc, lambda j, i: off + j)]

    wspec = lambda off: pl.BlockSpec((K, tc), lambda j, i: (0, off + j))
    bspec = lambda off: pl.BlockSpec((1, tc), lambda j, i: (0, off + j))
    return pl.pallas_call(
        body, name=name, grid=(nb, S // ts),
        in_specs=half(0) + half(nb) + [wspec(0), wspec(nb), bspec(0), bspec(nb)],
        out_specs=pl.BlockSpec((ts, tc), lambda j, i: (i, j)),
        out_shape=jax.ShapeDtypeStruct((S, F), BF16),
        scratch_shapes=[pltpu.VMEM((HALO + ts, tc), F32)] * 2,
        compiler_params=_cparams("parallel", "parallel"),
    )(h0, h0, h0, h0, w, w, b, b)


def _ffn_act_bwd(h0, w, b, da, *, name):
    S, F2 = h0.shape
    F = F2 // 2
    ts, tc = _row_tile(S), FFN_TC
    nb = F // tc
    n = S // ts
    K = w.shape[0]
    R = ts + HALO

    def body(xg_ref, pg_ref, ng_ref, xv_ref, pv_ref, nv_ref, wg_ref, wv_ref, bg_ref, bv_ref, da_ref, dan_ref,
             dx_ref, dw_ref, db_ref, eg_ref, ev_ref, gg_ref, gv_ref):
        i = pl.program_id(1)
        last = i == n - 1
        for x_ref, p_ref, n_ref, e_ref in ((xg_ref, pg_ref, ng_ref, eg_ref), (xv_ref, pv_ref, nv_ref, ev_ref)):
            e_ref[0:HALO, :] = jnp.where(i > 0, p_ref[...], 0.0)
            e_ref[HALO:HALO + ts, :] = x_ref[...]
            e_ref[HALO + ts:, :] = jnp.where(last, 0.0, n_ref[...])
        gg_ref[0:ts, :] = da_ref[...]
        gg_ref[ts:, :] = jnp.where(last, 0.0, dan_ref[...])
        dav = gg_ref[...]
        pg = _conv_taps(eg_ref, wg_ref, bg_ref, R, HALO)
        pv = _conv_taps(ev_ref, wv_ref, bv_ref, R, HALO)
        s, ds = _dsilu(pg)
        gg_ref[...] = dav * pv * ds
        gv_ref[...] = dav * pg * s
        first = i == 0
        for h, (g_ref, e_ref, w_ref) in enumerate(((gg_ref, eg_ref, wg_ref), (gv_ref, ev_ref, wv_ref))):
            dx = w_ref[K - 1:K, :] * g_ref[pl.ds(0, ts), :]
            for k in range(K - 1):
                dx = dx + w_ref[k:k + 1, :] * g_ref[pl.ds(K - 1 - k, ts), :]
            dx_ref[h] = dx.astype(BF16)
            gc = g_ref[pl.ds(0, ts), :]
            dws = [jnp.sum(gc * e_ref[pl.ds(HALO - (K - 1) + k, ts), :], axis=0, keepdims=True) for k in range(K)]
            _acc_out(db_ref.at[h], jnp.sum(gc, axis=0, keepdims=True), first)
            _acc_out(dw_ref.at[h], jnp.concatenate(dws, axis=0), first)

    nxt = lambda off: pl.BlockSpec((HALO, tc), lambda j, i: (jnp.minimum((i + 1) * (ts // HALO), S // HALO - 1), off + j))

    def half(off):
        return [pl.BlockSpec((ts, tc), lambda j, i: (i, off + j)), _prev_halo_spec(ts, tc, lambda j, i: off + j), nxt(off)]

    wspec = lambda off: pl.BlockSpec((K, tc), lambda j, i: (0, off + j))
    bspec = lambda off: pl.BlockSpec((1, tc), lambda j, i: (0, off + j))
    return pl.pallas_call(
        body, name=name, grid=(nb, n),
        in_specs=half(0) + half(nb) + [wspec(0), wspec(nb), bspec(0), bspec(nb),
                                       pl.BlockSpec((ts, tc), lambda j, i: (i, j)), nxt(0)],
        out_specs=[pl.BlockSpec((2, ts, tc), lambda j, i: (0, i, j)),
                   pl.BlockSpec((2, K, tc), lambda j, i: (0, 0, j)),
                   pl.BlockSpec((2, 1, tc), lambda j, i: (0, 0, j))],
        out_shape=[jax.ShapeDtypeStruct((2, S, F), BF16), jax.ShapeDtypeStruct((2, K, F), F32),
                   jax.ShapeDtypeStruct((2, 1, F), F32)],
        scratch_shapes=[pltpu.VMEM((HALO + R, tc), F32)] * 2 + [pltpu.VMEM((R, tc), F32)] * 2,
        compiler_params=_cparams("parallel", "arbitrary"),
    )(h0, h0, h0, h0, h0, h0, w, w, b, b, da, da)


CUM_TILE = 256
ATT_TILE = 512
NEG = -1e30
N_PAIR = ATTN_HEADS // 2
NT_DIMS = (((1,), (1,)), ((), ()))


def _split3(x):
    a = x.astype(BF16)
    r = x - a.astype(F32)
    b = r.astype(BF16)
    c = (r - b.astype(F32)).astype(BF16)
    return a, b, c


def _tri_dot(tri, x):
    return sum(jnp.dot(tri, p, preferred_element_type=F32) for p in _split3(x))


def _log_sigmoid(x):
    return jnp.minimum(x, 0.0) - jnp.log(1.0 + jnp.exp(-jnp.abs(x)))


def _col(v, idx, lane):
    return jnp.sum(jnp.where(lane == idx, v, 0.0), axis=1, keepdims=True)


def _fox_pre(rest, col, fb, *, name):
    S = rest.shape[0]
    ts = min(CUM_TILE, S)

    def body(f_ref, fb_ref, ccol_ref, crow_ref, carry_ref):
        i = pl.program_id(0)

        @pl.when(i == 0)
        def _():
            carry_ref[...] = jnp.zeros_like(carry_ref)

        lane = lax.broadcasted_iota(jnp.int32, (1, LANE), 1)
        logf = jnp.where(lane < ATTN_HEADS, _log_sigmoid(f_ref[...] + fb_ref[...]), 0.0)
        r = lax.broadcasted_iota(jnp.int32, (ts, ts), 0)
        c = lax.broadcasted_iota(jnp.int32, (ts, ts), 1)
        tri = jnp.where(c <= r, 1.0, 0.0).astype(BF16)
        cs = _tri_dot(tri, logf) + carry_ref[...]
        ccol_ref[...] = cs
        carry_ref[...] = cs[ts - 1:ts, :]
        crow_ref[...] = cs.T[0:SUBLANE, :]

    return pl.pallas_call(
        body, name=name, grid=(S // ts,),
        in_specs=[pl.BlockSpec((ts, LANE), lambda i: (i, col)), pl.BlockSpec((1, LANE), lambda i: (0, 0))],
        out_specs=[pl.BlockSpec((ts, LANE), lambda i: (i, 0)), pl.BlockSpec((SUBLANE, ts), lambda i: (0, i))],
        out_shape=[jax.ShapeDtypeStruct((S, LANE), F32), jax.ShapeDtypeStruct((SUBLANE, S), F32)],
        scratch_shapes=[pltpu.VMEM((1, LANE), F32)],
        compiler_params=_cparams("arbitrary"),
    )(rest, fb)


def _fox_post(dcq, dck, rest, col, fb, ddt, *, name):
    S = rest.shape[0]
    ts = min(CUM_TILE, S)
    n = S // ts

    def body(dcq_ref, dck_ref, f_ref, fb_ref, ddt_ref, o_ref, db_ref, carry_ref):
        i = pl.program_id(0)

        @pl.when(i == 0)
        def _():
            carry_ref[...] = jnp.zeros_like(carry_ref)

        lane = lax.broadcasted_iota(jnp.int32, (1, LANE), 1)
        dc = jnp.zeros((ts, LANE), F32)
        for h in range(ATTN_HEADS):
            d = dcq_ref[h // 2] - dck_ref[h // 2]
            dc = jnp.where(lane == h, _col(d, h % 2, lane), dc)
        r = lax.broadcasted_iota(jnp.int32, (ts, ts), 0)
        c = lax.broadcasted_iota(jnp.int32, (ts, ts), 1)
        tri = jnp.where(c >= r, 1.0, 0.0).astype(BF16)
        rc = _tri_dot(tri, dc) + carry_ref[...]
        carry_ref[...] = rc[0:1, :]
        df = rc * _sigmoid(-(f_ref[...] + fb_ref[...]))
        out = jnp.where(lane < ATTN_HEADS, df, ddt_ref[...])
        o_ref[...] = out.astype(BF16)
        _acc_out(db_ref, jnp.sum(out, axis=0, keepdims=True), i == 0)

    rev = lambda i: n - 1 - i
    return pl.pallas_call(
        body, name=name, grid=(n,),
        in_specs=[pl.BlockSpec((N_PAIR, ts, LANE), lambda i: (0, rev(i), 0)),
                  pl.BlockSpec((N_PAIR, ts, LANE), lambda i: (0, rev(i), 0)),
                  pl.BlockSpec((ts, LANE), lambda i: (rev(i), col)),
                  pl.BlockSpec((1, LANE), lambda i: (0, 0)),
                  pl.BlockSpec((ts, LANE), lambda i: (rev(i), 0))],
        out_specs=[pl.BlockSpec((ts, LANE), lambda i: (rev(i), 0)), pl.BlockSpec((1, LANE), lambda i: (0, 0))],
        out_shape=[jax.ShapeDtypeStruct((S, LANE), BF16), jax.ShapeDtypeStruct((1, LANE), F32)],
        scratch_shapes=[pltpu.VMEM((1, LANE), F32)],
        compiler_params=_cparams("arbitrary"),
    )(dcq, dck, rest, fb, ddt)


def _head_masks():
    lane = lax.broadcasted_iota(jnp.int32, (1, LANE), 1)
    return lane, (lane < ATTN_HEAD_DIM, lane >= ATTN_HEAD_DIM)


def _causal_mask(t):
    r = lax.broadcasted_iota(jnp.int32, (t, t), 0)
    c = lax.broadcasted_iota(jnp.int32, (t, t), 1)
    return r, c


def _attn_fwd(qkv, ccol, crow, *, name):
    S = qkv.shape[0]
    t = min(ATT_TILE, S)
    scale = ATTN_HEAD_DIM ** -0.5

    def body(q_ref, k_ref, v_ref, cc_ref, cr_ref, o_ref, lse_ref, m_ref, l_ref, acc_ref):
        j, i = pl.program_id(0), pl.program_id(1)
        lane, masks = _head_masks()
        q2 = q_ref[...] * scale
        qh = [jnp.where(mk, q2, 0.0).astype(BF16) for mk in masks]
        ccv = cc_ref[...]
        cq = [_col(ccv, 2 * j + hh, lane) for hh in range(2)]
        m_ref[...] = jnp.full(m_ref.shape, NEG, F32)
        l_ref[...] = jnp.zeros_like(l_ref)
        acc_ref[...] = jnp.zeros_like(acc_ref)

        def step(kb, masked):
            off = pl.multiple_of(kb * t, t)
            kblk = k_ref[pl.ds(off, t), :]
            vblk = v_ref[pl.ds(off, t), :]
            for hh in range(2):
                s = lax.dot_general(qh[hh], kblk, NT_DIMS, preferred_element_type=F32)
                s = s + (cq[hh] - cr_ref[hh, :, pl.ds(off, t)])
                if masked:
                    r, c = _causal_mask(t)
                    s = jnp.where(c <= r, s, NEG)
                m_old = m_ref[hh]
                m_new = jnp.maximum(m_old, jnp.max(s, axis=1, keepdims=True))
                p = jnp.exp(s - m_new)
                alpha = jnp.exp(m_old - m_new)
                l_ref[hh] = alpha * l_ref[hh] + jnp.sum(p, axis=1, keepdims=True)
                acc_ref[hh] = alpha * acc_ref[hh] + jnp.dot(p.astype(BF16), vblk, preferred_element_type=F32)
                m_ref[hh] = m_new

        def loop_body(kb, carry):
            step(kb, False)
            return carry

        lax.fori_loop(0, i, loop_body, 0)
        step(i, True)
        o = [acc_ref[hh] / l_ref[hh] for hh in range(2)]
        o_ref[...] = jnp.where(masks[0], o[0], o[1]).astype(BF16)
        lse = [m_ref[hh] + jnp.log(l_ref[hh]) for hh in range(2)]
        lse_ref[0] = jnp.where(lane == 0, lse[0], jnp.where(lane == 1, lse[1], 0.0))

    return pl.pallas_call(
        body, name=name, grid=(N_PAIR, S // t),
        in_specs=[pl.BlockSpec((t, LANE), lambda j, i: (i, j)),
                  pl.BlockSpec((S, LANE), lambda j, i: (0, N_PAIR + j)),
                  pl.BlockSpec((S, LANE), lambda j, i: (0, 2 * N_PAIR + j)),
                  pl.BlockSpec((t, LANE), lambda j, i: (i, 0)),
                  pl.BlockSpec((2, 1, S), lambda j, i: (j, 0, 0))],
        out_specs=[pl.BlockSpec((t, LANE), lambda j, i: (i, j)),
                   pl.BlockSpec((1, t, LANE), lambda j, i: (j, i, 0))],
        out_shape=[jax.ShapeDtypeStruct((S, N_PAIR * LANE), BF16), jax.ShapeDtypeStruct((N_PAIR, S, LANE), F32)],
        scratch_shapes=[pltpu.VMEM((2, t, 1), F32), pltpu.VMEM((2, t, 1), F32), pltpu.VMEM((2, t, LANE), F32)],
        compiler_params=_cparams("parallel", "parallel"),
    )(qkv, qkv, qkv, ccol, crow)


def _attn_bwd_dq(qkv, o, do, lse, ccol, crow, *, name):
    S = qkv.shape[0]
    t = min(ATT_TILE, S)
    scale = ATTN_HEAD_DIM ** -0.5

    def body(q_ref, k_ref, v_ref, o_ref, do_ref, lse_ref, cc_ref, cr_ref,
             dq_ref, dcq_ref, lrow_ref, drow_ref, acc_ref, rs_ref, tr_ref):
        j, i = pl.program_id(0), pl.program_id(1)
        lane, masks = _head_masks()
        q2 = q_ref[...] * scale
        do2 = do_ref[...]
        qh = [jnp.where(mk, q2, 0.0).astype(BF16) for mk in masks]
        doh = [jnp.where(mk, do2, 0.0).astype(BF16) for mk in masks]
        of = o_ref[...].astype(F32)
        delta = [jnp.sum(doh[hh].astype(F32) * of, axis=1, keepdims=True) for hh in range(2)]
        ccv = cc_ref[...]
        cq = [_col(ccv, 2 * j + hh, lane) for hh in range(2)]
        lsev = lse_ref[0]
        lse = [_col(lsev, hh, lane) for hh in range(2)]
        acc_ref[...] = jnp.zeros_like(acc_ref)
        rs_ref[...] = jnp.zeros_like(rs_ref)

        def step(kb, masked):
            off = pl.multiple_of(kb * t, t)
            kblk = k_ref[pl.ds(off, t), :]
            vblk = v_ref[pl.ds(off, t), :]
            for hh in range(2):
                s = lax.dot_general(qh[hh], kblk, NT_DIMS, preferred_element_type=F32)
                s = s + (cq[hh] - cr_ref[hh, :, pl.ds(off, t)])
                if masked:
                    r, c = _causal_mask(t)
                    s = jnp.where(c <= r, s, NEG)
                p = jnp.exp(s - lse[hh])
                dp = lax.dot_general(doh[hh], vblk, NT_DIMS, preferred_element_type=F32)
                ds = p * (dp - delta[hh])
                rs_ref[hh] += jnp.sum(ds, axis=1, keepdims=True)
                acc_ref[hh] += jnp.dot(ds.astype(BF16), kblk, preferred_element_type=F32)

        def loop_body(kb, carry):
            step(kb, False)
            return carry

        lax.fori_loop(0, i, loop_body, 0)
        step(i, True)
        dq_ref[...] = (jnp.where(masks[0], acc_ref[0], acc_ref[1]) * scale).astype(BF16)
        dcq_ref[0] = jnp.where(lane == 0, rs_ref[0], jnp.where(lane == 1, rs_ref[1], 0.0))
        stats = jnp.where(lane == 0, lse[0], jnp.where(lane == 1, lse[1],
                          jnp.where(lane == 2, delta[0], jnp.where(lane == 3, delta[1], 0.0))))
        tr_ref[...] = stats.T[0:SUBLANE, :]
        for hh in range(2):
            lrow_ref[hh] = tr_ref[hh:hh + 1, :]
            drow_ref[hh] = tr_ref[2 + hh:3 + hh, :]

    qspec = pl.BlockSpec((t, LANE), lambda j, i: (i, j))
    stat = pl.BlockSpec((1, t, LANE), lambda j, i: (j, i, 0))
    rowstat = pl.BlockSpec((2, 1, t), lambda j, i: (j, 0, i))
    return pl.pallas_call(
        body, name=name, grid=(N_PAIR, S // t),
        in_specs=[qspec,
                  pl.BlockSpec((S, LANE), lambda j, i: (0, N_PAIR + j)),
                  pl.BlockSpec((S, LANE), lambda j, i: (0, 2 * N_PAIR + j)),
                  qspec, qspec, stat,
                  pl.BlockSpec((t, LANE), lambda j, i: (i, 0)),
                  pl.BlockSpec((2, 1, S), lambda j, i: (j, 0, 0))],
        out_specs=[qspec, stat, rowstat, rowstat],
        out_shape=[jax.ShapeDtypeStruct((S, N_PAIR * LANE), BF16), jax.ShapeDtypeStruct((N_PAIR, S, LANE), F32),
                   jax.ShapeDtypeStruct((ATTN_HEADS, 1, S), F32), jax.ShapeDtypeStruct((ATTN_HEADS, 1, S), F32)],
        scratch_shapes=[pltpu.VMEM((2, t, LANE), F32), pltpu.VMEM((2, t, 1), F32), pltpu.VMEM((SUBLANE, t), F32)],
        compiler_params=_cparams("parallel", "parallel"),
    )(qkv, qkv, qkv, o, do, lse, ccol, crow)


def _attn_bwd_dkv(qkv, do, lrow, drow, ccol, crow, *, name):
    S = qkv.shape[0]
    t = min(ATT_TILE, S)
    n = S // t
    scale = ATTN_HEAD_DIM ** -0.5

    def body(q_ref, k_ref, v_ref, do_ref, lrow_ref, drow_ref, cc_ref, cr_ref,
             dk_ref, dv_ref, dck_ref, dka_ref, dva_ref, cs_ref):
        j, kb = pl.program_id(0), pl.program_id(1)
        lane, masks = _head_masks()
        k2 = k_ref[...]
        v2 = v_ref[...]
        kh = [jnp.where(mk, k2, 0.0).astype(BF16) for mk in masks]
        vh = [jnp.where(mk, v2, 0.0).astype(BF16) for mk in masks]
        ccv = cc_ref[...]
        ck = [_col(ccv, 2 * j + hh, lane) for hh in range(2)]
        dka_ref[...] = jnp.zeros_like(dka_ref)
        dva_ref[...] = jnp.zeros_like(dva_ref)
        cs_ref[...] = jnp.zeros_like(cs_ref)

        def step(qb, masked):
            off = pl.multiple_of(qb * t, t)
            qblk = (q_ref[pl.ds(off, t), :] * scale).astype(BF16)
            doblk = do_ref[pl.ds(off, t), :]
            for hh in range(2):
                st = lax.dot_general(kh[hh], qblk, NT_DIMS, preferred_element_type=F32)
                st = st + (cr_ref[hh, :, pl.ds(off, t)] - ck[hh])
                if masked:
                    r, c = _causal_mask(t)
                    st = jnp.where(r <= c, st, NEG)
                pt = jnp.exp(st - lrow_ref[hh, :, pl.ds(off, t)])
                dpt = lax.dot_general(vh[hh], doblk, NT_DIMS, preferred_element_type=F32)
                dst = pt * (dpt - drow_ref[hh, :, pl.ds(off, t)])
                cs_ref[hh] += jnp.sum(dst, axis=1, keepdims=True)
                dva_ref[hh] += jnp.dot(pt.astype(BF16), doblk, preferred_element_type=F32)
                dka_ref[hh] += jnp.dot(dst.astype(BF16), qblk, preferred_element_type=F32)

        step(kb, True)

        def loop_body(qb, carry):
            step(qb, False)
            return carry

        lax.fori_loop(kb + 1, n, loop_body, 0)
        dk_ref[...] = jnp.where(masks[0], dka_ref[0], dka_ref[1]).astype(BF16)
        dv_ref[...] = jnp.where(masks[0], dva_ref[0], dva_ref[1]).astype(BF16)
        dck_ref[0] = jnp.where(lane == 0, cs_ref[0], jnp.where(lane == 1, cs_ref[1], 0.0))

    full = lambda cb: pl.BlockSpec((S, LANE), lambda j, kb: (0, cb(j)))
    kspec = lambda base: pl.BlockSpec((t, LANE), lambda j, kb: (kb, base + j))
    rows = pl.BlockSpec((2, 1, S), lambda j, kb: (j, 0, 0))
    oblk = pl.BlockSpec((t, LANE), lambda j, kb: (kb, j))
    return pl.pallas_call(
        body, name=name, grid=(N_PAIR, n),
        in_specs=[full(lambda j: j), kspec(N_PAIR), kspec(2 * N_PAIR), full(lambda j: j), rows, rows,
                  pl.BlockSpec((t, LANE), lambda j, kb: (kb, 0)), rows],
        out_specs=[oblk, oblk, pl.BlockSpec((1, t, LANE), lambda j, kb: (j, kb, 0))],
        out_shape=[jax.ShapeDtypeStruct((S, N_PAIR * LANE), BF16)] * 2 + [jax.ShapeDtypeStruct((N_PAIR, S, LANE), F32)],
        scratch_shapes=[pltpu.VMEM((2, t, LANE), F32), pltpu.VMEM((2, t, LANE), F32), pltpu.VMEM((2, t, 1), F32)],
        compiler_params=_cparams("parallel", "parallel"),
    )(qkv, qkv, qkv, do, lrow, drow, ccol, crow)


DT_LANE0 = ATTN_HEADS
SSD_PAIRS = SSD_HEADS // 2
SSD_X = SSD_HEADS * SSD_HEAD_DIM
SSD_B0 = SSD_X
SSD_C0 = SSD_X + SSD_GROUPS * SSD_STATE
SSD_CH = SSD_X + 2 * SSD_GROUPS * SSD_STATE
TN_DIMS = (((0,), (0,)), ((), ()))


def _softplus(x):
    return jnp.maximum(x, 0.0) + jnp.log(1.0 + jnp.exp(-jnp.abs(x)))


def _ssd_prep(fdt, dtb, alog):
    L = fdt.shape[0]
    lane = lax.broadcasted_iota(jnp.int32, (1, LANE), 1)
    hl = jnp.logical_and(lane >= DT_LANE0, lane < DT_LANE0 + SSD_HEADS)
    dtv = jnp.where(hl, _softplus(fdt + dtb), 0.0)
    A = jnp.where(hl, -jnp.exp(alog), 0.0)
    r = lax.broadcasted_iota(jnp.int32, (L, L), 0)
    c = lax.broadcasted_iota(jnp.int32, (L, L), 1)
    cs = _tri_dot(jnp.where(c <= r, 1.0, 0.0).astype(BF16), dtv * A)
    return lane, hl, dtv, A, cs, r, c


def _halves(lane, v0, v1):
    return jnp.where(lane < SSD_HEAD_DIM, v0, v1)


def _ssd_fwd(xc, rest, col, dtb, alog, dskip, *, name):
    S = xc.shape[0]
    L = SSD_CHUNK
    nc = S // L

    def body(xc_ref, f_ref, dtb_ref, al_ref, dk_ref, y_ref, hs_ref, h_ref, cst_ref):
        i = pl.program_id(0)

        @pl.when(i == 0)
        def _():
            h_ref[...] = jnp.zeros_like(h_ref)

        lane, hl, dtv, A, cs, r, c = _ssd_prep(f_ref[...], dtb_ref[...], al_ref[...])
        cst_ref[...] = cs.T
        cs_last = cs[L - 1:L, :]
        ecs = jnp.exp(cs)
        dec = jnp.exp(cs_last - cs)
        cd = jnp.exp(cs_last)
        dkv = dk_ref[...]
        prow = lax.broadcasted_iota(jnp.int32, (LANE, 1), 0)
        for g in range(SSD_GROUPS):
            Bg = xc_ref[:, SSD_B0 + g * SSD_STATE:SSD_B0 + (g + 1) * SSD_STATE].astype(BF16)
            Cg = xc_ref[:, SSD_C0 + g * SSD_STATE:SSD_C0 + (g + 1) * SSD_STATE].astype(BF16)
            CB = lax.dot_general(Cg, Bg, NT_DIMS, preferred_element_type=F32)
            for pp in range(SSD_PAIRS // SSD_GROUPS):
                pi = g * (SSD_PAIRS // SSD_GROUPS) + pp
                hl0 = DT_LANE0 + 2 * pi
                x2 = xc_ref[:, pi * LANE:(pi + 1) * LANE]
                xd = x2 * _halves(lane, _col(dtv, hl0, lane), _col(dtv, hl0 + 1, lane))
                xdb = xd.astype(BF16)
                yd = jnp.zeros((L, LANE), F32)
                for hh in range(2):
                    seg = _col(cs, hl0 + hh, lane) - cst_ref[hl0 + hh:hl0 + hh + 1, :]
                    M = CB * jnp.exp(jnp.where(c <= r, seg, NEG))
                    yh = jnp.dot(M.astype(BF16), xdb, preferred_element_type=F32)
                    yd = jnp.where((lane >= SSD_HEAD_DIM) if hh else (lane < SSD_HEAD_DIM), yh, yd)
                hp = h_ref[pi]
                hs_ref[0, pi] = hp
                yo = lax.dot_general(Cg, hp.astype(BF16), NT_DIMS, preferred_element_type=F32)
                yo = yo * _halves(lane, _col(ecs, hl0, lane), _col(ecs, hl0 + 1, lane))
                dsk = _halves(lane, _col(dkv, hl0, lane), _col(dkv, hl0 + 1, lane))
                y_ref[:, pi * LANE:(pi + 1) * LANE] = yd + yo + dsk * x2
                xw = (xd * _halves(lane, _col(dec, hl0, lane), _col(dec, hl0 + 1, lane))).astype(BF16)
                st = lax.dot_general(xw, Bg, TN_DIMS, preferred_element_type=F32)
                cdp = jnp.where(prow < SSD_HEAD_DIM, _col(cd, hl0, lane), _col(cd, hl0 + 1, lane))
                h_ref[pi] = cdp * hp + st

    vec = pl.BlockSpec((1, LANE), lambda i: (0, 0))
    return pl.pallas_call(
        body, name=name, grid=(nc,),
        in_specs=[pl.BlockSpec((L, SSD_CH), lambda i: (i, 0)), pl.BlockSpec((L, LANE), lambda i: (i, col)),
                  vec, vec, vec],
        out_specs=[pl.BlockSpec((L, SSD_X), lambda i: (i, 0)),
                   pl.BlockSpec((1, SSD_PAIRS, LANE, SSD_STATE), lambda i: (i, 0, 0, 0))],
        out_shape=[jax.ShapeDtypeStruct((S, SSD_X), F32),
                   jax.ShapeDtypeStruct((nc, SSD_PAIRS, LANE, SSD_STATE), F32)],
        scratch_shapes=[pltpu.VMEM((SSD_PAIRS, LANE, SSD_STATE), F32), pltpu.VMEM((LANE, L), F32)],
        compiler_params=_cparams("arbitrary"),
    )(xc, rest, dtb, alog, dskip)


def _pair_sums(lane, v):
    lo = jnp.sum(jnp.where(lane < SSD_HEAD_DIM, v, 0.0), axis=1, keepdims=True)
    return lo, jnp.sum(v, axis=1, keepdims=True) - lo


def _ssd_bwd(xc, rest, col, dtb, alog, dskip, hs, dy, *, name):
    S = xc.shape[0]
    L = SSD_CHUNK
    nc = S // L
    PG = SSD_PAIRS // SSD_GROUPS

    def body(xc_ref, f_ref, dtb_ref, al_ref, dk_ref, hs_ref, dy_ref, dxc_ref, ddt_ref, dp_ref, dh_ref, cst_ref):
        i = pl.program_id(0)

        @pl.when(i == 0)
        def _():
            dh_ref[...] = jnp.zeros_like(dh_ref)
            dp_ref[...] = jnp.zeros_like(dp_ref)

        fv = f_ref[...] + dtb_ref[...]
        lane, hl, dtv, A, cs, r, c = _ssd_prep(f_ref[...], dtb_ref[...], al_ref[...])
        cst_ref[...] = cs.T
        cs_last = cs[L - 1:L, :]
        ecs = jnp.exp(cs)
        dec = jnp.exp(cs_last - cs)
        cd = jnp.exp(cs_last)
        dkv = dk_ref[...]
        prow = lax.broadcasted_iota(jnp.int32, (LANE, 1), 0)
        lrow = lax.broadcasted_iota(jnp.int32, (L, 1), 0)
        is_last = lrow == L - 1
        causal = c <= r
        dcs = jnp.zeros((L, LANE), F32)
        ddt = jnp.zeros((L, LANE), F32)
        dD = jnp.zeros((1, LANE), F32)
        for g in range(SSD_GROUPS):
            Bg = xc_ref[:, SSD_B0 + g * SSD_STATE:SSD_B0 + (g + 1) * SSD_STATE].astype(BF16)
            Cg = xc_ref[:, SSD_C0 + g * SSD_STATE:SSD_C0 + (g + 1) * SSD_STATE].astype(BF16)
            CB = lax.dot_general(Cg, Bg, NT_DIMS, preferred_element_type=F32)
            dCB = jnp.zeros((L, L), F32)
            dB = jnp.zeros((L, SSD_STATE), F32)
            dC = jnp.zeros((L, SSD_STATE), F32)
            for pp in range(PG):
                pi = g * PG + pp
                hl0 = DT_LANE0 + 2 * pi
                x2 = xc_ref[:, pi * LANE:(pi + 1) * LANE]
                dy2 = dy_ref[:, pi * LANE:(pi + 1) * LANE]
                dtp = _halves(lane, _col(dtv, hl0, lane), _col(dtv, hl0 + 1, lane))
                xd = x2 * dtp
                xdb = xd.astype(BF16)
                dsk = _halves(lane, _col(dkv, hl0, lane), _col(dkv, hl0 + 1, lane))
                dx2 = dsk * dy2
                sD = _pair_sums(lane, dy2 * x2)
                hp = hs_ref[0, pi]
                hpb = hp.astype(BF16)
                ecsp = _halves(lane, _col(ecs, hl0, lane), _col(ecs, hl0 + 1, lane))
                yo = lax.dot_general(Cg, hpb, NT_DIMS, preferred_element_type=F32) * ecsp
                dW = (ecsp * dy2).astype(BF16)
                dC = dC + jnp.dot(dW, hpb, preferred_element_type=F32)
                dhp = lax.dot_general(dW, Cg, TN_DIMS, preferred_element_type=F32)
                sYo = _pair_sums(lane, dy2 * yo)
                dhn = dh_ref[pi]
                cdh = (_col(cd, hl0, lane), _col(cd, hl0 + 1, lane))
                dhp = dhp + jnp.where(prow < SSD_HEAD_DIM, cdh[0], cdh[1]) * dhn
                rs = jnp.sum(dhn * hp, axis=1, keepdims=True)
                lo = jnp.sum(jnp.where(prow < SSD_HEAD_DIM, rs, 0.0), axis=0, keepdims=True)
                dcd = (lo, jnp.sum(rs, axis=0, keepdims=True) - lo)
                dhnb = dhn.astype(BF16)
                decp = _halves(lane, _col(dec, hl0, lane), _col(dec, hl0 + 1, lane))
                G = lax.dot_general(Bg, dhnb, NT_DIMS, preferred_element_type=F32)
                dxd = decp * G
                sdec = _pair_sums(lane, xd * G)
                dB = dB + jnp.dot((xd * decp).astype(BF16), dhnb, preferred_element_type=F32)
                dh_ref[pi] = dhp
                for hh in range(2):
                    hmask = (lane >= SSD_HEAD_DIM) if hh else (lane < SSD_HEAD_DIM)
                    seg = _col(cs, hl0 + hh, lane) - cst_ref[hl0 + hh:hl0 + hh + 1, :]
                    Lm = jnp.exp(jnp.where(causal, seg, NEG))
                    M = CB * Lm
                    dyh = jnp.where(hmask, dy2, 0.0).astype(BF16)
                    dM = lax.dot_general(dyh, xdb, NT_DIMS, preferred_element_type=F32)
                    dxd = dxd + lax.dot_general(M.astype(BF16), dyh, TN_DIMS, preferred_element_type=F32)
                    Q = dM * M
                    dCB = dCB + dM * Lm
                    dech = _col(dec, hl0 + hh, lane)
                    dd = sdec[hh] * dech
                    end = dcd[hh] * cdh[hh] + jnp.sum(dd, axis=0, keepdims=True)
                    dcs_h = (sYo[hh] - dd + jnp.sum(Q, axis=1, keepdims=True)
                             - jnp.sum(Q.T, axis=1, keepdims=True) + jnp.where(is_last, end, 0.0))
                    dcs = jnp.where(lane == hl0 + hh, dcs_h, dcs)
                    dD = jnp.where(lane == hl0 + hh, jnp.sum(sD[hh], axis=0, keepdims=True), dD)
                sdt = _pair_sums(lane, dxd * x2)
                ddt = jnp.where(lane == hl0, sdt[0], jnp.where(lane == hl0 + 1, sdt[1], ddt))
                dxc_ref[:, pi * LANE:(pi + 1) * LANE] = dx2 + dxd * dtp
            dCBb = dCB.astype(BF16)
            dC = dC + jnp.dot(dCBb, Bg, preferred_element_type=F32)
            dB = dB + lax.dot_general(dCBb, Cg, TN_DIMS, preferred_element_type=F32)
            dxc_ref[:, SSD_B0 + g * SSD_STATE:SSD_B0 + (g + 1) * SSD_STATE] = dB
            dxc_ref[:, SSD_C0 + g * SSD_STATE:SSD_C0 + (g + 1) * SSD_STATE] = dC
        da = _tri_dot(jnp.where(c >= r, 1.0, 0.0).astype(BF16), dcs)
        ddtv = ddt + da * A
        ddt_ref[...] = jnp.where(hl, ddtv * _sigmoid(fv), 0.0)
        dal = jnp.sum(da * dtv, axis=0, keepdims=True) * A
        dp_ref[0:1, :] += dal
        dp_ref[1:2, :] += dD

    rev = lambda i: nc - 1 - i
    vec = pl.BlockSpec((1, LANE), lambda i: (0, 0))
    return pl.pallas_call(
        body, name=name, grid=(nc,),
        in_specs=[pl.BlockSpec((L, SSD_CH), lambda i: (rev(i), 0)), pl.BlockSpec((L, LANE), lambda i: (rev(i), col)),
                  vec, vec, vec,
                  pl.BlockSpec((1, SSD_PAIRS, LANE, SSD_STATE), lambda i: (rev(i), 0, 0, 0)),
                  pl.BlockSpec((L, SSD_X), lambda i: (rev(i), 0))],
        out_specs=[pl.BlockSpec((L, SSD_CH), lambda i: (rev(i), 0)), pl.BlockSpec((L, LANE), lambda i: (rev(i), 0)),
                   pl.BlockSpec((SUBLANE, LANE), lambda i: (0, 0))],
        out_shape=[jax.ShapeDtypeStruct((S, SSD_CH), F32), jax.ShapeDtypeStruct((S, LANE), F32),
                   jax.ShapeDtypeStruct((SUBLANE, LANE), F32)],
        scratch_shapes=[pltpu.VMEM((SSD_PAIRS, LANE, SSD_STATE), F32), pltpu.VMEM((LANE, L), F32)],
        compiler_params=_cparams("arbitrary"),
    )(xc, rest, dtb, alog, dskip, hs, dy)


def _gate_norm_fwd(y, rest, zcol, nw, *, name):
    S, W = y.shape
    ts = _row_tile(S)
    GW = W // SSD_GROUPS

    def body(y_ref, z_ref, nw_ref, o_ref):
        z = z_ref[...]
        t = y_ref[...] * (z * _sigmoid(z))
        for g in range(SSD_GROUPS):
            cols = slice(g * GW, (g + 1) * GW)
            tg = t[:, cols]
            rr = lax.rsqrt(jnp.mean(tg * tg, axis=-1, keepdims=True) + NORM_EPS)
            o_ref[:, cols] = (tg * rr * nw_ref[:, cols]).astype(BF16)

    row = pl.BlockSpec((ts, W), lambda i: (i, 0))
    return pl.pallas_call(
        body, name=name, grid=(S // ts,),
        in_specs=[row, pl.BlockSpec((ts, W), lambda i: (i, zcol)), pl.BlockSpec((1, W), lambda i: (0, 0))],
        out_specs=row,
        out_shape=jax.ShapeDtypeStruct((S, W), BF16),
        compiler_params=_cparams("parallel"),
    )(y, rest, nw)


def _gate_norm_bwd(y, rest, zcol, nw, do, *, name):
    S, W = y.shape
    ts = _row_tile(S)
    GW = W // SSD_GROUPS

    def body(y_ref, z_ref, nw_ref, do_ref, dy_ref, dz_ref, dnw_ref):
        z = z_ref[...]
        yv = y_ref[...]
        s, ds = _dsilu(z)
        sz = z * s
        t = yv * sz
        dov = do_ref[...]
        parts = []
        for g in range(SSD_GROUPS):
            cols = slice(g * GW, (g + 1) * GW)
            tg = t[:, cols]
            rr = lax.rsqrt(jnp.mean(tg * tg, axis=-1, keepdims=True) + NORM_EPS)
            n = tg * rr
            dog = dov[:, cols]
            dn = dog * nw_ref[:, cols]
            dt = rr * (dn - n * jnp.mean(dn * n, axis=-1, keepdims=True))
            dy_ref[:, cols] = dt * sz[:, cols]
            dz_ref[:, cols] = (dt * yv[:, cols] * ds[:, cols]).astype(BF16)
            parts.append(jnp.sum(dog * n, axis=0, keepdims=True))
        _acc_out(dnw_ref, jnp.concatenate(parts, axis=1), pl.program_id(0) == 0)

    row = pl.BlockSpec((ts, W), lambda i: (i, 0))
    vec = pl.BlockSpec((1, W), lambda i: (0, 0))
    return pl.pallas_call(
        body, name=name, grid=(S // ts,),
        in_specs=[row, pl.BlockSpec((ts, W), lambda i: (i, zcol)), vec, row],
        out_specs=[row, row, vec],
        out_shape=[jax.ShapeDtypeStruct((S, W), F32), jax.ShapeDtypeStruct((S, W), BF16),
                   jax.ShapeDtypeStruct((1, W), F32)],
        compiler_params=_cparams("arbitrary"),
    )(y, rest, nw, do)


N_DEV = 8
MESH = pl.DeviceIdType.MESH
ANY = pl.BlockSpec(memory_space=pl.ANY)


def _all_gather(xs, *, name):
    n = len(xs)

    def body(*refs):
        x_refs, o_refs = refs[:n], refs[n:2 * n]
        send_sems, recv_sems, local_sems = refs[2 * n:]
        px, py, pc = lax.axis_index("x"), lax.axis_index("y"), lax.axis_index("c")
        me, sibling = (px, py, pc), (px, py, 1 - pc)
        chips = [(1 - px, py), (px, 1 - py), (1 - px, 1 - py)]

        def copy(a, k, block, to, src=None):
            slot = o_refs[a].at[4 * block[0] + 2 * block[1] + block[2]]
            return pltpu.make_async_remote_copy(
                src_ref=slot if src is None else src, dst_ref=slot,
                send_sem=send_sems.at[a, k], recv_sem=recv_sems.at[a, k], device_id=to, device_id_type=MESH)

        mine = [pltpu.make_async_copy(x_refs[a], o_refs[a].at[4 * px + 2 * py + pc], local_sems.at[a]) for a in range(n)]
        for cp in mine:
            cp.start()
        first = []
        for a in range(n):
            first.append(copy(a, 0, me, sibling, src=x_refs[a]))
            first += [copy(a, 1 + j, me, (*chip, pc), src=x_refs[a]) for j, chip in enumerate(chips)]
        for cp in first:
            cp.start()
        passed = []
        for j, chip in enumerate(chips):
            for a in range(n):
                copy(a, 1 + j, (*chip, pc), me).wait_recv()
                fwd = copy(a, 4 + j, (*chip, pc), sibling)
                fwd.start()
                passed.append(fwd)
        for a in range(n):
            copy(a, 0, sibling, me).wait_recv()
            for j, chip in enumerate(chips):
                copy(a, 4 + j, (*chip, 1 - pc), me).wait_recv()
        for cp in first + passed:
            cp.wait_send()
        for cp in mine:
            cp.wait()

    return pl.pallas_call(
        body, name=name,
        out_shape=[jax.ShapeDtypeStruct((N_DEV,) + x.shape, x.dtype) for x in xs],
        in_specs=[ANY] * n, out_specs=[ANY] * n,
        scratch_shapes=[pltpu.SemaphoreType.DMA((n, N_DEV - 1)), pltpu.SemaphoreType.DMA((n, N_DEV - 1)),
                        pltpu.SemaphoreType.DMA((n,))],
    )(*xs)


def _exchange(sends, *, name):
    n = len(sends)

    def body(*refs):
        s_refs, r_refs = refs[:n], refs[n:2 * n]
        send_sems, recv_sems, local_sems = refs[2 * n:]
        px, py, pc = lax.axis_index("x"), lax.axis_index("y"), lax.axis_index("c")
        me = 4 * px + 2 * py + pc
        local = [pltpu.make_async_copy(s_refs[a].at[me], r_refs[a].at[me], local_sems.at[a]) for a in range(n)]
        for cp in local:
            cp.start()
        sent, landing = [], []
        for k in range(1, N_DEV):
            qx = 1 - px if k & 4 else px
            qy = 1 - py if k & 2 else py
            qc = 1 - pc if k & 1 else pc
            peer = 4 * qx + 2 * qy + qc
            for a in range(n):
                def rdma(dst_slot):
                    return pltpu.make_async_remote_copy(
                        src_ref=s_refs[a].at[peer], dst_ref=r_refs[a].at[dst_slot],
                        send_sem=send_sems.at[a, k - 1], recv_sem=recv_sems.at[a, k - 1],
                        device_id=(qx, qy, qc), device_id_type=MESH)
                cp = rdma(me)
                cp.start()
                sent.append(cp)
                landing.append(rdma(peer))
        for cp in landing:
            cp.wait_recv()
        for cp in sent:
            cp.wait_send()
        for cp in local:
            cp.wait()

    return pl.pallas_call(
        body, name=name,
        out_shape=[jax.ShapeDtypeStruct(s.shape, s.dtype) for s in sends],
        in_specs=[ANY] * n, out_specs=[ANY] * n,
        scratch_shapes=[pltpu.SemaphoreType.DMA((n, N_DEV - 1)), pltpu.SemaphoreType.DMA((n, N_DEV - 1)),
                        pltpu.SemaphoreType.DMA((n,))],
    )(*sends)


IN_POOL, IN_QKV, IN_F, IN_Z, IN_XBC, IN_DT, IN_GATE, IN_TOTAL = 0, 512, 2048, 2056, 3080, 4616, 4632, 7704
QKV_W = IN_F - IN_QKV
REST_Z_BLK, REST_POOL_BLK, REST_XBC0, REST_FDT_BLK = 3, 8, 4608, 48
REST_FDT0 = REST_FDT_BLK * LANE
REST_USED = REST_FDT0 + (IN_Z - IN_F) + (IN_GATE - IN_DT)
REST_W = REST_FDT0 + LANE


def _w_in_runs(shard):
    segs = [(0, IN_QKV, IN_F, 0), (1, IN_GATE, IN_TOTAL, 0), (1, IN_Z, IN_XBC, 3072), (1, IN_POOL, IN_QKV, 4096),
            (1, IN_XBC, IN_DT, REST_XBC0), (1, IN_F, IN_Z, REST_FDT0), (1, IN_DT, IN_GATE, REST_FDT0 + IN_Z - IN_F)]
    runs = []
    for dst, t0, t1, d0 in segs:
        for j in range(N_DEV):
            lo, hi = max(t0, shard * j), min(t1, shard * (j + 1))
            if lo < hi:
                runs.append((dst, d0 + lo - t0, j, lo - shard * j, hi - lo))
    return runs


def _repack_w_in(g, shard, *, name):
    _, L, K, P = g.shape
    tr = 256
    runs = _w_in_runs(shard)

    def body(g_ref, q_ref, r_ref):
        r_ref[:, REST_USED:] = jnp.zeros((tr, REST_W - REST_USED), g_ref.dtype)
        for dst, b, j, a, n in runs:
            (q_ref, r_ref)[dst][:, b:b + n] = g_ref[j, :, a:a + n]

    return pl.pallas_call(
        body, name=name, grid=(L, K // tr),
        in_specs=[pl.BlockSpec((N_DEV, None, tr, P), lambda l, i: (0, l, i, 0))],
        out_specs=[pl.BlockSpec((None, tr, QKV_W), lambda l, i: (l, i, 0)),
                   pl.BlockSpec((None, tr, REST_W), lambda l, i: (l, i, 0))],
        out_shape=[jax.ShapeDtypeStruct((L, K, QKV_W), g.dtype), jax.ShapeDtypeStruct((L, K, REST_W), g.dtype)],
        compiler_params=_cparams("parallel", "parallel"),
    )(g)


def _held_index(l, li, i, n):
    return jnp.where(l == li, i, jnp.where(l > li, n - 1, 0))


def _repack_dw_in(dwq, dwr, shard, *, name):
    L = len(dwq)
    K = dwq[0].shape[0]
    tr = 128
    nt = K // tr
    runs = _w_in_runs(shard)

    def body(*refs):
        srcs, o_ref = refs[:2 * L], refs[2 * L]
        l = pl.program_id(0)
        for li in range(L):
            @pl.when(l == li)
            def _():
                for dst, b, j, a, n in runs:
                    o_ref[j, :, a:a + n] = srcs[2 * li + dst][:, b:b + n]

    in_specs = []
    for li in range(L):
        hold = functools.partial(lambda li, l, i: (_held_index(l, li, i, nt), 0), li)
        in_specs += [pl.BlockSpec((tr, QKV_W), hold), pl.BlockSpec((tr, REST_W), hold)]
    args = [x for pair in zip(dwq, dwr) for x in pair]
    return pl.pallas_call(
        body, name=name, grid=(L, nt),
        in_specs=in_specs,
        out_specs=pl.BlockSpec((N_DEV, None, tr, shard), lambda l, i: (0, l, i, 0)),
        out_shape=jax.ShapeDtypeStruct((N_DEV, L, K, shard), F32),
        compiler_params=_cparams("arbitrary", "arbitrary"),
    )(*args)


FLAT_W = 1024
ADAM_ROWS = 128


def _adam_math(w, g, m, v):
    m2 = ADAM_B1 * m + (1.0 - ADAM_B1) * g
    v2 = ADAM_B2 * v + (1.0 - ADAM_B2) * (g * g)
    m_hat = m2 / (1.0 - ADAM_B1 ** ADAM_STEP)
    v_hat = v2 / (1.0 - ADAM_B2 ** ADAM_STEP)
    delta = -ADAM_LR * (m_hat / (jnp.sqrt(v_hat) + ADAM_EPS) + ADAM_WD * w)
    return delta, m2, v2


def _row_div(R):
    best = R
    for t in range(SUBLANE, min(R, ADAM_ROWS) + 1, SUBLANE):
        if R % t == 0:
            best = t
    return best


def _sum8(ref_of):
    g = ref_of(0)
    for s in range(1, N_DEV):
        g = g + ref_of(s)
    return g


def _sum_adam(recvs, w, m, v, *, name):
    L, R, C = w.shape
    tr = _row_div(R)
    nt = R // tr
    single = not isinstance(recvs, (list, tuple))
    recvs = [recvs] if single else list(recvs)
    nr = len(recvs)

    def body(*refs):
        r_refs = refs[:nr]
        w_ref, m_ref, v_ref, g_ref, d_ref, m2_ref, v2_ref = refs[nr:]

        def run(ref_of):
            g = _sum8(ref_of)
            delta, m2, v2 = _adam_math(w_ref[...], g, m_ref[...], v_ref[...])
            g_ref[...] = g
            d_ref[...] = delta
            m2_ref[...] = m2
            v2_ref[...] = v2

        if single:
            run(lambda s: r_refs[0][s])
        else:
            l = pl.program_id(0)
            for li in range(L):
                @pl.when(l == li)
                def _():
                    run(lambda s: r_refs[li][s])

    if single:
        r_specs = [pl.BlockSpec((N_DEV, None, tr, C), lambda l, i: (0, l, i, 0))]
    else:
        r_specs = [pl.BlockSpec((N_DEV, tr, C), functools.partial(lambda li, l, i: (0, _held_index(l, li, i, nt), 0), li))
                   for li in range(L)]
    row = pl.BlockSpec((None, tr, C), lambda l, i: (l, i, 0))
    return pl.pallas_call(
        body, name=name, grid=(L, nt),
        in_specs=r_specs + [row, row, row],
        out_specs=[row] * 4,
        out_shape=[jax.ShapeDtypeStruct((L, R, C), F32)] * 4,
        compiler_params=_cparams("arbitrary", "arbitrary"),
    )(*recvs, w, m, v)


def _sum_parts(parts, *, name):
    _, R, C = parts.shape

    def body(p_ref, o_ref):
        o_ref[...] = _sum8(lambda s: p_ref[s])

    return pl.pallas_call(
        body, name=name, out_shape=jax.ShapeDtypeStruct((R, C), F32),
        in_specs=[pl.BlockSpec(memory_space=pltpu.VMEM)], out_specs=pl.BlockSpec(memory_space=pltpu.VMEM),
    )(parts)


def _adam(g, w, m, v, *, name):
    def body(g_ref, w_ref, m_ref, v_ref, d_ref, m2_ref, v2_ref):
        delta, m2, v2 = _adam_math(w_ref[...], g_ref[...], m_ref[...], v_ref[...])
        d_ref[...] = delta
        m2_ref[...] = m2
        v2_ref[...] = v2

    whole = pl.BlockSpec(memory_space=pltpu.VMEM)
    return pl.pallas_call(
        body, name=name, in_specs=[whole] * 4, out_specs=[whole] * 3,
        out_shape=[jax.ShapeDtypeStruct(g.shape, F32)] * 3,
    )(g, w, m, v)


SHARDED = ("w_in", "ffn_up", "p_pool", "p_attn", "p_ssd", "w_out", "ffn_down", "ssd_conv_w", "ffn_conv_w")
REPLICATED = ("norm_mix", "pool_mix", "pool_scale", "f_bias", "ssd_conv_b", "ssd_dt_bias", "ssd_a_log", "ssd_d",
              "ssd_norm", "norm_ffn", "ffn_conv_b", "norm_final")
WEIGHTS = ("norm_mix", "w_in", "pool_mix", "pool_scale", "f_bias", "ssd_conv_w", "ssd_conv_b", "ssd_dt_bias",
           "ssd_a_log", "ssd_d", "ssd_norm", "p_pool", "p_attn", "p_ssd", "w_out", "norm_ffn", "ffn_up",
           "ffn_conv_w", "ffn_conv_b", "ffn_down", "norm_final")


def _round_up(n, k):
    return -(-n // k) * k


def _pad_last(x, width):
    return jnp.pad(x, [(0, 0)] * (x.ndim - 1) + [(0, width - x.shape[-1])])


def _flat_rows(parts, rows):
    flat = jnp.concatenate([p.reshape(-1) for p in parts])
    return jnp.pad(flat, (0, rows * FLAT_W - flat.shape[0])).reshape(rows, FLAT_W)


def _lanes(v, lane0):
    return jnp.pad(v, (lane0, LANE - lane0 - v.shape[0]))[None]


def _cols_by_device(g):
    _, L, R, c = g.shape
    return jnp.moveaxis(g, 0, 2).reshape(L, R, N_DEV * c)


def _rows_by_device(g):
    _, L, r, C = g.shape
    return jnp.moveaxis(g, 0, 1).reshape(L, N_DEV * r, C)


def _layer_fwd(x, P, l):
    S = x.shape[0]
    nm = lambda s: f"l{l}_{s}"
    u = _rms_fwd(x, P["nm"], name=nm("rms_mix"))
    qkv = _mm(u, P["w_qkv"], out_dtype=BF16, name=nm("proj_qkv"))
    rest = _mm(u, P["w_rest"], name=nm("proj_rest"))
    d, ya0 = _pool_fwd(rest, REST_POOL_BLK, P["mix"], P["pscale"], name=nm("pool"))
    ya = _mm(ya0, P["p_pool"], name=nm("p_pool"))
    ccol, crow = _fox_pre(rest, REST_FDT_BLK, P["fb"], name=nm("fox_pre"))
    crow = crow.reshape(ATTN_HEADS, 1, S)
    o, lse = _attn_fwd(qkv, ccol, crow, name=nm("attn"))
    yb = _mm(o, P["p_attn"], name=nm("p_attn"))
    xc = _conv_silu_fwd(rest, REST_XBC0, SSD_CH, P["cw"], P["cb"], name=nm("ssd_conv"))
    y, hs = _ssd_fwd(xc, rest, REST_FDT_BLK, P["dtb"], P["alog"], P["dsk"], name=nm("ssd"))
    yc0 = _gate_norm_fwd(y, rest, REST_Z_BLK, P["snw"], name=nm("ssd_norm"))
    yc = _mm(yc0, P["p_ssd"], name=nm("p_ssd"))
    merged = _merge_fwd(rest, ya, yb, yc, name=nm("merge"))
    x1 = _mm(merged, P["w_out"], add=x, name=nm("w_out"))
    u2 = _rms_fwd(x1, P["nf"], name=nm("rms_ffn"))
    h0 = _mm_up_fwd(u2, P["g_up"], l, name=nm("ffn_up"))
    a = _ffn_act_fwd(h0, P["fcw"], P["fcb"], name=nm("ffn_act"))
    x2 = _mm(a, P["w_down"], add=x1, name=nm("ffn_down"))
    saved = dict(x=x, u=u, qkv=qkv, rest=rest, d=d, ya0=ya0, ya=ya, ccol=ccol, crow=crow, o=o, lse=lse, yb=yb,
                 xc=xc, y=y, hs=hs, yc0=yc0, yc=yc, merged=merged, x1=x1, u2=u2, h0=h0, a=a)
    return x2, saved


def _layer_bwd(g, gb, sv, P, l):
    nm = lambda s: f"l{l}_{s}_bwd"
    rest = sv["rest"]
    G = {}
    up_blocks, up_w, up_pad = P["up_blocks"]
    half = up_blocks // 2
    da = _mm(gb, P["w_down"], tb=True, name=nm("ffn_down_dx"))
    dwd = _mm(sv["a"], gb, ta=True, name=nm("ffn_down_dw"))
    G["ffn_down"] = dwd.reshape(half, up_pad, -1)[:, :up_w].reshape(N_DEV, -1, dwd.shape[1])
    dh, dfcw, dfcb = _ffn_act_bwd(sv["h0"], P["fcw"], P["fcb"], da, name=nm("ffn_act"))
    du2 = _mm_up_dx(dh, P["g_up"], l, name=nm("ffn_up_dx"))
    G["ffn_up"] = _mm_up_dw(sv["u2"], dh, up_blocks, up_w, name=nm("ffn_up_dw"))
    taps = dfcw.shape[1]
    G["ffn_conv_w"] = jnp.moveaxis(dfcw.reshape(2, taps, half, up_pad)[..., :up_w], 2, 1).reshape(N_DEV, taps, up_w)
    G["ffn_conv_b"] = dfcb.reshape(2, half, up_pad)[..., :up_w].reshape(-1)
    dx1, dx1b, dnf = _rms_bwd(sv["x1"], P["nf"], du2, g, name=nm("rms_ffn"))
    G["norm_ffn"] = dnf[0]
    dmerged = _mm(dx1b, P["w_out"], tb=True, name=nm("w_out_dx"))
    dwo = _mm(sv["merged"], dx1b, ta=True, name=nm("w_out_dw"))
    G["w_out"] = dwo.reshape(N_DEV, -1, dwo.shape[1])
    dgl, dya, dyb, dyc = _merge_bwd(rest, sv["ya"], sv["yb"], sv["yc"], dmerged, name=nm("merge"))
    G["p_pool"] = _mm_tn_blocks(sv["ya0"], dya, N_DEV, name=nm("p_pool_dw"))
    dya0 = _mm(dya, P["p_pool"], tb=True, name=nm("p_pool_dx"))
    G["p_attn"] = _mm_tn_blocks(sv["o"], dyb, N_DEV, name=nm("p_attn_dw"))
    do = _mm(dyb, P["p_attn"], tb=True, out_dtype=BF16, name=nm("p_attn_dx"))
    dps = _mm(sv["yc0"], dyc, ta=True, name=nm("p_ssd_dw"))
    G["p_ssd"] = dps.reshape(N_DEV, -1, dps.shape[1])
    dyc0 = _mm(dyc, P["p_ssd"], tb=True, name=nm("p_ssd_dx"))
    dpv, dmix, dpsc = _pool_bwd(dya0, sv["d"], P["mix"], P["pscale"], name=nm("pool"))
    G["pool_mix"], G["pool_scale"] = dmix, dpsc[0]
    dq, dcq, lrow, drow = _attn_bwd_dq(sv["qkv"], sv["o"], do, sv["lse"], sv["ccol"], sv["crow"], name=nm("attn_dq"))
    dk, dv, dck = _attn_bwd_dkv(sv["qkv"], do, lrow, drow, sv["ccol"], sv["crow"], name=nm("attn_dkv"))
    dy, dz, dsnw = _gate_norm_bwd(sv["y"], rest, REST_Z_BLK, P["snw"], dyc0, name=nm("ssd_norm"))
    G["ssd_norm"] = dsnw[0]
    dxc, ddt, dpar = _ssd_bwd(sv["xc"], rest, REST_FDT_BLK, P["dtb"], P["alog"], P["dsk"], sv["hs"], dy, name=nm("ssd"))
    G["ssd_a_log"] = dpar[0, DT_LANE0:DT_LANE0 + SSD_HEADS]
    G["ssd_d"] = dpar[1, DT_LANE0:DT_LANE0 + SSD_HEADS]
    dxbc, dcw, dcb = _conv_silu_bwd(rest, REST_XBC0, SSD_CH, P["cw"], P["cb"], dxc, name=nm("ssd_conv"))
    G["ssd_conv_w"] = jnp.moveaxis(dcw.reshape(dcw.shape[0], N_DEV, -1), 1, 0)
    G["ssd_conv_b"] = dcb[0]
    dfdt, dfdtb = _fox_post(dcq, dck, rest, REST_FDT_BLK, P["fb"], ddt, name=nm("fox_post"))
    G["f_bias"] = dfdtb[0, :ATTN_HEADS]
    G["ssd_dt_bias"] = dfdtb[0, DT_LANE0:DT_LANE0 + SSD_HEADS]
    dqkv = jnp.concatenate([dq, dk, dv], axis=1)
    drest = jnp.concatenate([dgl, dz, dpv, dxbc, dfdt], axis=1)
    du = _mm(dqkv, P["w_qkv"], tb=True, name=nm("proj_qkv_dx"))
    du = _mm(drest, P["w_rest"], tb=True, add=du, name=nm("proj_rest_dx"))
    G["w_qkv"] = _mm(sv["u"], dqkv, ta=True, name=nm("proj_qkv_dw"))
    G["w_rest"] = _mm(sv["u"], drest, ta=True, name=nm("proj_rest_dw"))
    dx, dxb, dnm = _rms_bwd(sv["x"], P["nm"], du, dx1, name=nm("rms_mix"))
    G["norm_mix"] = dnm[0]
    return dx, dxb, G


def kernel(x, norm_mix, w_in, pool_mix, pool_scale, f_bias, ssd_conv_w, ssd_conv_b, ssd_dt_bias, ssd_a_log, ssd_d, ssd_norm, p_pool, p_attn, p_ssd, w_out, norm_ffn, ffn_up, ffn_conv_w, ffn_conv_b, ffn_down, norm_final, loss_target, m_norm_mix, m_w_in, m_pool_mix, m_pool_scale, m_f_bias, m_ssd_conv_w, m_ssd_conv_b, m_ssd_dt_bias, m_ssd_a_log, m_ssd_d, m_ssd_norm, m_p_pool, m_p_attn, m_p_ssd, m_w_out, m_norm_ffn, m_ffn_up, m_ffn_conv_w, m_ffn_conv_b, m_ffn_down, m_norm_final, v_norm_mix, v_w_in, v_pool_mix, v_pool_scale, v_f_bias, v_ssd_conv_w, v_ssd_conv_b, v_ssd_dt_bias, v_ssd_a_log, v_ssd_d, v_ssd_norm, v_p_pool, v_p_attn, v_p_ssd, v_w_out, v_norm_ffn, v_ffn_up, v_ffn_conv_w, v_ffn_conv_b, v_ffn_down, v_norm_final):
    args = dict(locals())
    W = {n: args[n] for n in WEIGHTS}
    M = {n: args["m_" + n] for n in WEIGHTS}
    V = {n: args["v_" + n] for n in WEIGHTS}
    L = norm_mix.shape[0]
    shard = w_in.shape[-1]
    up_w = ffn_up.shape[-1]
    up_pad = _round_up(up_w, LANE)

    bf = lambda n: W[n].astype(BF16)
    g_win, g_up, g_pp, g_pa, g_ps, g_wo, g_dn, g_cw, g_fcw = _all_gather(
        [_pad_last(bf("w_in"), _round_up(shard, LANE)), _pad_last(bf("ffn_up"), up_pad), bf("p_pool"), bf("p_attn"),
         bf("p_ssd"), bf("w_out"), bf("ffn_down"), ssd_conv_w, ffn_conv_w], name="gather_weights")
    w_qkv, w_rest = _repack_w_in(g_win, shard, name="repack_w_in")
    p_pool_f, p_attn_f = _cols_by_device(g_pp), _cols_by_device(g_pa)
    p_ssd_f, w_out_f = _rows_by_device(g_ps), _rows_by_device(g_wo)
    half = N_DEV // 2
    w_down_f = _rows_by_device(g_dn)
    w_down_p = jnp.pad(w_down_f.reshape(L, half, up_w, -1), ((0, 0), (0, 0), (0, up_pad - up_w), (0, 0)))
    w_down_p = w_down_p.reshape(L, half * up_pad, -1)
    cw_f = _cols_by_device(g_cw)
    fcw_p = _cols_by_device(_pad_last(g_fcw, up_pad))
    fcb_p = _pad_last(ffn_conv_b.reshape(L, N_DEV, up_w), up_pad).reshape(L, 1, N_DEV * up_pad)

    params = [dict(
        nm=norm_mix[l][None], w_qkv=w_qkv[l], w_rest=w_rest[l], mix=pool_mix[l].astype(BF16),
        pscale=pool_scale[l][None], fb=_lanes(f_bias[l], 0), cw=cw_f[l], cb=ssd_conv_b[l][None],
        dtb=_lanes(ssd_dt_bias[l], DT_LANE0), alog=_lanes(ssd_a_log[l], DT_LANE0), dsk=_lanes(ssd_d[l], DT_LANE0),
        snw=ssd_norm[l][None], p_pool=p_pool_f[l], p_attn=p_attn_f[l], p_ssd=p_ssd_f[l], w_out=w_out_f[l],
        nf=norm_ffn[l][None], g_up=g_up, up_blocks=(N_DEV, up_w, up_pad), fcw=fcw_p[l], fcb=fcb_p[l],
        w_down=w_down_p[l]) for l in range(L)]
    h, saves = x[0], []
    for l in range(L):
        h, sv = _layer_fwd(h, params[l], l)
        saves.append(sv)
    loss_part, g, gb, dnfin = _loss_head(h, norm_final[None], loss_target[0], name="loss_head")
    G = [None] * L
    for l in reversed(range(L)):
        g, gb, G[l] = _layer_bwd(g, gb, saves[l], params[l], l)

    per_layer = [n for n in SHARDED if n != "w_in"]
    rep_grads = {n: jnp.stack([G[l][n] for l in range(L)]) for n in REPLICATED if n != "norm_final"}
    rep_grads["norm_final"] = dnfin[0]
    rep_size = sum(W[n].size for n in REPLICATED)
    rep_rows = _round_up(-(-rep_size // (N_DEV * FLAT_W)), SUBLANE)
    rep_send = _flat_rows([rep_grads[n] for n in REPLICATED], N_DEV * rep_rows).reshape(N_DEV, rep_rows, FLAT_W)
    dw_in = _repack_dw_in([G[l]["w_qkv"] for l in range(L)], [G[l]["w_rest"] for l in range(L)], shard,
                          name="repack_dw_in")
    sends = [dw_in] + [G[l][n] for n in per_layer for l in range(L)] + [rep_send]
    recvs = _exchange(sends, name="exchange_grads")

    out = {}

    def finish(n, recv):
        shape = W[n].shape
        as3 = lambda a: a.reshape(L, -1, shape[-1])
        res = _sum_adam(recv, as3(W[n]), as3(M[n]), as3(V[n]), name="sum_adam_" + n)
        for key, arr in zip(("grad_", "delta_", "new_m_", "new_v_"), res):
            out[key + n] = arr.reshape(shape)

    finish("w_in", recvs[0])
    for i, n in enumerate(per_layer):
        finish(n, recvs[1 + L * i:1 + L * (i + 1)])

    rep_sum = _sum_parts(recvs[-1], name="sum_replicated")
    rep_g = _all_gather([rep_sum], name="gather_replicated_grads")[0].reshape(N_DEV * rep_rows, FLAT_W)
    flat_rep = lambda D: _flat_rows([D[n] for n in REPLICATED], N_DEV * rep_rows)
    d_rp, m_rp, v_rp = _adam(rep_g, flat_rep(W), flat_rep(M), flat_rep(V), name="adam_replicated")
    off = 0
    for n in REPLICATED:
        size = W[n].size
        for key, arr in (("grad_", rep_g), ("delta_", d_rp), ("new_m_", m_rp), ("new_v_", v_rp)):
            out[key + n] = arr.reshape(-1)[off:off + size].reshape(W[n].shape)
        off += size

    loss = lax.psum(loss_part[0, 0], ("x", "y", "c"))
    return (loss, g[None], *[out["grad_" + n] for n in WEIGHTS], *[out["delta_" + n] for n in WEIGHTS],
            *[out["new_m_" + n] for n in WEIGHTS], *[out["new_v_" + n] for n in WEIGHTS])
```

```python
import functools

import jax
import jax.numpy as jnp
from jax import lax
from jax.experimental import pallas as pl
from jax.experimental.pallas import tpu as pltpu

F32 = jnp.float32
BF16 = jnp.bfloat16

LANE = 128
SUBLANE = 8
VMEM_LIMIT = 56 * 1024 * 1024

NORM_EPS = 1e-6
POOL_WINDOWS = (2, 4, 8, 16)
POOL_GROUP_DIM = 128
ATTN_HEADS = 8
ATTN_HEAD_DIM = 64
SSD_HEAD_DIM = 64
SSD_HEADS = 16
SSD_GROUPS = 2
SSD_STATE = 128
SSD_CHUNK = 128

ADAM_LR = 0.001
ADAM_B1 = 0.9
ADAM_B2 = 0.999
ADAM_EPS = 1e-08
ADAM_WD = 0.01
ADAM_STEP = 10


def _cparams(*sem):
    return pltpu.CompilerParams(dimension_semantics=tuple(sem), vmem_limit_bytes=VMEM_LIMIT)


def _tile(n, pref):
    if n <= pref:
        return n
    assert n % LANE == 0, n
    q = n // LANE
    best = 1
    for d in range(1, pref // LANE + 1):
        if q % d == 0:
            best = d
    return best * LANE


GRAD_WIRE = BF16


def _mm_call(args, in_specs, out_spec, out_shape, grid, *, ta, tb, nk, acc_shape, has_add, keep, name):
    dims = (((0 if ta else 1,), (1 if tb else 0,)), ((), ()))

    def body(*refs):
        a_ref, b_ref = refs[0], refs[1]
        o_ref, acc_ref = refs[-2], refs[-1]
        k = pl.program_id(2)

        @pl.when(k == 0)
        def _():
            acc_ref[...] = jnp.zeros_like(acc_ref)

        acc_ref[...] += lax.dot_general(a_ref[...], b_ref[...], dims, preferred_element_type=F32)

        @pl.when(k == nk - 1)
        def _():
            r = acc_ref[...]
            if has_add:
                r = r + refs[2][...]
            if keep is not None:
                r = r[:, :keep]
            o_ref[...] = r.astype(o_ref.dtype)

    return pl.pallas_call(
        body, name=name, grid=grid, in_specs=in_specs, out_specs=out_spec, out_shape=out_shape,
        scratch_shapes=[pltpu.VMEM(acc_shape, F32)],
        compiler_params=_cparams("parallel", "parallel", "arbitrary"),
    )(*args)


def _mm(a, b, *, ta=False, tb=False, add=None, out_dtype=F32, name):
    assert a.dtype == BF16 and b.dtype == BF16, (a.dtype, b.dtype)
    if ta:
        K, M = a.shape
    else:
        M, K = a.shape
    if tb:
        N, K2 = b.shape
    else:
        K2, N = b.shape
    assert K == K2, (a.shape, b.shape, ta, tb)
    tm, tn, tk = _tile(M, 512), _tile(N, 1024), _tile(K, 2048)
    a_spec = pl.BlockSpec((tk, tm), lambda i, j, k: (k, i)) if ta else pl.BlockSpec((tm, tk), lambda i, j, k: (i, k))
    b_spec = pl.BlockSpec((tn, tk), lambda i, j, k: (j, k)) if tb else pl.BlockSpec((tk, tn), lambda i, j, k: (k, j))
    in_specs, args = [a_spec, b_spec], [a, b]
    if add is not None:
        in_specs.append(pl.BlockSpec((tm, tn), lambda i, j, k: (i, j)))
        args.append(add)
    return _mm_call(args, in_specs, pl.BlockSpec((tm, tn), lambda i, j, k: (i, j)),
                    jax.ShapeDtypeStruct((M, N), out_dtype), (M // tm, N // tn, K // tk),
                    ta=ta, tb=tb, nk=K // tk, acc_shape=(tm, tn), has_add=add is not None, keep=None, name=name)


def _mm_tn_blocks(a, b, nblk, *, name):
    K, M = a.shape
    N = b.shape[1]
    bw = N // nblk
    tm, tk = _tile(M, 512), _tile(K, 2048)
    return _mm_call([a, b],
                    [pl.BlockSpec((tk, tm), lambda i, j, k: (k, i)), pl.BlockSpec((tk, bw), lambda i, j, k: (k, j))],
                    pl.BlockSpec((None, tm, bw), lambda i, j, k: (j, i, 0)),
                    jax.ShapeDtypeStruct((nblk, M, bw), GRAD_WIRE), (M // tm, nblk, K // tk),
                    ta=True, tb=False, nk=K // tk, acc_shape=(tm, bw), has_add=False, keep=None, name=name)


def _half_of(b, half):
    hi = jnp.where(b >= half, 1, 0)
    return hi, b - half * hi


def _mm_up_fwd(u, g_up, l, *, name):
    S, K = u.shape
    nblk, _, _, bw = g_up.shape
    tm = _tile(S, 512)
    return _mm_call([u, g_up],
                    [pl.BlockSpec((tm, K), lambda i, j, k: (i, 0)),
                     pl.BlockSpec((None, None, K, bw), lambda i, j, k: (j, l, 0, 0))],
                    pl.BlockSpec((tm, bw), lambda i, j, k: (i, j)),
                    jax.ShapeDtypeStruct((S, nblk * bw), F32), (S // tm, nblk, 1),
                    ta=False, tb=False, nk=1, acc_shape=(tm, bw), has_add=False, keep=None, name=name)


def _mm_up_dx(dh, g_up, l, *, name):
    _, S, _ = dh.shape
    nblk, _, N, bw = g_up.shape
    half = nblk // 2
    tm = _tile(S, 512)
    return _mm_call([dh, g_up],
                    [pl.BlockSpec((None, tm, bw), lambda i, j, k: (_half_of(k, half)[0], i, _half_of(k, half)[1])),
                     pl.BlockSpec((None, None, N, bw), lambda i, j, k: (k, l, 0, 0))],
                    pl.BlockSpec((tm, N), lambda i, j, k: (i, 0)),
                    jax.ShapeDtypeStruct((S, N), F32), (S // tm, 1, nblk),
                    ta=False, tb=True, nk=nblk, acc_shape=(tm, N), has_add=False, keep=None, name=name)


def _mm_up_dw(u, dh, nblk, keep, *, name):
    S, M = u.shape
    half = nblk // 2
    bw = dh.shape[2] // half
    tm, tk = _tile(M, 512), _tile(S, 2048)
    return _mm_call([u, dh],
                    [pl.BlockSpec((tk, tm), lambda i, j, k: (k, i)),
                     pl.BlockSpec((None, tk, bw), lambda i, j, k: (_half_of(j, half)[0], k, _half_of(j, half)[1]))],
                    pl.BlockSpec((None, tm, keep), lambda i, j, k: (j, i, 0)),
                    jax.ShapeDtypeStruct((nblk, M, keep), GRAD_WIRE), (M // tm, nblk, S // tk),
                    ta=True, tb=False, nk=S // tk, acc_shape=(tm, bw), has_add=False, keep=keep, name=name)


ROW_TILE = 512
HALO = 8
POOL_HALO = 16


def _row_tile(S):
    return min(ROW_TILE, S)


def _acc_out(ref, val, first):
    @pl.when(first)
    def _():
        ref[...] = val

    @pl.when(jnp.logical_not(first))
    def _():
        ref[...] += val


def _rms_fwd(x, w, *, name):
    S, D = x.shape
    ts = _row_tile(S)

    def body(x_ref, w_ref, u_ref):
        xv = x_ref[...]
        r = lax.rsqrt(jnp.mean(xv * xv, axis=-1, keepdims=True) + NORM_EPS)
        u_ref[...] = (xv * r * w_ref[...]).astype(BF16)

    return pl.pallas_call(
        body, name=name, grid=(S // ts,),
        in_specs=[pl.BlockSpec((ts, D), lambda i: (i, 0)), pl.BlockSpec((1, D), lambda i: (0, 0))],
        out_specs=pl.BlockSpec((ts, D), lambda i: (i, 0)),
        out_shape=jax.ShapeDtypeStruct((S, D), BF16),
        compiler_params=_cparams("parallel"),
    )(x, w)


def _rms_bwd(x, w, du, g, *, name):
    S, D = x.shape
    ts = _row_tile(S)

    def body(x_ref, w_ref, du_ref, g_ref, dx_ref, dxb_ref, dw_ref):
        xv = x_ref[...]
        r = lax.rsqrt(jnp.mean(xv * xv, axis=-1, keepdims=True) + NORM_EPS)
        n = xv * r
        duv = du_ref[...]
        dn = duv * w_ref[...]
        dx = g_ref[...] + r * (dn - n * jnp.mean(dn * n, axis=-1, keepdims=True))
        dx_ref[...] = dx
        dxb_ref[...] = dx.astype(BF16)
        _acc_out(dw_ref, jnp.sum(duv * n, axis=0, keepdims=True), pl.program_id(0) == 0)

    row = pl.BlockSpec((ts, D), lambda i: (i, 0))
    vec = pl.BlockSpec((1, D), lambda i: (0, 0))
    return pl.pallas_call(
        body, name=name, grid=(S // ts,),
        in_specs=[row, vec, row, row],
        out_specs=[row, row, vec],
        out_shape=[jax.ShapeDtypeStruct((S, D), F32), jax.ShapeDtypeStruct((S, D), BF16),
                   jax.ShapeDtypeStruct((1, D), F32)],
        compiler_params=_cparams("arbitrary"),
    )(x, w, du, g)


def _loss_head(x, w, target, *, name):
    S, D = x.shape
    ts = _row_tile(S)

    def body(x_ref, w_ref, t_ref, loss_ref, dx_ref, dxb_ref, dw_ref):
        xv = x_ref[...]
        wv = w_ref[...]
        r = lax.rsqrt(jnp.mean(xv * xv, axis=-1, keepdims=True) + NORM_EPS)
        n = xv * r
        err = n * wv - t_ref[...]
        part = 0.5 * jnp.sum(jnp.mean(err * err, axis=-1, keepdims=True), axis=0, keepdims=True)
        dy = err * (1.0 / D)
        dn = dy * wv
        dx = r * (dn - n * jnp.mean(dn * n, axis=-1, keepdims=True))
        dx_ref[...] = dx
        dxb_ref[...] = dx.astype(BF16)
        first = pl.program_id(0) == 0
        _acc_out(dw_ref, jnp.sum(dy * n, axis=0, keepdims=True), first)
        _acc_out(loss_ref, jnp.broadcast_to(part, loss_ref.shape), first)

    row = pl.BlockSpec((ts, D), lambda i: (i, 0))
    vec = pl.BlockSpec((1, D), lambda i: (0, 0))
    return pl.pallas_call(
        body, name=name, grid=(S // ts,),
        in_specs=[row, vec, row],
        out_specs=[pl.BlockSpec((SUBLANE, LANE), lambda i: (0, 0)), row, row, vec],
        out_shape=[jax.ShapeDtypeStruct((SUBLANE, LANE), F32), jax.ShapeDtypeStruct((S, D), F32),
                   jax.ShapeDtypeStruct((S, D), BF16), jax.ShapeDtypeStruct((1, D), F32)],
        compiler_params=_cparams("arbitrary"),
    )(x, w, target)


def _sigmoid(x):
    return 1.0 / (1.0 + jnp.exp(-x))


def _merge_fwd(gl, ya, yb, yc, *, name):
    S, D = ya.shape
    ts = _row_tile(S)

    def body(ga_ref, gb_ref, gc_ref, ya_ref, yb_ref, yc_ref, o_ref):
        m = (_sigmoid(ga_ref[...]) * ya_ref[...] + _sigmoid(gb_ref[...]) * yb_ref[...]
             + _sigmoid(gc_ref[...]) * yc_ref[...])
        o_ref[...] = m.astype(BF16)

    row = pl.BlockSpec((ts, D), lambda i: (i, 0))
    gspec = [pl.BlockSpec((ts, D), functools.partial(lambda b, i: (i, b), b)) for b in range(3)]
    return pl.pallas_call(
        body, name=name, grid=(S // ts,),
        in_specs=gspec + [row, row, row],
        out_specs=row,
        out_shape=jax.ShapeDtypeStruct((S, D), BF16),
        compiler_params=_cparams("parallel"),
    )(gl, gl, gl, ya, yb, yc)


def _merge_bwd(gl, ya, yb, yc, dm, *, name):
    S, D = ya.shape
    ts = _row_tile(S)

    def body(ga_ref, gb_ref, gc_ref, ya_ref, yb_ref, yc_ref, dm_ref, dgl_ref, da_ref, db_ref, dc_ref):
        dmv = dm_ref[...]
        for b, (g_ref, y_ref, dy_ref) in enumerate(((ga_ref, ya_ref, da_ref), (gb_ref, yb_ref, db_ref),
                                                     (gc_ref, yc_ref, dc_ref))):
            s = _sigmoid(g_ref[...])
            dy_ref[...] = (dmv * s).astype(BF16)
            dgl_ref[:, b * D:(b + 1) * D] = (dmv * y_ref[...] * s * (1.0 - s)).astype(BF16)

    row = pl.BlockSpec((ts, D), lambda i: (i, 0))
    gspec = [pl.BlockSpec((ts, D), functools.partial(lambda b, i: (i, b), b)) for b in range(3)]
    return pl.pallas_call(
        body, name=name, grid=(S // ts,),
        in_specs=gspec + [row, row, row, row],
        out_specs=[pl.BlockSpec((ts, 3 * D), lambda i: (i, 0)), row, row, row],
        out_shape=[jax.ShapeDtypeStruct((S, 3 * D), BF16)] + [jax.ShapeDtypeStruct((S, D), BF16)] * 3,
        compiler_params=_cparams("parallel"),
    )(gl, gl, gl, ya, yb, yc, dm)


POOL_WIDTH = 512


def _pool_cnt(t, w):
    return jnp.minimum(t + 1, w).astype(F32)


def _pool_fwd(rest, col, mix, scale, *, name):
    S = rest.shape[0]
    W, G, H = POOL_WIDTH, POOL_GROUP_DIM, POOL_HALO
    ts = _row_tile(S)

    def body(v_ref, h_ref, mix_ref, sc_ref, d_ref, y_ref, ext_ref):
        i = pl.program_id(0)
        cur = v_ref[...]
        ext_ref[0:H, :] = jnp.where(i > 0, h_ref[...], 0.0)
        ext_ref[H:, :] = cur
        t = i * ts + lax.broadcasted_iota(jnp.int32, (ts, 1), 0)
        for g, w in enumerate(POOL_WINDOWS):
            cols = slice(g * G, (g + 1) * G)
            acc = cur[:, cols]
            for k in range(1, w):
                acc = acc + ext_ref[pl.ds(H - k, ts), cols]
            db = (acc / _pool_cnt(t, w) - cur[:, cols]).astype(BF16)
            d_ref[:, cols] = db
            y = jnp.dot(db, mix_ref[g], preferred_element_type=F32) * sc_ref[:, cols]
            y_ref[:, cols] = y.astype(BF16)

    row = pl.BlockSpec((ts, W), lambda i: (i, 0))
    return pl.pallas_call(
        body, name=name, grid=(S // ts,),
        in_specs=[pl.BlockSpec((ts, W), lambda i: (i, col)),
                  pl.BlockSpec((H, W), lambda i: (jnp.maximum(i * (ts // H) - 1, 0), col)),
                  pl.BlockSpec((len(POOL_WINDOWS), G, G), lambda i: (0, 0, 0)),
                  pl.BlockSpec((1, W), lambda i: (0, 0))],
        out_specs=[row, row],
        out_shape=[jax.ShapeDtypeStruct((S, W), BF16)] * 2,
        scratch_shapes=[pltpu.VMEM((H + ts, W), F32)],
        compiler_params=_cparams("parallel"),
    )(rest, rest, mix, scale)


def _pool_bwd(dy, d, mix, scale, *, name):
    S = dy.shape[0]
    W, G, H = POOL_WIDTH, POOL_GROUP_DIM, POOL_HALO
    ts = _row_tile(S)
    n = S // ts
    NT = (((1,), (1,)), ((), ()))
    TN = (((0,), (0,)), ((), ()))

    def body(dy_ref, dyn_ref, d_ref, mix_ref, sc_ref, dv_ref, dmix_ref, dsc_ref, ext_ref):
        i = pl.program_id(0)

        @pl.when(i == 0)
        def _():
            dmix_ref[...] = jnp.zeros_like(dmix_ref)
            dsc_ref[...] = jnp.zeros_like(dsc_ref)

        t = i * ts + lax.broadcasted_iota(jnp.int32, (ts, 1), 0)
        tn = (i + 1) * ts + lax.broadcasted_iota(jnp.int32, (H, 1), 0)
        for g, w in enumerate(POOL_WINDOWS):
            cols = slice(g * G, (g + 1) * G)
            sc = sc_ref[:, cols]
            dyv = dy_ref[:, cols]
            db = d_ref[:, cols]
            mg = mix_ref[g]
            yp = jnp.dot(db, mg, preferred_element_type=F32)
            dsc_ref[:, cols] += jnp.sum(dyv * yp, axis=0, keepdims=True)
            dyp = (dyv * sc).astype(BF16)
            dd = lax.dot_general(dyp, mg, NT, preferred_element_type=F32)
            dmix_ref[g] += lax.dot_general(db, dyp, TN, preferred_element_type=F32)
            dyn = (jnp.where(i < n - 1, dyn_ref[:, cols], 0.0) * sc).astype(BF16)
            ddn = lax.dot_general(dyn, mg, NT, preferred_element_type=F32)
            e = dd / _pool_cnt(t, w)
            ext_ref[0:ts, cols] = e
            ext_ref[ts:, cols] = ddn / _pool_cnt(tn, w)
            acc = e
            for k in range(1, w):
                acc = acc + ext_ref[pl.ds(k, ts), cols]
            dv_ref[:, cols] = (acc - dd).astype(BF16)

    row = pl.BlockSpec((ts, W), lambda i: (i, 0))
    return pl.pallas_call(
        body, name=name, grid=(n,),
        in_specs=[row,
                  pl.BlockSpec((H, W), lambda i: (jnp.minimum((i + 1) * (ts // H), S // H - 1), 0)),
                  row,
                  pl.BlockSpec((len(POOL_WINDOWS), G, G), lambda i: (0, 0, 0)),
                  pl.BlockSpec((1, W), lambda i: (0, 0))],
        out_specs=[row, pl.BlockSpec((len(POOL_WINDOWS), G, G), lambda i: (0, 0, 0)),
                   pl.BlockSpec((1, W), lambda i: (0, 0))],
        out_shape=[jax.ShapeDtypeStruct((S, W), BF16),
                   jax.ShapeDtypeStruct((len(POOL_WINDOWS), G, G), F32),
                   jax.ShapeDtypeStruct((1, W), F32)],
        scratch_shapes=[pltpu.VMEM((ts + H, W), F32)],
        compiler_params=_cparams("arbitrary"),
    )(dy, dy, d, mix, scale)


def _conv_taps(ext_ref, w_ref, b_ref, rows, first):
    K = w_ref.shape[0]
    pre = b_ref[...] + w_ref[K - 1:K, :] * ext_ref[pl.ds(first, rows), :]
    for k in range(K - 1):
        pre = pre + w_ref[k:k + 1, :] * ext_ref[pl.ds(first - (K - 1) + k, rows), :]
    return pre


def _prev_halo_spec(ts, tc, colfn):
    return pl.BlockSpec((HALO, tc), lambda *g: (jnp.maximum(g[-1] * (ts // HALO) - 1, 0), colfn(*g)))


def _conv_silu_fwd(rest, col0, C, w, b, *, name):
    S = rest.shape[0]
    ts, tc = _row_tile(S), 512
    assert col0 % tc == 0 and C % tc == 0
    cb = col0 // tc

    def body(x_ref, h_ref, w_ref, b_ref, o_ref, ext_ref):
        i = pl.program_id(1)
        ext_ref[0:HALO, :] = jnp.where(i > 0, h_ref[...], 0.0)
        ext_ref[HALO:, :] = x_ref[...]
        pre = _conv_taps(ext_ref, w_ref, b_ref, ts, HALO)
        o_ref[...] = pre * _sigmoid(pre)

    K = w.shape[0]
    return pl.pallas_call(
        body, name=name, grid=(C // tc, S // ts),
        in_specs=[pl.BlockSpec((ts, tc), lambda j, i: (i, cb + j)),
                  _prev_halo_spec(ts, tc, lambda j, i: cb + j),
                  pl.BlockSpec((K, tc), lambda j, i: (0, j)),
                  pl.BlockSpec((1, tc), lambda j, i: (0, j))],
        out_specs=pl.BlockSpec((ts, tc), lambda j, i: (i, j)),
        out_shape=jax.ShapeDtypeStruct((S, C), F32),
        scratch_shapes=[pltpu.VMEM((HALO + ts, tc), F32)],
        compiler_params=_cparams("parallel", "parallel"),
    )(rest, rest, w, b)


def _dsilu(pre):
    s = _sigmoid(pre)
    return s, s * (1.0 + pre * (1.0 - s))


def _conv_silu_bwd(rest, col0, C, w, b, dy, *, name):
    S = rest.shape[0]
    ts, tc = _row_tile(S), 512
    cb = col0 // tc
    n = S // ts
    K = w.shape[0]
    R = ts + HALO

    def body(x_ref, hp_ref, hn_ref, w_ref, b_ref, dy_ref, dyn_ref, dx_ref, dw_ref, db_ref, ext_ref, g_ref):
        i = pl.program_id(1)
        last = i == n - 1
        ext_ref[0:HALO, :] = jnp.where(i > 0, hp_ref[...], 0.0)
        ext_ref[HALO:HALO + ts, :] = x_ref[...]
        ext_ref[HALO + ts:, :] = jnp.where(last, 0.0, hn_ref[...])
        g_ref[0:ts, :] = dy_ref[...]
        g_ref[ts:, :] = jnp.where(last, 0.0, dyn_ref[...])
        pre = _conv_taps(ext_ref, w_ref, b_ref, R, HALO)
        g_ref[...] = g_ref[...] * _dsilu(pre)[1]
        dx = w_ref[K - 1:K, :] * g_ref[pl.ds(0, ts), :]
        for k in range(K - 1):
            dx = dx + w_ref[k:k + 1, :] * g_ref[pl.ds(K - 1 - k, ts), :]
        dx_ref[...] = dx.astype(BF16)
        gc = g_ref[pl.ds(0, ts), :]
        first = i == 0
        _acc_out(db_ref, jnp.sum(gc, axis=0, keepdims=True), first)
        dws = [jnp.sum(gc * ext_ref[pl.ds(HALO - (K - 1) + k, ts), :], axis=0, keepdims=True) for k in range(K)]
        _acc_out(dw_ref, jnp.concatenate(dws, axis=0), first)

    nxt = lambda cf: pl.BlockSpec((HALO, tc), lambda j, i: (jnp.minimum((i + 1) * (ts // HALO), S // HALO - 1), cf(j)))
    return pl.pallas_call(
        body, name=name, grid=(C // tc, n),
        in_specs=[pl.BlockSpec((ts, tc), lambda j, i: (i, cb + j)),
                  _prev_halo_spec(ts, tc, lambda j, i: cb + j),
                  nxt(lambda j: cb + j),
                  pl.BlockSpec((K, tc), lambda j, i: (0, j)),
                  pl.BlockSpec((1, tc), lambda j, i: (0, j)),
                  pl.BlockSpec((ts, tc), lambda j, i: (i, j)),
                  nxt(lambda j: j)],
        out_specs=[pl.BlockSpec((ts, tc), lambda j, i: (i, j)),
                   pl.BlockSpec((K, tc), lambda j, i: (0, j)),
                   pl.BlockSpec((1, tc), lambda j, i: (0, j))],
        out_shape=[jax.ShapeDtypeStruct((S, C), BF16), jax.ShapeDtypeStruct((K, C), F32),
                   jax.ShapeDtypeStruct((1, C), F32)],
        scratch_shapes=[pltpu.VMEM((HALO + R, tc), F32), pltpu.VMEM((R, tc), F32)],
        compiler_params=_cparams("parallel", "arbitrary"),
    )(rest, rest, rest, w, b, dy, dy)


FFN_TC = 256


def _ffn_act_fwd(h0, w, b, *, name):
    S, F2 = h0.shape
    F = F2 // 2
    ts, tc = _row_tile(S), FFN_TC
    nb = F // tc
    K = w.shape[0]

    def body(xg_ref, hg_ref, xv_ref, hv_ref, wg_ref, wv_ref, bg_ref, bv_ref, o_ref, eg_ref, ev_ref):
        i = pl.program_id(1)
        for x_ref, h_ref, e_ref in ((xg_ref, hg_ref, eg_ref), (xv_ref, hv_ref, ev_ref)):
            e_ref[0:HALO, :] = jnp.where(i > 0, h_ref[...], 0.0)
            e_ref[HALO:, :] = x_ref[...]
        pg = _conv_taps(eg_ref, wg_ref, bg_ref, ts, HALO)
        pv = _conv_taps(ev_ref, wv_ref, bv_ref, ts, HALO)
        o_ref[...] = (pg * _sigmoid(pg) * pv).astype(BF16)

    def half(off):
        return [pl.BlockSpec((ts, tc), lambda j, i: (i, off + j)), _prev_halo_spec(ts, tc, lambda j, i: off + j)]

    wspec = lambda off: pl.BlockSpec((K, tc), lambda j, i: (0, off + j))
    bspec = lambda off: pl.BlockSpec((1, tc), lambda j, i: (0, off + j))
    return pl.pallas_call(
        body, name=name, grid=(nb, S // ts),
        in_specs=half(0) + half(nb) + [wspec(0), wspec(nb), bspec(0), bspec(nb)],
        out_specs=pl.BlockSpec((ts, tc), lambda j, i: (i, j)),
        out_shape=jax.ShapeDtypeStruct((S, F), BF16),
        scratch_shapes=[pltpu.VMEM((HALO + ts, tc), F32)] * 2,
        compiler_params=_cparams("parallel", "parallel"),
    )(h0, h0, h0, h0, w, w, b, b)


def _ffn_act_bwd(h0, w, b, da, *, name):
    S, F2 = h0.shape
    F = F2 // 2
    ts, tc = _row_tile(S), FFN_TC
    nb = F // tc
    n = S // ts
    K = w.shape[0]
    R = ts + HALO

    def body(xg_ref, pg_ref, ng_ref, xv_ref, pv_ref, nv_ref, wg_ref, wv_ref, bg_ref, bv_ref, da_ref, dan_ref,
             dx_ref, dw_ref, db_ref, eg_ref, ev_ref, gg_ref, gv_ref):
        i = pl.program_id(1)
        last = i == n - 1
        for x_ref, p_ref, n_ref, e_ref in ((xg_ref, pg_ref, ng_ref, eg_ref), (xv_ref, pv_ref, nv_ref, ev_ref)):
            e_ref[0:HALO, :] = jnp.where(i > 0, p_ref[...], 0.0)
            e_ref[HALO:HALO + ts, :] = x_ref[...]
            e_ref[HALO + ts:, :] = jnp.where(last, 0.0, n_ref[...])
        gg_ref[0:ts, :] = da_ref[...]
        gg_ref[ts:, :] = jnp.where(last, 0.0, dan_ref[...])
        dav = gg_ref[...]
        pg = _conv_taps(eg_ref, wg_ref, bg_ref, R, HALO)
        pv = _conv_taps(ev_ref, wv_ref, bv_ref, R, HALO)
        s, ds = _dsilu(pg)
        gg_ref[...] = dav * pv * ds
        gv_ref[...] = dav * pg * s
        first = i == 0
        for h, (g_ref, e_ref, w_ref) in enumerate(((gg_ref, eg_ref, wg_ref), (gv_ref, ev_ref, wv_ref))):
            dx = w_ref[K - 1:K, :] * g_ref[pl.ds(0, ts), :]
            for k in range(K - 1):
                dx = dx + w_ref[k:k + 1, :] * g_ref[pl.ds(K - 1 - k, ts), :]
            dx_ref[h] = dx.astype(BF16)
            gc = g_ref[pl.ds(0, ts), :]
            dws = [jnp.sum(gc * e_ref[pl.ds(HALO - (K - 1) + k, ts), :], axis=0, keepdims=True) for k in range(K)]
            _acc_out(db_ref.at[h], jnp.sum(gc, axis=0, keepdims=True), first)
            _acc_out(dw_ref.at[h], jnp.concatenate(dws, axis=0), first)

    nxt = lambda off: pl.BlockSpec((HALO, tc), lambda j, i: (jnp.minimum((i + 1) * (ts // HALO), S // HALO - 1), off + j))

    def half(off):
        return [pl.BlockSpec((ts, tc), lambda j, i: (i, off + j)), _prev_halo_spec(ts, tc, lambda j, i: off + j), nxt(off)]

    wspec = lambda off: pl.BlockSpec((K, tc), lambda j, i: (0, off + j))
    bspec = lambda off: pl.BlockSpec((1, tc), lambda j, i: (0, off + j))
    return pl.pallas_call(
        body, name=name, grid=(nb, n),
        in_specs=half(0) + half(nb) + [wspec(0), wspec(nb), bspec(0), bspec(nb),
                                       pl.BlockSpec((ts, tc), lambda j, i: (i, j)), nxt(0)],
        out_specs=[pl.BlockSpec((2, ts, tc), lambda j, i: (0, i, j)),
                   pl.BlockSpec((2, K, tc), lambda j, i: (0, 0, j)),
                   pl.BlockSpec((2, 1, tc), lambda j, i: (0, 0, j))],
        out_shape=[jax.ShapeDtypeStruct((2, S, F), BF16), jax.ShapeDtypeStruct((2, K, F), F32),
                   jax.ShapeDtypeStruct((2, 1, F), F32)],
        scratch_shapes=[pltpu.VMEM((HALO + R, tc), F32)] * 2 + [pltpu.VMEM((R, tc), F32)] * 2,
        compiler_params=_cparams("parallel", "arbitrary"),
    )(h0, h0, h0, h0, h0, h0, w, w, b, b, da, da)


CUM_TILE = 256
ATT_TILE = 512
NEG = -1e30
N_PAIR = ATTN_HEADS // 2
NT_DIMS = (((1,), (1,)), ((), ()))


def _split3(x):
    a = x.astype(BF16)
    r = x - a.astype(F32)
    b = r.astype(BF16)
    c = (r - b.astype(F32)).astype(BF16)
    return a, b, c


def _tri_dot(tri, x):
    return sum(jnp.dot(tri, p, preferred_element_type=F32) for p in _split3(x))


def _log_sigmoid(x):
    return jnp.minimum(x, 0.0) - jnp.log(1.0 + jnp.exp(-jnp.abs(x)))


def _col(v, idx, lane):
    return jnp.sum(jnp.where(lane == idx, v, 0.0), axis=1, keepdims=True)


def _fox_pre(rest, col, fb, *, name):
    S = rest.shape[0]
    ts = min(CUM_TILE, S)

    def body(f_ref, fb_ref, ccol_ref, carry_ref):
        i = pl.program_id(0)

        @pl.when(i == 0)
        def _():
            carry_ref[...] = jnp.zeros_like(carry_ref)

        lane = lax.broadcasted_iota(jnp.int32, (1, LANE), 1)
        logf = jnp.where(lane < ATTN_HEADS, _log_sigmoid(f_ref[...] + fb_ref[...]), 0.0)
        r = lax.broadcasted_iota(jnp.int32, (ts, ts), 0)
        c = lax.broadcasted_iota(jnp.int32, (ts, ts), 1)
        tri = jnp.where(c <= r, 1.0, 0.0).astype(BF16)
        cs = _tri_dot(tri, logf) + carry_ref[...]
        ccol_ref[...] = cs
        carry_ref[...] = cs[ts - 1:ts, :]

    return pl.pallas_call(
        body, name=name, grid=(S // ts,),
        in_specs=[pl.BlockSpec((ts, LANE), lambda i: (i, col)), pl.BlockSpec((1, LANE), lambda i: (0, 0))],
        out_specs=pl.BlockSpec((ts, LANE), lambda i: (i, 0)),
        out_shape=jax.ShapeDtypeStruct((S, LANE), F32),
        scratch_shapes=[pltpu.VMEM((1, LANE), F32)],
        compiler_params=_cparams("arbitrary"),
    )(rest, fb)


def _fox_post(dcq, dck, rest, col, fb, ddt, *, name):
    S = rest.shape[0]
    ts = min(CUM_TILE, S)
    n = S // ts

    def body(dcq_ref, dck_ref, f_ref, fb_ref, ddt_ref, o_ref, db_ref, carry_ref):
        i = pl.program_id(0)

        @pl.when(i == 0)
        def _():
            carry_ref[...] = jnp.zeros_like(carry_ref)

        lane = lax.broadcasted_iota(jnp.int32, (1, LANE), 1)
        dc = jnp.zeros((ts, LANE), F32)
        for h in range(ATTN_HEADS):
            d = dcq_ref[h // 2] - dck_ref[h // 2]
            dc = jnp.where(lane == h, _col(d, h % 2, lane), dc)
        r = lax.broadcasted_iota(jnp.int32, (ts, ts), 0)
        c = lax.broadcasted_iota(jnp.int32, (ts, ts), 1)
        tri = jnp.where(c >= r, 1.0, 0.0).astype(BF16)
        rc = _tri_dot(tri, dc) + carry_ref[...]
        carry_ref[...] = rc[0:1, :]
        df = rc * _sigmoid(-(f_ref[...] + fb_ref[...]))
        out = jnp.where(lane < ATTN_HEADS, df, ddt_ref[...])
        o_ref[...] = out.astype(BF16)
        _acc_out(db_ref, jnp.sum(out, axis=0, keepdims=True), i == 0)

    rev = lambda i: n - 1 - i
    return pl.pallas_call(
        body, name=name, grid=(n,),
        in_specs=[pl.BlockSpec((N_PAIR, ts, LANE), lambda i: (0, rev(i), 0)),
                  pl.BlockSpec((N_PAIR, ts, LANE), lambda i: (0, rev(i), 0)),
                  pl.BlockSpec((ts, LANE), lambda i: (rev(i), col)),
                  pl.BlockSpec((1, LANE), lambda i: (0, 0)),
                  pl.BlockSpec((ts, LANE), lambda i: (rev(i), 0))],
        out_specs=[pl.BlockSpec((ts, LANE), lambda i: (rev(i), 0)), pl.BlockSpec((1, LANE), lambda i: (0, 0))],
        out_shape=[jax.ShapeDtypeStruct((S, LANE), BF16), jax.ShapeDtypeStruct((1, LANE), F32)],
        scratch_shapes=[pltpu.VMEM((1, LANE), F32)],
        compiler_params=_cparams("arbitrary"),
    )(dcq, dck, rest, fb, ddt)


def _head_masks():
    lane = lax.broadcasted_iota(jnp.int32, (1, LANE), 1)
    return lane, (lane < ATTN_HEAD_DIM, lane >= ATTN_HEAD_DIM)


def _causal_mask(t):
    r = lax.broadcasted_iota(jnp.int32, (t, t), 0)
    c = lax.broadcasted_iota(jnp.int32, (t, t), 1)
    return r, c


AUG_ONE = 0
AUG_C = 3


def _free_lane0(hh):
    return ATTN_HEAD_DIM if hh == 0 else 0


def _augment(x, head_mask, lane, f0, c, key_side):
    terms = _split3(c)
    one = jnp.ones((), BF16)
    ones_at, terms_at = (AUG_ONE, AUG_C) if key_side else (AUG_C, AUG_ONE)
    out = jnp.where(head_mask, x, jnp.zeros((), BF16))
    for k in range(3):
        out = jnp.where(lane == f0 + ones_at + k, one, out)
        out = jnp.where(lane == f0 + terms_at + k, terms[k], out)
    return out


def _attn_fwd(qkv, ccol, *, name):
    S = qkv.shape[0]
    t = min(ATT_TILE, S)
    n = S // t
    scale = ATTN_HEAD_DIM ** -0.5

    def body(q_ref, k_ref, v_ref, cc_ref, o_ref, lrow_ref, kaug_ref, vt_ref, m_ref, l_ref, acc_ref):
        j, i = pl.program_id(0), pl.program_id(1)
        lane, masks = _head_masks()

        @pl.when(i == 0)
        def _():
            def fill(b, carry):
                off = pl.multiple_of(b * t, t)
                kblk = k_ref[pl.ds(off, t), :]
                ccb = cc_ref[pl.ds(off, t), :]
                for hh in range(2):
                    ck = _col(ccb, 2 * j + hh, lane)
                    kaug_ref[hh, pl.ds(off, t), :] = _augment(kblk, masks[hh], lane, _free_lane0(hh), -ck, True)
                vt_ref[:, pl.ds(off, t)] = v_ref[pl.ds(off, t), :].astype(F32).T.astype(BF16)
                return carry
            lax.fori_loop(0, n, fill, 0)

        q2 = q_ref[...] * scale
        ccq = cc_ref[pl.ds(pl.multiple_of(i * t, t), t), :]
        qaug = [_augment(q2, masks[hh], lane, _free_lane0(hh), _col(ccq, 2 * j + hh, lane), False) for hh in range(2)]
        m_ref[...] = jnp.full(m_ref.shape, NEG, F32)
        l_ref[...] = jnp.zeros_like(l_ref)
        acc_ref[...] = jnp.zeros_like(acc_ref)

        def step(kb, masked):
            off = pl.multiple_of(kb * t, t)
            vt = vt_ref[:, pl.ds(off, t)]
            for hh in range(2):
                st = lax.dot_general(kaug_ref[hh, pl.ds(off, t), :], qaug[hh], NT_DIMS, preferred_element_type=F32)
                if masked:
                    r, c = _causal_mask(t)
                    st = jnp.where(r <= c, st, NEG)
                m_old = m_ref[hh]
                m_new = jnp.maximum(m_old, jnp.max(st, axis=0, keepdims=True))
                p = jnp.exp(st - m_new)
                alpha = jnp.exp(m_old - m_new)
                l_ref[hh] = alpha * l_ref[hh] + jnp.sum(p, axis=0, keepdims=True)
                acc_ref[hh] = alpha * acc_ref[hh] + jnp.dot(vt, p.astype(BF16), preferred_element_type=F32)
                m_ref[hh] = m_new

        def loop_body(kb, carry):
            step(kb, False)
            return carry

        lax.fori_loop(0, i, loop_body, 0)
        step(i, True)
        drow = lax.broadcasted_iota(jnp.int32, (LANE, 1), 0)
        ot = jnp.where(drow < ATTN_HEAD_DIM, acc_ref[0] / l_ref[0], acc_ref[1] / l_ref[1])
        o_ref[...] = ot.T.astype(BF16)
        for hh in range(2):
            lrow_ref[hh] = m_ref[hh] + jnp.log(l_ref[hh])

    return pl.pallas_call(
        body, name=name, grid=(N_PAIR, n),
        in_specs=[pl.BlockSpec((t, LANE), lambda j, i: (i, j)),
                  pl.BlockSpec((S, LANE), lambda j, i: (0, N_PAIR + j)),
                  pl.BlockSpec((S, LANE), lambda j, i: (0, 2 * N_PAIR + j)),
                  pl.BlockSpec((S, LANE), lambda j, i: (0, 0))],
        out_specs=[pl.BlockSpec((t, LANE), lambda j, i: (i, j)),
                   pl.BlockSpec((2, 1, t), lambda j, i: (j, 0, i))],
        out_shape=[jax.ShapeDtypeStruct((S, N_PAIR * LANE), BF16), jax.ShapeDtypeStruct((ATTN_HEADS, 1, S), F32)],
        scratch_shapes=[pltpu.VMEM((2, S, LANE), BF16), pltpu.VMEM((LANE, S), BF16),
                        pltpu.VMEM((2, 1, t), F32), pltpu.VMEM((2, 1, t), F32), pltpu.VMEM((2, LANE, t), F32)],
        compiler_params=_cparams("parallel", "arbitrary"),
    )(qkv, qkv, qkv, ccol)


def _attn_bwd(qkv, o, do, lrow, ccol, *, name):
    S = qkv.shape[0]
    t = min(ATT_TILE, S)
    n = S // t
    scale = ATTN_HEAD_DIM ** -0.5

    def body(q_ref, k_ref, v_ref, o_ref, do_ref, lrow_ref, cc_ref,
             dq_ref, dk_ref, dv_ref, dcq_ref, dck_ref, qaug_ref, drow_ref, dqt_ref, dka_ref, dva_ref):
        j, kb = pl.program_id(0), pl.program_id(1)
        lane, masks = _head_masks()
        drows = lax.broadcasted_iota(jnp.int32, (LANE, 1), 0)

        @pl.when(kb == 0)
        def _():
            dqt_ref[...] = jnp.zeros_like(dqt_ref)

            def fill(b, carry):
                off = pl.multiple_of(b * t, t)
                q2 = q_ref[pl.ds(off, t), :] * scale
                ccb = cc_ref[pl.ds(off, t), :]
                prod_t = (do_ref[pl.ds(off, t), :].astype(F32) * o_ref[pl.ds(off, t), :].astype(F32)).T
                lo = jnp.sum(jnp.where(drows < ATTN_HEAD_DIM, prod_t, 0.0), axis=0, keepdims=True)
                drow_ref[0, :, pl.ds(off, t)] = lo
                drow_ref[1, :, pl.ds(off, t)] = jnp.sum(prod_t, axis=0, keepdims=True) - lo
                for hh in range(2):
                    cq = _col(ccb, 2 * j + hh, lane)
                    qaug_ref[hh, pl.ds(off, t), :] = _augment(q2, masks[hh], lane, _free_lane0(hh), cq, False)
                return carry
            lax.fori_loop(0, n, fill, 0)

        koff = pl.multiple_of(kb * t, t)
        kblk = k_ref[...]
        v2 = v_ref[...]
        cck = cc_ref[pl.ds(koff, t), :]
        kaug = [_augment(kblk, masks[hh], lane, _free_lane0(hh), -_col(cck, 2 * j + hh, lane), True) for hh in range(2)]
        kaug_t = [ka.astype(F32).T.astype(BF16) for ka in kaug]
        vh = [jnp.where(mk, v2, jnp.zeros((), BF16)) for mk in masks]
        dka_ref[...] = jnp.zeros_like(dka_ref)
        dva_ref[...] = jnp.zeros_like(dva_ref)

        def step(qb, masked):
            off = pl.multiple_of(qb * t, t)
            doblk = do_ref[pl.ds(off, t), :]
            for hh in range(2):
                qa = qaug_ref[hh, pl.ds(off, t), :]
                st = lax.dot_general(kaug[hh], qa, NT_DIMS, preferred_element_type=F32)
                if masked:
                    r, c = _causal_mask(t)
                    st = jnp.where(r <= c, st, NEG)
                pt = jnp.exp(st - lrow_ref[hh, :, pl.ds(off, t)])
                dpt = lax.dot_general(vh[hh], doblk, NT_DIMS, preferred_element_type=F32)
                dst = (pt * (dpt - drow_ref[hh, :, pl.ds(off, t)])).astype(BF16)
                dva_ref[hh] += jnp.dot(pt.astype(BF16), doblk, preferred_element_type=F32)
                dka_ref[hh] += jnp.dot(dst, qa, preferred_element_type=F32)
                dqt_ref[hh, :, pl.ds(off, t)] += jnp.dot(kaug_t[hh], dst, preferred_element_type=F32)

        step(kb, True)

        def loop_body(qb, carry):
            step(qb, False)
            return carry

        lax.fori_loop(kb + 1, n, loop_body, 0)
        dk_ref[...] = jnp.where(masks[0], dka_ref[0], dka_ref[1]).astype(BF16)
        dv_ref[...] = jnp.where(masks[0], dva_ref[0], dva_ref[1]).astype(BF16)
        dck = [_col(dka_ref[hh], _free_lane0(hh) + AUG_C, lane) for hh in range(2)]
        dck_ref[0] = jnp.where(lane == 0, dck[0], jnp.where(lane == 1, dck[1], 0.0))

        @pl.when(kb == n - 1)
        def _():
            def flush(b, carry):
                off = pl.multiple_of(b * t, t)
                d = [dqt_ref[hh, :, pl.ds(off, t)].T for hh in range(2)]
                dq_ref[pl.ds(off, t), :] = (jnp.where(masks[0], d[0], d[1]) * scale).astype(BF16)
                dcq = [_col(d[hh], _free_lane0(hh) + AUG_ONE, lane) for hh in range(2)]
                dcq_ref[0, pl.ds(off, t), :] = jnp.where(lane == 0, dcq[0], jnp.where(lane == 1, dcq[1], 0.0))
                return carry
            lax.fori_loop(0, n, flush, 0)

    full = lambda cb: pl.BlockSpec((S, LANE), lambda j, kb: (0, cb(j)))
    kspec = lambda base: pl.BlockSpec((t, LANE), lambda j, kb: (kb, base + j))
    oblk = pl.BlockSpec((t, LANE), lambda j, kb: (kb, j))
    return pl.pallas_call(
        body, name=name, grid=(N_PAIR, n),
        in_specs=[full(lambda j: j), kspec(N_PAIR), kspec(2 * N_PAIR), full(lambda j: j), full(lambda j: j),
                  pl.BlockSpec((2, 1, S), lambda j, kb: (j, 0, 0)), full(lambda j: 0)],
        out_specs=[full(lambda j: j), oblk, oblk,
                   pl.BlockSpec((1, S, LANE), lambda j, kb: (j, 0, 0)),
                   pl.BlockSpec((1, t, LANE), lambda j, kb: (j, kb, 0))],
        out_shape=[jax.ShapeDtypeStruct((S, N_PAIR * LANE), BF16)] * 3 + [jax.ShapeDtypeStruct((N_PAIR, S, LANE), F32)] * 2,
        scratch_shapes=[pltpu.VMEM((2, S, LANE), BF16), pltpu.VMEM((2, 1, S), F32), pltpu.VMEM((2, LANE, S), F32),
                        pltpu.VMEM((2, t, LANE), F32), pltpu.VMEM((2, t, LANE), F32)],
        compiler_params=_cparams("parallel", "arbitrary"),
    )(qkv, qkv, qkv, o, do, lrow, ccol)


DT_LANE0 = ATTN_HEADS
SSD_PAIRS = SSD_HEADS // 2
SSD_X = SSD_HEADS * SSD_HEAD_DIM
SSD_B0 = SSD_X
SSD_C0 = SSD_X + SSD_GROUPS * SSD_STATE
SSD_CH = SSD_X + 2 * SSD_GROUPS * SSD_STATE
TN_DIMS = (((0,), (0,)), ((), ()))


def _softplus(x):
    return jnp.maximum(x, 0.0) + jnp.log(1.0 + jnp.exp(-jnp.abs(x)))


def _ssd_prep(fdt, dtb, alog):
    L = fdt.shape[0]
    lane = lax.broadcasted_iota(jnp.int32, (1, LANE), 1)
    hl = jnp.logical_and(lane >= DT_LANE0, lane < DT_LANE0 + SSD_HEADS)
    dtv = jnp.where(hl, _softplus(fdt + dtb), 0.0)
    A = jnp.where(hl, -jnp.exp(alog), 0.0)
    r = lax.broadcasted_iota(jnp.int32, (L, L), 0)
    c = lax.broadcasted_iota(jnp.int32, (L, L), 1)
    cs = _tri_dot(jnp.where(c <= r, 1.0, 0.0).astype(BF16), dtv * A)
    return lane, hl, dtv, A, cs, r, c


def _halves(lane, v0, v1):
    return jnp.where(lane < SSD_HEAD_DIM, v0, v1)


def _ssd_fwd(xc, rest, col, dtb, alog, dskip, *, name):
    S = xc.shape[0]
    L = SSD_CHUNK
    nc = S // L

    def body(xc_ref, f_ref, dtb_ref, al_ref, dk_ref, y_ref, hs_ref, h_ref, cst_ref):
        i = pl.program_id(0)

        @pl.when(i == 0)
        def _():
            h_ref[...] = jnp.zeros_like(h_ref)

        lane, hl, dtv, A, cs, r, c = _ssd_prep(f_ref[...], dtb_ref[...], al_ref[...])
        cst_ref[...] = cs.T
        cs_last = cs[L - 1:L, :]
        ecs = jnp.exp(cs)
        dec = jnp.exp(cs_last - cs)
        cd = jnp.exp(cs_last)
        dkv = dk_ref[...]
        prow = lax.broadcasted_iota(jnp.int32, (LANE, 1), 0)
        for g in range(SSD_GROUPS):
            Bg = xc_ref[:, SSD_B0 + g * SSD_STATE:SSD_B0 + (g + 1) * SSD_STATE].astype(BF16)
            Cg = xc_ref[:, SSD_C0 + g * SSD_STATE:SSD_C0 + (g + 1) * SSD_STATE].astype(BF16)
            CB = lax.dot_general(Cg, Bg, NT_DIMS, preferred_element_type=F32)
            for pp in range(SSD_PAIRS // SSD_GROUPS):
                pi = g * (SSD_PAIRS // SSD_GROUPS) + pp
                hl0 = DT_LANE0 + 2 * pi
                x2 = xc_ref[:, pi * LANE:(pi + 1) * LANE]
                xd = x2 * _halves(lane, _col(dtv, hl0, lane), _col(dtv, hl0 + 1, lane))
                xdb = xd.astype(BF16)
                yd = jnp.zeros((L, LANE), F32)
                for hh in range(2):
                    seg = _col(cs, hl0 + hh, lane) - cst_ref[hl0 + hh:hl0 + hh + 1, :]
                    M = CB * jnp.exp(jnp.where(c <= r, seg, NEG))
                    yh = jnp.dot(M.astype(BF16), xdb, preferred_element_type=F32)
                    yd = jnp.where((lane >= SSD_HEAD_DIM) if hh else (lane < SSD_HEAD_DIM), yh, yd)
                hp = h_ref[pi]
                hs_ref[0, pi] = hp
                yo = lax.dot_general(Cg, hp.astype(BF16), NT_DIMS, preferred_element_type=F32)
                yo = yo * _halves(lane, _col(ecs, hl0, lane), _col(ecs, hl0 + 1, lane))
                dsk = _halves(lane, _col(dkv, hl0, lane), _col(dkv, hl0 + 1, lane))
                y_ref[:, pi * LANE:(pi + 1) * LANE] = yd + yo + dsk * x2
                xw = (xd * _halves(lane, _col(dec, hl0, lane), _col(dec, hl0 + 1, lane))).astype(BF16)
                st = lax.dot_general(xw, Bg, TN_DIMS, preferred_element_type=F32)
                cdp = jnp.where(prow < SSD_HEAD_DIM, _col(cd, hl0, lane), _col(cd, hl0 + 1, lane))
                h_ref[pi] = cdp * hp + st

    vec = pl.BlockSpec((1, LANE), lambda i: (0, 0))
    return pl.pallas_call(
        body, name=name, grid=(nc,),
        in_specs=[pl.BlockSpec((L, SSD_CH), lambda i: (i, 0)), pl.BlockSpec((L, LANE), lambda i: (i, col)),
                  vec, vec, vec],
        out_specs=[pl.BlockSpec((L, SSD_X), lambda i: (i, 0)),
                   pl.BlockSpec((1, SSD_PAIRS, LANE, SSD_STATE), lambda i: (i, 0, 0, 0))],
        out_shape=[jax.ShapeDtypeStruct((S, SSD_X), F32),
                   jax.ShapeDtypeStruct((nc, SSD_PAIRS, LANE, SSD_STATE), F32)],
        scratch_shapes=[pltpu.VMEM((SSD_PAIRS, LANE, SSD_STATE), F32), pltpu.VMEM((LANE, L), F32)],
        compiler_params=_cparams("arbitrary"),
    )(xc, rest, dtb, alog, dskip)


def _pair_sums(lane, v):
    lo = jnp.sum(jnp.where(lane < SSD_HEAD_DIM, v, 0.0), axis=1, keepdims=True)
    return lo, jnp.sum(v, axis=1, keepdims=True) - lo


def _ssd_bwd(xc, rest, col, dtb, alog, dskip, hs, dy, *, name):
    S = xc.shape[0]
    L = SSD_CHUNK
    nc = S // L
    PG = SSD_PAIRS // SSD_GROUPS

    def body(xc_ref, f_ref, dtb_ref, al_ref, dk_ref, hs_ref, dy_ref, dxc_ref, ddt_ref, dp_ref, dh_ref, cst_ref):
        i = pl.program_id(0)

        @pl.when(i == 0)
        def _():
            dh_ref[...] = jnp.zeros_like(dh_ref)
            dp_ref[...] = jnp.zeros_like(dp_ref)

        fv = f_ref[...] + dtb_ref[...]
        lane, hl, dtv, A, cs, r, c = _ssd_prep(f_ref[...], dtb_ref[...], al_ref[...])
        cst_ref[...] = cs.T
        cs_last = cs[L - 1:L, :]
        ecs = jnp.exp(cs)
        dec = jnp.exp(cs_last - cs)
        cd = jnp.exp(cs_last)
        dkv = dk_ref[...]
        prow = lax.broadcasted_iota(jnp.int32, (LANE, 1), 0)
        lrow = lax.broadcasted_iota(jnp.int32, (L, 1), 0)
        is_last = lrow == L - 1
        causal = c <= r
        dcs = jnp.zeros((L, LANE), F32)
        ddt = jnp.zeros((L, LANE), F32)
        dD = jnp.zeros((1, LANE), F32)
        for g in range(SSD_GROUPS):
            Bg = xc_ref[:, SSD_B0 + g * SSD_STATE:SSD_B0 + (g + 1) * SSD_STATE].astype(BF16)
            Cg = xc_ref[:, SSD_C0 + g * SSD_STATE:SSD_C0 + (g + 1) * SSD_STATE].astype(BF16)
            CB = lax.dot_general(Cg, Bg, NT_DIMS, preferred_element_type=F32)
            dCB = jnp.zeros((L, L), F32)
            dB = jnp.zeros((L, SSD_STATE), F32)
            dC = jnp.zeros((L, SSD_STATE), F32)
            for pp in range(PG):
                pi = g * PG + pp
                hl0 = DT_LANE0 + 2 * pi
                x2 = xc_ref[:, pi * LANE:(pi + 1) * LANE]
                dy2 = dy_ref[:, pi * LANE:(pi + 1) * LANE]
                dtp = _halves(lane, _col(dtv, hl0, lane), _col(dtv, hl0 + 1, lane))
                xd = x2 * dtp
                xdb = xd.astype(BF16)
                dsk = _halves(lane, _col(dkv, hl0, lane), _col(dkv, hl0 + 1, lane))
                dx2 = dsk * dy2
                sD = _pair_sums(lane, dy2 * x2)
                hp = hs_ref[0, pi]
                hpb = hp.astype(BF16)
                ecsp = _halves(lane, _col(ecs, hl0, lane), _col(ecs, hl0 + 1, lane))
                yo = lax.dot_general(Cg, hpb, NT_DIMS, preferred_element_type=F32) * ecsp
                dW = (ecsp * dy2).astype(BF16)
                dC = dC + jnp.dot(dW, hpb, preferred_element_type=F32)
                dhp = lax.dot_general(dW, Cg, TN_DIMS, preferred_element_type=F32)
                sYo = _pair_sums(lane, dy2 * yo)
                dhn = dh_ref[pi]
                cdh = (_col(cd, hl0, lane), _col(cd, hl0 + 1, lane))
                dhp = dhp + jnp.where(prow < SSD_HEAD_DIM, cdh[0], cdh[1]) * dhn
                rs = jnp.sum(dhn * hp, axis=1, keepdims=True)
                lo = jnp.sum(jnp.where(prow < SSD_HEAD_DIM, rs, 0.0), axis=0, keepdims=True)
                dcd = (lo, jnp.sum(rs, axis=0, keepdims=True) - lo)
                dhnb = dhn.astype(BF16)
                decp = _halves(lane, _col(dec, hl0, lane), _col(dec, hl0 + 1, lane))
                G = lax.dot_general(Bg, dhnb, NT_DIMS, preferred_element_type=F32)
                dxd = decp * G
                sdec = _pair_sums(lane, xd * G)
                dB = dB + jnp.dot((xd * decp).astype(BF16), dhnb, preferred_element_type=F32)
                dh_ref[pi] = dhp
                for hh in range(2):
                    hmask = (lane >= SSD_HEAD_DIM) if hh else (lane < SSD_HEAD_DIM)
                    seg = _col(cs, hl0 + hh, lane) - cst_ref[hl0 + hh:hl0 + hh + 1, :]
                    Lm = jnp.exp(jnp.where(causal, seg, NEG))
                    M = CB * Lm
                    dyh = jnp.where(hmask, dy2, 0.0).astype(BF16)
                    dM = lax.dot_general(dyh, xdb, NT_DIMS, preferred_element_type=F32)
                    dxd = dxd + lax.dot_general(M.astype(BF16), dyh, TN_DIMS, preferred_element_type=F32)
                    Q = dM * M
                    dCB = dCB + dM * Lm
                    dech = _col(dec, hl0 + hh, lane)
                    dd = sdec[hh] * dech
                    end = dcd[hh] * cdh[hh] + jnp.sum(dd, axis=0, keepdims=True)
                    dcs_h = (sYo[hh] - dd + jnp.sum(Q, axis=1, keepdims=True)
                             - jnp.sum(Q.T, axis=1, keepdims=True) + jnp.where(is_last, end, 0.0))
                    dcs = jnp.where(lane == hl0 + hh, dcs_h, dcs)
                    dD = jnp.where(lane == hl0 + hh, jnp.sum(sD[hh], axis=0, keepdims=True), dD)
                sdt = _pair_sums(lane, dxd * x2)
                ddt = jnp.where(lane == hl0, sdt[0], jnp.where(lane == hl0 + 1, sdt[1], ddt))
                dxc_ref[:, pi * LANE:(pi + 1) * LANE] = dx2 + dxd * dtp
            dCBb = dCB.astype(BF16)
            dC = dC + jnp.dot(dCBb, Bg, preferred_element_type=F32)
            dB = dB + lax.dot_general(dCBb, Cg, TN_DIMS, preferred_element_type=F32)
            dxc_ref[:, SSD_B0 + g * SSD_STATE:SSD_B0 + (g + 1) * SSD_STATE] = dB
            dxc_ref[:, SSD_C0 + g * SSD_STATE:SSD_C0 + (g + 1) * SSD_STATE] = dC
        da = _tri_dot(jnp.where(c >= r, 1.0, 0.0).astype(BF16), dcs)
        ddtv = ddt + da * A
        ddt_ref[...] = jnp.where(hl, ddtv * _sigmoid(fv), 0.0)
        dal = jnp.sum(da * dtv, axis=0, keepdims=True) * A
        dp_ref[0:1, :] += dal
        dp_ref[1:2, :] += dD

    rev = lambda i: nc - 1 - i
    vec = pl.BlockSpec((1, LANE), lambda i: (0, 0))
    return pl.pallas_call(
        body, name=name, grid=(nc,),
        in_specs=[pl.BlockSpec((L, SSD_CH), lambda i: (rev(i), 0)), pl.BlockSpec((L, LANE), lambda i: (rev(i), col)),
                  vec, vec, vec,
                  pl.BlockSpec((1, SSD_PAIRS, LANE, SSD_STATE), lambda i: (rev(i), 0, 0, 0)),
                  pl.BlockSpec((L, SSD_X), lambda i: (rev(i), 0))],
        out_specs=[pl.BlockSpec((L, SSD_CH), lambda i: (rev(i), 0)), pl.BlockSpec((L, LANE), lambda i: (rev(i), 0)),
                   pl.BlockSpec((SUBLANE, LANE), lambda i: (0, 0))],
        out_shape=[jax.ShapeDtypeStruct((S, SSD_CH), F32), jax.ShapeDtypeStruct((S, LANE), F32),
                   jax.ShapeDtypeStruct((SUBLANE, LANE), F32)],
        scratch_shapes=[pltpu.VMEM((SSD_PAIRS, LANE, SSD_STATE), F32), pltpu.VMEM((LANE, L), F32)],
        compiler_params=_cparams("arbitrary"),
    )(xc, rest, dtb, alog, dskip, hs, dy)


def _gate_norm_fwd(y, rest, zcol, nw, *, name):
    S, W = y.shape
    ts = _row_tile(S)
    GW = W // SSD_GROUPS

    def body(y_ref, z_ref, nw_ref, o_ref):
        z = z_ref[...]
        t = y_ref[...] * (z * _sigmoid(z))
        for g in range(SSD_GROUPS):
            cols = slice(g * GW, (g + 1) * GW)
            tg = t[:, cols]
            rr = lax.rsqrt(jnp.mean(tg * tg, axis=-1, keepdims=True) + NORM_EPS)
            o_ref[:, cols] = (tg * rr * nw_ref[:, cols]).astype(BF16)

    row = pl.BlockSpec((ts, W), lambda i: (i, 0))
    return pl.pallas_call(
        body, name=name, grid=(S // ts,),
        in_specs=[row, pl.BlockSpec((ts, W), lambda i: (i, zcol)), pl.BlockSpec((1, W), lambda i: (0, 0))],
        out_specs=row,
        out_shape=jax.ShapeDtypeStruct((S, W), BF16),
        compiler_params=_cparams("parallel"),
    )(y, rest, nw)


def _gate_norm_bwd(y, rest, zcol, nw, do, *, name):
    S, W = y.shape
    ts = _row_tile(S)
    GW = W // SSD_GROUPS

    def body(y_ref, z_ref, nw_ref, do_ref, dy_ref, dz_ref, dnw_ref):
        z = z_ref[...]
        yv = y_ref[...]
        s, ds = _dsilu(z)
        sz = z * s
        t = yv * sz
        dov = do_ref[...]
        parts = []
        for g in range(SSD_GROUPS):
            cols = slice(g * GW, (g + 1) * GW)
            tg = t[:, cols]
            rr = lax.rsqrt(jnp.mean(tg * tg, axis=-1, keepdims=True) + NORM_EPS)
            n = tg * rr
            dog = dov[:, cols]
            dn = dog * nw_ref[:, cols]
            dt = rr * (dn - n * jnp.mean(dn * n, axis=-1, keepdims=True))
            dy_ref[:, cols] = dt * sz[:, cols]
            dz_ref[:, cols] = (dt * yv[:, cols] * ds[:, cols]).astype(BF16)
            parts.append(jnp.sum(dog * n, axis=0, keepdims=True))
        _acc_out(dnw_ref, jnp.concatenate(parts, axis=1), pl.program_id(0) == 0)

    row = pl.BlockSpec((ts, W), lambda i: (i, 0))
    vec = pl.BlockSpec((1, W), lambda i: (0, 0))
    return pl.pallas_call(
        body, name=name, grid=(S // ts,),
        in_specs=[row, pl.BlockSpec((ts, W), lambda i: (i, zcol)), vec, row],
        out_specs=[row, row, vec],
        out_shape=[jax.ShapeDtypeStruct((S, W), F32), jax.ShapeDtypeStruct((S, W), BF16),
                   jax.ShapeDtypeStruct((1, W), F32)],
        compiler_params=_cparams("arbitrary"),
    )(y, rest, nw, do)


N_DEV = 8
MESH = pl.DeviceIdType.MESH
ANY = pl.BlockSpec(memory_space=pl.ANY)


def _all_gather(xs, *, name):
    n = len(xs)

    def body(*refs):
        x_refs, o_refs = refs[:n], refs[n:2 * n]
        send_sems, recv_sems, local_sems = refs[2 * n:]
        px, py, pc = lax.axis_index("x"), lax.axis_index("y"), lax.axis_index("c")
        me, sibling = (px, py, pc), (px, py, 1 - pc)
        chips = [(1 - px, py), (px, 1 - py), (1 - px, 1 - py)]

        def copy(a, k, block, to, src=None):
            slot = o_refs[a].at[4 * block[0] + 2 * block[1] + block[2]]
            return pltpu.make_async_remote_copy(
                src_ref=slot if src is None else src, dst_ref=slot,
                send_sem=send_sems.at[a, k], recv_sem=recv_sems.at[a, k], device_id=to, device_id_type=MESH)

        mine = [pltpu.make_async_copy(x_refs[a], o_refs[a].at[4 * px + 2 * py + pc], local_sems.at[a]) for a in range(n)]
        for cp in mine:
            cp.start()
        first = []
        for a in range(n):
            first.append(copy(a, 0, me, sibling, src=x_refs[a]))
            first += [copy(a, 1 + j, me, (*chip, pc), src=x_refs[a]) for j, chip in enumerate(chips)]
        for cp in first:
            cp.start()
        passed = []
        for j, chip in enumerate(chips):
            for a in range(n):
                copy(a, 1 + j, (*chip, pc), me).wait_recv()
                fwd = copy(a, 4 + j, (*chip, pc), sibling)
                fwd.start()
                passed.append(fwd)
        for a in range(n):
            copy(a, 0, sibling, me).wait_recv()
            for j, chip in enumerate(chips):
                copy(a, 4 + j, (*chip, 1 - pc), me).wait_recv()
        for cp in first + passed:
            cp.wait_send()
        for cp in mine:
            cp.wait()

    return pl.pallas_call(
        body, name=name,
        out_shape=[jax.ShapeDtypeStruct((N_DEV,) + x.shape, x.dtype) for x in xs],
        in_specs=[ANY] * n, out_specs=[ANY] * n,
        scratch_shapes=[pltpu.SemaphoreType.DMA((n, N_DEV - 1)), pltpu.SemaphoreType.DMA((n, N_DEV - 1)),
                        pltpu.SemaphoreType.DMA((n,))],
    )(*xs)


def _exchange(sends, *, name):
    n = len(sends)

    def body(*refs):
        s_refs, r_refs = refs[:n], refs[n:2 * n]
        send_sems, recv_sems, local_sems = refs[2 * n:]
        px, py, pc = lax.axis_index("x"), lax.axis_index("y"), lax.axis_index("c")
        me = 4 * px + 2 * py + pc
        local = [pltpu.make_async_copy(s_refs[a].at[me], r_refs[a].at[me], local_sems.at[a]) for a in range(n)]
        for cp in local:
            cp.start()
        sent, landing = [], []
        for k in range(1, N_DEV):
            qx = 1 - px if k & 4 else px
            qy = 1 - py if k & 2 else py
            qc = 1 - pc if k & 1 else pc
            peer = 4 * qx + 2 * qy + qc
            for a in range(n):
                def rdma(dst_slot):
                    return pltpu.make_async_remote_copy(
                        src_ref=s_refs[a].at[peer], dst_ref=r_refs[a].at[dst_slot],
                        send_sem=send_sems.at[a, k - 1], recv_sem=recv_sems.at[a, k - 1],
                        device_id=(qx, qy, qc), device_id_type=MESH)
                cp = rdma(me)
                cp.start()
                sent.append(cp)
                landing.append(rdma(peer))
        for cp in landing:
            cp.wait_recv()
        for cp in sent:
            cp.wait_send()
        for cp in local:
            cp.wait()

    return pl.pallas_call(
        body, name=name,
        out_shape=[jax.ShapeDtypeStruct(s.shape, s.dtype) for s in sends],
        in_specs=[ANY] * n, out_specs=[ANY] * n,
        scratch_shapes=[pltpu.SemaphoreType.DMA((n, N_DEV - 1)), pltpu.SemaphoreType.DMA((n, N_DEV - 1)),
                        pltpu.SemaphoreType.DMA((n,))],
    )(*sends)


IN_POOL, IN_QKV, IN_F, IN_Z, IN_XBC, IN_DT, IN_GATE, IN_TOTAL = 0, 512, 2048, 2056, 3080, 4616, 4632, 7704
QKV_W = IN_F - IN_QKV
REST_Z_BLK, REST_POOL_BLK, REST_XBC0, REST_FDT_BLK = 3, 8, 4608, 48
REST_FDT0 = REST_FDT_BLK * LANE
REST_USED = REST_FDT0 + (IN_Z - IN_F) + (IN_GATE - IN_DT)
REST_W = REST_FDT0 + LANE


def _w_in_runs(shard):
    segs = [(0, IN_QKV, IN_F, 0), (1, IN_GATE, IN_TOTAL, 0), (1, IN_Z, IN_XBC, 3072), (1, IN_POOL, IN_QKV, 4096),
            (1, IN_XBC, IN_DT, REST_XBC0), (1, IN_F, IN_Z, REST_FDT0), (1, IN_DT, IN_GATE, REST_FDT0 + IN_Z - IN_F)]
    runs = []
    for dst, t0, t1, d0 in segs:
        for j in range(N_DEV):
            lo, hi = max(t0, shard * j), min(t1, shard * (j + 1))
            if lo < hi:
                runs.append((dst, d0 + lo - t0, j, lo - shard * j, hi - lo))
    return runs


def _repack_w_in(g, shard, *, name):
    _, L, K, P = g.shape
    tr = 256
    runs = _w_in_runs(shard)

    def body(g_ref, q_ref, r_ref):
        r_ref[:, REST_USED:] = jnp.zeros((tr, REST_W - REST_USED), g_ref.dtype)
        for dst, b, j, a, n in runs:
            (q_ref, r_ref)[dst][:, b:b + n] = g_ref[j, :, a:a + n]

    return pl.pallas_call(
        body, name=name, grid=(L, K // tr),
        in_specs=[pl.BlockSpec((N_DEV, None, tr, P), lambda l, i: (0, l, i, 0))],
        out_specs=[pl.BlockSpec((None, tr, QKV_W), lambda l, i: (l, i, 0)),
                   pl.BlockSpec((None, tr, REST_W), lambda l, i: (l, i, 0))],
        out_shape=[jax.ShapeDtypeStruct((L, K, QKV_W), g.dtype), jax.ShapeDtypeStruct((L, K, REST_W), g.dtype)],
        compiler_params=_cparams("parallel", "parallel"),
    )(g)


def _held_index(l, li, i, n):
    return jnp.where(l == li, i, jnp.where(l > li, n - 1, 0))


def _repack_dw_in(dwq, dwr, shard, *, name):
    L = len(dwq)
    K = dwq[0].shape[0]
    tr = 128
    nt = K // tr
    runs = _w_in_runs(shard)

    def body(*refs):
        srcs, o_ref = refs[:2 * L], refs[2 * L]
        l = pl.program_id(0)
        for li in range(L):
            @pl.when(l == li)
            def _():
                for dst, b, j, a, n in runs:
                    o_ref[j, :, a:a + n] = srcs[2 * li + dst][:, b:b + n]

    in_specs = []
    for li in range(L):
        hold = functools.partial(lambda li, l, i: (_held_index(l, li, i, nt), 0), li)
        in_specs += [pl.BlockSpec((tr, QKV_W), hold), pl.BlockSpec((tr, REST_W), hold)]
    args = [x for pair in zip(dwq, dwr) for x in pair]
    return pl.pallas_call(
        body, name=name, grid=(L, nt),
        in_specs=in_specs,
        out_specs=pl.BlockSpec((N_DEV, None, tr, shard), lambda l, i: (0, l, i, 0)),
        out_shape=jax.ShapeDtypeStruct((N_DEV, L, K, shard), dwq[0].dtype),
        compiler_params=_cparams("arbitrary", "arbitrary"),
    )(*args)


FLAT_W = 1024
ADAM_ROWS = 128


def _adam_math(w, g, m, v):
    m2 = ADAM_B1 * m + (1.0 - ADAM_B1) * g
    v2 = ADAM_B2 * v + (1.0 - ADAM_B2) * (g * g)
    m_hat = m2 / (1.0 - ADAM_B1 ** ADAM_STEP)
    v_hat = v2 / (1.0 - ADAM_B2 ** ADAM_STEP)
    delta = -ADAM_LR * (m_hat / (jnp.sqrt(v_hat) + ADAM_EPS) + ADAM_WD * w)
    return delta, m2, v2


def _row_div(R, align):
    best = R
    for t in range(align, min(R, ADAM_ROWS) + 1, align):
        if R % t == 0:
            best = t
    return best


def _sum8(ref_of):
    g = ref_of(0).astype(F32)
    for s in range(1, N_DEV):
        g = g + ref_of(s).astype(F32)
    return g


def _sum_adam(recvs, w, m, v, *, name):
    L, R, C = w.shape
    single = not isinstance(recvs, (list, tuple))
    recvs = [recvs] if single else list(recvs)
    tr = _row_div(R, SUBLANE * (4 // recvs[0].dtype.itemsize))
    nt = R // tr
    nr = len(recvs)

    def body(*refs):
        r_refs = refs[:nr]
        w_ref, m_ref, v_ref, g_ref, d_ref, m2_ref, v2_ref = refs[nr:]

        def run(ref_of):
            g = _sum8(ref_of)
            delta, m2, v2 = _adam_math(w_ref[...], g, m_ref[...], v_ref[...])
            g_ref[...] = g
            d_ref[...] = delta
            m2_ref[...] = m2
            v2_ref[...] = v2

        if single:
            run(lambda s: r_refs[0][s])
        else:
            l = pl.program_id(0)
            for li in range(L):
                @pl.when(l == li)
                def _():
                    run(lambda s: r_refs[li][s])

    if single:
        r_specs = [pl.BlockSpec((N_DEV, None, tr, C), lambda l, i: (0, l, i, 0))]
    else:
        r_specs = [pl.BlockSpec((N_DEV, tr, C), functools.partial(lambda li, l, i: (0, _held_index(l, li, i, nt), 0), li))
                   for li in range(L)]
    row = pl.BlockSpec((None, tr, C), lambda l, i: (l, i, 0))
    return pl.pallas_call(
        body, name=name, grid=(L, nt),
        in_specs=r_specs + [row, row, row],
        out_specs=[row] * 4,
        out_shape=[jax.ShapeDtypeStruct((L, R, C), F32)] * 4,
        compiler_params=_cparams("arbitrary", "arbitrary"),
    )(*recvs, w, m, v)


def _sum_parts(parts, *, name):
    _, R, C = parts.shape

    def body(p_ref, o_ref):
        o_ref[...] = _sum8(lambda s: p_ref[s])

    return pl.pallas_call(
        body, name=name, out_shape=jax.ShapeDtypeStruct((R, C), F32),
        in_specs=[pl.BlockSpec(memory_space=pltpu.VMEM)], out_specs=pl.BlockSpec(memory_space=pltpu.VMEM),
    )(parts)


def _adam(g, w, m, v, *, name):
    def body(g_ref, w_ref, m_ref, v_ref, d_ref, m2_ref, v2_ref):
        delta, m2, v2 = _adam_math(w_ref[...], g_ref[...], m_ref[...], v_ref[...])
        d_ref[...] = delta
        m2_ref[...] = m2
        v2_ref[...] = v2

    whole = pl.BlockSpec(memory_space=pltpu.VMEM)
    return pl.pallas_call(
        body, name=name, in_specs=[whole] * 4, out_specs=[whole] * 3,
        out_shape=[jax.ShapeDtypeStruct(g.shape, F32)] * 3,
    )(g, w, m, v)


SHARDED = ("w_in", "ffn_up", "p_pool", "p_attn", "p_ssd", "w_out", "ffn_down", "ssd_conv_w", "ffn_conv_w")
REPLICATED = ("norm_mix", "pool_mix", "pool_scale", "f_bias", "ssd_conv_b", "ssd_dt_bias", "ssd_a_log", "ssd_d",
              "ssd_norm", "norm_ffn", "ffn_conv_b", "norm_final")
WEIGHTS = ("norm_mix", "w_in", "pool_mix", "pool_scale", "f_bias", "ssd_conv_w", "ssd_conv_b", "ssd_dt_bias",
           "ssd_a_log", "ssd_d", "ssd_norm", "p_pool", "p_attn", "p_ssd", "w_out", "norm_ffn", "ffn_up",
           "ffn_conv_w", "ffn_conv_b", "ffn_down", "norm_final")


def _round_up(n, k):
    return -(-n // k) * k


def _pad_last(x, width):
    return jnp.pad(x, [(0, 0)] * (x.ndim - 1) + [(0, width - x.shape[-1])])


def _flat_rows(parts, rows):
    flat = jnp.concatenate([p.reshape(-1) for p in parts])
    return jnp.pad(flat, (0, rows * FLAT_W - flat.shape[0])).reshape(rows, FLAT_W)


def _lanes(v, lane0):
    return jnp.pad(v, (lane0, LANE - lane0 - v.shape[0]))[None]


def _cols_by_device(g):
    _, L, R, c = g.shape
    return jnp.moveaxis(g, 0, 2).reshape(L, R, N_DEV * c)


def _rows_by_device(g):
    _, L, r, C = g.shape
    return jnp.moveaxis(g, 0, 1).reshape(L, N_DEV * r, C)


def _layer_fwd(x, P, l):
    nm = lambda s: f"l{l}_{s}"
    u = _rms_fwd(x, P["nm"], name=nm("rms_mix"))
    qkv = _mm(u, P["w_qkv"], out_dtype=BF16, name=nm("proj_qkv"))
    rest = _mm(u, P["w_rest"], name=nm("proj_rest"))
    d, ya0 = _pool_fwd(rest, REST_POOL_BLK, P["mix"], P["pscale"], name=nm("pool"))
    ya = _mm(ya0, P["p_pool"], name=nm("p_pool"))
    ccol = _fox_pre(rest, REST_FDT_BLK, P["fb"], name=nm("fox_pre"))
    o, lrow = _attn_fwd(qkv, ccol, name=nm("attn"))
    yb = _mm(o, P["p_attn"], name=nm("p_attn"))
    xc = _conv_silu_fwd(rest, REST_XBC0, SSD_CH, P["cw"], P["cb"], name=nm("ssd_conv"))
    y, hs = _ssd_fwd(xc, rest, REST_FDT_BLK, P["dtb"], P["alog"], P["dsk"], name=nm("ssd"))
    yc0 = _gate_norm_fwd(y, rest, REST_Z_BLK, P["snw"], name=nm("ssd_norm"))
    yc = _mm(yc0, P["p_ssd"], name=nm("p_ssd"))
    merged = _merge_fwd(rest, ya, yb, yc, name=nm("merge"))
    x1 = _mm(merged, P["w_out"], add=x, name=nm("w_out"))
    u2 = _rms_fwd(x1, P["nf"], name=nm("rms_ffn"))
    h0 = _mm_up_fwd(u2, P["g_up"], l, name=nm("ffn_up"))
    a = _ffn_act_fwd(h0, P["fcw"], P["fcb"], name=nm("ffn_act"))
    x2 = _mm(a, P["w_down"], add=x1, name=nm("ffn_down"))
    saved = dict(x=x, u=u, qkv=qkv, rest=rest, d=d, ya0=ya0, ya=ya, ccol=ccol, o=o, lrow=lrow, yb=yb,
                 xc=xc, y=y, hs=hs, yc0=yc0, yc=yc, merged=merged, x1=x1, u2=u2, h0=h0, a=a)
    return x2, saved


def _layer_bwd(g, gb, sv, P, l):
    nm = lambda s: f"l{l}_{s}_bwd"
    rest = sv["rest"]
    G = {}
    up_blocks, up_w, up_pad = P["up_blocks"]
    half = up_blocks // 2
    da = _mm(gb, P["w_down"], tb=True, name=nm("ffn_down_dx"))
    dwd = _mm(sv["a"], gb, ta=True, out_dtype=GRAD_WIRE, name=nm("ffn_down_dw"))
    G["ffn_down"] = dwd.reshape(half, up_pad, -1)[:, :up_w].reshape(N_DEV, -1, dwd.shape[1])
    dh, dfcw, dfcb = _ffn_act_bwd(sv["h0"], P["fcw"], P["fcb"], da, name=nm("ffn_act"))
    du2 = _mm_up_dx(dh, P["g_up"], l, name=nm("ffn_up_dx"))
    G["ffn_up"] = _mm_up_dw(sv["u2"], dh, up_blocks, up_w, name=nm("ffn_up_dw"))
    taps = dfcw.shape[1]
    G["ffn_conv_w"] = jnp.moveaxis(dfcw.reshape(2, taps, half, up_pad)[..., :up_w], 2, 1).reshape(N_DEV, taps, up_w)
    G["ffn_conv_b"] = dfcb.reshape(2, half, up_pad)[..., :up_w].reshape(-1)
    dx1, dx1b, dnf = _rms_bwd(sv["x1"], P["nf"], du2, g, name=nm("rms_ffn"))
    G["norm_ffn"] = dnf[0]
    dmerged = _mm(dx1b, P["w_out"], tb=True, name=nm("w_out_dx"))
    dwo = _mm(sv["merged"], dx1b, ta=True, out_dtype=GRAD_WIRE, name=nm("w_out_dw"))
    G["w_out"] = dwo.reshape(N_DEV, -1, dwo.shape[1])
    dgl, dya, dyb, dyc = _merge_bwd(rest, sv["ya"], sv["yb"], sv["yc"], dmerged, name=nm("merge"))
    G["p_pool"] = _mm_tn_blocks(sv["ya0"], dya, N_DEV, name=nm("p_pool_dw"))
    dya0 = _mm(dya, P["p_pool"], tb=True, name=nm("p_pool_dx"))
    G["p_attn"] = _mm_tn_blocks(sv["o"], dyb, N_DEV, name=nm("p_attn_dw"))
    do = _mm(dyb, P["p_attn"], tb=True, out_dtype=BF16, name=nm("p_attn_dx"))
    dps = _mm(sv["yc0"], dyc, ta=True, out_dtype=GRAD_WIRE, name=nm("p_ssd_dw"))
    G["p_ssd"] = dps.reshape(N_DEV, -1, dps.shape[1])
    dyc0 = _mm(dyc, P["p_ssd"], tb=True, name=nm("p_ssd_dx"))
    dpv, dmix, dpsc = _pool_bwd(dya0, sv["d"], P["mix"], P["pscale"], name=nm("pool"))
    G["pool_mix"], G["pool_scale"] = dmix, dpsc[0]
    dq, dk, dv, dcq, dck = _attn_bwd(sv["qkv"], sv["o"], do, sv["lrow"], sv["ccol"], name=nm("attn"))
    dy, dz, dsnw = _gate_norm_bwd(sv["y"], rest, REST_Z_BLK, P["snw"], dyc0, name=nm("ssd_norm"))
    G["ssd_norm"] = dsnw[0]
    dxc, ddt, dpar = _ssd_bwd(sv["xc"], rest, REST_FDT_BLK, P["dtb"], P["alog"], P["dsk"], sv["hs"], dy, name=nm("ssd"))
    G["ssd_a_log"] = dpar[0, DT_LANE0:DT_LANE0 + SSD_HEADS]
    G["ssd_d"] = dpar[1, DT_LANE0:DT_LANE0 + SSD_HEADS]
    dxbc, dcw, dcb = _conv_silu_bwd(rest, REST_XBC0, SSD_CH, P["cw"], P["cb"], dxc, name=nm("ssd_conv"))
    G["ssd_conv_w"] = jnp.moveaxis(dcw.reshape(dcw.shape[0], N_DEV, -1), 1, 0)
    G["ssd_conv_b"] = dcb[0]
    dfdt, dfdtb = _fox_post(dcq, dck, rest, REST_FDT_BLK, P["fb"], ddt, name=nm("fox_post"))
    G["f_bias"] = dfdtb[0, :ATTN_HEADS]
    G["ssd_dt_bias"] = dfdtb[0, DT_LANE0:DT_LANE0 + SSD_HEADS]
    dqkv = jnp.concatenate([dq, dk, dv], axis=1)
    drest = jnp.concatenate([dgl, dz, dpv, dxbc, dfdt], axis=1)
    du = _mm(dqkv, P["w_qkv"], tb=True, name=nm("proj_qkv_dx"))
    du = _mm(drest, P["w_rest"], tb=True, add=du, name=nm("proj_rest_dx"))
    G["w_qkv"] = _mm(sv["u"], dqkv, ta=True, out_dtype=GRAD_WIRE, name=nm("proj_qkv_dw"))
    G["w_rest"] = _mm(sv["u"], drest, ta=True, out_dtype=GRAD_WIRE, name=nm("proj_rest_dw"))
    dx, dxb, dnm = _rms_bwd(sv["x"], P["nm"], du, dx1, name=nm("rms_mix"))
    G["norm_mix"] = dnm[0]
    return dx, dxb, G


def kernel(x, norm_mix, w_in, pool_mix, pool_scale, f_bias, ssd_conv_w, ssd_conv_b, ssd_dt_bias, ssd_a_log, ssd_d, ssd_norm, p_pool, p_attn, p_ssd, w_out, norm_ffn, ffn_up, ffn_conv_w, ffn_conv_b, ffn_down, norm_final, loss_target, m_norm_mix, m_w_in, m_pool_mix, m_pool_scale, m_f_bias, m_ssd_conv_w, m_ssd_conv_b, m_ssd_dt_bias, m_ssd_a_log, m_ssd_d, m_ssd_norm, m_p_pool, m_p_attn, m_p_ssd, m_w_out, m_norm_ffn, m_ffn_up, m_ffn_conv_w, m_ffn_conv_b, m_ffn_down, m_norm_final, v_norm_mix, v_w_in, v_pool_mix, v_pool_scale, v_f_bias, v_ssd_conv_w, v_ssd_conv_b, v_ssd_dt_bias, v_ssd_a_log, v_ssd_d, v_ssd_norm, v_p_pool, v_p_attn, v_p_ssd, v_w_out, v_norm_ffn, v_ffn_up, v_ffn_conv_w, v_ffn_conv_b, v_ffn_down, v_norm_final):
    args = dict(locals())
    W = {n: args[n] for n in WEIGHTS}
    M = {n: args["m_" + n] for n in WEIGHTS}
    V = {n: args["v_" + n] for n in WEIGHTS}
    L = norm_mix.shape[0]
    shard = w_in.shape[-1]
    up_w = ffn_up.shape[-1]
    up_pad = _round_up(up_w, LANE)

    bf = lambda n: W[n].astype(BF16)
    g_win, g_up, g_pp, g_pa, g_ps, g_wo, g_dn, g_cw, g_fcw = _all_gather(
        [_pad_last(bf("w_in"), _round_up(shard, LANE)), _pad_last(bf("ffn_up"), up_pad), bf("p_pool"), bf("p_attn"),
         bf("p_ssd"), bf("w_out"), bf("ffn_down"), ssd_conv_w, ffn_conv_w], name="gather_weights")
    w_qkv, w_rest = _repack_w_in(g_win, shard, name="repack_w_in")
    p_pool_f, p_attn_f = _cols_by_device(g_pp), _cols_by_device(g_pa)
    p_ssd_f, w_out_f = _rows_by_device(g_ps), _rows_by_device(g_wo)
    half = N_DEV // 2
    w_down_f = _rows_by_device(g_dn)
    w_down_p = jnp.pad(w_down_f.reshape(L, half, up_w, -1), ((0, 0), (0, 0), (0, up_pad - up_w), (0, 0)))
    w_down_p = w_down_p.reshape(L, half * up_pad, -1)
    cw_f = _cols_by_device(g_cw)
    fcw_p = _cols_by_device(_pad_last(g_fcw, up_pad))
    fcb_p = _pad_last(ffn_conv_b.reshape(L, N_DEV, up_w), up_pad).reshape(L, 1, N_DEV * up_pad)

    params = [dict(
        nm=norm_mix[l][None], w_qkv=w_qkv[l], w_rest=w_rest[l], mix=pool_mix[l].astype(BF16),
        pscale=pool_scale[l][None], fb=_lanes(f_bias[l], 0), cw=cw_f[l], cb=ssd_conv_b[l][None],
        dtb=_lanes(ssd_dt_bias[l], DT_LANE0), alog=_lanes(ssd_a_log[l], DT_LANE0), dsk=_lanes(ssd_d[l], DT_LANE0),
        snw=ssd_norm[l][None], p_pool=p_pool_f[l], p_attn=p_attn_f[l], p_ssd=p_ssd_f[l], w_out=w_out_f[l],
        nf=norm_ffn[l][None], g_up=g_up, up_blocks=(N_DEV, up_w, up_pad), fcw=fcw_p[l], fcb=fcb_p[l],
        w_down=w_down_p[l]) for l in range(L)]
    h, saves = x[0], []
    for l in range(L):
        h, sv = _layer_fwd(h, params[l], l)
        saves.append(sv)
    loss_part, g, gb, dnfin = _loss_head(h, norm_final[None], loss_target[0], name="loss_head")
    G = [None] * L
    for l in reversed(range(L)):
        g, gb, G[l] = _layer_bwd(g, gb, saves[l], params[l], l)

    per_layer = [n for n in SHARDED if n != "w_in"]
    rep_grads = {n: jnp.stack([G[l][n] for l in range(L)]) for n in REPLICATED if n != "norm_final"}
    rep_grads["norm_final"] = dnfin[0]
    rep_size = sum(W[n].size for n in REPLICATED)
    rep_rows = _round_up(-(-rep_size // (N_DEV * FLAT_W)), SUBLANE)
    rep_send = _flat_rows([rep_grads[n] for n in REPLICATED], N_DEV * rep_rows).reshape(N_DEV, rep_rows, FLAT_W)
    dw_in = _repack_dw_in([G[l]["w_qkv"] for l in range(L)], [G[l]["w_rest"] for l in range(L)], shard,
                          name="repack_dw_in")
    sends = [dw_in] + [G[l][n] for n in per_layer for l in range(L)] + [rep_send]
    recvs = _exchange(sends, name="exchange_grads")

    out = {}

    def finish(n, recv):
        shape = W[n].shape
        as3 = lambda a: a.reshape(L, -1, shape[-1])
        res = _sum_adam(recv, as3(W[n]), as3(M[n]), as3(V[n]), name="sum_adam_" + n)
        for key, arr in zip(("grad_", "delta_", "new_m_", "new_v_"), res):
            out[key + n] = arr.reshape(shape)

    finish("w_in", recvs[0])
    for i, n in enumerate(per_layer):
        finish(n, recvs[1 + L * i:1 + L * (i + 1)])

    rep_sum = _sum_parts(recvs[-1], name="sum_replicated")
    rep_g = _all_gather([rep_sum], name="gather_replicated_grads")[0].reshape(N_DEV * rep_rows, FLAT_W)
    flat_rep = lambda D: _flat_rows([D[n] for n in REPLICATED], N_DEV * rep_rows)
    d_rp, m_rp, v_rp = _adam(rep_g, flat_rep(W), flat_rep(M), flat_rep(V), name="adam_replicated")
    off = 0
    for n in REPLICATED:
        size = W[n].size
        for key, arr in (("grad_", rep_g), ("delta_", d_rp), ("new_m_", m_rp), ("new_v_", v_rp)):
            out[key + n] = arr.reshape(-1)[off:off + size].reshape(W[n].shape)
        off += size

    loss = lax.psum(loss_part[0, 0], ("x", "y", "c"))
    return (loss, g[None], *[out["grad_" + n] for n in WEIGHTS], *[out["delta_" + n] for n in WEIGHTS],
            *[out["new_m_" + n] for n in WEIGHTS], *[out["new_v_" + n] for n in WEIGHTS])
```

```python
import functools

import jax
import jax.numpy as jnp
from jax import lax
from jax.experimental import pallas as pl
from jax.experimental.pallas import tpu as pltpu

F32 = jnp.float32
BF16 = jnp.bfloat16

LANE = 128
SUBLANE = 8
VMEM_LIMIT = 56 * 1024 * 1024

NORM_EPS = 1e-6
POOL_WINDOWS = (2, 4, 8, 16)
POOL_GROUP_DIM = 128
ATTN_HEADS = 8
ATTN_HEAD_DIM = 64
SSD_HEAD_DIM = 64
SSD_HEADS = 16
SSD_GROUPS = 2
SSD_STATE = 128
SSD_CHUNK = 128

ADAM_LR = 0.001
ADAM_B1 = 0.9
ADAM_B2 = 0.999
ADAM_EPS = 1e-08
ADAM_WD = 0.01
ADAM_STEP = 10


def _cparams(*sem):
    return pltpu.CompilerParams(dimension_semantics=tuple(sem), vmem_limit_bytes=VMEM_LIMIT)


def _tile(n, pref):
    if n <= pref:
        return n
    assert n % LANE == 0, n
    q = n // LANE
    best = 1
    for d in range(1, pref // LANE + 1):
        if q % d == 0:
            best = d
    return best * LANE


GRAD_WIRE = BF16


def _mm_call(args, in_specs, out_spec, out_shape, grid, *, ta, tb, nk, acc_shape, has_add, keep, name):
    dims = (((0 if ta else 1,), (1 if tb else 0,)), ((), ()))

    def body(*refs):
        a_ref, b_ref = refs[0], refs[1]
        o_ref = refs[2 + has_add]
        d = lax.dot_general(a_ref[...], b_ref[...], dims, preferred_element_type=F32)

        def finish(r):
            if has_add:
                r = r + refs[2][...]
            if keep is not None:
                r = r[:, :keep]
            o_ref[...] = r.astype(o_ref.dtype)

        if nk == 1:
            finish(d)
            return
        acc_ref = refs[-1]
        k = pl.program_id(2)

        @pl.when(k == 0)
        def _():
            acc_ref[...] = d

        @pl.when(jnp.logical_and(k > 0, k < nk - 1))
        def _():
            acc_ref[...] += d

        @pl.when(k == nk - 1)
        def _():
            finish(acc_ref[...] + d)

    return pl.pallas_call(
        body, name=name, grid=grid, in_specs=in_specs, out_specs=out_spec, out_shape=out_shape,
        scratch_shapes=[pltpu.VMEM(acc_shape, F32)] if nk > 1 else [],
        compiler_params=_cparams("parallel", "parallel", "arbitrary"),
    )(*args)


MM_VMEM_BUDGET = 40 * 1024 * 1024
MM_MAX_ROWS = 2048


def _mm_rows(M, tn, tk, nk, out_bytes, has_add):
    best = None
    for tm in range(LANE, min(M, MM_MAX_ROWS) + 1, LANE):
        if M % tm:
            continue
        blocks = 2 * (tm * tk * 2 + tk * tn * 2 + tm * tn * out_bytes + (tm * tn * 4 if has_add else 0))
        blocks += tm * tn * 4 * (2 if nk > 1 else 1)
        if blocks <= MM_VMEM_BUDGET:
            best = tm
    return M if best is None else best


def _mm(a, b, *, ta=False, tb=False, add=None, out_dtype=F32, name):
    assert a.dtype == BF16 and b.dtype == BF16, (a.dtype, b.dtype)
    if ta:
        K, M = a.shape
    else:
        M, K = a.shape
    if tb:
        N, K2 = b.shape
    else:
        K2, N = b.shape
    assert K == K2, (a.shape, b.shape, ta, tb)
    tn, tk = _tile(N, 1024), _tile(K, 2048)
    tm = _mm_rows(M, tn, tk, K // tk, jnp.dtype(out_dtype).itemsize, add is not None)
    a_spec =pl.BlockSpec((tk, tm), lambda i, j, k: (k, i)) if ta else pl.BlockSpec((tm, tk), lambda i, j, k: (i, k))
    b_spec = pl.BlockSpec((tn, tk), lambda i, j, k: (j, k)) if tb else pl.BlockSpec((tk, tn), lambda i, j, k: (k, j))
    in_specs, args = [a_spec, b_spec], [a, b]
    if add is not None:
        in_specs.append(pl.BlockSpec((tm, tn), lambda i, j, k: (i, j)))
        args.append(add)
    return _mm_call(args, in_specs, pl.BlockSpec((tm, tn), lambda i, j, k: (i, j)),
                    jax.ShapeDtypeStruct((M, N), out_dtype), (M // tm, N // tn, K // tk),
                    ta=ta, tb=tb, nk=K // tk, acc_shape=(tm, tn), has_add=add is not None, keep=None, name=name)


def _mm_tn_blocks(a, b, nblk, *, name):
    K, M = a.shape
    N = b.shape[1]
    bw = N // nblk
    tk = _tile(K, 2048)
    tm = _mm_rows(M, bw, tk, K // tk, jnp.dtype(GRAD_WIRE).itemsize, False)
    return _mm_call([a, b],
                    [pl.BlockSpec((tk, tm), lambda i, j, k: (k, i)), pl.BlockSpec((tk, bw), lambda i, j, k: (k, j))],
                    pl.BlockSpec((None, tm, bw), lambda i, j, k: (j, i, 0)),
                    jax.ShapeDtypeStruct((nblk, M, bw), GRAD_WIRE), (M // tm, nblk, K // tk),
                    ta=True, tb=False, nk=K // tk, acc_shape=(tm, bw), has_add=False, keep=None, name=name)


def _half_of(b, half):
    hi = jnp.where(b >= half, 1, 0)
    return hi, b - half * hi


def _mm_up_fwd(u, g_up, l, *, name):
    S, K = u.shape
    nblk, _, _, bw = g_up.shape
    tm = _mm_rows(S, bw, K, 1, 4, False)
    return _mm_call([u, g_up],
                    [pl.BlockSpec((tm, K), lambda i, j, k: (i, 0)),
                     pl.BlockSpec((None, None, K, bw), lambda i, j, k: (j, l, 0, 0))],
                    pl.BlockSpec((tm, bw), lambda i, j, k: (i, j)),
                    jax.ShapeDtypeStruct((S, nblk * bw), F32), (S // tm, nblk, 1),
                    ta=False, tb=False, nk=1, acc_shape=(tm, bw), has_add=False, keep=None, name=name)


def _mm_up_dx(dh, g_up, l, *, name):
    _, S, _ = dh.shape
    nblk, _, N, bw = g_up.shape
    half = nblk // 2
    tm = _mm_rows(S, N, bw, nblk, 4, False)
    return _mm_call([dh, g_up],
                    [pl.BlockSpec((None, tm, bw), lambda i, j, k: (_half_of(k, half)[0], i, _half_of(k, half)[1])),
                     pl.BlockSpec((None, None, N, bw), lambda i, j, k: (k, l, 0, 0))],
                    pl.BlockSpec((tm, N), lambda i, j, k: (i, 0)),
                    jax.ShapeDtypeStruct((S, N), F32), (S // tm, 1, nblk),
                    ta=False, tb=True, nk=nblk, acc_shape=(tm, N), has_add=False, keep=None, name=name)


def _mm_up_dw(u, dh, nblk, keep, *, name):
    S, M = u.shape
    half = nblk // 2
    bw = dh.shape[2] // half
    tk = _tile(S, 2048)
    tm = _mm_rows(M, bw, tk, S // tk, jnp.dtype(GRAD_WIRE).itemsize, False)
    return _mm_call([u, dh],
                    [pl.BlockSpec((tk, tm), lambda i, j, k: (k, i)),
                     pl.BlockSpec((None, tk, bw), lambda i, j, k: (_half_of(j, half)[0], k, _half_of(j, half)[1]))],
                    pl.BlockSpec((None, tm, keep), lambda i, j, k: (j, i, 0)),
                    jax.ShapeDtypeStruct((nblk, M, keep), GRAD_WIRE), (M // tm, nblk, S // tk),
                    ta=True, tb=False, nk=S // tk, acc_shape=(tm, bw), has_add=False, keep=keep, name=name)


ROW_TILE = 512
HALO = 8
POOL_HALO = 16


def _row_tile(S):
    return min(ROW_TILE, S)


def _acc_out(ref, val, first):
    @pl.when(first)
    def _():
        ref[...] = val

    @pl.when(jnp.logical_not(first))
    def _():
        ref[...] += val


def _rms_fwd(x, w, *, name):
    S, D = x.shape
    ts = _row_tile(S)

    def body(x_ref, w_ref, u_ref):
        xv = x_ref[...]
        r = lax.rsqrt(jnp.mean(xv * xv, axis=-1, keepdims=True) + NORM_EPS)
        u_ref[...] = (xv * r * w_ref[...]).astype(BF16)

    return pl.pallas_call(
        body, name=name, grid=(S // ts,),
        in_specs=[pl.BlockSpec((ts, D), lambda i: (i, 0)), pl.BlockSpec((1, D), lambda i: (0, 0))],
        out_specs=pl.BlockSpec((ts, D), lambda i: (i, 0)),
        out_shape=jax.ShapeDtypeStruct((S, D), BF16),
        compiler_params=_cparams("parallel"),
    )(x, w)


def _rms_bwd(x, w, du, g, *, name):
    S, D = x.shape
    ts = _row_tile(S)

    def body(x_ref, w_ref, du_ref, g_ref, dx_ref, dxb_ref, dw_ref):
        xv = x_ref[...]
        r = lax.rsqrt(jnp.mean(xv * xv, axis=-1, keepdims=True) + NORM_EPS)
        n = xv * r
        duv = du_ref[...]
        dn = duv * w_ref[...]
        dx = g_ref[...] + r * (dn - n * jnp.mean(dn * n, axis=-1, keepdims=True))
        dx_ref[...] = dx
        dxb_ref[...] = dx.astype(BF16)
        _acc_out(dw_ref, jnp.sum(duv * n, axis=0, keepdims=True), pl.program_id(0) == 0)

    row = pl.BlockSpec((ts, D), lambda i: (i, 0))
    vec = pl.BlockSpec((1, D), lambda i: (0, 0))
    return pl.pallas_call(
        body, name=name, grid=(S // ts,),
        in_specs=[row, vec, row, row],
        out_specs=[row, row, vec],
        out_shape=[jax.ShapeDtypeStruct((S, D), F32), jax.ShapeDtypeStruct((S, D), BF16),
                   jax.ShapeDtypeStruct((1, D), F32)],
        compiler_params=_cparams("arbitrary"),
    )(x, w, du, g)


def _loss_head(x, w, target, *, name):
    S, D = x.shape
    ts = _row_tile(S)

    def body(x_ref, w_ref, t_ref, loss_ref, dx_ref, dxb_ref, dw_ref):
        xv = x_ref[...]
        wv = w_ref[...]
        r = lax.rsqrt(jnp.mean(xv * xv, axis=-1, keepdims=True) + NORM_EPS)
        n = xv * r
        err = n * wv - t_ref[...]
        part = 0.5 * jnp.sum(jnp.mean(err * err, axis=-1, keepdims=True), axis=0, keepdims=True)
        dy = err * (1.0 / D)
        dn = dy * wv
        dx = r * (dn - n * jnp.mean(dn * n, axis=-1, keepdims=True))
        dx_ref[...] = dx
        dxb_ref[...] = dx.astype(BF16)
        first = pl.program_id(0) == 0
        _acc_out(dw_ref, jnp.sum(dy * n, axis=0, keepdims=True), first)
        _acc_out(loss_ref, jnp.broadcast_to(part, loss_ref.shape), first)

    row = pl.BlockSpec((ts, D), lambda i: (i, 0))
    vec = pl.BlockSpec((1, D), lambda i: (0, 0))
    return pl.pallas_call(
        body, name=name, grid=(S // ts,),
        in_specs=[row, vec, row],
        out_specs=[pl.BlockSpec((SUBLANE, LANE), lambda i: (0, 0)), row, row, vec],
        out_shape=[jax.ShapeDtypeStruct((SUBLANE, LANE), F32), jax.ShapeDtypeStruct((S, D), F32),
                   jax.ShapeDtypeStruct((S, D), BF16), jax.ShapeDtypeStruct((1, D), F32)],
        compiler_params=_cparams("arbitrary"),
    )(x, w, target)


def _sigmoid(x):
    return 1.0 / (1.0 + jnp.exp(-x))


def _merge_fwd(gl, ya, yb, yc, *, name):
    S, D = ya.shape
    ts = _row_tile(S)

    def body(ga_ref, gb_ref, gc_ref, ya_ref, yb_ref, yc_ref, o_ref):
        m = (_sigmoid(ga_ref[...]) * ya_ref[...] + _sigmoid(gb_ref[...]) * yb_ref[...]
             + _sigmoid(gc_ref[...]) * yc_ref[...])
        o_ref[...] = m.astype(BF16)

    row = pl.BlockSpec((ts, D), lambda i: (i, 0))
    gspec = [pl.BlockSpec((ts, D), functools.partial(lambda b, i: (i, b), b)) for b in range(3)]
    return pl.pallas_call(
        body, name=name, grid=(S // ts,),
        in_specs=gspec + [row, row, row],
        out_specs=row,
        out_shape=jax.ShapeDtypeStruct((S, D), BF16),
        compiler_params=_cparams("parallel"),
    )(gl, gl, gl, ya, yb, yc)


def _merge_bwd(gl, ya, yb, yc, dm, *, name):
    S, D = ya.shape
    ts = _row_tile(S)

    def body(ga_ref, gb_ref, gc_ref, ya_ref, yb_ref, yc_ref, dm_ref, dgl_ref, da_ref, db_ref, dc_ref):
        dmv = dm_ref[...]
        for b, (g_ref, y_ref, dy_ref) in enumerate(((ga_ref, ya_ref, da_ref), (gb_ref, yb_ref, db_ref),
                                                     (gc_ref, yc_ref, dc_ref))):
            s = _sigmoid(g_ref[...])
            dy_ref[...] = (dmv * s).astype(BF16)
            dgl_ref[:, b * D:(b + 1) * D] = (dmv * y_ref[...] * s * (1.0 - s)).astype(BF16)

    row = pl.BlockSpec((ts, D), lambda i: (i, 0))
    gspec = [pl.BlockSpec((ts, D), functools.partial(lambda b, i: (i, b), b)) for b in range(3)]
    return pl.pallas_call(
        body, name=name, grid=(S // ts,),
        in_specs=gspec + [row, row, row, row],
        out_specs=[pl.BlockSpec((ts, 3 * D), lambda i: (i, 0)), row, row, row],
        out_shape=[jax.ShapeDtypeStruct((S, 3 * D), BF16)] + [jax.ShapeDtypeStruct((S, D), BF16)] * 3,
        compiler_params=_cparams("parallel"),
    )(gl, gl, gl, ya, yb, yc, dm)


POOL_WIDTH = 512


def _pool_cnt(t, w):
    return jnp.minimum(t + 1, w).astype(F32)


def _pool_fwd(rest, col, mix, scale, *, name):
    S = rest.shape[0]
    W, G, H = POOL_WIDTH, POOL_GROUP_DIM, POOL_HALO
    ts = _row_tile(S)

    def body(v_ref, h_ref, mix_ref, sc_ref, d_ref, y_ref, ext_ref):
        i = pl.program_id(0)
        cur = v_ref[...]
        ext_ref[0:H, :] = jnp.where(i > 0, h_ref[...], 0.0)
        ext_ref[H:, :] = cur
        t = i * ts + lax.broadcasted_iota(jnp.int32, (ts, 1), 0)
        for g, w in enumerate(POOL_WINDOWS):
            cols = slice(g * G, (g + 1) * G)
            acc = cur[:, cols]
            for k in range(1, w):
                acc = acc + ext_ref[pl.ds(H - k, ts), cols]
            db = (acc / _pool_cnt(t, w) - cur[:, cols]).astype(BF16)
            d_ref[:, cols] = db
            y = jnp.dot(db, mix_ref[g], preferred_element_type=F32) * sc_ref[:, cols]
            y_ref[:, cols] = y.astype(BF16)

    row = pl.BlockSpec((ts, W), lambda i: (i, 0))
    return pl.pallas_call(
        body, name=name, grid=(S // ts,),
        in_specs=[pl.BlockSpec((ts, W), lambda i: (i, col)),
                  pl.BlockSpec((H, W), lambda i: (jnp.maximum(i * (ts // H) - 1, 0), col)),
                  pl.BlockSpec((len(POOL_WINDOWS), G, G), lambda i: (0, 0, 0)),
                  pl.BlockSpec((1, W), lambda i: (0, 0))],
        out_specs=[row, row],
        out_shape=[jax.ShapeDtypeStruct((S, W), BF16)] * 2,
        scratch_shapes=[pltpu.VMEM((H + ts, W), F32)],
        compiler_params=_cparams("parallel"),
    )(rest, rest, mix, scale)


def _pool_bwd(dy, d, mix, scale, *, name):
    S = dy.shape[0]
    W, G, H = POOL_WIDTH, POOL_GROUP_DIM, POOL_HALO
    ts = _row_tile(S)
    n = S // ts
    NT = (((1,), (1,)), ((), ()))
    TN = (((0,), (0,)), ((), ()))

    def body(dy_ref, dyn_ref, d_ref, mix_ref, sc_ref, dv_ref, dmix_ref, dsc_ref, ext_ref):
        i = pl.program_id(0)

        @pl.when(i == 0)
        def _():
            dmix_ref[...] = jnp.zeros_like(dmix_ref)
            dsc_ref[...] = jnp.zeros_like(dsc_ref)

        t = i * ts + lax.broadcasted_iota(jnp.int32, (ts, 1), 0)
        tn = (i + 1) * ts + lax.broadcasted_iota(jnp.int32, (H, 1), 0)
        for g, w in enumerate(POOL_WINDOWS):
            cols = slice(g * G, (g + 1) * G)
            sc = sc_ref[:, cols]
            dyv = dy_ref[:, cols]
            db = d_ref[:, cols]
            mg = mix_ref[g]
            yp = jnp.dot(db, mg, preferred_element_type=F32)
            dsc_ref[:, cols] += jnp.sum(dyv * yp, axis=0, keepdims=True)
            dyp = (dyv * sc).astype(BF16)
            dd = lax.dot_general(dyp, mg, NT, preferred_element_type=F32)
            dmix_ref[g] += lax.dot_general(db, dyp, TN, preferred_element_type=F32)
            dyn = (jnp.where(i < n - 1, dyn_ref[:, cols], 0.0) * sc).astype(BF16)
            ddn = lax.dot_general(dyn, mg, NT, preferred_element_type=F32)
            e = dd / _pool_cnt(t, w)
            ext_ref[0:ts, cols] = e
            ext_ref[ts:, cols] = ddn / _pool_cnt(tn, w)
            acc = e
            for k in range(1, w):
                acc = acc + ext_ref[pl.ds(k, ts), cols]
            dv_ref[:, cols] = (acc - dd).astype(BF16)

    row = pl.BlockSpec((ts, W), lambda i: (i, 0))
    return pl.pallas_call(
        body, name=name, grid=(n,),
        in_specs=[row,
                  pl.BlockSpec((H, W), lambda i: (jnp.minimum((i + 1) * (ts // H), S // H - 1), 0)),
                  row,
                  pl.BlockSpec((len(POOL_WINDOWS), G, G), lambda i: (0, 0, 0)),
                  pl.BlockSpec((1, W), lambda i: (0, 0))],
        out_specs=[row, pl.BlockSpec((len(POOL_WINDOWS), G, G), lambda i: (0, 0, 0)),
                   pl.BlockSpec((1, W), lambda i: (0, 0))],
        out_shape=[jax.ShapeDtypeStruct((S, W), BF16),
                   jax.ShapeDtypeStruct((len(POOL_WINDOWS), G, G), F32),
                   jax.ShapeDtypeStruct((1, W), F32)],
        scratch_shapes=[pltpu.VMEM((ts + H, W), F32)],
        compiler_params=_cparams("arbitrary"),
    )(dy, dy, d, mix, scale)


def _conv_taps(ext_ref, w_ref, b_ref, rows, first):
    K = w_ref.shape[0]
    pre = b_ref[...] + w_ref[K - 1:K, :] * ext_ref[pl.ds(first, rows), :]
    for k in range(K - 1):
        pre = pre + w_ref[k:k + 1, :] * ext_ref[pl.ds(first - (K - 1) + k, rows), :]
    return pre


def _prev_halo_spec(ts, tc, colfn):
    return pl.BlockSpec((HALO, tc), lambda *g: (jnp.maximum(g[-1] * (ts // HALO) - 1, 0), colfn(*g)))


def _conv_silu_fwd(rest, col0, C, w, b, *, name):
    S = rest.shape[0]
    ts, tc = _row_tile(S), 512
    assert col0 % tc == 0 and C % tc == 0
    cb = col0 // tc

    def body(x_ref, h_ref, w_ref, b_ref, o_ref, ext_ref):
        i = pl.program_id(1)
        ext_ref[0:HALO, :] = jnp.where(i > 0, h_ref[...], 0.0)
        ext_ref[HALO:, :] = x_ref[...]
        pre = _conv_taps(ext_ref, w_ref, b_ref, ts, HALO)
        o_ref[...] = pre * _sigmoid(pre)

    K = w.shape[0]
    return pl.pallas_call(
        body, name=name, grid=(C // tc, S // ts),
        in_specs=[pl.BlockSpec((ts, tc), lambda j, i: (i, cb + j)),
                  _prev_halo_spec(ts, tc, lambda j, i: cb + j),
                  pl.BlockSpec((K, tc), lambda j, i: (0, j)),
                  pl.BlockSpec((1, tc), lambda j, i: (0, j))],
        out_specs=pl.BlockSpec((ts, tc), lambda j, i: (i, j)),
        out_shape=jax.ShapeDtypeStruct((S, C), F32),
        scratch_shapes=[pltpu.VMEM((HALO + ts, tc), F32)],
        compiler_params=_cparams("parallel", "parallel"),
    )(rest, rest, w, b)


def _dsilu(pre):
    s = _sigmoid(pre)
    return s, s * (1.0 + pre * (1.0 - s))


def _conv_silu_bwd(rest, col0, C, w, b, dy, *, name):
    S = rest.shape[0]
    ts, tc = _row_tile(S), 512
    cb = col0 // tc
    n = S // ts
    K = w.shape[0]
    R = ts + HALO

    def body(x_ref, hp_ref, hn_ref, w_ref, b_ref, dy_ref, dyn_ref, dx_ref, dw_ref, db_ref, ext_ref, g_ref):
        i = pl.program_id(1)
        last = i == n - 1
        ext_ref[0:HALO, :] = jnp.where(i > 0, hp_ref[...], 0.0)
        ext_ref[HALO:HALO + ts, :] = x_ref[...]
        ext_ref[HALO + ts:, :] = jnp.where(last, 0.0, hn_ref[...])
        g_ref[0:ts, :] = dy_ref[...]
        g_ref[ts:, :] = jnp.where(last, 0.0, dyn_ref[...])
        pre = _conv_taps(ext_ref, w_ref, b_ref, R, HALO)
        g_ref[...] = g_ref[...] * _dsilu(pre)[1]
        dx = w_ref[K - 1:K, :] * g_ref[pl.ds(0, ts), :]
        for k in range(K - 1):
            dx = dx + w_ref[k:k + 1, :] * g_ref[pl.ds(K - 1 - k, ts), :]
        dx_ref[...] = dx.astype(BF16)
        gc = g_ref[pl.ds(0, ts), :]
        first = i == 0
        _acc_out(db_ref, jnp.sum(gc, axis=0, keepdims=True), first)
        dws = [jnp.sum(gc * ext_ref[pl.ds(HALO - (K - 1) + k, ts), :], axis=0, keepdims=True) for k in range(K)]
        _acc_out(dw_ref, jnp.concatenate(dws, axis=0), first)

    nxt = lambda cf: pl.BlockSpec((HALO, tc), lambda j, i: (jnp.minimum((i + 1) * (ts // HALO), S // HALO - 1), cf(j)))
    return pl.pallas_call(
        body, name=name, grid=(C // tc, n),
        in_specs=[pl.BlockSpec((ts, tc), lambda j, i: (i, cb + j)),
                  _prev_halo_spec(ts, tc, lambda j, i: cb + j),
                  nxt(lambda j: cb + j),
                  pl.BlockSpec((K, tc), lambda j, i: (0, j)),
                  pl.BlockSpec((1, tc), lambda j, i: (0, j)),
                  pl.BlockSpec((ts, tc), lambda j, i: (i, j)),
                  nxt(lambda j: j)],
        out_specs=[pl.BlockSpec((ts, tc), lambda j, i: (i, j)),
                   pl.BlockSpec((K, tc), lambda j, i: (0, j)),
                   pl.BlockSpec((1, tc), lambda j, i: (0, j))],
        out_shape=[jax.ShapeDtypeStruct((S, C), BF16), jax.ShapeDtypeStruct((K, C), F32),
                   jax.ShapeDtypeStruct((1, C), F32)],
        scratch_shapes=[pltpu.VMEM((HALO + R, tc), F32), pltpu.VMEM((R, tc), F32)],
        compiler_params=_cparams("parallel", "arbitrary"),
    )(rest, rest, rest, w, b, dy, dy)


FFN_TC = 256


def _ffn_act_fwd(h0, w, b, *, name):
    S, F2 = h0.shape
    F = F2 // 2
    ts, tc = _row_tile(S), FFN_TC
    nb = F // tc
    K = w.shape[0]

    def body(xg_ref, hg_ref, xv_ref, hv_ref, wg_ref, wv_ref, bg_ref, bv_ref, o_ref, eg_ref, ev_ref):
        i = pl.program_id(1)
        for x_ref, h_ref, e_ref in ((xg_ref, hg_ref, eg_ref), (xv_ref, hv_ref, ev_ref)):
            e_ref[0:HALO, :] = jnp.where(i > 0, h_ref[...], 0.0)
            e_ref[HALO:, :] = x_ref[...]
        pg = _conv_taps(eg_ref, wg_ref, bg_ref, ts, HALO)
        pv = _conv_taps(ev_ref, wv_ref, bv_ref, ts, HALO)
        o_ref[...] = (pg * _sigmoid(pg) * pv).astype(BF16)

    def half(off):
        return [pl.BlockSpec((ts, tc), lambda j, i: (i, off + j)), _prev_halo_spec(ts, tc, lambda j, i: off + j)]

    wspec = lambda off: pl.BlockSpec((K, tc), lambda j, i: (0, off + j))
    bspec = lambda off: pl.BlockSpec((1, tc), lambda j, i: (0, off + j))
    return pl.pallas_call(
        body, name=name, grid=(nb, S // ts),
        in_specs=half(0) + half(nb) + [wspec(0), wspec(nb), bspec(0), bspec(nb)],
        out_specs=pl.BlockSpec((ts, tc), lambda j, i: (i, j)),
        out_shape=jax.ShapeDtypeStruct((S, F), BF16),
        scratch_shapes=[pltpu.VMEM((HALO + ts, tc), F32)] * 2,
        compiler_params=_cparams("parallel", "parallel"),
    )(h0, h0, h0, h0, w, w, b, b)


def _ffn_act_bwd(h0, w, b, da, *, name):
    S, F2 = h0.shape
    F = F2 // 2
    ts, tc = _row_tile(S), FFN_TC
    nb = F // tc
    n = S // ts
    K = w.shape[0]
    R = ts + HALO

    def body(xg_ref, pg_ref, ng_ref, xv_ref, pv_ref, nv_ref, wg_ref, wv_ref, bg_ref, bv_ref, da_ref, dan_ref,
             dx_ref, dw_ref, db_ref, eg_ref, ev_ref, gg_ref, gv_ref):
        i = pl.program_id(1)
        last = i == n - 1
        for x_ref, p_ref, n_ref, e_ref in ((xg_ref, pg_ref, ng_ref, eg_ref), (xv_ref, pv_ref, nv_ref, ev_ref)):
            e_ref[0:HALO, :] = jnp.where(i > 0, p_ref[...], 0.0)
            e_ref[HALO:HALO + ts, :] = x_ref[...]
            e_ref[HALO + ts:, :] = jnp.where(last, 0.0, n_ref[...])
        gg_ref[0:ts, :] = da_ref[...]
        gg_ref[ts:, :] = jnp.where(last, 0.0, dan_ref[...])
        dav = gg_ref[...]
        pg = _conv_taps(eg_ref, wg_ref, bg_ref, R, HALO)
        pv = _conv_taps(ev_ref, wv_ref, bv_ref, R, HALO)
        s, ds = _dsilu(pg)
        gg_ref[...] = dav * pv * ds
        gv_ref[...] = dav * pg * s
        first = i == 0
        for h, (g_ref, e_ref, w_ref) in enumerate(((gg_ref, eg_ref, wg_ref), (gv_ref, ev_ref, wv_ref))):
            dx = w_ref[K - 1:K, :] * g_ref[pl.ds(0, ts), :]
            for k in range(K - 1):
                dx = dx + w_ref[k:k + 1, :] * g_ref[pl.ds(K - 1 - k, ts), :]
            dx_ref[h] = dx.astype(BF16)
            gc = g_ref[pl.ds(0, ts), :]
            dws = [jnp.sum(gc * e_ref[pl.ds(HALO - (K - 1) + k, ts), :], axis=0, keepdims=True) for k in range(K)]
            _acc_out(db_ref.at[h], jnp.sum(gc, axis=0, keepdims=True), first)
            _acc_out(dw_ref.at[h], jnp.concatenate(dws, axis=0), first)

    nxt = lambda off: pl.BlockSpec((HALO, tc), lambda j, i: (jnp.minimum((i + 1) * (ts // HALO), S // HALO - 1), off + j))

    def half(off):
        return [pl.BlockSpec((ts, tc), lambda j, i: (i, off + j)), _prev_halo_spec(ts, tc, lambda j, i: off + j), nxt(off)]

    wspec = lambda off: pl.BlockSpec((K, tc), lambda j, i: (0, off + j))
    bspec = lambda off: pl.BlockSpec((1, tc), lambda j, i: (0, off + j))
    return pl.pallas_call(
        body, name=name, grid=(nb, n),
        in_specs=half(0) + half(nb) + [wspec(0), wspec(nb), bspec(0), bspec(nb),
                                       pl.BlockSpec((ts, tc), lambda j, i: (i, j)), nxt(0)],
        out_specs=[pl.BlockSpec((2, ts, tc), lambda j, i: (0, i, j)),
                   pl.BlockSpec((2, K, tc), lambda j, i: (0, 0, j)),
                   pl.BlockSpec((2, 1, tc), lambda j, i: (0, 0, j))],
        out_shape=[jax.ShapeDtypeStruct((2, S, F), BF16), jax.ShapeDtypeStruct((2, K, F), F32),
                   jax.ShapeDtypeStruct((2, 1, F), F32)],
        scratch_shapes=[pltpu.VMEM((HALO + R, tc), F32)] * 2 + [pltpu.VMEM((R, tc), F32)] * 2,
        compiler_params=_cparams("parallel", "arbitrary"),
    )(h0, h0, h0, h0, h0, h0, w, w, b, b, da, da)


CUM_TILE = 256
ATT_TILE = 512
ATT_TILE_FWD = 1024
NEG = -1e30
N_PAIR = ATTN_HEADS // 2
NT_DIMS = (((1,), (1,)), ((), ()))


def _split3(x):
    a = x.astype(BF16)
    r = x - a.astype(F32)
    b = r.astype(BF16)
    c = (r - b.astype(F32)).astype(BF16)
    return a, b, c


def _tri_dot(tri, x):
    return sum(jnp.dot(tri, p, preferred_element_type=F32) for p in _split3(x))


def _log_sigmoid(x):
    return jnp.minimum(x, 0.0) - jnp.log(1.0 + jnp.exp(-jnp.abs(x)))


def _col(v, idx, lane):
    return jnp.sum(jnp.where(lane == idx, v, 0.0), axis=1, keepdims=True)


def _fox_pre(rest, col, fb, *, name):
    S = rest.shape[0]
    ts = min(CUM_TILE, S)

    def body(f_ref, fb_ref, ccol_ref, carry_ref):
        i = pl.program_id(0)

        @pl.when(i == 0)
        def _():
            carry_ref[...] = jnp.zeros_like(carry_ref)

        lane = lax.broadcasted_iota(jnp.int32, (1, LANE), 1)
        logf = jnp.where(lane < ATTN_HEADS, _log_sigmoid(f_ref[...] + fb_ref[...]), 0.0)
        r = lax.broadcasted_iota(jnp.int32, (ts, ts), 0)
        c = lax.broadcasted_iota(jnp.int32, (ts, ts), 1)
        tri = jnp.where(c <= r, 1.0, 0.0).astype(BF16)
        cs = _tri_dot(tri, logf) + carry_ref[...]
        ccol_ref[...] = cs
        carry_ref[...] = cs[ts - 1:ts, :]

    return pl.pallas_call(
        body, name=name, grid=(S // ts,),
        in_specs=[pl.BlockSpec((ts, LANE), lambda i: (i, col)), pl.BlockSpec((1, LANE), lambda i: (0, 0))],
        out_specs=pl.BlockSpec((ts, LANE), lambda i: (i, 0)),
        out_shape=jax.ShapeDtypeStruct((S, LANE), F32),
        scratch_shapes=[pltpu.VMEM((1, LANE), F32)],
        compiler_params=_cparams("arbitrary"),
    )(rest, fb)


def _fox_post(dcq, dck, rest, col, fb, ddt, *, name):
    S = rest.shape[0]
    ts = min(CUM_TILE, S)
    n = S // ts

    def body(dcq_ref, dck_ref, f_ref, fb_ref, ddt_ref, o_ref, db_ref, carry_ref):
        i = pl.program_id(0)

        @pl.when(i == 0)
        def _():
            carry_ref[...] = jnp.zeros_like(carry_ref)

        lane = lax.broadcasted_iota(jnp.int32, (1, LANE), 1)
        dc = jnp.zeros((ts, LANE), F32)
        for h in range(ATTN_HEADS):
            d = dcq_ref[h // 2] - dck_ref[h // 2]
            dc = jnp.where(lane == h, _col(d, h % 2, lane), dc)
        r = lax.broadcasted_iota(jnp.int32, (ts, ts), 0)
        c = lax.broadcasted_iota(jnp.int32, (ts, ts), 1)
        tri = jnp.where(c >= r, 1.0, 0.0).astype(BF16)
        rc = _tri_dot(tri, dc) + carry_ref[...]
        carry_ref[...] = rc[0:1, :]
        df = rc * _sigmoid(-(f_ref[...] + fb_ref[...]))
        out = jnp.where(lane < ATTN_HEADS, df, ddt_ref[...])
        o_ref[...] = out.astype(BF16)
        _acc_out(db_ref, jnp.sum(out, axis=0, keepdims=True), i == 0)

    rev = lambda i: n - 1 - i
    return pl.pallas_call(
        body, name=name, grid=(n,),
        in_specs=[pl.BlockSpec((N_PAIR, ts, LANE), lambda i: (0, rev(i), 0)),
                  pl.BlockSpec((N_PAIR, ts, LANE), lambda i: (0, rev(i), 0)),
                  pl.BlockSpec((ts, LANE), lambda i: (rev(i), col)),
                  pl.BlockSpec((1, LANE), lambda i: (0, 0)),
                  pl.BlockSpec((ts, LANE), lambda i: (rev(i), 0))],
        out_specs=[pl.BlockSpec((ts, LANE), lambda i: (rev(i), 0)), pl.BlockSpec((1, LANE), lambda i: (0, 0))],
        out_shape=[jax.ShapeDtypeStruct((S, LANE), BF16), jax.ShapeDtypeStruct((1, LANE), F32)],
        scratch_shapes=[pltpu.VMEM((1, LANE), F32)],
        compiler_params=_cparams("arbitrary"),
    )(dcq, dck, rest, fb, ddt)


def _head_masks():
    lane = lax.broadcasted_iota(jnp.int32, (1, LANE), 1)
    return lane, (lane < ATTN_HEAD_DIM, lane >= ATTN_HEAD_DIM)


def _causal_mask(t):
    r = lax.broadcasted_iota(jnp.int32, (t, t), 0)
    c = lax.broadcasted_iota(jnp.int32, (t, t), 1)
    return r, c


AUG_ONE = 0
AUG_C = 3


def _free_lane0(hh):
    return ATTN_HEAD_DIM if hh == 0 else 0


def _augment(x, head_mask, lane, f0, c, key_side):
    terms = _split3(c)
    one = jnp.ones((), BF16)
    ones_at, terms_at = (AUG_ONE, AUG_C) if key_side else (AUG_C, AUG_ONE)
    out = jnp.where(head_mask, x, jnp.zeros((), BF16))
    for k in range(3):
        out = jnp.where(lane == f0 + ones_at + k, one, out)
        out = jnp.where(lane == f0 + terms_at + k, terms[k], out)
    return out


def _attn_fwd(qkv, ccol, *, name):
    S = qkv.shape[0]
    t = min(ATT_TILE_FWD, S)
    n = S // t
    scale = ATTN_HEAD_DIM ** -0.5

    def body(q_ref, k_ref, v_ref, cc_ref, o_ref, lrow_ref, kaug_ref, vt_ref, m_ref, l_ref, acc_ref):
        j, i = pl.program_id(0), pl.program_id(1)
        lane, masks = _head_masks()

        @pl.when(i == 0)
        def _():
            def fill(b, carry):
                off = pl.multiple_of(b * t, t)
                kblk = k_ref[pl.ds(off, t), :]
                ccb = cc_ref[pl.ds(off, t), :]
                for hh in range(2):
                    ck = _col(ccb, 2 * j + hh, lane)
                    kaug_ref[hh, pl.ds(off, t), :] = _augment(kblk, masks[hh], lane, _free_lane0(hh), -ck, True)
                vt_ref[:, pl.ds(off, t)] = v_ref[pl.ds(off, t), :].astype(F32).T.astype(BF16)
                return carry
            lax.fori_loop(0, n, fill, 0)

        q2 = q_ref[...] * scale
        ccq = cc_ref[pl.ds(pl.multiple_of(i * t, t), t), :]
        qaug = [_augment(q2, masks[hh], lane, _free_lane0(hh), _col(ccq, 2 * j + hh, lane), False) for hh in range(2)]
        m_ref[...] = jnp.full(m_ref.shape, NEG, F32)
        l_ref[...] = jnp.zeros_like(l_ref)
        acc_ref[...] = jnp.zeros_like(acc_ref)

        def step(kb, masked):
            off = pl.multiple_of(kb * t, t)
            vt = vt_ref[:, pl.ds(off, t)]
            for hh in range(2):
                st = lax.dot_general(kaug_ref[hh, pl.ds(off, t), :], qaug[hh], NT_DIMS, preferred_element_type=F32)
                if masked:
                    r, c = _causal_mask(t)
                    st = jnp.where(r <= c, st, NEG)
                m_old = m_ref[hh]
                m_new = jnp.maximum(m_old, jnp.max(st, axis=0, keepdims=True))
                p = jnp.exp(st - m_new)
                alpha = jnp.exp(m_old - m_new)
                l_ref[hh] = alpha * l_ref[hh] + jnp.sum(p, axis=0, keepdims=True)
                acc_ref[hh] = alpha * acc_ref[hh] + jnp.dot(vt, p.astype(BF16), preferred_element_type=F32)
                m_ref[hh] = m_new

        def loop_body(kb, carry):
            step(kb, False)
            return carry

        lax.fori_loop(0, i, loop_body, 0)
        step(i, True)
        drow = lax.broadcasted_iota(jnp.int32, (LANE, 1), 0)
        ot = jnp.where(drow < ATTN_HEAD_DIM, acc_ref[0] / l_ref[0], acc_ref[1] / l_ref[1])
        o_ref[...] = ot.T.astype(BF16)
        for hh in range(2):
            lrow_ref[hh] = m_ref[hh] + jnp.log(l_ref[hh])

    return pl.pallas_call(
        body, name=name, grid=(N_PAIR, n),
        in_specs=[pl.BlockSpec((t, LANE), lambda j, i: (i, j)),
                  pl.BlockSpec((S, LANE), lambda j, i: (0, N_PAIR + j)),
                  pl.BlockSpec((S, LANE), lambda j, i: (0, 2 * N_PAIR + j)),
                  pl.BlockSpec((S, LANE), lambda j, i: (0, 0))],
        out_specs=[pl.BlockSpec((t, LANE), lambda j, i: (i, j)),
                   pl.BlockSpec((2, 1, t), lambda j, i: (j, 0, i))],
        out_shape=[jax.ShapeDtypeStruct((S, N_PAIR * LANE), BF16), jax.ShapeDtypeStruct((ATTN_HEADS, 1, S), F32)],
        scratch_shapes=[pltpu.VMEM((2, S, LANE), BF16), pltpu.VMEM((LANE, S), BF16),
                        pltpu.VMEM((2, 1, t), F32), pltpu.VMEM((2, 1, t), F32), pltpu.VMEM((2, LANE, t), F32)],
        compiler_params=_cparams("parallel", "arbitrary"),
    )(qkv, qkv, qkv, ccol)


def _attn_bwd(qkv, o, do, lrow, ccol, *, name):
    S = qkv.shape[0]
    t = min(ATT_TILE, S)
    n = S // t
    scale = ATTN_HEAD_DIM ** -0.5

    def body(q_ref, k_ref, v_ref, o_ref, do_ref, lrow_ref, cc_ref,
             dq_ref, dk_ref, dv_ref, dcq_ref, dck_ref, qaug_ref, drow_ref, dqt_ref, dka_ref, dva_ref):
        j, kb = pl.program_id(0), pl.program_id(1)
        lane, masks = _head_masks()
        drows = lax.broadcasted_iota(jnp.int32, (LANE, 1), 0)

        @pl.when(kb == 0)
        def _():
            dqt_ref[...] = jnp.zeros_like(dqt_ref)

            def fill(b, carry):
                off = pl.multiple_of(b * t, t)
                q2 = q_ref[pl.ds(off, t), :] * scale
                ccb = cc_ref[pl.ds(off, t), :]
                prod_t = (do_ref[pl.ds(off, t), :].astype(F32) * o_ref[pl.ds(off, t), :].astype(F32)).T
                lo = jnp.sum(jnp.where(drows < ATTN_HEAD_DIM, prod_t, 0.0), axis=0, keepdims=True)
                drow_ref[0, :, pl.ds(off, t)] = lo
                drow_ref[1, :, pl.ds(off, t)] = jnp.sum(prod_t, axis=0, keepdims=True) - lo
                for hh in range(2):
                    cq = _col(ccb, 2 * j + hh, lane)
                    qaug_ref[hh, pl.ds(off, t), :] = _augment(q2, masks[hh], lane, _free_lane0(hh), cq, False)
                return carry
            lax.fori_loop(0, n, fill, 0)

        koff = pl.multiple_of(kb * t, t)
        kblk = k_ref[...]
        v2 = v_ref[...]
        cck = cc_ref[pl.ds(koff, t), :]
        kaug = [_augment(kblk, masks[hh], lane, _free_lane0(hh), -_col(cck, 2 * j + hh, lane), True) for hh in range(2)]
        kaug_t = [ka.astype(F32).T.astype(BF16) for ka in kaug]
        vh = [jnp.where(mk, v2, jnp.zeros((), BF16)) for mk in masks]
        dka_ref[...] = jnp.zeros_like(dka_ref)
        dva_ref[...] = jnp.zeros_like(dva_ref)

        def step(qb, masked):
            off = pl.multiple_of(qb * t, t)
            doblk = do_ref[pl.ds(off, t), :]
            for hh in range(2):
                qa = qaug_ref[hh, pl.ds(off, t), :]
                st = lax.dot_general(kaug[hh], qa, NT_DIMS, preferred_element_type=F32)
                if masked:
                    r, c = _causal_mask(t)
                    st = jnp.where(r <= c, st, NEG)
                pt = jnp.exp(st - lrow_ref[hh, :, pl.ds(off, t)])
                dpt = lax.dot_general(vh[hh], doblk, NT_DIMS, preferred_element_type=F32)
                dst = (pt * (dpt - drow_ref[hh, :, pl.ds(off, t)])).astype(BF16)
                dva_ref[hh] += jnp.dot(pt.astype(BF16), doblk, preferred_element_type=F32)
                dka_ref[hh] += jnp.dot(dst, qa, preferred_element_type=F32)
                dqt_ref[hh, :, pl.ds(off, t)] += jnp.dot(kaug_t[hh], dst, preferred_element_type=F32)

        step(kb, True)

        def loop_body(qb, carry):
            step(qb, False)
            return carry

        lax.fori_loop(kb + 1, n, loop_body, 0)
        dk_ref[...] = jnp.where(masks[0], dka_ref[0], dka_ref[1]).astype(BF16)
        dv_ref[...] = jnp.where(masks[0], dva_ref[0], dva_ref[1]).astype(BF16)
        dck = [_col(dka_ref[hh], _free_lane0(hh) + AUG_C, lane) for hh in range(2)]
        dck_ref[0] = jnp.where(lane == 0, dck[0], jnp.where(lane == 1, dck[1], 0.0))

        @pl.when(kb == n - 1)
        def _():
            def flush(b, carry):
                off = pl.multiple_of(b * t, t)
                d = [dqt_ref[hh, :, pl.ds(off, t)].T for hh in range(2)]
                dq_ref[pl.ds(off, t), :] = (jnp.where(masks[0], d[0], d[1]) * scale).astype(BF16)
                dcq = [_col(d[hh], _free_lane0(hh) + AUG_ONE, lane) for hh in range(2)]
                dcq_ref[0, pl.ds(off, t), :] = jnp.where(lane == 0, dcq[0], jnp.where(lane == 1, dcq[1], 0.0))
                return carry
            lax.fori_loop(0, n, flush, 0)

    full = lambda cb: pl.BlockSpec((S, LANE), lambda j, kb: (0, cb(j)))
    kspec = lambda base: pl.BlockSpec((t, LANE), lambda j, kb: (kb, base + j))
    oblk = pl.BlockSpec((t, LANE), lambda j, kb: (kb, j))
    return pl.pallas_call(
        body, name=name, grid=(N_PAIR, n),
        in_specs=[full(lambda j: j), kspec(N_PAIR), kspec(2 * N_PAIR), full(lambda j: j), full(lambda j: j),
                  pl.BlockSpec((2, 1, S), lambda j, kb: (j, 0, 0)), full(lambda j: 0)],
        out_specs=[full(lambda j: j), oblk, oblk,
                   pl.BlockSpec((1, S, LANE), lambda j, kb: (j, 0, 0)),
                   pl.BlockSpec((1, t, LANE), lambda j, kb: (j, kb, 0))],
        out_shape=[jax.ShapeDtypeStruct((S, N_PAIR * LANE), BF16)] * 3 + [jax.ShapeDtypeStruct((N_PAIR, S, LANE), F32)] * 2,
        scratch_shapes=[pltpu.VMEM((2, S, LANE), BF16), pltpu.VMEM((2, 1, S), F32), pltpu.VMEM((2, LANE, S), F32),
                        pltpu.VMEM((2, t, LANE), F32), pltpu.VMEM((2, t, LANE), F32)],
        compiler_params=_cparams("parallel", "arbitrary"),
    )(qkv, qkv, qkv, o, do, lrow, ccol)


DT_LANE0 = ATTN_HEADS
SSD_PAIRS = SSD_HEADS // 2
SSD_X = SSD_HEADS * SSD_HEAD_DIM
SSD_B0 = SSD_X
SSD_C0 = SSD_X + SSD_GROUPS * SSD_STATE
SSD_CH = SSD_X + 2 * SSD_GROUPS * SSD_STATE
TN_DIMS = (((0,), (0,)), ((), ()))


def _softplus(x):
    return jnp.maximum(x, 0.0) + jnp.log(1.0 + jnp.exp(-jnp.abs(x)))


def _ssd_prep(fdt, dtb, alog):
    L = fdt.shape[0]
    lane = lax.broadcasted_iota(jnp.int32, (1, LANE), 1)
    hl = jnp.logical_and(lane >= DT_LANE0, lane < DT_LANE0 + SSD_HEADS)
    dtv = jnp.where(hl, _softplus(fdt + dtb), 0.0)
    A = jnp.where(hl, -jnp.exp(alog), 0.0)
    r = lax.broadcasted_iota(jnp.int32, (L, L), 0)
    c = lax.broadcasted_iota(jnp.int32, (L, L), 1)
    cs = _tri_dot(jnp.where(c <= r, 1.0, 0.0).astype(BF16), dtv * A)
    return lane, hl, dtv, A, cs, r, c


def _halves(lane, v0, v1):
    return jnp.where(lane < SSD_HEAD_DIM, v0, v1)


def _ssd_fwd(xc, rest, col, dtb, alog, dskip, *, name):
    S = xc.shape[0]
    L = SSD_CHUNK
    nc = S // L

    def body(xc_ref, f_ref, dtb_ref, al_ref, dk_ref, y_ref, hs_ref, h_ref, cst_ref):
        i = pl.program_id(0)

        @pl.when(i == 0)
        def _():
            h_ref[...] = jnp.zeros_like(h_ref)

        lane, hl, dtv, A, cs, r, c = _ssd_prep(f_ref[...], dtb_ref[...], al_ref[...])
        cst_ref[...] = cs.T
        cs_last = cs[L - 1:L, :]
        ecs = jnp.exp(cs)
        dec = jnp.exp(cs_last - cs)
        cd = jnp.exp(cs_last)
        dkv = dk_ref[...]
        prow = lax.broadcasted_iota(jnp.int32, (LANE, 1), 0)
        for g in range(SSD_GROUPS):
            Bg = xc_ref[:, SSD_B0 + g * SSD_STATE:SSD_B0 + (g + 1) * SSD_STATE].astype(BF16)
            Cg = xc_ref[:, SSD_C0 + g * SSD_STATE:SSD_C0 + (g + 1) * SSD_STATE].astype(BF16)
            CB = lax.dot_general(Cg, Bg, NT_DIMS, preferred_element_type=F32)
            for pp in range(SSD_PAIRS // SSD_GROUPS):
                pi = g * (SSD_PAIRS // SSD_GROUPS) + pp
                hl0 = DT_LANE0 + 2 * pi
                x2 = xc_ref[:, pi * LANE:(pi + 1) * LANE]
                xd = x2 * _halves(lane, _col(dtv, hl0, lane), _col(dtv, hl0 + 1, lane))
                xdb = xd.astype(BF16)
                yd = jnp.zeros((L, LANE), F32)
                for hh in range(2):
                    seg = _col(cs, hl0 + hh, lane) - cst_ref[hl0 + hh:hl0 + hh + 1, :]
                    M = CB * jnp.exp(jnp.where(c <= r, seg, NEG))
                    yh = jnp.dot(M.astype(BF16), xdb, preferred_element_type=F32)
                    yd = jnp.where((lane >= SSD_HEAD_DIM) if hh else (lane < SSD_HEAD_DIM), yh, yd)
                hp = h_ref[pi]
                hs_ref[0, pi] = hp
                yo = lax.dot_general(Cg, hp.astype(BF16), NT_DIMS, preferred_element_type=F32)
                yo = yo * _halves(lane, _col(ecs, hl0, lane), _col(ecs, hl0 + 1, lane))
                dsk = _halves(lane, _col(dkv, hl0, lane), _col(dkv, hl0 + 1, lane))
                y_ref[:, pi * LANE:(pi + 1) * LANE] = yd + yo + dsk * x2
                xw = (xd * _halves(lane, _col(dec, hl0, lane), _col(dec, hl0 + 1, lane))).astype(BF16)
                st = lax.dot_general(xw, Bg, TN_DIMS, preferred_element_type=F32)
                cdp = jnp.where(prow < SSD_HEAD_DIM, _col(cd, hl0, lane), _col(cd, hl0 + 1, lane))
                h_ref[pi] = cdp * hp + st

    vec = pl.BlockSpec((1, LANE), lambda i: (0, 0))
    return pl.pallas_call(
        body, name=name, grid=(nc,),
        in_specs=[pl.BlockSpec((L, SSD_CH), lambda i: (i, 0)), pl.BlockSpec((L, LANE), lambda i: (i, col)),
                  vec, vec, vec],
        out_specs=[pl.BlockSpec((L, SSD_X), lambda i: (i, 0)),
                   pl.BlockSpec((1, SSD_PAIRS, LANE, SSD_STATE), lambda i: (i, 0, 0, 0))],
        out_shape=[jax.ShapeDtypeStruct((S, SSD_X), F32),
                   jax.ShapeDtypeStruct((nc, SSD_PAIRS, LANE, SSD_STATE), F32)],
        scratch_shapes=[pltpu.VMEM((SSD_PAIRS, LANE, SSD_STATE), F32), pltpu.VMEM((LANE, L), F32)],
        compiler_params=_cparams("arbitrary"),
    )(xc, rest, dtb, alog, dskip)


def _pair_sums(lane, v):
    lo = jnp.sum(jnp.where(lane < SSD_HEAD_DIM, v, 0.0), axis=1, keepdims=True)
    return lo, jnp.sum(v, axis=1, keepdims=True) - lo


def _ssd_bwd(xc, rest, col, dtb, alog, dskip, hs, dy, *, name):
    S = xc.shape[0]
    L = SSD_CHUNK
    nc = S // L
    PG = SSD_PAIRS // SSD_GROUPS

    def body(xc_ref, f_ref, dtb_ref, al_ref, dk_ref, hs_ref, dy_ref, dxc_ref, ddt_ref, dp_ref, dh_ref, cst_ref):
        i = pl.program_id(0)

        @pl.when(i == 0)
        def _():
            dh_ref[...] = jnp.zeros_like(dh_ref)
            dp_ref[...] = jnp.zeros_like(dp_ref)

        fv = f_ref[...] + dtb_ref[...]
        lane, hl, dtv, A, cs, r, c = _ssd_prep(f_ref[...], dtb_ref[...], al_ref[...])
        cst_ref[...] = cs.T
        cs_last = cs[L - 1:L, :]
        ecs = jnp.exp(cs)
        dec = jnp.exp(cs_last - cs)
        cd = jnp.exp(cs_last)
        dkv = dk_ref[...]
        prow = lax.broadcasted_iota(jnp.int32, (LANE, 1), 0)
        lrow = lax.broadcasted_iota(jnp.int32, (L, 1), 0)
        is_last = lrow == L - 1
        causal = c <= r
        dcs = jnp.zeros((L, LANE), F32)
        ddt = jnp.zeros((L, LANE), F32)
        dD = jnp.zeros((1, LANE), F32)
        for g in range(SSD_GROUPS):
            Bg = xc_ref[:, SSD_B0 + g * SSD_STATE:SSD_B0 + (g + 1) * SSD_STATE].astype(BF16)
            Cg = xc_ref[:, SSD_C0 + g * SSD_STATE:SSD_C0 + (g + 1) * SSD_STATE].astype(BF16)
            CB = lax.dot_general(Cg, Bg, NT_DIMS, preferred_element_type=F32)
            dCB = jnp.zeros((L, L), F32)
            dB = jnp.zeros((L, SSD_STATE), F32)
            dC = jnp.zeros((L, SSD_STATE), F32)
            for pp in range(PG):
                pi = g * PG + pp
                hl0 = DT_LANE0 + 2 * pi
                x2 = xc_ref[:, pi * LANE:(pi + 1) * LANE]
                dy2 = dy_ref[:, pi * LANE:(pi + 1) * LANE]
                dtp = _halves(lane, _col(dtv, hl0, lane), _col(dtv, hl0 + 1, lane))
                xd = x2 * dtp
                xdb = xd.astype(BF16)
                dsk = _halves(lane, _col(dkv, hl0, lane), _col(dkv, hl0 + 1, lane))
                dx2 = dsk * dy2
                sD = _pair_sums(lane, dy2 * x2)
                hp = hs_ref[0, pi]
                hpb = hp.astype(BF16)
                ecsp = _halves(lane, _col(ecs, hl0, lane), _col(ecs, hl0 + 1, lane))
                yo = lax.dot_general(Cg, hpb, NT_DIMS, preferred_element_type=F32) * ecsp
                dW = (ecsp * dy2).astype(BF16)
                dC = dC + jnp.dot(dW, hpb, preferred_element_type=F32)
                dhp = lax.dot_general(dW, Cg, TN_DIMS, preferred_element_type=F32)
                sYo = _pair_sums(lane, dy2 * yo)
                dhn = dh_ref[pi]
                cdh = (_col(cd, hl0, lane), _col(cd, hl0 + 1, lane))
                dhp = dhp + jnp.where(prow < SSD_HEAD_DIM, cdh[0], cdh[1]) * dhn
                rs = jnp.sum(dhn * hp, axis=1, keepdims=True)
                lo = jnp.sum(jnp.where(prow < SSD_HEAD_DIM, rs, 0.0), axis=0, keepdims=True)
                dcd = (lo, jnp.sum(rs, axis=0, keepdims=True) - lo)
                dhnb = dhn.astype(BF16)
                decp = _halves(lane, _col(dec, hl0, lane), _col(dec, hl0 + 1, lane))
                G = lax.dot_general(Bg, dhnb, NT_DIMS, preferred_element_type=F32)
                dxd = decp * G
                sdec = _pair_sums(lane, xd * G)
                dB = dB + jnp.dot((xd * decp).astype(BF16), dhnb, preferred_element_type=F32)
                dh_ref[pi] = dhp
                for hh in range(2):
                    hmask = (lane >= SSD_HEAD_DIM) if hh else (lane < SSD_HEAD_DIM)
                    seg = _col(cs, hl0 + hh, lane) - cst_ref[hl0 + hh:hl0 + hh + 1, :]
                    Lm = jnp.exp(jnp.where(causal, seg, NEG))
                    M = CB * Lm
                    dyh = jnp.where(hmask, dy2, 0.0).astype(BF16)
                    dM = lax.dot_general(dyh, xdb, NT_DIMS, preferred_element_type=F32)
                    dxd = dxd + lax.dot_general(M.astype(BF16), dyh, TN_DIMS, preferred_element_type=F32)
                    Q = dM * M
                    dCB = dCB + dM * Lm
                    dech = _col(dec, hl0 + hh, lane)
                    dd = sdec[hh] * dech
                    end = dcd[hh] * cdh[hh] + jnp.sum(dd, axis=0, keepdims=True)
                    dcs_h = (sYo[hh] - dd + jnp.sum(Q, axis=1, keepdims=True)
                             - jnp.sum(Q.T, axis=1, keepdims=True) + jnp.where(is_last, end, 0.0))
                    dcs = jnp.where(lane == hl0 + hh, dcs_h, dcs)
                    dD = jnp.where(lane == hl0 + hh, jnp.sum(sD[hh], axis=0, keepdims=True), dD)
                sdt = _pair_sums(lane, dxd * x2)
                ddt = jnp.where(lane == hl0, sdt[0], jnp.where(lane == hl0 + 1, sdt[1], ddt))
                dxc_ref[:, pi * LANE:(pi + 1) * LANE] = dx2 + dxd * dtp
            dCBb = dCB.astype(BF16)
            dC = dC + jnp.dot(dCBb, Bg, preferred_element_type=F32)
            dB = dB + lax.dot_general(dCBb, Cg, TN_DIMS, preferred_element_type=F32)
            dxc_ref[:, SSD_B0 + g * SSD_STATE:SSD_B0 + (g + 1) * SSD_STATE] = dB
            dxc_ref[:, SSD_C0 + g * SSD_STATE:SSD_C0 + (g + 1) * SSD_STATE] = dC
        da = _tri_dot(jnp.where(c >= r, 1.0, 0.0).astype(BF16), dcs)
        ddtv = ddt + da * A
        ddt_ref[...] = jnp.where(hl, ddtv * _sigmoid(fv), 0.0)
        dal = jnp.sum(da * dtv, axis=0, keepdims=True) * A
        dp_ref[0:1, :] += dal
        dp_ref[1:2, :] += dD

    rev = lambda i: nc - 1 - i
    vec = pl.BlockSpec((1, LANE), lambda i: (0, 0))
    return pl.pallas_call(
        body, name=name, grid=(nc,),
        in_specs=[pl.BlockSpec((L, SSD_CH), lambda i: (rev(i), 0)), pl.BlockSpec((L, LANE), lambda i: (rev(i), col)),
                  vec, vec, vec,
                  pl.BlockSpec((1, SSD_PAIRS, LANE, SSD_STATE), lambda i: (rev(i), 0, 0, 0)),
                  pl.BlockSpec((L, SSD_X), lambda i: (rev(i), 0))],
        out_specs=[pl.BlockSpec((L, SSD_CH), lambda i: (rev(i), 0)), pl.BlockSpec((L, LANE), lambda i: (rev(i), 0)),
                   pl.BlockSpec((SUBLANE, LANE), lambda i: (0, 0))],
        out_shape=[jax.ShapeDtypeStruct((S, SSD_CH), F32), jax.ShapeDtypeStruct((S, LANE), F32),
                   jax.ShapeDtypeStruct((SUBLANE, LANE), F32)],
        scratch_shapes=[pltpu.VMEM((SSD_PAIRS, LANE, SSD_STATE), F32), pltpu.VMEM((LANE, L), F32)],
        compiler_params=_cparams("arbitrary"),
    )(xc, rest, dtb, alog, dskip, hs, dy)


def _gate_norm_fwd(y, rest, zcol, nw, *, name):
    S, W = y.shape
    ts = _row_tile(S)
    GW = W // SSD_GROUPS

    def body(y_ref, z_ref, nw_ref, o_ref):
        z = z_ref[...]
        t = y_ref[...] * (z * _sigmoid(z))
        for g in range(SSD_GROUPS):
            cols = slice(g * GW, (g + 1) * GW)
            tg = t[:, cols]
            rr = lax.rsqrt(jnp.mean(tg * tg, axis=-1, keepdims=True) + NORM_EPS)
            o_ref[:, cols] = (tg * rr * nw_ref[:, cols]).astype(BF16)

    row = pl.BlockSpec((ts, W), lambda i: (i, 0))
    return pl.pallas_call(
        body, name=name, grid=(S // ts,),
        in_specs=[row, pl.BlockSpec((ts, W), lambda i: (i, zcol)), pl.BlockSpec((1, W), lambda i: (0, 0))],
        out_specs=row,
        out_shape=jax.ShapeDtypeStruct((S, W), BF16),
        compiler_params=_cparams("parallel"),
    )(y, rest, nw)


def _gate_norm_bwd(y, rest, zcol, nw, do, *, name):
    S, W = y.shape
    ts = _row_tile(S)
    GW = W // SSD_GROUPS

    def body(y_ref, z_ref, nw_ref, do_ref, dy_ref, dz_ref, dnw_ref):
        z = z_ref[...]
        yv = y_ref[...]
        s, ds = _dsilu(z)
        sz = z * s
        t = yv * sz
        dov = do_ref[...]
        parts = []
        for g in range(SSD_GROUPS):
            cols = slice(g * GW, (g + 1) * GW)
            tg = t[:, cols]
            rr = lax.rsqrt(jnp.mean(tg * tg, axis=-1, keepdims=True) + NORM_EPS)
            n = tg * rr
            dog = dov[:, cols]
            dn = dog * nw_ref[:, cols]
            dt = rr * (dn - n * jnp.mean(dn * n, axis=-1, keepdims=True))
            dy_ref[:, cols] = dt * sz[:, cols]
            dz_ref[:, cols] = (dt * yv[:, cols] * ds[:, cols]).astype(BF16)
            parts.append(jnp.sum(dog * n, axis=0, keepdims=True))
        _acc_out(dnw_ref, jnp.concatenate(parts, axis=1), pl.program_id(0) == 0)

    row = pl.BlockSpec((ts, W), lambda i: (i, 0))
    vec = pl.BlockSpec((1, W), lambda i: (0, 0))
    return pl.pallas_call(
        body, name=name, grid=(S // ts,),
        in_specs=[row, pl.BlockSpec((ts, W), lambda i: (i, zcol)), vec, row],
        out_specs=[row, row, vec],
        out_shape=[jax.ShapeDtypeStruct((S, W), F32), jax.ShapeDtypeStruct((S, W), BF16),
                   jax.ShapeDtypeStruct((1, W), F32)],
        compiler_params=_cparams("arbitrary"),
    )(y, rest, nw, do)


N_DEV = 8
MESH = pl.DeviceIdType.MESH
ANY = pl.BlockSpec(memory_space=pl.ANY)


def _all_gather(xs, *, name):
    n = len(xs)

    def body(*refs):
        x_refs, o_refs = refs[:n], refs[n:2 * n]
        send_sems, recv_sems, local_sems = refs[2 * n:]
        px, py, pc = lax.axis_index("x"), lax.axis_index("y"), lax.axis_index("c")
        me, sibling = (px, py, pc), (px, py, 1 - pc)
        chips = [(1 - px, py), (px, 1 - py), (1 - px, 1 - py)]

        def copy(a, k, block, to, src=None):
            slot = o_refs[a].at[4 * block[0] + 2 * block[1] + block[2]]
            return pltpu.make_async_remote_copy(
                src_ref=slot if src is None else src, dst_ref=slot,
                send_sem=send_sems.at[a, k], recv_sem=recv_sems.at[a, k], device_id=to, device_id_type=MESH)

        mine = [pltpu.make_async_copy(x_refs[a], o_refs[a].at[4 * px + 2 * py + pc], local_sems.at[a]) for a in range(n)]
        for cp in mine:
            cp.start()
        first = []
        for a in range(n):
            first.append(copy(a, 0, me, sibling, src=x_refs[a]))
            first += [copy(a, 1 + j, me, (*chip, pc), src=x_refs[a]) for j, chip in enumerate(chips)]
        for cp in first:
            cp.start()
        passed = []
        for j, chip in enumerate(chips):
            for a in range(n):
                copy(a, 1 + j, (*chip, pc), me).wait_recv()
                fwd = copy(a, 4 + j, (*chip, pc), sibling)
                fwd.start()
                passed.append(fwd)
        for a in range(n):
            copy(a, 0, sibling, me).wait_recv()
            for j, chip in enumerate(chips):
                copy(a, 4 + j, (*chip, 1 - pc), me).wait_recv()
        for cp in first + passed:
            cp.wait_send()
        for cp in mine:
            cp.wait()

    return pl.pallas_call(
        body, name=name,
        out_shape=[jax.ShapeDtypeStruct((N_DEV,) + x.shape, x.dtype) for x in xs],
        in_specs=[ANY] * n, out_specs=[ANY] * n,
        scratch_shapes=[pltpu.SemaphoreType.DMA((n, N_DEV - 1)), pltpu.SemaphoreType.DMA((n, N_DEV - 1)),
                        pltpu.SemaphoreType.DMA((n,))],
    )(*xs)


def _exchange(sends, *, name):
    n = len(sends)

    def body(*refs):
        s_refs, r_refs = refs[:n], refs[n:2 * n]
        send_sems, recv_sems, local_sems = refs[2 * n:]
        px, py, pc = lax.axis_index("x"), lax.axis_index("y"), lax.axis_index("c")
        me = 4 * px + 2 * py + pc
        local = [pltpu.make_async_copy(s_refs[a].at[me], r_refs[a].at[me], local_sems.at[a]) for a in range(n)]
        for cp in local:
            cp.start()
        sent, landing = [], []
        for k in range(1, N_DEV):
            qx = 1 - px if k & 4 else px
            qy = 1 - py if k & 2 else py
            qc = 1 - pc if k & 1 else pc
            peer = 4 * qx + 2 * qy + qc
            for a in range(n):
                def rdma(dst_slot):
                    return pltpu.make_async_remote_copy(
                        src_ref=s_refs[a].at[peer], dst_ref=r_refs[a].at[dst_slot],
                        send_sem=send_sems.at[a, k - 1], recv_sem=recv_sems.at[a, k - 1],
                        device_id=(qx, qy, qc), device_id_type=MESH)
                cp = rdma(me)
                cp.start()
                sent.append(cp)
                landing.append(rdma(peer))
        for cp in landing:
            cp.wait_recv()
        for cp in sent:
            cp.wait_send()
        for cp in local:
            cp.wait()

    return pl.pallas_call(
        body, name=name,
        out_shape=[jax.ShapeDtypeStruct(s.shape, s.dtype) for s in sends],
        in_specs=[ANY] * n, out_specs=[ANY] * n,
        scratch_shapes=[pltpu.SemaphoreType.DMA((n, N_DEV - 1)), pltpu.SemaphoreType.DMA((n, N_DEV - 1)),
                        pltpu.SemaphoreType.DMA((n,))],
    )(*sends)


IN_POOL, IN_QKV, IN_F, IN_Z, IN_XBC, IN_DT, IN_GATE, IN_TOTAL = 0, 512, 2048, 2056, 3080, 4616, 4632, 7704
QKV_W = IN_F - IN_QKV
REST_Z_BLK, REST_POOL_BLK, REST_XBC0, REST_FDT_BLK = 3, 8, 4608, 48
REST_FDT0 = REST_FDT_BLK * LANE
REST_USED = REST_FDT0 + (IN_Z - IN_F) + (IN_GATE - IN_DT)
REST_W = REST_FDT0 + LANE


def _w_in_runs(shard):
    segs = [(0, IN_QKV, IN_F, 0), (1, IN_GATE, IN_TOTAL, 0), (1, IN_Z, IN_XBC, 3072), (1, IN_POOL, IN_QKV, 4096),
            (1, IN_XBC, IN_DT, REST_XBC0), (1, IN_F, IN_Z, REST_FDT0), (1, IN_DT, IN_GATE, REST_FDT0 + IN_Z - IN_F)]
    runs = []
    for dst, t0, t1, d0 in segs:
        for j in range(N_DEV):
            lo, hi = max(t0, shard * j), min(t1, shard * (j + 1))
            if lo < hi:
                runs.append((dst, d0 + lo - t0, j, lo - shard * j, hi - lo))
    return runs


def _repack_w_in(g, shard, *, name):
    _, L, K, P = g.shape
    tr = 256
    runs = _w_in_runs(shard)

    def body(g_ref, q_ref, r_ref):
        r_ref[:, REST_USED:] = jnp.zeros((tr, REST_W - REST_USED), g_ref.dtype)
        for dst, b, j, a, n in runs:
            (q_ref, r_ref)[dst][:, b:b + n] = g_ref[j, :, a:a + n]

    return pl.pallas_call(
        body, name=name, grid=(L, K // tr),
        in_specs=[pl.BlockSpec((N_DEV, None, tr, P), lambda l, i: (0, l, i, 0))],
        out_specs=[pl.BlockSpec((None, tr, QKV_W), lambda l, i: (l, i, 0)),
                   pl.BlockSpec((None, tr, REST_W), lambda l, i: (l, i, 0))],
        out_shape=[jax.ShapeDtypeStruct((L, K, QKV_W), g.dtype), jax.ShapeDtypeStruct((L, K, REST_W), g.dtype)],
        compiler_params=_cparams("parallel", "parallel"),
    )(g)


def _held_index(l, li, i, n):
    return jnp.where(l == li, i, jnp.where(l > li, n - 1, 0))


def _repack_dw_in(dwq, dwr, shard, *, name):
    L = len(dwq)
    K = dwq[0].shape[0]
    tr = 128
    nt = K // tr
    runs = _w_in_runs(shard)

    def body(*refs):
        srcs, o_ref = refs[:2 * L], refs[2 * L]
        l = pl.program_id(0)
        for li in range(L):
            @pl.when(l == li)
            def _():
                for dst, b, j, a, n in runs:
                    o_ref[j, :, a:a + n] = srcs[2 * li + dst][:, b:b + n]

    in_specs = []
    for li in range(L):
        hold = functools.partial(lambda li, l, i: (_held_index(l, li, i, nt), 0), li)
        in_specs += [pl.BlockSpec((tr, QKV_W), hold), pl.BlockSpec((tr, REST_W), hold)]
    args = [x for pair in zip(dwq, dwr) for x in pair]
    return pl.pallas_call(
        body, name=name, grid=(L, nt),
        in_specs=in_specs,
        out_specs=pl.BlockSpec((N_DEV, None, tr, shard), lambda l, i: (0, l, i, 0)),
        out_shape=jax.ShapeDtypeStruct((N_DEV, L, K, shard), dwq[0].dtype),
        compiler_params=_cparams("arbitrary", "arbitrary"),
    )(*args)


FLAT_W = 1024
ADAM_ROWS = 128


def _adam_math(w, g, m, v):
    m2 = ADAM_B1 * m + (1.0 - ADAM_B1) * g
    v2 = ADAM_B2 * v + (1.0 - ADAM_B2) * (g * g)
    m_hat = m2 / (1.0 - ADAM_B1 ** ADAM_STEP)
    v_hat = v2 / (1.0 - ADAM_B2 ** ADAM_STEP)
    delta = -ADAM_LR * (m_hat / (jnp.sqrt(v_hat) + ADAM_EPS) + ADAM_WD * w)
    return delta, m2, v2


def _row_div(R, align):
    best = R
    for t in range(align, min(R, ADAM_ROWS) + 1, align):
        if R % t == 0:
            best = t
    return best


def _sum8(ref_of):
    g = ref_of(0).astype(F32)
    for s in range(1, N_DEV):
        g = g + ref_of(s).astype(F32)
    return g


def _sum_adam(recvs, w, m, v, *, name):
    L, R, C = w.shape
    single = not isinstance(recvs, (list, tuple))
    recvs = [recvs] if single else list(recvs)
    tr = _row_div(R, SUBLANE * (4 // recvs[0].dtype.itemsize))
    nt = R // tr
    nr = len(recvs)

    def body(*refs):
        r_refs = refs[:nr]
        w_ref, m_ref, v_ref, g_ref, d_ref, m2_ref, v2_ref = refs[nr:]

        def run(ref_of):
            g = _sum8(ref_of)
            delta, m2, v2 = _adam_math(w_ref[...], g, m_ref[...], v_ref[...])
            g_ref[...] = g
            d_ref[...] = delta
            m2_ref[...] = m2
            v2_ref[...] = v2

        if single:
            run(lambda s: r_refs[0][s])
        else:
            l = pl.program_id(0)
            for li in range(L):
                @pl.when(l == li)
                def _():
                    run(lambda s: r_refs[li][s])

    if single:
        r_specs = [pl.BlockSpec((N_DEV, None, tr, C), lambda l, i: (0, l, i, 0))]
    else:
        r_specs = [pl.BlockSpec((N_DEV, tr, C), functools.partial(lambda li, l, i: (0, _held_index(l, li, i, nt), 0), li))
                   for li in range(L)]
    row = pl.BlockSpec((None, tr, C), lambda l, i: (l, i, 0))
    return pl.pallas_call(
        body, name=name, grid=(L, nt),
        in_specs=r_specs + [row, row, row],
        out_specs=[row] * 4,
        out_shape=[jax.ShapeDtypeStruct((L, R, C), F32)] * 4,
        compiler_params=_cparams("arbitrary", "arbitrary"),
    )(*recvs, w, m, v)


def _sum_parts(parts, *, name):
    _, R, C = parts.shape

    def body(p_ref, o_ref):
        o_ref[...] = _sum8(lambda s: p_ref[s])

    return pl.pallas_call(
        body, name=name, out_shape=jax.ShapeDtypeStruct((R, C), F32),
        in_specs=[pl.BlockSpec(memory_space=pltpu.VMEM)], out_specs=pl.BlockSpec(memory_space=pltpu.VMEM),
    )(parts)


def _adam(g, w, m, v, *, name):
    def body(g_ref, w_ref, m_ref, v_ref, d_ref, m2_ref, v2_ref):
        delta, m2, v2 = _adam_math(w_ref[...], g_ref[...], m_ref[...], v_ref[...])
        d_ref[...] = delta
        m2_ref[...] = m2
        v2_ref[...] = v2

    whole = pl.BlockSpec(memory_space=pltpu.VMEM)
    return pl.pallas_call(
        body, name=name, in_specs=[whole] * 4, out_specs=[whole] * 3,
        out_shape=[jax.ShapeDtypeStruct(g.shape, F32)] * 3,
    )(g, w, m, v)


SHARDED = ("w_in", "ffn_up", "p_pool", "p_attn", "p_ssd", "w_out", "ffn_down", "ssd_conv_w", "ffn_conv_w")
REPLICATED = ("norm_mix", "pool_mix", "pool_scale", "f_bias", "ssd_conv_b", "ssd_dt_bias", "ssd_a_log", "ssd_d",
              "ssd_norm", "norm_ffn", "ffn_conv_b", "norm_final")
WEIGHTS = ("norm_mix", "w_in", "pool_mix", "pool_scale", "f_bias", "ssd_conv_w", "ssd_conv_b", "ssd_dt_bias",
           "ssd_a_log", "ssd_d", "ssd_norm", "p_pool", "p_attn", "p_ssd", "w_out", "norm_ffn", "ffn_up",
           "ffn_conv_w", "ffn_conv_b", "ffn_down", "norm_final")


def _round_up(n, k):
    return -(-n // k) * k


def _pad_last(x, width):
    return jnp.pad(x, [(0, 0)] * (x.ndim - 1) + [(0, width - x.shape[-1])])


def _flat_rows(parts, rows):
    flat = jnp.concatenate([p.reshape(-1) for p in parts])
    return jnp.pad(flat, (0, rows * FLAT_W - flat.shape[0])).reshape(rows, FLAT_W)


def _lanes(v, lane0):
    return jnp.pad(v, (lane0, LANE - lane0 - v.shape[0]))[None]


def _cols_by_device(g):
    _, L, R, c = g.shape
    return jnp.moveaxis(g, 0, 2).reshape(L, R, N_DEV * c)


def _rows_by_device(g):
    _, L, r, C = g.shape
    return jnp.moveaxis(g, 0, 1).reshape(L, N_DEV * r, C)


def _layer_fwd(x, P, l):
    nm = lambda s: f"l{l}_{s}"
    u = _rms_fwd(x, P["nm"], name=nm("rms_mix"))
    qkv = _mm(u, P["w_qkv"], out_dtype=BF16, name=nm("proj_qkv"))
    rest = _mm(u, P["w_rest"], name=nm("proj_rest"))
    d, ya0 = _pool_fwd(rest, REST_POOL_BLK, P["mix"], P["pscale"], name=nm("pool"))
    ya = _mm(ya0, P["p_pool"], name=nm("p_pool"))
    ccol = _fox_pre(rest, REST_FDT_BLK, P["fb"], name=nm("fox_pre"))
    o, lrow = _attn_fwd(qkv, ccol, name=nm("attn"))
    yb = _mm(o, P["p_attn"], name=nm("p_attn"))
    xc = _conv_silu_fwd(rest, REST_XBC0, SSD_CH, P["cw"], P["cb"], name=nm("ssd_conv"))
    y, hs = _ssd_fwd(xc, rest, REST_FDT_BLK, P["dtb"], P["alog"], P["dsk"], name=nm("ssd"))
    yc0 = _gate_norm_fwd(y, rest, REST_Z_BLK, P["snw"], name=nm("ssd_norm"))
    yc = _mm(yc0, P["p_ssd"], name=nm("p_ssd"))
    merged = _merge_fwd(rest, ya, yb, yc, name=nm("merge"))
    x1 = _mm(merged, P["w_out"], add=x, name=nm("w_out"))
    u2 = _rms_fwd(x1, P["nf"], name=nm("rms_ffn"))
    h0 = _mm_up_fwd(u2, P["g_up"], l, name=nm("ffn_up"))
    a = _ffn_act_fwd(h0, P["fcw"], P["fcb"], name=nm("ffn_act"))
    x2 = _mm(a, P["w_down"], add=x1, name=nm("ffn_down"))
    saved = dict(x=x, u=u, qkv=qkv, rest=rest, d=d, ya0=ya0, ya=ya, ccol=ccol, o=o, lrow=lrow, yb=yb,
                 xc=xc, y=y, hs=hs, yc0=yc0, yc=yc, merged=merged, x1=x1, u2=u2, h0=h0, a=a)
    return x2, saved


def _layer_bwd(g, gb, sv, P, l):
    nm = lambda s: f"l{l}_{s}_bwd"
    rest = sv["rest"]
    G = {}
    up_blocks, up_w, up_pad = P["up_blocks"]
    half = up_blocks // 2
    da = _mm(gb, P["w_down"], tb=True, name=nm("ffn_down_dx"))
    dwd = _mm(sv["a"], gb, ta=True, out_dtype=GRAD_WIRE, name=nm("ffn_down_dw"))
    G["ffn_down"] = dwd.reshape(half, up_pad, -1)[:, :up_w].reshape(N_DEV, -1, dwd.shape[1])
    dh, dfcw, dfcb = _ffn_act_bwd(sv["h0"], P["fcw"], P["fcb"], da, name=nm("ffn_act"))
    du2 = _mm_up_dx(dh, P["g_up"], l, name=nm("ffn_up_dx"))
    G["ffn_up"] = _mm_up_dw(sv["u2"], dh, up_blocks, up_w, name=nm("ffn_up_dw"))
    taps = dfcw.shape[1]
    G["ffn_conv_w"] = jnp.moveaxis(dfcw.reshape(2, taps, half, up_pad)[..., :up_w], 2, 1).reshape(N_DEV, taps, up_w)
    G["ffn_conv_b"] = dfcb.reshape(2, half, up_pad)[..., :up_w].reshape(-1)
    dx1, dx1b, dnf = _rms_bwd(sv["x1"], P["nf"], du2, g, name=nm("rms_ffn"))
    G["norm_ffn"] = dnf[0]
    dmerged = _mm(dx1b, P["w_out"], tb=True, name=nm("w_out_dx"))
    dwo = _mm(sv["merged"], dx1b, ta=True, out_dtype=GRAD_WIRE, name=nm("w_out_dw"))
    G["w_out"] = dwo.reshape(N_DEV, -1, dwo.shape[1])
    dgl, dya, dyb, dyc = _merge_bwd(rest, sv["ya"], sv["yb"], sv["yc"], dmerged, name=nm("merge"))
    G["p_pool"] = _mm_tn_blocks(sv["ya0"], dya, N_DEV, name=nm("p_pool_dw"))
    dya0 = _mm(dya, P["p_pool"], tb=True, name=nm("p_pool_dx"))
    G["p_attn"] = _mm_tn_blocks(sv["o"], dyb, N_DEV, name=nm("p_attn_dw"))
    do = _mm(dyb, P["p_attn"], tb=True, out_dtype=BF16, name=nm("p_attn_dx"))
    dps = _mm(sv["yc0"], dyc, ta=True, out_dtype=GRAD_WIRE, name=nm("p_ssd_dw"))
    G["p_ssd"] = dps.reshape(N_DEV, -1, dps.shape[1])
    dyc0 = _mm(dyc, P["p_ssd"], tb=True, name=nm("p_ssd_dx"))
    dpv, dmix, dpsc = _pool_bwd(dya0, sv["d"], P["mix"], P["pscale"], name=nm("pool"))
    G["pool_mix"], G["pool_scale"] = dmix, dpsc[0]
    dq, dk, dv, dcq, dck = _attn_bwd(sv["qkv"], sv["o"], do, sv["lrow"], sv["ccol"], name=nm("attn"))
    dy, dz, dsnw = _gate_norm_bwd(sv["y"], rest, REST_Z_BLK, P["snw"], dyc0, name=nm("ssd_norm"))
    G["ssd_norm"] = dsnw[0]
    dxc, ddt, dpar = _ssd_bwd(sv["xc"], rest, REST_FDT_BLK, P["dtb"], P["alog"], P["dsk"], sv["hs"], dy, name=nm("ssd"))
    G["ssd_a_log"] = dpar[0, DT_LANE0:DT_LANE0 + SSD_HEADS]
    G["ssd_d"] = dpar[1, DT_LANE0:DT_LANE0 + SSD_HEADS]
    dxbc, dcw, dcb = _conv_silu_bwd(rest, REST_XBC0, SSD_CH, P["cw"], P["cb"], dxc, name=nm("ssd_conv"))
    G["ssd_conv_w"] = jnp.moveaxis(dcw.reshape(dcw.shape[0], N_DEV, -1), 1, 0)
    G["ssd_conv_b"] = dcb[0]
    dfdt, dfdtb = _fox_post(dcq, dck, rest, REST_FDT_BLK, P["fb"], ddt, name=nm("fox_post"))
    G["f_bias"] = dfdtb[0, :ATTN_HEADS]
    G["ssd_dt_bias"] = dfdtb[0, DT_LANE0:DT_LANE0 + SSD_HEADS]
    dqkv = jnp.concatenate([dq, dk, dv], axis=1)
    drest = jnp.concatenate([dgl, dz, dpv, dxbc, dfdt], axis=1)
    du = _mm(dqkv, P["w_qkv"], tb=True, name=nm("proj_qkv_dx"))
    du = _mm(drest, P["w_rest"], tb=True, add=du, name=nm("proj_rest_dx"))
    G["w_qkv"] = _mm(sv["u"], dqkv, ta=True, out_dtype=GRAD_WIRE, name=nm("proj_qkv_dw"))
    G["w_rest"] = _mm(sv["u"], drest, ta=True, out_dtype=GRAD_WIRE, name=nm("proj_rest_dw"))
    dx, dxb, dnm = _rms_bwd(sv["x"], P["nm"], du, dx1, name=nm("rms_mix"))
    G["norm_mix"] = dnm[0]
    return dx, dxb, G


def kernel(x, norm_mix, w_in, pool_mix, pool_scale, f_bias, ssd_conv_w, ssd_conv_b, ssd_dt_bias, ssd_a_log, ssd_d, ssd_norm, p_pool, p_attn, p_ssd, w_out, norm_ffn, ffn_up, ffn_conv_w, ffn_conv_b, ffn_down, norm_final, loss_target, m_norm_mix, m_w_in, m_pool_mix, m_pool_scale, m_f_bias, m_ssd_conv_w, m_ssd_conv_b, m_ssd_dt_bias, m_ssd_a_log, m_ssd_d, m_ssd_norm, m_p_pool, m_p_attn, m_p_ssd, m_w_out, m_norm_ffn, m_ffn_up, m_ffn_conv_w, m_ffn_conv_b, m_ffn_down, m_norm_final, v_norm_mix, v_w_in, v_pool_mix, v_pool_scale, v_f_bias, v_ssd_conv_w, v_ssd_conv_b, v_ssd_dt_bias, v_ssd_a_log, v_ssd_d, v_ssd_norm, v_p_pool, v_p_attn, v_p_ssd, v_w_out, v_norm_ffn, v_ffn_up, v_ffn_conv_w, v_ffn_conv_b, v_ffn_down, v_norm_final):
    args = dict(locals())
    W = {n: args[n] for n in WEIGHTS}
    M = {n: args["m_" + n] for n in WEIGHTS}
    V = {n: args["v_" + n] for n in WEIGHTS}
    L = norm_mix.shape[0]
    shard = w_in.shape[-1]
    up_w = ffn_up.shape[-1]
    up_pad = _round_up(up_w, LANE)

    bf = lambda n: W[n].astype(BF16)
    g_win, g_up, g_pp, g_pa, g_ps, g_wo, g_dn, g_cw, g_fcw = _all_gather(
        [_pad_last(bf("w_in"), _round_up(shard, LANE)), _pad_last(bf("ffn_up"), up_pad), bf("p_pool"), bf("p_attn"),
         bf("p_ssd"), bf("w_out"), bf("ffn_down"), ssd_conv_w, ffn_conv_w], name="gather_weights")
    w_qkv, w_rest = _repack_w_in(g_win, shard, name="repack_w_in")
    p_pool_f, p_attn_f = _cols_by_device(g_pp), _cols_by_device(g_pa)
    p_ssd_f, w_out_f = _rows_by_device(g_ps), _rows_by_device(g_wo)
    half = N_DEV // 2
    w_down_f = _rows_by_device(g_dn)
    w_down_p = jnp.pad(w_down_f.reshape(L, half, up_w, -1), ((0, 0), (0, 0), (0, up_pad - up_w), (0, 0)))
    w_down_p = w_down_p.reshape(L, half * up_pad, -1)
    cw_f = _cols_by_device(g_cw)
    fcw_p = _cols_by_device(_pad_last(g_fcw, up_pad))
    fcb_p = _pad_last(ffn_conv_b.reshape(L, N_DEV, up_w), up_pad).reshape(L, 1, N_DEV * up_pad)

    params = [dict(
        nm=norm_mix[l][None], w_qkv=w_qkv[l], w_rest=w_rest[l], mix=pool_mix[l].astype(BF16),
        pscale=pool_scale[l][None], fb=_lanes(f_bias[l], 0), cw=cw_f[l], cb=ssd_conv_b[l][None],
        dtb=_lanes(ssd_dt_bias[l], DT_LANE0), alog=_lanes(ssd_a_log[l], DT_LANE0), dsk=_lanes(ssd_d[l], DT_LANE0),
        snw=ssd_norm[l][None], p_pool=p_pool_f[l], p_attn=p_attn_f[l], p_ssd=p_ssd_f[l], w_out=w_out_f[l],
        nf=norm_ffn[l][None], g_up=g_up, up_blocks=(N_DEV, up_w, up_pad), fcw=fcw_p[l], fcb=fcb_p[l],
        w_down=w_down_p[l]) for l in range(L)]
    h, saves = x[0], []
    for l in range(L):
        h, sv = _layer_fwd(h, params[l], l)
        saves.append(sv)
    loss_part, g, gb, dnfin = _loss_head(h, norm_final[None], loss_target[0], name="loss_head")
    G = [None] * L
    for l in reversed(range(L)):
        g, gb, G[l] = _layer_bwd(g, gb, saves[l], params[l], l)

    per_layer = [n for n in SHARDED if n != "w_in"]
    rep_grads = {n: jnp.stack([G[l][n] for l in range(L)]) for n in REPLICATED if n != "norm_final"}
    rep_grads["norm_final"] = dnfin[0]
    rep_size = sum(W[n].size for n in REPLICATED)
    rep_rows = _round_up(-(-rep_size // (N_DEV * FLAT_W)), SUBLANE)
    rep_send = _flat_rows([rep_grads[n] for n in REPLICATED], N_DEV * rep_rows).reshape(N_DEV, rep_rows, FLAT_W)
    dw_in = _repack_dw_in([G[l]["w_qkv"] for l in range(L)], [G[l]["w_rest"] for l in range(L)], shard,
                          name="repack_dw_in")
    sends = [dw_in] + [G[l][n] for n in per_layer for l in range(L)] + [rep_send]
    recvs = _exchange(sends, name="exchange_grads")

    out = {}

    def finish(n, recv):
        shape = W[n].shape
        as3 = lambda a: a.reshape(L, -1, shape[-1])
        res = _sum_adam(recv, as3(W[n]), as3(M[n]), as3(V[n]), name="sum_adam_" + n)
        for key, arr in zip(("grad_", "delta_", "new_m_", "new_v_"), res):
            out[key + n] = arr.reshape(shape)

    finish("w_in", recvs[0])
    for i, n in enumerate(per_layer):
        finish(n, recvs[1 + L * i:1 + L * (i + 1)])

    rep_sum = _sum_parts(recvs[-1], name="sum_replicated")
    rep_g = _all_gather([rep_sum], name="gather_replicated_grads")[0].reshape(N_DEV * rep_rows, FLAT_W)
    flat_rep = lambda D: _flat_rows([D[n] for n in REPLICATED], N_DEV * rep_rows)
    d_rp, m_rp, v_rp = _adam(rep_g, flat_rep(W), flat_rep(M), flat_rep(V), name="adam_replicated")
    off = 0
    for n in REPLICATED:
        size = W[n].size
        for key, arr in (("grad_", rep_g), ("delta_", d_rp), ("new_m_", m_rp), ("new_v_", v_rp)):
            out[key + n] = arr.reshape(-1)[off:off + size].reshape(W[n].shape)
        off += size

    loss = lax.psum(loss_part[0, 0], ("x", "y", "c"))
    return (loss, g[None], *[out["grad_" + n] for n in WEIGHTS], *[out["delta_" + n] for n in WEIGHTS],
            *[out["new_m_" + n] for n in WEIGHTS], *[out["new_v_" + n] for n in WEIGHTS])
```

```python
import functools

import jax
import jax.numpy as jnp
from jax import lax
from jax.experimental import pallas as pl
from jax.experimental.pallas import tpu as pltpu

F32 = jnp.float32
BF16 = jnp.bfloat16

LANE = 128
SUBLANE = 8
VMEM_LIMIT = 56 * 1024 * 1024

NORM_EPS = 1e-6
POOL_WINDOWS = (2, 4, 8, 16)
POOL_GROUP_DIM = 128
ATTN_HEADS = 8
ATTN_HEAD_DIM = 64
SSD_HEAD_DIM = 64
SSD_HEADS = 16
SSD_GROUPS = 2
SSD_STATE = 128
SSD_CHUNK = 128

ADAM_LR = 0.001
ADAM_B1 = 0.9
ADAM_B2 = 0.999
ADAM_EPS = 1e-08
ADAM_WD = 0.01
ADAM_STEP = 10


def _cparams(*sem):
    return pltpu.CompilerParams(dimension_semantics=tuple(sem), vmem_limit_bytes=VMEM_LIMIT)


def _tile(n, pref):
    if n <= pref:
        return n
    assert n % LANE == 0, n
    q = n // LANE
    best = 1
    for d in range(1, pref // LANE + 1):
        if q % d == 0:
            best = d
    return best * LANE


GRAD_WIRE = BF16


def _mm_call(args, in_specs, out_spec, out_shape, grid, *, ta, tb, nk, acc_shape, has_add, keep, name):
    dims = (((0 if ta else 1,), (1 if tb else 0,)), ((), ()))

    def body(*refs):
        a_ref, b_ref = refs[0], refs[1]
        o_ref = refs[2 + has_add]
        d = lax.dot_general(a_ref[...], b_ref[...], dims, preferred_element_type=F32)

        def finish(r):
            if has_add:
                r = r + refs[2][...]
            if keep is not None:
                r = r[:, :keep]
            o_ref[...] = r.astype(o_ref.dtype)

        if nk == 1:
            finish(d)
            return
        acc_ref = refs[-1]
        k = pl.program_id(2)

        @pl.when(k == 0)
        def _():
            acc_ref[...] = d

        @pl.when(jnp.logical_and(k > 0, k < nk - 1))
        def _():
            acc_ref[...] += d

        @pl.when(k == nk - 1)
        def _():
            finish(acc_ref[...] + d)

    return pl.pallas_call(
        body, name=name, grid=grid, in_specs=in_specs, out_specs=out_spec, out_shape=out_shape,
        scratch_shapes=[pltpu.VMEM(acc_shape, F32)] if nk > 1 else [],
        compiler_params=_cparams("parallel", "parallel", "arbitrary"),
    )(*args)


MM_VMEM_BUDGET = 40 * 1024 * 1024
MM_MAX_ROWS = 2048


def _mm_rows(M, tn, tk, nk, out_bytes, has_add):
    best = None
    for tm in range(LANE, min(M, MM_MAX_ROWS) + 1, LANE):
        if M % tm:
            continue
        blocks = 2 * (tm * tk * 2 + tk * tn * 2 + tm * tn * out_bytes + (tm * tn * 4 if has_add else 0))
        blocks += tm * tn * 4 * (2 if nk > 1 else 1)
        if blocks <= MM_VMEM_BUDGET:
            best = tm
    return M if best is None else best


def _mm(a, b, *, ta=False, tb=False, add=None, out_dtype=F32, name):
    assert a.dtype == BF16 and b.dtype == BF16, (a.dtype, b.dtype)
    if ta:
        K, M = a.shape
    else:
        M, K = a.shape
    if tb:
        N, K2 = b.shape
    else:
        K2, N = b.shape
    assert K == K2, (a.shape, b.shape, ta, tb)
    tn, tk = _tile(N, 1024), _tile(K, 2048)
    tm = _mm_rows(M, tn, tk, K // tk, jnp.dtype(out_dtype).itemsize, add is not None)
    a_spec =pl.BlockSpec((tk, tm), lambda i, j, k: (k, i)) if ta else pl.BlockSpec((tm, tk), lambda i, j, k: (i, k))
    b_spec = pl.BlockSpec((tn, tk), lambda i, j, k: (j, k)) if tb else pl.BlockSpec((tk, tn), lambda i, j, k: (k, j))
    in_specs, args = [a_spec, b_spec], [a, b]
    if add is not None:
        in_specs.append(pl.BlockSpec((tm, tn), lambda i, j, k: (i, j)))
        args.append(add)
    return _mm_call(args, in_specs, pl.BlockSpec((tm, tn), lambda i, j, k: (i, j)),
                    jax.ShapeDtypeStruct((M, N), out_dtype), (M // tm, N // tn, K // tk),
                    ta=ta, tb=tb, nk=K // tk, acc_shape=(tm, tn), has_add=add is not None, keep=None, name=name)


def _mm_tn_blocks(a, b, nblk, *, name):
    K, M = a.shape
    N = b.shape[1]
    bw = N // nblk
    tk = _tile(K, 2048)
    tm = _mm_rows(M, bw, tk, K // tk, jnp.dtype(GRAD_WIRE).itemsize, False)
    return _mm_call([a, b],
                    [pl.BlockSpec((tk, tm), lambda i, j, k: (k, i)), pl.BlockSpec((tk, bw), lambda i, j, k: (k, j))],
                    pl.BlockSpec((None, tm, bw), lambda i, j, k: (j, i, 0)),
                    jax.ShapeDtypeStruct((nblk, M, bw), GRAD_WIRE), (M // tm, nblk, K // tk),
                    ta=True, tb=False, nk=K // tk, acc_shape=(tm, bw), has_add=False, keep=None, name=name)


def _half_of(b, half):
    hi = jnp.where(b >= half, 1, 0)
    return hi, b - half * hi


def _mm_up_fwd(u, g_up, l, *, name):
    S, K = u.shape
    nblk, _, _, bw = g_up.shape
    tm = _mm_rows(S, bw, K, 1, 4, False)
    return _mm_call([u, g_up],
                    [pl.BlockSpec((tm, K), lambda i, j, k: (i, 0)),
                     pl.BlockSpec((None, None, K, bw), lambda i, j, k: (j, l, 0, 0))],
                    pl.BlockSpec((tm, bw), lambda i, j, k: (i, j)),
                    jax.ShapeDtypeStruct((S, nblk * bw), F32), (S // tm, nblk, 1),
                    ta=False, tb=False, nk=1, acc_shape=(tm, bw), has_add=False, keep=None, name=name)


def _mm_up_dx(dh, g_up, l, *, name):
    _, S, _ = dh.shape
    nblk, _, N, bw = g_up.shape
    half = nblk // 2
    tm = _mm_rows(S, N, bw, nblk, 4, False)
    return _mm_call([dh, g_up],
                    [pl.BlockSpec((None, tm, bw), lambda i, j, k: (_half_of(k, half)[0], i, _half_of(k, half)[1])),
                     pl.BlockSpec((None, None, N, bw), lambda i, j, k: (k, l, 0, 0))],
                    pl.BlockSpec((tm, N), lambda i, j, k: (i, 0)),
                    jax.ShapeDtypeStruct((S, N), F32), (S // tm, 1, nblk),
                    ta=False, tb=True, nk=nblk, acc_shape=(tm, N), has_add=False, keep=None, name=name)


def _mm_up_dw(u, dh, nblk, keep, *, name):
    S, M = u.shape
    half = nblk // 2
    bw = dh.shape[2] // half
    tk = _tile(S, 2048)
    tm = _mm_rows(M, bw, tk, S // tk, jnp.dtype(GRAD_WIRE).itemsize, False)
    return _mm_call([u, dh],
                    [pl.BlockSpec((tk, tm), lambda i, j, k: (k, i)),
                     pl.BlockSpec((None, tk, bw), lambda i, j, k: (_half_of(j, half)[0], k, _half_of(j, half)[1]))],
                    pl.BlockSpec((None, tm, keep), lambda i, j, k: (j, i, 0)),
                    jax.ShapeDtypeStruct((nblk, M, keep), GRAD_WIRE), (M // tm, nblk, S // tk),
                    ta=True, tb=False, nk=S // tk, acc_shape=(tm, bw), has_add=False, keep=keep, name=name)


ROW_TILE = 512
HALO = 8
POOL_HALO = 16


def _row_tile(S):
    return min(ROW_TILE, S)


def _acc_out(ref, val, first):
    @pl.when(first)
    def _():
        ref[...] = val

    @pl.when(jnp.logical_not(first))
    def _():
        ref[...] += val


def _rms_fwd(x, w, *, name):
    S, D = x.shape
    ts = _row_tile(S)

    def body(x_ref, w_ref, u_ref):
        xv = x_ref[...]
        r = lax.rsqrt(jnp.mean(xv * xv, axis=-1, keepdims=True) + NORM_EPS)
        u_ref[...] = (xv * r * w_ref[...]).astype(BF16)

    return pl.pallas_call(
        body, name=name, grid=(S // ts,),
        in_specs=[pl.BlockSpec((ts, D), lambda i: (i, 0)), pl.BlockSpec((1, D), lambda i: (0, 0))],
        out_specs=pl.BlockSpec((ts, D), lambda i: (i, 0)),
        out_shape=jax.ShapeDtypeStruct((S, D), BF16),
        compiler_params=_cparams("parallel"),
    )(x, w)


def _rms_bwd(x, w, du, g, *, name):
    S, D = x.shape
    ts = _row_tile(S)

    def body(x_ref, w_ref, du_ref, g_ref, dx_ref, dxb_ref, dw_ref):
        xv = x_ref[...]
        r = lax.rsqrt(jnp.mean(xv * xv, axis=-1, keepdims=True) + NORM_EPS)
        n = xv * r
        duv = du_ref[...]
        dn = duv * w_ref[...]
        dx = g_ref[...] + r * (dn - n * jnp.mean(dn * n, axis=-1, keepdims=True))
        dx_ref[...] = dx
        dxb_ref[...] = dx.astype(BF16)
        _acc_out(dw_ref, jnp.sum(duv * n, axis=0, keepdims=True), pl.program_id(0) == 0)

    row = pl.BlockSpec((ts, D), lambda i: (i, 0))
    vec = pl.BlockSpec((1, D), lambda i: (0, 0))
    return pl.pallas_call(
        body, name=name, grid=(S // ts,),
        in_specs=[row, vec, row, row],
        out_specs=[row, row, vec],
        out_shape=[jax.ShapeDtypeStruct((S, D), F32), jax.ShapeDtypeStruct((S, D), BF16),
                   jax.ShapeDtypeStruct((1, D), F32)],
        compiler_params=_cparams("arbitrary"),
    )(x, w, du, g)


def _loss_head(x, w, target, *, name):
    S, D = x.shape
    ts = _row_tile(S)

    def body(x_ref, w_ref, t_ref, loss_ref, dx_ref, dxb_ref, dw_ref):
        xv = x_ref[...]
        wv = w_ref[...]
        r = lax.rsqrt(jnp.mean(xv * xv, axis=-1, keepdims=True) + NORM_EPS)
        n = xv * r
        err = n * wv - t_ref[...]
        part = 0.5 * jnp.sum(jnp.mean(err * err, axis=-1, keepdims=True), axis=0, keepdims=True)
        dy = err * (1.0 / D)
        dn = dy * wv
        dx = r * (dn - n * jnp.mean(dn * n, axis=-1, keepdims=True))
        dx_ref[...] = dx
        dxb_ref[...] = dx.astype(BF16)
        first = pl.program_id(0) == 0
        _acc_out(dw_ref, jnp.sum(dy * n, axis=0, keepdims=True), first)
        _acc_out(loss_ref, jnp.broadcast_to(part, loss_ref.shape), first)

    row = pl.BlockSpec((ts, D), lambda i: (i, 0))
    vec = pl.BlockSpec((1, D), lambda i: (0, 0))
    return pl.pallas_call(
        body, name=name, grid=(S // ts,),
        in_specs=[row, vec, row],
        out_specs=[pl.BlockSpec((SUBLANE, LANE), lambda i: (0, 0)), row, row, vec],
        out_shape=[jax.ShapeDtypeStruct((SUBLANE, LANE), F32), jax.ShapeDtypeStruct((S, D), F32),
                   jax.ShapeDtypeStruct((S, D), BF16), jax.ShapeDtypeStruct((1, D), F32)],
        compiler_params=_cparams("arbitrary"),
    )(x, w, target)


def _sigmoid(x):
    return 1.0 / (1.0 + jnp.exp(-x))


def _merge_fwd(gl, ya, yb, yc, *, name):
    S, D = ya.shape
    ts = _row_tile(S)

    def body(ga_ref, gb_ref, gc_ref, ya_ref, yb_ref, yc_ref, o_ref):
        m = (_sigmoid(ga_ref[...]) * ya_ref[...] + _sigmoid(gb_ref[...]) * yb_ref[...]
             + _sigmoid(gc_ref[...]) * yc_ref[...])
        o_ref[...] = m.astype(BF16)

    row = pl.BlockSpec((ts, D), lambda i: (i, 0))
    gspec = [pl.BlockSpec((ts, D), functools.partial(lambda b, i: (i, b), b)) for b in range(3)]
    return pl.pallas_call(
        body, name=name, grid=(S // ts,),
        in_specs=gspec + [row, row, row],
        out_specs=row,
        out_shape=jax.ShapeDtypeStruct((S, D), BF16),
        compiler_params=_cparams("parallel"),
    )(gl, gl, gl, ya, yb, yc)


def _merge_bwd(gl, ya, yb, yc, dm, *, name):
    S, D = ya.shape
    ts = _row_tile(S)

    def body(ga_ref, gb_ref, gc_ref, ya_ref, yb_ref, yc_ref, dm_ref, dgl_ref, da_ref, db_ref, dc_ref):
        dmv = dm_ref[...]
        for b, (g_ref, y_ref, dy_ref) in enumerate(((ga_ref, ya_ref, da_ref), (gb_ref, yb_ref, db_ref),
                                                     (gc_ref, yc_ref, dc_ref))):
            s = _sigmoid(g_ref[...])
            dy_ref[...] = (dmv * s).astype(BF16)
            dgl_ref[:, b * D:(b + 1) * D] = (dmv * y_ref[...] * s * (1.0 - s)).astype(BF16)

    row = pl.BlockSpec((ts, D), lambda i: (i, 0))
    gspec = [pl.BlockSpec((ts, D), functools.partial(lambda b, i: (i, b), b)) for b in range(3)]
    return pl.pallas_call(
        body, name=name, grid=(S // ts,),
        in_specs=gspec + [row, row, row, row],
        out_specs=[pl.BlockSpec((ts, 3 * D), lambda i: (i, 0)), row, row, row],
        out_shape=[jax.ShapeDtypeStruct((S, 3 * D), BF16)] + [jax.ShapeDtypeStruct((S, D), BF16)] * 3,
        compiler_params=_cparams("parallel"),
    )(gl, gl, gl, ya, yb, yc, dm)


POOL_WIDTH = 512


def _pool_cnt(t, w):
    return jnp.minimum(t + 1, w).astype(F32)


def _pool_fwd(rest, col, mix, scale, *, name):
    S = rest.shape[0]
    W, G, H = POOL_WIDTH, POOL_GROUP_DIM, POOL_HALO
    ts = _row_tile(S)

    def body(v_ref, h_ref, mix_ref, sc_ref, d_ref, y_ref, ext_ref):
        i = pl.program_id(0)
        cur = v_ref[...]
        ext_ref[0:H, :] = jnp.where(i > 0, h_ref[...], 0.0)
        ext_ref[H:, :] = cur
        t = i * ts + lax.broadcasted_iota(jnp.int32, (ts, 1), 0)
        for g, w in enumerate(POOL_WINDOWS):
            cols = slice(g * G, (g + 1) * G)
            acc = cur[:, cols]
            for k in range(1, w):
                acc = acc + ext_ref[pl.ds(H - k, ts), cols]
            db = (acc / _pool_cnt(t, w) - cur[:, cols]).astype(BF16)
            d_ref[:, cols] = db
            y = jnp.dot(db, mix_ref[g], preferred_element_type=F32) * sc_ref[:, cols]
            y_ref[:, cols] = y.astype(BF16)

    row = pl.BlockSpec((ts, W), lambda i: (i, 0))
    return pl.pallas_call(
        body, name=name, grid=(S // ts,),
        in_specs=[pl.BlockSpec((ts, W), lambda i: (i, col)),
                  pl.BlockSpec((H, W), lambda i: (jnp.maximum(i * (ts // H) - 1, 0), col)),
                  pl.BlockSpec((len(POOL_WINDOWS), G, G), lambda i: (0, 0, 0)),
                  pl.BlockSpec((1, W), lambda i: (0, 0))],
        out_specs=[row, row],
        out_shape=[jax.ShapeDtypeStruct((S, W), BF16)] * 2,
        scratch_shapes=[pltpu.VMEM((H + ts, W), F32)],
        compiler_params=_cparams("parallel"),
    )(rest, rest, mix, scale)


def _pool_bwd(dy, d, mix, scale, *, name):
    S = dy.shape[0]
    W, G, H = POOL_WIDTH, POOL_GROUP_DIM, POOL_HALO
    ts = _row_tile(S)
    n = S // ts
    NT = (((1,), (1,)), ((), ()))
    TN = (((0,), (0,)), ((), ()))

    def body(dy_ref, dyn_ref, d_ref, mix_ref, sc_ref, dv_ref, dmix_ref, dsc_ref, ext_ref):
        i = pl.program_id(0)

        @pl.when(i == 0)
        def _():
            dmix_ref[...] = jnp.zeros_like(dmix_ref)
            dsc_ref[...] = jnp.zeros_like(dsc_ref)

        t = i * ts + lax.broadcasted_iota(jnp.int32, (ts, 1), 0)
        tn = (i + 1) * ts + lax.broadcasted_iota(jnp.int32, (H, 1), 0)
        for g, w in enumerate(POOL_WINDOWS):
            cols = slice(g * G, (g + 1) * G)
            sc = sc_ref[:, cols]
            dyv = dy_ref[:, cols]
            db = d_ref[:, cols]
            mg = mix_ref[g]
            yp = jnp.dot(db, mg, preferred_element_type=F32)
            dsc_ref[:, cols] += jnp.sum(dyv * yp, axis=0, keepdims=True)
            dyp = (dyv * sc).astype(BF16)
            dd = lax.dot_general(dyp, mg, NT, preferred_element_type=F32)
            dmix_ref[g] += lax.dot_general(db, dyp, TN, preferred_element_type=F32)
            dyn = (jnp.where(i < n - 1, dyn_ref[:, cols], 0.0) * sc).astype(BF16)
            ddn = lax.dot_general(dyn, mg, NT, preferred_element_type=F32)
            e = dd / _pool_cnt(t, w)
            ext_ref[0:ts, cols] = e
            ext_ref[ts:, cols] = ddn / _pool_cnt(tn, w)
            acc = e
            for k in range(1, w):
                acc = acc + ext_ref[pl.ds(k, ts), cols]
            dv_ref[:, cols] = (acc - dd).astype(BF16)

    row = pl.BlockSpec((ts, W), lambda i: (i, 0))
    return pl.pallas_call(
        body, name=name, grid=(n,),
        in_specs=[row,
                  pl.BlockSpec((H, W), lambda i: (jnp.minimum((i + 1) * (ts // H), S // H - 1), 0)),
                  row,
                  pl.BlockSpec((len(POOL_WINDOWS), G, G), lambda i: (0, 0, 0)),
                  pl.BlockSpec((1, W), lambda i: (0, 0))],
        out_specs=[row, pl.BlockSpec((len(POOL_WINDOWS), G, G), lambda i: (0, 0, 0)),
                   pl.BlockSpec((1, W), lambda i: (0, 0))],
        out_shape=[jax.ShapeDtypeStruct((S, W), BF16),
                   jax.ShapeDtypeStruct((len(POOL_WINDOWS), G, G), F32),
                   jax.ShapeDtypeStruct((1, W), F32)],
        scratch_shapes=[pltpu.VMEM((ts + H, W), F32)],
        compiler_params=_cparams("arbitrary"),
    )(dy, dy, d, mix, scale)


def _conv_taps(ext_ref, w_ref, b_ref, rows, first):
    K = w_ref.shape[0]
    pre = b_ref[...] + w_ref[K - 1:K, :] * ext_ref[pl.ds(first, rows), :]
    for k in range(K - 1):
        pre = pre + w_ref[k:k + 1, :] * ext_ref[pl.ds(first - (K - 1) + k, rows), :]
    return pre


def _prev_halo_spec(ts, tc, colfn):
    return pl.BlockSpec((HALO, tc), lambda *g: (jnp.maximum(g[-1] * (ts // HALO) - 1, 0), colfn(*g)))


def _conv_silu_fwd(rest, col0, C, w, b, *, name):
    S = rest.shape[0]
    ts, tc = _row_tile(S), 512
    assert col0 % tc == 0 and C % tc == 0
    cb = col0 // tc

    def body(x_ref, h_ref, w_ref, b_ref, o_ref, ext_ref):
        i = pl.program_id(1)
        ext_ref[0:HALO, :] = jnp.where(i > 0, h_ref[...], 0.0)
        ext_ref[HALO:, :] = x_ref[...]
        pre = _conv_taps(ext_ref, w_ref, b_ref, ts, HALO)
        o_ref[...] = pre * _sigmoid(pre)

    K = w.shape[0]
    return pl.pallas_call(
        body, name=name, grid=(C // tc, S // ts),
        in_specs=[pl.BlockSpec((ts, tc), lambda j, i: (i, cb + j)),
                  _prev_halo_spec(ts, tc, lambda j, i: cb + j),
                  pl.BlockSpec((K, tc), lambda j, i: (0, j)),
                  pl.BlockSpec((1, tc), lambda j, i: (0, j))],
        out_specs=pl.BlockSpec((ts, tc), lambda j, i: (i, j)),
        out_shape=jax.ShapeDtypeStruct((S, C), F32),
        scratch_shapes=[pltpu.VMEM((HALO + ts, tc), F32)],
        compiler_params=_cparams("parallel", "parallel"),
    )(rest, rest, w, b)


def _dsilu(pre):
    s = _sigmoid(pre)
    return s, s * (1.0 + pre * (1.0 - s))


def _conv_silu_bwd(rest, col0, C, w, b, dy, *, name):
    S = rest.shape[0]
    ts, tc = _row_tile(S), 512
    cb = col0 // tc
    n = S // ts
    K = w.shape[0]
    R = ts + HALO

    def body(x_ref, hp_ref, hn_ref, w_ref, b_ref, dy_ref, dyn_ref, dx_ref, dw_ref, db_ref, ext_ref, g_ref):
        i = pl.program_id(1)
        last = i == n - 1
        ext_ref[0:HALO, :] = jnp.where(i > 0, hp_ref[...], 0.0)
        ext_ref[HALO:HALO + ts, :] = x_ref[...]
        ext_ref[HALO + ts:, :] = jnp.where(last, 0.0, hn_ref[...])
        g_ref[0:ts, :] = dy_ref[...]
        g_ref[ts:, :] = jnp.where(last, 0.0, dyn_ref[...])
        pre = _conv_taps(ext_ref, w_ref, b_ref, R, HALO)
        g_ref[...] = g_ref[...] * _dsilu(pre)[1]
        dx = w_ref[K - 1:K, :] * g_ref[pl.ds(0, ts), :]
        for k in range(K - 1):
            dx = dx + w_ref[k:k + 1, :] * g_ref[pl.ds(K - 1 - k, ts), :]
        dx_ref[...] = dx.astype(BF16)
        gc = g_ref[pl.ds(0, ts), :]
        first = i == 0
        _acc_out(db_ref, jnp.sum(gc, axis=0, keepdims=True), first)
        dws = [jnp.sum(gc * ext_ref[pl.ds(HALO - (K - 1) + k, ts), :], axis=0, keepdims=True) for k in range(K)]
        _acc_out(dw_ref, jnp.concatenate(dws, axis=0), first)

    nxt = lambda cf: pl.BlockSpec((HALO, tc), lambda j, i: (jnp.minimum((i + 1) * (ts // HALO), S // HALO - 1), cf(j)))
    return pl.pallas_call(
        body, name=name, grid=(C // tc, n),
        in_specs=[pl.BlockSpec((ts, tc), lambda j, i: (i, cb + j)),
                  _prev_halo_spec(ts, tc, lambda j, i: cb + j),
                  nxt(lambda j: cb + j),
                  pl.BlockSpec((K, tc), lambda j, i: (0, j)),
                  pl.BlockSpec((1, tc), lambda j, i: (0, j)),
                  pl.BlockSpec((ts, tc), lambda j, i: (i, j)),
                  nxt(lambda j: j)],
        out_specs=[pl.BlockSpec((ts, tc), lambda j, i: (i, j)),
                   pl.BlockSpec((K, tc), lambda j, i: (0, j)),
                   pl.BlockSpec((1, tc), lambda j, i: (0, j))],
        out_shape=[jax.ShapeDtypeStruct((S, C), BF16), jax.ShapeDtypeStruct((K, C), F32),
                   jax.ShapeDtypeStruct((1, C), F32)],
        scratch_shapes=[pltpu.VMEM((HALO + R, tc), F32), pltpu.VMEM((R, tc), F32)],
        compiler_params=_cparams("parallel", "arbitrary"),
    )(rest, rest, rest, w, b, dy, dy)


FFN_TC = 256


def _ffn_act_fwd(h0, w, b, *, name):
    S, F2 = h0.shape
    F = F2 // 2
    ts, tc = _row_tile(S), FFN_TC
    nb = F // tc
    K = w.shape[0]

    def body(xg_ref, hg_ref, xv_ref, hv_ref, wg_ref, wv_ref, bg_ref, bv_ref, o_ref, eg_ref, ev_ref):
        i = pl.program_id(1)
        for x_ref, h_ref, e_ref in ((xg_ref, hg_ref, eg_ref), (xv_ref, hv_ref, ev_ref)):
            e_ref[0:HALO, :] = jnp.where(i > 0, h_ref[...], 0.0)
            e_ref[HALO:, :] = x_ref[...]
        pg = _conv_taps(eg_ref, wg_ref, bg_ref, ts, HALO)
        pv = _conv_taps(ev_ref, wv_ref, bv_ref, ts, HALO)
        o_ref[...] = (pg * _sigmoid(pg) * pv).astype(BF16)

    def half(off):
        return [pl.BlockSpec((ts, tc), lambda j, i: (i, off + j)), _prev_halo_spec(ts, tc, lambda j, i: off + j)]

    wspec = lambda off: pl.BlockSpec((K, tc), lambda j, i: (0, off + j))
    bspec = lambda off: pl.BlockSpec((1, tc), lambda j, i: (0, off + j))
    return pl.pallas_call(
        body, name=name, grid=(nb, S // ts),
        in_specs=half(0) + half(nb) + [wspec(0), wspec(nb), bspec(0), bspec(nb)],
        out_specs=pl.BlockSpec((ts, tc), lambda j, i: (i, j)),
        out_shape=jax.ShapeDtypeStruct((S, F), BF16),
        scratch_shapes=[pltpu.VMEM((HALO + ts, tc), F32)] * 2,
        compiler_params=_cparams("parallel", "parallel"),
    )(h0, h0, h0, h0, w, w, b, b)


def _ffn_act_bwd(h0, w, b, da, *, name):
    S, F2 = h0.shape
    F = F2 // 2
    ts, tc = _row_tile(S), FFN_TC
    nb = F // tc
    n = S // ts
    K = w.shape[0]
    R = ts + HALO

    def body(xg_ref, pg_ref, ng_ref, xv_ref, pv_ref, nv_ref, wg_ref, wv_ref, bg_ref, bv_ref, da_ref, dan_ref,
             dx_ref, dw_ref, db_ref, eg_ref, ev_ref, gg_ref, gv_ref):
        i = pl.program_id(1)
        last = i == n - 1
        for x_ref, p_ref, n_ref, e_ref in ((xg_ref, pg_ref, ng_ref, eg_ref), (xv_ref, pv_ref, nv_ref, ev_ref)):
            e_ref[0:HALO, :] = jnp.where(i > 0, p_ref[...], 0.0)
            e_ref[HALO:HALO + ts, :] = x_ref[...]
            e_ref[HALO + ts:, :] = jnp.where(last, 0.0, n_ref[...])
        gg_ref[0:ts, :] = da_ref[...]
        gg_ref[ts:, :] = jnp.where(last, 0.0, dan_ref[...])
        dav = gg_ref[...]
        pg = _conv_taps(eg_ref, wg_ref, bg_ref, R, HALO)
        pv = _conv_taps(ev_ref, wv_ref, bv_ref, R, HALO)
        s, ds = _dsilu(pg)
        gg_ref[...] = dav * pv * ds
        gv_ref[...] = dav * pg * s
        first = i == 0
        for h, (g_ref, e_ref, w_ref) in enumerate(((gg_ref, eg_ref, wg_ref), (gv_ref, ev_ref, wv_ref))):
            dx = w_ref[K - 1:K, :] * g_ref[pl.ds(0, ts), :]
            for k in range(K - 1):
                dx = dx + w_ref[k:k + 1, :] * g_ref[pl.ds(K - 1 - k, ts), :]
            dx_ref[h] = dx.astype(BF16)
            gc = g_ref[pl.ds(0, ts), :]
            dws = [jnp.sum(gc * e_ref[pl.ds(HALO - (K - 1) + k, ts), :], axis=0, keepdims=True) for k in range(K)]
            _acc_out(db_ref.at[h], jnp.sum(gc, axis=0, keepdims=True), first)
            _acc_out(dw_ref.at[h], jnp.concatenate(dws, axis=0), first)

    nxt = lambda off: pl.BlockSpec((HALO, tc), lambda j, i: (jnp.minimum((i + 1) * (ts // HALO), S // HALO - 1), off + j))

    def half(off):
        return [pl.BlockSpec((ts, tc), lambda j, i: (i, off + j)), _prev_halo_spec(ts, tc, lambda j, i: off + j), nxt(off)]

    wspec = lambda off: pl.BlockSpec((K, tc), lambda j, i: (0, off + j))
    bspec = lambda off: pl.BlockSpec((1, tc), lambda j, i: (0, off + j))
    return pl.pallas_call(
        body, name=name, grid=(nb, n),
        in_specs=half(0) + half(nb) + [wspec(0), wspec(nb), bspec(0), bspec(nb),
                                       pl.BlockSpec((ts, tc), lambda j, i: (i, j)), nxt(0)],
        out_specs=[pl.BlockSpec((2, ts, tc), lambda j, i: (0, i, j)),
                   pl.BlockSpec((2, K, tc), lambda j, i: (0, 0, j)),
                   pl.BlockSpec((2, 1, tc), lambda j, i: (0, 0, j))],
        out_shape=[jax.ShapeDtypeStruct((2, S, F), BF16), jax.ShapeDtypeStruct((2, K, F), F32),
                   jax.ShapeDtypeStruct((2, 1, F), F32)],
        scratch_shapes=[pltpu.VMEM((HALO + R, tc), F32)] * 2 + [pltpu.VMEM((R, tc), F32)] * 2,
        compiler_params=_cparams("parallel", "arbitrary"),
    )(h0, h0, h0, h0, h0, h0, w, w, b, b, da, da)


CUM_TILE = 256
ATT_TILE = 512
ATT_TILE_FWD = 1024
NEG = -1e30
N_PAIR = ATTN_HEADS // 2
NT_DIMS = (((1,), (1,)), ((), ()))


def _split3(x):
    a = x.astype(BF16)
    r = x - a.astype(F32)
    b = r.astype(BF16)
    c = (r - b.astype(F32)).astype(BF16)
    return a, b, c


def _tri_dot(tri, x):
    return sum(jnp.dot(tri, p, preferred_element_type=F32) for p in _split3(x))


def _log_sigmoid(x):
    return jnp.minimum(x, 0.0) - jnp.log(1.0 + jnp.exp(-jnp.abs(x)))


def _col(v, idx, lane):
    return jnp.sum(jnp.where(lane == idx, v, 0.0), axis=1, keepdims=True)


def _fox_pre(rest, col, fb, *, name):
    S = rest.shape[0]
    ts = min(CUM_TILE, S)

    def body(f_ref, fb_ref, ccol_ref, carry_ref):
        i = pl.program_id(0)

        @pl.when(i == 0)
        def _():
            carry_ref[...] = jnp.zeros_like(carry_ref)

        lane = lax.broadcasted_iota(jnp.int32, (1, LANE), 1)
        logf = jnp.where(lane < ATTN_HEADS, _log_sigmoid(f_ref[...] + fb_ref[...]), 0.0)
        r = lax.broadcasted_iota(jnp.int32, (ts, ts), 0)
        c = lax.broadcasted_iota(jnp.int32, (ts, ts), 1)
        tri = jnp.where(c <= r, 1.0, 0.0).astype(BF16)
        cs = _tri_dot(tri, logf) + carry_ref[...]
        ccol_ref[...] = cs
        carry_ref[...] = cs[ts - 1:ts, :]

    return pl.pallas_call(
        body, name=name, grid=(S // ts,),
        in_specs=[pl.BlockSpec((ts, LANE), lambda i: (i, col)), pl.BlockSpec((1, LANE), lambda i: (0, 0))],
        out_specs=pl.BlockSpec((ts, LANE), lambda i: (i, 0)),
        out_shape=jax.ShapeDtypeStruct((S, LANE), F32),
        scratch_shapes=[pltpu.VMEM((1, LANE), F32)],
        compiler_params=_cparams("arbitrary"),
    )(rest, fb)


def _fox_post(dcq, dck, rest, col, fb, ddt, *, name):
    S = rest.shape[0]
    ts = min(CUM_TILE, S)
    n = S // ts

    def body(dcq_ref, dck_ref, f_ref, fb_ref, ddt_ref, o_ref, db_ref, carry_ref):
        i = pl.program_id(0)

        @pl.when(i == 0)
        def _():
            carry_ref[...] = jnp.zeros_like(carry_ref)

        lane = lax.broadcasted_iota(jnp.int32, (1, LANE), 1)
        dc = jnp.zeros((ts, LANE), F32)
        for h in range(ATTN_HEADS):
            d = dcq_ref[h // 2] - dck_ref[h // 2]
            dc = jnp.where(lane == h, _col(d, h % 2, lane), dc)
        r = lax.broadcasted_iota(jnp.int32, (ts, ts), 0)
        c = lax.broadcasted_iota(jnp.int32, (ts, ts), 1)
        tri = jnp.where(c >= r, 1.0, 0.0).astype(BF16)
        rc = _tri_dot(tri, dc) + carry_ref[...]
        carry_ref[...] = rc[0:1, :]
        df = rc * _sigmoid(-(f_ref[...] + fb_ref[...]))
        out = jnp.where(lane < ATTN_HEADS, df, ddt_ref[...])
        o_ref[...] = out.astype(BF16)
        _acc_out(db_ref, jnp.sum(out, axis=0, keepdims=True), i == 0)

    rev = lambda i: n - 1 - i
    return pl.pallas_call(
        body, name=name, grid=(n,),
        in_specs=[pl.BlockSpec((N_PAIR, ts, LANE), lambda i: (0, rev(i), 0)),
                  pl.BlockSpec((N_PAIR, ts, LANE), lambda i: (0, rev(i), 0)),
                  pl.BlockSpec((ts, LANE), lambda i: (rev(i), col)),
                  pl.BlockSpec((1, LANE), lambda i: (0, 0)),
                  pl.BlockSpec((ts, LANE), lambda i: (rev(i), 0))],
        out_specs=[pl.BlockSpec((ts, LANE), lambda i: (rev(i), 0)), pl.BlockSpec((1, LANE), lambda i: (0, 0))],
        out_shape=[jax.ShapeDtypeStruct((S, LANE), BF16), jax.ShapeDtypeStruct((1, LANE), F32)],
        scratch_shapes=[pltpu.VMEM((1, LANE), F32)],
        compiler_params=_cparams("arbitrary"),
    )(dcq, dck, rest, fb, ddt)


def _head_masks():
    lane = lax.broadcasted_iota(jnp.int32, (1, LANE), 1)
    return lane, (lane < ATTN_HEAD_DIM, lane >= ATTN_HEAD_DIM)


def _causal_mask(t):
    r = lax.broadcasted_iota(jnp.int32, (t, t), 0)
    c = lax.broadcasted_iota(jnp.int32, (t, t), 1)
    return r, c


AUG_ONE = 0
AUG_C = 3


def _free_lane0(hh):
    return ATTN_HEAD_DIM if hh == 0 else 0


def _augment(x, head_mask, lane, f0, c, key_side):
    terms = _split3(c)
    one = jnp.ones((), BF16)
    ones_at, terms_at = (AUG_ONE, AUG_C) if key_side else (AUG_C, AUG_ONE)
    out = jnp.where(head_mask, x, jnp.zeros((), BF16))
    for k in range(3):
        out = jnp.where(lane == f0 + ones_at + k, one, out)
        out = jnp.where(lane == f0 + terms_at + k, terms[k], out)
    return out


def _carry_start(kind, in_refs, out_refs, sems, first):
    @pl.when(first)
    def _():
        _a2a_start(kind, in_refs, out_refs, sems)


def _carry_wait(kind, in_refs, out_refs, sems, last):
    @pl.when(last)
    def _():
        _a2a_wait(kind, in_refs, out_refs, sems)


def _attn_fwd(qkv, ccol, *, comm=None, name):
    S = qkv.shape[0]
    t = min(ATT_TILE_FWD, S)
    n = S // t
    scale = ATTN_HEAD_DIM ** -0.5

    nc = len(comm[1]) if comm else 0

    def body(*refs):
        q_ref, k_ref, v_ref, cc_ref = refs[:4]
        o_ref, lrow_ref = refs[4 + nc:6 + nc]
        kaug_ref, vt_ref, m_ref, l_ref, acc_ref = refs[6 + 2 * nc:11 + 2 * nc]
        j, i = pl.program_id(0), pl.program_id(1)
        lane, masks = _head_masks()
        if comm:
            _carry_start(comm[0], refs[4:4 + nc], refs[6 + nc:6 + 2 * nc], refs[11 + 2 * nc:],
                         jnp.logical_and(j == 0, i == 0))

        @pl.when(i == 0)
        def _():
            def fill(b, carry):
                off = pl.multiple_of(b * t, t)
                kblk = k_ref[pl.ds(off, t), :]
                ccb = cc_ref[pl.ds(off, t), :]
                for hh in range(2):
                    ck = _col(ccb, 2 * j + hh, lane)
                    kaug_ref[hh, pl.ds(off, t), :] = _augment(kblk, masks[hh], lane, _free_lane0(hh), -ck, True)
                vt_ref[:, pl.ds(off, t)] = v_ref[pl.ds(off, t), :].astype(F32).T.astype(BF16)
                return carry
            lax.fori_loop(0, n, fill, 0)

        q2 = q_ref[...] * scale
        ccq = cc_ref[pl.ds(pl.multiple_of(i * t, t), t), :]
        qaug = [_augment(q2, masks[hh], lane, _free_lane0(hh), _col(ccq, 2 * j + hh, lane), False) for hh in range(2)]
        m_ref[...] = jnp.full(m_ref.shape, NEG, F32)
        l_ref[...] = jnp.zeros_like(l_ref)
        acc_ref[...] = jnp.zeros_like(acc_ref)

        def step(kb, masked):
            off = pl.multiple_of(kb * t, t)
            vt = vt_ref[:, pl.ds(off, t)]
            for hh in range(2):
                st = lax.dot_general(kaug_ref[hh, pl.ds(off, t), :], qaug[hh], NT_DIMS, preferred_element_type=F32)
                if masked:
                    r, c = _causal_mask(t)
                    st = jnp.where(r <= c, st, NEG)
                m_old = m_ref[hh]
                m_new = jnp.maximum(m_old, jnp.max(st, axis=0, keepdims=True))
                p = jnp.exp(st - m_new)
                alpha = jnp.exp(m_old - m_new)
                l_ref[hh] = alpha * l_ref[hh] + jnp.sum(p, axis=0, keepdims=True)
                acc_ref[hh] = alpha * acc_ref[hh] + jnp.dot(vt, p.astype(BF16), preferred_element_type=F32)
                m_ref[hh] = m_new

        def loop_body(kb, carry):
            step(kb, False)
            return carry

        lax.fori_loop(0, i, loop_body, 0)
        step(i, True)
        drow = lax.broadcasted_iota(jnp.int32, (LANE, 1), 0)
        ot = jnp.where(drow < ATTN_HEAD_DIM, acc_ref[0] / l_ref[0], acc_ref[1] / l_ref[1])
        o_ref[...] = ot.T.astype(BF16)
        for hh in range(2):
            lrow_ref[hh] = m_ref[hh] + jnp.log(l_ref[hh])
        if comm:
            _carry_wait(comm[0], refs[4:4 + nc], refs[6 + nc:6 + 2 * nc], refs[11 + 2 * nc:],
                        jnp.logical_and(j == N_PAIR - 1, i == n - 1))

    res = pl.pallas_call(
        body, name=name, grid=(N_PAIR, n),
        in_specs=[pl.BlockSpec((t, LANE), lambda j, i: (i, j)),
                  pl.BlockSpec((S, LANE), lambda j, i: (0, N_PAIR + j)),
                  pl.BlockSpec((S, LANE), lambda j, i: (0, 2 * N_PAIR + j)),
                  pl.BlockSpec((S, LANE), lambda j, i: (0, 0))] + [ANY] * nc,
        out_specs=[pl.BlockSpec((t, LANE), lambda j, i: (i, j)),
                   pl.BlockSpec((2, 1, t), lambda j, i: (j, 0, i))] + [ANY] * nc,
        out_shape=[jax.ShapeDtypeStruct((S, N_PAIR * LANE), BF16), jax.ShapeDtypeStruct((ATTN_HEADS, 1, S), F32)]
        + (_a2a_out_shapes(*comm) if comm else []),
        scratch_shapes=[pltpu.VMEM((2, S, LANE), BF16), pltpu.VMEM((LANE, S), BF16),
                        pltpu.VMEM((2, 1, t), F32), pltpu.VMEM((2, 1, t), F32), pltpu.VMEM((2, LANE, t), F32)]
        + (_a2a_sems(nc) if comm else []),
        compiler_params=_cparams("arbitrary" if comm else "parallel", "arbitrary"),
    )(qkv, qkv, qkv, ccol, *(comm[1] if comm else []))
    return res[0], res[1], list(res[2:])


def _attn_bwd(qkv, o, do, lrow, ccol, *, comm=None, name):
    S = qkv.shape[0]
    t = min(ATT_TILE, S)
    n = S // t
    scale = ATTN_HEAD_DIM ** -0.5

    nc = len(comm[1]) if comm else 0

    def body(*refs):
        q_ref, k_ref, v_ref, o_ref, do_ref, lrow_ref, cc_ref = refs[:7]
        dq_ref, dk_ref, dv_ref, dcq_ref, dck_ref = refs[7 + nc:12 + nc]
        qaug_ref, drow_ref, dqt_ref, dka_ref, dva_ref = refs[12 + 2 * nc:17 + 2 * nc]
        carried = (refs[7:7 + nc], refs[12 + nc:12 + 2 * nc], refs[17 + 2 * nc:])
        j, kb = pl.program_id(0), pl.program_id(1)
        lane, masks = _head_masks()
        if comm:
            _carry_start(comm[0], *carried, jnp.logical_and(j == 0, kb == 0))
        drows = lax.broadcasted_iota(jnp.int32, (LANE, 1), 0)

        @pl.when(kb == 0)
        def _():
            dqt_ref[...] = jnp.zeros_like(dqt_ref)

            def fill(b, carry):
                off = pl.multiple_of(b * t, t)
                q2 = q_ref[pl.ds(off, t), :] * scale
                ccb = cc_ref[pl.ds(off, t), :]
                prod_t = (do_ref[pl.ds(off, t), :].astype(F32) * o_ref[pl.ds(off, t), :].astype(F32)).T
                lo = jnp.sum(jnp.where(drows < ATTN_HEAD_DIM, prod_t, 0.0), axis=0, keepdims=True)
                drow_ref[0, :, pl.ds(off, t)] = lo
                drow_ref[1, :, pl.ds(off, t)] = jnp.sum(prod_t, axis=0, keepdims=True) - lo
                for hh in range(2):
                    cq = _col(ccb, 2 * j + hh, lane)
                    qaug_ref[hh, pl.ds(off, t), :] = _augment(q2, masks[hh], lane, _free_lane0(hh), cq, False)
                return carry
            lax.fori_loop(0, n, fill, 0)

        koff = pl.multiple_of(kb * t, t)
        kblk = k_ref[...]
        v2 = v_ref[...]
        cck = cc_ref[pl.ds(koff, t), :]
        kaug = [_augment(kblk, masks[hh], lane, _free_lane0(hh), -_col(cck, 2 * j + hh, lane), True) for hh in range(2)]
        kaug_t = [ka.astype(F32).T.astype(BF16) for ka in kaug]
        vh = [jnp.where(mk, v2, jnp.zeros((), BF16)) for mk in masks]
        dka_ref[...] = jnp.zeros_like(dka_ref)
        dva_ref[...] = jnp.zeros_like(dva_ref)

        def step(qb, masked):
            off = pl.multiple_of(qb * t, t)
            doblk = do_ref[pl.ds(off, t), :]
            for hh in range(2):
                qa = qaug_ref[hh, pl.ds(off, t), :]
                st = lax.dot_general(kaug[hh], qa, NT_DIMS, preferred_element_type=F32)
                if masked:
                    r, c = _causal_mask(t)
                    st = jnp.where(r <= c, st, NEG)
                pt = jnp.exp(st - lrow_ref[hh, :, pl.ds(off, t)])
                dpt = lax.dot_general(vh[hh], doblk, NT_DIMS, preferred_element_type=F32)
                dst = (pt * (dpt - drow_ref[hh, :, pl.ds(off, t)])).astype(BF16)
                dva_ref[hh] += jnp.dot(pt.astype(BF16), doblk, preferred_element_type=F32)
                dka_ref[hh] += jnp.dot(dst, qa, preferred_element_type=F32)
                dqt_ref[hh, :, pl.ds(off, t)] += jnp.dot(kaug_t[hh], dst, preferred_element_type=F32)

        step(kb, True)

        def loop_body(qb, carry):
            step(qb, False)
            return carry

        lax.fori_loop(kb + 1, n, loop_body, 0)
        dk_ref[...] = jnp.where(masks[0], dka_ref[0], dka_ref[1]).astype(BF16)
        dv_ref[...] = jnp.where(masks[0], dva_ref[0], dva_ref[1]).astype(BF16)
        dck = [_col(dka_ref[hh], _free_lane0(hh) + AUG_C, lane) for hh in range(2)]
        dck_ref[0] = jnp.where(lane == 0, dck[0], jnp.where(lane == 1, dck[1], 0.0))

        @pl.when(kb == n - 1)
        def _():
            def flush(b, carry):
                off = pl.multiple_of(b * t, t)
                d = [dqt_ref[hh, :, pl.ds(off, t)].T for hh in range(2)]
                dq_ref[pl.ds(off, t), :] = (jnp.where(masks[0], d[0], d[1]) * scale).astype(BF16)
                dcq = [_col(d[hh], _free_lane0(hh) + AUG_ONE, lane) for hh in range(2)]
                dcq_ref[0, pl.ds(off, t), :] = jnp.where(lane == 0, dcq[0], jnp.where(lane == 1, dcq[1], 0.0))
                return carry
            lax.fori_loop(0, n, flush, 0)

        if comm:
            _carry_wait(comm[0], *carried, jnp.logical_and(j == N_PAIR - 1, kb == n - 1))

    full = lambda cb: pl.BlockSpec((S, LANE), lambda j, kb: (0, cb(j)))
    kspec = lambda base: pl.BlockSpec((t, LANE), lambda j, kb: (kb, base + j))
    oblk = pl.BlockSpec((t, LANE), lambda j, kb: (kb, j))
    res = pl.pallas_call(
        body, name=name, grid=(N_PAIR, n),
        in_specs=[full(lambda j: j), kspec(N_PAIR), kspec(2 * N_PAIR), full(lambda j: j), full(lambda j: j),
                  pl.BlockSpec((2, 1, S), lambda j, kb: (j, 0, 0)), full(lambda j: 0)] + [ANY] * nc,
        out_specs=[full(lambda j: j), oblk, oblk,
                   pl.BlockSpec((1, S, LANE), lambda j, kb: (j, 0, 0)),
                   pl.BlockSpec((1, t, LANE), lambda j, kb: (j, kb, 0))] + [ANY] * nc,
        out_shape=[jax.ShapeDtypeStruct((S, N_PAIR * LANE), BF16)] * 3 + [jax.ShapeDtypeStruct((N_PAIR, S, LANE), F32)] * 2
        + (_a2a_out_shapes(*comm) if comm else []),
        scratch_shapes=[pltpu.VMEM((2, S, LANE), BF16), pltpu.VMEM((2, 1, S), F32), pltpu.VMEM((2, LANE, S), F32),
                        pltpu.VMEM((2, t, LANE), F32), pltpu.VMEM((2, t, LANE), F32)]
        + (_a2a_sems(nc) if comm else []),
        compiler_params=_cparams("arbitrary" if comm else "parallel", "arbitrary"),
    )(qkv, qkv, qkv, o, do, lrow, ccol, *(comm[1] if comm else []))
    return tuple(res[:5]) + (list(res[5:]),)


DT_LANE0 = ATTN_HEADS
SSD_PAIRS = SSD_HEADS // 2
SSD_X = SSD_HEADS * SSD_HEAD_DIM
SSD_B0 = SSD_X
SSD_C0 = SSD_X + SSD_GROUPS * SSD_STATE
SSD_CH = SSD_X + 2 * SSD_GROUPS * SSD_STATE
TN_DIMS = (((0,), (0,)), ((), ()))


def _softplus(x):
    return jnp.maximum(x, 0.0) + jnp.log(1.0 + jnp.exp(-jnp.abs(x)))


def _ssd_prep(fdt, dtb, alog):
    L = fdt.shape[0]
    lane = lax.broadcasted_iota(jnp.int32, (1, LANE), 1)
    hl = jnp.logical_and(lane >= DT_LANE0, lane < DT_LANE0 + SSD_HEADS)
    dtv = jnp.where(hl, _softplus(fdt + dtb), 0.0)
    A = jnp.where(hl, -jnp.exp(alog), 0.0)
    r = lax.broadcasted_iota(jnp.int32, (L, L), 0)
    c = lax.broadcasted_iota(jnp.int32, (L, L), 1)
    cs = _tri_dot(jnp.where(c <= r, 1.0, 0.0).astype(BF16), dtv * A)
    return lane, hl, dtv, A, cs, r, c


def _halves(lane, v0, v1):
    return jnp.where(lane < SSD_HEAD_DIM, v0, v1)


def _ssd_fwd(xc, rest, col, dtb, alog, dskip, *, name):
    S = xc.shape[0]
    L = SSD_CHUNK
    nc = S // L

    def body(xc_ref, f_ref, dtb_ref, al_ref, dk_ref, y_ref, hs_ref, h_ref, cst_ref):
        i = pl.program_id(0)

        @pl.when(i == 0)
        def _():
            h_ref[...] = jnp.zeros_like(h_ref)

        lane, hl, dtv, A, cs, r, c = _ssd_prep(f_ref[...], dtb_ref[...], al_ref[...])
        cst_ref[...] = cs.T
        cs_last = cs[L - 1:L, :]
        ecs = jnp.exp(cs)
        dec = jnp.exp(cs_last - cs)
        cd = jnp.exp(cs_last)
        dkv = dk_ref[...]
        prow = lax.broadcasted_iota(jnp.int32, (LANE, 1), 0)
        for g in range(SSD_GROUPS):
            Bg = xc_ref[:, SSD_B0 + g * SSD_STATE:SSD_B0 + (g + 1) * SSD_STATE].astype(BF16)
            Cg = xc_ref[:, SSD_C0 + g * SSD_STATE:SSD_C0 + (g + 1) * SSD_STATE].astype(BF16)
            CB = lax.dot_general(Cg, Bg, NT_DIMS, preferred_element_type=F32)
            for pp in range(SSD_PAIRS // SSD_GROUPS):
                pi = g * (SSD_PAIRS // SSD_GROUPS) + pp
                hl0 = DT_LANE0 + 2 * pi
                x2 = xc_ref[:, pi * LANE:(pi + 1) * LANE]
                xd = x2 * _halves(lane, _col(dtv, hl0, lane), _col(dtv, hl0 + 1, lane))
                xdb = xd.astype(BF16)
                yd = jnp.zeros((L, LANE), F32)
                for hh in range(2):
                    seg = _col(cs, hl0 + hh, lane) - cst_ref[hl0 + hh:hl0 + hh + 1, :]
                    M = CB * jnp.exp(jnp.where(c <= r, seg, NEG))
                    yh = jnp.dot(M.astype(BF16), xdb, preferred_element_type=F32)
                    yd = jnp.where((lane >= SSD_HEAD_DIM) if hh else (lane < SSD_HEAD_DIM), yh, yd)
                hp = h_ref[pi]
                hs_ref[0, pi] = hp
                yo = lax.dot_general(Cg, hp.astype(BF16), NT_DIMS, preferred_element_type=F32)
                yo = yo * _halves(lane, _col(ecs, hl0, lane), _col(ecs, hl0 + 1, lane))
                dsk = _halves(lane, _col(dkv, hl0, lane), _col(dkv, hl0 + 1, lane))
                y_ref[:, pi * LANE:(pi + 1) * LANE] = yd + yo + dsk * x2
                xw = (xd * _halves(lane, _col(dec, hl0, lane), _col(dec, hl0 + 1, lane))).astype(BF16)
                st = lax.dot_general(xw, Bg, TN_DIMS, preferred_element_type=F32)
                cdp = jnp.where(prow < SSD_HEAD_DIM, _col(cd, hl0, lane), _col(cd, hl0 + 1, lane))
                h_ref[pi] = cdp * hp + st

    vec = pl.BlockSpec((1, LANE), lambda i: (0, 0))
    return pl.pallas_call(
        body, name=name, grid=(nc,),
        in_specs=[pl.BlockSpec((L, SSD_CH), lambda i: (i, 0)), pl.BlockSpec((L, LANE), lambda i: (i, col)),
                  vec, vec, vec],
        out_specs=[pl.BlockSpec((L, SSD_X), lambda i: (i, 0)),
                   pl.BlockSpec((1, SSD_PAIRS, LANE, SSD_STATE), lambda i: (i, 0, 0, 0))],
        out_shape=[jax.ShapeDtypeStruct((S, SSD_X), F32),
                   jax.ShapeDtypeStruct((nc, SSD_PAIRS, LANE, SSD_STATE), F32)],
        scratch_shapes=[pltpu.VMEM((SSD_PAIRS, LANE, SSD_STATE), F32), pltpu.VMEM((LANE, L), F32)],
        compiler_params=_cparams("arbitrary"),
    )(xc, rest, dtb, alog, dskip)


def _pair_sums(lane, v):
    lo = jnp.sum(jnp.where(lane < SSD_HEAD_DIM, v, 0.0), axis=1, keepdims=True)
    return lo, jnp.sum(v, axis=1, keepdims=True) - lo


def _ssd_bwd(xc, rest, col, dtb, alog, dskip, hs, dy, *, name):
    S = xc.shape[0]
    L = SSD_CHUNK
    nc = S // L
    PG = SSD_PAIRS // SSD_GROUPS

    def body(xc_ref, f_ref, dtb_ref, al_ref, dk_ref, hs_ref, dy_ref, dxc_ref, ddt_ref, dp_ref, dh_ref, cst_ref):
        i = pl.program_id(0)

        @pl.when(i == 0)
        def _():
            dh_ref[...] = jnp.zeros_like(dh_ref)
            dp_ref[...] = jnp.zeros_like(dp_ref)

        fv = f_ref[...] + dtb_ref[...]
        lane, hl, dtv, A, cs, r, c = _ssd_prep(f_ref[...], dtb_ref[...], al_ref[...])
        cst_ref[...] = cs.T
        cs_last = cs[L - 1:L, :]
        ecs = jnp.exp(cs)
        dec = jnp.exp(cs_last - cs)
        cd = jnp.exp(cs_last)
        dkv = dk_ref[...]
        prow = lax.broadcasted_iota(jnp.int32, (LANE, 1), 0)
        lrow = lax.broadcasted_iota(jnp.int32, (L, 1), 0)
        is_last = lrow == L - 1
        causal = c <= r
        dcs = jnp.zeros((L, LANE), F32)
        ddt = jnp.zeros((L, LANE), F32)
        dD = jnp.zeros((1, LANE), F32)
        for g in range(SSD_GROUPS):
            Bg = xc_ref[:, SSD_B0 + g * SSD_STATE:SSD_B0 + (g + 1) * SSD_STATE].astype(BF16)
            Cg = xc_ref[:, SSD_C0 + g * SSD_STATE:SSD_C0 + (g + 1) * SSD_STATE].astype(BF16)
            CB = lax.dot_general(Cg, Bg, NT_DIMS, preferred_element_type=F32)
            dCB = jnp.zeros((L, L), F32)
            dB = jnp.zeros((L, SSD_STATE), F32)
            dC = jnp.zeros((L, SSD_STATE), F32)
            for pp in range(PG):
                pi = g * PG + pp
                hl0 = DT_LANE0 + 2 * pi
                x2 = xc_ref[:, pi * LANE:(pi + 1) * LANE]
                dy2 = dy_ref[:, pi * LANE:(pi + 1) * LANE]
                dtp = _halves(lane, _col(dtv, hl0, lane), _col(dtv, hl0 + 1, lane))
                xd = x2 * dtp
                xdb = xd.astype(BF16)
                dsk = _halves(lane, _col(dkv, hl0, lane), _col(dkv, hl0 + 1, lane))
                dx2 = dsk * dy2
                sD = _pair_sums(lane, dy2 * x2)
                hp = hs_ref[0, pi]
                hpb = hp.astype(BF16)
                ecsp = _halves(lane, _col(ecs, hl0, lane), _col(ecs, hl0 + 1, lane))
                yo = lax.dot_general(Cg, hpb, NT_DIMS, preferred_element_type=F32) * ecsp
                dW = (ecsp * dy2).astype(BF16)
                dC = dC + jnp.dot(dW, hpb, preferred_element_type=F32)
                dhp = lax.dot_general(dW, Cg, TN_DIMS, preferred_element_type=F32)
                sYo = _pair_sums(lane, dy2 * yo)
                dhn = dh_ref[pi]
                cdh = (_col(cd, hl0, lane), _col(cd, hl0 + 1, lane))
                dhp = dhp + jnp.where(prow < SSD_HEAD_DIM, cdh[0], cdh[1]) * dhn
                rs = jnp.sum(dhn * hp, axis=1, keepdims=True)
                lo = jnp.sum(jnp.where(prow < SSD_HEAD_DIM, rs, 0.0), axis=0, keepdims=True)
                dcd = (lo, jnp.sum(rs, axis=0, keepdims=True) - lo)
                dhnb = dhn.astype(BF16)
                decp = _halves(lane, _col(dec, hl0, lane), _col(dec, hl0 + 1, lane))
                G = lax.dot_general(Bg, dhnb, NT_DIMS, preferred_element_type=F32)
                dxd = decp * G
                sdec = _pair_sums(lane, xd * G)
                dB = dB + jnp.dot((xd * decp).astype(BF16), dhnb, preferred_element_type=F32)
                dh_ref[pi] = dhp
                for hh in range(2):
                    hmask = (lane >= SSD_HEAD_DIM) if hh else (lane < SSD_HEAD_DIM)
                    seg = _col(cs, hl0 + hh, lane) - cst_ref[hl0 + hh:hl0 + hh + 1, :]
                    Lm = jnp.exp(jnp.where(causal, seg, NEG))
                    M = CB * Lm
                    dyh = jnp.where(hmask, dy2, 0.0).astype(BF16)
                    dM = lax.dot_general(dyh, xdb, NT_DIMS, preferred_element_type=F32)
                    dxd = dxd + lax.dot_general(M.astype(BF16), dyh, TN_DIMS, preferred_element_type=F32)
                    Q = dM * M
                    dCB = dCB + dM * Lm
                    dech = _col(dec, hl0 + hh, lane)
                    dd = sdec[hh] * dech
                    end = dcd[hh] * cdh[hh] + jnp.sum(dd, axis=0, keepdims=True)
                    dcs_h = (sYo[hh] - dd + jnp.sum(Q, axis=1, keepdims=True)
                             - jnp.sum(Q.T, axis=1, keepdims=True) + jnp.where(is_last, end, 0.0))
                    dcs = jnp.where(lane == hl0 + hh, dcs_h, dcs)
                    dD = jnp.where(lane == hl0 + hh, jnp.sum(sD[hh], axis=0, keepdims=True), dD)
                sdt = _pair_sums(lane, dxd * x2)
                ddt = jnp.where(lane == hl0, sdt[0], jnp.where(lane == hl0 + 1, sdt[1], ddt))
                dxc_ref[:, pi * LANE:(pi + 1) * LANE] = dx2 + dxd * dtp
            dCBb = dCB.astype(BF16)
            dC = dC + jnp.dot(dCBb, Bg, preferred_element_type=F32)
            dB = dB + lax.dot_general(dCBb, Cg, TN_DIMS, preferred_element_type=F32)
            dxc_ref[:, SSD_B0 + g * SSD_STATE:SSD_B0 + (g + 1) * SSD_STATE] = dB
            dxc_ref[:, SSD_C0 + g * SSD_STATE:SSD_C0 + (g + 1) * SSD_STATE] = dC
        da = _tri_dot(jnp.where(c >= r, 1.0, 0.0).astype(BF16), dcs)
        ddtv = ddt + da * A
        ddt_ref[...] = jnp.where(hl, ddtv * _sigmoid(fv), 0.0)
        dal = jnp.sum(da * dtv, axis=0, keepdims=True) * A
        dp_ref[0:1, :] += dal
        dp_ref[1:2, :] += dD

    rev = lambda i: nc - 1 - i
    vec = pl.BlockSpec((1, LANE), lambda i: (0, 0))
    return pl.pallas_call(
        body, name=name, grid=(nc,),
        in_specs=[pl.BlockSpec((L, SSD_CH), lambda i: (rev(i), 0)), pl.BlockSpec((L, LANE), lambda i: (rev(i), col)),
                  vec, vec, vec,
                  pl.BlockSpec((1, SSD_PAIRS, LANE, SSD_STATE), lambda i: (rev(i), 0, 0, 0)),
                  pl.BlockSpec((L, SSD_X), lambda i: (rev(i), 0))],
        out_specs=[pl.BlockSpec((L, SSD_CH), lambda i: (rev(i), 0)), pl.BlockSpec((L, LANE), lambda i: (rev(i), 0)),
                   pl.BlockSpec((SUBLANE, LANE), lambda i: (0, 0))],
        out_shape=[jax.ShapeDtypeStruct((S, SSD_CH), F32), jax.ShapeDtypeStruct((S, LANE), F32),
                   jax.ShapeDtypeStruct((SUBLANE, LANE), F32)],
        scratch_shapes=[pltpu.VMEM((SSD_PAIRS, LANE, SSD_STATE), F32), pltpu.VMEM((LANE, L), F32)],
        compiler_params=_cparams("arbitrary"),
    )(xc, rest, dtb, alog, dskip, hs, dy)


def _gate_norm_fwd(y, rest, zcol, nw, *, name):
    S, W = y.shape
    ts = _row_tile(S)
    GW = W // SSD_GROUPS

    def body(y_ref, z_ref, nw_ref, o_ref):
        z = z_ref[...]
        t = y_ref[...] * (z * _sigmoid(z))
        for g in range(SSD_GROUPS):
            cols = slice(g * GW, (g + 1) * GW)
            tg = t[:, cols]
            rr = lax.rsqrt(jnp.mean(tg * tg, axis=-1, keepdims=True) + NORM_EPS)
            o_ref[:, cols] = (tg * rr * nw_ref[:, cols]).astype(BF16)

    row = pl.BlockSpec((ts, W), lambda i: (i, 0))
    return pl.pallas_call(
        body, name=name, grid=(S // ts,),
        in_specs=[row, pl.BlockSpec((ts, W), lambda i: (i, zcol)), pl.BlockSpec((1, W), lambda i: (0, 0))],
        out_specs=row,
        out_shape=jax.ShapeDtypeStruct((S, W), BF16),
        compiler_params=_cparams("parallel"),
    )(y, rest, nw)


def _gate_norm_bwd(y, rest, zcol, nw, do, *, name):
    S, W = y.shape
    ts = _row_tile(S)
    GW = W // SSD_GROUPS

    def body(y_ref, z_ref, nw_ref, do_ref, dy_ref, dz_ref, dnw_ref):
        z = z_ref[...]
        yv = y_ref[...]
        s, ds = _dsilu(z)
        sz = z * s
        t = yv * sz
        dov = do_ref[...]
        parts = []
        for g in range(SSD_GROUPS):
            cols = slice(g * GW, (g + 1) * GW)
            tg = t[:, cols]
            rr = lax.rsqrt(jnp.mean(tg * tg, axis=-1, keepdims=True) + NORM_EPS)
            n = tg * rr
            dog = dov[:, cols]
            dn = dog * nw_ref[:, cols]
            dt = rr * (dn - n * jnp.mean(dn * n, axis=-1, keepdims=True))
            dy_ref[:, cols] = dt * sz[:, cols]
            dz_ref[:, cols] = (dt * yv[:, cols] * ds[:, cols]).astype(BF16)
            parts.append(jnp.sum(dog * n, axis=0, keepdims=True))
        _acc_out(dnw_ref, jnp.concatenate(parts, axis=1), pl.program_id(0) == 0)

    row = pl.BlockSpec((ts, W), lambda i: (i, 0))
    vec = pl.BlockSpec((1, W), lambda i: (0, 0))
    return pl.pallas_call(
        body, name=name, grid=(S // ts,),
        in_specs=[row, pl.BlockSpec((ts, W), lambda i: (i, zcol)), vec, row],
        out_specs=[row, row, vec],
        out_shape=[jax.ShapeDtypeStruct((S, W), F32), jax.ShapeDtypeStruct((S, W), BF16),
                   jax.ShapeDtypeStruct((1, W), F32)],
        compiler_params=_cparams("arbitrary"),
    )(y, rest, nw, do)


N_DEV = 8
MESH = pl.DeviceIdType.MESH
ANY = pl.BlockSpec(memory_space=pl.ANY)


def _all_gather(xs, *, name):
    n = len(xs)

    def body(*refs):
        x_refs, o_refs = refs[:n], refs[n:2 * n]
        send_sems, recv_sems, local_sems = refs[2 * n:]
        px, py, pc = lax.axis_index("x"), lax.axis_index("y"), lax.axis_index("c")
        me, sibling = (px, py, pc), (px, py, 1 - pc)
        chips = [(1 - px, py), (px, 1 - py), (1 - px, 1 - py)]

        def copy(a, k, block, to, src=None):
            slot = o_refs[a].at[4 * block[0] + 2 * block[1] + block[2]]
            return pltpu.make_async_remote_copy(
                src_ref=slot if src is None else src, dst_ref=slot,
                send_sem=send_sems.at[a, k], recv_sem=recv_sems.at[a, k], device_id=to, device_id_type=MESH)

        mine = [pltpu.make_async_copy(x_refs[a], o_refs[a].at[4 * px + 2 * py + pc], local_sems.at[a]) for a in range(n)]
        for cp in mine:
            cp.start()
        first = []
        for a in range(n):
            first.append(copy(a, 0, me, sibling, src=x_refs[a]))
            first += [copy(a, 1 + j, me, (*chip, pc), src=x_refs[a]) for j, chip in enumerate(chips)]
        for cp in first:
            cp.start()
        passed = []
        for j, chip in enumerate(chips):
            for a in range(n):
                copy(a, 1 + j, (*chip, pc), me).wait_recv()
                fwd = copy(a, 4 + j, (*chip, pc), sibling)
                fwd.start()
                passed.append(fwd)
        for a in range(n):
            copy(a, 0, sibling, me).wait_recv()
            for j, chip in enumerate(chips):
                copy(a, 4 + j, (*chip, 1 - pc), me).wait_recv()
        for cp in first + passed:
            cp.wait_send()
        for cp in mine:
            cp.wait()

    return pl.pallas_call(
        body, name=name,
        out_shape=[jax.ShapeDtypeStruct((N_DEV,) + x.shape, x.dtype) for x in xs],
        in_specs=[ANY] * n, out_specs=[ANY] * n,
        scratch_shapes=[pltpu.SemaphoreType.DMA((n, N_DEV - 1)), pltpu.SemaphoreType.DMA((n, N_DEV - 1)),
                        pltpu.SemaphoreType.DMA((n,))],
    )(*xs)


def _exchange(sends, *, name):
    n = len(sends)

    def body(*refs):
        _a2a_start("exchange", refs[:n], refs[n:2 * n], refs[2 * n:])
        _a2a_wait("exchange", refs[:n], refs[n:2 * n], refs[2 * n:])

    return pl.pallas_call(
        body, name=name,
        out_shape=_a2a_out_shapes("exchange", sends), in_specs=[ANY] * n, out_specs=[ANY] * n,
        scratch_shapes=_a2a_sems(n),
    )(*sends)


def _a2a_out_shapes(kind, arrays):
    if kind == "gather":
        return [jax.ShapeDtypeStruct((N_DEV,) + x.shape, x.dtype) for x in arrays]
    return [jax.ShapeDtypeStruct(x.shape, x.dtype) for x in arrays]


def _a2a_sems(n):
    return [pltpu.SemaphoreType.DMA((n, N_DEV - 1)), pltpu.SemaphoreType.DMA((n, N_DEV - 1)),
            pltpu.SemaphoreType.DMA((n,))]


def _a2a_copies(kind, in_refs, out_refs, send_sems, recv_sems, local_sems, arrivals):
    px, py, pc = lax.axis_index("x"), lax.axis_index("y"), lax.axis_index("c")
    me = 4 * px + 2 * py + pc
    n = len(in_refs)
    src = (lambda a, p: in_refs[a].at[p]) if kind == "exchange" else (lambda a, p: in_refs[a])
    local = [pltpu.make_async_copy(src(a, me), out_refs[a].at[me], local_sems.at[a]) for a in range(n)]
    sent, landing = [], []
    for k in range(1, N_DEV):
        qx = 1 - px if k & 4 else px
        qy = 1 - py if k & 2 else py
        qc = 1 - pc if k & 1 else pc
        peer = 4 * qx + 2 * qy + qc
        for a in range(n):
            def rdma(dst_slot):
                return pltpu.make_async_remote_copy(
                    src_ref=src(a, peer), dst_ref=out_refs[a].at[dst_slot],
                    send_sem=send_sems.at[a, k - 1], recv_sem=recv_sems.at[a, k - 1],
                    device_id=(qx, qy, qc), device_id_type=MESH)
            sent.append(rdma(me))
            if arrivals:
                landing.append(rdma(peer))
    return local, sent, landing


def _a2a_start(kind, in_refs, out_refs, sems):
    local, sent, _ = _a2a_copies(kind, in_refs, out_refs, *sems, arrivals=False)
    for cp in local + sent:
        cp.start()


def _a2a_wait(kind, in_refs, out_refs, sems):
    local, sent, landing = _a2a_copies(kind, in_refs, out_refs, *sems, arrivals=True)
    for cp in landing:
        cp.wait_recv()
    for cp in sent:
        cp.wait_send()
    for cp in local:
        cp.wait()


IN_POOL, IN_QKV, IN_F, IN_Z, IN_XBC, IN_DT, IN_GATE, IN_TOTAL = 0, 512, 2048, 2056, 3080, 4616, 4632, 7704
QKV_W = IN_F - IN_QKV
REST_Z_BLK, REST_POOL_BLK, REST_XBC0, REST_FDT_BLK = 3, 8, 4608, 48
REST_FDT0 = REST_FDT_BLK * LANE
REST_USED = REST_FDT0 + (IN_Z - IN_F) + (IN_GATE - IN_DT)
REST_W = REST_FDT0 + LANE


def _w_in_runs(shard):
    segs = [(0, IN_QKV, IN_F, 0), (1, IN_GATE, IN_TOTAL, 0), (1, IN_Z, IN_XBC, 3072), (1, IN_POOL, IN_QKV, 4096),
            (1, IN_XBC, IN_DT, REST_XBC0), (1, IN_F, IN_Z, REST_FDT0), (1, IN_DT, IN_GATE, REST_FDT0 + IN_Z - IN_F)]
    runs = []
    for dst, t0, t1, d0 in segs:
        for j in range(N_DEV):
            lo, hi = max(t0, shard * j), min(t1, shard * (j + 1))
            if lo < hi:
                runs.append((dst, d0 + lo - t0, j, lo - shard * j, hi - lo))
    return runs


def _repack_w_in(g, shard, *, name):
    _, L, K, P = g.shape
    tr = 256
    runs = _w_in_runs(shard)

    def body(g_ref, q_ref, r_ref):
        r_ref[:, REST_USED:] = jnp.zeros((tr, REST_W - REST_USED), g_ref.dtype)
        for dst, b, j, a, n in runs:
            (q_ref, r_ref)[dst][:, b:b + n] = g_ref[j, :, a:a + n]

    return pl.pallas_call(
        body, name=name, grid=(L, K // tr),
        in_specs=[pl.BlockSpec((N_DEV, None, tr, P), lambda l, i: (0, l, i, 0))],
        out_specs=[pl.BlockSpec((None, tr, QKV_W), lambda l, i: (l, i, 0)),
                   pl.BlockSpec((None, tr, REST_W), lambda l, i: (l, i, 0))],
        out_shape=[jax.ShapeDtypeStruct((L, K, QKV_W), g.dtype), jax.ShapeDtypeStruct((L, K, REST_W), g.dtype)],
        compiler_params=_cparams("parallel", "parallel"),
    )(g)


def _held_index(l, li, i, n):
    return jnp.where(l == li, i, jnp.where(l > li, n - 1, 0))


def _repack_dw_in(dwq, dwr, shard, *, name):
    L = len(dwq)
    K = dwq[0].shape[0]
    tr = 128
    nt = K // tr
    runs = _w_in_runs(shard)

    def body(*refs):
        srcs, o_ref = refs[:2 * L], refs[2 * L]
        l = pl.program_id(0)
        for li in range(L):
            @pl.when(l == li)
            def _():
                for dst, b, j, a, n in runs:
                    o_ref[j, :, a:a + n] = srcs[2 * li + dst][:, b:b + n]

    in_specs = []
    for li in range(L):
        hold = functools.partial(lambda li, l, i: (_held_index(l, li, i, nt), 0), li)
        in_specs += [pl.BlockSpec((tr, QKV_W), hold), pl.BlockSpec((tr, REST_W), hold)]
    args = [x for pair in zip(dwq, dwr) for x in pair]
    return pl.pallas_call(
        body, name=name, grid=(L, nt),
        in_specs=in_specs,
        out_specs=pl.BlockSpec((N_DEV, None, tr, shard), lambda l, i: (0, l, i, 0)),
        out_shape=jax.ShapeDtypeStruct((N_DEV, L, K, shard), dwq[0].dtype),
        compiler_params=_cparams("arbitrary", "arbitrary"),
    )(*args)


FLAT_W = 1024
ADAM_ROWS = 128


def _adam_math(w, g, m, v):
    m2 = ADAM_B1 * m + (1.0 - ADAM_B1) * g
    v2 = ADAM_B2 * v + (1.0 - ADAM_B2) * (g * g)
    m_hat = m2 / (1.0 - ADAM_B1 ** ADAM_STEP)
    v_hat = v2 / (1.0 - ADAM_B2 ** ADAM_STEP)
    delta = -ADAM_LR * (m_hat / (jnp.sqrt(v_hat) + ADAM_EPS) + ADAM_WD * w)
    return delta, m2, v2


def _row_div(R, align):
    best = R
    for t in range(align, min(R, ADAM_ROWS) + 1, align):
        if R % t == 0:
            best = t
    return best


def _sum8(ref_of):
    g = ref_of(0).astype(F32)
    for s in range(1, N_DEV):
        g = g + ref_of(s).astype(F32)
    return g


def _sum_adam(recvs, w, m, v, *, name):
    L, R, C = w.shape
    single = not isinstance(recvs, (list, tuple))
    recvs = [recvs] if single else list(recvs)
    tr = _row_div(R, SUBLANE * (4 // recvs[0].dtype.itemsize))
    nt = R // tr
    nr = len(recvs)

    def body(*refs):
        r_refs = refs[:nr]
        w_ref, m_ref, v_ref, g_ref, d_ref, m2_ref, v2_ref = refs[nr:]

        def run(ref_of):
            g = _sum8(ref_of)
            delta, m2, v2 = _adam_math(w_ref[...], g, m_ref[...], v_ref[...])
            g_ref[...] = g
            d_ref[...] = delta
            m2_ref[...] = m2
            v2_ref[...] = v2

        if single:
            run(lambda s: r_refs[0][s])
        else:
            l = pl.program_id(0)
            for li in range(L):
                @pl.when(l == li)
                def _():
                    run(lambda s: r_refs[li][s])

    if single:
        r_specs = [pl.BlockSpec((N_DEV, None, tr, C), lambda l, i: (0, l, i, 0))]
    else:
        r_specs = [pl.BlockSpec((N_DEV, tr, C), functools.partial(lambda li, l, i: (0, _held_index(l, li, i, nt), 0), li))
                   for li in range(L)]
    row = pl.BlockSpec((None, tr, C), lambda l, i: (l, i, 0))
    return pl.pallas_call(
        body, name=name, grid=(L, nt),
        in_specs=r_specs + [row, row, row],
        out_specs=[row] * 4,
        out_shape=[jax.ShapeDtypeStruct((L, R, C), F32)] * 4,
        compiler_params=_cparams("arbitrary", "arbitrary"),
    )(*recvs, w, m, v)


def _sum_parts(parts, *, name):
    _, R, C = parts.shape

    def body(p_ref, o_ref):
        o_ref[...] = _sum8(lambda s: p_ref[s])

    return pl.pallas_call(
        body, name=name, out_shape=jax.ShapeDtypeStruct((R, C), F32),
        in_specs=[pl.BlockSpec(memory_space=pltpu.VMEM)], out_specs=pl.BlockSpec(memory_space=pltpu.VMEM),
    )(parts)


def _adam(g, w, m, v, *, name):
    def body(g_ref, w_ref, m_ref, v_ref, d_ref, m2_ref, v2_ref):
        delta, m2, v2 = _adam_math(w_ref[...], g_ref[...], m_ref[...], v_ref[...])
        d_ref[...] = delta
        m2_ref[...] = m2
        v2_ref[...] = v2

    whole = pl.BlockSpec(memory_space=pltpu.VMEM)
    return pl.pallas_call(
        body, name=name, in_specs=[whole] * 4, out_specs=[whole] * 3,
        out_shape=[jax.ShapeDtypeStruct(g.shape, F32)] * 3,
    )(g, w, m, v)


SHARDED = ("w_in", "ffn_up", "p_pool", "p_attn", "p_ssd", "w_out", "ffn_down", "ssd_conv_w", "ffn_conv_w")
REPLICATED = ("norm_mix", "pool_mix", "pool_scale", "f_bias", "ssd_conv_b", "ssd_dt_bias", "ssd_a_log", "ssd_d",
              "ssd_norm", "norm_ffn", "ffn_conv_b", "norm_final")
WEIGHTS = ("norm_mix", "w_in", "pool_mix", "pool_scale", "f_bias", "ssd_conv_w", "ssd_conv_b", "ssd_dt_bias",
           "ssd_a_log", "ssd_d", "ssd_norm", "p_pool", "p_attn", "p_ssd", "w_out", "norm_ffn", "ffn_up",
           "ffn_conv_w", "ffn_conv_b", "ffn_down", "norm_final")


def _round_up(n, k):
    return -(-n // k) * k


def _pad_last(x, width):
    return jnp.pad(x, [(0, 0)] * (x.ndim - 1) + [(0, width - x.shape[-1])])


def _flat_rows(parts, rows):
    flat = jnp.concatenate([p.reshape(-1) for p in parts])
    return jnp.pad(flat, (0, rows * FLAT_W - flat.shape[0])).reshape(rows, FLAT_W)


def _lanes(v, lane0):
    return jnp.pad(v, (lane0, LANE - lane0 - v.shape[0]))[None]


def _cols_by_device(g):
    _, L, R, c = g.shape
    return jnp.moveaxis(g, 0, 2).reshape(L, R, N_DEV * c)


def _rows_by_device(g):
    _, L, r, C = g.shape
    return jnp.moveaxis(g, 0, 1).reshape(L, N_DEV * r, C)


def _gather_payload(W, l, shard_pad, up_pad):
    bf = lambda n: W[n][l].astype(BF16)
    return [_pad_last(bf("w_in"), shard_pad), _pad_last(bf("ffn_up"), up_pad), bf("p_pool"), bf("p_attn"),
            bf("p_ssd"), bf("w_out"), bf("ffn_down"), W["ssd_conv_w"][l], W["ffn_conv_w"][l]]


def _layer_params(gathered, W, l, shard, up_w, up_pad):
    g_win, g_up, g_pp, g_pa, g_ps, g_wo, g_dn, g_cw, g_fcw = [g[:, None] for g in gathered]
    w_qkv, w_rest = _repack_w_in(g_win, shard, name=f"l{l}_repack_w_in")
    half = N_DEV // 2
    w_down = _rows_by_device(g_dn)[0]
    w_down_p = jnp.pad(w_down.reshape(half, up_w, -1), ((0, 0), (0, up_pad - up_w), (0, 0))).reshape(half * up_pad, -1)
    fcb_p = _pad_last(W["ffn_conv_b"][l].reshape(N_DEV, up_w), up_pad).reshape(1, N_DEV * up_pad)
    return dict(
        nm=W["norm_mix"][l][None], w_qkv=w_qkv[0], w_rest=w_rest[0], mix=W["pool_mix"][l].astype(BF16),
        pscale=W["pool_scale"][l][None], fb=_lanes(W["f_bias"][l], 0), cw=_cols_by_device(g_cw)[0],
        cb=W["ssd_conv_b"][l][None], dtb=_lanes(W["ssd_dt_bias"][l], DT_LANE0),
        alog=_lanes(W["ssd_a_log"][l], DT_LANE0), dsk=_lanes(W["ssd_d"][l], DT_LANE0), snw=W["ssd_norm"][l][None],
        p_pool=_cols_by_device(g_pp)[0], p_attn=_cols_by_device(g_pa)[0], p_ssd=_rows_by_device(g_ps)[0],
        w_out=_rows_by_device(g_wo)[0], nf=W["norm_ffn"][l][None], g_up=g_up, up_blocks=(N_DEV, up_w, up_pad),
        fcw=_cols_by_device(_pad_last(g_fcw, up_pad))[0], fcb=fcb_p, w_down=w_down_p)


def _layer_fwd(x, P, l, comm):
    nm = lambda s: f"l{l}_{s}"
    u = _rms_fwd(x, P["nm"], name=nm("rms_mix"))
    qkv = _mm(u, P["w_qkv"], out_dtype=BF16, name=nm("proj_qkv"))
    rest = _mm(u, P["w_rest"], name=nm("proj_rest"))
    d, ya0 = _pool_fwd(rest, REST_POOL_BLK, P["mix"], P["pscale"], name=nm("pool"))
    ya = _mm(ya0, P["p_pool"], name=nm("p_pool"))
    ccol = _fox_pre(rest, REST_FDT_BLK, P["fb"], name=nm("fox_pre"))
    o, lrow, carried = _attn_fwd(qkv, ccol, comm=comm, name=nm("attn"))
    yb = _mm(o, P["p_attn"], name=nm("p_attn"))
    xc = _conv_silu_fwd(rest, REST_XBC0, SSD_CH, P["cw"], P["cb"], name=nm("ssd_conv"))
    y, hs = _ssd_fwd(xc, rest, REST_FDT_BLK, P["dtb"], P["alog"], P["dsk"], name=nm("ssd"))
    yc0 = _gate_norm_fwd(y, rest, REST_Z_BLK, P["snw"], name=nm("ssd_norm"))
    yc = _mm(yc0, P["p_ssd"], name=nm("p_ssd"))
    merged = _merge_fwd(rest, ya, yb, yc, name=nm("merge"))
    x1 = _mm(merged, P["w_out"], add=x, name=nm("w_out"))
    u2 = _rms_fwd(x1, P["nf"], name=nm("rms_ffn"))
    h0 = _mm_up_fwd(u2, P["g_up"], 0, name=nm("ffn_up"))
    a = _ffn_act_fwd(h0, P["fcw"], P["fcb"], name=nm("ffn_act"))
    x2 = _mm(a, P["w_down"], add=x1, name=nm("ffn_down"))
    saved = dict(x=x, u=u, qkv=qkv, rest=rest, d=d, ya0=ya0, ya=ya, ccol=ccol, o=o, lrow=lrow, yb=yb,
                 xc=xc, y=y, hs=hs, yc0=yc0, yc=yc, merged=merged, x1=x1, u2=u2, h0=h0, a=a)
    return x2, saved, carried


FFN_GRADS = ("ffn_up", "ffn_down", "ffn_conv_w")
MIX_GRADS = ("w_in", "p_pool", "p_attn", "p_ssd", "w_out", "ssd_conv_w")


def _layer_bwd(g, gb, sv, P, l, pending):
    nm = lambda s: f"l{l}_{s}_bwd"
    rest = sv["rest"]
    G = {}
    up_blocks, up_w, up_pad = P["up_blocks"]
    half = up_blocks // 2
    da = _mm(gb, P["w_down"], tb=True, name=nm("ffn_down_dx"))
    dwd = _mm(sv["a"], gb, ta=True, out_dtype=GRAD_WIRE, name=nm("ffn_down_dw"))
    G["ffn_down"] = dwd.reshape(half, up_pad, -1)[:, :up_w].reshape(N_DEV, -1, dwd.shape[1])
    dh, dfcw, dfcb = _ffn_act_bwd(sv["h0"], P["fcw"], P["fcb"], da, name=nm("ffn_act"))
    du2 = _mm_up_dx(dh, P["g_up"], 0, name=nm("ffn_up_dx"))
    G["ffn_up"] = _mm_up_dw(sv["u2"], dh, up_blocks, up_w, name=nm("ffn_up_dw"))
    taps = dfcw.shape[1]
    G["ffn_conv_w"] = jnp.moveaxis(dfcw.reshape(2, taps, half, up_pad)[..., :up_w], 2, 1).reshape(N_DEV, taps, up_w)
    G["ffn_conv_b"] = dfcb.reshape(2, half, up_pad)[..., :up_w].reshape(-1)
    dx1, dx1b, dnf = _rms_bwd(sv["x1"], P["nf"], du2, g, name=nm("rms_ffn"))
    G["norm_ffn"] = dnf[0]
    dmerged = _mm(dx1b, P["w_out"], tb=True, name=nm("w_out_dx"))
    dwo = _mm(sv["merged"], dx1b, ta=True, out_dtype=GRAD_WIRE, name=nm("w_out_dw"))
    G["w_out"] = dwo.reshape(N_DEV, -1, dwo.shape[1])
    dgl, dya, dyb, dyc = _merge_bwd(rest, sv["ya"], sv["yb"], sv["yc"], dmerged, name=nm("merge"))
    G["p_pool"] = _mm_tn_blocks(sv["ya0"], dya, N_DEV, name=nm("p_pool_dw"))
    dya0 = _mm(dya, P["p_pool"], tb=True, name=nm("p_pool_dx"))
    G["p_attn"] = _mm_tn_blocks(sv["o"], dyb, N_DEV, name=nm("p_attn_dw"))
    do = _mm(dyb, P["p_attn"], tb=True, out_dtype=BF16, name=nm("p_attn_dx"))
    dps = _mm(sv["yc0"], dyc, ta=True, out_dtype=GRAD_WIRE, name=nm("p_ssd_dw"))
    G["p_ssd"] = dps.reshape(N_DEV, -1, dps.shape[1])
    dyc0 = _mm(dyc, P["p_ssd"], tb=True, name=nm("p_ssd_dx"))
    dpv, dmix, dpsc = _pool_bwd(dya0, sv["d"], P["mix"], P["pscale"], name=nm("pool"))
    G["pool_mix"], G["pool_scale"] = dmix, dpsc[0]
    carried = [((n, l), G[n]) for n in FFN_GRADS] + pending
    dq, dk, dv, dcq, dck, got = _attn_bwd(sv["qkv"], sv["o"], do, sv["lrow"], sv["ccol"],
                                          comm=("exchange", [a for _, a in carried]), name=nm("attn"))
    received = {key: r for (key, _), r in zip(carried, got)}
    dy, dz, dsnw = _gate_norm_bwd(sv["y"], rest, REST_Z_BLK, P["snw"], dyc0, name=nm("ssd_norm"))
    G["ssd_norm"] = dsnw[0]
    dxc, ddt, dpar = _ssd_bwd(sv["xc"], rest, REST_FDT_BLK, P["dtb"], P["alog"], P["dsk"], sv["hs"], dy, name=nm("ssd"))
    G["ssd_a_log"] = dpar[0, DT_LANE0:DT_LANE0 + SSD_HEADS]
    G["ssd_d"] = dpar[1, DT_LANE0:DT_LANE0 + SSD_HEADS]
    dxbc, dcw, dcb = _conv_silu_bwd(rest, REST_XBC0, SSD_CH, P["cw"], P["cb"], dxc, name=nm("ssd_conv"))
    G["ssd_conv_w"] = jnp.moveaxis(dcw.reshape(dcw.shape[0], N_DEV, -1), 1, 0)
    G["ssd_conv_b"] = dcb[0]
    dfdt, dfdtb = _fox_post(dcq, dck, rest, REST_FDT_BLK, P["fb"], ddt, name=nm("fox_post"))
    G["f_bias"] = dfdtb[0, :ATTN_HEADS]
    G["ssd_dt_bias"] = dfdtb[0, DT_LANE0:DT_LANE0 + SSD_HEADS]
    dqkv = jnp.concatenate([dq, dk, dv], axis=1)
    drest = jnp.concatenate([dgl, dz, dpv, dxbc, dfdt], axis=1)
    du = _mm(dqkv, P["w_qkv"], tb=True, name=nm("proj_qkv_dx"))
    du = _mm(drest, P["w_rest"], tb=True, add=du, name=nm("proj_rest_dx"))
    G["w_qkv"] = _mm(sv["u"], dqkv, ta=True, out_dtype=GRAD_WIRE, name=nm("proj_qkv_dw"))
    G["w_rest"] = _mm(sv["u"], drest, ta=True, out_dtype=GRAD_WIRE, name=nm("proj_rest_dw"))
    dx, dxb, dnm = _rms_bwd(sv["x"], P["nm"], du, dx1, name=nm("rms_mix"))
    G["norm_mix"] = dnm[0]
    return dx, dxb, G, received


def kernel(x, norm_mix, w_in, pool_mix, pool_scale, f_bias, ssd_conv_w, ssd_conv_b, ssd_dt_bias, ssd_a_log, ssd_d, ssd_norm, p_pool, p_attn, p_ssd, w_out, norm_ffn, ffn_up, ffn_conv_w, ffn_conv_b, ffn_down, norm_final, loss_target, m_norm_mix, m_w_in, m_pool_mix, m_pool_scale, m_f_bias, m_ssd_conv_w, m_ssd_conv_b, m_ssd_dt_bias, m_ssd_a_log, m_ssd_d, m_ssd_norm, m_p_pool, m_p_attn, m_p_ssd, m_w_out, m_norm_ffn, m_ffn_up, m_ffn_conv_w, m_ffn_conv_b, m_ffn_down, m_norm_final, v_norm_mix, v_w_in, v_pool_mix, v_pool_scale, v_f_bias, v_ssd_conv_w, v_ssd_conv_b, v_ssd_dt_bias, v_ssd_a_log, v_ssd_d, v_ssd_norm, v_p_pool, v_p_attn, v_p_ssd, v_w_out, v_norm_ffn, v_ffn_up, v_ffn_conv_w, v_ffn_conv_b, v_ffn_down, v_norm_final):
    args = dict(locals())
    W = {n: args[n] for n in WEIGHTS}
    M = {n: args["m_" + n] for n in WEIGHTS}
    V = {n: args["v_" + n] for n in WEIGHTS}
    L = norm_mix.shape[0]
    shard = w_in.shape[-1]
    up_w = ffn_up.shape[-1]
    up_pad = _round_up(up_w, LANE)

    payload = lambda l: _gather_payload(W, l, _round_up(shard, LANE), up_pad)
    params = [_layer_params(_all_gather(payload(0), name="gather_weights_l0"), W, 0, shard, up_w, up_pad)]
    h, saves = x[0], []
    for l in range(L):
        h, sv, got = _layer_fwd(h, params[l], l, ("gather", payload(l + 1)) if l + 1 < L else None)
        saves.append(sv)
        if l + 1 < L:
            params.append(_layer_params(got, W, l + 1, shard, up_w, up_pad))
    loss_part, g, gb, dnfin = _loss_head(h, norm_final[None], loss_target[0], name="loss_head")

    G, received, pending = [None] * L, {}, []
    for l in reversed(range(L)):
        g, gb, G[l], got = _layer_bwd(g, gb, saves[l], params[l], l, pending)
        received.update(got)
        dw_in = _repack_dw_in([G[l]["w_qkv"]], [G[l]["w_rest"]], shard, name=f"l{l}_repack_dw_in")[:, 0]
        pending = [(("w_in", l), dw_in)] + [((n, l), G[l][n]) for n in MIX_GRADS[1:]]

    rep_grads = {n: jnp.stack([G[l][n] for l in range(L)]) for n in REPLICATED if n != "norm_final"}
    rep_grads["norm_final"] = dnfin[0]
    rep_size = sum(W[n].size for n in REPLICATED)
    rep_rows = _round_up(-(-rep_size // (N_DEV * FLAT_W)), SUBLANE)
    rep_send = _flat_rows([rep_grads[n] for n in REPLICATED], N_DEV * rep_rows).reshape(N_DEV, rep_rows, FLAT_W)
    last = _exchange([a for _, a in pending] + [rep_send], name="exchange_grads")
    received.update({key: r for (key, _), r in zip(pending, last)})

    out = {}
    for n in SHARDED:
        shape = W[n].shape
        as3 = lambda a: a.reshape(L, -1, shape[-1])
        res = _sum_adam([received[(n, l)] for l in range(L)], as3(W[n]), as3(M[n]), as3(V[n]), name="sum_adam_" + n)
        for key, arr in zip(("grad_", "delta_", "new_m_", "new_v_"), res):
            out[key + n] = arr.reshape(shape)

    rep_sum = _sum_parts(last[-1], name="sum_replicated")
    rep_g = _all_gather([rep_sum], name="gather_replicated_grads")[0].reshape(N_DEV * rep_rows, FLAT_W)
    flat_rep = lambda D: _flat_rows([D[n] for n in REPLICATED], N_DEV * rep_rows)
    d_rp, m_rp, v_rp = _adam(rep_g, flat_rep(W), flat_rep(M), flat_rep(V), name="adam_replicated")
    off = 0
    for n in REPLICATED:
        size = W[n].size
        for key, arr in (("grad_", rep_g), ("delta_", d_rp), ("new_m_", m_rp), ("new_v_", v_rp)):
            out[key + n] = arr.reshape(-1)[off:off + size].reshape(W[n].shape)
        off += size

    loss = lax.psum(loss_part[0, 0], ("x", "y", "c"))
    return (loss, g[None], *[out["grad_" + n] for n in WEIGHTS], *[out["delta_" + n] for n in WEIGHTS],
            *[out["new_m_" + n] for n in WEIGHTS], *[out["new_v_" + n] for n in WEIGHTS])
```

```python
import functools

import jax
import jax.numpy as jnp
from jax import lax
from jax.experimental import pallas as pl
from jax.experimental.pallas import tpu as pltpu

F32 = jnp.float32
BF16 = jnp.bfloat16

LANE = 128
SUBLANE = 8
VMEM_LIMIT = 56 * 1024 * 1024

NORM_EPS = 1e-6
POOL_WINDOWS = (2, 4, 8, 16)
POOL_GROUP_DIM = 128
ATTN_HEADS = 8
ATTN_HEAD_DIM = 64
SSD_HEAD_DIM = 64
SSD_HEADS = 16
SSD_GROUPS = 2
SSD_STATE = 128
SSD_CHUNK = 128

ADAM_LR = 0.001
ADAM_B1 = 0.9
ADAM_B2 = 0.999
ADAM_EPS = 1e-08
ADAM_WD = 0.01
ADAM_STEP = 10


def _cparams(*sem):
    return pltpu.CompilerParams(dimension_semantics=tuple(sem), vmem_limit_bytes=VMEM_LIMIT)


def _tile(n, pref):
    if n <= pref:
        return n
    assert n % LANE == 0, n
    q = n // LANE
    best = 1
    for d in range(1, pref // LANE + 1):
        if q % d == 0:
            best = d
    return best * LANE


GRAD_WIRE = BF16


def _mm_call(args, in_specs, out_spec, out_shape, grid, *, ta, tb, nk, acc_shape, has_add, keep, name, comm=None):
    dims = (((0 if ta else 1,), (1 if tb else 0,)), ((), ()))
    nc = len(comm[1]) if comm else 0
    n_in = 2 + has_add

    def body(*refs):
        a_ref, b_ref = refs[0], refs[1]
        o_ref = refs[n_in + nc]
        acc_ref = refs[n_in + 2 * nc + 1] if nk > 1 else None
        carried = (refs[n_in:n_in + nc], refs[n_in + nc + 1:n_in + 2 * nc + 1], refs[n_in + 2 * nc + 1 + (nk > 1):])
        i, j, k = pl.program_id(0), pl.program_id(1), pl.program_id(2)
        if comm:
            _carry_start(comm[0], *carried, jnp.logical_and(jnp.logical_and(i == 0, j == 0), k == 0))
        d = lax.dot_general(a_ref[...], b_ref[...], dims, preferred_element_type=F32)

        def finish(r):
            if has_add:
                r = r + refs[2][...]
            if keep is not None:
                r = r[:, :keep]
            o_ref[...] = r.astype(o_ref.dtype)

        if nk == 1:
            finish(d)
        else:
            @pl.when(k == 0)
            def _():
                acc_ref[...] = d

            @pl.when(jnp.logical_and(k > 0, k < nk - 1))
            def _():
                acc_ref[...] += d

            @pl.when(k == nk - 1)
            def _():
                finish(acc_ref[...] + d)

        if comm:
            _carry_wait(comm[0], *carried, jnp.logical_and(jnp.logical_and(i == grid[0] - 1, j == grid[1] - 1),
                                                           k == nk - 1))

    res = pl.pallas_call(
        body, name=name, grid=grid, in_specs=in_specs + [ANY] * nc, out_specs=[out_spec] + [ANY] * nc,
        out_shape=[out_shape] + (_a2a_out_shapes(*comm) if comm else []),
        scratch_shapes=([pltpu.VMEM(acc_shape, F32)] if nk > 1 else []) + (_a2a_sems(nc) if comm else []),
        compiler_params=_cparams(*(("arbitrary",) * 3 if comm else ("parallel", "parallel", "arbitrary"))),
    )(*args, *(comm[1] if comm else []))
    return (res[0], list(res[1:])) if comm else res[0]


MM_VMEM_BUDGET = 40 * 1024 * 1024
MM_MAX_ROWS = 2048


def _mm_rows(M, tn, tk, nk, out_bytes, has_add):
    best = None
    for tm in range(LANE, min(M, MM_MAX_ROWS) + 1, LANE):
        if M % tm:
            continue
        blocks = 2 * (tm * tk * 2 + tk * tn * 2 + tm * tn * out_bytes + (tm * tn * 4 if has_add else 0))
        blocks += tm * tn * 4 * (2 if nk > 1 else 1)
        if blocks <= MM_VMEM_BUDGET:
            best = tm
    return M if best is None else best


def _mm(a, b, *, ta=False, tb=False, add=None, out_dtype=F32, comm=None, name):
    assert a.dtype == BF16 and b.dtype == BF16, (a.dtype, b.dtype)
    if ta:
        K, M = a.shape
    else:
        M, K = a.shape
    if tb:
        N, K2 = b.shape
    else:
        K2, N = b.shape
    assert K == K2, (a.shape, b.shape, ta, tb)
    tn, tk = _tile(N, 1024), _tile(K, 2048)
    tm = _mm_rows(M, tn, tk, K // tk, jnp.dtype(out_dtype).itemsize, add is not None)
    a_spec =pl.BlockSpec((tk, tm), lambda i, j, k: (k, i)) if ta else pl.BlockSpec((tm, tk), lambda i, j, k: (i, k))
    b_spec = pl.BlockSpec((tn, tk), lambda i, j, k: (j, k)) if tb else pl.BlockSpec((tk, tn), lambda i, j, k: (k, j))
    in_specs, args = [a_spec, b_spec], [a, b]
    if add is not None:
        in_specs.append(pl.BlockSpec((tm, tn), lambda i, j, k: (i, j)))
        args.append(add)
    return _mm_call(args, in_specs, pl.BlockSpec((tm, tn), lambda i, j, k: (i, j)),
                    jax.ShapeDtypeStruct((M, N), out_dtype), (M // tm, N // tn, K // tk),
                    ta=ta, tb=tb, nk=K // tk, acc_shape=(tm, tn), has_add=add is not None, keep=None, name=name,
                    comm=comm)


def _mm_tn_blocks(a, b, nblk, *, name):
    K, M = a.shape
    N = b.shape[1]
    bw = N // nblk
    tk = _tile(K, 2048)
    tm = _mm_rows(M, bw, tk, K // tk, jnp.dtype(GRAD_WIRE).itemsize, False)
    return _mm_call([a, b],
                    [pl.BlockSpec((tk, tm), lambda i, j, k: (k, i)), pl.BlockSpec((tk, bw), lambda i, j, k: (k, j))],
                    pl.BlockSpec((None, tm, bw), lambda i, j, k: (j, i, 0)),
                    jax.ShapeDtypeStruct((nblk, M, bw), GRAD_WIRE), (M // tm, nblk, K // tk),
                    ta=True, tb=False, nk=K // tk, acc_shape=(tm, bw), has_add=False, keep=None, name=name)


def _half_of(b, half):
    hi = jnp.where(b >= half, 1, 0)
    return hi, b - half * hi


def _mm_up_fwd(u, g_up, l, *, name):
    S, K = u.shape
    nblk, _, _, bw = g_up.shape
    tm = _mm_rows(S, bw, K, 1, 4, False)
    return _mm_call([u, g_up],
                    [pl.BlockSpec((tm, K), lambda i, j, k: (i, 0)),
                     pl.BlockSpec((None, None, K, bw), lambda i, j, k: (j, l, 0, 0))],
                    pl.BlockSpec((tm, bw), lambda i, j, k: (i, j)),
                    jax.ShapeDtypeStruct((S, nblk * bw), F32), (S // tm, nblk, 1),
                    ta=False, tb=False, nk=1, acc_shape=(tm, bw), has_add=False, keep=None, name=name)


def _mm_up_dx(dh, g_up, l, *, name):
    _, S, _ = dh.shape
    nblk, _, N, bw = g_up.shape
    half = nblk // 2
    tm = _mm_rows(S, N, bw, nblk, 4, False)
    return _mm_call([dh, g_up],
                    [pl.BlockSpec((None, tm, bw), lambda i, j, k: (_half_of(k, half)[0], i, _half_of(k, half)[1])),
                     pl.BlockSpec((None, None, N, bw), lambda i, j, k: (k, l, 0, 0))],
                    pl.BlockSpec((tm, N), lambda i, j, k: (i, 0)),
                    jax.ShapeDtypeStruct((S, N), F32), (S // tm, 1, nblk),
                    ta=False, tb=True, nk=nblk, acc_shape=(tm, N), has_add=False, keep=None, name=name)


def _mm_up_dw(u, dh, nblk, keep, *, name):
    S, M = u.shape
    half = nblk // 2
    bw = dh.shape[2] // half
    tk = _tile(S, 2048)
    tm = _mm_rows(M, bw, tk, S // tk, jnp.dtype(GRAD_WIRE).itemsize, False)
    return _mm_call([u, dh],
                    [pl.BlockSpec((tk, tm), lambda i, j, k: (k, i)),
                     pl.BlockSpec((None, tk, bw), lambda i, j, k: (_half_of(j, half)[0], k, _half_of(j, half)[1]))],
                    pl.BlockSpec((None, tm, keep), lambda i, j, k: (j, i, 0)),
                    jax.ShapeDtypeStruct((nblk, M, keep), GRAD_WIRE), (M // tm, nblk, S // tk),
                    ta=True, tb=False, nk=S // tk, acc_shape=(tm, bw), has_add=False, keep=keep, name=name)


ROW_TILE = 512
HALO = 8
POOL_HALO = 16


def _row_tile(S):
    return min(ROW_TILE, S)


def _acc_out(ref, val, first):
    @pl.when(first)
    def _():
        ref[...] = val

    @pl.when(jnp.logical_not(first))
    def _():
        ref[...] += val


def _rms_fwd(x, w, *, name):
    S, D = x.shape
    ts = _row_tile(S)

    def body(x_ref, w_ref, u_ref):
        xv = x_ref[...]
        r = lax.rsqrt(jnp.mean(xv * xv, axis=-1, keepdims=True) + NORM_EPS)
        u_ref[...] = (xv * r * w_ref[...]).astype(BF16)

    return pl.pallas_call(
        body, name=name, grid=(S // ts,),
        in_specs=[pl.BlockSpec((ts, D), lambda i: (i, 0)), pl.BlockSpec((1, D), lambda i: (0, 0))],
        out_specs=pl.BlockSpec((ts, D), lambda i: (i, 0)),
        out_shape=jax.ShapeDtypeStruct((S, D), BF16),
        compiler_params=_cparams("parallel"),
    )(x, w)


def _rms_bwd(x, w, du, g, *, name):
    S, D = x.shape
    ts = _row_tile(S)

    def body(x_ref, w_ref, du_ref, g_ref, dx_ref, dxb_ref, dw_ref):
        xv = x_ref[...]
        r = lax.rsqrt(jnp.mean(xv * xv, axis=-1, keepdims=True) + NORM_EPS)
        n = xv * r
        duv = du_ref[...]
        dn = duv * w_ref[...]
        dx = g_ref[...] + r * (dn - n * jnp.mean(dn * n, axis=-1, keepdims=True))
        dx_ref[...] = dx
        dxb_ref[...] = dx.astype(BF16)
        _acc_out(dw_ref, jnp.sum(duv * n, axis=0, keepdims=True), pl.program_id(0) == 0)

    row = pl.BlockSpec((ts, D), lambda i: (i, 0))
    vec = pl.BlockSpec((1, D), lambda i: (0, 0))
    return pl.pallas_call(
        body, name=name, grid=(S // ts,),
        in_specs=[row, vec, row, row],
        out_specs=[row, row, vec],
        out_shape=[jax.ShapeDtypeStruct((S, D), F32), jax.ShapeDtypeStruct((S, D), BF16),
                   jax.ShapeDtypeStruct((1, D), F32)],
        compiler_params=_cparams("arbitrary"),
    )(x, w, du, g)


def _loss_head(x, w, target, *, name):
    S, D = x.shape
    ts = _row_tile(S)

    def body(x_ref, w_ref, t_ref, loss_ref, dx_ref, dxb_ref, dw_ref):
        xv = x_ref[...]
        wv = w_ref[...]
        r = lax.rsqrt(jnp.mean(xv * xv, axis=-1, keepdims=True) + NORM_EPS)
        n = xv * r
        err = n * wv - t_ref[...]
        part = 0.5 * jnp.sum(jnp.mean(err * err, axis=-1, keepdims=True), axis=0, keepdims=True)
        dy = err * (1.0 / D)
        dn = dy * wv
        dx = r * (dn - n * jnp.mean(dn * n, axis=-1, keepdims=True))
        dx_ref[...] = dx
        dxb_ref[...] = dx.astype(BF16)
        first = pl.program_id(0) == 0
        _acc_out(dw_ref, jnp.sum(dy * n, axis=0, keepdims=True), first)
        _acc_out(loss_ref, jnp.broadcast_to(part, loss_ref.shape), first)

    row = pl.BlockSpec((ts, D), lambda i: (i, 0))
    vec = pl.BlockSpec((1, D), lambda i: (0, 0))
    return pl.pallas_call(
        body, name=name, grid=(S // ts,),
        in_specs=[row, vec, row],
        out_specs=[pl.BlockSpec((SUBLANE, LANE), lambda i: (0, 0)), row, row, vec],
        out_shape=[jax.ShapeDtypeStruct((SUBLANE, LANE), F32), jax.ShapeDtypeStruct((S, D), F32),
                   jax.ShapeDtypeStruct((S, D), BF16), jax.ShapeDtypeStruct((1, D), F32)],
        compiler_params=_cparams("arbitrary"),
    )(x, w, target)


def _sigmoid(x):
    return 1.0 / (1.0 + jnp.exp(-x))


def _merge_fwd(gl, ya, yb, yc, *, name):
    S, D = ya.shape
    ts = _row_tile(S)

    def body(ga_ref, gb_ref, gc_ref, ya_ref, yb_ref, yc_ref, o_ref):
        m = (_sigmoid(ga_ref[...]) * ya_ref[...] + _sigmoid(gb_ref[...]) * yb_ref[...]
             + _sigmoid(gc_ref[...]) * yc_ref[...])
        o_ref[...] = m.astype(BF16)

    row = pl.BlockSpec((ts, D), lambda i: (i, 0))
    gspec = [pl.BlockSpec((ts, D), functools.partial(lambda b, i: (i, b), b)) for b in range(3)]
    return pl.pallas_call(
        body, name=name, grid=(S // ts,),
        in_specs=gspec + [row, row, row],
        out_specs=row,
        out_shape=jax.ShapeDtypeStruct((S, D), BF16),
        compiler_params=_cparams("parallel"),
    )(gl, gl, gl, ya, yb, yc)


def _merge_bwd(gl, ya, yb, yc, dm, *, name):
    S, D = ya.shape
    ts = _row_tile(S)

    def body(ga_ref, gb_ref, gc_ref, ya_ref, yb_ref, yc_ref, dm_ref, dgl_ref, da_ref, db_ref, dc_ref):
        dmv = dm_ref[...]
        for b, (g_ref, y_ref, dy_ref) in enumerate(((ga_ref, ya_ref, da_ref), (gb_ref, yb_ref, db_ref),
                                                     (gc_ref, yc_ref, dc_ref))):
            s = _sigmoid(g_ref[...])
            dy_ref[...] = (dmv * s).astype(BF16)
            dgl_ref[:, b * D:(b + 1) * D] = (dmv * y_ref[...] * s * (1.0 - s)).astype(BF16)

    row = pl.BlockSpec((ts, D), lambda i: (i, 0))
    gspec = [pl.BlockSpec((ts, D), functools.partial(lambda b, i: (i, b), b)) for b in range(3)]
    return pl.pallas_call(
        body, name=name, grid=(S // ts,),
        in_specs=gspec + [row, row, row, row],
        out_specs=[pl.BlockSpec((ts, 3 * D), lambda i: (i, 0)), row, row, row],
        out_shape=[jax.ShapeDtypeStruct((S, 3 * D), BF16)] + [jax.ShapeDtypeStruct((S, D), BF16)] * 3,
        compiler_params=_cparams("parallel"),
    )(gl, gl, gl, ya, yb, yc, dm)


POOL_WIDTH = 512


def _pool_cnt(t, w):
    return jnp.minimum(t + 1, w).astype(F32)


def _pool_fwd(rest, col, mix, scale, *, name):
    S = rest.shape[0]
    W, G, H = POOL_WIDTH, POOL_GROUP_DIM, POOL_HALO
    ts = _row_tile(S)

    def body(v_ref, h_ref, mix_ref, sc_ref, d_ref, y_ref, ext_ref):
        i = pl.program_id(0)
        cur = v_ref[...]
        ext_ref[0:H, :] = jnp.where(i > 0, h_ref[...], 0.0)
        ext_ref[H:, :] = cur
        t = i * ts + lax.broadcasted_iota(jnp.int32, (ts, 1), 0)
        for g, w in enumerate(POOL_WINDOWS):
            cols = slice(g * G, (g + 1) * G)
            acc = cur[:, cols]
            for k in range(1, w):
                acc = acc + ext_ref[pl.ds(H - k, ts), cols]
            db = (acc / _pool_cnt(t, w) - cur[:, cols]).astype(BF16)
            d_ref[:, cols] = db
            y = jnp.dot(db, mix_ref[g], preferred_element_type=F32) * sc_ref[:, cols]
            y_ref[:, cols] = y.astype(BF16)

    row = pl.BlockSpec((ts, W), lambda i: (i, 0))
    return pl.pallas_call(
        body, name=name, grid=(S // ts,),
        in_specs=[pl.BlockSpec((ts, W), lambda i: (i, col)),
                  pl.BlockSpec((H, W), lambda i: (jnp.maximum(i * (ts // H) - 1, 0), col)),
                  pl.BlockSpec((len(POOL_WINDOWS), G, G), lambda i: (0, 0, 0)),
                  pl.BlockSpec((1, W), lambda i: (0, 0))],
        out_specs=[row, row],
        out_shape=[jax.ShapeDtypeStruct((S, W), BF16)] * 2,
        scratch_shapes=[pltpu.VMEM((H + ts, W), F32)],
        compiler_params=_cparams("parallel"),
    )(rest, rest, mix, scale)


def _pool_bwd(dy, d, mix, scale, *, name):
    S = dy.shape[0]
    W, G, H = POOL_WIDTH, POOL_GROUP_DIM, POOL_HALO
    ts = _row_tile(S)
    n = S // ts
    NT = (((1,), (1,)), ((), ()))
    TN = (((0,), (0,)), ((), ()))

    def body(dy_ref, dyn_ref, d_ref, mix_ref, sc_ref, dv_ref, dmix_ref, dsc_ref, ext_ref):
        i = pl.program_id(0)

        @pl.when(i == 0)
        def _():
            dmix_ref[...] = jnp.zeros_like(dmix_ref)
            dsc_ref[...] = jnp.zeros_like(dsc_ref)

        t = i * ts + lax.broadcasted_iota(jnp.int32, (ts, 1), 0)
        tn = (i + 1) * ts + lax.broadcasted_iota(jnp.int32, (H, 1), 0)
        for g, w in enumerate(POOL_WINDOWS):
            cols = slice(g * G, (g + 1) * G)
            sc = sc_ref[:, cols]
            dyv = dy_ref[:, cols]
            db = d_ref[:, cols]
            mg = mix_ref[g]
            yp = jnp.dot(db, mg, preferred_element_type=F32)
            dsc_ref[:, cols] += jnp.sum(dyv * yp, axis=0, keepdims=True)
            dyp = (dyv * sc).astype(BF16)
            dd = lax.dot_general(dyp, mg, NT, preferred_element_type=F32)
            dmix_ref[g] += lax.dot_general(db, dyp, TN, preferred_element_type=F32)
            dyn = (jnp.where(i < n - 1, dyn_ref[:, cols], 0.0) * sc).astype(BF16)
            ddn = lax.dot_general(dyn, mg, NT, preferred_element_type=F32)
            e = dd / _pool_cnt(t, w)
            ext_ref[0:ts, cols] = e
            ext_ref[ts:, cols] = ddn / _pool_cnt(tn, w)
            acc = e
            for k in range(1, w):
                acc = acc + ext_ref[pl.ds(k, ts), cols]
            dv_ref[:, cols] = (acc - dd).astype(BF16)

    row = pl.BlockSpec((ts, W), lambda i: (i, 0))
    return pl.pallas_call(
        body, name=name, grid=(n,),
        in_specs=[row,
                  pl.BlockSpec((H, W), lambda i: (jnp.minimum((i + 1) * (ts // H), S // H - 1), 0)),
                  row,
                  pl.BlockSpec((len(POOL_WINDOWS), G, G), lambda i: (0, 0, 0)),
                  pl.BlockSpec((1, W), lambda i: (0, 0))],
        out_specs=[row, pl.BlockSpec((len(POOL_WINDOWS), G, G), lambda i: (0, 0, 0)),
                   pl.BlockSpec((1, W), lambda i: (0, 0))],
        out_shape=[jax.ShapeDtypeStruct((S, W), BF16),
                   jax.ShapeDtypeStruct((len(POOL_WINDOWS), G, G), F32),
                   jax.ShapeDtypeStruct((1, W), F32)],
        scratch_shapes=[pltpu.VMEM((ts + H, W), F32)],
        compiler_params=_cparams("arbitrary"),
    )(dy, dy, d, mix, scale)


def _conv_taps(ext_ref, w_ref, b_ref, rows, first):
    K = w_ref.shape[0]
    pre = b_ref[...] + w_ref[K - 1:K, :] * ext_ref[pl.ds(first, rows), :]
    for k in range(K - 1):
        pre = pre + w_ref[k:k + 1, :] * ext_ref[pl.ds(first - (K - 1) + k, rows), :]
    return pre


def _prev_halo_spec(ts, tc, colfn):
    return pl.BlockSpec((HALO, tc), lambda *g: (jnp.maximum(g[-1] * (ts // HALO) - 1, 0), colfn(*g)))


def _conv_silu_fwd(rest, col0, C, w, b, *, name):
    S = rest.shape[0]
    ts, tc = _row_tile(S), 512
    assert col0 % tc == 0 and C % tc == 0
    cb = col0 // tc

    def body(x_ref, h_ref, w_ref, b_ref, o_ref, ext_ref):
        i = pl.program_id(1)
        ext_ref[0:HALO, :] = jnp.where(i > 0, h_ref[...], 0.0)
        ext_ref[HALO:, :] = x_ref[...]
        pre = _conv_taps(ext_ref, w_ref, b_ref, ts, HALO)
        o_ref[...] = pre * _sigmoid(pre)

    K = w.shape[0]
    return pl.pallas_call(
        body, name=name, grid=(C // tc, S // ts),
        in_specs=[pl.BlockSpec((ts, tc), lambda j, i: (i, cb + j)),
                  _prev_halo_spec(ts, tc, lambda j, i: cb + j),
                  pl.BlockSpec((K, tc), lambda j, i: (0, j)),
                  pl.BlockSpec((1, tc), lambda j, i: (0, j))],
        out_specs=pl.BlockSpec((ts, tc), lambda j, i: (i, j)),
        out_shape=jax.ShapeDtypeStruct((S, C), F32),
        scratch_shapes=[pltpu.VMEM((HALO + ts, tc), F32)],
        compiler_params=_cparams("parallel", "parallel"),
    )(rest, rest, w, b)


def _dsilu(pre):
    s = _sigmoid(pre)
    return s, s * (1.0 + pre * (1.0 - s))


def _conv_silu_bwd(rest, col0, C, w, b, dy, *, name):
    S = rest.shape[0]
    ts, tc = _row_tile(S), 512
    cb = col0 // tc
    n = S // ts
    K = w.shape[0]
    R = ts + HALO

    def body(x_ref, hp_ref, hn_ref, w_ref, b_ref, dy_ref, dyn_ref, dx_ref, dw_ref, db_ref, ext_ref, g_ref):
        i = pl.program_id(1)
        last = i == n - 1
        ext_ref[0:HALO, :] = jnp.where(i > 0, hp_ref[...], 0.0)
        ext_ref[HALO:HALO + ts, :] = x_ref[...]
        ext_ref[HALO + ts:, :] = jnp.where(last, 0.0, hn_ref[...])
        g_ref[0:ts, :] = dy_ref[...]
        g_ref[ts:, :] = jnp.where(last, 0.0, dyn_ref[...])
        pre = _conv_taps(ext_ref, w_ref, b_ref, R, HALO)
        g_ref[...] = g_ref[...] * _dsilu(pre)[1]
        dx = w_ref[K - 1:K, :] * g_ref[pl.ds(0, ts), :]
        for k in range(K - 1):
            dx = dx + w_ref[k:k + 1, :] * g_ref[pl.ds(K - 1 - k, ts), :]
        dx_ref[...] = dx.astype(BF16)
        gc = g_ref[pl.ds(0, ts), :]
        first = i == 0
        _acc_out(db_ref, jnp.sum(gc, axis=0, keepdims=True), first)
        dws = [jnp.sum(gc * ext_ref[pl.ds(HALO - (K - 1) + k, ts), :], axis=0, keepdims=True) for k in range(K)]
        _acc_out(dw_ref, jnp.concatenate(dws, axis=0), first)

    nxt = lambda cf: pl.BlockSpec((HALO, tc), lambda j, i: (jnp.minimum((i + 1) * (ts // HALO), S // HALO - 1), cf(j)))
    return pl.pallas_call(
        body, name=name, grid=(C // tc, n),
        in_specs=[pl.BlockSpec((ts, tc), lambda j, i: (i, cb + j)),
                  _prev_halo_spec(ts, tc, lambda j, i: cb + j),
                  nxt(lambda j: cb + j),
                  pl.BlockSpec((K, tc), lambda j, i: (0, j)),
                  pl.BlockSpec((1, tc), lambda j, i: (0, j)),
                  pl.BlockSpec((ts, tc), lambda j, i: (i, j)),
                  nxt(lambda j: j)],
        out_specs=[pl.BlockSpec((ts, tc), lambda j, i: (i, j)),
                   pl.BlockSpec((K, tc), lambda j, i: (0, j)),
                   pl.BlockSpec((1, tc), lambda j, i: (0, j))],
        out_shape=[jax.ShapeDtypeStruct((S, C), BF16), jax.ShapeDtypeStruct((K, C), F32),
                   jax.ShapeDtypeStruct((1, C), F32)],
        scratch_shapes=[pltpu.VMEM((HALO + R, tc), F32), pltpu.VMEM((R, tc), F32)],
        compiler_params=_cparams("parallel", "arbitrary"),
    )(rest, rest, rest, w, b, dy, dy)


FFN_TC = 256


def _ffn_act_fwd(h0, w, b, *, name):
    S, F2 = h0.shape
    F = F2 // 2
    ts, tc = _row_tile(S), FFN_TC
    nb = F // tc
    K = w.shape[0]

    def body(xg_ref, hg_ref, xv_ref, hv_ref, wg_ref, wv_ref, bg_ref, bv_ref, o_ref, eg_ref, ev_ref):
        i = pl.program_id(1)
        for x_ref, h_ref, e_ref in ((xg_ref, hg_ref, eg_ref), (xv_ref, hv_ref, ev_ref)):
            e_ref[0:HALO, :] = jnp.where(i > 0, h_ref[...], 0.0)
            e_ref[HALO:, :] = x_ref[...]
        pg = _conv_taps(eg_ref, wg_ref, bg_ref, ts, HALO)
        pv = _conv_taps(ev_ref, wv_ref, bv_ref, ts, HALO)
        o_ref[...] = (pg * _sigmoid(pg) * pv).astype(BF16)

    def half(off):
        return [pl.BlockSpec((ts, tc), lambda j, i: (i, off + j)), _prev_halo_spec(ts, tc, lambda j, i: off + j)]

    wspec = lambda off: pl.BlockSpec((K, tc), lambda j, i: (0, off + j))
    bspec = lambda off: pl.BlockSpec((1, tc), lambda j, i: (0, off + j))
    return pl.pallas_call(
        body, name=name, grid=(nb, S // ts),
        in_specs=half(0) + half(nb) + [wspec(0), wspec(nb), bspec(0), bspec(nb)],
        out_specs=pl.BlockSpec((ts, tc), lambda j, i: (i, j)),
        out_shape=jax.ShapeDtypeStruct((S, F), BF16),
        scratch_shapes=[pltpu.VMEM((HALO + ts, tc), F32)] * 2,
        compiler_params=_cparams("parallel", "parallel"),
    )(h0, h0, h0, h0, w, w, b, b)


def _ffn_act_bwd(h0, w, b, da, *, name):
    S, F2 = h0.shape
    F = F2 // 2
    ts, tc = _row_tile(S), FFN_TC
    nb = F // tc
    n = S // ts
    K = w.shape[0]
    R = ts + HALO

    def body(xg_ref, pg_ref, ng_ref, xv_ref, pv_ref, nv_ref, wg_ref, wv_ref, bg_ref, bv_ref, da_ref, dan_ref,
             dx_ref, dw_ref, db_ref, eg_ref, ev_ref, gg_ref, gv_ref):
        i = pl.program_id(1)
        last = i == n - 1
        for x_ref, p_ref, n_ref, e_ref in ((xg_ref, pg_ref, ng_ref, eg_ref), (xv_ref, pv_ref, nv_ref, ev_ref)):
            e_ref[0:HALO, :] = jnp.where(i > 0, p_ref[...], 0.0)
            e_ref[HALO:HALO + ts, :] = x_ref[...]
            e_ref[HALO + ts:, :] = jnp.where(last, 0.0, n_ref[...])
        gg_ref[0:ts, :] = da_ref[...]
        gg_ref[ts:, :] = jnp.where(last, 0.0, dan_ref[...])
        dav = gg_ref[...]
        pg = _conv_taps(eg_ref, wg_ref, bg_ref, R, HALO)
        pv = _conv_taps(ev_ref, wv_ref, bv_ref, R, HALO)
        s, ds = _dsilu(pg)
        gg_ref[...] = dav * pv * ds
        gv_ref[...] = dav * pg * s
        first = i == 0
        for h, (g_ref, e_ref, w_ref) in enumerate(((gg_ref, eg_ref, wg_ref), (gv_ref, ev_ref, wv_ref))):
            dx = w_ref[K - 1:K, :] * g_ref[pl.ds(0, ts), :]
            for k in range(K - 1):
                dx = dx + w_ref[k:k + 1, :] * g_ref[pl.ds(K - 1 - k, ts), :]
            dx_ref[h] = dx.astype(BF16)
            gc = g_ref[pl.ds(0, ts), :]
            dws = [jnp.sum(gc * e_ref[pl.ds(HALO - (K - 1) + k, ts), :], axis=0, keepdims=True) for k in range(K)]
            _acc_out(db_ref.at[h], jnp.sum(gc, axis=0, keepdims=True), first)
            _acc_out(dw_ref.at[h], jnp.concatenate(dws, axis=0), first)

    nxt = lambda off: pl.BlockSpec((HALO, tc), lambda j, i: (jnp.minimum((i + 1) * (ts // HALO), S // HALO - 1), off + j))

    def half(off):
        return [pl.BlockSpec((ts, tc), lambda j, i: (i, off + j)), _prev_halo_spec(ts, tc, lambda j, i: off + j), nxt(off)]

    wspec = lambda off: pl.BlockSpec((K, tc), lambda j, i: (0, off + j))
    bspec = lambda off: pl.BlockSpec((1, tc), lambda j, i: (0, off + j))
    return pl.pallas_call(
        body, name=name, grid=(nb, n),
        in_specs=half(0) + half(nb) + [wspec(0), wspec(nb), bspec(0), bspec(nb),
                                       pl.BlockSpec((ts, tc), lambda j, i: (i, j)), nxt(0)],
        out_specs=[pl.BlockSpec((2, ts, tc), lambda j, i: (0, i, j)),
                   pl.BlockSpec((2, K, tc), lambda j, i: (0, 0, j)),
                   pl.BlockSpec((2, 1, tc), lambda j, i: (0, 0, j))],
        out_shape=[jax.ShapeDtypeStruct((2, S, F), BF16), jax.ShapeDtypeStruct((2, K, F), F32),
                   jax.ShapeDtypeStruct((2, 1, F), F32)],
        scratch_shapes=[pltpu.VMEM((HALO + R, tc), F32)] * 2 + [pltpu.VMEM((R, tc), F32)] * 2,
        compiler_params=_cparams("parallel", "arbitrary"),
    )(h0, h0, h0, h0, h0, h0, w, w, b, b, da, da)


CUM_TILE = 256
ATT_TILE = 512
ATT_TILE_FWD = 1024
NEG = -1e30
N_PAIR = ATTN_HEADS // 2
NT_DIMS = (((1,), (1,)), ((), ()))


def _split3(x):
    a = x.astype(BF16)
    r = x - a.astype(F32)
    b = r.astype(BF16)
    c = (r - b.astype(F32)).astype(BF16)
    return a, b, c


def _tri_dot(tri, x):
    return sum(jnp.dot(tri, p, preferred_element_type=F32) for p in _split3(x))


def _log_sigmoid(x):
    return jnp.minimum(x, 0.0) - jnp.log(1.0 + jnp.exp(-jnp.abs(x)))


def _col(v, idx, lane):
    return jnp.sum(jnp.where(lane == idx, v, 0.0), axis=1, keepdims=True)


def _fox_pre(rest, col, fb, *, name):
    S = rest.shape[0]
    ts = min(CUM_TILE, S)

    def body(f_ref, fb_ref, ccol_ref, carry_ref):
        i = pl.program_id(0)

        @pl.when(i == 0)
        def _():
            carry_ref[...] = jnp.zeros_like(carry_ref)

        lane = lax.broadcasted_iota(jnp.int32, (1, LANE), 1)
        logf = jnp.where(lane < ATTN_HEADS, _log_sigmoid(f_ref[...] + fb_ref[...]), 0.0)
        r = lax.broadcasted_iota(jnp.int32, (ts, ts), 0)
        c = lax.broadcasted_iota(jnp.int32, (ts, ts), 1)
        tri = jnp.where(c <= r, 1.0, 0.0).astype(BF16)
        cs = _tri_dot(tri, logf) + carry_ref[...]
        ccol_ref[...] = cs
        carry_ref[...] = cs[ts - 1:ts, :]

    return pl.pallas_call(
        body, name=name, grid=(S // ts,),
        in_specs=[pl.BlockSpec((ts, LANE), lambda i: (i, col)), pl.BlockSpec((1, LANE), lambda i: (0, 0))],
        out_specs=pl.BlockSpec((ts, LANE), lambda i: (i, 0)),
        out_shape=jax.ShapeDtypeStruct((S, LANE), F32),
        scratch_shapes=[pltpu.VMEM((1, LANE), F32)],
        compiler_params=_cparams("arbitrary"),
    )(rest, fb)


def _fox_post(dcq, dck, rest, col, fb, ddt, *, name):
    S = rest.shape[0]
    ts = min(CUM_TILE, S)
    n = S // ts

    def body(dcq_ref, dck_ref, f_ref, fb_ref, ddt_ref, o_ref, db_ref, carry_ref):
        i = pl.program_id(0)

        @pl.when(i == 0)
        def _():
            carry_ref[...] = jnp.zeros_like(carry_ref)

        lane = lax.broadcasted_iota(jnp.int32, (1, LANE), 1)
        dc = jnp.zeros((ts, LANE), F32)
        for h in range(ATTN_HEADS):
            d = dcq_ref[h // 2] - dck_ref[h // 2]
            dc = jnp.where(lane == h, _col(d, h % 2, lane), dc)
        r = lax.broadcasted_iota(jnp.int32, (ts, ts), 0)
        c = lax.broadcasted_iota(jnp.int32, (ts, ts), 1)
        tri = jnp.where(c >= r, 1.0, 0.0).astype(BF16)
        rc = _tri_dot(tri, dc) + carry_ref[...]
        carry_ref[...] = rc[0:1, :]
        df = rc * _sigmoid(-(f_ref[...] + fb_ref[...]))
        out = jnp.where(lane < ATTN_HEADS, df, ddt_ref[...])
        o_ref[...] = out.astype(BF16)
        _acc_out(db_ref, jnp.sum(out, axis=0, keepdims=True), i == 0)

    rev = lambda i: n - 1 - i
    return pl.pallas_call(
        body, name=name, grid=(n,),
        in_specs=[pl.BlockSpec((N_PAIR, ts, LANE), lambda i: (0, rev(i), 0)),
                  pl.BlockSpec((N_PAIR, ts, LANE), lambda i: (0, rev(i), 0)),
                  pl.BlockSpec((ts, LANE), lambda i: (rev(i), col)),
                  pl.BlockSpec((1, LANE), lambda i: (0, 0)),
                  pl.BlockSpec((ts, LANE), lambda i: (rev(i), 0))],
        out_specs=[pl.BlockSpec((ts, LANE), lambda i: (rev(i), 0)), pl.BlockSpec((1, LANE), lambda i: (0, 0))],
        out_shape=[jax.ShapeDtypeStruct((S, LANE), BF16), jax.ShapeDtypeStruct((1, LANE), F32)],
        scratch_shapes=[pltpu.VMEM((1, LANE), F32)],
        compiler_params=_cparams("arbitrary"),
    )(dcq, dck, rest, fb, ddt)


def _head_masks():
    lane = lax.broadcasted_iota(jnp.int32, (1, LANE), 1)
    return lane, (lane < ATTN_HEAD_DIM, lane >= ATTN_HEAD_DIM)


def _causal_mask(t):
    r = lax.broadcasted_iota(jnp.int32, (t, t), 0)
    c = lax.broadcasted_iota(jnp.int32, (t, t), 1)
    return r, c


AUG_ONE = 0
AUG_C = 3


def _free_lane0(hh):
    return ATTN_HEAD_DIM if hh == 0 else 0


def _augment(x, head_mask, lane, f0, c, key_side):
    terms = _split3(c)
    one = jnp.ones((), BF16)
    ones_at, terms_at = (AUG_ONE, AUG_C) if key_side else (AUG_C, AUG_ONE)
    out = jnp.where(head_mask, x, jnp.zeros((), BF16))
    for k in range(3):
        out = jnp.where(lane == f0 + ones_at + k, one, out)
        out = jnp.where(lane == f0 + terms_at + k, terms[k], out)
    return out


def _carry_start(kind, in_refs, out_refs, sems, first):
    @pl.when(first)
    def _():
        _a2a_start(kind, in_refs, out_refs, sems)


def _carry_wait(kind, in_refs, out_refs, sems, last):
    @pl.when(last)
    def _():
        _a2a_wait(kind, in_refs, out_refs, sems)


def _attn_fwd(qkv, ccol, *, comm=None, name):
    S = qkv.shape[0]
    t = min(ATT_TILE_FWD, S)
    n = S // t
    scale = ATTN_HEAD_DIM ** -0.5

    nc = len(comm[1]) if comm else 0

    def body(*refs):
        q_ref, k_ref, v_ref, cc_ref = refs[:4]
        o_ref, lrow_ref = refs[4 + nc:6 + nc]
        kaug_ref, vt_ref, m_ref, l_ref, acc_ref = refs[6 + 2 * nc:11 + 2 * nc]
        j, i = pl.program_id(0), pl.program_id(1)
        lane, masks = _head_masks()
        if comm:
            _carry_start(comm[0], refs[4:4 + nc], refs[6 + nc:6 + 2 * nc], refs[11 + 2 * nc:],
                         jnp.logical_and(j == 0, i == 0))

        @pl.when(i == 0)
        def _():
            def fill(b, carry):
                off = pl.multiple_of(b * t, t)
                kblk = k_ref[pl.ds(off, t), :]
                ccb = cc_ref[pl.ds(off, t), :]
                for hh in range(2):
                    ck = _col(ccb, 2 * j + hh, lane)
                    kaug_ref[hh, pl.ds(off, t), :] = _augment(kblk, masks[hh], lane, _free_lane0(hh), -ck, True)
                vt_ref[:, pl.ds(off, t)] = v_ref[pl.ds(off, t), :].astype(F32).T.astype(BF16)
                return carry
            lax.fori_loop(0, n, fill, 0)

        q2 = q_ref[...] * scale
        ccq = cc_ref[pl.ds(pl.multiple_of(i * t, t), t), :]
        qaug = [_augment(q2, masks[hh], lane, _free_lane0(hh), _col(ccq, 2 * j + hh, lane), False) for hh in range(2)]
        m_ref[...] = jnp.full(m_ref.shape, NEG, F32)
        l_ref[...] = jnp.zeros_like(l_ref)
        acc_ref[...] = jnp.zeros_like(acc_ref)

        def step(kb, masked):
            off = pl.multiple_of(kb * t, t)
            vt = vt_ref[:, pl.ds(off, t)]
            for hh in range(2):
                st = lax.dot_general(kaug_ref[hh, pl.ds(off, t), :], qaug[hh], NT_DIMS, preferred_element_type=F32)
                if masked:
                    r, c = _causal_mask(t)
                    st = jnp.where(r <= c, st, NEG)
                m_old = m_ref[hh]
                m_new = jnp.maximum(m_old, jnp.max(st, axis=0, keepdims=True))
                p = jnp.exp(st - m_new)
                alpha = jnp.exp(m_old - m_new)
                l_ref[hh] = alpha * l_ref[hh] + jnp.sum(p, axis=0, keepdims=True)
                acc_ref[hh] = alpha * acc_ref[hh] + jnp.dot(vt, p.astype(BF16), preferred_element_type=F32)
                m_ref[hh] = m_new

        def loop_body(kb, carry):
            step(kb, False)
            return carry

        lax.fori_loop(0, i, loop_body, 0)
        step(i, True)
        drow = lax.broadcasted_iota(jnp.int32, (LANE, 1), 0)
        ot = jnp.where(drow < ATTN_HEAD_DIM, acc_ref[0] / l_ref[0], acc_ref[1] / l_ref[1])
        o_ref[...] = ot.T.astype(BF16)
        for hh in range(2):
            lrow_ref[hh] = m_ref[hh] + jnp.log(l_ref[hh])
        if comm:
            _carry_wait(comm[0], refs[4:4 + nc], refs[6 + nc:6 + 2 * nc], refs[11 + 2 * nc:],
                        jnp.logical_and(j == N_PAIR - 1, i == n - 1))

    res = pl.pallas_call(
        body, name=name, grid=(N_PAIR, n),
        in_specs=[pl.BlockSpec((t, LANE), lambda j, i: (i, j)),
                  pl.BlockSpec((S, LANE), lambda j, i: (0, N_PAIR + j)),
                  pl.BlockSpec((S, LANE), lambda j, i: (0, 2 * N_PAIR + j)),
                  pl.BlockSpec((S, LANE), lambda j, i: (0, 0))] + [ANY] * nc,
        out_specs=[pl.BlockSpec((t, LANE), lambda j, i: (i, j)),
                   pl.BlockSpec((2, 1, t), lambda j, i: (j, 0, i))] + [ANY] * nc,
        out_shape=[jax.ShapeDtypeStruct((S, N_PAIR * LANE), BF16), jax.ShapeDtypeStruct((ATTN_HEADS, 1, S), F32)]
        + (_a2a_out_shapes(*comm) if comm else []),
        scratch_shapes=[pltpu.VMEM((2, S, LANE), BF16), pltpu.VMEM((LANE, S), BF16),
                        pltpu.VMEM((2, 1, t), F32), pltpu.VMEM((2, 1, t), F32), pltpu.VMEM((2, LANE, t), F32)]
        + (_a2a_sems(nc) if comm else []),
        compiler_params=_cparams("arbitrary" if comm else "parallel", "arbitrary"),
    )(qkv, qkv, qkv, ccol, *(comm[1] if comm else []))
    return res[0], res[1], list(res[2:])


def _attn_bwd(qkv, o, do, lrow, ccol, *, comm=None, name):
    S = qkv.shape[0]
    t = min(ATT_TILE, S)
    n = S // t
    scale = ATTN_HEAD_DIM ** -0.5

    nc = len(comm[1]) if comm else 0

    def body(*refs):
        q_ref, k_ref, v_ref, o_ref, do_ref, lrow_ref, cc_ref = refs[:7]
        dq_ref, dk_ref, dv_ref, dcq_ref, dck_ref = refs[7 + nc:12 + nc]
        qaug_ref, drow_ref, dqt_ref, dka_ref, dva_ref = refs[12 + 2 * nc:17 + 2 * nc]
        carried = (refs[7:7 + nc], refs[12 + nc:12 + 2 * nc], refs[17 + 2 * nc:])
        j, kb = pl.program_id(0), pl.program_id(1)
        lane, masks = _head_masks()
        if comm:
            _carry_start(comm[0], *carried, jnp.logical_and(j == 0, kb == 0))
        drows = lax.broadcasted_iota(jnp.int32, (LANE, 1), 0)

        @pl.when(kb == 0)
        def _():
            dqt_ref[...] = jnp.zeros_like(dqt_ref)

            def fill(b, carry):
                off = pl.multiple_of(b * t, t)
                q2 = q_ref[pl.ds(off, t), :] * scale
                ccb = cc_ref[pl.ds(off, t), :]
                prod_t = (do_ref[pl.ds(off, t), :].astype(F32) * o_ref[pl.ds(off, t), :].astype(F32)).T
                lo = jnp.sum(jnp.where(drows < ATTN_HEAD_DIM, prod_t, 0.0), axis=0, keepdims=True)
                drow_ref[0, :, pl.ds(off, t)] = lo
                drow_ref[1, :, pl.ds(off, t)] = jnp.sum(prod_t, axis=0, keepdims=True) - lo
                for hh in range(2):
                    cq = _col(ccb, 2 * j + hh, lane)
                    qaug_ref[hh, pl.ds(off, t), :] = _augment(q2, masks[hh], lane, _free_lane0(hh), cq, False)
                return carry
            lax.fori_loop(0, n, fill, 0)

        koff = pl.multiple_of(kb * t, t)
        kblk = k_ref[...]
        v2 = v_ref[...]
        cck = cc_ref[pl.ds(koff, t), :]
        kaug = [_augment(kblk, masks[hh], lane, _free_lane0(hh), -_col(cck, 2 * j + hh, lane), True) for hh in range(2)]
        kaug_t = [ka.astype(F32).T.astype(BF16) for ka in kaug]
        vh = [jnp.where(mk, v2, jnp.zeros((), BF16)) for mk in masks]
        dka_ref[...] = jnp.zeros_like(dka_ref)
        dva_ref[...] = jnp.zeros_like(dva_ref)

        def step(qb, masked):
            off = pl.multiple_of(qb * t, t)
            doblk = do_ref[pl.ds(off, t), :]
            for hh in range(2):
                qa = qaug_ref[hh, pl.ds(off, t), :]
                st = lax.dot_general(kaug[hh], qa, NT_DIMS, preferred_element_type=F32)
                if masked:
                    r, c = _causal_mask(t)
                    st = jnp.where(r <= c, st, NEG)
                pt = jnp.exp(st - lrow_ref[hh, :, pl.ds(off, t)])
                dpt = lax.dot_general(vh[hh], doblk, NT_DIMS, preferred_element_type=F32)
                dst = (pt * (dpt - drow_ref[hh, :, pl.ds(off, t)])).astype(BF16)
                dva_ref[hh] += jnp.dot(pt.astype(BF16), doblk, preferred_element_type=F32)
                dka_ref[hh] += jnp.dot(dst, qa, preferred_element_type=F32)
                dqt_ref[hh, :, pl.ds(off, t)] += jnp.dot(kaug_t[hh], dst, preferred_element_type=F32)

        step(kb, True)

        def loop_body(qb, carry):
            step(qb, False)
            return carry

        lax.fori_loop(kb + 1, n, loop_body, 0)
        dk_ref[...] = jnp.where(masks[0], dka_ref[0], dka_ref[1]).astype(BF16)
        dv_ref[...] = jnp.where(masks[0], dva_ref[0], dva_ref[1]).astype(BF16)
        dck = [_col(dka_ref[hh], _free_lane0(hh) + AUG_C, lane) for hh in range(2)]
        dck_ref[0] = jnp.where(lane == 0, dck[0], jnp.where(lane == 1, dck[1], 0.0))

        @pl.when(kb == n - 1)
        def _():
            def flush(b, carry):
                off = pl.multiple_of(b * t, t)
                d = [dqt_ref[hh, :, pl.ds(off, t)].T for hh in range(2)]
                dq_ref[pl.ds(off, t), :] = (jnp.where(masks[0], d[0], d[1]) * scale).astype(BF16)
                dcq = [_col(d[hh], _free_lane0(hh) + AUG_ONE, lane) for hh in range(2)]
                dcq_ref[0, pl.ds(off, t), :] = jnp.where(lane == 0, dcq[0], jnp.where(lane == 1, dcq[1], 0.0))
                return carry
            lax.fori_loop(0, n, flush, 0)

        if comm:
            _carry_wait(comm[0], *carried, jnp.logical_and(j == N_PAIR - 1, kb == n - 1))

    full = lambda cb: pl.BlockSpec((S, LANE), lambda j, kb: (0, cb(j)))
    kspec = lambda base: pl.BlockSpec((t, LANE), lambda j, kb: (kb, base + j))
    oblk = pl.BlockSpec((t, LANE), lambda j, kb: (kb, j))
    res = pl.pallas_call(
        body, name=name, grid=(N_PAIR, n),
        in_specs=[full(lambda j: j), kspec(N_PAIR), kspec(2 * N_PAIR), full(lambda j: j), full(lambda j: j),
                  pl.BlockSpec((2, 1, S), lambda j, kb: (j, 0, 0)), full(lambda j: 0)] + [ANY] * nc,
        out_specs=[full(lambda j: j), oblk, oblk,
                   pl.BlockSpec((1, S, LANE), lambda j, kb: (j, 0, 0)),
                   pl.BlockSpec((1, t, LANE), lambda j, kb: (j, kb, 0))] + [ANY] * nc,
        out_shape=[jax.ShapeDtypeStruct((S, N_PAIR * LANE), BF16)] * 3 + [jax.ShapeDtypeStruct((N_PAIR, S, LANE), F32)] * 2
        + (_a2a_out_shapes(*comm) if comm else []),
        scratch_shapes=[pltpu.VMEM((2, S, LANE), BF16), pltpu.VMEM((2, 1, S), F32), pltpu.VMEM((2, LANE, S), F32),
                        pltpu.VMEM((2, t, LANE), F32), pltpu.VMEM((2, t, LANE), F32)]
        + (_a2a_sems(nc) if comm else []),
        compiler_params=_cparams("arbitrary" if comm else "parallel", "arbitrary"),
    )(qkv, qkv, qkv, o, do, lrow, ccol, *(comm[1] if comm else []))
    return tuple(res[:5]) + (list(res[5:]),)


DT_LANE0 = ATTN_HEADS
SSD_PAIRS = SSD_HEADS // 2
SSD_X = SSD_HEADS * SSD_HEAD_DIM
SSD_B0 = SSD_X
SSD_C0 = SSD_X + SSD_GROUPS * SSD_STATE
SSD_CH = SSD_X + 2 * SSD_GROUPS * SSD_STATE
TN_DIMS = (((0,), (0,)), ((), ()))


def _softplus(x):
    return jnp.maximum(x, 0.0) + jnp.log(1.0 + jnp.exp(-jnp.abs(x)))


def _ssd_prep(fdt, dtb, alog):
    L = fdt.shape[0]
    lane = lax.broadcasted_iota(jnp.int32, (1, LANE), 1)
    hl = jnp.logical_and(lane >= DT_LANE0, lane < DT_LANE0 + SSD_HEADS)
    dtv = jnp.where(hl, _softplus(fdt + dtb), 0.0)
    A = jnp.where(hl, -jnp.exp(alog), 0.0)
    r = lax.broadcasted_iota(jnp.int32, (L, L), 0)
    c = lax.broadcasted_iota(jnp.int32, (L, L), 1)
    cs = _tri_dot(jnp.where(c <= r, 1.0, 0.0).astype(BF16), dtv * A)
    return lane, hl, dtv, A, cs, r, c


def _halves(lane, v0, v1):
    return jnp.where(lane < SSD_HEAD_DIM, v0, v1)


def _ssd_fwd(xc, rest, col, dtb, alog, dskip, *, name):
    S = xc.shape[0]
    L = SSD_CHUNK
    nc = S // L

    def body(xc_ref, f_ref, dtb_ref, al_ref, dk_ref, y_ref, hs_ref, h_ref, cst_ref):
        i = pl.program_id(0)

        @pl.when(i == 0)
        def _():
            h_ref[...] = jnp.zeros_like(h_ref)

        lane, hl, dtv, A, cs, r, c = _ssd_prep(f_ref[...], dtb_ref[...], al_ref[...])
        cst_ref[...] = cs.T
        cs_last = cs[L - 1:L, :]
        ecs = jnp.exp(cs)
        dec = jnp.exp(cs_last - cs)
        cd = jnp.exp(cs_last)
        dkv = dk_ref[...]
        prow = lax.broadcasted_iota(jnp.int32, (LANE, 1), 0)
        for g in range(SSD_GROUPS):
            Bg = xc_ref[:, SSD_B0 + g * SSD_STATE:SSD_B0 + (g + 1) * SSD_STATE].astype(BF16)
            Cg = xc_ref[:, SSD_C0 + g * SSD_STATE:SSD_C0 + (g + 1) * SSD_STATE].astype(BF16)
            CB = lax.dot_general(Cg, Bg, NT_DIMS, preferred_element_type=F32)
            for pp in range(SSD_PAIRS // SSD_GROUPS):
                pi = g * (SSD_PAIRS // SSD_GROUPS) + pp
                hl0 = DT_LANE0 + 2 * pi
                x2 = xc_ref[:, pi * LANE:(pi + 1) * LANE]
                xd = x2 * _halves(lane, _col(dtv, hl0, lane), _col(dtv, hl0 + 1, lane))
                xdb = xd.astype(BF16)
                yd = jnp.zeros((L, LANE), F32)
                for hh in range(2):
                    seg = _col(cs, hl0 + hh, lane) - cst_ref[hl0 + hh:hl0 + hh + 1, :]
                    M = CB * jnp.exp(jnp.where(c <= r, seg, NEG))
                    yh = jnp.dot(M.astype(BF16), xdb, preferred_element_type=F32)
                    yd = jnp.where((lane >= SSD_HEAD_DIM) if hh else (lane < SSD_HEAD_DIM), yh, yd)
                hp = h_ref[pi]
                hs_ref[0, pi] = hp
                yo = lax.dot_general(Cg, hp.astype(BF16), NT_DIMS, preferred_element_type=F32)
                yo = yo * _halves(lane, _col(ecs, hl0, lane), _col(ecs, hl0 + 1, lane))
                dsk = _halves(lane, _col(dkv, hl0, lane), _col(dkv, hl0 + 1, lane))
                y_ref[:, pi * LANE:(pi + 1) * LANE] = yd + yo + dsk * x2
                xw = (xd * _halves(lane, _col(dec, hl0, lane), _col(dec, hl0 + 1, lane))).astype(BF16)
                st = lax.dot_general(xw, Bg, TN_DIMS, preferred_element_type=F32)
                cdp = jnp.where(prow < SSD_HEAD_DIM, _col(cd, hl0, lane), _col(cd, hl0 + 1, lane))
                h_ref[pi] = cdp * hp + st

    vec = pl.BlockSpec((1, LANE), lambda i: (0, 0))
    return pl.pallas_call(
        body, name=name, grid=(nc,),
        in_specs=[pl.BlockSpec((L, SSD_CH), lambda i: (i, 0)), pl.BlockSpec((L, LANE), lambda i: (i, col)),
                  vec, vec, vec],
        out_specs=[pl.BlockSpec((L, SSD_X), lambda i: (i, 0)),
                   pl.BlockSpec((1, SSD_PAIRS, LANE, SSD_STATE), lambda i: (i, 0, 0, 0))],
        out_shape=[jax.ShapeDtypeStruct((S, SSD_X), F32),
                   jax.ShapeDtypeStruct((nc, SSD_PAIRS, LANE, SSD_STATE), F32)],
        scratch_shapes=[pltpu.VMEM((SSD_PAIRS, LANE, SSD_STATE), F32), pltpu.VMEM((LANE, L), F32)],
        compiler_params=_cparams("arbitrary"),
    )(xc, rest, dtb, alog, dskip)


def _pair_sums(lane, v):
    lo = jnp.sum(jnp.where(lane < SSD_HEAD_DIM, v, 0.0), axis=1, keepdims=True)
    return lo, jnp.sum(v, axis=1, keepdims=True) - lo


def _ssd_bwd(xc, rest, col, dtb, alog, dskip, hs, dy, *, name):
    S = xc.shape[0]
    L = SSD_CHUNK
    nc = S // L
    PG = SSD_PAIRS // SSD_GROUPS

    def body(xc_ref, f_ref, dtb_ref, al_ref, dk_ref, hs_ref, dy_ref, dxc_ref, ddt_ref, dp_ref, dh_ref, cst_ref):
        i = pl.program_id(0)

        @pl.when(i == 0)
        def _():
            dh_ref[...] = jnp.zeros_like(dh_ref)
            dp_ref[...] = jnp.zeros_like(dp_ref)

        fv = f_ref[...] + dtb_ref[...]
        lane, hl, dtv, A, cs, r, c = _ssd_prep(f_ref[...], dtb_ref[...], al_ref[...])
        cst_ref[...] = cs.T
        cs_last = cs[L - 1:L, :]
        ecs = jnp.exp(cs)
        dec = jnp.exp(cs_last - cs)
        cd = jnp.exp(cs_last)
        dkv = dk_ref[...]
        prow = lax.broadcasted_iota(jnp.int32, (LANE, 1), 0)
        lrow = lax.broadcasted_iota(jnp.int32, (L, 1), 0)
        is_last = lrow == L - 1
        causal = c <= r
        dcs = jnp.zeros((L, LANE), F32)
        ddt = jnp.zeros((L, LANE), F32)
        dD = jnp.zeros((1, LANE), F32)
        for g in range(SSD_GROUPS):
            Bg = xc_ref[:, SSD_B0 + g * SSD_STATE:SSD_B0 + (g + 1) * SSD_STATE].astype(BF16)
            Cg = xc_ref[:, SSD_C0 + g * SSD_STATE:SSD_C0 + (g + 1) * SSD_STATE].astype(BF16)
            CB = lax.dot_general(Cg, Bg, NT_DIMS, preferred_element_type=F32)
            dCB = jnp.zeros((L, L), F32)
            dB = jnp.zeros((L, SSD_STATE), F32)
            dC = jnp.zeros((L, SSD_STATE), F32)
            for pp in range(PG):
                pi = g * PG + pp
                hl0 = DT_LANE0 + 2 * pi
                x2 = xc_ref[:, pi * LANE:(pi + 1) * LANE]
                dy2 = dy_ref[:, pi * LANE:(pi + 1) * LANE]
                dtp = _halves(lane, _col(dtv, hl0, lane), _col(dtv, hl0 + 1, lane))
                xd = x2 * dtp
                xdb = xd.astype(BF16)
                dsk = _halves(lane, _col(dkv, hl0, lane), _col(dkv, hl0 + 1, lane))
                dx2 = dsk * dy2
                sD = _pair_sums(lane, dy2 * x2)
                hp = hs_ref[0, pi]
                hpb = hp.astype(BF16)
                ecsp = _halves(lane, _col(ecs, hl0, lane), _col(ecs, hl0 + 1, lane))
                yo = lax.dot_general(Cg, hpb, NT_DIMS, preferred_element_type=F32) * ecsp
                dW = (ecsp * dy2).astype(BF16)
                dC = dC + jnp.dot(dW, hpb, preferred_element_type=F32)
                dhp = lax.dot_general(dW, Cg, TN_DIMS, preferred_element_type=F32)
                sYo = _pair_sums(lane, dy2 * yo)
                dhn = dh_ref[pi]
                cdh = (_col(cd, hl0, lane), _col(cd, hl0 + 1, lane))
                dhp = dhp + jnp.where(prow < SSD_HEAD_DIM, cdh[0], cdh[1]) * dhn
                rs = jnp.sum(dhn * hp, axis=1, keepdims=True)
                lo = jnp.sum(jnp.where(prow < SSD_HEAD_DIM, rs, 0.0), axis=0, keepdims=True)
                dcd = (lo, jnp.sum(rs, axis=0, keepdims=True) - lo)
                dhnb = dhn.astype(BF16)
                decp = _halves(lane, _col(dec, hl0, lane), _col(dec, hl0 + 1, lane))
                G = lax.dot_general(Bg, dhnb, NT_DIMS, preferred_element_type=F32)
                dxd = decp * G
                sdec = _pair_sums(lane, xd * G)
                dB = dB + jnp.dot((xd * decp).astype(BF16), dhnb, preferred_element_type=F32)
                dh_ref[pi] = dhp
                for hh in range(2):
                    hmask = (lane >= SSD_HEAD_DIM) if hh else (lane < SSD_HEAD_DIM)
                    seg = _col(cs, hl0 + hh, lane) - cst_ref[hl0 + hh:hl0 + hh + 1, :]
                    Lm = jnp.exp(jnp.where(causal, seg, NEG))
                    M = CB * Lm
                    dyh = jnp.where(hmask, dy2, 0.0).astype(BF16)
                    dM = lax.dot_general(dyh, xdb, NT_DIMS, preferred_element_type=F32)
                    dxd = dxd + lax.dot_general(M.astype(BF16), dyh, TN_DIMS, preferred_element_type=F32)
                    Q = dM * M
                    dCB = dCB + dM * Lm
                    dech = _col(dec, hl0 + hh, lane)
                    dd = sdec[hh] * dech
                    end = dcd[hh] * cdh[hh] + jnp.sum(dd, axis=0, keepdims=True)
                    dcs_h = (sYo[hh] - dd + jnp.sum(Q, axis=1, keepdims=True)
                             - jnp.sum(Q.T, axis=1, keepdims=True) + jnp.where(is_last, end, 0.0))
                    dcs = jnp.where(lane == hl0 + hh, dcs_h, dcs)
                    dD = jnp.where(lane == hl0 + hh, jnp.sum(sD[hh], axis=0, keepdims=True), dD)
                sdt = _pair_sums(lane, dxd * x2)
                ddt = jnp.where(lane == hl0, sdt[0], jnp.where(lane == hl0 + 1, sdt[1], ddt))
                dxc_ref[:, pi * LANE:(pi + 1) * LANE] = dx2 + dxd * dtp
            dCBb = dCB.astype(BF16)
            dC = dC + jnp.dot(dCBb, Bg, preferred_element_type=F32)
            dB = dB + lax.dot_general(dCBb, Cg, TN_DIMS, preferred_element_type=F32)
            dxc_ref[:, SSD_B0 + g * SSD_STATE:SSD_B0 + (g + 1) * SSD_STATE] = dB
            dxc_ref[:, SSD_C0 + g * SSD_STATE:SSD_C0 + (g + 1) * SSD_STATE] = dC
        da = _tri_dot(jnp.where(c >= r, 1.0, 0.0).astype(BF16), dcs)
        ddtv = ddt + da * A
        ddt_ref[...] = jnp.where(hl, ddtv * _sigmoid(fv), 0.0)
        dal = jnp.sum(da * dtv, axis=0, keepdims=True) * A
        dp_ref[0:1, :] += dal
        dp_ref[1:2, :] += dD

    rev = lambda i: nc - 1 - i
    vec = pl.BlockSpec((1, LANE), lambda i: (0, 0))
    return pl.pallas_call(
        body, name=name, grid=(nc,),
        in_specs=[pl.BlockSpec((L, SSD_CH), lambda i: (rev(i), 0)), pl.BlockSpec((L, LANE), lambda i: (rev(i), col)),
                  vec, vec, vec,
                  pl.BlockSpec((1, SSD_PAIRS, LANE, SSD_STATE), lambda i: (rev(i), 0, 0, 0)),
                  pl.BlockSpec((L, SSD_X), lambda i: (rev(i), 0))],
        out_specs=[pl.BlockSpec((L, SSD_CH), lambda i: (rev(i), 0)), pl.BlockSpec((L, LANE), lambda i: (rev(i), 0)),
                   pl.BlockSpec((SUBLANE, LANE), lambda i: (0, 0))],
        out_shape=[jax.ShapeDtypeStruct((S, SSD_CH), F32), jax.ShapeDtypeStruct((S, LANE), F32),
                   jax.ShapeDtypeStruct((SUBLANE, LANE), F32)],
        scratch_shapes=[pltpu.VMEM((SSD_PAIRS, LANE, SSD_STATE), F32), pltpu.VMEM((LANE, L), F32)],
        compiler_params=_cparams("arbitrary"),
    )(xc, rest, dtb, alog, dskip, hs, dy)


def _gate_norm_fwd(y, rest, zcol, nw, *, name):
    S, W = y.shape
    ts = _row_tile(S)
    GW = W // SSD_GROUPS

    def body(y_ref, z_ref, nw_ref, o_ref):
        z = z_ref[...]
        t = y_ref[...] * (z * _sigmoid(z))
        for g in range(SSD_GROUPS):
            cols = slice(g * GW, (g + 1) * GW)
            tg = t[:, cols]
            rr = lax.rsqrt(jnp.mean(tg * tg, axis=-1, keepdims=True) + NORM_EPS)
            o_ref[:, cols] = (tg * rr * nw_ref[:, cols]).astype(BF16)

    row = pl.BlockSpec((ts, W), lambda i: (i, 0))
    return pl.pallas_call(
        body, name=name, grid=(S // ts,),
        in_specs=[row, pl.BlockSpec((ts, W), lambda i: (i, zcol)), pl.BlockSpec((1, W), lambda i: (0, 0))],
        out_specs=row,
        out_shape=jax.ShapeDtypeStruct((S, W), BF16),
        compiler_params=_cparams("parallel"),
    )(y, rest, nw)


def _gate_norm_bwd(y, rest, zcol, nw, do, *, name):
    S, W = y.shape
    ts = _row_tile(S)
    GW = W // SSD_GROUPS

    def body(y_ref, z_ref, nw_ref, do_ref, dy_ref, dz_ref, dnw_ref):
        z = z_ref[...]
        yv = y_ref[...]
        s, ds = _dsilu(z)
        sz = z * s
        t = yv * sz
        dov = do_ref[...]
        parts = []
        for g in range(SSD_GROUPS):
            cols = slice(g * GW, (g + 1) * GW)
            tg = t[:, cols]
            rr = lax.rsqrt(jnp.mean(tg * tg, axis=-1, keepdims=True) + NORM_EPS)
            n = tg * rr
            dog = dov[:, cols]
            dn = dog * nw_ref[:, cols]
            dt = rr * (dn - n * jnp.mean(dn * n, axis=-1, keepdims=True))
            dy_ref[:, cols] = dt * sz[:, cols]
            dz_ref[:, cols] = (dt * yv[:, cols] * ds[:, cols]).astype(BF16)
            parts.append(jnp.sum(dog * n, axis=0, keepdims=True))
        _acc_out(dnw_ref, jnp.concatenate(parts, axis=1), pl.program_id(0) == 0)

    row = pl.BlockSpec((ts, W), lambda i: (i, 0))
    vec = pl.BlockSpec((1, W), lambda i: (0, 0))
    return pl.pallas_call(
        body, name=name, grid=(S // ts,),
        in_specs=[row, pl.BlockSpec((ts, W), lambda i: (i, zcol)), vec, row],
        out_specs=[row, row, vec],
        out_shape=[jax.ShapeDtypeStruct((S, W), F32), jax.ShapeDtypeStruct((S, W), BF16),
                   jax.ShapeDtypeStruct((1, W), F32)],
        compiler_params=_cparams("arbitrary"),
    )(y, rest, nw, do)


N_DEV = 8
MESH = pl.DeviceIdType.MESH
ANY = pl.BlockSpec(memory_space=pl.ANY)


def _all_gather(xs, *, name):
    n = len(xs)

    def body(*refs):
        x_refs, o_refs = refs[:n], refs[n:2 * n]
        send_sems, recv_sems, local_sems = refs[2 * n:]
        px, py, pc = lax.axis_index("x"), lax.axis_index("y"), lax.axis_index("c")
        me, sibling = (px, py, pc), (px, py, 1 - pc)
        chips = [(1 - px, py), (px, 1 - py), (1 - px, 1 - py)]

        def copy(a, k, block, to, src=None):
            slot = o_refs[a].at[4 * block[0] + 2 * block[1] + block[2]]
            return pltpu.make_async_remote_copy(
                src_ref=slot if src is None else src, dst_ref=slot,
                send_sem=send_sems.at[a, k], recv_sem=recv_sems.at[a, k], device_id=to, device_id_type=MESH)

        mine = [pltpu.make_async_copy(x_refs[a], o_refs[a].at[4 * px + 2 * py + pc], local_sems.at[a]) for a in range(n)]
        for cp in mine:
            cp.start()
        first = []
        for a in range(n):
            first.append(copy(a, 0, me, sibling, src=x_refs[a]))
            first += [copy(a, 1 + j, me, (*chip, pc), src=x_refs[a]) for j, chip in enumerate(chips)]
        for cp in first:
            cp.start()
        passed = []
        for j, chip in enumerate(chips):
            for a in range(n):
                copy(a, 1 + j, (*chip, pc), me).wait_recv()
                fwd = copy(a, 4 + j, (*chip, pc), sibling)
                fwd.start()
                passed.append(fwd)
        for a in range(n):
            copy(a, 0, sibling, me).wait_recv()
            for j, chip in enumerate(chips):
                copy(a, 4 + j, (*chip, 1 - pc), me).wait_recv()
        for cp in first + passed:
            cp.wait_send()
        for cp in mine:
            cp.wait()

    return pl.pallas_call(
        body, name=name,
        out_shape=[jax.ShapeDtypeStruct((N_DEV,) + x.shape, x.dtype) for x in xs],
        in_specs=[ANY] * n, out_specs=[ANY] * n,
        scratch_shapes=[pltpu.SemaphoreType.DMA((n, N_DEV - 1)), pltpu.SemaphoreType.DMA((n, N_DEV - 1)),
                        pltpu.SemaphoreType.DMA((n,))],
    )(*xs)


def _exchange(sends, *, name):
    n = len(sends)

    def body(*refs):
        _a2a_start("exchange", refs[:n], refs[n:2 * n], refs[2 * n:])
        _a2a_wait("exchange", refs[:n], refs[n:2 * n], refs[2 * n:])

    return pl.pallas_call(
        body, name=name,
        out_shape=_a2a_out_shapes("exchange", sends), in_specs=[ANY] * n, out_specs=[ANY] * n,
        scratch_shapes=_a2a_sems(n),
    )(*sends)


def _a2a_out_shapes(kind, arrays):
    if kind == "gather":
        return [jax.ShapeDtypeStruct((N_DEV,) + x.shape, x.dtype) for x in arrays]
    return [jax.ShapeDtypeStruct(x.shape, x.dtype) for x in arrays]


def _a2a_sems(n):
    return [pltpu.SemaphoreType.DMA((n, N_DEV - 1)), pltpu.SemaphoreType.DMA((n, N_DEV - 1)),
            pltpu.SemaphoreType.DMA((n,))]


def _a2a_copies(kind, in_refs, out_refs, send_sems, recv_sems, local_sems, arrivals):
    px, py, pc = lax.axis_index("x"), lax.axis_index("y"), lax.axis_index("c")
    me = 4 * px + 2 * py + pc
    n = len(in_refs)
    src = (lambda a, p: in_refs[a].at[p]) if kind == "exchange" else (lambda a, p: in_refs[a])
    local = [pltpu.make_async_copy(src(a, me), out_refs[a].at[me], local_sems.at[a]) for a in range(n)]
    sent, landing = [], []
    for k in range(1, N_DEV):
        qx = 1 - px if k & 4 else px
        qy = 1 - py if k & 2 else py
        qc = 1 - pc if k & 1 else pc
        peer = 4 * qx + 2 * qy + qc
        for a in range(n):
            def rdma(dst_slot):
                return pltpu.make_async_remote_copy(
                    src_ref=src(a, peer), dst_ref=out_refs[a].at[dst_slot],
                    send_sem=send_sems.at[a, k - 1], recv_sem=recv_sems.at[a, k - 1],
                    device_id=(qx, qy, qc), device_id_type=MESH)
            sent.append(rdma(me))
            if arrivals:
                landing.append(rdma(peer))
    return local, sent, landing


def _a2a_start(kind, in_refs, out_refs, sems):
    local, sent, _ = _a2a_copies(kind, in_refs, out_refs, *sems, arrivals=False)
    for cp in local + sent:
        cp.start()


def _a2a_wait(kind, in_refs, out_refs, sems):
    local, sent, landing = _a2a_copies(kind, in_refs, out_refs, *sems, arrivals=True)
    for cp in landing:
        cp.wait_recv()
    for cp in sent:
        cp.wait_send()
    for cp in local:
        cp.wait()


IN_POOL, IN_QKV, IN_F, IN_Z, IN_XBC, IN_DT, IN_GATE, IN_TOTAL = 0, 512, 2048, 2056, 3080, 4616, 4632, 7704
QKV_W = IN_F - IN_QKV
REST_Z_BLK, REST_POOL_BLK, REST_XBC0, REST_FDT_BLK = 3, 8, 4608, 48
REST_FDT0 = REST_FDT_BLK * LANE
REST_USED = REST_FDT0 + (IN_Z - IN_F) + (IN_GATE - IN_DT)
REST_W = REST_FDT0 + LANE


def _w_in_runs(shard):
    segs = [(0, IN_QKV, IN_F, 0), (1, IN_GATE, IN_TOTAL, 0), (1, IN_Z, IN_XBC, 3072), (1, IN_POOL, IN_QKV, 4096),
            (1, IN_XBC, IN_DT, REST_XBC0), (1, IN_F, IN_Z, REST_FDT0), (1, IN_DT, IN_GATE, REST_FDT0 + IN_Z - IN_F)]
    runs = []
    for dst, t0, t1, d0 in segs:
        for j in range(N_DEV):
            lo, hi = max(t0, shard * j), min(t1, shard * (j + 1))
            if lo < hi:
                runs.append((dst, d0 + lo - t0, j, lo - shard * j, hi - lo))
    return runs


def _repack_w_in(g, shard, *, name):
    _, L, K, P = g.shape
    tr = 256
    runs = _w_in_runs(shard)

    def body(g_ref, q_ref, r_ref):
        r_ref[:, REST_USED:] = jnp.zeros((tr, REST_W - REST_USED), g_ref.dtype)
        for dst, b, j, a, n in runs:
            (q_ref, r_ref)[dst][:, b:b + n] = g_ref[j, :, a:a + n]

    return pl.pallas_call(
        body, name=name, grid=(L, K // tr),
        in_specs=[pl.BlockSpec((N_DEV, None, tr, P), lambda l, i: (0, l, i, 0))],
        out_specs=[pl.BlockSpec((None, tr, QKV_W), lambda l, i: (l, i, 0)),
                   pl.BlockSpec((None, tr, REST_W), lambda l, i: (l, i, 0))],
        out_shape=[jax.ShapeDtypeStruct((L, K, QKV_W), g.dtype), jax.ShapeDtypeStruct((L, K, REST_W), g.dtype)],
        compiler_params=_cparams("parallel", "parallel"),
    )(g)


def _held_index(l, li, i, n):
    return jnp.where(l == li, i, jnp.where(l > li, n - 1, 0))


def _repack_dw_in(dwq, dwr, shard, *, name):
    L = len(dwq)
    K = dwq[0].shape[0]
    tr = 128
    nt = K // tr
    runs = _w_in_runs(shard)

    def body(*refs):
        srcs, o_ref = refs[:2 * L], refs[2 * L]
        l = pl.program_id(0)
        for li in range(L):
            @pl.when(l == li)
            def _():
                for dst, b, j, a, n in runs:
                    o_ref[j, :, a:a + n] = srcs[2 * li + dst][:, b:b + n]

    in_specs = []
    for li in range(L):
        hold = functools.partial(lambda li, l, i: (_held_index(l, li, i, nt), 0), li)
        in_specs += [pl.BlockSpec((tr, QKV_W), hold), pl.BlockSpec((tr, REST_W), hold)]
    args = [x for pair in zip(dwq, dwr) for x in pair]
    return pl.pallas_call(
        body, name=name, grid=(L, nt),
        in_specs=in_specs,
        out_specs=pl.BlockSpec((N_DEV, None, tr, shard), lambda l, i: (0, l, i, 0)),
        out_shape=jax.ShapeDtypeStruct((N_DEV, L, K, shard), dwq[0].dtype),
        compiler_params=_cparams("arbitrary", "arbitrary"),
    )(*args)


FLAT_W = 1024
ADAM_ROWS = 128


def _adam_math(w, g, m, v):
    m2 = ADAM_B1 * m + (1.0 - ADAM_B1) * g
    v2 = ADAM_B2 * v + (1.0 - ADAM_B2) * (g * g)
    m_hat = m2 / (1.0 - ADAM_B1 ** ADAM_STEP)
    v_hat = v2 / (1.0 - ADAM_B2 ** ADAM_STEP)
    delta = -ADAM_LR * (m_hat / (jnp.sqrt(v_hat) + ADAM_EPS) + ADAM_WD * w)
    return delta, m2, v2


def _row_div(R, align):
    best = R
    for t in range(align, min(R, ADAM_ROWS) + 1, align):
        if R % t == 0:
            best = t
    return best


def _sum8(ref_of):
    g = ref_of(0).astype(F32)
    for s in range(1, N_DEV):
        g = g + ref_of(s).astype(F32)
    return g


def _sum_adam(recvs, w, m, v, *, name):
    L, R, C = w.shape
    single = not isinstance(recvs, (list, tuple))
    recvs = [recvs] if single else list(recvs)
    tr = _row_div(R, SUBLANE * (4 // recvs[0].dtype.itemsize))
    nt = R // tr
    nr = len(recvs)

    def body(*refs):
        r_refs = refs[:nr]
        w_ref, m_ref, v_ref, g_ref, d_ref, m2_ref, v2_ref = refs[nr:]

        def run(ref_of):
            g = _sum8(ref_of)
            delta, m2, v2 = _adam_math(w_ref[...], g, m_ref[...], v_ref[...])
            g_ref[...] = g
            d_ref[...] = delta
            m2_ref[...] = m2
            v2_ref[...] = v2

        if single:
            run(lambda s: r_refs[0][s])
        else:
            l = pl.program_id(0)
            for li in range(L):
                @pl.when(l == li)
                def _():
                    run(lambda s: r_refs[li][s])

    if single:
        r_specs = [pl.BlockSpec((N_DEV, None, tr, C), lambda l, i: (0, l, i, 0))]
    else:
        r_specs = [pl.BlockSpec((N_DEV, tr, C), functools.partial(lambda li, l, i: (0, _held_index(l, li, i, nt), 0), li))
                   for li in range(L)]
    row = pl.BlockSpec((None, tr, C), lambda l, i: (l, i, 0))
    return pl.pallas_call(
        body, name=name, grid=(L, nt),
        in_specs=r_specs + [row, row, row],
        out_specs=[row] * 4,
        out_shape=[jax.ShapeDtypeStruct((L, R, C), F32)] * 4,
        compiler_params=_cparams("arbitrary", "arbitrary"),
    )(*recvs, w, m, v)


def _sum_parts(parts, *, name):
    _, R, C = parts.shape

    def body(p_ref, o_ref):
        o_ref[...] = _sum8(lambda s: p_ref[s])

    return pl.pallas_call(
        body, name=name, out_shape=jax.ShapeDtypeStruct((R, C), F32),
        in_specs=[pl.BlockSpec(memory_space=pltpu.VMEM)], out_specs=pl.BlockSpec(memory_space=pltpu.VMEM),
    )(parts)


def _adam(g, w, m, v, *, name):
    def body(g_ref, w_ref, m_ref, v_ref, d_ref, m2_ref, v2_ref):
        delta, m2, v2 = _adam_math(w_ref[...], g_ref[...], m_ref[...], v_ref[...])
        d_ref[...] = delta
        m2_ref[...] = m2
        v2_ref[...] = v2

    whole = pl.BlockSpec(memory_space=pltpu.VMEM)
    return pl.pallas_call(
        body, name=name, in_specs=[whole] * 4, out_specs=[whole] * 3,
        out_shape=[jax.ShapeDtypeStruct(g.shape, F32)] * 3,
    )(g, w, m, v)


SHARDED = ("w_in", "ffn_up", "p_pool", "p_attn", "p_ssd", "w_out", "ffn_down", "ssd_conv_w", "ffn_conv_w")
REPLICATED = ("norm_mix", "pool_mix", "pool_scale", "f_bias", "ssd_conv_b", "ssd_dt_bias", "ssd_a_log", "ssd_d",
              "ssd_norm", "norm_ffn", "ffn_conv_b", "norm_final")
WEIGHTS = ("norm_mix", "w_in", "pool_mix", "pool_scale", "f_bias", "ssd_conv_w", "ssd_conv_b", "ssd_dt_bias",
           "ssd_a_log", "ssd_d", "ssd_norm", "p_pool", "p_attn", "p_ssd", "w_out", "norm_ffn", "ffn_up",
           "ffn_conv_w", "ffn_conv_b", "ffn_down", "norm_final")


def _round_up(n, k):
    return -(-n // k) * k


def _pad_last(x, width):
    return jnp.pad(x, [(0, 0)] * (x.ndim - 1) + [(0, width - x.shape[-1])])


def _flat_rows(parts, rows):
    flat = jnp.concatenate([p.reshape(-1) for p in parts])
    return jnp.pad(flat, (0, rows * FLAT_W - flat.shape[0])).reshape(rows, FLAT_W)


def _lanes(v, lane0):
    return jnp.pad(v, (lane0, LANE - lane0 - v.shape[0]))[None]


def _cols_by_device(g):
    _, L, R, c = g.shape
    return jnp.moveaxis(g, 0, 2).reshape(L, R, N_DEV * c)


def _rows_by_device(g):
    _, L, r, C = g.shape
    return jnp.moveaxis(g, 0, 1).reshape(L, N_DEV * r, C)


def _gather_payload(W, l, shard_pad, up_pad):
    bf = lambda n: W[n][l].astype(BF16)
    return [_pad_last(bf("w_in"), shard_pad), _pad_last(bf("ffn_up"), up_pad), bf("p_pool"), bf("p_attn"),
            bf("p_ssd"), bf("w_out"), bf("ffn_down"), W["ssd_conv_w"][l], W["ffn_conv_w"][l]]


def _proj_params(g_win, W, l, shard):
    w_qkv, w_rest = _repack_w_in(g_win[:, None], shard, name=f"l{l}_repack_w_in")
    return dict(nm=W["norm_mix"][l][None], w_qkv=w_qkv[0], w_rest=w_rest[0], fb=_lanes(W["f_bias"][l], 0), shard=shard)


def _mixer_params(gathered, W, l, up_w, up_pad):
    g_up, g_pp, g_pa, g_ps, g_wo, g_dn, g_cw, g_fcw = [g[:, None] for g in gathered]
    half = N_DEV // 2
    w_down = _rows_by_device(g_dn)[0]
    w_down_p = jnp.pad(w_down.reshape(half, up_w, -1), ((0, 0), (0, up_pad - up_w), (0, 0))).reshape(half * up_pad, -1)
    fcb_p = _pad_last(W["ffn_conv_b"][l].reshape(N_DEV, up_w), up_pad).reshape(1, N_DEV * up_pad)
    return dict(
        mix=W["pool_mix"][l].astype(BF16), pscale=W["pool_scale"][l][None], cw=_cols_by_device(g_cw)[0],
        cb=W["ssd_conv_b"][l][None], dtb=_lanes(W["ssd_dt_bias"][l], DT_LANE0),
        alog=_lanes(W["ssd_a_log"][l], DT_LANE0), dsk=_lanes(W["ssd_d"][l], DT_LANE0), snw=W["ssd_norm"][l][None],
        p_pool=_cols_by_device(g_pp)[0], p_attn=_cols_by_device(g_pa)[0], p_ssd=_rows_by_device(g_ps)[0],
        w_out=_rows_by_device(g_wo)[0], nf=W["norm_ffn"][l][None], g_up=g_up, up_blocks=(N_DEV, up_w, up_pad),
        fcw=_cols_by_device(_pad_last(g_fcw, up_pad))[0], fcb=fcb_p, w_down=w_down_p)


def _layer_fwd(x, W, l, g_win, g_mixer, to_gather, dims):
    nm = lambda s: f"l{l}_{s}"
    P = _proj_params(g_win, W, l, dims[0])
    u = _rms_fwd(x, P["nm"], name=nm("rms_mix"))
    qkv = _mm(u, P["w_qkv"], out_dtype=BF16, name=nm("proj_qkv"))
    rest = _mm(u, P["w_rest"], name=nm("proj_rest"))
    ccol = _fox_pre(rest, REST_FDT_BLK, P["fb"], name=nm("fox_pre"))
    o, lrow, got = _attn_fwd(qkv, ccol, comm=("gather", to_gather) if to_gather else None, name=nm("attn"))
    if g_mixer is None:
        g_mixer, got = got[:len(SHARDED) - 1], got[len(SHARDED) - 1:]
    P.update(_mixer_params(g_mixer, W, l, *dims[1:]))
    d, ya0 = _pool_fwd(rest, REST_POOL_BLK, P["mix"], P["pscale"], name=nm("pool"))
    ya = _mm(ya0, P["p_pool"], name=nm("p_pool"))
    yb = _mm(o, P["p_attn"], name=nm("p_attn"))
    xc = _conv_silu_fwd(rest, REST_XBC0, SSD_CH, P["cw"], P["cb"], name=nm("ssd_conv"))
    y, hs = _ssd_fwd(xc, rest, REST_FDT_BLK, P["dtb"], P["alog"], P["dsk"], name=nm("ssd"))
    yc0 = _gate_norm_fwd(y, rest, REST_Z_BLK, P["snw"], name=nm("ssd_norm"))
    yc = _mm(yc0, P["p_ssd"], name=nm("p_ssd"))
    merged = _merge_fwd(rest, ya, yb, yc, name=nm("merge"))
    x1 = _mm(merged, P["w_out"], add=x, name=nm("w_out"))
    u2 = _rms_fwd(x1, P["nf"], name=nm("rms_ffn"))
    h0 = _mm_up_fwd(u2, P["g_up"], 0, name=nm("ffn_up"))
    a = _ffn_act_fwd(h0, P["fcw"], P["fcb"], name=nm("ffn_act"))
    x2 = _mm(a, P["w_down"], add=x1, name=nm("ffn_down"))
    saved = dict(x=x, u=u, qkv=qkv, rest=rest, d=d, ya0=ya0, ya=ya, ccol=ccol, o=o, lrow=lrow, yb=yb,
                 xc=xc, y=y, hs=hs, yc0=yc0, yc=yc, merged=merged, x1=x1, u2=u2, h0=h0, a=a)
    return x2, saved, P, got


EARLY_GRADS = ("ffn_up", "ffn_down", "ffn_conv_w", "p_pool", "p_attn", "p_ssd", "w_out")
LATE_GRADS = ("w_in", "ssd_conv_w")


def _layer_bwd(g, gb, sv, P, l):
    nm = lambda s: f"l{l}_{s}_bwd"
    rest = sv["rest"]
    G = {}
    up_blocks, up_w, up_pad = P["up_blocks"]
    half = up_blocks // 2
    da = _mm(gb, P["w_down"], tb=True, name=nm("ffn_down_dx"))
    dwd = _mm(sv["a"], gb, ta=True, out_dtype=GRAD_WIRE, name=nm("ffn_down_dw"))
    G["ffn_down"] = dwd.reshape(half, up_pad, -1)[:, :up_w].reshape(N_DEV, -1, dwd.shape[1])
    dh, dfcw, dfcb = _ffn_act_bwd(sv["h0"], P["fcw"], P["fcb"], da, name=nm("ffn_act"))
    du2 = _mm_up_dx(dh, P["g_up"], 0, name=nm("ffn_up_dx"))
    G["ffn_up"] = _mm_up_dw(sv["u2"], dh, up_blocks, up_w, name=nm("ffn_up_dw"))
    taps = dfcw.shape[1]
    G["ffn_conv_w"] = jnp.moveaxis(dfcw.reshape(2, taps, half, up_pad)[..., :up_w], 2, 1).reshape(N_DEV, taps, up_w)
    G["ffn_conv_b"] = dfcb.reshape(2, half, up_pad)[..., :up_w].reshape(-1)
    dx1, dx1b, dnf = _rms_bwd(sv["x1"], P["nf"], du2, g, name=nm("rms_ffn"))
    G["norm_ffn"] = dnf[0]
    dmerged = _mm(dx1b, P["w_out"], tb=True, name=nm("w_out_dx"))
    dwo = _mm(sv["merged"], dx1b, ta=True, out_dtype=GRAD_WIRE, name=nm("w_out_dw"))
    G["w_out"] = dwo.reshape(N_DEV, -1, dwo.shape[1])
    dgl, dya, dyb, dyc = _merge_bwd(rest, sv["ya"], sv["yb"], sv["yc"], dmerged, name=nm("merge"))
    G["p_pool"] = _mm_tn_blocks(sv["ya0"], dya, N_DEV, name=nm("p_pool_dw"))
    dya0 = _mm(dya, P["p_pool"], tb=True, name=nm("p_pool_dx"))
    G["p_attn"] = _mm_tn_blocks(sv["o"], dyb, N_DEV, name=nm("p_attn_dw"))
    do = _mm(dyb, P["p_attn"], tb=True, out_dtype=BF16, name=nm("p_attn_dx"))
    dps = _mm(sv["yc0"], dyc, ta=True, out_dtype=GRAD_WIRE, name=nm("p_ssd_dw"))
    G["p_ssd"] = dps.reshape(N_DEV, -1, dps.shape[1])
    dyc0 = _mm(dyc, P["p_ssd"], tb=True, name=nm("p_ssd_dx"))
    dpv, dmix, dpsc = _pool_bwd(dya0, sv["d"], P["mix"], P["pscale"], name=nm("pool"))
    G["pool_mix"], G["pool_scale"] = dmix, dpsc[0]
    dq, dk, dv, dcq, dck, got = _attn_bwd(sv["qkv"], sv["o"], do, sv["lrow"], sv["ccol"],
                                          comm=("exchange", [G[n] for n in EARLY_GRADS]), name=nm("attn"))
    received = {(n, l): r for n, r in zip(EARLY_GRADS, got)}
    dy, dz, dsnw = _gate_norm_bwd(sv["y"], rest, REST_Z_BLK, P["snw"], dyc0, name=nm("ssd_norm"))
    G["ssd_norm"] = dsnw[0]
    dxc, ddt, dpar = _ssd_bwd(sv["xc"], rest, REST_FDT_BLK, P["dtb"], P["alog"], P["dsk"], sv["hs"], dy, name=nm("ssd"))
    G["ssd_a_log"] = dpar[0, DT_LANE0:DT_LANE0 + SSD_HEADS]
    G["ssd_d"] = dpar[1, DT_LANE0:DT_LANE0 + SSD_HEADS]
    dxbc, dcw, dcb = _conv_silu_bwd(rest, REST_XBC0, SSD_CH, P["cw"], P["cb"], dxc, name=nm("ssd_conv"))
    G["ssd_conv_w"] = jnp.moveaxis(dcw.reshape(dcw.shape[0], N_DEV, -1), 1, 0)
    G["ssd_conv_b"] = dcb[0]
    dfdt, dfdtb = _fox_post(dcq, dck, rest, REST_FDT_BLK, P["fb"], ddt, name=nm("fox_post"))
    G["f_bias"] = dfdtb[0, :ATTN_HEADS]
    G["ssd_dt_bias"] = dfdtb[0, DT_LANE0:DT_LANE0 + SSD_HEADS]
    dqkv = jnp.concatenate([dq, dk, dv], axis=1)
    drest = jnp.concatenate([dgl, dz, dpv, dxbc, dfdt], axis=1)
    dwq = _mm(sv["u"], dqkv, ta=True, out_dtype=GRAD_WIRE, name=nm("proj_qkv_dw"))
    dwr = _mm(sv["u"], drest, ta=True, out_dtype=GRAD_WIRE, name=nm("proj_rest_dw"))
    G["w_in"] = _repack_dw_in([dwq], [dwr], P["shard"], name=nm("repack_dw_in"))[:, 0]
    du = _mm(dqkv, P["w_qkv"], tb=True, name=nm("proj_qkv_dx"))
    du, got = _mm(drest, P["w_rest"], tb=True, add=du, comm=("exchange", [G[n] for n in LATE_GRADS]),
                  name=nm("proj_rest_dx"))
    received.update({(n, l): r for n, r in zip(LATE_GRADS, got)})
    dx, dxb, dnm = _rms_bwd(sv["x"], P["nm"], du, dx1, name=nm("rms_mix"))
    G["norm_mix"] = dnm[0]
    return dx, dxb, G, received


def kernel(x, norm_mix, w_in, pool_mix, pool_scale, f_bias, ssd_conv_w, ssd_conv_b, ssd_dt_bias, ssd_a_log, ssd_d, ssd_norm, p_pool, p_attn, p_ssd, w_out, norm_ffn, ffn_up, ffn_conv_w, ffn_conv_b, ffn_down, norm_final, loss_target, m_norm_mix, m_w_in, m_pool_mix, m_pool_scale, m_f_bias, m_ssd_conv_w, m_ssd_conv_b, m_ssd_dt_bias, m_ssd_a_log, m_ssd_d, m_ssd_norm, m_p_pool, m_p_attn, m_p_ssd, m_w_out, m_norm_ffn, m_ffn_up, m_ffn_conv_w, m_ffn_conv_b, m_ffn_down, m_norm_final, v_norm_mix, v_w_in, v_pool_mix, v_pool_scale, v_f_bias, v_ssd_conv_w, v_ssd_conv_b, v_ssd_dt_bias, v_ssd_a_log, v_ssd_d, v_ssd_norm, v_p_pool, v_p_attn, v_p_ssd, v_w_out, v_norm_ffn, v_ffn_up, v_ffn_conv_w, v_ffn_conv_b, v_ffn_down, v_norm_final):
    args = dict(locals())
    W = {n: args[n] for n in WEIGHTS}
    M = {n: args["m_" + n] for n in WEIGHTS}
    V = {n: args["v_" + n] for n in WEIGHTS}
    L = norm_mix.shape[0]
    shard = w_in.shape[-1]
    up_w = ffn_up.shape[-1]
    up_pad = _round_up(up_w, LANE)

    payload = lambda l: _gather_payload(W, l, _round_up(shard, LANE), up_pad)
    first = payload(0)
    g_win = _all_gather(first[:1], name="gather_w_in_l0")[0]
    g_mixer, to_gather = None, first[1:]
    h, saves, params = x[0], [], []
    for l in range(L):
        to_gather = to_gather + (payload(l + 1) if l + 1 < L else [])
        h, sv, P, got = _layer_fwd(h, W, l, g_win, g_mixer, to_gather, (shard, up_w, up_pad))
        saves.append(sv)
        params.append(P)
        if l + 1 < L:
            g_win, g_mixer, to_gather = got[0], got[1:], []
    loss_part, g, gb, dnfin = _loss_head(h, norm_final[None], loss_target[0], name="loss_head")

    G, received = [None] * L, {}
    for l in reversed(range(L)):
        g, gb, G[l], got = _layer_bwd(g, gb, saves[l], params[l], l)
        received.update(got)

    rep_grads = {n: jnp.stack([G[l][n] for l in range(L)]) for n in REPLICATED if n != "norm_final"}
    rep_grads["norm_final"] = dnfin[0]
    rep_size = sum(W[n].size for n in REPLICATED)
    rep_rows = _round_up(-(-rep_size // (N_DEV * FLAT_W)), SUBLANE)
    rep_send = _flat_rows([rep_grads[n] for n in REPLICATED], N_DEV * rep_rows).reshape(N_DEV, rep_rows, FLAT_W)
    last = _exchange([rep_send], name="exchange_replicated_grads")

    out = {}
    for n in SHARDED:
        shape = W[n].shape
        as3 = lambda a: a.reshape(L, -1, shape[-1])
        res = _sum_adam([received[(n, l)] for l in range(L)], as3(W[n]), as3(M[n]), as3(V[n]), name="sum_adam_" + n)
        for key, arr in zip(("grad_", "delta_", "new_m_", "new_v_"), res):
            out[key + n] = arr.reshape(shape)

    rep_sum = _sum_parts(last[-1], name="sum_replicated")
    rep_g = _all_gather([rep_sum], name="gather_replicated_grads")[0].reshape(N_DEV * rep_rows, FLAT_W)
    flat_rep = lambda D: _flat_rows([D[n] for n in REPLICATED], N_DEV * rep_rows)
    d_rp, m_rp, v_rp = _adam(rep_g, flat_rep(W), flat_rep(M), flat_rep(V), name="adam_replicated")
    off = 0
    for n in REPLICATED:
        size = W[n].size
        for key, arr in (("grad_", rep_g), ("delta_", d_rp), ("new_m_", m_rp), ("new_v_", v_rp)):
            out[key + n] = arr.reshape(-1)[off:off + size].reshape(W[n].shape)
        off += size

    loss = lax.psum(loss_part[0, 0], ("x", "y", "c"))
    return (loss, g[None], *[out["grad_" + n] for n in WEIGHTS], *[out["delta_" + n] for n in WEIGHTS],
            *[out["new_m_" + n] for n in WEIGHTS], *[out["new_v_" + n] for n in WEIGHTS])
```

```python
import functools

import jax
import jax.numpy as jnp
from jax import lax
from jax.experimental import pallas as pl
from jax.experimental.pallas import tpu as pltpu

F32 = jnp.float32
BF16 = jnp.bfloat16

LANE = 128
SUBLANE = 8
VMEM_LIMIT = 56 * 1024 * 1024

NORM_EPS = 1e-6
POOL_WINDOWS = (2, 4, 8, 16)
POOL_GROUP_DIM = 128
ATTN_HEADS = 8
ATTN_HEAD_DIM = 64
SSD_HEAD_DIM = 64
SSD_HEADS = 16
SSD_GROUPS = 2
SSD_STATE = 128
SSD_CHUNK = 128

ADAM_LR = 0.001
ADAM_B1 = 0.9
ADAM_B2 = 0.999
ADAM_EPS = 1e-08
ADAM_WD = 0.01
ADAM_STEP = 10


def _cparams(*sem):
    return pltpu.CompilerParams(dimension_semantics=tuple(sem), vmem_limit_bytes=VMEM_LIMIT)


def _tile(n, pref):
    if n <= pref:
        return n
    assert n % LANE == 0, n
    q = n // LANE
    best = 1
    for d in range(1, pref // LANE + 1):
        if q % d == 0:
            best = d
    return best * LANE


GRAD_WIRE = BF16


def _mm_call(args, in_specs, out_spec, out_shape, grid, *, ta, tb, nk, acc_shape, has_add, keep, name, comm=None):
    dims = (((0 if ta else 1,), (1 if tb else 0,)), ((), ()))
    nc = len(comm[1]) if comm else 0
    n_in = 2 + has_add

    def body(*refs):
        a_ref, b_ref = refs[0], refs[1]
        o_ref = refs[n_in + nc]
        acc_ref = refs[n_in + 2 * nc + 1] if nk > 1 else None
        carried = (refs[n_in:n_in + nc], refs[n_in + nc + 1:n_in + 2 * nc + 1], refs[n_in + 2 * nc + 1 + (nk > 1):])
        i, j, k = pl.program_id(0), pl.program_id(1), pl.program_id(2)
        if comm:
            _carry_start(comm[0], *carried, jnp.logical_and(jnp.logical_and(i == 0, j == 0), k == 0))
        d = lax.dot_general(a_ref[...], b_ref[...], dims, preferred_element_type=F32)

        def finish(r):
            if has_add:
                r = r + refs[2][...]
            if keep is not None:
                r = r[:, :keep]
            o_ref[...] = r.astype(o_ref.dtype)

        if nk == 1:
            finish(d)
        else:
            @pl.when(k == 0)
            def _():
                acc_ref[...] = d

            @pl.when(jnp.logical_and(k > 0, k < nk - 1))
            def _():
                acc_ref[...] += d

            @pl.when(k == nk - 1)
            def _():
                finish(acc_ref[...] + d)

        if comm:
            _carry_wait(comm[0], *carried, jnp.logical_and(jnp.logical_and(i == grid[0] - 1, j == grid[1] - 1),
                                                           k == nk - 1))

    res = pl.pallas_call(
        body, name=name, grid=grid, in_specs=in_specs + [ANY] * nc, out_specs=[out_spec] + [ANY] * nc,
        out_shape=[out_shape] + (_a2a_out_shapes(*comm) if comm else []),
        scratch_shapes=([pltpu.VMEM(acc_shape, F32)] if nk > 1 else []) + (_a2a_sems(nc) if comm else []),
        compiler_params=_cparams(*(("arbitrary",) * 3 if comm else ("parallel", "parallel", "arbitrary"))),
    )(*args, *(comm[1] if comm else []))
    return (res[0], list(res[1:])) if comm else res[0]


MM_VMEM_BUDGET = 40 * 1024 * 1024
MM_MAX_ROWS = 2048


def _mm_rows(M, tn, tk, nk, out_bytes, has_add):
    best = None
    for tm in range(LANE, min(M, MM_MAX_ROWS) + 1, LANE):
        if M % tm:
            continue
        blocks = 2 * (tm * tk * 2 + tk * tn * 2 + tm * tn * out_bytes + (tm * tn * 4 if has_add else 0))
        blocks += tm * tn * 4 * (2 if nk > 1 else 1)
        if blocks <= MM_VMEM_BUDGET:
            best = tm
    return M if best is None else best


def _mm(a, b, *, ta=False, tb=False, add=None, out_dtype=F32, comm=None, name):
    assert a.dtype == BF16 and b.dtype == BF16, (a.dtype, b.dtype)
    if ta:
        K, M = a.shape
    else:
        M, K = a.shape
    if tb:
        N, K2 = b.shape
    else:
        K2, N = b.shape
    assert K == K2, (a.shape, b.shape, ta, tb)
    tn, tk = _tile(N, 1024), _tile(K, 2048)
    tm = _mm_rows(M, tn, tk, K // tk, jnp.dtype(out_dtype).itemsize, add is not None)
    a_spec =pl.BlockSpec((tk, tm), lambda i, j, k: (k, i)) if ta else pl.BlockSpec((tm, tk), lambda i, j, k: (i, k))
    b_spec = pl.BlockSpec((tn, tk), lambda i, j, k: (j, k)) if tb else pl.BlockSpec((tk, tn), lambda i, j, k: (k, j))
    in_specs, args = [a_spec, b_spec], [a, b]
    if add is not None:
        in_specs.append(pl.BlockSpec((tm, tn), lambda i, j, k: (i, j)))
        args.append(add)
    return _mm_call(args, in_specs, pl.BlockSpec((tm, tn), lambda i, j, k: (i, j)),
                    jax.ShapeDtypeStruct((M, N), out_dtype), (M // tm, N // tn, K // tk),
                    ta=ta, tb=tb, nk=K // tk, acc_shape=(tm, tn), has_add=add is not None, keep=None, name=name,
                    comm=comm)


def _mm_tn_blocks(a, b, nblk, *, name):
    K, M = a.shape
    N = b.shape[1]
    bw = N // nblk
    tk = _tile(K, 2048)
    tm = _mm_rows(M, bw, tk, K // tk, jnp.dtype(GRAD_WIRE).itemsize, False)
    return _mm_call([a, b],
                    [pl.BlockSpec((tk, tm), lambda i, j, k: (k, i)), pl.BlockSpec((tk, bw), lambda i, j, k: (k, j))],
                    pl.BlockSpec((None, tm, bw), lambda i, j, k: (j, i, 0)),
                    jax.ShapeDtypeStruct((nblk, M, bw), GRAD_WIRE), (M // tm, nblk, K // tk),
                    ta=True, tb=False, nk=K // tk, acc_shape=(tm, bw), has_add=False, keep=None, name=name)


def _half_of(b, half):
    hi = jnp.where(b >= half, 1, 0)
    return hi, b - half * hi


def _mm_up_fwd(u, g_up, l, *, name):
    S, K = u.shape
    nblk, _, _, bw = g_up.shape
    tm = _mm_rows(S, bw, K, 1, 4, False)
    return _mm_call([u, g_up],
                    [pl.BlockSpec((tm, K), lambda i, j, k: (i, 0)),
                     pl.BlockSpec((None, None, K, bw), lambda i, j, k: (j, l, 0, 0))],
                    pl.BlockSpec((tm, bw), lambda i, j, k: (i, j)),
                    jax.ShapeDtypeStruct((S, nblk * bw), F32), (S // tm, nblk, 1),
                    ta=False, tb=False, nk=1, acc_shape=(tm, bw), has_add=False, keep=None, name=name)


def _mm_up_dx(dh, g_up, l, *, name):
    _, S, _ = dh.shape
    nblk, _, N, bw = g_up.shape
    half = nblk // 2
    tm = _mm_rows(S, N, bw, nblk, 4, False)
    return _mm_call([dh, g_up],
                    [pl.BlockSpec((None, tm, bw), lambda i, j, k: (_half_of(k, half)[0], i, _half_of(k, half)[1])),
                     pl.BlockSpec((None, None, N, bw), lambda i, j, k: (k, l, 0, 0))],
                    pl.BlockSpec((tm, N), lambda i, j, k: (i, 0)),
                    jax.ShapeDtypeStruct((S, N), F32), (S // tm, 1, nblk),
                    ta=False, tb=True, nk=nblk, acc_shape=(tm, N), has_add=False, keep=None, name=name)


def _mm_up_dw(u, dh, nblk, keep, *, name):
    S, M = u.shape
    half = nblk // 2
    bw = dh.shape[2] // half
    tk = _tile(S, 2048)
    tm = _mm_rows(M, bw, tk, S // tk, jnp.dtype(GRAD_WIRE).itemsize, False)
    return _mm_call([u, dh],
                    [pl.BlockSpec((tk, tm), lambda i, j, k: (k, i)),
                     pl.BlockSpec((None, tk, bw), lambda i, j, k: (_half_of(j, half)[0], k, _half_of(j, half)[1]))],
                    pl.BlockSpec((None, tm, keep), lambda i, j, k: (j, i, 0)),
                    jax.ShapeDtypeStruct((nblk, M, keep), GRAD_WIRE), (M // tm, nblk, S // tk),
                    ta=True, tb=False, nk=S // tk, acc_shape=(tm, bw), has_add=False, keep=keep, name=name)


ROW_TILE = 512
HALO = 8
POOL_HALO = 16


def _row_tile(S):
    return min(ROW_TILE, S)


def _acc_out(ref, val, first):
    @pl.when(first)
    def _():
        ref[...] = val

    @pl.when(jnp.logical_not(first))
    def _():
        ref[...] += val


def _rms_fwd(x, w, *, name):
    S, D = x.shape
    ts = _row_tile(S)

    def body(x_ref, w_ref, u_ref):
        xv = x_ref[...]
        r = lax.rsqrt(jnp.mean(xv * xv, axis=-1, keepdims=True) + NORM_EPS)
        u_ref[...] = (xv * r * w_ref[...]).astype(BF16)

    return pl.pallas_call(
        body, name=name, grid=(S // ts,),
        in_specs=[pl.BlockSpec((ts, D), lambda i: (i, 0)), pl.BlockSpec((1, D), lambda i: (0, 0))],
        out_specs=pl.BlockSpec((ts, D), lambda i: (i, 0)),
        out_shape=jax.ShapeDtypeStruct((S, D), BF16),
        compiler_params=_cparams("parallel"),
    )(x, w)


def _rms_bwd(x, w, du, g, *, name):
    S, D = x.shape
    ts = _row_tile(S)

    def body(x_ref, w_ref, du_ref, g_ref, dx_ref, dxb_ref, dw_ref):
        xv = x_ref[...]
        r = lax.rsqrt(jnp.mean(xv * xv, axis=-1, keepdims=True) + NORM_EPS)
        n = xv * r
        duv = du_ref[...]
        dn = duv * w_ref[...]
        dx = g_ref[...] + r * (dn - n * jnp.mean(dn * n, axis=-1, keepdims=True))
        dx_ref[...] = dx
        dxb_ref[...] = dx.astype(BF16)
        _acc_out(dw_ref, jnp.sum(duv * n, axis=0, keepdims=True), pl.program_id(0) == 0)

    row = pl.BlockSpec((ts, D), lambda i: (i, 0))
    vec = pl.BlockSpec((1, D), lambda i: (0, 0))
    return pl.pallas_call(
        body, name=name, grid=(S // ts,),
        in_specs=[row, vec, row, row],
        out_specs=[row, row, vec],
        out_shape=[jax.ShapeDtypeStruct((S, D), F32), jax.ShapeDtypeStruct((S, D), BF16),
                   jax.ShapeDtypeStruct((1, D), F32)],
        compiler_params=_cparams("arbitrary"),
    )(x, w, du, g)


def _loss_head(x, w, target, *, name):
    S, D = x.shape
    ts = _row_tile(S)

    def body(x_ref, w_ref, t_ref, loss_ref, dx_ref, dxb_ref, dw_ref):
        xv = x_ref[...]
        wv = w_ref[...]
        r = lax.rsqrt(jnp.mean(xv * xv, axis=-1, keepdims=True) + NORM_EPS)
        n = xv * r
        err = n * wv - t_ref[...]
        part = 0.5 * jnp.sum(jnp.mean(err * err, axis=-1, keepdims=True), axis=0, keepdims=True)
        dy = err * (1.0 / D)
        dn = dy * wv
        dx = r * (dn - n * jnp.mean(dn * n, axis=-1, keepdims=True))
        dx_ref[...] = dx
        dxb_ref[...] = dx.astype(BF16)
        first = pl.program_id(0) == 0
        _acc_out(dw_ref, jnp.sum(dy * n, axis=0, keepdims=True), first)
        _acc_out(loss_ref, jnp.broadcast_to(part, loss_ref.shape), first)

    row = pl.BlockSpec((ts, D), lambda i: (i, 0))
    vec = pl.BlockSpec((1, D), lambda i: (0, 0))
    return pl.pallas_call(
        body, name=name, grid=(S // ts,),
        in_specs=[row, vec, row],
        out_specs=[pl.BlockSpec((SUBLANE, LANE), lambda i: (0, 0)), row, row, vec],
        out_shape=[jax.ShapeDtypeStruct((SUBLANE, LANE), F32), jax.ShapeDtypeStruct((S, D), F32),
                   jax.ShapeDtypeStruct((S, D), BF16), jax.ShapeDtypeStruct((1, D), F32)],
        compiler_params=_cparams("arbitrary"),
    )(x, w, target)


def _sigmoid(x):
    return 1.0 / (1.0 + jnp.exp(-x))


def _merge_fwd(gl, ya, yb, yc, *, name):
    S, D = ya.shape
    ts = _row_tile(S)

    def body(ga_ref, gb_ref, gc_ref, ya_ref, yb_ref, yc_ref, o_ref):
        m = (_sigmoid(ga_ref[...]) * ya_ref[...] + _sigmoid(gb_ref[...]) * yb_ref[...]
             + _sigmoid(gc_ref[...]) * yc_ref[...])
        o_ref[...] = m.astype(BF16)

    row = pl.BlockSpec((ts, D), lambda i: (i, 0))
    gspec = [pl.BlockSpec((ts, D), functools.partial(lambda b, i: (i, b), b)) for b in range(3)]
    return pl.pallas_call(
        body, name=name, grid=(S // ts,),
        in_specs=gspec + [row, row, row],
        out_specs=row,
        out_shape=jax.ShapeDtypeStruct((S, D), BF16),
        compiler_params=_cparams("parallel"),
    )(gl, gl, gl, ya, yb, yc)


def _merge_bwd(gl, ya, yb, yc, dm, *, name):
    S, D = ya.shape
    ts = _row_tile(S)

    def body(ga_ref, gb_ref, gc_ref, ya_ref, yb_ref, yc_ref, dm_ref, dgl_ref, da_ref, db_ref, dc_ref):
        dmv = dm_ref[...]
        for b, (g_ref, y_ref, dy_ref) in enumerate(((ga_ref, ya_ref, da_ref), (gb_ref, yb_ref, db_ref),
                                                     (gc_ref, yc_ref, dc_ref))):
            s = _sigmoid(g_ref[...])
            dy_ref[...] = (dmv * s).astype(BF16)
            dgl_ref[:, b * D:(b + 1) * D] = (dmv * y_ref[...] * s * (1.0 - s)).astype(BF16)

    row = pl.BlockSpec((ts, D), lambda i: (i, 0))
    gspec = [pl.BlockSpec((ts, D), functools.partial(lambda b, i: (i, b), b)) for b in range(3)]
    return pl.pallas_call(
        body, name=name, grid=(S // ts,),
        in_specs=gspec + [row, row, row, row],
        out_specs=[pl.BlockSpec((ts, 3 * D), lambda i: (i, 0)), row, row, row],
        out_shape=[jax.ShapeDtypeStruct((S, 3 * D), BF16)] + [jax.ShapeDtypeStruct((S, D), BF16)] * 3,
        compiler_params=_cparams("parallel"),
    )(gl, gl, gl, ya, yb, yc, dm)


POOL_WIDTH = 512


def _pool_cnt(t, w):
    return jnp.minimum(t + 1, w).astype(F32)


def _pool_fwd(rest, col, mix, scale, *, name):
    S = rest.shape[0]
    W, G, H = POOL_WIDTH, POOL_GROUP_DIM, POOL_HALO
    ts = _row_tile(S)

    def body(v_ref, h_ref, mix_ref, sc_ref, d_ref, y_ref, ext_ref):
        i = pl.program_id(0)
        cur = v_ref[...]
        ext_ref[0:H, :] = jnp.where(i > 0, h_ref[...], 0.0)
        ext_ref[H:, :] = cur
        t = i * ts + lax.broadcasted_iota(jnp.int32, (ts, 1), 0)
        for g, w in enumerate(POOL_WINDOWS):
            cols = slice(g * G, (g + 1) * G)
            acc = cur[:, cols]
            for k in range(1, w):
                acc = acc + ext_ref[pl.ds(H - k, ts), cols]
            db = (acc / _pool_cnt(t, w) - cur[:, cols]).astype(BF16)
            d_ref[:, cols] = db
            y = jnp.dot(db, mix_ref[g], preferred_element_type=F32) * sc_ref[:, cols]
            y_ref[:, cols] = y.astype(BF16)

    row = pl.BlockSpec((ts, W), lambda i: (i, 0))
    return pl.pallas_call(
        body, name=name, grid=(S // ts,),
        in_specs=[pl.BlockSpec((ts, W), lambda i: (i, col)),
                  pl.BlockSpec((H, W), lambda i: (jnp.maximum(i * (ts // H) - 1, 0), col)),
                  pl.BlockSpec((len(POOL_WINDOWS), G, G), lambda i: (0, 0, 0)),
                  pl.BlockSpec((1, W), lambda i: (0, 0))],
        out_specs=[row, row],
        out_shape=[jax.ShapeDtypeStruct((S, W), BF16)] * 2,
        scratch_shapes=[pltpu.VMEM((H + ts, W), F32)],
        compiler_params=_cparams("parallel"),
    )(rest, rest, mix, scale)


def _pool_bwd(dy, d, mix, scale, *, name):
    S = dy.shape[0]
    W, G, H = POOL_WIDTH, POOL_GROUP_DIM, POOL_HALO
    ts = _row_tile(S)
    n = S // ts
    NT = (((1,), (1,)), ((), ()))
    TN = (((0,), (0,)), ((), ()))

    def body(dy_ref, dyn_ref, d_ref, mix_ref, sc_ref, dv_ref, dmix_ref, dsc_ref, ext_ref):
        i = pl.program_id(0)

        @pl.when(i == 0)
        def _():
            dmix_ref[...] = jnp.zeros_like(dmix_ref)
            dsc_ref[...] = jnp.zeros_like(dsc_ref)

        t = i * ts + lax.broadcasted_iota(jnp.int32, (ts, 1), 0)
        tn = (i + 1) * ts + lax.broadcasted_iota(jnp.int32, (H, 1), 0)
        for g, w in enumerate(POOL_WINDOWS):
            cols = slice(g * G, (g + 1) * G)
            sc = sc_ref[:, cols]
            dyv = dy_ref[:, cols]
            db = d_ref[:, cols]
            mg = mix_ref[g]
            yp = jnp.dot(db, mg, preferred_element_type=F32)
            dsc_ref[:, cols] += jnp.sum(dyv * yp, axis=0, keepdims=True)
            dyp = (dyv * sc).astype(BF16)
            dd = lax.dot_general(dyp, mg, NT, preferred_element_type=F32)
            dmix_ref[g] += lax.dot_general(db, dyp, TN, preferred_element_type=F32)
            dyn = (jnp.where(i < n - 1, dyn_ref[:, cols], 0.0) * sc).astype(BF16)
            ddn = lax.dot_general(dyn, mg, NT, preferred_element_type=F32)
            e = dd / _pool_cnt(t, w)
            ext_ref[0:ts, cols] = e
            ext_ref[ts:, cols] = ddn / _pool_cnt(tn, w)
            acc = e
            for k in range(1, w):
                acc = acc + ext_ref[pl.ds(k, ts), cols]
            dv_ref[:, cols] = (acc - dd).astype(BF16)

    row = pl.BlockSpec((ts, W), lambda i: (i, 0))
    return pl.pallas_call(
        body, name=name, grid=(n,),
        in_specs=[row,
                  pl.BlockSpec((H, W), lambda i: (jnp.minimum((i + 1) * (ts // H), S // H - 1), 0)),
                  row,
                  pl.BlockSpec((len(POOL_WINDOWS), G, G), lambda i: (0, 0, 0)),
                  pl.BlockSpec((1, W), lambda i: (0, 0))],
        out_specs=[row, pl.BlockSpec((len(POOL_WINDOWS), G, G), lambda i: (0, 0, 0)),
                   pl.BlockSpec((1, W), lambda i: (0, 0))],
        out_shape=[jax.ShapeDtypeStruct((S, W), BF16),
                   jax.ShapeDtypeStruct((len(POOL_WINDOWS), G, G), F32),
                   jax.ShapeDtypeStruct((1, W), F32)],
        scratch_shapes=[pltpu.VMEM((ts + H, W), F32)],
        compiler_params=_cparams("arbitrary"),
    )(dy, dy, d, mix, scale)


def _conv_taps(ext_ref, w_ref, b_ref, rows, first):
    K = w_ref.shape[0]
    pre = b_ref[...] + w_ref[K - 1:K, :] * ext_ref[pl.ds(first, rows), :]
    for k in range(K - 1):
        pre = pre + w_ref[k:k + 1, :] * ext_ref[pl.ds(first - (K - 1) + k, rows), :]
    return pre


ROW_CHUNK = 40


def _chunks(rows):
    ch = max(c for c in range(SUBLANE, ROW_CHUNK + 1, SUBLANE) if rows % c == 0)
    return [(r0, ch) for r0 in range(0, rows, ch)]


def _conv_bwd_taps(g_ref, ext_ref, w_ref, dx_store, ts):
    K = w_ref.shape[0]
    db, dws = 0.0, [0.0] * K
    fold = lambda v: jnp.sum(v.reshape(v.shape[0] // SUBLANE, SUBLANE, v.shape[1]), axis=0)
    for r0, ch in _chunks(ts):
        dx = w_ref[K - 1:K, :] * g_ref[pl.ds(r0, ch), :]
        for k in range(K - 1):
            dx = dx + w_ref[k:k + 1, :] * g_ref[pl.ds(r0 + K - 1 - k, ch), :]
        dx_store(r0, ch, dx.astype(BF16))
        gc = g_ref[pl.ds(r0, ch), :]
        db = db + fold(gc)
        for k in range(K):
            dws[k] = dws[k] + fold(gc * ext_ref[pl.ds(HALO - (K - 1) + k + r0, ch), :])
    total = lambda v: jnp.sum(v, axis=0, keepdims=True)
    return total(db), jnp.concatenate([total(d) for d in dws], axis=0)


def _prev_halo_spec(ts, tc, colfn):
    return pl.BlockSpec((HALO, tc), lambda *g: (jnp.maximum(g[-1] * (ts // HALO) - 1, 0), colfn(*g)))


def _conv_silu_fwd(rest, col0, C, w, b, *, name):
    S = rest.shape[0]
    ts, tc = _row_tile(S), 512
    assert col0 % tc == 0 and C % tc == 0
    cb = col0 // tc

    def body(x_ref, h_ref, w_ref, b_ref, o_ref, ext_ref):
        i = pl.program_id(1)
        ext_ref[0:HALO, :] = jnp.where(i > 0, h_ref[...], 0.0)
        ext_ref[HALO:, :] = x_ref[...]
        for r0, ch in _chunks(ts):
            pre = _conv_taps(ext_ref, w_ref, b_ref, ch, HALO + r0)
            o_ref[pl.ds(r0, ch), :] = pre * _sigmoid(pre)

    K = w.shape[0]
    return pl.pallas_call(
        body, name=name, grid=(C // tc, S // ts),
        in_specs=[pl.BlockSpec((ts, tc), lambda j, i: (i, cb + j)),
                  _prev_halo_spec(ts, tc, lambda j, i: cb + j),
                  pl.BlockSpec((K, tc), lambda j, i: (0, j)),
                  pl.BlockSpec((1, tc), lambda j, i: (0, j))],
        out_specs=pl.BlockSpec((ts, tc), lambda j, i: (i, j)),
        out_shape=jax.ShapeDtypeStruct((S, C), F32),
        scratch_shapes=[pltpu.VMEM((HALO + ts, tc), F32)],
        compiler_params=_cparams("parallel", "parallel"),
    )(rest, rest, w, b)


def _dsilu(pre):
    s = _sigmoid(pre)
    return s, s * (1.0 + pre * (1.0 - s))


def _conv_silu_bwd(rest, col0, C, w, b, dy, *, name):
    S = rest.shape[0]
    ts, tc = _row_tile(S), 512
    cb = col0 // tc
    n = S // ts
    K = w.shape[0]
    R = ts + HALO

    def body(x_ref, hp_ref, hn_ref, w_ref, b_ref, dy_ref, dyn_ref, dx_ref, dw_ref, db_ref, ext_ref, g_ref):
        i = pl.program_id(1)
        last = i == n - 1
        ext_ref[0:HALO, :] = jnp.where(i > 0, hp_ref[...], 0.0)
        ext_ref[HALO:HALO + ts, :] = x_ref[...]
        ext_ref[HALO + ts:, :] = jnp.where(last, 0.0, hn_ref[...])
        g_ref[0:ts, :] = dy_ref[...]
        g_ref[ts:, :] = jnp.where(last, 0.0, dyn_ref[...])
        for r0, ch in _chunks(R):
            pre = _conv_taps(ext_ref, w_ref, b_ref, ch, HALO + r0)
            g_ref[pl.ds(r0, ch), :] = g_ref[pl.ds(r0, ch), :] * _dsilu(pre)[1]

        def dx_store(r0, ch, v):
            dx_ref[pl.ds(r0, ch), :] = v

        db, dw = _conv_bwd_taps(g_ref, ext_ref, w_ref, dx_store, ts)
        _acc_out(db_ref, db, i == 0)
        _acc_out(dw_ref, dw, i == 0)

    nxt = lambda cf: pl.BlockSpec((HALO, tc), lambda j, i: (jnp.minimum((i + 1) * (ts // HALO), S // HALO - 1), cf(j)))
    return pl.pallas_call(
        body, name=name, grid=(C // tc, n),
        in_specs=[pl.BlockSpec((ts, tc), lambda j, i: (i, cb + j)),
                  _prev_halo_spec(ts, tc, lambda j, i: cb + j),
                  nxt(lambda j: cb + j),
                  pl.BlockSpec((K, tc), lambda j, i: (0, j)),
                  pl.BlockSpec((1, tc), lambda j, i: (0, j)),
                  pl.BlockSpec((ts, tc), lambda j, i: (i, j)),
                  nxt(lambda j: j)],
        out_specs=[pl.BlockSpec((ts, tc), lambda j, i: (i, j)),
                   pl.BlockSpec((K, tc), lambda j, i: (0, j)),
                   pl.BlockSpec((1, tc), lambda j, i: (0, j))],
        out_shape=[jax.ShapeDtypeStruct((S, C), BF16), jax.ShapeDtypeStruct((K, C), F32),
                   jax.ShapeDtypeStruct((1, C), F32)],
        scratch_shapes=[pltpu.VMEM((HALO + R, tc), F32), pltpu.VMEM((R, tc), F32)],
        compiler_params=_cparams("parallel", "arbitrary"),
    )(rest, rest, rest, w, b, dy, dy)


FFN_TC = 256


def _ffn_act_fwd(h0, w, b, *, name):
    S, F2 = h0.shape
    F = F2 // 2
    ts, tc = _row_tile(S), FFN_TC
    nb = F // tc
    K = w.shape[0]

    def body(xg_ref, hg_ref, xv_ref, hv_ref, wg_ref, wv_ref, bg_ref, bv_ref, o_ref, eg_ref, ev_ref):
        i = pl.program_id(1)
        for x_ref, h_ref, e_ref in ((xg_ref, hg_ref, eg_ref), (xv_ref, hv_ref, ev_ref)):
            e_ref[0:HALO, :] = jnp.where(i > 0, h_ref[...], 0.0)
            e_ref[HALO:, :] = x_ref[...]
        for r0, ch in _chunks(ts):
            pg = _conv_taps(eg_ref, wg_ref, bg_ref, ch, HALO + r0)
            pv = _conv_taps(ev_ref, wv_ref, bv_ref, ch, HALO + r0)
            o_ref[pl.ds(r0, ch), :] = (pg * _sigmoid(pg) * pv).astype(BF16)

    def half(off):
        return [pl.BlockSpec((ts, tc), lambda j, i: (i, off + j)), _prev_halo_spec(ts, tc, lambda j, i: off + j)]

    wspec = lambda off: pl.BlockSpec((K, tc), lambda j, i: (0, off + j))
    bspec = lambda off: pl.BlockSpec((1, tc), lambda j, i: (0, off + j))
    return pl.pallas_call(
        body, name=name, grid=(nb, S // ts),
        in_specs=half(0) + half(nb) + [wspec(0), wspec(nb), bspec(0), bspec(nb)],
        out_specs=pl.BlockSpec((ts, tc), lambda j, i: (i, j)),
        out_shape=jax.ShapeDtypeStruct((S, F), BF16),
        scratch_shapes=[pltpu.VMEM((HALO + ts, tc), F32)] * 2,
        compiler_params=_cparams("parallel", "parallel"),
    )(h0, h0, h0, h0, w, w, b, b)


def _ffn_act_bwd(h0, w, b, da, *, name):
    S, F2 = h0.shape
    F = F2 // 2
    ts, tc = _row_tile(S), FFN_TC
    nb = F // tc
    n = S // ts
    K = w.shape[0]
    R = ts + HALO

    def body(xg_ref, pg_ref, ng_ref, xv_ref, pv_ref, nv_ref, wg_ref, wv_ref, bg_ref, bv_ref, da_ref, dan_ref,
             dx_ref, dw_ref, db_ref, eg_ref, ev_ref, gg_ref, gv_ref):
        i = pl.program_id(1)
        last = i == n - 1
        for x_ref, p_ref, n_ref, e_ref in ((xg_ref, pg_ref, ng_ref, eg_ref), (xv_ref, pv_ref, nv_ref, ev_ref)):
            e_ref[0:HALO, :] = jnp.where(i > 0, p_ref[...], 0.0)
            e_ref[HALO:HALO + ts, :] = x_ref[...]
            e_ref[HALO + ts:, :] = jnp.where(last, 0.0, n_ref[...])
        gg_ref[0:ts, :] = da_ref[...]
        gg_ref[ts:, :] = jnp.where(last, 0.0, dan_ref[...])
        for r0, ch in _chunks(R):
            rows = pl.ds(r0, ch)
            dav = gg_ref[rows, :]
            pg = _conv_taps(eg_ref, wg_ref, bg_ref, ch, HALO + r0)
            pv = _conv_taps(ev_ref, wv_ref, bv_ref, ch, HALO + r0)
            s, ds = _dsilu(pg)
            gg_ref[rows, :] = dav * pv * ds
            gv_ref[rows, :] = dav * pg * s
        for h, (g_ref, e_ref, w_ref) in enumerate(((gg_ref, eg_ref, wg_ref), (gv_ref, ev_ref, wv_ref))):
            def dx_store(r0, ch, v):
                dx_ref[h, pl.ds(r0, ch), :] = v

            db, dw = _conv_bwd_taps(g_ref, e_ref, w_ref, dx_store, ts)
            _acc_out(db_ref.at[h], db, i == 0)
            _acc_out(dw_ref.at[h], dw, i == 0)

    nxt = lambda off: pl.BlockSpec((HALO, tc), lambda j, i: (jnp.minimum((i + 1) * (ts // HALO), S // HALO - 1), off + j))

    def half(off):
        return [pl.BlockSpec((ts, tc), lambda j, i: (i, off + j)), _prev_halo_spec(ts, tc, lambda j, i: off + j), nxt(off)]

    wspec = lambda off: pl.BlockSpec((K, tc), lambda j, i: (0, off + j))
    bspec = lambda off: pl.BlockSpec((1, tc), lambda j, i: (0, off + j))
    return pl.pallas_call(
        body, name=name, grid=(nb, n),
        in_specs=half(0) + half(nb) + [wspec(0), wspec(nb), bspec(0), bspec(nb),
                                       pl.BlockSpec((ts, tc), lambda j, i: (i, j)), nxt(0)],
        out_specs=[pl.BlockSpec((2, ts, tc), lambda j, i: (0, i, j)),
                   pl.BlockSpec((2, K, tc), lambda j, i: (0, 0, j)),
                   pl.BlockSpec((2, 1, tc), lambda j, i: (0, 0, j))],
        out_shape=[jax.ShapeDtypeStruct((2, S, F), BF16), jax.ShapeDtypeStruct((2, K, F), F32),
                   jax.ShapeDtypeStruct((2, 1, F), F32)],
        scratch_shapes=[pltpu.VMEM((HALO + R, tc), F32)] * 2 + [pltpu.VMEM((R, tc), F32)] * 2,
        compiler_params=_cparams("parallel", "arbitrary"),
    )(h0, h0, h0, h0, h0, h0, w, w, b, b, da, da)


CUM_TILE = 256
ATT_TILE = 512
ATT_TILE_FWD = 1024
NEG = -1e30
N_PAIR = ATTN_HEADS // 2
NT_DIMS = (((1,), (1,)), ((), ()))


def _split3(x):
    a = x.astype(BF16)
    r = x - a.astype(F32)
    b = r.astype(BF16)
    c = (r - b.astype(F32)).astype(BF16)
    return a, b, c


def _tri_dot(tri, x):
    return sum(jnp.dot(tri, p, preferred_element_type=F32) for p in _split3(x))


def _log_sigmoid(x):
    return jnp.minimum(x, 0.0) - jnp.log(1.0 + jnp.exp(-jnp.abs(x)))


def _col(v, idx, lane):
    return jnp.sum(jnp.where(lane == idx, v, 0.0), axis=1, keepdims=True)


def _fox_pre(rest, col, fb, *, name):
    S = rest.shape[0]
    ts = min(CUM_TILE, S)

    def body(f_ref, fb_ref, ccol_ref, carry_ref):
        i = pl.program_id(0)

        @pl.when(i == 0)
        def _():
            carry_ref[...] = jnp.zeros_like(carry_ref)

        lane = lax.broadcasted_iota(jnp.int32, (1, LANE), 1)
        logf = jnp.where(lane < ATTN_HEADS, _log_sigmoid(f_ref[...] + fb_ref[...]), 0.0)
        r = lax.broadcasted_iota(jnp.int32, (ts, ts), 0)
        c = lax.broadcasted_iota(jnp.int32, (ts, ts), 1)
        tri = jnp.where(c <= r, 1.0, 0.0).astype(BF16)
        cs = _tri_dot(tri, logf) + carry_ref[...]
        ccol_ref[...] = cs
        carry_ref[...] = cs[ts - 1:ts, :]

    return pl.pallas_call(
        body, name=name, grid=(S // ts,),
        in_specs=[pl.BlockSpec((ts, LANE), lambda i: (i, col)), pl.BlockSpec((1, LANE), lambda i: (0, 0))],
        out_specs=pl.BlockSpec((ts, LANE), lambda i: (i, 0)),
        out_shape=jax.ShapeDtypeStruct((S, LANE), F32),
        scratch_shapes=[pltpu.VMEM((1, LANE), F32)],
        compiler_params=_cparams("arbitrary"),
    )(rest, fb)


def _fox_post(dcq, dck, rest, col, fb, ddt, *, name):
    S = rest.shape[0]
    ts = min(CUM_TILE, S)
    n = S // ts

    def body(dcq_ref, dck_ref, f_ref, fb_ref, ddt_ref, o_ref, db_ref, carry_ref):
        i = pl.program_id(0)

        @pl.when(i == 0)
        def _():
            carry_ref[...] = jnp.zeros_like(carry_ref)

        lane = lax.broadcasted_iota(jnp.int32, (1, LANE), 1)
        dc = jnp.zeros((ts, LANE), F32)
        for h in range(ATTN_HEADS):
            d = dcq_ref[h // 2] - dck_ref[h // 2]
            dc = jnp.where(lane == h, _col(d, h % 2, lane), dc)
        r = lax.broadcasted_iota(jnp.int32, (ts, ts), 0)
        c = lax.broadcasted_iota(jnp.int32, (ts, ts), 1)
        tri = jnp.where(c >= r, 1.0, 0.0).astype(BF16)
        rc = _tri_dot(tri, dc) + carry_ref[...]
        carry_ref[...] = rc[0:1, :]
        df = rc * _sigmoid(-(f_ref[...] + fb_ref[...]))
        out = jnp.where(lane < ATTN_HEADS, df, ddt_ref[...])
        o_ref[...] = out.astype(BF16)
        _acc_out(db_ref, jnp.sum(out, axis=0, keepdims=True), i == 0)

    rev = lambda i: n - 1 - i
    return pl.pallas_call(
        body, name=name, grid=(n,),
        in_specs=[pl.BlockSpec((N_PAIR, ts, LANE), lambda i: (0, rev(i), 0)),
                  pl.BlockSpec((N_PAIR, ts, LANE), lambda i: (0, rev(i), 0)),
                  pl.BlockSpec((ts, LANE), lambda i: (rev(i), col)),
                  pl.BlockSpec((1, LANE), lambda i: (0, 0)),
                  pl.BlockSpec((ts, LANE), lambda i: (rev(i), 0))],
        out_specs=[pl.BlockSpec((ts, LANE), lambda i: (rev(i), 0)), pl.BlockSpec((1, LANE), lambda i: (0, 0))],
        out_shape=[jax.ShapeDtypeStruct((S, LANE), BF16), jax.ShapeDtypeStruct((1, LANE), F32)],
        scratch_shapes=[pltpu.VMEM((1, LANE), F32)],
        compiler_params=_cparams("arbitrary"),
    )(dcq, dck, rest, fb, ddt)


def _head_masks():
    lane = lax.broadcasted_iota(jnp.int32, (1, LANE), 1)
    return lane, (lane < ATTN_HEAD_DIM, lane >= ATTN_HEAD_DIM)


def _causal_mask(t):
    r = lax.broadcasted_iota(jnp.int32, (t, t), 0)
    c = lax.broadcasted_iota(jnp.int32, (t, t), 1)
    return r, c


AUG_ONE = 0
AUG_C = 3


def _free_lane0(hh):
    return ATTN_HEAD_DIM if hh == 0 else 0


def _augment(x, head_mask, lane, f0, c, key_side):
    terms = _split3(c)
    one = jnp.ones((), BF16)
    ones_at, terms_at = (AUG_ONE, AUG_C) if key_side else (AUG_C, AUG_ONE)
    out = jnp.where(head_mask, x, jnp.zeros((), BF16))
    for k in range(3):
        out = jnp.where(lane == f0 + ones_at + k, one, out)
        out = jnp.where(lane == f0 + terms_at + k, terms[k], out)
    return out


def _carry_start(kind, in_refs, out_refs, sems, first):
    @pl.when(first)
    def _():
        if kind == "gather":
            _gather_phase(0, in_refs, out_refs, sems)
        else:
            _a2a_start(in_refs, out_refs, sems)


def _carry_mid(kind, in_refs, out_refs, sems, mid):
    if kind == "gather":
        @pl.when(mid)
        def _():
            _gather_phase(1, in_refs, out_refs, sems)


def _carry_wait(kind, in_refs, out_refs, sems, last):
    @pl.when(last)
    def _():
        if kind == "gather":
            _gather_phase(2, in_refs, out_refs, sems)
        else:
            _a2a_wait(in_refs, out_refs, sems)


def _attn_fwd(qkv, ccol, *, comm=None, name):
    S = qkv.shape[0]
    t = min(ATT_TILE_FWD, S)
    n = S // t
    scale = ATTN_HEAD_DIM ** -0.5

    nc = len(comm[1]) if comm else 0

    def body(*refs):
        q_ref, k_ref, v_ref, cc_ref = refs[:4]
        o_ref, lrow_ref = refs[4 + nc:6 + nc]
        kaug_ref, vt_ref, m_ref, l_ref, acc_ref = refs[6 + 2 * nc:11 + 2 * nc]
        j, i = pl.program_id(0), pl.program_id(1)
        lane, masks = _head_masks()
        if comm:
            carried = (refs[4:4 + nc], refs[6 + nc:6 + 2 * nc], refs[11 + 2 * nc:])
            _carry_start(comm[0], *carried, jnp.logical_and(j == 0, i == 0))
            _carry_mid(comm[0], *carried, jnp.logical_and(j == N_PAIR - 1, i == 0))

        @pl.when(i == 0)
        def _():
            def fill(b, carry):
                off = pl.multiple_of(b * t, t)
                kblk = k_ref[pl.ds(off, t), :]
                ccb = cc_ref[pl.ds(off, t), :]
                for hh in range(2):
                    ck = _col(ccb, 2 * j + hh, lane)
                    kaug_ref[hh, pl.ds(off, t), :] = _augment(kblk, masks[hh], lane, _free_lane0(hh), -ck, True)
                vt_ref[:, pl.ds(off, t)] = v_ref[pl.ds(off, t), :].astype(F32).T.astype(BF16)
                return carry
            lax.fori_loop(0, n, fill, 0)

        q2 = q_ref[...] * scale
        ccq = cc_ref[pl.ds(pl.multiple_of(i * t, t), t), :]
        qaug = [_augment(q2, masks[hh], lane, _free_lane0(hh), _col(ccq, 2 * j + hh, lane), False) for hh in range(2)]
        m_ref[...] = jnp.full(m_ref.shape, NEG, F32)
        l_ref[...] = jnp.zeros_like(l_ref)
        acc_ref[...] = jnp.zeros_like(acc_ref)

        def step(kb, masked):
            off = pl.multiple_of(kb * t, t)
            vt = vt_ref[:, pl.ds(off, t)]
            for hh in range(2):
                st = lax.dot_general(kaug_ref[hh, pl.ds(off, t), :], qaug[hh], NT_DIMS, preferred_element_type=F32)
                if masked:
                    r, c = _causal_mask(t)
                    st = jnp.where(r <= c, st, NEG)
                m_old = m_ref[hh]
                m_new = jnp.maximum(m_old, jnp.max(st, axis=0, keepdims=True))
                p = jnp.exp(st - m_new)
                alpha = jnp.exp(m_old - m_new)
                l_ref[hh] = alpha * l_ref[hh] + jnp.sum(p, axis=0, keepdims=True)
                acc_ref[hh] = alpha * acc_ref[hh] + jnp.dot(vt, p.astype(BF16), preferred_element_type=F32)
                m_ref[hh] = m_new

        def loop_body(kb, carry):
            step(kb, False)
            return carry

        lax.fori_loop(0, i, loop_body, 0)
        step(i, True)
        drow = lax.broadcasted_iota(jnp.int32, (LANE, 1), 0)
        ot = jnp.where(drow < ATTN_HEAD_DIM, acc_ref[0] / l_ref[0], acc_ref[1] / l_ref[1])
        o_ref[...] = ot.T.astype(BF16)
        for hh in range(2):
            lrow_ref[hh] = m_ref[hh] + jnp.log(l_ref[hh])
        if comm:
            _carry_wait(comm[0], *carried, jnp.logical_and(j == N_PAIR - 1, i == n - 1))

    res = pl.pallas_call(
        body, name=name, grid=(N_PAIR, n),
        in_specs=[pl.BlockSpec((t, LANE), lambda j, i: (i, j)),
                  pl.BlockSpec((S, LANE), lambda j, i: (0, N_PAIR + j)),
                  pl.BlockSpec((S, LANE), lambda j, i: (0, 2 * N_PAIR + j)),
                  pl.BlockSpec((S, LANE), lambda j, i: (0, 0))] + [ANY] * nc,
        out_specs=[pl.BlockSpec((t, LANE), lambda j, i: (i, j)),
                   pl.BlockSpec((2, 1, t), lambda j, i: (j, 0, i))] + [ANY] * nc,
        out_shape=[jax.ShapeDtypeStruct((S, N_PAIR * LANE), BF16), jax.ShapeDtypeStruct((ATTN_HEADS, 1, S), F32)]
        + (_a2a_out_shapes(*comm) if comm else []),
        scratch_shapes=[pltpu.VMEM((2, S, LANE), BF16), pltpu.VMEM((LANE, S), BF16),
                        pltpu.VMEM((2, 1, t), F32), pltpu.VMEM((2, 1, t), F32), pltpu.VMEM((2, LANE, t), F32)]
        + (_a2a_sems(nc) if comm else []),
        compiler_params=_cparams("arbitrary" if comm else "parallel", "arbitrary"),
    )(qkv, qkv, qkv, ccol, *(comm[1] if comm else []))
    return res[0], res[1], list(res[2:])


def _attn_bwd(qkv, o, do, lrow, ccol, *, comm=None, name):
    S = qkv.shape[0]
    t = min(ATT_TILE, S)
    n = S // t
    scale = ATTN_HEAD_DIM ** -0.5

    nc = len(comm[1]) if comm else 0

    def body(*refs):
        q_ref, k_ref, v_ref, o_ref, do_ref, lrow_ref, cc_ref = refs[:7]
        dq_ref, dk_ref, dv_ref, dcq_ref, dck_ref = refs[7 + nc:12 + nc]
        qaug_ref, drow_ref, dqt_ref, dka_ref, dva_ref = refs[12 + 2 * nc:17 + 2 * nc]
        carried = (refs[7:7 + nc], refs[12 + nc:12 + 2 * nc], refs[17 + 2 * nc:])
        j, kb = pl.program_id(0), pl.program_id(1)
        lane, masks = _head_masks()
        if comm:
            _carry_start(comm[0], *carried, jnp.logical_and(j == 0, kb == 0))
        drows = lax.broadcasted_iota(jnp.int32, (LANE, 1), 0)

        @pl.when(kb == 0)
        def _():
            dqt_ref[...] = jnp.zeros_like(dqt_ref)

            def fill(b, carry):
                off = pl.multiple_of(b * t, t)
                q2 = q_ref[pl.ds(off, t), :] * scale
                ccb = cc_ref[pl.ds(off, t), :]
                prod_t = (do_ref[pl.ds(off, t), :].astype(F32) * o_ref[pl.ds(off, t), :].astype(F32)).T
                lo = jnp.sum(jnp.where(drows < ATTN_HEAD_DIM, prod_t, 0.0), axis=0, keepdims=True)
                drow_ref[0, :, pl.ds(off, t)] = lo
                drow_ref[1, :, pl.ds(off, t)] = jnp.sum(prod_t, axis=0, keepdims=True) - lo
                for hh in range(2):
                    cq = _col(ccb, 2 * j + hh, lane)
                    qaug_ref[hh, pl.ds(off, t), :] = _augment(q2, masks[hh], lane, _free_lane0(hh), cq, False)
                return carry
            lax.fori_loop(0, n, fill, 0)

        koff = pl.multiple_of(kb * t, t)
        kblk = k_ref[...]
        v2 = v_ref[...]
        cck = cc_ref[pl.ds(koff, t), :]
        kaug = [_augment(kblk, masks[hh], lane, _free_lane0(hh), -_col(cck, 2 * j + hh, lane), True) for hh in range(2)]
        kaug_t = [ka.astype(F32).T.astype(BF16) for ka in kaug]
        vh = [jnp.where(mk, v2, jnp.zeros((), BF16)) for mk in masks]
        dka_ref[...] = jnp.zeros_like(dka_ref)
        dva_ref[...] = jnp.zeros_like(dva_ref)

        def step(qb, masked):
            off = pl.multiple_of(qb * t, t)
            doblk = do_ref[pl.ds(off, t), :]
            for hh in range(2):
                qa = qaug_ref[hh, pl.ds(off, t), :]
                st = lax.dot_general(kaug[hh], qa, NT_DIMS, preferred_element_type=F32)
                if masked:
                    r, c = _causal_mask(t)
                    st = jnp.where(r <= c, st, NEG)
                pt = jnp.exp(st - lrow_ref[hh, :, pl.ds(off, t)])
                dpt = lax.dot_general(vh[hh], doblk, NT_DIMS, preferred_element_type=F32)
                dst = (pt * (dpt - drow_ref[hh, :, pl.ds(off, t)])).astype(BF16)
                dva_ref[hh] += jnp.dot(pt.astype(BF16), doblk, preferred_element_type=F32)
                dka_ref[hh] += jnp.dot(dst, qa, preferred_element_type=F32)
                dqt_ref[hh, :, pl.ds(off, t)] += jnp.dot(kaug_t[hh], dst, preferred_element_type=F32)

        step(kb, True)

        def loop_body(qb, carry):
            step(qb, False)
            return carry

        lax.fori_loop(kb + 1, n, loop_body, 0)
        dk_ref[...] = jnp.where(masks[0], dka_ref[0], dka_ref[1]).astype(BF16)
        dv_ref[...] = jnp.where(masks[0], dva_ref[0], dva_ref[1]).astype(BF16)
        dck = [_col(dka_ref[hh], _free_lane0(hh) + AUG_C, lane) for hh in range(2)]
        dck_ref[0] = jnp.where(lane == 0, dck[0], jnp.where(lane == 1, dck[1], 0.0))

        @pl.when(kb == n - 1)
        def _():
            def flush(b, carry):
                off = pl.multiple_of(b * t, t)
                d = [dqt_ref[hh, :, pl.ds(off, t)].T for hh in range(2)]
                dq_ref[pl.ds(off, t), :] = (jnp.where(masks[0], d[0], d[1]) * scale).astype(BF16)
                dcq = [_col(d[hh], _free_lane0(hh) + AUG_ONE, lane) for hh in range(2)]
                dcq_ref[0, pl.ds(off, t), :] = jnp.where(lane == 0, dcq[0], jnp.where(lane == 1, dcq[1], 0.0))
                return carry
            lax.fori_loop(0, n, flush, 0)

        if comm:
            _carry_wait(comm[0], *carried, jnp.logical_and(j == N_PAIR - 1, kb == n - 1))

    full = lambda cb: pl.BlockSpec((S, LANE), lambda j, kb: (0, cb(j)))
    kspec = lambda base: pl.BlockSpec((t, LANE), lambda j, kb: (kb, base + j))
    oblk = pl.BlockSpec((t, LANE), lambda j, kb: (kb, j))
    res = pl.pallas_call(
        body, name=name, grid=(N_PAIR, n),
        in_specs=[full(lambda j: j), kspec(N_PAIR), kspec(2 * N_PAIR), full(lambda j: j), full(lambda j: j),
                  pl.BlockSpec((2, 1, S), lambda j, kb: (j, 0, 0)), full(lambda j: 0)] + [ANY] * nc,
        out_specs=[full(lambda j: j), oblk, oblk,
                   pl.BlockSpec((1, S, LANE), lambda j, kb: (j, 0, 0)),
                   pl.BlockSpec((1, t, LANE), lambda j, kb: (j, kb, 0))] + [ANY] * nc,
        out_shape=[jax.ShapeDtypeStruct((S, N_PAIR * LANE), BF16)] * 3 + [jax.ShapeDtypeStruct((N_PAIR, S, LANE), F32)] * 2
        + (_a2a_out_shapes(*comm) if comm else []),
        scratch_shapes=[pltpu.VMEM((2, S, LANE), BF16), pltpu.VMEM((2, 1, S), F32), pltpu.VMEM((2, LANE, S), F32),
                        pltpu.VMEM((2, t, LANE), F32), pltpu.VMEM((2, t, LANE), F32)]
        + (_a2a_sems(nc) if comm else []),
        compiler_params=_cparams("arbitrary" if comm else "parallel", "arbitrary"),
    )(qkv, qkv, qkv, o, do, lrow, ccol, *(comm[1] if comm else []))
    return tuple(res[:5]) + (list(res[5:]),)


DT_LANE0 = ATTN_HEADS
SSD_PAIRS = SSD_HEADS // 2
SSD_X = SSD_HEADS * SSD_HEAD_DIM
SSD_B0 = SSD_X
SSD_C0 = SSD_X + SSD_GROUPS * SSD_STATE
SSD_CH = SSD_X + 2 * SSD_GROUPS * SSD_STATE
TN_DIMS = (((0,), (0,)), ((), ()))


def _softplus(x):
    return jnp.maximum(x, 0.0) + jnp.log(1.0 + jnp.exp(-jnp.abs(x)))


def _ssd_prep(fdt, dtb, alog):
    L = fdt.shape[0]
    lane = lax.broadcasted_iota(jnp.int32, (1, LANE), 1)
    hl = jnp.logical_and(lane >= DT_LANE0, lane < DT_LANE0 + SSD_HEADS)
    dtv = jnp.where(hl, _softplus(fdt + dtb), 0.0)
    A = jnp.where(hl, -jnp.exp(alog), 0.0)
    r = lax.broadcasted_iota(jnp.int32, (L, L), 0)
    c = lax.broadcasted_iota(jnp.int32, (L, L), 1)
    cs = _tri_dot(jnp.where(c <= r, 1.0, 0.0).astype(BF16), dtv * A)
    return lane, hl, dtv, A, cs, r, c


def _halves(lane, v0, v1):
    return jnp.where(lane < SSD_HEAD_DIM, v0, v1)


def _ssd_fwd(xc, rest, col, dtb, alog, dskip, *, name):
    S = xc.shape[0]
    L = SSD_CHUNK
    nc = S // L

    def body(xc_ref, f_ref, dtb_ref, al_ref, dk_ref, y_ref, hs_ref, h_ref, cst_ref):
        i = pl.program_id(0)

        @pl.when(i == 0)
        def _():
            h_ref[...] = jnp.zeros_like(h_ref)

        lane, hl, dtv, A, cs, r, c = _ssd_prep(f_ref[...], dtb_ref[...], al_ref[...])
        cst_ref[...] = cs.T
        cs_last = cs[L - 1:L, :]
        ecs = jnp.exp(cs)
        dec = jnp.exp(cs_last - cs)
        cd = jnp.exp(cs_last)
        dkv = dk_ref[...]
        prow = lax.broadcasted_iota(jnp.int32, (LANE, 1), 0)
        for g in range(SSD_GROUPS):
            Bg = xc_ref[:, SSD_B0 + g * SSD_STATE:SSD_B0 + (g + 1) * SSD_STATE].astype(BF16)
            Cg = xc_ref[:, SSD_C0 + g * SSD_STATE:SSD_C0 + (g + 1) * SSD_STATE].astype(BF16)
            CB = lax.dot_general(Cg, Bg, NT_DIMS, preferred_element_type=F32)
            for pp in range(SSD_PAIRS // SSD_GROUPS):
                pi = g * (SSD_PAIRS // SSD_GROUPS) + pp
                hl0 = DT_LANE0 + 2 * pi
                x2 = xc_ref[:, pi * LANE:(pi + 1) * LANE]
                xd = x2 * _halves(lane, _col(dtv, hl0, lane), _col(dtv, hl0 + 1, lane))
                xdb = xd.astype(BF16)
                yd = jnp.zeros((L, LANE), F32)
                for hh in range(2):
                    seg = _col(cs, hl0 + hh, lane) - cst_ref[hl0 + hh:hl0 + hh + 1, :]
                    M = CB * jnp.exp(jnp.where(c <= r, seg, NEG))
                    yh = jnp.dot(M.astype(BF16), xdb, preferred_element_type=F32)
                    yd = jnp.where((lane >= SSD_HEAD_DIM) if hh else (lane < SSD_HEAD_DIM), yh, yd)
                hp = h_ref[pi]
                hs_ref[0, pi] = hp
                yo = lax.dot_general(Cg, hp.astype(BF16), NT_DIMS, preferred_element_type=F32)
                yo = yo * _halves(lane, _col(ecs, hl0, lane), _col(ecs, hl0 + 1, lane))
                dsk = _halves(lane, _col(dkv, hl0, lane), _col(dkv, hl0 + 1, lane))
                y_ref[:, pi * LANE:(pi + 1) * LANE] = yd + yo + dsk * x2
                xw = (xd * _halves(lane, _col(dec, hl0, lane), _col(dec, hl0 + 1, lane))).astype(BF16)
                st = lax.dot_general(xw, Bg, TN_DIMS, preferred_element_type=F32)
                cdp = jnp.where(prow < SSD_HEAD_DIM, _col(cd, hl0, lane), _col(cd, hl0 + 1, lane))
                h_ref[pi] = cdp * hp + st

    vec = pl.BlockSpec((1, LANE), lambda i: (0, 0))
    return pl.pallas_call(
        body, name=name, grid=(nc,),
        in_specs=[pl.BlockSpec((L, SSD_CH), lambda i: (i, 0)), pl.BlockSpec((L, LANE), lambda i: (i, col)),
                  vec, vec, vec],
        out_specs=[pl.BlockSpec((L, SSD_X), lambda i: (i, 0)),
                   pl.BlockSpec((1, SSD_PAIRS, LANE, SSD_STATE), lambda i: (i, 0, 0, 0))],
        out_shape=[jax.ShapeDtypeStruct((S, SSD_X), F32),
                   jax.ShapeDtypeStruct((nc, SSD_PAIRS, LANE, SSD_STATE), F32)],
        scratch_shapes=[pltpu.VMEM((SSD_PAIRS, LANE, SSD_STATE), F32), pltpu.VMEM((LANE, L), F32)],
        compiler_params=_cparams("arbitrary"),
    )(xc, rest, dtb, alog, dskip)


def _pair_sums(lane, v):
    lo = jnp.sum(jnp.where(lane < SSD_HEAD_DIM, v, 0.0), axis=1, keepdims=True)
    return lo, jnp.sum(v, axis=1, keepdims=True) - lo


def _ssd_bwd(xc, rest, col, dtb, alog, dskip, hs, dy, *, name):
    S = xc.shape[0]
    L = SSD_CHUNK
    nc = S // L
    PG = SSD_PAIRS // SSD_GROUPS

    def body(xc_ref, f_ref, dtb_ref, al_ref, dk_ref, hs_ref, dy_ref, dxc_ref, ddt_ref, dp_ref, dh_ref, cst_ref):
        i = pl.program_id(0)

        @pl.when(i == 0)
        def _():
            dh_ref[...] = jnp.zeros_like(dh_ref)
            dp_ref[...] = jnp.zeros_like(dp_ref)

        fv = f_ref[...] + dtb_ref[...]
        lane, hl, dtv, A, cs, r, c = _ssd_prep(f_ref[...], dtb_ref[...], al_ref[...])
        cst_ref[...] = cs.T
        cs_last = cs[L - 1:L, :]
        ecs = jnp.exp(cs)
        dec = jnp.exp(cs_last - cs)
        cd = jnp.exp(cs_last)
        dkv = dk_ref[...]
        prow = lax.broadcasted_iota(jnp.int32, (LANE, 1), 0)
        lrow = lax.broadcasted_iota(jnp.int32, (L, 1), 0)
        is_last = lrow == L - 1
        causal = c <= r
        dcs = jnp.zeros((L, LANE), F32)
        ddt = jnp.zeros((L, LANE), F32)
        dD = jnp.zeros((1, LANE), F32)
        for g in range(SSD_GROUPS):
            Bg = xc_ref[:, SSD_B0 + g * SSD_STATE:SSD_B0 + (g + 1) * SSD_STATE].astype(BF16)
            Cg = xc_ref[:, SSD_C0 + g * SSD_STATE:SSD_C0 + (g + 1) * SSD_STATE].astype(BF16)
            CB = lax.dot_general(Cg, Bg, NT_DIMS, preferred_element_type=F32)
            dCB = jnp.zeros((L, L), F32)
            dB = jnp.zeros((L, SSD_STATE), F32)
            dC = jnp.zeros((L, SSD_STATE), F32)
            for pp in range(PG):
                pi = g * PG + pp
                hl0 = DT_LANE0 + 2 * pi
                x2 = xc_ref[:, pi * LANE:(pi + 1) * LANE]
                dy2 = dy_ref[:, pi * LANE:(pi + 1) * LANE]
                dtp = _halves(lane, _col(dtv, hl0, lane), _col(dtv, hl0 + 1, lane))
                xd = x2 * dtp
                xdb = xd.astype(BF16)
                dsk = _halves(lane, _col(dkv, hl0, lane), _col(dkv, hl0 + 1, lane))
                dx2 = dsk * dy2
                sD = _pair_sums(lane, dy2 * x2)
                hp = hs_ref[0, pi]
                hpb = hp.astype(BF16)
                ecsp = _halves(lane, _col(ecs, hl0, lane), _col(ecs, hl0 + 1, lane))
                yo = lax.dot_general(Cg, hpb, NT_DIMS, preferred_element_type=F32) * ecsp
                dW = (ecsp * dy2).astype(BF16)
                dC = dC + jnp.dot(dW, hpb, preferred_element_type=F32)
                dhp = lax.dot_general(dW, Cg, TN_DIMS, preferred_element_type=F32)
                sYo = _pair_sums(lane, dy2 * yo)
                dhn = dh_ref[pi]
                cdh = (_col(cd, hl0, lane), _col(cd, hl0 + 1, lane))
                dhp = dhp + jnp.where(prow < SSD_HEAD_DIM, cdh[0], cdh[1]) * dhn
                rs = jnp.sum(dhn * hp, axis=1, keepdims=True)
                lo = jnp.sum(jnp.where(prow < SSD_HEAD_DIM, rs, 0.0), axis=0, keepdims=True)
                dcd = (lo, jnp.sum(rs, axis=0, keepdims=True) - lo)
                dhnb = dhn.astype(BF16)
                decp = _halves(lane, _col(dec, hl0, lane), _col(dec, hl0 + 1, lane))
                G = lax.dot_general(Bg, dhnb, NT_DIMS, preferred_element_type=F32)
                dxd = decp * G
                sdec = _pair_sums(lane, xd * G)
                dB = dB + jnp.dot((xd * decp).astype(BF16), dhnb, preferred_element_type=F32)
                dh_ref[pi] = dhp
                for hh in range(2):
                    hmask = (lane >= SSD_HEAD_DIM) if hh else (lane < SSD_HEAD_DIM)
                    seg = _col(cs, hl0 + hh, lane) - cst_ref[hl0 + hh:hl0 + hh + 1, :]
                    Lm = jnp.exp(jnp.where(causal, seg, NEG))
                    M = CB * Lm
                    dyh = jnp.where(hmask, dy2, 0.0).astype(BF16)
                    dM = lax.dot_general(dyh, xdb, NT_DIMS, preferred_element_type=F32)
                    dxd = dxd + lax.dot_general(M.astype(BF16), dyh, TN_DIMS, preferred_element_type=F32)
                    Q = dM * M
                    dCB = dCB + dM * Lm
                    dech = _col(dec, hl0 + hh, lane)
                    dd = sdec[hh] * dech
                    end = dcd[hh] * cdh[hh] + jnp.sum(dd, axis=0, keepdims=True)
                    dcs_h = (sYo[hh] - dd + jnp.sum(Q, axis=1, keepdims=True)
                             - jnp.sum(Q.T, axis=1, keepdims=True) + jnp.where(is_last, end, 0.0))
                    dcs = jnp.where(lane == hl0 + hh, dcs_h, dcs)
                    dD = jnp.where(lane == hl0 + hh, jnp.sum(sD[hh], axis=0, keepdims=True), dD)
                sdt = _pair_sums(lane, dxd * x2)
                ddt = jnp.where(lane == hl0, sdt[0], jnp.where(lane == hl0 + 1, sdt[1], ddt))
                dxc_ref[:, pi * LANE:(pi + 1) * LANE] = dx2 + dxd * dtp
            dCBb = dCB.astype(BF16)
            dC = dC + jnp.dot(dCBb, Bg, preferred_element_type=F32)
            dB = dB + lax.dot_general(dCBb, Cg, TN_DIMS, preferred_element_type=F32)
            dxc_ref[:, SSD_B0 + g * SSD_STATE:SSD_B0 + (g + 1) * SSD_STATE] = dB
            dxc_ref[:, SSD_C0 + g * SSD_STATE:SSD_C0 + (g + 1) * SSD_STATE] = dC
        da = _tri_dot(jnp.where(c >= r, 1.0, 0.0).astype(BF16), dcs)
        ddtv = ddt + da * A
        ddt_ref[...] = jnp.where(hl, ddtv * _sigmoid(fv), 0.0)
        dal = jnp.sum(da * dtv, axis=0, keepdims=True) * A
        dp_ref[0:1, :] += dal
        dp_ref[1:2, :] += dD

    rev = lambda i: nc - 1 - i
    vec = pl.BlockSpec((1, LANE), lambda i: (0, 0))
    return pl.pallas_call(
        body, name=name, grid=(nc,),
        in_specs=[pl.BlockSpec((L, SSD_CH), lambda i: (rev(i), 0)), pl.BlockSpec((L, LANE), lambda i: (rev(i), col)),
                  vec, vec, vec,
                  pl.BlockSpec((1, SSD_PAIRS, LANE, SSD_STATE), lambda i: (rev(i), 0, 0, 0)),
                  pl.BlockSpec((L, SSD_X), lambda i: (rev(i), 0))],
        out_specs=[pl.BlockSpec((L, SSD_CH), lambda i: (rev(i), 0)), pl.BlockSpec((L, LANE), lambda i: (rev(i), 0)),
                   pl.BlockSpec((SUBLANE, LANE), lambda i: (0, 0))],
        out_shape=[jax.ShapeDtypeStruct((S, SSD_CH), F32), jax.ShapeDtypeStruct((S, LANE), F32),
                   jax.ShapeDtypeStruct((SUBLANE, LANE), F32)],
        scratch_shapes=[pltpu.VMEM((SSD_PAIRS, LANE, SSD_STATE), F32), pltpu.VMEM((LANE, L), F32)],
        compiler_params=_cparams("arbitrary"),
    )(xc, rest, dtb, alog, dskip, hs, dy)


def _gate_norm_fwd(y, rest, zcol, nw, *, name):
    S, W = y.shape
    ts = _row_tile(S)
    GW = W // SSD_GROUPS

    def body(y_ref, z_ref, nw_ref, o_ref):
        z = z_ref[...]
        t = y_ref[...] * (z * _sigmoid(z))
        for g in range(SSD_GROUPS):
            cols = slice(g * GW, (g + 1) * GW)
            tg = t[:, cols]
            rr = lax.rsqrt(jnp.mean(tg * tg, axis=-1, keepdims=True) + NORM_EPS)
            o_ref[:, cols] = (tg * rr * nw_ref[:, cols]).astype(BF16)

    row = pl.BlockSpec((ts, W), lambda i: (i, 0))
    return pl.pallas_call(
        body, name=name, grid=(S // ts,),
        in_specs=[row, pl.BlockSpec((ts, W), lambda i: (i, zcol)), pl.BlockSpec((1, W), lambda i: (0, 0))],
        out_specs=row,
        out_shape=jax.ShapeDtypeStruct((S, W), BF16),
        compiler_params=_cparams("parallel"),
    )(y, rest, nw)


def _gate_norm_bwd(y, rest, zcol, nw, do, *, name):
    S, W = y.shape
    ts = _row_tile(S)
    GW = W // SSD_GROUPS

    def body(y_ref, z_ref, nw_ref, do_ref, dy_ref, dz_ref, dnw_ref):
        z = z_ref[...]
        yv = y_ref[...]
        s, ds = _dsilu(z)
        sz = z * s
        t = yv * sz
        dov = do_ref[...]
        parts = []
        for g in range(SSD_GROUPS):
            cols = slice(g * GW, (g + 1) * GW)
            tg = t[:, cols]
            rr = lax.rsqrt(jnp.mean(tg * tg, axis=-1, keepdims=True) + NORM_EPS)
            n = tg * rr
            dog = dov[:, cols]
            dn = dog * nw_ref[:, cols]
            dt = rr * (dn - n * jnp.mean(dn * n, axis=-1, keepdims=True))
            dy_ref[:, cols] = dt * sz[:, cols]
            dz_ref[:, cols] = (dt * yv[:, cols] * ds[:, cols]).astype(BF16)
            parts.append(jnp.sum(dog * n, axis=0, keepdims=True))
        _acc_out(dnw_ref, jnp.concatenate(parts, axis=1), pl.program_id(0) == 0)

    row = pl.BlockSpec((ts, W), lambda i: (i, 0))
    vec = pl.BlockSpec((1, W), lambda i: (0, 0))
    return pl.pallas_call(
        body, name=name, grid=(S // ts,),
        in_specs=[row, pl.BlockSpec((ts, W), lambda i: (i, zcol)), vec, row],
        out_specs=[row, row, vec],
        out_shape=[jax.ShapeDtypeStruct((S, W), F32), jax.ShapeDtypeStruct((S, W), BF16),
                   jax.ShapeDtypeStruct((1, W), F32)],
        compiler_params=_cparams("arbitrary"),
    )(y, rest, nw, do)


N_DEV = 8
MESH = pl.DeviceIdType.MESH
ANY = pl.BlockSpec(memory_space=pl.ANY)


def _all_gather(xs, *, name):
    n = len(xs)

    def body(*refs):
        for phase in range(3):
            _gather_phase(phase, refs[:n], refs[n:2 * n], refs[2 * n:])

    return pl.pallas_call(
        body, name=name,
        out_shape=_a2a_out_shapes("gather", xs), in_specs=[ANY] * n, out_specs=[ANY] * n,
        scratch_shapes=_a2a_sems(n),
    )(*xs)


def _gather_phase(phase, x_refs, o_refs, sems):
    send_sems, recv_sems, local_sems = sems
    n = len(x_refs)
    px, py, pc = lax.axis_index("x"), lax.axis_index("y"), lax.axis_index("c")
    me, sibling = (px, py, pc), (px, py, 1 - pc)
    chips = [(1 - px, py), (px, 1 - py), (1 - px, 1 - py)]

    def copy(a, k, block, to, src=None):
        slot = o_refs[a].at[4 * block[0] + 2 * block[1] + block[2]]
        return pltpu.make_async_remote_copy(
            src_ref=slot if src is None else src, dst_ref=slot,
            send_sem=send_sems.at[a, k], recv_sem=recv_sems.at[a, k], device_id=to, device_id_type=MESH)

    def own(a):
        return ([copy(a, 0, me, sibling, src=x_refs[a])]
                + [copy(a, 1 + j, me, (*chip, pc), src=x_refs[a]) for j, chip in enumerate(chips)])

    mine = lambda a: pltpu.make_async_copy(x_refs[a], o_refs[a].at[4 * px + 2 * py + pc], local_sems.at[a])
    passed = lambda a, j: copy(a, 4 + j, (*chips[j], pc), sibling)
    if phase == 0:
        for a in range(n):
            mine(a).start()
            for cp in own(a):
                cp.start()
    elif phase == 1:
        for j, chip in enumerate(chips):
            for a in range(n):
                copy(a, 1 + j, (*chip, pc), me).wait_recv()
                passed(a, j).start()
    else:
        for a in range(n):
            copy(a, 0, sibling, me).wait_recv()
            for j, chip in enumerate(chips):
                copy(a, 4 + j, (*chip, 1 - pc), me).wait_recv()
            for cp in own(a) + [passed(a, j) for j in range(len(chips))]:
                cp.wait_send()
            mine(a).wait()


def _exchange(sends, *, name):
    n = len(sends)

    def body(*refs):
        _a2a_start(refs[:n], refs[n:2 * n], refs[2 * n:])
        _a2a_wait(refs[:n], refs[n:2 * n], refs[2 * n:])

    return pl.pallas_call(
        body, name=name,
        out_shape=_a2a_out_shapes("exchange", sends), in_specs=[ANY] * n, out_specs=[ANY] * n,
        scratch_shapes=_a2a_sems(n),
    )(*sends)


def _a2a_out_shapes(kind, arrays):
    if kind == "gather":
        return [jax.ShapeDtypeStruct((N_DEV,) + x.shape, x.dtype) for x in arrays]
    return [jax.ShapeDtypeStruct(x.shape, x.dtype) for x in arrays]


def _a2a_sems(n):
    return [pltpu.SemaphoreType.DMA((n, N_DEV - 1)), pltpu.SemaphoreType.DMA((n, N_DEV - 1)),
            pltpu.SemaphoreType.DMA((n,))]


def _a2a_copies(in_refs, out_refs, send_sems, recv_sems, local_sems, arrivals):
    px, py, pc = lax.axis_index("x"), lax.axis_index("y"), lax.axis_index("c")
    me = 4 * px + 2 * py + pc
    n = len(in_refs)
    src = lambda a, p: in_refs[a].at[p]
    local = [pltpu.make_async_copy(src(a, me), out_refs[a].at[me], local_sems.at[a]) for a in range(n)]
    sent, landing = [], []
    for k in range(1, N_DEV):
        qx = 1 - px if k & 4 else px
        qy = 1 - py if k & 2 else py
        qc = 1 - pc if k & 1 else pc
        peer = 4 * qx + 2 * qy + qc
        for a in range(n):
            def rdma(dst_slot):
                return pltpu.make_async_remote_copy(
                    src_ref=src(a, peer), dst_ref=out_refs[a].at[dst_slot],
                    send_sem=send_sems.at[a, k - 1], recv_sem=recv_sems.at[a, k - 1],
                    device_id=(qx, qy, qc), device_id_type=MESH)
            sent.append(rdma(me))
            if arrivals:
                landing.append(rdma(peer))
    return local, sent, landing


def _a2a_start(in_refs, out_refs, sems):
    local, sent, _ = _a2a_copies(in_refs, out_refs, *sems, arrivals=False)
    for cp in local + sent:
        cp.start()


def _a2a_wait(in_refs, out_refs, sems):
    local, sent, landing = _a2a_copies(in_refs, out_refs, *sems, arrivals=True)
    for cp in landing:
        cp.wait_recv()
    for cp in sent:
        cp.wait_send()
    for cp in local:
        cp.wait()


IN_POOL, IN_QKV, IN_F, IN_Z, IN_XBC, IN_DT, IN_GATE, IN_TOTAL = 0, 512, 2048, 2056, 3080, 4616, 4632, 7704
QKV_W = IN_F - IN_QKV
REST_Z_BLK, REST_POOL_BLK, REST_XBC0, REST_FDT_BLK = 3, 8, 4608, 48
REST_FDT0 = REST_FDT_BLK * LANE
REST_USED = REST_FDT0 + (IN_Z - IN_F) + (IN_GATE - IN_DT)
REST_W = REST_FDT0 + LANE


def _w_in_runs(shard):
    segs = [(0, IN_QKV, IN_F, 0), (1, IN_GATE, IN_TOTAL, 0), (1, IN_Z, IN_XBC, 3072), (1, IN_POOL, IN_QKV, 4096),
            (1, IN_XBC, IN_DT, REST_XBC0), (1, IN_F, IN_Z, REST_FDT0), (1, IN_DT, IN_GATE, REST_FDT0 + IN_Z - IN_F)]
    runs = []
    for dst, t0, t1, d0 in segs:
        for j in range(N_DEV):
            lo, hi = max(t0, shard * j), min(t1, shard * (j + 1))
            if lo < hi:
                runs.append((dst, d0 + lo - t0, j, lo - shard * j, hi - lo))
    return runs


def _repack_w_in(g, shard, *, name):
    _, L, K, P = g.shape
    tr = 256
    runs = _w_in_runs(shard)

    def body(g_ref, q_ref, r_ref):
        r_ref[:, REST_USED:] = jnp.zeros((tr, REST_W - REST_USED), g_ref.dtype)
        for dst, b, j, a, n in runs:
            (q_ref, r_ref)[dst][:, b:b + n] = g_ref[j, :, a:a + n]

    return pl.pallas_call(
        body, name=name, grid=(L, K // tr),
        in_specs=[pl.BlockSpec((N_DEV, None, tr, P), lambda l, i: (0, l, i, 0))],
        out_specs=[pl.BlockSpec((None, tr, QKV_W), lambda l, i: (l, i, 0)),
                   pl.BlockSpec((None, tr, REST_W), lambda l, i: (l, i, 0))],
        out_shape=[jax.ShapeDtypeStruct((L, K, QKV_W), g.dtype), jax.ShapeDtypeStruct((L, K, REST_W), g.dtype)],
        compiler_params=_cparams("parallel", "parallel"),
    )(g)


def _held_index(l, li, i, n):
    return jnp.where(l == li, i, jnp.where(l > li, n - 1, 0))


def _repack_dw_in(dwq, dwr, shard, *, name):
    L = len(dwq)
    K = dwq[0].shape[0]
    tr = 128
    nt = K // tr
    runs = _w_in_runs(shard)

    def body(*refs):
        srcs, o_ref = refs[:2 * L], refs[2 * L]
        l = pl.program_id(0)
        for li in range(L):
            @pl.when(l == li)
            def _():
                for dst, b, j, a, n in runs:
                    o_ref[j, :, a:a + n] = srcs[2 * li + dst][:, b:b + n]

    in_specs = []
    for li in range(L):
        hold = functools.partial(lambda li, l, i: (_held_index(l, li, i, nt), 0), li)
        in_specs += [pl.BlockSpec((tr, QKV_W), hold), pl.BlockSpec((tr, REST_W), hold)]
    args = [x for pair in zip(dwq, dwr) for x in pair]
    return pl.pallas_call(
        body, name=name, grid=(L, nt),
        in_specs=in_specs,
        out_specs=pl.BlockSpec((N_DEV, None, tr, shard), lambda l, i: (0, l, i, 0)),
        out_shape=jax.ShapeDtypeStruct((N_DEV, L, K, shard), dwq[0].dtype),
        compiler_params=_cparams("arbitrary", "arbitrary"),
    )(*args)


FLAT_W = 1024
ADAM_ROWS = 128


def _adam_math(w, g, m, v):
    m2 = ADAM_B1 * m + (1.0 - ADAM_B1) * g
    v2 = ADAM_B2 * v + (1.0 - ADAM_B2) * (g * g)
    m_hat = m2 / (1.0 - ADAM_B1 ** ADAM_STEP)
    v_hat = v2 / (1.0 - ADAM_B2 ** ADAM_STEP)
    delta = -ADAM_LR * (m_hat / (jnp.sqrt(v_hat) + ADAM_EPS) + ADAM_WD * w)
    return delta, m2, v2


def _row_div(R, align):
    best = R
    for t in range(align, min(R, ADAM_ROWS) + 1, align):
        if R % t == 0:
            best = t
    return best


def _sum8(ref_of):
    g = ref_of(0).astype(F32)
    for s in range(1, N_DEV):
        g = g + ref_of(s).astype(F32)
    return g


def _sum_adam(recvs, w, m, v, *, name):
    L, R, C = w.shape
    single = not isinstance(recvs, (list, tuple))
    recvs = [recvs] if single else list(recvs)
    tr = _row_div(R, SUBLANE * (4 // recvs[0].dtype.itemsize))
    nt = R // tr
    nr = len(recvs)

    def body(*refs):
        r_refs = refs[:nr]
        w_ref, m_ref, v_ref, g_ref, d_ref, m2_ref, v2_ref = refs[nr:]

        def run(ref_of):
            g = _sum8(ref_of)
            delta, m2, v2 = _adam_math(w_ref[...], g, m_ref[...], v_ref[...])
            g_ref[...] = g
            d_ref[...] = delta
            m2_ref[...] = m2
            v2_ref[...] = v2

        if single:
            run(lambda s: r_refs[0][s])
        else:
            l = pl.program_id(0)
            for li in range(L):
                @pl.when(l == li)
                def _():
                    run(lambda s: r_refs[li][s])

    if single:
        r_specs = [pl.BlockSpec((N_DEV, None, tr, C), lambda l, i: (0, l, i, 0))]
    else:
        r_specs = [pl.BlockSpec((N_DEV, tr, C), functools.partial(lambda li, l, i: (0, _held_index(l, li, i, nt), 0), li))
                   for li in range(L)]
    row = pl.BlockSpec((None, tr, C), lambda l, i: (l, i, 0))
    return pl.pallas_call(
        body, name=name, grid=(L, nt),
        in_specs=r_specs + [row, row, row],
        out_specs=[row] * 4,
        out_shape=[jax.ShapeDtypeStruct((L, R, C), F32)] * 4,
        compiler_params=_cparams("arbitrary", "arbitrary"),
    )(*recvs, w, m, v)


def _sum_parts(parts, *, name):
    _, R, C = parts.shape

    def body(p_ref, o_ref):
        o_ref[...] = _sum8(lambda s: p_ref[s])

    return pl.pallas_call(
        body, name=name, out_shape=jax.ShapeDtypeStruct((R, C), F32),
        in_specs=[pl.BlockSpec(memory_space=pltpu.VMEM)], out_specs=pl.BlockSpec(memory_space=pltpu.VMEM),
    )(parts)


def _adam(g, w, m, v, *, name):
    def body(g_ref, w_ref, m_ref, v_ref, d_ref, m2_ref, v2_ref):
        delta, m2, v2 = _adam_math(w_ref[...], g_ref[...], m_ref[...], v_ref[...])
        d_ref[...] = delta
        m2_ref[...] = m2
        v2_ref[...] = v2

    whole = pl.BlockSpec(memory_space=pltpu.VMEM)
    return pl.pallas_call(
        body, name=name, in_specs=[whole] * 4, out_specs=[whole] * 3,
        out_shape=[jax.ShapeDtypeStruct(g.shape, F32)] * 3,
    )(g, w, m, v)


SHARDED = ("w_in", "ffn_up", "p_pool", "p_attn", "p_ssd", "w_out", "ffn_down", "ssd_conv_w", "ffn_conv_w")
REPLICATED = ("norm_mix", "pool_mix", "pool_scale", "f_bias", "ssd_conv_b", "ssd_dt_bias", "ssd_a_log", "ssd_d",
              "ssd_norm", "norm_ffn", "ffn_conv_b", "norm_final")
WEIGHTS = ("norm_mix", "w_in", "pool_mix", "pool_scale", "f_bias", "ssd_conv_w", "ssd_conv_b", "ssd_dt_bias",
           "ssd_a_log", "ssd_d", "ssd_norm", "p_pool", "p_attn", "p_ssd", "w_out", "norm_ffn", "ffn_up",
           "ffn_conv_w", "ffn_conv_b", "ffn_down", "norm_final")


def _round_up(n, k):
    return -(-n // k) * k


def _pad_last(x, width):
    return jnp.pad(x, [(0, 0)] * (x.ndim - 1) + [(0, width - x.shape[-1])])


def _flat_rows(parts, rows):
    flat = jnp.concatenate([p.reshape(-1) for p in parts])
    return jnp.pad(flat, (0, rows * FLAT_W - flat.shape[0])).reshape(rows, FLAT_W)


def _lanes(v, lane0):
    return jnp.pad(v, (lane0, LANE - lane0 - v.shape[0]))[None]


def _cols_by_device(g):
    _, L, R, c = g.shape
    return jnp.moveaxis(g, 0, 2).reshape(L, R, N_DEV * c)


def _rows_by_device(g):
    _, L, r, C = g.shape
    return jnp.moveaxis(g, 0, 1).reshape(L, N_DEV * r, C)


def _gather_payload(W, l, shard_pad, up_pad):
    bf = lambda n: W[n][l].astype(BF16)
    return [_pad_last(bf("w_in"), shard_pad), _pad_last(bf("ffn_up"), up_pad), bf("p_pool"), bf("p_attn"),
            bf("p_ssd"), bf("w_out"), bf("ffn_down"), W["ssd_conv_w"][l], W["ffn_conv_w"][l]]


def _proj_params(g_win, W, l, shard):
    w_qkv, w_rest = _repack_w_in(g_win[:, None], shard, name=f"l{l}_repack_w_in")
    return dict(nm=W["norm_mix"][l][None], w_qkv=w_qkv[0], w_rest=w_rest[0], fb=_lanes(W["f_bias"][l], 0), shard=shard)


def _mixer_params(gathered, W, l, up_w, up_pad):
    g_up, g_pp, g_pa, g_ps, g_wo, g_dn, g_cw, g_fcw = [g[:, None] for g in gathered]
    half = N_DEV // 2
    w_down = _rows_by_device(g_dn)[0]
    w_down_p = jnp.pad(w_down.reshape(half, up_w, -1), ((0, 0), (0, up_pad - up_w), (0, 0))).reshape(half * up_pad, -1)
    fcb_p = _pad_last(W["ffn_conv_b"][l].reshape(N_DEV, up_w), up_pad).reshape(1, N_DEV * up_pad)
    return dict(
        mix=W["pool_mix"][l].astype(BF16), pscale=W["pool_scale"][l][None], cw=_cols_by_device(g_cw)[0],
        cb=W["ssd_conv_b"][l][None], dtb=_lanes(W["ssd_dt_bias"][l], DT_LANE0),
        alog=_lanes(W["ssd_a_log"][l], DT_LANE0), dsk=_lanes(W["ssd_d"][l], DT_LANE0), snw=W["ssd_norm"][l][None],
        p_pool=_cols_by_device(g_pp)[0], p_attn=_cols_by_device(g_pa)[0], p_ssd=_rows_by_device(g_ps)[0],
        w_out=_rows_by_device(g_wo)[0], nf=W["norm_ffn"][l][None], g_up=g_up, up_blocks=(N_DEV, up_w, up_pad),
        fcw=_cols_by_device(_pad_last(g_fcw, up_pad))[0], fcb=fcb_p, w_down=w_down_p)


def _layer_fwd(x, W, l, g_win, g_mixer, to_gather, dims):
    nm = lambda s: f"l{l}_{s}"
    P = _proj_params(g_win, W, l, dims[0])
    u = _rms_fwd(x, P["nm"], name=nm("rms_mix"))
    qkv = _mm(u, P["w_qkv"], out_dtype=BF16, name=nm("proj_qkv"))
    rest = _mm(u, P["w_rest"], name=nm("proj_rest"))
    ccol = _fox_pre(rest, REST_FDT_BLK, P["fb"], name=nm("fox_pre"))
    o, lrow, got = _attn_fwd(qkv, ccol, comm=("gather", to_gather) if to_gather else None, name=nm("attn"))
    if g_mixer is None:
        g_mixer, got = got[:len(SHARDED) - 1], got[len(SHARDED) - 1:]
    P.update(_mixer_params(g_mixer, W, l, *dims[1:]))
    d, ya0 = _pool_fwd(rest, REST_POOL_BLK, P["mix"], P["pscale"], name=nm("pool"))
    ya = _mm(ya0, P["p_pool"], name=nm("p_pool"))
    yb = _mm(o, P["p_attn"], name=nm("p_attn"))
    xc = _conv_silu_fwd(rest, REST_XBC0, SSD_CH, P["cw"], P["cb"], name=nm("ssd_conv"))
    y, hs = _ssd_fwd(xc, rest, REST_FDT_BLK, P["dtb"], P["alog"], P["dsk"], name=nm("ssd"))
    yc0 = _gate_norm_fwd(y, rest, REST_Z_BLK, P["snw"], name=nm("ssd_norm"))
    yc = _mm(yc0, P["p_ssd"], name=nm("p_ssd"))
    merged = _merge_fwd(rest, ya, yb, yc, name=nm("merge"))
    x1 = _mm(merged, P["w_out"], add=x, name=nm("w_out"))
    u2 = _rms_fwd(x1, P["nf"], name=nm("rms_ffn"))
    h0 = _mm_up_fwd(u2, P["g_up"], 0, name=nm("ffn_up"))
    a = _ffn_act_fwd(h0, P["fcw"], P["fcb"], name=nm("ffn_act"))
    x2 = _mm(a, P["w_down"], add=x1, name=nm("ffn_down"))
    saved = dict(x=x, u=u, qkv=qkv, rest=rest, d=d, ya0=ya0, ya=ya, ccol=ccol, o=o, lrow=lrow, yb=yb,
                 xc=xc, y=y, hs=hs, yc0=yc0, yc=yc, merged=merged, x1=x1, u2=u2, h0=h0, a=a)
    return x2, saved, P, got


EARLY_GRADS = ("ffn_up", "ffn_down", "ffn_conv_w", "p_pool", "p_attn", "p_ssd", "w_out")
LATE_GRADS = ("w_in", "ssd_conv_w")


def _layer_bwd(g, gb, sv, P, l):
    nm = lambda s: f"l{l}_{s}_bwd"
    rest = sv["rest"]
    G = {}
    up_blocks, up_w, up_pad = P["up_blocks"]
    half = up_blocks // 2
    da = _mm(gb, P["w_down"], tb=True, name=nm("ffn_down_dx"))
    dwd = _mm(sv["a"], gb, ta=True, out_dtype=GRAD_WIRE, name=nm("ffn_down_dw"))
    G["ffn_down"] = dwd.reshape(half, up_pad, -1)[:, :up_w].reshape(N_DEV, -1, dwd.shape[1])
    dh, dfcw, dfcb = _ffn_act_bwd(sv["h0"], P["fcw"], P["fcb"], da, name=nm("ffn_act"))
    du2 = _mm_up_dx(dh, P["g_up"], 0, name=nm("ffn_up_dx"))
    G["ffn_up"] = _mm_up_dw(sv["u2"], dh, up_blocks, up_w, name=nm("ffn_up_dw"))
    taps = dfcw.shape[1]
    G["ffn_conv_w"] = jnp.moveaxis(dfcw.reshape(2, taps, half, up_pad)[..., :up_w], 2, 1).reshape(N_DEV, taps, up_w)
    G["ffn_conv_b"] = dfcb.reshape(2, half, up_pad)[..., :up_w].reshape(-1)
    dx1, dx1b, dnf = _rms_bwd(sv["x1"], P["nf"], du2, g, name=nm("rms_ffn"))
    G["norm_ffn"] = dnf[0]
    dmerged = _mm(dx1b, P["w_out"], tb=True, name=nm("w_out_dx"))
    dwo = _mm(sv["merged"], dx1b, ta=True, out_dtype=GRAD_WIRE, name=nm("w_out_dw"))
    G["w_out"] = dwo.reshape(N_DEV, -1, dwo.shape[1])
    dgl, dya, dyb, dyc = _merge_bwd(rest, sv["ya"], sv["yb"], sv["yc"], dmerged, name=nm("merge"))
    G["p_pool"] = _mm_tn_blocks(sv["ya0"], dya, N_DEV, name=nm("p_pool_dw"))
    dya0 = _mm(dya, P["p_pool"], tb=True, name=nm("p_pool_dx"))
    G["p_attn"] = _mm_tn_blocks(sv["o"], dyb, N_DEV, name=nm("p_attn_dw"))
    do = _mm(dyb, P["p_attn"], tb=True, out_dtype=BF16, name=nm("p_attn_dx"))
    dps = _mm(sv["yc0"], dyc, ta=True, out_dtype=GRAD_WIRE, name=nm("p_ssd_dw"))
    G["p_ssd"] = dps.reshape(N_DEV, -1, dps.shape[1])
    dyc0 = _mm(dyc, P["p_ssd"], tb=True, name=nm("p_ssd_dx"))
    dpv, dmix, dpsc = _pool_bwd(dya0, sv["d"], P["mix"], P["pscale"], name=nm("pool"))
    G["pool_mix"], G["pool_scale"] = dmix, dpsc[0]
    dq, dk, dv, dcq, dck, got = _attn_bwd(sv["qkv"], sv["o"], do, sv["lrow"], sv["ccol"],
                                          comm=("exchange", [G[n] for n in EARLY_GRADS]), name=nm("attn"))
    received = {(n, l): r for n, r in zip(EARLY_GRADS, got)}
    dy, dz, dsnw = _gate_norm_bwd(sv["y"], rest, REST_Z_BLK, P["snw"], dyc0, name=nm("ssd_norm"))
    G["ssd_norm"] = dsnw[0]
    dxc, ddt, dpar = _ssd_bwd(sv["xc"], rest, REST_FDT_BLK, P["dtb"], P["alog"], P["dsk"], sv["hs"], dy, name=nm("ssd"))
    G["ssd_a_log"] = dpar[0, DT_LANE0:DT_LANE0 + SSD_HEADS]
    G["ssd_d"] = dpar[1, DT_LANE0:DT_LANE0 + SSD_HEADS]
    dxbc, dcw, dcb = _conv_silu_bwd(rest, REST_XBC0, SSD_CH, P["cw"], P["cb"], dxc, name=nm("ssd_conv"))
    G["ssd_conv_w"] = jnp.moveaxis(dcw.reshape(dcw.shape[0], N_DEV, -1), 1, 0)
    G["ssd_conv_b"] = dcb[0]
    dfdt, dfdtb = _fox_post(dcq, dck, rest, REST_FDT_BLK, P["fb"], ddt, name=nm("fox_post"))
    G["f_bias"] = dfdtb[0, :ATTN_HEADS]
    G["ssd_dt_bias"] = dfdtb[0, DT_LANE0:DT_LANE0 + SSD_HEADS]
    dqkv = jnp.concatenate([dq, dk, dv], axis=1)
    drest = jnp.concatenate([dgl, dz, dpv, dxbc, dfdt], axis=1)
    dwq = _mm(sv["u"], dqkv, ta=True, out_dtype=GRAD_WIRE, name=nm("proj_qkv_dw"))
    dwr = _mm(sv["u"], drest, ta=True, out_dtype=GRAD_WIRE, name=nm("proj_rest_dw"))
    G["w_in"] = _repack_dw_in([dwq], [dwr], P["shard"], name=nm("repack_dw_in"))[:, 0]
    du = _mm(dqkv, P["w_qkv"], tb=True, name=nm("proj_qkv_dx"))
    du, got = _mm(drest, P["w_rest"], tb=True, add=du, comm=("exchange", [G[n] for n in LATE_GRADS]),
                  name=nm("proj_rest_dx"))
    received.update({(n, l): r for n, r in zip(LATE_GRADS, got)})
    dx, dxb, dnm = _rms_bwd(sv["x"], P["nm"], du, dx1, name=nm("rms_mix"))
    G["norm_mix"] = dnm[0]
    return dx, dxb, G, received


def kernel(x, norm_mix, w_in, pool_mix, pool_scale, f_bias, ssd_conv_w, ssd_conv_b, ssd_dt_bias, ssd_a_log, ssd_d, ssd_norm, p_pool, p_attn, p_ssd, w_out, norm_ffn, ffn_up, ffn_conv_w, ffn_conv_b, ffn_down, norm_final, loss_target, m_norm_mix, m_w_in, m_pool_mix, m_pool_scale, m_f_bias, m_ssd_conv_w, m_ssd_conv_b, m_ssd_dt_bias, m_ssd_a_log, m_ssd_d, m_ssd_norm, m_p_pool, m_p_attn, m_p_ssd, m_w_out, m_norm_ffn, m_ffn_up, m_ffn_conv_w, m_ffn_conv_b, m_ffn_down, m_norm_final, v_norm_mix, v_w_in, v_pool_mix, v_pool_scale, v_f_bias, v_ssd_conv_w, v_ssd_conv_b, v_ssd_dt_bias, v_ssd_a_log, v_ssd_d, v_ssd_norm, v_p_pool, v_p_attn, v_p_ssd, v_w_out, v_norm_ffn, v_ffn_up, v_ffn_conv_w, v_ffn_conv_b, v_ffn_down, v_norm_final):
    args = dict(locals())
    W = {n: args[n] for n in WEIGHTS}
    M = {n: args["m_" + n] for n in WEIGHTS}
    V = {n: args["v_" + n] for n in WEIGHTS}
    L = norm_mix.shape[0]
    shard = w_in.shape[-1]
    up_w = ffn_up.shape[-1]
    up_pad = _round_up(up_w, LANE)

    payload = lambda l: _gather_payload(W, l, _round_up(shard, LANE), up_pad)
    first = payload(0)
    g_win = _all_gather(first[:1], name="gather_w_in_l0")[0]
    g_mixer, to_gather = None, first[1:]
    h, saves, params = x[0], [], []
    for l in range(L):
        to_gather = to_gather + (payload(l + 1) if l + 1 < L else [])
        h, sv, P, got = _layer_fwd(h, W, l, g_win, g_mixer, to_gather, (shard, up_w, up_pad))
        saves.append(sv)
        params.append(P)
        if l + 1 < L:
            g_win, g_mixer, to_gather = got[0], got[1:], []
    loss_part, g, gb, dnfin = _loss_head(h, norm_final[None], loss_target[0], name="loss_head")

    G, received = [None] * L, {}
    for l in reversed(range(L)):
        g, gb, G[l], got = _layer_bwd(g, gb, saves[l], params[l], l)
        received.update(got)

    rep_grads = {n: jnp.stack([G[l][n] for l in range(L)]) for n in REPLICATED if n != "norm_final"}
    rep_grads["norm_final"] = dnfin[0]
    rep_size = sum(W[n].size for n in REPLICATED)
    rep_rows = _round_up(-(-rep_size // (N_DEV * FLAT_W)), SUBLANE)
    rep_send = _flat_rows([rep_grads[n] for n in REPLICATED], N_DEV * rep_rows).reshape(N_DEV, rep_rows, FLAT_W)
    last = _exchange([rep_send], name="exchange_replicated_grads")

    out = {}
    for n in SHARDED:
        shape = W[n].shape
        as3 = lambda a: a.reshape(L, -1, shape[-1])
        res = _sum_adam([received[(n, l)] for l in range(L)], as3(W[n]), as3(M[n]), as3(V[n]), name="sum_adam_" + n)
        for key, arr in zip(("grad_", "delta_", "new_m_", "new_v_"), res):
            out[key + n] = arr.reshape(shape)

    rep_sum = _sum_parts(last[-1], name="sum_replicated")
    rep_g = _all_gather([rep_sum], name="gather_replicated_grads")[0].reshape(N_DEV * rep_rows, FLAT_W)
    flat_rep = lambda D: _flat_rows([D[n] for n in REPLICATED], N_DEV * rep_rows)
    d_rp, m_rp, v_rp = _adam(rep_g, flat_rep(W), flat_rep(M), flat_rep(V), name="adam_replicated")
    off = 0
    for n in REPLICATED:
        size = W[n].size
        for key, arr in (("grad_", rep_g), ("delta_", d_rp), ("new_m_", m_rp), ("new_v_", v_rp)):
            out[key + n] = arr.reshape(-1)[off:off + size].reshape(W[n].shape)
        off += size

    loss = lax.psum(loss_part[0, 0], ("x", "y", "c"))
    return (loss, g[None], *[out["grad_" + n] for n in WEIGHTS], *[out["delta_" + n] for n in WEIGHTS],
            *[out["new_m_" + n] for n in WEIGHTS], *[out["new_v_" + n] for n in WEIGHTS])
```

```python
import functools

import jax
import jax.numpy as jnp
from jax import lax
from jax.experimental import pallas as pl
from jax.experimental.pallas import tpu as pltpu

F32 = jnp.float32
BF16 = jnp.bfloat16

LANE = 128
SUBLANE = 8
VMEM_LIMIT = 56 * 1024 * 1024

NORM_EPS = 1e-6
POOL_WINDOWS = (2, 4, 8, 16)
POOL_GROUP_DIM = 128
ATTN_HEADS = 8
ATTN_HEAD_DIM = 64
SSD_HEAD_DIM = 64
SSD_HEADS = 16
SSD_GROUPS = 2
SSD_STATE = 128
SSD_CHUNK = 128

ADAM_LR = 0.001
ADAM_B1 = 0.9
ADAM_B2 = 0.999
ADAM_EPS = 1e-08
ADAM_WD = 0.01
ADAM_STEP = 10


def _cparams(*sem):
    return pltpu.CompilerParams(dimension_semantics=tuple(sem), vmem_limit_bytes=VMEM_LIMIT)


def _tile(n, pref):
    if n <= pref:
        return n
    assert n % LANE == 0, n
    q = n // LANE
    best = 1
    for d in range(1, pref // LANE + 1):
        if q % d == 0:
            best = d
    return best * LANE


GRAD_WIRE = BF16


def _mm_call(args, in_specs, out_spec, out_shape, grid, *, ta, tb, nk, acc_shape, has_add, keep, name, comm=None):
    dims = (((0 if ta else 1,), (1 if tb else 0,)), ((), ()))
    nc = len(comm[1]) if comm else 0
    n_in = 2 + has_add

    def body(*refs):
        a_ref, b_ref = refs[0], refs[1]
        o_ref = refs[n_in + nc]
        acc_ref = refs[n_in + 2 * nc + 1] if nk > 1 else None
        carried = (refs[n_in:n_in + nc], refs[n_in + nc + 1:n_in + 2 * nc + 1], refs[n_in + 2 * nc + 1 + (nk > 1):])
        i, j, k = pl.program_id(0), pl.program_id(1), pl.program_id(2)
        if comm:
            _carry_start(comm[0], *carried, jnp.logical_and(jnp.logical_and(i == 0, j == 0), k == 0))
        d = lax.dot_general(a_ref[...], b_ref[...], dims, preferred_element_type=F32)

        def finish(r):
            if has_add:
                r = r + refs[2][...]
            if keep is not None:
                r = r[:, :keep]
            o_ref[...] = r.astype(o_ref.dtype)

        if nk == 1:
            finish(d)
        else:
            @pl.when(k == 0)
            def _():
                acc_ref[...] = d

            @pl.when(jnp.logical_and(k > 0, k < nk - 1))
            def _():
                acc_ref[...] += d

            @pl.when(k == nk - 1)
            def _():
                finish(acc_ref[...] + d)

        if comm:
            _carry_wait(comm[0], *carried, jnp.logical_and(jnp.logical_and(i == grid[0] - 1, j == grid[1] - 1),
                                                           k == nk - 1))

    res = pl.pallas_call(
        body, name=name, grid=grid, in_specs=in_specs + [ANY] * nc, out_specs=[out_spec] + [ANY] * nc,
        out_shape=[out_shape] + (_a2a_out_shapes(*comm) if comm else []),
        scratch_shapes=([pltpu.VMEM(acc_shape, F32)] if nk > 1 else []) + (_a2a_sems(nc) if comm else []),
        compiler_params=_cparams(*(("arbitrary",) * 3 if comm else ("parallel", "parallel", "arbitrary"))),
    )(*args, *(comm[1] if comm else []))
    return (res[0], list(res[1:])) if comm else res[0]


MM_VMEM_BUDGET = 40 * 1024 * 1024
MM_MAX_ROWS = 2048


def _mm_rows(M, tn, tk, nk, out_bytes, has_add):
    best = None
    for tm in range(LANE, min(M, MM_MAX_ROWS) + 1, LANE):
        if M % tm:
            continue
        blocks = 2 * (tm * tk * 2 + tk * tn * 2 + tm * tn * out_bytes + (tm * tn * 4 if has_add else 0))
        blocks += tm * tn * 4 * (2 if nk > 1 else 1)
        if blocks <= MM_VMEM_BUDGET:
            best = tm
    return M if best is None else best


def _mm(a, b, *, ta=False, tb=False, add=None, out_dtype=F32, comm=None, name):
    assert a.dtype == BF16 and b.dtype == BF16, (a.dtype, b.dtype)
    if ta:
        K, M = a.shape
    else:
        M, K = a.shape
    if tb:
        N, K2 = b.shape
    else:
        K2, N = b.shape
    assert K == K2, (a.shape, b.shape, ta, tb)
    tn, tk = _tile(N, 1024), _tile(K, 2048)
    tm = _mm_rows(M, tn, tk, K // tk, jnp.dtype(out_dtype).itemsize, add is not None)
    a_spec =pl.BlockSpec((tk, tm), lambda i, j, k: (k, i)) if ta else pl.BlockSpec((tm, tk), lambda i, j, k: (i, k))
    b_spec = pl.BlockSpec((tn, tk), lambda i, j, k: (j, k)) if tb else pl.BlockSpec((tk, tn), lambda i, j, k: (k, j))
    in_specs, args = [a_spec, b_spec], [a, b]
    if add is not None:
        in_specs.append(pl.BlockSpec((tm, tn), lambda i, j, k: (i, j)))
        args.append(add)
    return _mm_call(args, in_specs, pl.BlockSpec((tm, tn), lambda i, j, k: (i, j)),
                    jax.ShapeDtypeStruct((M, N), out_dtype), (M // tm, N // tn, K // tk),
                    ta=ta, tb=tb, nk=K // tk, acc_shape=(tm, tn), has_add=add is not None, keep=None, name=name,
                    comm=comm)


def _half_of(b, half):
    hi = jnp.where(b >= half, 1, 0)
    return hi, b - half * hi


def _mm_up_fwd(u, g_up, l, *, name):
    S, K = u.shape
    nblk, _, _, bw = g_up.shape
    tm = _mm_rows(S, bw, K, 1, 4, False)
    return _mm_call([u, g_up],
                    [pl.BlockSpec((tm, K), lambda i, j, k: (i, 0)),
                     pl.BlockSpec((None, None, K, bw), lambda i, j, k: (j, l, 0, 0))],
                    pl.BlockSpec((tm, bw), lambda i, j, k: (i, j)),
                    jax.ShapeDtypeStruct((S, nblk * bw), F32), (S // tm, nblk, 1),
                    ta=False, tb=False, nk=1, acc_shape=(tm, bw), has_add=False, keep=None, name=name)


def _mm_up_dx(dh, g_up, l, *, name):
    _, S, _ = dh.shape
    nblk, _, N, bw = g_up.shape
    half = nblk // 2
    tm = _mm_rows(S, N, bw, nblk, 4, False)
    return _mm_call([dh, g_up],
                    [pl.BlockSpec((None, tm, bw), lambda i, j, k: (_half_of(k, half)[0], i, _half_of(k, half)[1])),
                     pl.BlockSpec((None, None, N, bw), lambda i, j, k: (k, l, 0, 0))],
                    pl.BlockSpec((tm, N), lambda i, j, k: (i, 0)),
                    jax.ShapeDtypeStruct((S, N), F32), (S // tm, 1, nblk),
                    ta=False, tb=True, nk=nblk, acc_shape=(tm, N), has_add=False, keep=None, name=name)


def _mm_up_dw(u, dh, nblk, keep, *, name):
    S, M = u.shape
    half = nblk // 2
    bw = dh.shape[2] // half
    tk = _tile(S, 2048)
    tm = _mm_rows(M, bw, tk, S // tk, jnp.dtype(GRAD_WIRE).itemsize, False)
    return _mm_call([u, dh],
                    [pl.BlockSpec((tk, tm), lambda i, j, k: (k, i)),
                     pl.BlockSpec((None, tk, bw), lambda i, j, k: (_half_of(j, half)[0], k, _half_of(j, half)[1]))],
                    pl.BlockSpec((None, tm, keep), lambda i, j, k: (j, i, 0)),
                    jax.ShapeDtypeStruct((nblk, M, keep), GRAD_WIRE), (M // tm, nblk, S // tk),
                    ta=True, tb=False, nk=S // tk, acc_shape=(tm, bw), has_add=False, keep=keep, name=name)


ROW_TILE = 512
HALO = 8
POOL_HALO = 16


def _row_tile(S):
    return min(ROW_TILE, S)


def _acc_out(ref, val, first):
    @pl.when(first)
    def _():
        ref[...] = val

    @pl.when(jnp.logical_not(first))
    def _():
        ref[...] += val


def _rms_fwd(x, w, *, name):
    S, D = x.shape
    ts = _row_tile(S)

    def body(x_ref, w_ref, u_ref):
        xv = x_ref[...]
        r = lax.rsqrt(jnp.mean(xv * xv, axis=-1, keepdims=True) + NORM_EPS)
        u_ref[...] = (xv * r * w_ref[...]).astype(BF16)

    return pl.pallas_call(
        body, name=name, grid=(S // ts,),
        in_specs=[pl.BlockSpec((ts, D), lambda i: (i, 0)), pl.BlockSpec((1, D), lambda i: (0, 0))],
        out_specs=pl.BlockSpec((ts, D), lambda i: (i, 0)),
        out_shape=jax.ShapeDtypeStruct((S, D), BF16),
        compiler_params=_cparams("parallel"),
    )(x, w)


def _rms_bwd(x, w, du, g, *, name):
    S, D = x.shape
    ts = _row_tile(S)

    def body(x_ref, w_ref, du_ref, g_ref, dx_ref, dxb_ref, dw_ref):
        xv = x_ref[...]
        r = lax.rsqrt(jnp.mean(xv * xv, axis=-1, keepdims=True) + NORM_EPS)
        n = xv * r
        duv = du_ref[...]
        dn = duv * w_ref[...]
        dx = g_ref[...] + r * (dn - n * jnp.mean(dn * n, axis=-1, keepdims=True))
        dx_ref[...] = dx
        dxb_ref[...] = dx.astype(BF16)
        _acc_out(dw_ref, jnp.sum(duv * n, axis=0, keepdims=True), pl.program_id(0) == 0)

    row = pl.BlockSpec((ts, D), lambda i: (i, 0))
    vec = pl.BlockSpec((1, D), lambda i: (0, 0))
    return pl.pallas_call(
        body, name=name, grid=(S // ts,),
        in_specs=[row, vec, row, row],
        out_specs=[row, row, vec],
        out_shape=[jax.ShapeDtypeStruct((S, D), F32), jax.ShapeDtypeStruct((S, D), BF16),
                   jax.ShapeDtypeStruct((1, D), F32)],
        compiler_params=_cparams("arbitrary"),
    )(x, w, du, g)


def _loss_head(x, w, target, *, name):
    S, D = x.shape
    ts = _row_tile(S)

    def body(x_ref, w_ref, t_ref, loss_ref, dx_ref, dxb_ref, dw_ref):
        xv = x_ref[...]
        wv = w_ref[...]
        r = lax.rsqrt(jnp.mean(xv * xv, axis=-1, keepdims=True) + NORM_EPS)
        n = xv * r
        err = n * wv - t_ref[...]
        part = 0.5 * jnp.sum(jnp.mean(err * err, axis=-1, keepdims=True), axis=0, keepdims=True)
        dy = err * (1.0 / D)
        dn = dy * wv
        dx = r * (dn - n * jnp.mean(dn * n, axis=-1, keepdims=True))
        dx_ref[...] = dx
        dxb_ref[...] = dx.astype(BF16)
        first = pl.program_id(0) == 0
        _acc_out(dw_ref, jnp.sum(dy * n, axis=0, keepdims=True), first)
        _acc_out(loss_ref, jnp.broadcast_to(part, loss_ref.shape), first)

    row = pl.BlockSpec((ts, D), lambda i: (i, 0))
    vec = pl.BlockSpec((1, D), lambda i: (0, 0))
    return pl.pallas_call(
        body, name=name, grid=(S // ts,),
        in_specs=[row, vec, row],
        out_specs=[pl.BlockSpec((SUBLANE, LANE), lambda i: (0, 0)), row, row, vec],
        out_shape=[jax.ShapeDtypeStruct((SUBLANE, LANE), F32), jax.ShapeDtypeStruct((S, D), F32),
                   jax.ShapeDtypeStruct((S, D), BF16), jax.ShapeDtypeStruct((1, D), F32)],
        compiler_params=_cparams("arbitrary"),
    )(x, w, target)


def _sigmoid(x):
    return 1.0 / (1.0 + jnp.exp(-x))


def _merge_fwd(gl, ya, yb, yc, *, name):
    S, D = ya.shape
    ts = _row_tile(S)

    def body(ga_ref, gb_ref, gc_ref, ya_ref, yb_ref, yc_ref, o_ref):
        m = (_sigmoid(ga_ref[...]) * ya_ref[...] + _sigmoid(gb_ref[...]) * yb_ref[...]
             + _sigmoid(gc_ref[...]) * yc_ref[...])
        o_ref[...] = m.astype(BF16)

    row = pl.BlockSpec((ts, D), lambda i: (i, 0))
    gspec = [pl.BlockSpec((ts, D), functools.partial(lambda b, i: (i, b), b)) for b in range(3)]
    return pl.pallas_call(
        body, name=name, grid=(S // ts,),
        in_specs=gspec + [row, row, row],
        out_specs=row,
        out_shape=jax.ShapeDtypeStruct((S, D), BF16),
        compiler_params=_cparams("parallel"),
    )(gl, gl, gl, ya, yb, yc)


def _merge_bwd(gl, ya, yb, yc, dm, *, name):
    S, D = ya.shape
    ts = _row_tile(S)

    def body(ga_ref, gb_ref, gc_ref, ya_ref, yb_ref, yc_ref, dm_ref, dgl_ref, da_ref, db_ref, dc_ref):
        dmv = dm_ref[...]
        for b, (g_ref, y_ref, dy_ref) in enumerate(((ga_ref, ya_ref, da_ref), (gb_ref, yb_ref, db_ref),
                                                     (gc_ref, yc_ref, dc_ref))):
            s = _sigmoid(g_ref[...])
            dy_ref[...] = (dmv * s).astype(BF16)
            dgl_ref[:, b * D:(b + 1) * D] = (dmv * y_ref[...] * s * (1.0 - s)).astype(BF16)

    row = pl.BlockSpec((ts, D), lambda i: (i, 0))
    gspec = [pl.BlockSpec((ts, D), functools.partial(lambda b, i: (i, b), b)) for b in range(3)]
    return pl.pallas_call(
        body, name=name, grid=(S // ts,),
        in_specs=gspec + [row, row, row, row],
        out_specs=[pl.BlockSpec((ts, 3 * D), lambda i: (i, 0)), row, row, row],
        out_shape=[jax.ShapeDtypeStruct((S, 3 * D), BF16)] + [jax.ShapeDtypeStruct((S, D), BF16)] * 3,
        compiler_params=_cparams("parallel"),
    )(gl, gl, gl, ya, yb, yc, dm)


POOL_WIDTH = 512


def _pool_cnt(t, w):
    return jnp.minimum(t + 1, w).astype(F32)


def _pool_fwd(rest, col, mix, scale, *, name):
    S = rest.shape[0]
    W, G, H = POOL_WIDTH, POOL_GROUP_DIM, POOL_HALO
    ts = _row_tile(S)

    def body(v_ref, h_ref, mix_ref, sc_ref, d_ref, y_ref, ext_ref):
        i = pl.program_id(0)
        cur = v_ref[...]
        ext_ref[0:H, :] = jnp.where(i > 0, h_ref[...], 0.0)
        ext_ref[H:, :] = cur
        t = i * ts + lax.broadcasted_iota(jnp.int32, (ts, 1), 0)
        for g, w in enumerate(POOL_WINDOWS):
            cols = slice(g * G, (g + 1) * G)
            acc = cur[:, cols]
            for k in range(1, w):
                acc = acc + ext_ref[pl.ds(H - k, ts), cols]
            db = (acc / _pool_cnt(t, w) - cur[:, cols]).astype(BF16)
            d_ref[:, cols] = db
            y = jnp.dot(db, mix_ref[g], preferred_element_type=F32) * sc_ref[:, cols]
            y_ref[:, cols] = y.astype(BF16)

    row = pl.BlockSpec((ts, W), lambda i: (i, 0))
    return pl.pallas_call(
        body, name=name, grid=(S // ts,),
        in_specs=[pl.BlockSpec((ts, W), lambda i: (i, col)),
                  pl.BlockSpec((H, W), lambda i: (jnp.maximum(i * (ts // H) - 1, 0), col)),
                  pl.BlockSpec((len(POOL_WINDOWS), G, G), lambda i: (0, 0, 0)),
                  pl.BlockSpec((1, W), lambda i: (0, 0))],
        out_specs=[row, row],
        out_shape=[jax.ShapeDtypeStruct((S, W), BF16)] * 2,
        scratch_shapes=[pltpu.VMEM((H + ts, W), F32)],
        compiler_params=_cparams("parallel"),
    )(rest, rest, mix, scale)


def _pool_bwd(dy, d, mix, scale, *, name):
    S = dy.shape[0]
    W, G, H = POOL_WIDTH, POOL_GROUP_DIM, POOL_HALO
    ts = _row_tile(S)
    n = S // ts
    NT = (((1,), (1,)), ((), ()))
    TN = (((0,), (0,)), ((), ()))

    def body(dy_ref, dyn_ref, d_ref, mix_ref, sc_ref, dv_ref, dmix_ref, dsc_ref, ext_ref):
        i = pl.program_id(0)

        @pl.when(i == 0)
        def _():
            dmix_ref[...] = jnp.zeros_like(dmix_ref)
            dsc_ref[...] = jnp.zeros_like(dsc_ref)

        t = i * ts + lax.broadcasted_iota(jnp.int32, (ts, 1), 0)
        tn = (i + 1) * ts + lax.broadcasted_iota(jnp.int32, (H, 1), 0)
        for g, w in enumerate(POOL_WINDOWS):
            cols = slice(g * G, (g + 1) * G)
            sc = sc_ref[:, cols]
            dyv = dy_ref[:, cols]
            db = d_ref[:, cols]
            mg = mix_ref[g]
            yp = jnp.dot(db, mg, preferred_element_type=F32)
            dsc_ref[:, cols] += jnp.sum(dyv * yp, axis=0, keepdims=True)
            dyp = (dyv * sc).astype(BF16)
            dd = lax.dot_general(dyp, mg, NT, preferred_element_type=F32)
            dmix_ref[g] += lax.dot_general(db, dyp, TN, preferred_element_type=F32)
            dyn = (jnp.where(i < n - 1, dyn_ref[:, cols], 0.0) * sc).astype(BF16)
            ddn = lax.dot_general(dyn, mg, NT, preferred_element_type=F32)
            e = dd / _pool_cnt(t, w)
            ext_ref[0:ts, cols] = e
            ext_ref[ts:, cols] = ddn / _pool_cnt(tn, w)
            acc = e
            for k in range(1, w):
                acc = acc + ext_ref[pl.ds(k, ts), cols]
            dv_ref[:, cols] = (acc - dd).astype(BF16)

    row = pl.BlockSpec((ts, W), lambda i: (i, 0))
    return pl.pallas_call(
        body, name=name, grid=(n,),
        in_specs=[row,
                  pl.BlockSpec((H, W), lambda i: (jnp.minimum((i + 1) * (ts // H), S // H - 1), 0)),
                  row,
                  pl.BlockSpec((len(POOL_WINDOWS), G, G), lambda i: (0, 0, 0)),
                  pl.BlockSpec((1, W), lambda i: (0, 0))],
        out_specs=[row, pl.BlockSpec((len(POOL_WINDOWS), G, G), lambda i: (0, 0, 0)),
                   pl.BlockSpec((1, W), lambda i: (0, 0))],
        out_shape=[jax.ShapeDtypeStruct((S, W), BF16),
                   jax.ShapeDtypeStruct((len(POOL_WINDOWS), G, G), F32),
                   jax.ShapeDtypeStruct((1, W), F32)],
        scratch_shapes=[pltpu.VMEM((ts + H, W), F32)],
        compiler_params=_cparams("arbitrary"),
    )(dy, dy, d, mix, scale)


def _conv_taps(ext_ref, w_ref, b_ref, rows, first):
    K = w_ref.shape[0]
    pre = b_ref[...] + w_ref[K - 1:K, :] * ext_ref[pl.ds(first, rows), :]
    for k in range(K - 1):
        pre = pre + w_ref[k:k + 1, :] * ext_ref[pl.ds(first - (K - 1) + k, rows), :]
    return pre


ROW_CHUNK = 40


def _chunks(rows):
    ch = max(c for c in range(SUBLANE, ROW_CHUNK + 1, SUBLANE) if rows % c == 0)
    return [(r0, ch) for r0 in range(0, rows, ch)]


def _conv_bwd_taps(g_ref, ext_ref, w_ref, dx_store, ts):
    K = w_ref.shape[0]
    db, dws = 0.0, [0.0] * K
    fold = lambda v: jnp.sum(v.reshape(v.shape[0] // SUBLANE, SUBLANE, v.shape[1]), axis=0)
    for r0, ch in _chunks(ts):
        dx = w_ref[K - 1:K, :] * g_ref[pl.ds(r0, ch), :]
        for k in range(K - 1):
            dx = dx + w_ref[k:k + 1, :] * g_ref[pl.ds(r0 + K - 1 - k, ch), :]
        dx_store(r0, ch, dx.astype(BF16))
        gc = g_ref[pl.ds(r0, ch), :]
        db = db + fold(gc)
        for k in range(K):
            dws[k] = dws[k] + fold(gc * ext_ref[pl.ds(HALO - (K - 1) + k + r0, ch), :])
    total = lambda v: jnp.sum(v, axis=0, keepdims=True)
    return total(db), jnp.concatenate([total(d) for d in dws], axis=0)


def _prev_halo_spec(ts, tc, colfn):
    return pl.BlockSpec((HALO, tc), lambda *g: (jnp.maximum(g[-1] * (ts // HALO) - 1, 0), colfn(*g)))


def _conv_silu_fwd(rest, col0, C, w, b, *, name):
    S = rest.shape[0]
    ts, tc = _row_tile(S), 512
    assert col0 % tc == 0 and C % tc == 0
    cb = col0 // tc

    def body(x_ref, h_ref, w_ref, b_ref, o_ref, ext_ref):
        i = pl.program_id(1)
        ext_ref[0:HALO, :] = jnp.where(i > 0, h_ref[...], 0.0)
        ext_ref[HALO:, :] = x_ref[...]
        for r0, ch in _chunks(ts):
            pre = _conv_taps(ext_ref, w_ref, b_ref, ch, HALO + r0)
            o_ref[pl.ds(r0, ch), :] = pre * _sigmoid(pre)

    K = w.shape[0]
    return pl.pallas_call(
        body, name=name, grid=(C // tc, S // ts),
        in_specs=[pl.BlockSpec((ts, tc), lambda j, i: (i, cb + j)),
                  _prev_halo_spec(ts, tc, lambda j, i: cb + j),
                  pl.BlockSpec((K, tc), lambda j, i: (0, j)),
                  pl.BlockSpec((1, tc), lambda j, i: (0, j))],
        out_specs=pl.BlockSpec((ts, tc), lambda j, i: (i, j)),
        out_shape=jax.ShapeDtypeStruct((S, C), F32),
        scratch_shapes=[pltpu.VMEM((HALO + ts, tc), F32)],
        compiler_params=_cparams("parallel", "parallel"),
    )(rest, rest, w, b)


def _dsilu(pre):
    s = _sigmoid(pre)
    return s, s * (1.0 + pre * (1.0 - s))


def _conv_silu_bwd(rest, col0, C, w, b, dy, *, name):
    S = rest.shape[0]
    ts, tc = _row_tile(S), 512
    cb = col0 // tc
    n = S // ts
    K = w.shape[0]
    R = ts + HALO

    def body(x_ref, hp_ref, hn_ref, w_ref, b_ref, dy_ref, dyn_ref, dx_ref, dw_ref, db_ref, ext_ref, g_ref):
        i = pl.program_id(1)
        last = i == n - 1
        ext_ref[0:HALO, :] = jnp.where(i > 0, hp_ref[...], 0.0)
        ext_ref[HALO:HALO + ts, :] = x_ref[...]
        ext_ref[HALO + ts:, :] = jnp.where(last, 0.0, hn_ref[...])
        g_ref[0:ts, :] = dy_ref[...]
        g_ref[ts:, :] = jnp.where(last, 0.0, dyn_ref[...])
        for r0, ch in _chunks(R):
            pre = _conv_taps(ext_ref, w_ref, b_ref, ch, HALO + r0)
            g_ref[pl.ds(r0, ch), :] = g_ref[pl.ds(r0, ch), :] * _dsilu(pre)[1]

        def dx_store(r0, ch, v):
            dx_ref[pl.ds(r0, ch), :] = v

        db, dw = _conv_bwd_taps(g_ref, ext_ref, w_ref, dx_store, ts)
        _acc_out(db_ref, db, i == 0)
        _acc_out(dw_ref, dw, i == 0)

    nxt = lambda cf: pl.BlockSpec((HALO, tc), lambda j, i: (jnp.minimum((i + 1) * (ts // HALO), S // HALO - 1), cf(j)))
    return pl.pallas_call(
        body, name=name, grid=(C // tc, n),
        in_specs=[pl.BlockSpec((ts, tc), lambda j, i: (i, cb + j)),
                  _prev_halo_spec(ts, tc, lambda j, i: cb + j),
                  nxt(lambda j: cb + j),
                  pl.BlockSpec((K, tc), lambda j, i: (0, j)),
                  pl.BlockSpec((1, tc), lambda j, i: (0, j)),
                  pl.BlockSpec((ts, tc), lambda j, i: (i, j)),
                  nxt(lambda j: j)],
        out_specs=[pl.BlockSpec((ts, tc), lambda j, i: (i, j)),
                   pl.BlockSpec((K, tc), lambda j, i: (0, j)),
                   pl.BlockSpec((1, tc), lambda j, i: (0, j))],
        out_shape=[jax.ShapeDtypeStruct((S, C), BF16), jax.ShapeDtypeStruct((K, C), F32),
                   jax.ShapeDtypeStruct((1, C), F32)],
        scratch_shapes=[pltpu.VMEM((HALO + R, tc), F32), pltpu.VMEM((R, tc), F32)],
        compiler_params=_cparams("parallel", "arbitrary"),
    )(rest, rest, rest, w, b, dy, dy)


FFN_TC = 256


def _ffn_act_fwd(h0, w, b, *, name):
    S, F2 = h0.shape
    F = F2 // 2
    ts, tc = _row_tile(S), FFN_TC
    nb = F // tc
    K = w.shape[0]

    def body(xg_ref, hg_ref, xv_ref, hv_ref, wg_ref, wv_ref, bg_ref, bv_ref, o_ref, eg_ref, ev_ref):
        i = pl.program_id(1)
        for x_ref, h_ref, e_ref in ((xg_ref, hg_ref, eg_ref), (xv_ref, hv_ref, ev_ref)):
            e_ref[0:HALO, :] = jnp.where(i > 0, h_ref[...], 0.0)
            e_ref[HALO:, :] = x_ref[...]
        for r0, ch in _chunks(ts):
            pg = _conv_taps(eg_ref, wg_ref, bg_ref, ch, HALO + r0)
            pv = _conv_taps(ev_ref, wv_ref, bv_ref, ch, HALO + r0)
            o_ref[pl.ds(r0, ch), :] = (pg * _sigmoid(pg) * pv).astype(BF16)

    def half(off):
        return [pl.BlockSpec((ts, tc), lambda j, i: (i, off + j)), _prev_halo_spec(ts, tc, lambda j, i: off + j)]

    wspec = lambda off: pl.BlockSpec((K, tc), lambda j, i: (0, off + j))
    bspec = lambda off: pl.BlockSpec((1, tc), lambda j, i: (0, off + j))
    return pl.pallas_call(
        body, name=name, grid=(nb, S // ts),
        in_specs=half(0) + half(nb) + [wspec(0), wspec(nb), bspec(0), bspec(nb)],
        out_specs=pl.BlockSpec((ts, tc), lambda j, i: (i, j)),
        out_shape=jax.ShapeDtypeStruct((S, F), BF16),
        scratch_shapes=[pltpu.VMEM((HALO + ts, tc), F32)] * 2,
        compiler_params=_cparams("parallel", "parallel"),
    )(h0, h0, h0, h0, w, w, b, b)


def _ffn_act_bwd(h0, w, b, da, *, name):
    S, F2 = h0.shape
    F = F2 // 2
    ts, tc = _row_tile(S), FFN_TC
    nb = F // tc
    n = S // ts
    K = w.shape[0]
    R = ts + HALO

    def body(xg_ref, pg_ref, ng_ref, xv_ref, pv_ref, nv_ref, wg_ref, wv_ref, bg_ref, bv_ref, da_ref, dan_ref,
             dx_ref, dw_ref, db_ref, eg_ref, ev_ref, gg_ref, gv_ref):
        i = pl.program_id(1)
        last = i == n - 1
        for x_ref, p_ref, n_ref, e_ref in ((xg_ref, pg_ref, ng_ref, eg_ref), (xv_ref, pv_ref, nv_ref, ev_ref)):
            e_ref[0:HALO, :] = jnp.where(i > 0, p_ref[...], 0.0)
            e_ref[HALO:HALO + ts, :] = x_ref[...]
            e_ref[HALO + ts:, :] = jnp.where(last, 0.0, n_ref[...])
        gg_ref[0:ts, :] = da_ref[...]
        gg_ref[ts:, :] = jnp.where(last, 0.0, dan_ref[...])
        for r0, ch in _chunks(R):
            rows = pl.ds(r0, ch)
            dav = gg_ref[rows, :]
            pg = _conv_taps(eg_ref, wg_ref, bg_ref, ch, HALO + r0)
            pv = _conv_taps(ev_ref, wv_ref, bv_ref, ch, HALO + r0)
            s, ds = _dsilu(pg)
            gg_ref[rows, :] = dav * pv * ds
            gv_ref[rows, :] = dav * pg * s
        for h, (g_ref, e_ref, w_ref) in enumerate(((gg_ref, eg_ref, wg_ref), (gv_ref, ev_ref, wv_ref))):
            def dx_store(r0, ch, v):
                dx_ref[h, pl.ds(r0, ch), :] = v

            db, dw = _conv_bwd_taps(g_ref, e_ref, w_ref, dx_store, ts)
            _acc_out(db_ref.at[h], db, i == 0)
            _acc_out(dw_ref.at[h], dw, i == 0)

    nxt = lambda off: pl.BlockSpec((HALO, tc), lambda j, i: (jnp.minimum((i + 1) * (ts // HALO), S // HALO - 1), off + j))

    def half(off):
        return [pl.BlockSpec((ts, tc), lambda j, i: (i, off + j)), _prev_halo_spec(ts, tc, lambda j, i: off + j), nxt(off)]

    wspec = lambda off: pl.BlockSpec((K, tc), lambda j, i: (0, off + j))
    bspec = lambda off: pl.BlockSpec((1, tc), lambda j, i: (0, off + j))
    return pl.pallas_call(
        body, name=name, grid=(nb, n),
        in_specs=half(0) + half(nb) + [wspec(0), wspec(nb), bspec(0), bspec(nb),
                                       pl.BlockSpec((ts, tc), lambda j, i: (i, j)), nxt(0)],
        out_specs=[pl.BlockSpec((2, ts, tc), lambda j, i: (0, i, j)),
                   pl.BlockSpec((2, K, tc), lambda j, i: (0, 0, j)),
                   pl.BlockSpec((2, 1, tc), lambda j, i: (0, 0, j))],
        out_shape=[jax.ShapeDtypeStruct((2, S, F), BF16), jax.ShapeDtypeStruct((2, K, F), F32),
                   jax.ShapeDtypeStruct((2, 1, F), F32)],
        scratch_shapes=[pltpu.VMEM((HALO + R, tc), F32)] * 2 + [pltpu.VMEM((R, tc), F32)] * 2,
        compiler_params=_cparams("parallel", "arbitrary"),
    )(h0, h0, h0, h0, h0, h0, w, w, b, b, da, da)


CUM_TILE = 256
ATT_TILE = 512
ATT_TILE_FWD = 1024
NEG = -1e30
N_PAIR = ATTN_HEADS // 2
NT_DIMS = (((1,), (1,)), ((), ()))


def _split3(x):
    a = x.astype(BF16)
    r = x - a.astype(F32)
    b = r.astype(BF16)
    c = (r - b.astype(F32)).astype(BF16)
    return a, b, c


def _tri_dot(tri, x):
    return sum(jnp.dot(tri, p, preferred_element_type=F32) for p in _split3(x))


def _log_sigmoid(x):
    return jnp.minimum(x, 0.0) - jnp.log(1.0 + jnp.exp(-jnp.abs(x)))


def _col(v, idx, lane):
    return jnp.sum(jnp.where(lane == idx, v, 0.0), axis=1, keepdims=True)


def _fox_pre(rest, col, fb, *, name):
    S = rest.shape[0]
    ts = min(CUM_TILE, S)

    def body(f_ref, fb_ref, ccol_ref, carry_ref):
        i = pl.program_id(0)

        @pl.when(i == 0)
        def _():
            carry_ref[...] = jnp.zeros_like(carry_ref)

        lane = lax.broadcasted_iota(jnp.int32, (1, LANE), 1)
        logf = jnp.where(lane < ATTN_HEADS, _log_sigmoid(f_ref[...] + fb_ref[...]), 0.0)
        r = lax.broadcasted_iota(jnp.int32, (ts, ts), 0)
        c = lax.broadcasted_iota(jnp.int32, (ts, ts), 1)
        tri = jnp.where(c <= r, 1.0, 0.0).astype(BF16)
        cs = _tri_dot(tri, logf) + carry_ref[...]
        ccol_ref[...] = cs
        carry_ref[...] = cs[ts - 1:ts, :]

    return pl.pallas_call(
        body, name=name, grid=(S // ts,),
        in_specs=[pl.BlockSpec((ts, LANE), lambda i: (i, col)), pl.BlockSpec((1, LANE), lambda i: (0, 0))],
        out_specs=pl.BlockSpec((ts, LANE), lambda i: (i, 0)),
        out_shape=jax.ShapeDtypeStruct((S, LANE), F32),
        scratch_shapes=[pltpu.VMEM((1, LANE), F32)],
        compiler_params=_cparams("arbitrary"),
    )(rest, fb)


def _fox_post(dcq, dck, rest, col, fb, ddt, *, name):
    S = rest.shape[0]
    ts = min(CUM_TILE, S)
    n = S // ts

    def body(dcq_ref, dck_ref, f_ref, fb_ref, ddt_ref, o_ref, db_ref, carry_ref):
        i = pl.program_id(0)

        @pl.when(i == 0)
        def _():
            carry_ref[...] = jnp.zeros_like(carry_ref)

        lane = lax.broadcasted_iota(jnp.int32, (1, LANE), 1)
        dc = jnp.zeros((ts, LANE), F32)
        for h in range(ATTN_HEADS):
            d = dcq_ref[h // 2] - dck_ref[h // 2]
            dc = jnp.where(lane == h, _col(d, h % 2, lane), dc)
        r = lax.broadcasted_iota(jnp.int32, (ts, ts), 0)
        c = lax.broadcasted_iota(jnp.int32, (ts, ts), 1)
        tri = jnp.where(c >= r, 1.0, 0.0).astype(BF16)
        rc = _tri_dot(tri, dc) + carry_ref[...]
        carry_ref[...] = rc[0:1, :]
        df = rc * _sigmoid(-(f_ref[...] + fb_ref[...]))
        out = jnp.where(lane < ATTN_HEADS, df, ddt_ref[...])
        o_ref[...] = out.astype(BF16)
        _acc_out(db_ref, jnp.sum(out, axis=0, keepdims=True), i == 0)

    rev = lambda i: n - 1 - i
    return pl.pallas_call(
        body, name=name, grid=(n,),
        in_specs=[pl.BlockSpec((N_PAIR, ts, LANE), lambda i: (0, rev(i), 0)),
                  pl.BlockSpec((N_PAIR, ts, LANE), lambda i: (0, rev(i), 0)),
                  pl.BlockSpec((ts, LANE), lambda i: (rev(i), col)),
                  pl.BlockSpec((1, LANE), lambda i: (0, 0)),
                  pl.BlockSpec((ts, LANE), lambda i: (rev(i), 0))],
        out_specs=[pl.BlockSpec((ts, LANE), lambda i: (rev(i), 0)), pl.BlockSpec((1, LANE), lambda i: (0, 0))],
        out_shape=[jax.ShapeDtypeStruct((S, LANE), BF16), jax.ShapeDtypeStruct((1, LANE), F32)],
        scratch_shapes=[pltpu.VMEM((1, LANE), F32)],
        compiler_params=_cparams("arbitrary"),
    )(dcq, dck, rest, fb, ddt)


def _head_masks():
    lane = lax.broadcasted_iota(jnp.int32, (1, LANE), 1)
    return lane, (lane < ATTN_HEAD_DIM, lane >= ATTN_HEAD_DIM)


def _causal_mask(t):
    r = lax.broadcasted_iota(jnp.int32, (t, t), 0)
    c = lax.broadcasted_iota(jnp.int32, (t, t), 1)
    return r, c


AUG_ONE = 0
AUG_C = 3


def _free_lane0(hh):
    return ATTN_HEAD_DIM if hh == 0 else 0


def _augment(x, head_mask, lane, f0, c, key_side):
    terms = _split3(c)
    one = jnp.ones((), BF16)
    ones_at, terms_at = (AUG_ONE, AUG_C) if key_side else (AUG_C, AUG_ONE)
    out = jnp.where(head_mask, x, jnp.zeros((), BF16))
    for k in range(3):
        out = jnp.where(lane == f0 + ones_at + k, one, out)
        out = jnp.where(lane == f0 + terms_at + k, terms[k], out)
    return out


def _carry_start(kind, in_refs, out_refs, sems, first):
    @pl.when(first)
    def _():
        if kind == "gather":
            _gather_phase(0, in_refs, out_refs, sems)
        else:
            _a2a_start(in_refs, out_refs, sems)


def _carry_mid(kind, in_refs, out_refs, sems, mid):
    if kind == "gather":
        @pl.when(mid)
        def _():
            _gather_phase(1, in_refs, out_refs, sems)


def _carry_wait(kind, in_refs, out_refs, sems, last):
    @pl.when(last)
    def _():
        if kind == "gather":
            _gather_phase(2, in_refs, out_refs, sems)
        else:
            _a2a_wait(in_refs, out_refs, sems)


def _attn_fwd(qkv, ccol, *, comm=None, name):
    S = qkv.shape[0]
    t = min(ATT_TILE_FWD, S)
    n = S // t
    scale = ATTN_HEAD_DIM ** -0.5

    nc = len(comm[1]) if comm else 0

    def body(*refs):
        q_ref, k_ref, v_ref, cc_ref = refs[:4]
        o_ref, lrow_ref = refs[4 + nc:6 + nc]
        kaug_ref, vt_ref, m_ref, l_ref, acc_ref = refs[6 + 2 * nc:11 + 2 * nc]
        j, i = pl.program_id(0), pl.program_id(1)
        lane, masks = _head_masks()
        if comm:
            carried = (refs[4:4 + nc], refs[6 + nc:6 + 2 * nc], refs[11 + 2 * nc:])
            _carry_start(comm[0], *carried, jnp.logical_and(j == 0, i == 0))
            _carry_mid(comm[0], *carried, jnp.logical_and(j == N_PAIR - 1, i == 0))

        @pl.when(i == 0)
        def _():
            def fill(b, carry):
                off = pl.multiple_of(b * t, t)
                kblk = k_ref[pl.ds(off, t), :]
                ccb = cc_ref[pl.ds(off, t), :]
                for hh in range(2):
                    ck = _col(ccb, 2 * j + hh, lane)
                    kaug_ref[hh, pl.ds(off, t), :] = _augment(kblk, masks[hh], lane, _free_lane0(hh), -ck, True)
                vt_ref[:, pl.ds(off, t)] = v_ref[pl.ds(off, t), :].astype(F32).T.astype(BF16)
                return carry
            lax.fori_loop(0, n, fill, 0)

        q2 = q_ref[...] * scale
        ccq = cc_ref[pl.ds(pl.multiple_of(i * t, t), t), :]
        qaug = [_augment(q2, masks[hh], lane, _free_lane0(hh), _col(ccq, 2 * j + hh, lane), False) for hh in range(2)]
        m_ref[...] = jnp.full(m_ref.shape, NEG, F32)
        l_ref[...] = jnp.zeros_like(l_ref)
        acc_ref[...] = jnp.zeros_like(acc_ref)

        def step(kb, masked):
            off = pl.multiple_of(kb * t, t)
            vt = vt_ref[:, pl.ds(off, t)]
            for hh in range(2):
                st = lax.dot_general(kaug_ref[hh, pl.ds(off, t), :], qaug[hh], NT_DIMS, preferred_element_type=F32)
                if masked:
                    r, c = _causal_mask(t)
                    st = jnp.where(r <= c, st, NEG)
                m_old = m_ref[hh]
                m_new = jnp.maximum(m_old, jnp.max(st, axis=0, keepdims=True))
                p = jnp.exp(st - m_new)
                alpha = jnp.exp(m_old - m_new)
                l_ref[hh] = alpha * l_ref[hh] + jnp.sum(p, axis=0, keepdims=True)
                acc_ref[hh] = alpha * acc_ref[hh] + jnp.dot(vt, p.astype(BF16), preferred_element_type=F32)
                m_ref[hh] = m_new

        def loop_body(kb, carry):
            step(kb, False)
            return carry

        lax.fori_loop(0, i, loop_body, 0)
        step(i, True)
        drow = lax.broadcasted_iota(jnp.int32, (LANE, 1), 0)
        ot = jnp.where(drow < ATTN_HEAD_DIM, acc_ref[0] / l_ref[0], acc_ref[1] / l_ref[1])
        o_ref[...] = ot.T.astype(BF16)
        for hh in range(2):
            lrow_ref[hh] = m_ref[hh] + jnp.log(l_ref[hh])
        if comm:
            _carry_wait(comm[0], *carried, jnp.logical_and(j == N_PAIR - 1, i == n - 1))

    res = pl.pallas_call(
        body, name=name, grid=(N_PAIR, n),
        in_specs=[pl.BlockSpec((t, LANE), lambda j, i: (i, j)),
                  pl.BlockSpec((S, LANE), lambda j, i: (0, N_PAIR + j)),
                  pl.BlockSpec((S, LANE), lambda j, i: (0, 2 * N_PAIR + j)),
                  pl.BlockSpec((S, LANE), lambda j, i: (0, 0))] + [ANY] * nc,
        out_specs=[pl.BlockSpec((t, LANE), lambda j, i: (i, j)),
                   pl.BlockSpec((2, 1, t), lambda j, i: (j, 0, i))] + [ANY] * nc,
        out_shape=[jax.ShapeDtypeStruct((S, N_PAIR * LANE), BF16), jax.ShapeDtypeStruct((ATTN_HEADS, 1, S), F32)]
        + (_a2a_out_shapes(*comm) if comm else []),
        scratch_shapes=[pltpu.VMEM((2, S, LANE), BF16), pltpu.VMEM((LANE, S), BF16),
                        pltpu.VMEM((2, 1, t), F32), pltpu.VMEM((2, 1, t), F32), pltpu.VMEM((2, LANE, t), F32)]
        + (_a2a_sems(nc) if comm else []),
        compiler_params=_cparams("arbitrary" if comm else "parallel", "arbitrary"),
    )(qkv, qkv, qkv, ccol, *(comm[1] if comm else []))
    return res[0], res[1], list(res[2:])


def _attn_bwd(qkv, o, do, lrow, ccol, *, comm=None, name):
    S = qkv.shape[0]
    t = min(ATT_TILE, S)
    n = S // t
    scale = ATTN_HEAD_DIM ** -0.5

    nc = len(comm[1]) if comm else 0

    def body(*refs):
        q_ref, k_ref, v_ref, o_ref, do_ref, lrow_ref, cc_ref = refs[:7]
        dq_ref, dk_ref, dv_ref, dcq_ref, dck_ref = refs[7 + nc:12 + nc]
        qaug_ref, drow_ref, dqt_ref, dka_ref, dva_ref = refs[12 + 2 * nc:17 + 2 * nc]
        carried = (refs[7:7 + nc], refs[12 + nc:12 + 2 * nc], refs[17 + 2 * nc:])
        j, kb = pl.program_id(0), pl.program_id(1)
        lane, masks = _head_masks()
        if comm:
            _carry_start(comm[0], *carried, jnp.logical_and(j == 0, kb == 0))
        drows = lax.broadcasted_iota(jnp.int32, (LANE, 1), 0)

        @pl.when(kb == 0)
        def _():
            dqt_ref[...] = jnp.zeros_like(dqt_ref)

            def fill(b, carry):
                off = pl.multiple_of(b * t, t)
                q2 = q_ref[pl.ds(off, t), :] * scale
                ccb = cc_ref[pl.ds(off, t), :]
                prod_t = (do_ref[pl.ds(off, t), :].astype(F32) * o_ref[pl.ds(off, t), :].astype(F32)).T
                lo = jnp.sum(jnp.where(drows < ATTN_HEAD_DIM, prod_t, 0.0), axis=0, keepdims=True)
                drow_ref[0, :, pl.ds(off, t)] = lo
                drow_ref[1, :, pl.ds(off, t)] = jnp.sum(prod_t, axis=0, keepdims=True) - lo
                for hh in range(2):
                    cq = _col(ccb, 2 * j + hh, lane)
                    qaug_ref[hh, pl.ds(off, t), :] = _augment(q2, masks[hh], lane, _free_lane0(hh), cq, False)
                return carry
            lax.fori_loop(0, n, fill, 0)

        koff = pl.multiple_of(kb * t, t)
        kblk = k_ref[...]
        v2 = v_ref[...]
        cck = cc_ref[pl.ds(koff, t), :]
        kaug = [_augment(kblk, masks[hh], lane, _free_lane0(hh), -_col(cck, 2 * j + hh, lane), True) for hh in range(2)]
        kaug_t = [ka.astype(F32).T.astype(BF16) for ka in kaug]
        vh = [jnp.where(mk, v2, jnp.zeros((), BF16)) for mk in masks]
        dka_ref[...] = jnp.zeros_like(dka_ref)
        dva_ref[...] = jnp.zeros_like(dva_ref)

        def step(qb, masked):
            off = pl.multiple_of(qb * t, t)
            doblk = do_ref[pl.ds(off, t), :]
            for hh in range(2):
                qa = qaug_ref[hh, pl.ds(off, t), :]
                st = lax.dot_general(kaug[hh], qa, NT_DIMS, preferred_element_type=F32)
                if masked:
                    r, c = _causal_mask(t)
                    st = jnp.where(r <= c, st, NEG)
                pt = jnp.exp(st - lrow_ref[hh, :, pl.ds(off, t)])
                dpt = lax.dot_general(vh[hh], doblk, NT_DIMS, preferred_element_type=F32)
                dst = (pt * (dpt - drow_ref[hh, :, pl.ds(off, t)])).astype(BF16)
                dva_ref[hh] += jnp.dot(pt.astype(BF16), doblk, preferred_element_type=F32)
                dka_ref[hh] += jnp.dot(dst, qa, preferred_element_type=F32)
                dqt_ref[hh, :, pl.ds(off, t)] += jnp.dot(kaug_t[hh], dst, preferred_element_type=F32)

        step(kb, True)

        def loop_body(qb, carry):
            step(qb, False)
            return carry

        lax.fori_loop(kb + 1, n, loop_body, 0)
        dk_ref[...] = jnp.where(masks[0], dka_ref[0], dka_ref[1]).astype(BF16)
        dv_ref[...] = jnp.where(masks[0], dva_ref[0], dva_ref[1]).astype(BF16)
        dck = [_col(dka_ref[hh], _free_lane0(hh) + AUG_C, lane) for hh in range(2)]
        dck_ref[0] = jnp.where(lane == 0, dck[0], jnp.where(lane == 1, dck[1], 0.0))

        @pl.when(kb == n - 1)
        def _():
            def flush(b, carry):
                off = pl.multiple_of(b * t, t)
                d = [dqt_ref[hh, :, pl.ds(off, t)].T for hh in range(2)]
                dq_ref[pl.ds(off, t), :] = (jnp.where(masks[0], d[0], d[1]) * scale).astype(BF16)
                dcq = [_col(d[hh], _free_lane0(hh) + AUG_ONE, lane) for hh in range(2)]
                dcq_ref[0, pl.ds(off, t), :] = jnp.where(lane == 0, dcq[0], jnp.where(lane == 1, dcq[1], 0.0))
                return carry
            lax.fori_loop(0, n, flush, 0)

        if comm:
            _carry_wait(comm[0], *carried, jnp.logical_and(j == N_PAIR - 1, kb == n - 1))

    full = lambda cb: pl.BlockSpec((S, LANE), lambda j, kb: (0, cb(j)))
    kspec = lambda base: pl.BlockSpec((t, LANE), lambda j, kb: (kb, base + j))
    oblk = pl.BlockSpec((t, LANE), lambda j, kb: (kb, j))
    res = pl.pallas_call(
        body, name=name, grid=(N_PAIR, n),
        in_specs=[full(lambda j: j), kspec(N_PAIR), kspec(2 * N_PAIR), full(lambda j: j), full(lambda j: j),
                  pl.BlockSpec((2, 1, S), lambda j, kb: (j, 0, 0)), full(lambda j: 0)] + [ANY] * nc,
        out_specs=[full(lambda j: j), oblk, oblk,
                   pl.BlockSpec((1, S, LANE), lambda j, kb: (j, 0, 0)),
                   pl.BlockSpec((1, t, LANE), lambda j, kb: (j, kb, 0))] + [ANY] * nc,
        out_shape=[jax.ShapeDtypeStruct((S, N_PAIR * LANE), BF16)] * 3 + [jax.ShapeDtypeStruct((N_PAIR, S, LANE), F32)] * 2
        + (_a2a_out_shapes(*comm) if comm else []),
        scratch_shapes=[pltpu.VMEM((2, S, LANE), BF16), pltpu.VMEM((2, 1, S), F32), pltpu.VMEM((2, LANE, S), F32),
                        pltpu.VMEM((2, t, LANE), F32), pltpu.VMEM((2, t, LANE), F32)]
        + (_a2a_sems(nc) if comm else []),
        compiler_params=_cparams("arbitrary" if comm else "parallel", "arbitrary"),
    )(qkv, qkv, qkv, o, do, lrow, ccol, *(comm[1] if comm else []))
    return tuple(res[:5]) + (list(res[5:]),)


DT_LANE0 = ATTN_HEADS
SSD_PAIRS = SSD_HEADS // 2
SSD_X = SSD_HEADS * SSD_HEAD_DIM
SSD_B0 = SSD_X
SSD_C0 = SSD_X + SSD_GROUPS * SSD_STATE
SSD_CH = SSD_X + 2 * SSD_GROUPS * SSD_STATE
TN_DIMS = (((0,), (0,)), ((), ()))


def _softplus(x):
    return jnp.maximum(x, 0.0) + jnp.log(1.0 + jnp.exp(-jnp.abs(x)))


def _ssd_prep(fdt, dtb, alog):
    L = fdt.shape[0]
    lane = lax.broadcasted_iota(jnp.int32, (1, LANE), 1)
    hl = jnp.logical_and(lane >= DT_LANE0, lane < DT_LANE0 + SSD_HEADS)
    dtv = jnp.where(hl, _softplus(fdt + dtb), 0.0)
    A = jnp.where(hl, -jnp.exp(alog), 0.0)
    r = lax.broadcasted_iota(jnp.int32, (L, L), 0)
    c = lax.broadcasted_iota(jnp.int32, (L, L), 1)
    cs = _tri_dot(jnp.where(c <= r, 1.0, 0.0).astype(BF16), dtv * A)
    return lane, hl, dtv, A, cs, r, c


def _halves(lane, v0, v1):
    return jnp.where(lane < SSD_HEAD_DIM, v0, v1)


def _head_select(width):
    r = jnp.arange(LANE)[:, None]
    c = jnp.arange(SSD_HEADS * width)[None, :]
    return (r == DT_LANE0 + c // width).astype(BF16)


def _spread(x, sel, terms):
    return sum(jnp.dot(p, sel, preferred_element_type=F32) for p in _split3(x)[:terms])


def _ssd_fwd(xc, rest, col, dtb, alog, dskip, *, name):
    S = xc.shape[0]
    L = SSD_CHUNK
    nc = S // L

    def body(xc_ref, f_ref, dtb_ref, al_ref, dk_ref, selp_ref, y_ref, hs_ref, h_ref, cst_ref):
        i = pl.program_id(0)

        @pl.when(i == 0)
        def _():
            h_ref[...] = jnp.zeros_like(h_ref)

        lane, hl, dtv, A, cs, r, c = _ssd_prep(f_ref[...], dtb_ref[...], al_ref[...])
        cst_ref[...] = cs.T
        cs_last = cs[L - 1:L, :]
        cd = jnp.exp(cs_last)
        dkv = dk_ref[...]
        selp = selp_ref[...]
        dt_p, ecs_p, dec_p = (_spread(v, selp, 2) for v in (dtv, jnp.exp(cs), jnp.exp(cs_last - cs)))
        prow = lax.broadcasted_iota(jnp.int32, (LANE, 1), 0)
        for g in range(SSD_GROUPS):
            Bg = xc_ref[:, SSD_B0 + g * SSD_STATE:SSD_B0 + (g + 1) * SSD_STATE].astype(BF16)
            Cg = xc_ref[:, SSD_C0 + g * SSD_STATE:SSD_C0 + (g + 1) * SSD_STATE].astype(BF16)
            CB = lax.dot_general(Cg, Bg, NT_DIMS, preferred_element_type=F32)
            for pp in range(SSD_PAIRS // SSD_GROUPS):
                pi = g * (SSD_PAIRS // SSD_GROUPS) + pp
                hl0 = DT_LANE0 + 2 * pi
                pair = slice(pi * LANE, (pi + 1) * LANE)
                x2 = xc_ref[:, pair]
                xd = x2 * dt_p[:, pair]
                xdb = xd.astype(BF16)
                yd = jnp.zeros((L, LANE), F32)
                for hh in range(2):
                    seg = _col(cs, hl0 + hh, lane) - cst_ref[hl0 + hh:hl0 + hh + 1, :]
                    M = CB * jnp.exp(jnp.where(c <= r, seg, NEG))
                    yh = jnp.dot(M.astype(BF16), xdb, preferred_element_type=F32)
                    yd = jnp.where((lane >= SSD_HEAD_DIM) if hh else (lane < SSD_HEAD_DIM), yh, yd)
                hp = h_ref[pi]
                hs_ref[0, pi] = hp
                yo = lax.dot_general(Cg, hp.astype(BF16), NT_DIMS, preferred_element_type=F32) * ecs_p[:, pair]
                dsk = _halves(lane, _col(dkv, hl0, lane), _col(dkv, hl0 + 1, lane))
                y_ref[:, pair] = yd + yo + dsk * x2
                xw = (xd * dec_p[:, pair]).astype(BF16)
                st = lax.dot_general(xw, Bg, TN_DIMS, preferred_element_type=F32)
                cdp = jnp.where(prow < SSD_HEAD_DIM, _col(cd, hl0, lane), _col(cd, hl0 + 1, lane))
                h_ref[pi] = cdp * hp + st

    vec = pl.BlockSpec((1, LANE), lambda i: (0, 0))
    selp = _head_select(SSD_HEAD_DIM)
    return pl.pallas_call(
        body, name=name, grid=(nc,),
        in_specs=[pl.BlockSpec((L, SSD_CH), lambda i: (i, 0)), pl.BlockSpec((L, LANE), lambda i: (i, col)),
                  vec, vec, vec, pl.BlockSpec(selp.shape, lambda i: (0, 0))],
        out_specs=[pl.BlockSpec((L, SSD_X), lambda i: (i, 0)),
                   pl.BlockSpec((1, SSD_PAIRS, LANE, SSD_STATE), lambda i: (i, 0, 0, 0))],
        out_shape=[jax.ShapeDtypeStruct((S, SSD_X), F32),
                   jax.ShapeDtypeStruct((nc, SSD_PAIRS, LANE, SSD_STATE), F32)],
        scratch_shapes=[pltpu.VMEM((SSD_PAIRS, LANE, SSD_STATE), F32), pltpu.VMEM((LANE, L), F32)],
        compiler_params=_cparams("arbitrary"),
    )(xc, rest, dtb, alog, dskip, selp)


def _pair_sums(lane, v):
    lo = jnp.sum(jnp.where(lane < SSD_HEAD_DIM, v, 0.0), axis=1, keepdims=True)
    return lo, jnp.sum(v, axis=1, keepdims=True) - lo


def _ssd_bwd(xc, rest, col, dtb, alog, dskip, hs, dy, *, name):
    S = xc.shape[0]
    L = SSD_CHUNK
    nc = S // L
    PG = SSD_PAIRS // SSD_GROUPS

    def body(xc_ref, f_ref, dtb_ref, al_ref, dk_ref, hs_ref, dy_ref, selp_ref,
             dxc_ref, ddt_ref, dp_ref, dh_ref, cst_ref):
        i = pl.program_id(0)

        @pl.when(i == 0)
        def _():
            dh_ref[...] = jnp.zeros_like(dh_ref)
            dp_ref[...] = jnp.zeros_like(dp_ref)

        fv = f_ref[...] + dtb_ref[...]
        lane, hl, dtv, A, cs, r, c = _ssd_prep(f_ref[...], dtb_ref[...], al_ref[...])
        cst_ref[...] = cs.T
        cs_last = cs[L - 1:L, :]
        cd = jnp.exp(cs_last)
        dkv = dk_ref[...]
        selp = selp_ref[...]
        dt_p, ecs_p, dec_p = (_spread(v, selp, 2) for v in (dtv, jnp.exp(cs), jnp.exp(cs_last - cs)))
        prow = lax.broadcasted_iota(jnp.int32, (LANE, 1), 0)
        lrow = lax.broadcasted_iota(jnp.int32, (L, 1), 0)
        is_last = lrow == L - 1
        causal = c <= r
        dcs = jnp.zeros((L, LANE), F32)
        ddt = jnp.zeros((L, LANE), F32)
        dD = jnp.zeros((1, LANE), F32)
        for g in range(SSD_GROUPS):
            Bg = xc_ref[:, SSD_B0 + g * SSD_STATE:SSD_B0 + (g + 1) * SSD_STATE].astype(BF16)
            Cg = xc_ref[:, SSD_C0 + g * SSD_STATE:SSD_C0 + (g + 1) * SSD_STATE].astype(BF16)
            CB = lax.dot_general(Cg, Bg, NT_DIMS, preferred_element_type=F32)
            dCB = jnp.zeros((L, L), F32)
            dB = jnp.zeros((L, SSD_STATE), F32)
            dC = jnp.zeros((L, SSD_STATE), F32)
            for pp in range(PG):
                pi = g * PG + pp
                hl0 = DT_LANE0 + 2 * pi
                pair = slice(pi * LANE, (pi + 1) * LANE)
                x2 = xc_ref[:, pair]
                dy2 = dy_ref[:, pair]
                dtp, ecsp, decp = dt_p[:, pair], ecs_p[:, pair], dec_p[:, pair]
                xd = x2 * dtp
                xdb = xd.astype(BF16)
                dsk = _halves(lane, _col(dkv, hl0, lane), _col(dkv, hl0 + 1, lane))
                dx2 = dsk * dy2
                sD = _pair_sums(lane, dy2 * x2)
                hp = hs_ref[0, pi]
                hpb = hp.astype(BF16)
                yo = lax.dot_general(Cg, hpb, NT_DIMS, preferred_element_type=F32) * ecsp
                dW = (ecsp * dy2).astype(BF16)
                dC = dC + jnp.dot(dW, hpb, preferred_element_type=F32)
                dhp = lax.dot_general(dW, Cg, TN_DIMS, preferred_element_type=F32)
                sYo = _pair_sums(lane, dy2 * yo)
                dhn = dh_ref[pi]
                cdh = (_col(cd, hl0, lane), _col(cd, hl0 + 1, lane))
                dhp = dhp + jnp.where(prow < SSD_HEAD_DIM, cdh[0], cdh[1]) * dhn
                rs = jnp.sum(dhn * hp, axis=1, keepdims=True)
                lo = jnp.sum(jnp.where(prow < SSD_HEAD_DIM, rs, 0.0), axis=0, keepdims=True)
                dcd = (lo, jnp.sum(rs, axis=0, keepdims=True) - lo)
                dhnb = dhn.astype(BF16)
                G = lax.dot_general(Bg, dhnb, NT_DIMS, preferred_element_type=F32)
                dxd = decp * G
                sdd = _pair_sums(lane, xd * dxd)
                dB = dB + jnp.dot((xd * decp).astype(BF16), dhnb, preferred_element_type=F32)
                dh_ref[pi] = dhp
                for hh in range(2):
                    hmask = (lane >= SSD_HEAD_DIM) if hh else (lane < SSD_HEAD_DIM)
                    seg = _col(cs, hl0 + hh, lane) - cst_ref[hl0 + hh:hl0 + hh + 1, :]
                    Lm = jnp.exp(jnp.where(causal, seg, NEG))
                    M = CB * Lm
                    dyh = jnp.where(hmask, dy2, 0.0).astype(BF16)
                    dM = lax.dot_general(dyh, xdb, NT_DIMS, preferred_element_type=F32)
                    dxd = dxd + lax.dot_general(M.astype(BF16), dyh, TN_DIMS, preferred_element_type=F32)
                    Q = dM * M
                    dCB = dCB + dM * Lm
                    dd = sdd[hh]
                    end = dcd[hh] * cdh[hh] + jnp.sum(dd, axis=0, keepdims=True)
                    dcs_h = (sYo[hh] - dd + jnp.sum(Q, axis=1, keepdims=True)
                             - jnp.sum(Q.T, axis=1, keepdims=True) + jnp.where(is_last, end, 0.0))
                    dcs = jnp.where(lane == hl0 + hh, dcs_h, dcs)
                    dD = jnp.where(lane == hl0 + hh, jnp.sum(sD[hh], axis=0, keepdims=True), dD)
                sdt = _pair_sums(lane, dxd * x2)
                ddt = jnp.where(lane == hl0, sdt[0], jnp.where(lane == hl0 + 1, sdt[1], ddt))
                dxc_ref[:, pair] = dx2 + dxd * dtp
            dCBb = dCB.astype(BF16)
            dC = dC + jnp.dot(dCBb, Bg, preferred_element_type=F32)
            dB = dB + lax.dot_general(dCBb, Cg, TN_DIMS, preferred_element_type=F32)
            dxc_ref[:, SSD_B0 + g * SSD_STATE:SSD_B0 + (g + 1) * SSD_STATE] = dB
            dxc_ref[:, SSD_C0 + g * SSD_STATE:SSD_C0 + (g + 1) * SSD_STATE] = dC
        da = _tri_dot(jnp.where(c >= r, 1.0, 0.0).astype(BF16), dcs)
        ddtv = ddt + da * A
        ddt_ref[...] = jnp.where(hl, ddtv * _sigmoid(fv), 0.0)
        dal = jnp.sum(da * dtv, axis=0, keepdims=True) * A
        dp_ref[0:1, :] += dal
        dp_ref[1:2, :] += dD

    rev = lambda i: nc - 1 - i
    vec = pl.BlockSpec((1, LANE), lambda i: (0, 0))
    selp = _head_select(SSD_HEAD_DIM)
    return pl.pallas_call(
        body, name=name, grid=(nc,),
        in_specs=[pl.BlockSpec((L, SSD_CH), lambda i: (rev(i), 0)), pl.BlockSpec((L, LANE), lambda i: (rev(i), col)),
                  vec, vec, vec,
                  pl.BlockSpec((1, SSD_PAIRS, LANE, SSD_STATE), lambda i: (rev(i), 0, 0, 0)),
                  pl.BlockSpec((L, SSD_X), lambda i: (rev(i), 0)), pl.BlockSpec(selp.shape, lambda i: (0, 0))],
        out_specs=[pl.BlockSpec((L, SSD_CH), lambda i: (rev(i), 0)), pl.BlockSpec((L, LANE), lambda i: (rev(i), 0)),
                   pl.BlockSpec((SUBLANE, LANE), lambda i: (0, 0))],
        out_shape=[jax.ShapeDtypeStruct((S, SSD_CH), F32), jax.ShapeDtypeStruct((S, LANE), F32),
                   jax.ShapeDtypeStruct((SUBLANE, LANE), F32)],
        scratch_shapes=[pltpu.VMEM((SSD_PAIRS, LANE, SSD_STATE), F32), pltpu.VMEM((LANE, L), F32)],
        compiler_params=_cparams("arbitrary"),
    )(xc, rest, dtb, alog, dskip, hs, dy, selp)


def _gate_norm_fwd(y, rest, zcol, nw, *, name):
    S, W = y.shape
    ts = _row_tile(S)
    GW = W // SSD_GROUPS

    def body(y_ref, z_ref, nw_ref, o_ref):
        z = z_ref[...]
        t = y_ref[...] * (z * _sigmoid(z))
        for g in range(SSD_GROUPS):
            cols = slice(g * GW, (g + 1) * GW)
            tg = t[:, cols]
            rr = lax.rsqrt(jnp.mean(tg * tg, axis=-1, keepdims=True) + NORM_EPS)
            o_ref[:, cols] = (tg * rr * nw_ref[:, cols]).astype(BF16)

    row = pl.BlockSpec((ts, W), lambda i: (i, 0))
    return pl.pallas_call(
        body, name=name, grid=(S // ts,),
        in_specs=[row, pl.BlockSpec((ts, W), lambda i: (i, zcol)), pl.BlockSpec((1, W), lambda i: (0, 0))],
        out_specs=row,
        out_shape=jax.ShapeDtypeStruct((S, W), BF16),
        compiler_params=_cparams("parallel"),
    )(y, rest, nw)


def _gate_norm_bwd(y, rest, zcol, nw, do, *, name):
    S, W = y.shape
    ts = _row_tile(S)
    GW = W // SSD_GROUPS

    def body(y_ref, z_ref, nw_ref, do_ref, dy_ref, dz_ref, dnw_ref):
        z = z_ref[...]
        yv = y_ref[...]
        s, ds = _dsilu(z)
        sz = z * s
        t = yv * sz
        dov = do_ref[...]
        parts = []
        for g in range(SSD_GROUPS):
            cols = slice(g * GW, (g + 1) * GW)
            tg = t[:, cols]
            rr = lax.rsqrt(jnp.mean(tg * tg, axis=-1, keepdims=True) + NORM_EPS)
            n = tg * rr
            dog = dov[:, cols]
            dn = dog * nw_ref[:, cols]
            dt = rr * (dn - n * jnp.mean(dn * n, axis=-1, keepdims=True))
            dy_ref[:, cols] = dt * sz[:, cols]
            dz_ref[:, cols] = (dt * yv[:, cols] * ds[:, cols]).astype(BF16)
            parts.append(jnp.sum(dog * n, axis=0, keepdims=True))
        _acc_out(dnw_ref, jnp.concatenate(parts, axis=1), pl.program_id(0) == 0)

    row = pl.BlockSpec((ts, W), lambda i: (i, 0))
    vec = pl.BlockSpec((1, W), lambda i: (0, 0))
    return pl.pallas_call(
        body, name=name, grid=(S // ts,),
        in_specs=[row, pl.BlockSpec((ts, W), lambda i: (i, zcol)), vec, row],
        out_specs=[row, row, vec],
        out_shape=[jax.ShapeDtypeStruct((S, W), F32), jax.ShapeDtypeStruct((S, W), BF16),
                   jax.ShapeDtypeStruct((1, W), F32)],
        compiler_params=_cparams("arbitrary"),
    )(y, rest, nw, do)


N_DEV = 8
MESH = pl.DeviceIdType.MESH
ANY = pl.BlockSpec(memory_space=pl.ANY)


def _all_gather(xs, *, name):
    n = len(xs)

    def body(*refs):
        for phase in range(3):
            _gather_phase(phase, refs[:n], refs[n:2 * n], refs[2 * n:])

    return pl.pallas_call(
        body, name=name,
        out_shape=_a2a_out_shapes("gather", xs), in_specs=[ANY] * n, out_specs=[ANY] * n,
        scratch_shapes=_a2a_sems(n),
    )(*xs)


def _gather_phase(phase, x_refs, o_refs, sems):
    send_sems, recv_sems, local_sems = sems
    n = len(x_refs)
    px, py, pc = lax.axis_index("x"), lax.axis_index("y"), lax.axis_index("c")
    me, sibling = (px, py, pc), (px, py, 1 - pc)
    chips = [(1 - px, py), (px, 1 - py), (1 - px, 1 - py)]

    def copy(a, k, block, to, src=None):
        slot = o_refs[a].at[4 * block[0] + 2 * block[1] + block[2]]
        return pltpu.make_async_remote_copy(
            src_ref=slot if src is None else src, dst_ref=slot,
            send_sem=send_sems.at[a, k], recv_sem=recv_sems.at[a, k], device_id=to, device_id_type=MESH)

    def own(a):
        return ([copy(a, 0, me, sibling, src=x_refs[a])]
                + [copy(a, 1 + j, me, (*chip, pc), src=x_refs[a]) for j, chip in enumerate(chips)])

    mine = lambda a: pltpu.make_async_copy(x_refs[a], o_refs[a].at[4 * px + 2 * py + pc], local_sems.at[a])
    passed = lambda a, j: copy(a, 4 + j, (*chips[j], pc), sibling)
    if phase == 0:
        for a in range(n):
            mine(a).start()
            for cp in own(a):
                cp.start()
    elif phase == 1:
        for j, chip in enumerate(chips):
            for a in range(n):
                copy(a, 1 + j, (*chip, pc), me).wait_recv()
                passed(a, j).start()
    else:
        for a in range(n):
            copy(a, 0, sibling, me).wait_recv()
            for j, chip in enumerate(chips):
                copy(a, 4 + j, (*chip, 1 - pc), me).wait_recv()
            for cp in own(a) + [passed(a, j) for j in range(len(chips))]:
                cp.wait_send()
            mine(a).wait()


def _exchange(sends, *, name):
    n = len(sends)

    def body(*refs):
        _a2a_start(refs[:n], refs[n:2 * n], refs[2 * n:])
        _a2a_wait(refs[:n], refs[n:2 * n], refs[2 * n:])

    return pl.pallas_call(
        body, name=name,
        out_shape=_a2a_out_shapes("exchange", sends), in_specs=[ANY] * n, out_specs=[ANY] * n,
        scratch_shapes=_a2a_sems(n),
    )(*sends)


def _a2a_out_shapes(kind, arrays):
    if kind == "gather":
        return [jax.ShapeDtypeStruct((N_DEV,) + x.shape, x.dtype) for x in arrays]
    return [jax.ShapeDtypeStruct(x.shape, x.dtype) for x in arrays]


def _a2a_sems(n):
    return [pltpu.SemaphoreType.DMA((n, N_DEV - 1)), pltpu.SemaphoreType.DMA((n, N_DEV - 1)),
            pltpu.SemaphoreType.DMA((n,))]


def _a2a_copies(in_refs, out_refs, send_sems, recv_sems, local_sems, arrivals):
    px, py, pc = lax.axis_index("x"), lax.axis_index("y"), lax.axis_index("c")
    me = 4 * px + 2 * py + pc
    n = len(in_refs)
    src = lambda a, p: in_refs[a].at[p]
    local = [pltpu.make_async_copy(src(a, me), out_refs[a].at[me], local_sems.at[a]) for a in range(n)]
    sent, landing = [], []
    for k in range(1, N_DEV):
        qx = 1 - px if k & 4 else px
        qy = 1 - py if k & 2 else py
        qc = 1 - pc if k & 1 else pc
        peer = 4 * qx + 2 * qy + qc
        for a in range(n):
            def rdma(dst_slot):
                return pltpu.make_async_remote_copy(
                    src_ref=src(a, peer), dst_ref=out_refs[a].at[dst_slot],
                    send_sem=send_sems.at[a, k - 1], recv_sem=recv_sems.at[a, k - 1],
                    device_id=(qx, qy, qc), device_id_type=MESH)
            sent.append(rdma(me))
            if arrivals:
                landing.append(rdma(peer))
    return local, sent, landing


def _a2a_start(in_refs, out_refs, sems):
    local, sent, _ = _a2a_copies(in_refs, out_refs, *sems, arrivals=False)
    for cp in local + sent:
        cp.start()


def _a2a_wait(in_refs, out_refs, sems):
    local, sent, landing = _a2a_copies(in_refs, out_refs, *sems, arrivals=True)
    for cp in landing:
        cp.wait_recv()
    for cp in sent:
        cp.wait_send()
    for cp in local:
        cp.wait()


IN_POOL, IN_QKV, IN_F, IN_Z, IN_XBC, IN_DT, IN_GATE, IN_TOTAL = 0, 512, 2048, 2056, 3080, 4616, 4632, 7704
QKV_W = IN_F - IN_QKV
REST_Z_BLK, REST_POOL_BLK, REST_XBC0, REST_FDT_BLK = 3, 8, 4608, 48
REST_FDT0 = REST_FDT_BLK * LANE
REST_USED = REST_FDT0 + (IN_Z - IN_F) + (IN_GATE - IN_DT)
REST_W = REST_FDT0 + LANE


def _w_in_runs(shard):
    segs = [(0, IN_QKV, IN_F, 0), (1, IN_GATE, IN_TOTAL, 0), (1, IN_Z, IN_XBC, 3072), (1, IN_POOL, IN_QKV, 4096),
            (1, IN_XBC, IN_DT, REST_XBC0), (1, IN_F, IN_Z, REST_FDT0), (1, IN_DT, IN_GATE, REST_FDT0 + IN_Z - IN_F)]
    runs = []
    for dst, t0, t1, d0 in segs:
        for j in range(N_DEV):
            lo, hi = max(t0, shard * j), min(t1, shard * (j + 1))
            if lo < hi:
                runs.append((dst, d0 + lo - t0, j, lo - shard * j, hi - lo))
    return runs


def _repack_w_in(g, shard, *, name):
    _, L, K, P = g.shape
    tr = 256
    runs = _w_in_runs(shard)

    def body(g_ref, q_ref, r_ref):
        r_ref[:, REST_USED:] = jnp.zeros((tr, REST_W - REST_USED), g_ref.dtype)
        for dst, b, j, a, n in runs:
            (q_ref, r_ref)[dst][:, b:b + n] = g_ref[j, :, a:a + n]

    return pl.pallas_call(
        body, name=name, grid=(L, K // tr),
        in_specs=[pl.BlockSpec((N_DEV, None, tr, P), lambda l, i: (0, l, i, 0))],
        out_specs=[pl.BlockSpec((None, tr, QKV_W), lambda l, i: (l, i, 0)),
                   pl.BlockSpec((None, tr, REST_W), lambda l, i: (l, i, 0))],
        out_shape=[jax.ShapeDtypeStruct((L, K, QKV_W), g.dtype), jax.ShapeDtypeStruct((L, K, REST_W), g.dtype)],
        compiler_params=_cparams("parallel", "parallel"),
    )(g)


def _held_index(l, li, i, n):
    return jnp.where(l == li, i, jnp.where(l > li, n - 1, 0))


def _repack_dw_in(dwq, dwr, shard, *, name):
    L = len(dwq)
    K = dwq[0].shape[0]
    tr = 128
    nt = K // tr
    runs = _w_in_runs(shard)

    def body(*refs):
        srcs, o_ref = refs[:2 * L], refs[2 * L]
        l = pl.program_id(0)
        for li in range(L):
            @pl.when(l == li)
            def _():
                for dst, b, j, a, n in runs:
                    o_ref[j, :, a:a + n] = srcs[2 * li + dst][:, b:b + n]

    in_specs = []
    for li in range(L):
        hold = functools.partial(lambda li, l, i: (_held_index(l, li, i, nt), 0), li)
        in_specs += [pl.BlockSpec((tr, QKV_W), hold), pl.BlockSpec((tr, REST_W), hold)]
    args = [x for pair in zip(dwq, dwr) for x in pair]
    return pl.pallas_call(
        body, name=name, grid=(L, nt),
        in_specs=in_specs,
        out_specs=pl.BlockSpec((N_DEV, None, tr, shard), lambda l, i: (0, l, i, 0)),
        out_shape=jax.ShapeDtypeStruct((N_DEV, L, K, shard), dwq[0].dtype),
        compiler_params=_cparams("arbitrary", "arbitrary"),
    )(*args)


FLAT_W = 1024
ADAM_ROWS = 128


def _adam_math(w, g, m, v):
    m2 = ADAM_B1 * m + (1.0 - ADAM_B1) * g
    v2 = ADAM_B2 * v + (1.0 - ADAM_B2) * (g * g)
    m_hat = m2 / (1.0 - ADAM_B1 ** ADAM_STEP)
    v_hat = v2 / (1.0 - ADAM_B2 ** ADAM_STEP)
    delta = -ADAM_LR * (m_hat / (jnp.sqrt(v_hat) + ADAM_EPS) + ADAM_WD * w)
    return delta, m2, v2


def _row_div(R, align):
    best = R
    for t in range(align, min(R, ADAM_ROWS) + 1, align):
        if R % t == 0:
            best = t
    return best


def _sum8(ref_of):
    g = ref_of(0).astype(F32)
    for s in range(1, N_DEV):
        g = g + ref_of(s).astype(F32)
    return g


def _sum_adam(recvs, w, m, v, *, name):
    L, R, C = w.shape
    single = not isinstance(recvs, (list, tuple))
    recvs = [recvs] if single else list(recvs)
    tr = _row_div(R, SUBLANE * (4 // recvs[0].dtype.itemsize))
    nt = R // tr
    nr = len(recvs)

    def body(*refs):
        r_refs = refs[:nr]
        w_ref, m_ref, v_ref, g_ref, d_ref, m2_ref, v2_ref = refs[nr:]

        def run(ref_of):
            g = _sum8(ref_of)
            delta, m2, v2 = _adam_math(w_ref[...], g, m_ref[...], v_ref[...])
            g_ref[...] = g
            d_ref[...] = delta
            m2_ref[...] = m2
            v2_ref[...] = v2

        if single:
            run(lambda s: r_refs[0][s])
        else:
            l = pl.program_id(0)
            for li in range(L):
                @pl.when(l == li)
                def _():
                    run(lambda s: r_refs[li][s])

    if single:
        r_specs = [pl.BlockSpec((N_DEV, None, tr, C), lambda l, i: (0, l, i, 0))]
    else:
        r_specs = [pl.BlockSpec((N_DEV, tr, C), functools.partial(lambda li, l, i: (0, _held_index(l, li, i, nt), 0), li))
                   for li in range(L)]
    row = pl.BlockSpec((None, tr, C), lambda l, i: (l, i, 0))
    return pl.pallas_call(
        body, name=name, grid=(L, nt),
        in_specs=r_specs + [row, row, row],
        out_specs=[row] * 4,
        out_shape=[jax.ShapeDtypeStruct((L, R, C), F32)] * 4,
        compiler_params=_cparams("arbitrary", "arbitrary"),
    )(*recvs, w, m, v)


def _sum_parts(parts, *, name):
    _, R, C = parts.shape

    def body(p_ref, o_ref):
        o_ref[...] = _sum8(lambda s: p_ref[s])

    return pl.pallas_call(
        body, name=name, out_shape=jax.ShapeDtypeStruct((R, C), F32),
        in_specs=[pl.BlockSpec(memory_space=pltpu.VMEM)], out_specs=pl.BlockSpec(memory_space=pltpu.VMEM),
    )(parts)


def _adam(g, w, m, v, *, name):
    def body(g_ref, w_ref, m_ref, v_ref, d_ref, m2_ref, v2_ref):
        delta, m2, v2 = _adam_math(w_ref[...], g_ref[...], m_ref[...], v_ref[...])
        d_ref[...] = delta
        m2_ref[...] = m2
        v2_ref[...] = v2

    whole = pl.BlockSpec(memory_space=pltpu.VMEM)
    return pl.pallas_call(
        body, name=name, in_specs=[whole] * 4, out_specs=[whole] * 3,
        out_shape=[jax.ShapeDtypeStruct(g.shape, F32)] * 3,
    )(g, w, m, v)


SHARDED = ("w_in", "ffn_up", "p_pool", "p_attn", "p_ssd", "w_out", "ffn_down", "ssd_conv_w", "ffn_conv_w")
REPLICATED = ("norm_mix", "pool_mix", "pool_scale", "f_bias", "ssd_conv_b", "ssd_dt_bias", "ssd_a_log", "ssd_d",
              "ssd_norm", "norm_ffn", "ffn_conv_b", "norm_final")
WEIGHTS = ("norm_mix", "w_in", "pool_mix", "pool_scale", "f_bias", "ssd_conv_w", "ssd_conv_b", "ssd_dt_bias",
           "ssd_a_log", "ssd_d", "ssd_norm", "p_pool", "p_attn", "p_ssd", "w_out", "norm_ffn", "ffn_up",
           "ffn_conv_w", "ffn_conv_b", "ffn_down", "norm_final")


def _round_up(n, k):
    return -(-n // k) * k


def _pad_last(x, width):
    return jnp.pad(x, [(0, 0)] * (x.ndim - 1) + [(0, width - x.shape[-1])])


def _flat_rows(parts, rows):
    flat = jnp.concatenate([p.reshape(-1) for p in parts])
    return jnp.pad(flat, (0, rows * FLAT_W - flat.shape[0])).reshape(rows, FLAT_W)


def _lanes(v, lane0):
    return jnp.pad(v, (lane0, LANE - lane0 - v.shape[0]))[None]


def _cols_by_device(g):
    _, L, R, c = g.shape
    return jnp.moveaxis(g, 0, 2).reshape(L, R, N_DEV * c)


def _rows_by_device(g):
    _, L, r, C = g.shape
    return jnp.moveaxis(g, 0, 1).reshape(L, N_DEV * r, C)


def _gather_payload(W, l, shard_pad, up_pad):
    bf = lambda n: W[n][l].astype(BF16)
    return [_pad_last(bf("w_in"), shard_pad), _pad_last(bf("ffn_up"), up_pad), bf("p_pool"), bf("p_attn"),
            bf("p_ssd"), bf("w_out"), bf("ffn_down"), W["ssd_conv_w"][l], W["ffn_conv_w"][l]]


def _proj_params(g_win, W, l, shard):
    w_qkv, w_rest = _repack_w_in(g_win[:, None], shard, name=f"l{l}_repack_w_in")
    return dict(nm=W["norm_mix"][l][None], w_qkv=w_qkv[0], w_rest=w_rest[0], fb=_lanes(W["f_bias"][l], 0), shard=shard)


def _mixer_params(gathered, W, l, up_w, up_pad):
    g_up, g_pp, g_pa, g_ps, g_wo, g_dn, g_cw, g_fcw = [g[:, None] for g in gathered]
    half = N_DEV // 2
    w_down = _rows_by_device(g_dn)[0]
    w_down_p = jnp.pad(w_down.reshape(half, up_w, -1), ((0, 0), (0, up_pad - up_w), (0, 0))).reshape(half * up_pad, -1)
    fcb_p = _pad_last(W["ffn_conv_b"][l].reshape(N_DEV, up_w), up_pad).reshape(1, N_DEV * up_pad)
    return dict(
        mix=W["pool_mix"][l].astype(BF16), pscale=W["pool_scale"][l][None], cw=_cols_by_device(g_cw)[0],
        cb=W["ssd_conv_b"][l][None], dtb=_lanes(W["ssd_dt_bias"][l], DT_LANE0),
        alog=_lanes(W["ssd_a_log"][l], DT_LANE0), dsk=_lanes(W["ssd_d"][l], DT_LANE0), snw=W["ssd_norm"][l][None],
        p_pool=_cols_by_device(g_pp)[0], p_attn=_cols_by_device(g_pa)[0], p_ssd=_rows_by_device(g_ps)[0],
        w_out=_rows_by_device(g_wo)[0], nf=W["norm_ffn"][l][None], g_up=g_up, up_blocks=(N_DEV, up_w, up_pad),
        fcw=_cols_by_device(_pad_last(g_fcw, up_pad))[0], fcb=fcb_p, w_down=w_down_p)


def _layer_fwd(x, W, l, g_win, g_mixer, to_gather, dims):
    nm = lambda s: f"l{l}_{s}"
    P = _proj_params(g_win, W, l, dims[0])
    u = _rms_fwd(x, P["nm"], name=nm("rms_mix"))
    qkv = _mm(u, P["w_qkv"], out_dtype=BF16, name=nm("proj_qkv"))
    rest = _mm(u, P["w_rest"], name=nm("proj_rest"))
    ccol = _fox_pre(rest, REST_FDT_BLK, P["fb"], name=nm("fox_pre"))
    o, lrow, got = _attn_fwd(qkv, ccol, comm=("gather", to_gather) if to_gather else None, name=nm("attn"))
    if g_mixer is None:
        g_mixer, got = got[:len(SHARDED) - 1], got[len(SHARDED) - 1:]
    P.update(_mixer_params(g_mixer, W, l, *dims[1:]))
    d, ya0 = _pool_fwd(rest, REST_POOL_BLK, P["mix"], P["pscale"], name=nm("pool"))
    ya = _mm(ya0, P["p_pool"], name=nm("p_pool"))
    yb = _mm(o, P["p_attn"], name=nm("p_attn"))
    xc = _conv_silu_fwd(rest, REST_XBC0, SSD_CH, P["cw"], P["cb"], name=nm("ssd_conv"))
    y, hs = _ssd_fwd(xc, rest, REST_FDT_BLK, P["dtb"], P["alog"], P["dsk"], name=nm("ssd"))
    yc0 = _gate_norm_fwd(y, rest, REST_Z_BLK, P["snw"], name=nm("ssd_norm"))
    yc = _mm(yc0, P["p_ssd"], name=nm("p_ssd"))
    merged = _merge_fwd(rest, ya, yb, yc, name=nm("merge"))
    x1 = _mm(merged, P["w_out"], add=x, name=nm("w_out"))
    u2 = _rms_fwd(x1, P["nf"], name=nm("rms_ffn"))
    h0 = _mm_up_fwd(u2, P["g_up"], 0, name=nm("ffn_up"))
    a = _ffn_act_fwd(h0, P["fcw"], P["fcb"], name=nm("ffn_act"))
    x2 = _mm(a, P["w_down"], add=x1, name=nm("ffn_down"))
    saved = dict(x=x, u=u, qkv=qkv, rest=rest, d=d, ya0=ya0, ya=ya, ccol=ccol, o=o, lrow=lrow, yb=yb,
                 xc=xc, y=y, hs=hs, yc0=yc0, yc=yc, merged=merged, x1=x1, u2=u2, h0=h0, a=a)
    return x2, saved, P, got


EARLY_GRADS = ("ffn_up", "ffn_down", "ffn_conv_w", "p_pool", "p_attn", "p_ssd", "w_out")
LATE_GRADS = ("w_in", "ssd_conv_w")


def _layer_bwd(g, gb, sv, P, l):
    nm = lambda s: f"l{l}_{s}_bwd"
    rest = sv["rest"]
    G = {}
    up_blocks, up_w, up_pad = P["up_blocks"]
    half = up_blocks // 2
    da = _mm(gb, P["w_down"], tb=True, name=nm("ffn_down_dx"))
    dwd = _mm(sv["a"], gb, ta=True, out_dtype=GRAD_WIRE, name=nm("ffn_down_dw"))
    G["ffn_down"] = dwd.reshape(half, up_pad, -1)[:, :up_w].reshape(N_DEV, -1, dwd.shape[1])
    dh, dfcw, dfcb = _ffn_act_bwd(sv["h0"], P["fcw"], P["fcb"], da, name=nm("ffn_act"))
    du2 = _mm_up_dx(dh, P["g_up"], 0, name=nm("ffn_up_dx"))
    G["ffn_up"] = _mm_up_dw(sv["u2"], dh, up_blocks, up_w, name=nm("ffn_up_dw"))
    taps = dfcw.shape[1]
    G["ffn_conv_w"] = jnp.moveaxis(dfcw.reshape(2, taps, half, up_pad)[..., :up_w], 2, 1).reshape(N_DEV, taps, up_w)
    G["ffn_conv_b"] = dfcb.reshape(2, half, up_pad)[..., :up_w].reshape(-1)
    dx1, dx1b, dnf = _rms_bwd(sv["x1"], P["nf"], du2, g, name=nm("rms_ffn"))
    G["norm_ffn"] = dnf[0]
    dmerged = _mm(dx1b, P["w_out"], tb=True, name=nm("w_out_dx"))
    dwo = _mm(sv["merged"], dx1b, ta=True, out_dtype=GRAD_WIRE, name=nm("w_out_dw"))
    G["w_out"] = dwo.reshape(N_DEV, -1, dwo.shape[1])
    dgl, dya, dyb, dyc = _merge_bwd(rest, sv["ya"], sv["yb"], sv["yc"], dmerged, name=nm("merge"))
    col_blocks = lambda dw: jnp.moveaxis(dw.reshape(dw.shape[0], N_DEV, -1), 1, 0)
    G["p_pool"] = col_blocks(_mm(sv["ya0"], dya, ta=True, out_dtype=GRAD_WIRE, name=nm("p_pool_dw")))
    dya0 = _mm(dya, P["p_pool"], tb=True, name=nm("p_pool_dx"))
    G["p_attn"] = col_blocks(_mm(sv["o"], dyb, ta=True, out_dtype=GRAD_WIRE, name=nm("p_attn_dw")))
    do = _mm(dyb, P["p_attn"], tb=True, out_dtype=BF16, name=nm("p_attn_dx"))
    dps = _mm(sv["yc0"], dyc, ta=True, out_dtype=GRAD_WIRE, name=nm("p_ssd_dw"))
    G["p_ssd"] = dps.reshape(N_DEV, -1, dps.shape[1])
    dyc0 = _mm(dyc, P["p_ssd"], tb=True, name=nm("p_ssd_dx"))
    dpv, dmix, dpsc = _pool_bwd(dya0, sv["d"], P["mix"], P["pscale"], name=nm("pool"))
    G["pool_mix"], G["pool_scale"] = dmix, dpsc[0]
    dq, dk, dv, dcq, dck, got = _attn_bwd(sv["qkv"], sv["o"], do, sv["lrow"], sv["ccol"],
                                          comm=("exchange", [G[n] for n in EARLY_GRADS]), name=nm("attn"))
    received = {(n, l): r for n, r in zip(EARLY_GRADS, got)}
    dy, dz, dsnw = _gate_norm_bwd(sv["y"], rest, REST_Z_BLK, P["snw"], dyc0, name=nm("ssd_norm"))
    G["ssd_norm"] = dsnw[0]
    dxc, ddt, dpar = _ssd_bwd(sv["xc"], rest, REST_FDT_BLK, P["dtb"], P["alog"], P["dsk"], sv["hs"], dy, name=nm("ssd"))
    G["ssd_a_log"] = dpar[0, DT_LANE0:DT_LANE0 + SSD_HEADS]
    G["ssd_d"] = dpar[1, DT_LANE0:DT_LANE0 + SSD_HEADS]
    dxbc, dcw, dcb = _conv_silu_bwd(rest, REST_XBC0, SSD_CH, P["cw"], P["cb"], dxc, name=nm("ssd_conv"))
    G["ssd_conv_w"] = jnp.moveaxis(dcw.reshape(dcw.shape[0], N_DEV, -1), 1, 0)
    G["ssd_conv_b"] = dcb[0]
    dfdt, dfdtb = _fox_post(dcq, dck, rest, REST_FDT_BLK, P["fb"], ddt, name=nm("fox_post"))
    G["f_bias"] = dfdtb[0, :ATTN_HEADS]
    G["ssd_dt_bias"] = dfdtb[0, DT_LANE0:DT_LANE0 + SSD_HEADS]
    dqkv = jnp.concatenate([dq, dk, dv], axis=1)
    drest = jnp.concatenate([dgl, dz, dpv, dxbc, dfdt], axis=1)
    dwq = _mm(sv["u"], dqkv, ta=True, out_dtype=GRAD_WIRE, name=nm("proj_qkv_dw"))
    dwr = _mm(sv["u"], drest, ta=True, out_dtype=GRAD_WIRE, name=nm("proj_rest_dw"))
    G["w_in"] = _repack_dw_in([dwq], [dwr], P["shard"], name=nm("repack_dw_in"))[:, 0]
    du = _mm(dqkv, P["w_qkv"], tb=True, name=nm("proj_qkv_dx"))
    du, got = _mm(drest, P["w_rest"], tb=True, add=du, comm=("exchange", [G[n] for n in LATE_GRADS]),
                  name=nm("proj_rest_dx"))
    received.update({(n, l): r for n, r in zip(LATE_GRADS, got)})
    dx, dxb, dnm = _rms_bwd(sv["x"], P["nm"], du, dx1, name=nm("rms_mix"))
    G["norm_mix"] = dnm[0]
    return dx, dxb, G, received


def kernel(x, norm_mix, w_in, pool_mix, pool_scale, f_bias, ssd_conv_w, ssd_conv_b, ssd_dt_bias, ssd_a_log, ssd_d, ssd_norm, p_pool, p_attn, p_ssd, w_out, norm_ffn, ffn_up, ffn_conv_w, ffn_conv_b, ffn_down, norm_final, loss_target, m_norm_mix, m_w_in, m_pool_mix, m_pool_scale, m_f_bias, m_ssd_conv_w, m_ssd_conv_b, m_ssd_dt_bias, m_ssd_a_log, m_ssd_d, m_ssd_norm, m_p_pool, m_p_attn, m_p_ssd, m_w_out, m_norm_ffn, m_ffn_up, m_ffn_conv_w, m_ffn_conv_b, m_ffn_down, m_norm_final, v_norm_mix, v_w_in, v_pool_mix, v_pool_scale, v_f_bias, v_ssd_conv_w, v_ssd_conv_b, v_ssd_dt_bias, v_ssd_a_log, v_ssd_d, v_ssd_norm, v_p_pool, v_p_attn, v_p_ssd, v_w_out, v_norm_ffn, v_ffn_up, v_ffn_conv_w, v_ffn_conv_b, v_ffn_down, v_norm_final):
    args = dict(locals())
    W = {n: args[n] for n in WEIGHTS}
    M = {n: args["m_" + n] for n in WEIGHTS}
    V = {n: args["v_" + n] for n in WEIGHTS}
    L = norm_mix.shape[0]
    shard = w_in.shape[-1]
    up_w = ffn_up.shape[-1]
    up_pad = _round_up(up_w, LANE)

    payload = lambda l: _gather_payload(W, l, _round_up(shard, LANE), up_pad)
    first = payload(0)
    g_win = _all_gather(first[:1], name="gather_w_in_l0")[0]
    g_mixer, to_gather = None, first[1:]
    h, saves, params = x[0], [], []
    for l in range(L):
        to_gather = to_gather + (payload(l + 1) if l + 1 < L else [])
        h, sv, P, got = _layer_fwd(h, W, l, g_win, g_mixer, to_gather, (shard, up_w, up_pad))
        saves.append(sv)
        params.append(P)
        if l + 1 < L:
            g_win, g_mixer, to_gather = got[0], got[1:], []
    loss_part, g, gb, dnfin = _loss_head(h, norm_final[None], loss_target[0], name="loss_head")

    G, received = [None] * L, {}
    for l in reversed(range(L)):
        g, gb, G[l], got = _layer_bwd(g, gb, saves[l], params[l], l)
        received.update(got)

    rep_grads = {n: jnp.stack([G[l][n] for l in range(L)]) for n in REPLICATED if n != "norm_final"}
    rep_grads["norm_final"] = dnfin[0]
    rep_size = sum(W[n].size for n in REPLICATED)
    rep_rows = _round_up(-(-rep_size // (N_DEV * FLAT_W)), SUBLANE)
    rep_send = _flat_rows([rep_grads[n] for n in REPLICATED], N_DEV * rep_rows).reshape(N_DEV, rep_rows, FLAT_W)
    last = _exchange([rep_send], name="exchange_replicated_grads")

    out = {}
    for n in SHARDED:
        shape = W[n].shape
        as3 = lambda a: a.reshape(L, -1, shape[-1])
        res = _sum_adam([received[(n, l)] for l in range(L)], as3(W[n]), as3(M[n]), as3(V[n]), name="sum_adam_" + n)
        for key, arr in zip(("grad_", "delta_", "new_m_", "new_v_"), res):
            out[key + n] = arr.reshape(shape)

    rep_sum = _sum_parts(last[-1], name="sum_replicated")
    rep_g = _all_gather([rep_sum], name="gather_replicated_grads")[0].reshape(N_DEV * rep_rows, FLAT_W)
    flat_rep = lambda D: _flat_rows([D[n] for n in REPLICATED], N_DEV * rep_rows)
    d_rp, m_rp, v_rp = _adam(rep_g, flat_rep(W), flat_rep(M), flat_rep(V), name="adam_replicated")
    off = 0
    for n in REPLICATED:
        size = W[n].size
        for key, arr in (("grad_", rep_g), ("delta_", d_rp), ("new_m_", m_rp), ("new_v_", v_rp)):
            out[key + n] = arr.reshape(-1)[off:off + size].reshape(W[n].shape)
        off += size

    loss = lax.psum(loss_part[0, 0], ("x", "y", "c"))
    return (loss, g[None], *[out["grad_" + n] for n in WEIGHTS], *[out["delta_" + n] for n in WEIGHTS],
            *[out["new_m_" + n] for n in WEIGHTS], *[out["new_v_" + n] for n in WEIGHTS])
```

```python
import functools

import jax
import jax.numpy as jnp
from jax import lax
from jax.experimental import pallas as pl
from jax.experimental.pallas import tpu as pltpu

F32 = jnp.float32
BF16 = jnp.bfloat16

LANE = 128
SUBLANE = 8
VMEM_LIMIT = 56 * 1024 * 1024

NORM_EPS = 1e-6
POOL_WINDOWS = (2, 4, 8, 16)
POOL_GROUP_DIM = 128
ATTN_HEADS = 8
ATTN_HEAD_DIM = 64
SSD_HEAD_DIM = 64
SSD_HEADS = 16
SSD_GROUPS = 2
SSD_STATE = 128
SSD_CHUNK = 128

ADAM_LR = 0.001
ADAM_B1 = 0.9
ADAM_B2 = 0.999
ADAM_EPS = 1e-08
ADAM_WD = 0.01
ADAM_STEP = 10


def _cparams(*sem):
    return pltpu.CompilerParams(dimension_semantics=tuple(sem), vmem_limit_bytes=VMEM_LIMIT)


def _tile(n, pref):
    if n <= pref:
        return n
    assert n % LANE == 0, n
    q = n // LANE
    best = 1
    for d in range(1, pref // LANE + 1):
        if q % d == 0:
            best = d
    return best * LANE


GRAD_WIRE = BF16


def _mm_call(args, in_specs, out_spec, out_shape, grid, *, ta, tb, nk, acc_shape, has_add, keep, name, comm=None):
    dims = (((0 if ta else 1,), (1 if tb else 0,)), ((), ()))
    nc = len(comm[1]) if comm else 0
    n_in = 2 + has_add

    def body(*refs):
        a_ref, b_ref = refs[0], refs[1]
        o_ref = refs[n_in + nc]
        acc_ref = refs[n_in + 2 * nc + 1] if nk > 1 else None
        carried = (refs[n_in:n_in + nc], refs[n_in + nc + 1:n_in + 2 * nc + 1], refs[n_in + 2 * nc + 1 + (nk > 1):])
        i, j, k = pl.program_id(0), pl.program_id(1), pl.program_id(2)
        if comm:
            _carry_start(comm[0], *carried, jnp.logical_and(jnp.logical_and(i == 0, j == 0), k == 0))
        d = lax.dot_general(a_ref[...], b_ref[...], dims, preferred_element_type=F32)

        def finish(r):
            if has_add:
                r = r + refs[2][...]
            if keep is not None:
                r = r[:, :keep]
            o_ref[...] = r.astype(o_ref.dtype)

        if nk == 1:
            finish(d)
        else:
            @pl.when(k == 0)
            def _():
                acc_ref[...] = d

            @pl.when(jnp.logical_and(k > 0, k < nk - 1))
            def _():
                acc_ref[...] += d

            @pl.when(k == nk - 1)
            def _():
                finish(acc_ref[...] + d)

        if comm:
            _carry_wait(comm[0], *carried, jnp.logical_and(jnp.logical_and(i == grid[0] - 1, j == grid[1] - 1),
                                                           k == nk - 1))

    res = pl.pallas_call(
        body, name=name, grid=grid, in_specs=in_specs + [ANY] * nc, out_specs=[out_spec] + [ANY] * nc,
        out_shape=[out_shape] + (_a2a_out_shapes(*comm) if comm else []),
        scratch_shapes=([pltpu.VMEM(acc_shape, F32)] if nk > 1 else []) + (_a2a_sems(nc) if comm else []),
        compiler_params=_cparams(*(("arbitrary",) * 3 if comm else ("parallel", "parallel", "arbitrary"))),
    )(*args, *(comm[1] if comm else []))
    return (res[0], list(res[1:])) if comm else res[0]


MM_VMEM_BUDGET = 40 * 1024 * 1024
MM_MAX_ROWS = 2048


def _mm_rows(M, tn, tk, nk, out_bytes, has_add):
    best = None
    for tm in range(LANE, min(M, MM_MAX_ROWS) + 1, LANE):
        if M % tm:
            continue
        blocks = 2 * (tm * tk * 2 + tk * tn * 2 + tm * tn * out_bytes + (tm * tn * 4 if has_add else 0))
        blocks += tm * tn * 4 * (2 if nk > 1 else 1)
        if blocks <= MM_VMEM_BUDGET:
            best = tm
    return M if best is None else best


def _mm(a, b, *, ta=False, tb=False, add=None, out_dtype=F32, comm=None, name):
    assert a.dtype == BF16 and b.dtype == BF16, (a.dtype, b.dtype)
    if ta:
        K, M = a.shape
    else:
        M, K = a.shape
    if tb:
        N, K2 = b.shape
    else:
        K2, N = b.shape
    assert K == K2, (a.shape, b.shape, ta, tb)
    tn, tk = _tile(N, 1024), _tile(K, 2048)
    tm = _mm_rows(M, tn, tk, K // tk, jnp.dtype(out_dtype).itemsize, add is not None)
    a_spec =pl.BlockSpec((tk, tm), lambda i, j, k: (k, i)) if ta else pl.BlockSpec((tm, tk), lambda i, j, k: (i, k))
    b_spec = pl.BlockSpec((tn, tk), lambda i, j, k: (j, k)) if tb else pl.BlockSpec((tk, tn), lambda i, j, k: (k, j))
    in_specs, args = [a_spec, b_spec], [a, b]
    if add is not None:
        in_specs.append(pl.BlockSpec((tm, tn), lambda i, j, k: (i, j)))
        args.append(add)
    return _mm_call(args, in_specs, pl.BlockSpec((tm, tn), lambda i, j, k: (i, j)),
                    jax.ShapeDtypeStruct((M, N), out_dtype), (M // tm, N // tn, K // tk),
                    ta=ta, tb=tb, nk=K // tk, acc_shape=(tm, tn), has_add=add is not None, keep=None, name=name,
                    comm=comm)


def _half_of(b, half):
    hi = jnp.where(b >= half, 1, 0)
    return hi, b - half * hi


def _mm_up_fwd(u, g_up, l, *, name):
    S, K = u.shape
    nblk, _, _, bw = g_up.shape
    tm = _mm_rows(S, bw, K, 1, 4, False)
    return _mm_call([u, g_up],
                    [pl.BlockSpec((tm, K), lambda i, j, k: (i, 0)),
                     pl.BlockSpec((None, None, K, bw), lambda i, j, k: (j, l, 0, 0))],
                    pl.BlockSpec((tm, bw), lambda i, j, k: (i, j)),
                    jax.ShapeDtypeStruct((S, nblk * bw), F32), (S // tm, nblk, 1),
                    ta=False, tb=False, nk=1, acc_shape=(tm, bw), has_add=False, keep=None, name=name)


def _mm_up_dx(dh, g_up, l, *, name):
    _, S, _ = dh.shape
    nblk, _, N, bw = g_up.shape
    half = nblk // 2
    tm = _mm_rows(S, N, bw, nblk, 4, False)
    return _mm_call([dh, g_up],
                    [pl.BlockSpec((None, tm, bw), lambda i, j, k: (_half_of(k, half)[0], i, _half_of(k, half)[1])),
                     pl.BlockSpec((None, None, N, bw), lambda i, j, k: (k, l, 0, 0))],
                    pl.BlockSpec((tm, N), lambda i, j, k: (i, 0)),
                    jax.ShapeDtypeStruct((S, N), F32), (S // tm, 1, nblk),
                    ta=False, tb=True, nk=nblk, acc_shape=(tm, N), has_add=False, keep=None, name=name)


def _mm_up_dw(u, dh, nblk, keep, *, name):
    S, M = u.shape
    half = nblk // 2
    bw = dh.shape[2] // half
    tk = _tile(S, 2048)
    tm = _mm_rows(M, bw, tk, S // tk, jnp.dtype(GRAD_WIRE).itemsize, False)
    return _mm_call([u, dh],
                    [pl.BlockSpec((tk, tm), lambda i, j, k: (k, i)),
                     pl.BlockSpec((None, tk, bw), lambda i, j, k: (_half_of(j, half)[0], k, _half_of(j, half)[1]))],
                    pl.BlockSpec((None, tm, keep), lambda i, j, k: (j, i, 0)),
                    jax.ShapeDtypeStruct((nblk, M, keep), GRAD_WIRE), (M // tm, nblk, S // tk),
                    ta=True, tb=False, nk=S // tk, acc_shape=(tm, bw), has_add=False, keep=keep, name=name)


ROW_TILE = 512
HALO = 8
POOL_HALO = 16


def _row_tile(S):
    return min(ROW_TILE, S)


def _acc_out(ref, val, first):
    @pl.when(first)
    def _():
        ref[...] = val

    @pl.when(jnp.logical_not(first))
    def _():
        ref[...] += val


def _rms_fwd(x, w, *, name):
    S, D = x.shape
    ts = _row_tile(S)

    def body(x_ref, w_ref, u_ref):
        xv = x_ref[...]
        r = lax.rsqrt(jnp.mean(xv * xv, axis=-1, keepdims=True) + NORM_EPS)
        u_ref[...] = (xv * r * w_ref[...]).astype(BF16)

    return pl.pallas_call(
        body, name=name, grid=(S // ts,),
        in_specs=[pl.BlockSpec((ts, D), lambda i: (i, 0)), pl.BlockSpec((1, D), lambda i: (0, 0))],
        out_specs=pl.BlockSpec((ts, D), lambda i: (i, 0)),
        out_shape=jax.ShapeDtypeStruct((S, D), BF16),
        compiler_params=_cparams("parallel"),
    )(x, w)


def _rms_bwd(x, w, du, g, *, name):
    S, D = x.shape
    ts = _row_tile(S)

    def body(x_ref, w_ref, du_ref, g_ref, dx_ref, dxb_ref, dw_ref):
        xv = x_ref[...]
        r = lax.rsqrt(jnp.mean(xv * xv, axis=-1, keepdims=True) + NORM_EPS)
        n = xv * r
        duv = du_ref[...]
        dn = duv * w_ref[...]
        dx = g_ref[...] + r * (dn - n * jnp.mean(dn * n, axis=-1, keepdims=True))
        dx_ref[...] = dx
        dxb_ref[...] = dx.astype(BF16)
        _acc_out(dw_ref, jnp.sum(duv * n, axis=0, keepdims=True), pl.program_id(0) == 0)

    row = pl.BlockSpec((ts, D), lambda i: (i, 0))
    vec = pl.BlockSpec((1, D), lambda i: (0, 0))
    return pl.pallas_call(
        body, name=name, grid=(S // ts,),
        in_specs=[row, vec, row, row],
        out_specs=[row, row, vec],
        out_shape=[jax.ShapeDtypeStruct((S, D), F32), jax.ShapeDtypeStruct((S, D), BF16),
                   jax.ShapeDtypeStruct((1, D), F32)],
        compiler_params=_cparams("arbitrary"),
    )(x, w, du, g)


def _loss_head(x, w, target, *, name):
    S, D = x.shape
    ts = _row_tile(S)

    def body(x_ref, w_ref, t_ref, loss_ref, dx_ref, dxb_ref, dw_ref):
        xv = x_ref[...]
        wv = w_ref[...]
        r = lax.rsqrt(jnp.mean(xv * xv, axis=-1, keepdims=True) + NORM_EPS)
        n = xv * r
        err = n * wv - t_ref[...]
        part = 0.5 * jnp.sum(jnp.mean(err * err, axis=-1, keepdims=True), axis=0, keepdims=True)
        dy = err * (1.0 / D)
        dn = dy * wv
        dx = r * (dn - n * jnp.mean(dn * n, axis=-1, keepdims=True))
        dx_ref[...] = dx
        dxb_ref[...] = dx.astype(BF16)
        first = pl.program_id(0) == 0
        _acc_out(dw_ref, jnp.sum(dy * n, axis=0, keepdims=True), first)
        _acc_out(loss_ref, jnp.broadcast_to(part, loss_ref.shape), first)

    row = pl.BlockSpec((ts, D), lambda i: (i, 0))
    vec = pl.BlockSpec((1, D), lambda i: (0, 0))
    return pl.pallas_call(
        body, name=name, grid=(S // ts,),
        in_specs=[row, vec, row],
        out_specs=[pl.BlockSpec((SUBLANE, LANE), lambda i: (0, 0)), row, row, vec],
        out_shape=[jax.ShapeDtypeStruct((SUBLANE, LANE), F32), jax.ShapeDtypeStruct((S, D), F32),
                   jax.ShapeDtypeStruct((S, D), BF16), jax.ShapeDtypeStruct((1, D), F32)],
        compiler_params=_cparams("arbitrary"),
    )(x, w, target)


def _sigmoid(x):
    return 1.0 / (1.0 + jnp.exp(-x))


def _merge_fwd(gl, ya, yb, yc, *, name):
    S, D = ya.shape
    ts = _row_tile(S)

    def body(ga_ref, gb_ref, gc_ref, ya_ref, yb_ref, yc_ref, o_ref):
        m = (_sigmoid(ga_ref[...]) * ya_ref[...] + _sigmoid(gb_ref[...]) * yb_ref[...]
             + _sigmoid(gc_ref[...]) * yc_ref[...])
        o_ref[...] = m.astype(BF16)

    row = pl.BlockSpec((ts, D), lambda i: (i, 0))
    gspec = [pl.BlockSpec((ts, D), functools.partial(lambda b, i: (i, b), b)) for b in range(3)]
    return pl.pallas_call(
        body, name=name, grid=(S // ts,),
        in_specs=gspec + [row, row, row],
        out_specs=row,
        out_shape=jax.ShapeDtypeStruct((S, D), BF16),
        compiler_params=_cparams("parallel"),
    )(gl, gl, gl, ya, yb, yc)


def _merge_bwd(gl, ya, yb, yc, dm, *, name):
    S, D = ya.shape
    ts = _row_tile(S)

    def body(ga_ref, gb_ref, gc_ref, ya_ref, yb_ref, yc_ref, dm_ref, dgl_ref, da_ref, db_ref, dc_ref):
        dmv = dm_ref[...]
        for b, (g_ref, y_ref, dy_ref) in enumerate(((ga_ref, ya_ref, da_ref), (gb_ref, yb_ref, db_ref),
                                                     (gc_ref, yc_ref, dc_ref))):
            s = _sigmoid(g_ref[...])
            dy_ref[...] = (dmv * s).astype(BF16)
            dgl_ref[:, b * D:(b + 1) * D] = (dmv * y_ref[...] * s * (1.0 - s)).astype(BF16)

    row = pl.BlockSpec((ts, D), lambda i: (i, 0))
    gspec = [pl.BlockSpec((ts, D), functools.partial(lambda b, i: (i, b), b)) for b in range(3)]
    return pl.pallas_call(
        body, name=name, grid=(S // ts,),
        in_specs=gspec + [row, row, row, row],
        out_specs=[pl.BlockSpec((ts, 3 * D), lambda i: (i, 0)), row, row, row],
        out_shape=[jax.ShapeDtypeStruct((S, 3 * D), BF16)] + [jax.ShapeDtypeStruct((S, D), BF16)] * 3,
        compiler_params=_cparams("parallel"),
    )(gl, gl, gl, ya, yb, yc, dm)


POOL_WIDTH = 512


def _pool_cnt(t, w):
    return jnp.minimum(t + 1, w).astype(F32)


def _pool_fwd(rest, col, mix, scale, *, name):
    S = rest.shape[0]
    W, G, H = POOL_WIDTH, POOL_GROUP_DIM, POOL_HALO
    ts = _row_tile(S)

    def body(v_ref, h_ref, mix_ref, sc_ref, d_ref, y_ref, ext_ref):
        i = pl.program_id(0)
        cur = v_ref[...]
        ext_ref[0:H, :] = jnp.where(i > 0, h_ref[...], 0.0)
        ext_ref[H:, :] = cur
        t = i * ts + lax.broadcasted_iota(jnp.int32, (ts, 1), 0)
        for g, w in enumerate(POOL_WINDOWS):
            cols = slice(g * G, (g + 1) * G)
            acc = cur[:, cols]
            for k in range(1, w):
                acc = acc + ext_ref[pl.ds(H - k, ts), cols]
            db = (acc / _pool_cnt(t, w) - cur[:, cols]).astype(BF16)
            d_ref[:, cols] = db
            y = jnp.dot(db, mix_ref[g], preferred_element_type=F32) * sc_ref[:, cols]
            y_ref[:, cols] = y.astype(BF16)

    row = pl.BlockSpec((ts, W), lambda i: (i, 0))
    return pl.pallas_call(
        body, name=name, grid=(S // ts,),
        in_specs=[pl.BlockSpec((ts, W), lambda i: (i, col)),
                  pl.BlockSpec((H, W), lambda i: (jnp.maximum(i * (ts // H) - 1, 0), col)),
                  pl.BlockSpec((len(POOL_WINDOWS), G, G), lambda i: (0, 0, 0)),
                  pl.BlockSpec((1, W), lambda i: (0, 0))],
        out_specs=[row, row],
        out_shape=[jax.ShapeDtypeStruct((S, W), BF16)] * 2,
        scratch_shapes=[pltpu.VMEM((H + ts, W), F32)],
        compiler_params=_cparams("parallel"),
    )(rest, rest, mix, scale)


def _pool_bwd(dy, d, mix, scale, *, name):
    S = dy.shape[0]
    W, G, H = POOL_WIDTH, POOL_GROUP_DIM, POOL_HALO
    ts = _row_tile(S)
    n = S // ts
    NT = (((1,), (1,)), ((), ()))
    TN = (((0,), (0,)), ((), ()))

    def body(dy_ref, dyn_ref, d_ref, mix_ref, sc_ref, dv_ref, dmix_ref, dsc_ref, ext_ref):
        i = pl.program_id(0)

        @pl.when(i == 0)
        def _():
            dmix_ref[...] = jnp.zeros_like(dmix_ref)
            dsc_ref[...] = jnp.zeros_like(dsc_ref)

        t = i * ts + lax.broadcasted_iota(jnp.int32, (ts, 1), 0)
        tn = (i + 1) * ts + lax.broadcasted_iota(jnp.int32, (H, 1), 0)
        for g, w in enumerate(POOL_WINDOWS):
            cols = slice(g * G, (g + 1) * G)
            sc = sc_ref[:, cols]
            dyv = dy_ref[:, cols]
            db = d_ref[:, cols]
            mg = mix_ref[g]
            yp = jnp.dot(db, mg, preferred_element_type=F32)
            dsc_ref[:, cols] += jnp.sum(dyv * yp, axis=0, keepdims=True)
            dyp = (dyv * sc).astype(BF16)
            dd = lax.dot_general(dyp, mg, NT, preferred_element_type=F32)
            dmix_ref[g] += lax.dot_general(db, dyp, TN, preferred_element_type=F32)
            dyn = (jnp.where(i < n - 1, dyn_ref[:, cols], 0.0) * sc).astype(BF16)
            ddn = lax.dot_general(dyn, mg, NT, preferred_element_type=F32)
            e = dd / _pool_cnt(t, w)
            ext_ref[0:ts, cols] = e
            ext_ref[ts:, cols] = ddn / _pool_cnt(tn, w)
            acc = e
            for k in range(1, w):
                acc = acc + ext_ref[pl.ds(k, ts), cols]
            dv_ref[:, cols] = (acc - dd).astype(BF16)

    row = pl.BlockSpec((ts, W), lambda i: (i, 0))
    return pl.pallas_call(
        body, name=name, grid=(n,),
        in_specs=[row,
                  pl.BlockSpec((H, W), lambda i: (jnp.minimum((i + 1) * (ts // H), S // H - 1), 0)),
                  row,
                  pl.BlockSpec((len(POOL_WINDOWS), G, G), lambda i: (0, 0, 0)),
                  pl.BlockSpec((1, W), lambda i: (0, 0))],
        out_specs=[row, pl.BlockSpec((len(POOL_WINDOWS), G, G), lambda i: (0, 0, 0)),
                   pl.BlockSpec((1, W), lambda i: (0, 0))],
        out_shape=[jax.ShapeDtypeStruct((S, W), BF16),
                   jax.ShapeDtypeStruct((len(POOL_WINDOWS), G, G), F32),
                   jax.ShapeDtypeStruct((1, W), F32)],
        scratch_shapes=[pltpu.VMEM((ts + H, W), F32)],
        compiler_params=_cparams("arbitrary"),
    )(dy, dy, d, mix, scale)


def _conv_taps(ext_ref, w_ref, b_ref, rows, first):
    K = w_ref.shape[0]
    pre = b_ref[...] + w_ref[K - 1:K, :] * ext_ref[pl.ds(first, rows), :]
    for k in range(K - 1):
        pre = pre + w_ref[k:k + 1, :] * ext_ref[pl.ds(first - (K - 1) + k, rows), :]
    return pre


CONV_ROW_TILE = 1024
ROW_CHUNK = 40


def _chunks(rows):
    ch = max(c for c in range(SUBLANE, ROW_CHUNK + 1, SUBLANE) if rows % c == 0)
    return [(r0, ch) for r0 in range(0, rows, ch)]


def _conv_bwd_taps(g_ref, ext_ref, w_ref, dx_store, ts):
    K = w_ref.shape[0]
    db, dws = 0.0, [0.0] * K
    fold = lambda v: jnp.sum(v.reshape(v.shape[0] // SUBLANE, SUBLANE, v.shape[1]), axis=0)
    for r0, ch in _chunks(ts):
        dx = w_ref[K - 1:K, :] * g_ref[pl.ds(r0, ch), :]
        for k in range(K - 1):
            dx = dx + w_ref[k:k + 1, :] * g_ref[pl.ds(r0 + K - 1 - k, ch), :]
        dx_store(r0, ch, dx.astype(BF16))
        gc = g_ref[pl.ds(r0, ch), :]
        db = db + fold(gc)
        for k in range(K):
            dws[k] = dws[k] + fold(gc * ext_ref[pl.ds(HALO - (K - 1) + k + r0, ch), :])
    total = lambda v: jnp.sum(v, axis=0, keepdims=True)
    return total(db), jnp.concatenate([total(d) for d in dws], axis=0)


def _prev_halo_spec(ts, tc, colfn):
    return pl.BlockSpec((HALO, tc), lambda *g: (jnp.maximum(g[-1] * (ts // HALO) - 1, 0), colfn(*g)))


def _conv_silu_fwd(rest, col0, C, w, b, *, name):
    S = rest.shape[0]
    ts, tc = min(CONV_ROW_TILE, S), 512
    assert col0 % tc == 0 and C % tc == 0
    cb = col0 // tc

    def body(x_ref, h_ref, w_ref, b_ref, o_ref, ext_ref):
        i = pl.program_id(1)
        ext_ref[0:HALO, :] = jnp.where(i > 0, h_ref[...], 0.0)
        ext_ref[HALO:, :] = x_ref[...]
        for r0, ch in _chunks(ts):
            pre = _conv_taps(ext_ref, w_ref, b_ref, ch, HALO + r0)
            o_ref[pl.ds(r0, ch), :] = pre * _sigmoid(pre)

    K = w.shape[0]
    return pl.pallas_call(
        body, name=name, grid=(C // tc, S // ts),
        in_specs=[pl.BlockSpec((ts, tc), lambda j, i: (i, cb + j)),
                  _prev_halo_spec(ts, tc, lambda j, i: cb + j),
                  pl.BlockSpec((K, tc), lambda j, i: (0, j)),
                  pl.BlockSpec((1, tc), lambda j, i: (0, j))],
        out_specs=pl.BlockSpec((ts, tc), lambda j, i: (i, j)),
        out_shape=jax.ShapeDtypeStruct((S, C), F32),
        scratch_shapes=[pltpu.VMEM((HALO + ts, tc), F32)],
        compiler_params=_cparams("parallel", "parallel"),
    )(rest, rest, w, b)


def _dsilu(pre):
    s = _sigmoid(pre)
    return s, s * (1.0 + pre * (1.0 - s))


def _conv_silu_bwd(rest, col0, C, w, b, dy, *, name):
    S = rest.shape[0]
    ts, tc = min(CONV_ROW_TILE, S), 512
    cb = col0 // tc
    n = S // ts
    K = w.shape[0]
    R = ts + HALO

    def body(x_ref, hp_ref, hn_ref, w_ref, b_ref, dy_ref, dyn_ref, dx_ref, dw_ref, db_ref, ext_ref, g_ref):
        i = pl.program_id(1)
        last = i == n - 1
        ext_ref[0:HALO, :] = jnp.where(i > 0, hp_ref[...], 0.0)
        ext_ref[HALO:HALO + ts, :] = x_ref[...]
        ext_ref[HALO + ts:, :] = jnp.where(last, 0.0, hn_ref[...])
        g_ref[0:ts, :] = dy_ref[...]
        g_ref[ts:, :] = jnp.where(last, 0.0, dyn_ref[...])
        for r0, ch in _chunks(R):
            pre = _conv_taps(ext_ref, w_ref, b_ref, ch, HALO + r0)
            g_ref[pl.ds(r0, ch), :] = g_ref[pl.ds(r0, ch), :] * _dsilu(pre)[1]

        def dx_store(r0, ch, v):
            dx_ref[pl.ds(r0, ch), :] = v

        db, dw = _conv_bwd_taps(g_ref, ext_ref, w_ref, dx_store, ts)
        _acc_out(db_ref, db, i == 0)
        _acc_out(dw_ref, dw, i == 0)

    nxt = lambda cf: pl.BlockSpec((HALO, tc), lambda j, i: (jnp.minimum((i + 1) * (ts // HALO), S // HALO - 1), cf(j)))
    return pl.pallas_call(
        body, name=name, grid=(C // tc, n),
        in_specs=[pl.BlockSpec((ts, tc), lambda j, i: (i, cb + j)),
                  _prev_halo_spec(ts, tc, lambda j, i: cb + j),
                  nxt(lambda j: cb + j),
                  pl.BlockSpec((K, tc), lambda j, i: (0, j)),
                  pl.BlockSpec((1, tc), lambda j, i: (0, j)),
                  pl.BlockSpec((ts, tc), lambda j, i: (i, j)),
                  nxt(lambda j: j)],
        out_specs=[pl.BlockSpec((ts, tc), lambda j, i: (i, j)),
                   pl.BlockSpec((K, tc), lambda j, i: (0, j)),
                   pl.BlockSpec((1, tc), lambda j, i: (0, j))],
        out_shape=[jax.ShapeDtypeStruct((S, C), BF16), jax.ShapeDtypeStruct((K, C), F32),
                   jax.ShapeDtypeStruct((1, C), F32)],
        scratch_shapes=[pltpu.VMEM((HALO + R, tc), F32), pltpu.VMEM((R, tc), F32)],
        compiler_params=_cparams("parallel", "arbitrary"),
    )(rest, rest, rest, w, b, dy, dy)


FFN_TC = 256


def _ffn_act_fwd(h0, w, b, *, name):
    S, F2 = h0.shape
    F = F2 // 2
    ts, tc = min(CONV_ROW_TILE, S), FFN_TC
    nb = F // tc
    K = w.shape[0]

    def body(xg_ref, hg_ref, xv_ref, hv_ref, wg_ref, wv_ref, bg_ref, bv_ref, o_ref, eg_ref, ev_ref):
        i = pl.program_id(1)
        for x_ref, h_ref, e_ref in ((xg_ref, hg_ref, eg_ref), (xv_ref, hv_ref, ev_ref)):
            e_ref[0:HALO, :] = jnp.where(i > 0, h_ref[...], 0.0)
            e_ref[HALO:, :] = x_ref[...]
        for r0, ch in _chunks(ts):
            pg = _conv_taps(eg_ref, wg_ref, bg_ref, ch, HALO + r0)
            pv = _conv_taps(ev_ref, wv_ref, bv_ref, ch, HALO + r0)
            o_ref[pl.ds(r0, ch), :] = (pg * _sigmoid(pg) * pv).astype(BF16)

    def half(off):
        return [pl.BlockSpec((ts, tc), lambda j, i: (i, off + j)), _prev_halo_spec(ts, tc, lambda j, i: off + j)]

    wspec = lambda off: pl.BlockSpec((K, tc), lambda j, i: (0, off + j))
    bspec = lambda off: pl.BlockSpec((1, tc), lambda j, i: (0, off + j))
    return pl.pallas_call(
        body, name=name, grid=(nb, S // ts),
        in_specs=half(0) + half(nb) + [wspec(0), wspec(nb), bspec(0), bspec(nb)],
        out_specs=pl.BlockSpec((ts, tc), lambda j, i: (i, j)),
        out_shape=jax.ShapeDtypeStruct((S, F), BF16),
        scratch_shapes=[pltpu.VMEM((HALO + ts, tc), F32)] * 2,
        compiler_params=_cparams("parallel", "parallel"),
    )(h0, h0, h0, h0, w, w, b, b)


def _ffn_act_bwd(h0, w, b, da, *, name):
    S, F2 = h0.shape
    F = F2 // 2
    ts, tc = min(CONV_ROW_TILE, S), FFN_TC
    nb = F // tc
    n = S // ts
    K = w.shape[0]
    R = ts + HALO

    def body(xg_ref, pg_ref, ng_ref, xv_ref, pv_ref, nv_ref, wg_ref, wv_ref, bg_ref, bv_ref, da_ref, dan_ref,
             dx_ref, dw_ref, db_ref, eg_ref, ev_ref, gg_ref, gv_ref):
        i = pl.program_id(1)
        last = i == n - 1
        for x_ref, p_ref, n_ref, e_ref in ((xg_ref, pg_ref, ng_ref, eg_ref), (xv_ref, pv_ref, nv_ref, ev_ref)):
            e_ref[0:HALO, :] = jnp.where(i > 0, p_ref[...], 0.0)
            e_ref[HALO:HALO + ts, :] = x_ref[...]
            e_ref[HALO + ts:, :] = jnp.where(last, 0.0, n_ref[...])
        gg_ref[0:ts, :] = da_ref[...]
        gg_ref[ts:, :] = jnp.where(last, 0.0, dan_ref[...])
        for r0, ch in _chunks(R):
            rows = pl.ds(r0, ch)
            dav = gg_ref[rows, :]
            pg = _conv_taps(eg_ref, wg_ref, bg_ref, ch, HALO + r0)
            pv = _conv_taps(ev_ref, wv_ref, bv_ref, ch, HALO + r0)
            s, ds = _dsilu(pg)
            gg_ref[rows, :] = dav * pv * ds
            gv_ref[rows, :] = dav * pg * s
        for h, (g_ref, e_ref, w_ref) in enumerate(((gg_ref, eg_ref, wg_ref), (gv_ref, ev_ref, wv_ref))):
            def dx_store(r0, ch, v):
                dx_ref[h, pl.ds(r0, ch), :] = v

            db, dw = _conv_bwd_taps(g_ref, e_ref, w_ref, dx_store, ts)
            _acc_out(db_ref.at[h], db, i == 0)
            _acc_out(dw_ref.at[h], dw, i == 0)

    nxt = lambda off: pl.BlockSpec((HALO, tc), lambda j, i: (jnp.minimum((i + 1) * (ts // HALO), S // HALO - 1), off + j))

    def half(off):
        return [pl.BlockSpec((ts, tc), lambda j, i: (i, off + j)), _prev_halo_spec(ts, tc, lambda j, i: off + j), nxt(off)]

    wspec = lambda off: pl.BlockSpec((K, tc), lambda j, i: (0, off + j))
    bspec = lambda off: pl.BlockSpec((1, tc), lambda j, i: (0, off + j))
    return pl.pallas_call(
        body, name=name, grid=(nb, n),
        in_specs=half(0) + half(nb) + [wspec(0), wspec(nb), bspec(0), bspec(nb),
                                       pl.BlockSpec((ts, tc), lambda j, i: (i, j)), nxt(0)],
        out_specs=[pl.BlockSpec((2, ts, tc), lambda j, i: (0, i, j)),
                   pl.BlockSpec((2, K, tc), lambda j, i: (0, 0, j)),
                   pl.BlockSpec((2, 1, tc), lambda j, i: (0, 0, j))],
        out_shape=[jax.ShapeDtypeStruct((2, S, F), BF16), jax.ShapeDtypeStruct((2, K, F), F32),
                   jax.ShapeDtypeStruct((2, 1, F), F32)],
        scratch_shapes=[pltpu.VMEM((HALO + R, tc), F32)] * 2 + [pltpu.VMEM((R, tc), F32)] * 2,
        compiler_params=_cparams("parallel", "arbitrary"),
    )(h0, h0, h0, h0, h0, h0, w, w, b, b, da, da)


CUM_TILE = 256
ATT_TILE = 512
ATT_TILE_FWD = 1024
NEG = -1e30
N_PAIR = ATTN_HEADS // 2
NT_DIMS = (((1,), (1,)), ((), ()))


def _split3(x):
    a = x.astype(BF16)
    r = x - a.astype(F32)
    b = r.astype(BF16)
    c = (r - b.astype(F32)).astype(BF16)
    return a, b, c


def _tri_dot(tri, x):
    return sum(jnp.dot(tri, p, preferred_element_type=F32) for p in _split3(x))


def _log_sigmoid(x):
    return jnp.minimum(x, 0.0) - jnp.log(1.0 + jnp.exp(-jnp.abs(x)))


def _col(v, idx, lane):
    return jnp.sum(jnp.where(lane == idx, v, 0.0), axis=1, keepdims=True)


def _fox_pre(rest, col, fb, *, name):
    S = rest.shape[0]
    ts = min(CUM_TILE, S)

    def body(f_ref, fb_ref, ccol_ref, carry_ref):
        i = pl.program_id(0)

        @pl.when(i == 0)
        def _():
            carry_ref[...] = jnp.zeros_like(carry_ref)

        lane = lax.broadcasted_iota(jnp.int32, (1, LANE), 1)
        logf = jnp.where(lane < ATTN_HEADS, _log_sigmoid(f_ref[...] + fb_ref[...]), 0.0)
        r = lax.broadcasted_iota(jnp.int32, (ts, ts), 0)
        c = lax.broadcasted_iota(jnp.int32, (ts, ts), 1)
        tri = jnp.where(c <= r, 1.0, 0.0).astype(BF16)
        cs = _tri_dot(tri, logf) + carry_ref[...]
        ccol_ref[...] = cs
        carry_ref[...] = cs[ts - 1:ts, :]

    return pl.pallas_call(
        body, name=name, grid=(S // ts,),
        in_specs=[pl.BlockSpec((ts, LANE), lambda i: (i, col)), pl.BlockSpec((1, LANE), lambda i: (0, 0))],
        out_specs=pl.BlockSpec((ts, LANE), lambda i: (i, 0)),
        out_shape=jax.ShapeDtypeStruct((S, LANE), F32),
        scratch_shapes=[pltpu.VMEM((1, LANE), F32)],
        compiler_params=_cparams("arbitrary"),
    )(rest, fb)


def _fox_post(dcq, dck, rest, col, fb, ddt, *, name):
    S = rest.shape[0]
    ts = min(CUM_TILE, S)
    n = S // ts

    def body(dcq_ref, dck_ref, f_ref, fb_ref, ddt_ref, o_ref, db_ref, carry_ref):
        i = pl.program_id(0)

        @pl.when(i == 0)
        def _():
            carry_ref[...] = jnp.zeros_like(carry_ref)

        lane = lax.broadcasted_iota(jnp.int32, (1, LANE), 1)
        dc = jnp.zeros((ts, LANE), F32)
        for h in range(ATTN_HEADS):
            d = dcq_ref[h // 2] - dck_ref[h // 2]
            dc = jnp.where(lane == h, _col(d, h % 2, lane), dc)
        r = lax.broadcasted_iota(jnp.int32, (ts, ts), 0)
        c = lax.broadcasted_iota(jnp.int32, (ts, ts), 1)
        tri = jnp.where(c >= r, 1.0, 0.0).astype(BF16)
        rc = _tri_dot(tri, dc) + carry_ref[...]
        carry_ref[...] = rc[0:1, :]
        df = rc * _sigmoid(-(f_ref[...] + fb_ref[...]))
        out = jnp.where(lane < ATTN_HEADS, df, ddt_ref[...])
        o_ref[...] = out.astype(BF16)
        _acc_out(db_ref, jnp.sum(out, axis=0, keepdims=True), i == 0)

    rev = lambda i: n - 1 - i
    return pl.pallas_call(
        body, name=name, grid=(n,),
        in_specs=[pl.BlockSpec((N_PAIR, ts, LANE), lambda i: (0, rev(i), 0)),
                  pl.BlockSpec((N_PAIR, ts, LANE), lambda i: (0, rev(i), 0)),
                  pl.BlockSpec((ts, LANE), lambda i: (rev(i), col)),
                  pl.BlockSpec((1, LANE), lambda i: (0, 0)),
                  pl.BlockSpec((ts, LANE), lambda i: (rev(i), 0))],
        out_specs=[pl.BlockSpec((ts, LANE), lambda i: (rev(i), 0)), pl.BlockSpec((1, LANE), lambda i: (0, 0))],
        out_shape=[jax.ShapeDtypeStruct((S, LANE), BF16), jax.ShapeDtypeStruct((1, LANE), F32)],
        scratch_shapes=[pltpu.VMEM((1, LANE), F32)],
        compiler_params=_cparams("arbitrary"),
    )(dcq, dck, rest, fb, ddt)


def _head_masks():
    lane = lax.broadcasted_iota(jnp.int32, (1, LANE), 1)
    return lane, (lane < ATTN_HEAD_DIM, lane >= ATTN_HEAD_DIM)


def _causal_mask(t):
    r = lax.broadcasted_iota(jnp.int32, (t, t), 0)
    c = lax.broadcasted_iota(jnp.int32, (t, t), 1)
    return r, c


AUG_ONE = 0
AUG_C = 3


def _free_lane0(hh):
    return ATTN_HEAD_DIM if hh == 0 else 0


def _augment(x, head_mask, lane, f0, c, key_side):
    terms = _split3(c)
    one = jnp.ones((), BF16)
    ones_at, terms_at = (AUG_ONE, AUG_C) if key_side else (AUG_C, AUG_ONE)
    out = jnp.where(head_mask, x, jnp.zeros((), BF16))
    for k in range(3):
        out = jnp.where(lane == f0 + ones_at + k, one, out)
        out = jnp.where(lane == f0 + terms_at + k, terms[k], out)
    return out


def _carry_start(kind, in_refs, out_refs, sems, first):
    @pl.when(first)
    def _():
        if kind == "gather":
            _gather_phase(0, in_refs, out_refs, sems)
        else:
            _a2a_start(in_refs, out_refs, sems)


def _carry_mid(kind, in_refs, out_refs, sems, mid):
    if kind == "gather":
        @pl.when(mid)
        def _():
            _gather_phase(1, in_refs, out_refs, sems)


def _carry_wait(kind, in_refs, out_refs, sems, last):
    @pl.when(last)
    def _():
        if kind == "gather":
            _gather_phase(2, in_refs, out_refs, sems)
        else:
            _a2a_wait(in_refs, out_refs, sems)


def _attn_fwd(qkv, ccol, *, comm=None, name):
    S = qkv.shape[0]
    t = min(ATT_TILE_FWD, S)
    n = S // t
    scale = ATTN_HEAD_DIM ** -0.5

    nc = len(comm[1]) if comm else 0

    def body(*refs):
        q_ref, k_ref, v_ref, cc_ref = refs[:4]
        o_ref, lrow_ref = refs[4 + nc:6 + nc]
        kaug_ref, vt_ref, m_ref, l_ref, acc_ref = refs[6 + 2 * nc:11 + 2 * nc]
        j, i = pl.program_id(0), pl.program_id(1)
        lane, masks = _head_masks()
        if comm:
            carried = (refs[4:4 + nc], refs[6 + nc:6 + 2 * nc], refs[11 + 2 * nc:])
            _carry_start(comm[0], *carried, jnp.logical_and(j == 0, i == 0))
            _carry_mid(comm[0], *carried, jnp.logical_and(j == N_PAIR - 1, i == 0))

        @pl.when(i == 0)
        def _():
            def fill(b, carry):
                off = pl.multiple_of(b * t, t)
                kblk = k_ref[pl.ds(off, t), :]
                ccb = cc_ref[pl.ds(off, t), :]
                for hh in range(2):
                    ck = _col(ccb, 2 * j + hh, lane)
                    kaug_ref[hh, pl.ds(off, t), :] = _augment(kblk, masks[hh], lane, _free_lane0(hh), -ck, True)
                vt_ref[:, pl.ds(off, t)] = v_ref[pl.ds(off, t), :].astype(F32).T.astype(BF16)
                return carry
            lax.fori_loop(0, n, fill, 0)

        q2 = q_ref[...] * scale
        ccq = cc_ref[pl.ds(pl.multiple_of(i * t, t), t), :]
        qaug = [_augment(q2, masks[hh], lane, _free_lane0(hh), _col(ccq, 2 * j + hh, lane), False) for hh in range(2)]
        m_ref[...] = jnp.full(m_ref.shape, NEG, F32)
        l_ref[...] = jnp.zeros_like(l_ref)
        acc_ref[...] = jnp.zeros_like(acc_ref)

        def step(kb, masked):
            off = pl.multiple_of(kb * t, t)
            vt = vt_ref[:, pl.ds(off, t)]
            for hh in range(2):
                st = lax.dot_general(kaug_ref[hh, pl.ds(off, t), :], qaug[hh], NT_DIMS, preferred_element_type=F32)
                if masked:
                    r, c = _causal_mask(t)
                    st = jnp.where(r <= c, st, NEG)
                m_old = m_ref[hh]
                m_new = jnp.maximum(m_old, jnp.max(st, axis=0, keepdims=True))
                p = jnp.exp(st - m_new)
                alpha = jnp.exp(m_old - m_new)
                l_ref[hh] = alpha * l_ref[hh] + jnp.sum(p, axis=0, keepdims=True)
                acc_ref[hh] = alpha * acc_ref[hh] + jnp.dot(vt, p.astype(BF16), preferred_element_type=F32)
                m_ref[hh] = m_new

        def loop_body(kb, carry):
            step(kb, False)
            return carry

        lax.fori_loop(0, i, loop_body, 0)
        step(i, True)
        drow = lax.broadcasted_iota(jnp.int32, (LANE, 1), 0)
        ot = jnp.where(drow < ATTN_HEAD_DIM, acc_ref[0] / l_ref[0], acc_ref[1] / l_ref[1])
        o_ref[...] = ot.T.astype(BF16)
        for hh in range(2):
            lrow_ref[hh] = m_ref[hh] + jnp.log(l_ref[hh])
        if comm:
            _carry_wait(comm[0], *carried, jnp.logical_and(j == N_PAIR - 1, i == n - 1))

    res = pl.pallas_call(
        body, name=name, grid=(N_PAIR, n),
        in_specs=[pl.BlockSpec((t, LANE), lambda j, i: (i, j)),
                  pl.BlockSpec((S, LANE), lambda j, i: (0, N_PAIR + j)),
                  pl.BlockSpec((S, LANE), lambda j, i: (0, 2 * N_PAIR + j)),
                  pl.BlockSpec((S, LANE), lambda j, i: (0, 0))] + [ANY] * nc,
        out_specs=[pl.BlockSpec((t, LANE), lambda j, i: (i, j)),
                   pl.BlockSpec((2, 1, t), lambda j, i: (j, 0, i))] + [ANY] * nc,
        out_shape=[jax.ShapeDtypeStruct((S, N_PAIR * LANE), BF16), jax.ShapeDtypeStruct((ATTN_HEADS, 1, S), F32)]
        + (_a2a_out_shapes(*comm) if comm else []),
        scratch_shapes=[pltpu.VMEM((2, S, LANE), BF16), pltpu.VMEM((LANE, S), BF16),
                        pltpu.VMEM((2, 1, t), F32), pltpu.VMEM((2, 1, t), F32), pltpu.VMEM((2, LANE, t), F32)]
        + (_a2a_sems(nc) if comm else []),
        compiler_params=_cparams("arbitrary" if comm else "parallel", "arbitrary"),
    )(qkv, qkv, qkv, ccol, *(comm[1] if comm else []))
    return res[0], res[1], list(res[2:])


def _attn_bwd(qkv, o, do, lrow, ccol, *, comm=None, name):
    S = qkv.shape[0]
    t = min(ATT_TILE, S)
    n = S // t
    scale = ATTN_HEAD_DIM ** -0.5

    nc = len(comm[1]) if comm else 0

    def body(*refs):
        q_ref, k_ref, v_ref, o_ref, do_ref, lrow_ref, cc_ref = refs[:7]
        dq_ref, dk_ref, dv_ref, dcq_ref, dck_ref = refs[7 + nc:12 + nc]
        qaug_ref, drow_ref, dqt_ref, dka_ref, dva_ref = refs[12 + 2 * nc:17 + 2 * nc]
        carried = (refs[7:7 + nc], refs[12 + nc:12 + 2 * nc], refs[17 + 2 * nc:])
        j, kb = pl.program_id(0), pl.program_id(1)
        lane, masks = _head_masks()
        if comm:
            _carry_start(comm[0], *carried, jnp.logical_and(j == 0, kb == 0))
        drows = lax.broadcasted_iota(jnp.int32, (LANE, 1), 0)

        @pl.when(kb == 0)
        def _():
            dqt_ref[...] = jnp.zeros_like(dqt_ref)

            def fill(b, carry):
                off = pl.multiple_of(b * t, t)
                q2 = q_ref[pl.ds(off, t), :] * scale
                ccb = cc_ref[pl.ds(off, t), :]
                prod_t = (do_ref[pl.ds(off, t), :].astype(F32) * o_ref[pl.ds(off, t), :].astype(F32)).T
                lo = jnp.sum(jnp.where(drows < ATTN_HEAD_DIM, prod_t, 0.0), axis=0, keepdims=True)
                drow_ref[0, :, pl.ds(off, t)] = lo
                drow_ref[1, :, pl.ds(off, t)] = jnp.sum(prod_t, axis=0, keepdims=True) - lo
                for hh in range(2):
                    cq = _col(ccb, 2 * j + hh, lane)
                    qaug_ref[hh, pl.ds(off, t), :] = _augment(q2, masks[hh], lane, _free_lane0(hh), cq, False)
                return carry
            lax.fori_loop(0, n, fill, 0)

        koff = pl.multiple_of(kb * t, t)
        kblk = k_ref[...]
        v2 = v_ref[...]
        cck = cc_ref[pl.ds(koff, t), :]
        kaug = [_augment(kblk, masks[hh], lane, _free_lane0(hh), -_col(cck, 2 * j + hh, lane), True) for hh in range(2)]
        kaug_t = [ka.astype(F32).T.astype(BF16) for ka in kaug]
        vh = [jnp.where(mk, v2, jnp.zeros((), BF16)) for mk in masks]
        dka_ref[...] = jnp.zeros_like(dka_ref)
        dva_ref[...] = jnp.zeros_like(dva_ref)

        def step(qb, masked):
            off = pl.multiple_of(qb * t, t)
            doblk = do_ref[pl.ds(off, t), :]
            for hh in range(2):
                qa = qaug_ref[hh, pl.ds(off, t), :]
                st = lax.dot_general(kaug[hh], qa, NT_DIMS, preferred_element_type=F32)
                if masked:
                    r, c = _causal_mask(t)
                    st = jnp.where(r <= c, st, NEG)
                pt = jnp.exp(st - lrow_ref[hh, :, pl.ds(off, t)])
                dpt = lax.dot_general(vh[hh], doblk, NT_DIMS, preferred_element_type=F32)
                dst = (pt * (dpt - drow_ref[hh, :, pl.ds(off, t)])).astype(BF16)
                dva_ref[hh] += jnp.dot(pt.astype(BF16), doblk, preferred_element_type=F32)
                dka_ref[hh] += jnp.dot(dst, qa, preferred_element_type=F32)
                dqt_ref[hh, :, pl.ds(off, t)] += jnp.dot(kaug_t[hh], dst, preferred_element_type=F32)

        step(kb, True)

        def loop_body(qb, carry):
            step(qb, False)
            return carry

        lax.fori_loop(kb + 1, n, loop_body, 0)
        dk_ref[...] = jnp.where(masks[0], dka_ref[0], dka_ref[1]).astype(BF16)
        dv_ref[...] = jnp.where(masks[0], dva_ref[0], dva_ref[1]).astype(BF16)
        dck = [_col(dka_ref[hh], _free_lane0(hh) + AUG_C, lane) for hh in range(2)]
        dck_ref[0] = jnp.where(lane == 0, dck[0], jnp.where(lane == 1, dck[1], 0.0))

        @pl.when(kb == n - 1)
        def _():
            def flush(b, carry):
                off = pl.multiple_of(b * t, t)
                d = [dqt_ref[hh, :, pl.ds(off, t)].T for hh in range(2)]
                dq_ref[pl.ds(off, t), :] = (jnp.where(masks[0], d[0], d[1]) * scale).astype(BF16)
                dcq = [_col(d[hh], _free_lane0(hh) + AUG_ONE, lane) for hh in range(2)]
                dcq_ref[0, pl.ds(off, t), :] = jnp.where(lane == 0, dcq[0], jnp.where(lane == 1, dcq[1], 0.0))
                return carry
            lax.fori_loop(0, n, flush, 0)

        if comm:
            _carry_wait(comm[0], *carried, jnp.logical_and(j == N_PAIR - 1, kb == n - 1))

    full = lambda cb: pl.BlockSpec((S, LANE), lambda j, kb: (0, cb(j)))
    kspec = lambda base: pl.BlockSpec((t, LANE), lambda j, kb: (kb, base + j))
    oblk = pl.BlockSpec((t, LANE), lambda j, kb: (kb, j))
    res = pl.pallas_call(
        body, name=name, grid=(N_PAIR, n),
        in_specs=[full(lambda j: j), kspec(N_PAIR), kspec(2 * N_PAIR), full(lambda j: j), full(lambda j: j),
                  pl.BlockSpec((2, 1, S), lambda j, kb: (j, 0, 0)), full(lambda j: 0)] + [ANY] * nc,
        out_specs=[full(lambda j: j), oblk, oblk,
                   pl.BlockSpec((1, S, LANE), lambda j, kb: (j, 0, 0)),
                   pl.BlockSpec((1, t, LANE), lambda j, kb: (j, kb, 0))] + [ANY] * nc,
        out_shape=[jax.ShapeDtypeStruct((S, N_PAIR * LANE), BF16)] * 3 + [jax.ShapeDtypeStruct((N_PAIR, S, LANE), F32)] * 2
        + (_a2a_out_shapes(*comm) if comm else []),
        scratch_shapes=[pltpu.VMEM((2, S, LANE), BF16), pltpu.VMEM((2, 1, S), F32), pltpu.VMEM((2, LANE, S), F32),
                        pltpu.VMEM((2, t, LANE), F32), pltpu.VMEM((2, t, LANE), F32)]
        + (_a2a_sems(nc) if comm else []),
        compiler_params=_cparams("arbitrary" if comm else "parallel", "arbitrary"),
    )(qkv, qkv, qkv, o, do, lrow, ccol, *(comm[1] if comm else []))
    return tuple(res[:5]) + (list(res[5:]),)


DT_LANE0 = ATTN_HEADS
SSD_PAIRS = SSD_HEADS // 2
SSD_X = SSD_HEADS * SSD_HEAD_DIM
SSD_B0 = SSD_X
SSD_C0 = SSD_X + SSD_GROUPS * SSD_STATE
SSD_CH = SSD_X + 2 * SSD_GROUPS * SSD_STATE
TN_DIMS = (((0,), (0,)), ((), ()))


def _softplus(x):
    return jnp.maximum(x, 0.0) + jnp.log(1.0 + jnp.exp(-jnp.abs(x)))


def _ssd_prep(fdt, dtb, alog):
    L = fdt.shape[0]
    lane = lax.broadcasted_iota(jnp.int32, (1, LANE), 1)
    hl = jnp.logical_and(lane >= DT_LANE0, lane < DT_LANE0 + SSD_HEADS)
    dtv = jnp.where(hl, _softplus(fdt + dtb), 0.0)
    A = jnp.where(hl, -jnp.exp(alog), 0.0)
    r = lax.broadcasted_iota(jnp.int32, (L, L), 0)
    c = lax.broadcasted_iota(jnp.int32, (L, L), 1)
    cs = _tri_dot(jnp.where(c <= r, 1.0, 0.0).astype(BF16), dtv * A)
    return lane, hl, dtv, A, cs, r, c


def _halves(lane, v0, v1):
    return jnp.where(lane < SSD_HEAD_DIM, v0, v1)


def _head_select(width):
    r = jnp.arange(LANE)[:, None]
    c = jnp.arange(SSD_HEADS * width)[None, :]
    return (r == DT_LANE0 + c // width).astype(BF16)


def _spread(x, sel, terms):
    return sum(jnp.dot(p, sel, preferred_element_type=F32) for p in _split3(x)[:terms])


def _ssd_fwd(xc, rest, col, dtb, alog, dskip, *, name):
    S = xc.shape[0]
    L = SSD_CHUNK
    nc = S // L

    def body(xc_ref, f_ref, dtb_ref, al_ref, dk_ref, selp_ref, y_ref, hs_ref, h_ref, cst_ref):
        i = pl.program_id(0)

        @pl.when(i == 0)
        def _():
            h_ref[...] = jnp.zeros_like(h_ref)

        lane, hl, dtv, A, cs, r, c = _ssd_prep(f_ref[...], dtb_ref[...], al_ref[...])
        cst_ref[...] = cs.T
        cs_last = cs[L - 1:L, :]
        cd = jnp.exp(cs_last)
        dkv = dk_ref[...]
        selp = selp_ref[...]
        dt_p, ecs_p, dec_p = (_spread(v, selp, 2) for v in (dtv, jnp.exp(cs), jnp.exp(cs_last - cs)))
        prow = lax.broadcasted_iota(jnp.int32, (LANE, 1), 0)
        for g in range(SSD_GROUPS):
            Bg = xc_ref[:, SSD_B0 + g * SSD_STATE:SSD_B0 + (g + 1) * SSD_STATE].astype(BF16)
            Cg = xc_ref[:, SSD_C0 + g * SSD_STATE:SSD_C0 + (g + 1) * SSD_STATE].astype(BF16)
            CB = lax.dot_general(Cg, Bg, NT_DIMS, preferred_element_type=F32)
            for pp in range(SSD_PAIRS // SSD_GROUPS):
                pi = g * (SSD_PAIRS // SSD_GROUPS) + pp
                hl0 = DT_LANE0 + 2 * pi
                pair = slice(pi * LANE, (pi + 1) * LANE)
                x2 = xc_ref[:, pair]
                xd = x2 * dt_p[:, pair]
                xdb = xd.astype(BF16)
                yd = jnp.zeros((L, LANE), F32)
                for hh in range(2):
                    seg = _col(cs, hl0 + hh, lane) - cst_ref[hl0 + hh:hl0 + hh + 1, :]
                    M = CB * jnp.exp(jnp.where(c <= r, seg, NEG))
                    yh = jnp.dot(M.astype(BF16), xdb, preferred_element_type=F32)
                    yd = jnp.where((lane >= SSD_HEAD_DIM) if hh else (lane < SSD_HEAD_DIM), yh, yd)
                hp = h_ref[pi]
                hs_ref[0, pi] = hp
                yo = lax.dot_general(Cg, hp.astype(BF16), NT_DIMS, preferred_element_type=F32) * ecs_p[:, pair]
                dsk = _halves(lane, _col(dkv, hl0, lane), _col(dkv, hl0 + 1, lane))
                y_ref[:, pair] = yd + yo + dsk * x2
                xw = (xd * dec_p[:, pair]).astype(BF16)
                st = lax.dot_general(xw, Bg, TN_DIMS, preferred_element_type=F32)
                cdp = jnp.where(prow < SSD_HEAD_DIM, _col(cd, hl0, lane), _col(cd, hl0 + 1, lane))
                h_ref[pi] = cdp * hp + st

    vec = pl.BlockSpec((1, LANE), lambda i: (0, 0))
    selp = _head_select(SSD_HEAD_DIM)
    return pl.pallas_call(
        body, name=name, grid=(nc,),
        in_specs=[pl.BlockSpec((L, SSD_CH), lambda i: (i, 0)), pl.BlockSpec((L, LANE), lambda i: (i, col)),
                  vec, vec, vec, pl.BlockSpec(selp.shape, lambda i: (0, 0))],
        out_specs=[pl.BlockSpec((L, SSD_X), lambda i: (i, 0)),
                   pl.BlockSpec((1, SSD_PAIRS, LANE, SSD_STATE), lambda i: (i, 0, 0, 0))],
        out_shape=[jax.ShapeDtypeStruct((S, SSD_X), F32),
                   jax.ShapeDtypeStruct((nc, SSD_PAIRS, LANE, SSD_STATE), F32)],
        scratch_shapes=[pltpu.VMEM((SSD_PAIRS, LANE, SSD_STATE), F32), pltpu.VMEM((LANE, L), F32)],
        compiler_params=_cparams("arbitrary"),
    )(xc, rest, dtb, alog, dskip, selp)


def _pair_sums(lane, v):
    lo = jnp.sum(jnp.where(lane < SSD_HEAD_DIM, v, 0.0), axis=1, keepdims=True)
    return lo, jnp.sum(v, axis=1, keepdims=True) - lo


def _ssd_bwd(xc, rest, col, dtb, alog, dskip, hs, dy, *, name):
    S = xc.shape[0]
    L = SSD_CHUNK
    nc = S // L
    PG = SSD_PAIRS // SSD_GROUPS

    def body(xc_ref, f_ref, dtb_ref, al_ref, dk_ref, hs_ref, dy_ref, selp_ref,
             dxc_ref, ddt_ref, dp_ref, dh_ref, cst_ref):
        i = pl.program_id(0)

        @pl.when(i == 0)
        def _():
            dh_ref[...] = jnp.zeros_like(dh_ref)
            dp_ref[...] = jnp.zeros_like(dp_ref)

        fv = f_ref[...] + dtb_ref[...]
        lane, hl, dtv, A, cs, r, c = _ssd_prep(f_ref[...], dtb_ref[...], al_ref[...])
        cst_ref[...] = cs.T
        cs_last = cs[L - 1:L, :]
        cd = jnp.exp(cs_last)
        dkv = dk_ref[...]
        selp = selp_ref[...]
        dt_p, ecs_p, dec_p = (_spread(v, selp, 2) for v in (dtv, jnp.exp(cs), jnp.exp(cs_last - cs)))
        prow = lax.broadcasted_iota(jnp.int32, (LANE, 1), 0)
        lrow = lax.broadcasted_iota(jnp.int32, (L, 1), 0)
        is_last = lrow == L - 1
        causal = c <= r
        dcs = jnp.zeros((L, LANE), F32)
        ddt = jnp.zeros((L, LANE), F32)
        dD = jnp.zeros((1, LANE), F32)
        for g in range(SSD_GROUPS):
            Bg = xc_ref[:, SSD_B0 + g * SSD_STATE:SSD_B0 + (g + 1) * SSD_STATE].astype(BF16)
            Cg = xc_ref[:, SSD_C0 + g * SSD_STATE:SSD_C0 + (g + 1) * SSD_STATE].astype(BF16)
            CB = lax.dot_general(Cg, Bg, NT_DIMS, preferred_element_type=F32)
            dCB = jnp.zeros((L, L), F32)
            dB = jnp.zeros((L, SSD_STATE), F32)
            dC = jnp.zeros((L, SSD_STATE), F32)
            for pp in range(PG):
                pi = g * PG + pp
                hl0 = DT_LANE0 + 2 * pi
                pair = slice(pi * LANE, (pi + 1) * LANE)
                x2 = xc_ref[:, pair]
                dy2 = dy_ref[:, pair]
                dtp, ecsp, decp = dt_p[:, pair], ecs_p[:, pair], dec_p[:, pair]
                xd = x2 * dtp
                xdb = xd.astype(BF16)
                dsk = _halves(lane, _col(dkv, hl0, lane), _col(dkv, hl0 + 1, lane))
                dx2 = dsk * dy2
                sD = _pair_sums(lane, dy2 * x2)
                hp = hs_ref[0, pi]
                hpb = hp.astype(BF16)
                yo = lax.dot_general(Cg, hpb, NT_DIMS, preferred_element_type=F32) * ecsp
                dW = (ecsp * dy2).astype(BF16)
                dC = dC + jnp.dot(dW, hpb, preferred_element_type=F32)
                dhp = lax.dot_general(dW, Cg, TN_DIMS, preferred_element_type=F32)
                sYo = _pair_sums(lane, dy2 * yo)
                dhn = dh_ref[pi]
                cdh = (_col(cd, hl0, lane), _col(cd, hl0 + 1, lane))
                dhp = dhp + jnp.where(prow < SSD_HEAD_DIM, cdh[0], cdh[1]) * dhn
                rs = jnp.sum(dhn * hp, axis=1, keepdims=True)
                lo = jnp.sum(jnp.where(prow < SSD_HEAD_DIM, rs, 0.0), axis=0, keepdims=True)
                dcd = (lo, jnp.sum(rs, axis=0, keepdims=True) - lo)
                dhnb = dhn.astype(BF16)
                G = lax.dot_general(Bg, dhnb, NT_DIMS, preferred_element_type=F32)
                dxd = decp * G
                sdd = _pair_sums(lane, xd * dxd)
                dB = dB + jnp.dot((xd * decp).astype(BF16), dhnb, preferred_element_type=F32)
                dh_ref[pi] = dhp
                for hh in range(2):
                    hmask = (lane >= SSD_HEAD_DIM) if hh else (lane < SSD_HEAD_DIM)
                    seg = _col(cs, hl0 + hh, lane) - cst_ref[hl0 + hh:hl0 + hh + 1, :]
                    Lm = jnp.exp(jnp.where(causal, seg, NEG))
                    M = CB * Lm
                    dyh = jnp.where(hmask, dy2, 0.0).astype(BF16)
                    dM = lax.dot_general(dyh, xdb, NT_DIMS, preferred_element_type=F32)
                    dxd = dxd + lax.dot_general(M.astype(BF16), dyh, TN_DIMS, preferred_element_type=F32)
                    Q = dM * M
                    dCB = dCB + dM * Lm
                    dd = sdd[hh]
                    end = dcd[hh] * cdh[hh] + jnp.sum(dd, axis=0, keepdims=True)
                    dcs_h = (sYo[hh] - dd + jnp.sum(Q, axis=1, keepdims=True)
                             - jnp.sum(Q.T, axis=1, keepdims=True) + jnp.where(is_last, end, 0.0))
                    dcs = jnp.where(lane == hl0 + hh, dcs_h, dcs)
                    dD = jnp.where(lane == hl0 + hh, jnp.sum(sD[hh], axis=0, keepdims=True), dD)
                sdt = _pair_sums(lane, dxd * x2)
                ddt = jnp.where(lane == hl0, sdt[0], jnp.where(lane == hl0 + 1, sdt[1], ddt))
                dxc_ref[:, pair] = dx2 + dxd * dtp
            dCBb = dCB.astype(BF16)
            dC = dC + jnp.dot(dCBb, Bg, preferred_element_type=F32)
            dB = dB + lax.dot_general(dCBb, Cg, TN_DIMS, preferred_element_type=F32)
            dxc_ref[:, SSD_B0 + g * SSD_STATE:SSD_B0 + (g + 1) * SSD_STATE] = dB
            dxc_ref[:, SSD_C0 + g * SSD_STATE:SSD_C0 + (g + 1) * SSD_STATE] = dC
        da = _tri_dot(jnp.where(c >= r, 1.0, 0.0).astype(BF16), dcs)
        ddtv = ddt + da * A
        ddt_ref[...] = jnp.where(hl, ddtv * _sigmoid(fv), 0.0)
        dal = jnp.sum(da * dtv, axis=0, keepdims=True) * A
        dp_ref[0:1, :] += dal
        dp_ref[1:2, :] += dD

    rev = lambda i: nc - 1 - i
    vec = pl.BlockSpec((1, LANE), lambda i: (0, 0))
    selp = _head_select(SSD_HEAD_DIM)
    return pl.pallas_call(
        body, name=name, grid=(nc,),
        in_specs=[pl.BlockSpec((L, SSD_CH), lambda i: (rev(i), 0)), pl.BlockSpec((L, LANE), lambda i: (rev(i), col)),
                  vec, vec, vec,
                  pl.BlockSpec((1, SSD_PAIRS, LANE, SSD_STATE), lambda i: (rev(i), 0, 0, 0)),
                  pl.BlockSpec((L, SSD_X), lambda i: (rev(i), 0)), pl.BlockSpec(selp.shape, lambda i: (0, 0))],
        out_specs=[pl.BlockSpec((L, SSD_CH), lambda i: (rev(i), 0)), pl.BlockSpec((L, LANE), lambda i: (rev(i), 0)),
                   pl.BlockSpec((SUBLANE, LANE), lambda i: (0, 0))],
        out_shape=[jax.ShapeDtypeStruct((S, SSD_CH), F32), jax.ShapeDtypeStruct((S, LANE), F32),
                   jax.ShapeDtypeStruct((SUBLANE, LANE), F32)],
        scratch_shapes=[pltpu.VMEM((SSD_PAIRS, LANE, SSD_STATE), F32), pltpu.VMEM((LANE, L), F32)],
        compiler_params=_cparams("arbitrary"),
    )(xc, rest, dtb, alog, dskip, hs, dy, selp)


def _gate_norm_fwd(y, rest, zcol, nw, *, name):
    S, W = y.shape
    ts = _row_tile(S)
    GW = W // SSD_GROUPS

    def body(y_ref, z_ref, nw_ref, o_ref):
        z = z_ref[...]
        t = y_ref[...] * (z * _sigmoid(z))
        for g in range(SSD_GROUPS):
            cols = slice(g * GW, (g + 1) * GW)
            tg = t[:, cols]
            rr = lax.rsqrt(jnp.mean(tg * tg, axis=-1, keepdims=True) + NORM_EPS)
            o_ref[:, cols] = (tg * rr * nw_ref[:, cols]).astype(BF16)

    row = pl.BlockSpec((ts, W), lambda i: (i, 0))
    return pl.pallas_call(
        body, name=name, grid=(S // ts,),
        in_specs=[row, pl.BlockSpec((ts, W), lambda i: (i, zcol)), pl.BlockSpec((1, W), lambda i: (0, 0))],
        out_specs=row,
        out_shape=jax.ShapeDtypeStruct((S, W), BF16),
        compiler_params=_cparams("parallel"),
    )(y, rest, nw)


def _gate_norm_bwd(y, rest, zcol, nw, do, *, name):
    S, W = y.shape
    ts = _row_tile(S)
    GW = W // SSD_GROUPS

    def body(y_ref, z_ref, nw_ref, do_ref, dy_ref, dz_ref, dnw_ref):
        z = z_ref[...]
        yv = y_ref[...]
        s, ds = _dsilu(z)
        sz = z * s
        t = yv * sz
        dov = do_ref[...]
        parts = []
        for g in range(SSD_GROUPS):
            cols = slice(g * GW, (g + 1) * GW)
            tg = t[:, cols]
            rr = lax.rsqrt(jnp.mean(tg * tg, axis=-1, keepdims=True) + NORM_EPS)
            n = tg * rr
            dog = dov[:, cols]
            dn = dog * nw_ref[:, cols]
            dt = rr * (dn - n * jnp.mean(dn * n, axis=-1, keepdims=True))
            dy_ref[:, cols] = dt * sz[:, cols]
            dz_ref[:, cols] = (dt * yv[:, cols] * ds[:, cols]).astype(BF16)
            parts.append(jnp.sum(dog * n, axis=0, keepdims=True))
        _acc_out(dnw_ref, jnp.concatenate(parts, axis=1), pl.program_id(0) == 0)

    row = pl.BlockSpec((ts, W), lambda i: (i, 0))
    vec = pl.BlockSpec((1, W), lambda i: (0, 0))
    return pl.pallas_call(
        body, name=name, grid=(S // ts,),
        in_specs=[row, pl.BlockSpec((ts, W), lambda i: (i, zcol)), vec, row],
        out_specs=[row, row, vec],
        out_shape=[jax.ShapeDtypeStruct((S, W), F32), jax.ShapeDtypeStruct((S, W), BF16),
                   jax.ShapeDtypeStruct((1, W), F32)],
        compiler_params=_cparams("arbitrary"),
    )(y, rest, nw, do)


N_DEV = 8
MESH = pl.DeviceIdType.MESH
ANY = pl.BlockSpec(memory_space=pl.ANY)


def _all_gather(xs, *, name):
    n = len(xs)

    def body(*refs):
        for phase in range(3):
            _gather_phase(phase, refs[:n], refs[n:2 * n], refs[2 * n:])

    return pl.pallas_call(
        body, name=name,
        out_shape=_a2a_out_shapes("gather", xs), in_specs=[ANY] * n, out_specs=[ANY] * n,
        scratch_shapes=_a2a_sems(n),
    )(*xs)


def _gather_phase(phase, x_refs, o_refs, sems):
    send_sems, recv_sems, local_sems = sems
    n = len(x_refs)
    px, py, pc = lax.axis_index("x"), lax.axis_index("y"), lax.axis_index("c")
    me, sibling = (px, py, pc), (px, py, 1 - pc)
    chips = [(1 - px, py), (px, 1 - py), (1 - px, 1 - py)]

    def copy(a, k, block, to, src=None):
        slot = o_refs[a].at[4 * block[0] + 2 * block[1] + block[2]]
        return pltpu.make_async_remote_copy(
            src_ref=slot if src is None else src, dst_ref=slot,
            send_sem=send_sems.at[a, k], recv_sem=recv_sems.at[a, k], device_id=to, device_id_type=MESH)

    def own(a):
        return ([copy(a, 0, me, sibling, src=x_refs[a])]
                + [copy(a, 1 + j, me, (*chip, pc), src=x_refs[a]) for j, chip in enumerate(chips)])

    mine = lambda a: pltpu.make_async_copy(x_refs[a], o_refs[a].at[4 * px + 2 * py + pc], local_sems.at[a])
    passed = lambda a, j: copy(a, 4 + j, (*chips[j], pc), sibling)
    if phase == 0:
        for a in range(n):
            mine(a).start()
            for cp in own(a):
                cp.start()
    elif phase == 1:
        for j, chip in enumerate(chips):
            for a in range(n):
                copy(a, 1 + j, (*chip, pc), me).wait_recv()
                passed(a, j).start()
    else:
        for a in range(n):
            copy(a, 0, sibling, me).wait_recv()
            for j, chip in enumerate(chips):
                copy(a, 4 + j, (*chip, 1 - pc), me).wait_recv()
            for cp in own(a) + [passed(a, j) for j in range(len(chips))]:
                cp.wait_send()
            mine(a).wait()


def _exchange(sends, *, name):
    n = len(sends)

    def body(*refs):
        _a2a_start(refs[:n], refs[n:2 * n], refs[2 * n:])
        _a2a_wait(refs[:n], refs[n:2 * n], refs[2 * n:])

    return pl.pallas_call(
        body, name=name,
        out_shape=_a2a_out_shapes("exchange", sends), in_specs=[ANY] * n, out_specs=[ANY] * n,
        scratch_shapes=_a2a_sems(n),
    )(*sends)


def _a2a_out_shapes(kind, arrays):
    if kind == "gather":
        return [jax.ShapeDtypeStruct((N_DEV,) + x.shape, x.dtype) for x in arrays]
    return [jax.ShapeDtypeStruct(x.shape, x.dtype) for x in arrays]


def _a2a_sems(n):
    return [pltpu.SemaphoreType.DMA((n, N_DEV - 1)), pltpu.SemaphoreType.DMA((n, N_DEV - 1)),
            pltpu.SemaphoreType.DMA((n,))]


def _a2a_copies(in_refs, out_refs, send_sems, recv_sems, local_sems, arrivals):
    px, py, pc = lax.axis_index("x"), lax.axis_index("y"), lax.axis_index("c")
    me = 4 * px + 2 * py + pc
    n = len(in_refs)
    src = lambda a, p: in_refs[a].at[p]
    local = [pltpu.make_async_copy(src(a, me), out_refs[a].at[me], local_sems.at[a]) for a in range(n)]
    sent, landing = [], []
    for k in range(1, N_DEV):
        qx = 1 - px if k & 4 else px
        qy = 1 - py if k & 2 else py
        qc = 1 - pc if k & 1 else pc
        peer = 4 * qx + 2 * qy + qc
        for a in range(n):
            def rdma(dst_slot):
                return pltpu.make_async_remote_copy(
                    src_ref=src(a, peer), dst_ref=out_refs[a].at[dst_slot],
                    send_sem=send_sems.at[a, k - 1], recv_sem=recv_sems.at[a, k - 1],
                    device_id=(qx, qy, qc), device_id_type=MESH)
            sent.append(rdma(me))
            if arrivals:
                landing.append(rdma(peer))
    return local, sent, landing


def _a2a_start(in_refs, out_refs, sems):
    local, sent, _ = _a2a_copies(in_refs, out_refs, *sems, arrivals=False)
    for cp in local + sent:
        cp.start()


def _a2a_wait(in_refs, out_refs, sems):
    local, sent, landing = _a2a_copies(in_refs, out_refs, *sems, arrivals=True)
    for cp in landing:
        cp.wait_recv()
    for cp in sent:
        cp.wait_send()
    for cp in local:
        cp.wait()


IN_POOL, IN_QKV, IN_F, IN_Z, IN_XBC, IN_DT, IN_GATE, IN_TOTAL = 0, 512, 2048, 2056, 3080, 4616, 4632, 7704
QKV_W = IN_F - IN_QKV
REST_Z_BLK, REST_POOL_BLK, REST_XBC0, REST_FDT_BLK = 3, 8, 4608, 48
REST_FDT0 = REST_FDT_BLK * LANE
REST_USED = REST_FDT0 + (IN_Z - IN_F) + (IN_GATE - IN_DT)
REST_W = REST_FDT0 + LANE


def _w_in_runs(shard):
    segs = [(0, IN_QKV, IN_F, 0), (1, IN_GATE, IN_TOTAL, 0), (1, IN_Z, IN_XBC, 3072), (1, IN_POOL, IN_QKV, 4096),
            (1, IN_XBC, IN_DT, REST_XBC0), (1, IN_F, IN_Z, REST_FDT0), (1, IN_DT, IN_GATE, REST_FDT0 + IN_Z - IN_F)]
    runs = []
    for dst, t0, t1, d0 in segs:
        for j in range(N_DEV):
            lo, hi = max(t0, shard * j), min(t1, shard * (j + 1))
            if lo < hi:
                runs.append((dst, d0 + lo - t0, j, lo - shard * j, hi - lo))
    return runs


def _repack_w_in(g, shard, *, name):
    _, L, K, P = g.shape
    tr = 256
    runs = _w_in_runs(shard)

    def body(g_ref, q_ref, r_ref):
        r_ref[:, REST_USED:] = jnp.zeros((tr, REST_W - REST_USED), g_ref.dtype)
        for dst, b, j, a, n in runs:
            (q_ref, r_ref)[dst][:, b:b + n] = g_ref[j, :, a:a + n]

    return pl.pallas_call(
        body, name=name, grid=(L, K // tr),
        in_specs=[pl.BlockSpec((N_DEV, None, tr, P), lambda l, i: (0, l, i, 0))],
        out_specs=[pl.BlockSpec((None, tr, QKV_W), lambda l, i: (l, i, 0)),
                   pl.BlockSpec((None, tr, REST_W), lambda l, i: (l, i, 0))],
        out_shape=[jax.ShapeDtypeStruct((L, K, QKV_W), g.dtype), jax.ShapeDtypeStruct((L, K, REST_W), g.dtype)],
        compiler_params=_cparams("parallel", "parallel"),
    )(g)


def _held_index(l, li, i, n):
    return jnp.where(l == li, i, jnp.where(l > li, n - 1, 0))


def _repack_dw_in(dwq, dwr, shard, *, name):
    L = len(dwq)
    K = dwq[0].shape[0]
    tr = 128
    nt = K // tr
    runs = _w_in_runs(shard)

    def body(*refs):
        srcs, o_ref = refs[:2 * L], refs[2 * L]
        l = pl.program_id(0)
        for li in range(L):
            @pl.when(l == li)
            def _():
                for dst, b, j, a, n in runs:
                    o_ref[j, :, a:a + n] = srcs[2 * li + dst][:, b:b + n]

    in_specs = []
    for li in range(L):
        hold = functools.partial(lambda li, l, i: (_held_index(l, li, i, nt), 0), li)
        in_specs += [pl.BlockSpec((tr, QKV_W), hold), pl.BlockSpec((tr, REST_W), hold)]
    args = [x for pair in zip(dwq, dwr) for x in pair]
    return pl.pallas_call(
        body, name=name, grid=(L, nt),
        in_specs=in_specs,
        out_specs=pl.BlockSpec((N_DEV, None, tr, shard), lambda l, i: (0, l, i, 0)),
        out_shape=jax.ShapeDtypeStruct((N_DEV, L, K, shard), dwq[0].dtype),
        compiler_params=_cparams("arbitrary", "arbitrary"),
    )(*args)


FLAT_W = 1024
ADAM_ROWS = 128


def _adam_math(w, g, m, v):
    m2 = ADAM_B1 * m + (1.0 - ADAM_B1) * g
    v2 = ADAM_B2 * v + (1.0 - ADAM_B2) * (g * g)
    m_hat = m2 / (1.0 - ADAM_B1 ** ADAM_STEP)
    v_hat = v2 / (1.0 - ADAM_B2 ** ADAM_STEP)
    delta = -ADAM_LR * (m_hat / (jnp.sqrt(v_hat) + ADAM_EPS) + ADAM_WD * w)
    return delta, m2, v2


def _row_div(R, align):
    best = R
    for t in range(align, min(R, ADAM_ROWS) + 1, align):
        if R % t == 0:
            best = t
    return best


def _sum8(ref_of):
    g = ref_of(0).astype(F32)
    for s in range(1, N_DEV):
        g = g + ref_of(s).astype(F32)
    return g


def _sum_adam(recvs, w, m, v, *, name):
    L, R, C = w.shape
    single = not isinstance(recvs, (list, tuple))
    recvs = [recvs] if single else list(recvs)
    tr = _row_div(R, SUBLANE * (4 // recvs[0].dtype.itemsize))
    nt = R // tr
    nr = len(recvs)

    def body(*refs):
        r_refs = refs[:nr]
        w_ref, m_ref, v_ref, g_ref, d_ref, m2_ref, v2_ref = refs[nr:]

        def run(ref_of):
            g = _sum8(ref_of)
            delta, m2, v2 = _adam_math(w_ref[...], g, m_ref[...], v_ref[...])
            g_ref[...] = g
            d_ref[...] = delta
            m2_ref[...] = m2
            v2_ref[...] = v2

        if single:
            run(lambda s: r_refs[0][s])
        else:
            l = pl.program_id(0)
            for li in range(L):
                @pl.when(l == li)
                def _():
                    run(lambda s: r_refs[li][s])

    if single:
        r_specs = [pl.BlockSpec((N_DEV, None, tr, C), lambda l, i: (0, l, i, 0))]
    else:
        r_specs = [pl.BlockSpec((N_DEV, tr, C), functools.partial(lambda li, l, i: (0, _held_index(l, li, i, nt), 0), li))
                   for li in range(L)]
    row = pl.BlockSpec((None, tr, C), lambda l, i: (l, i, 0))
    return pl.pallas_call(
        body, name=name, grid=(L, nt),
        in_specs=r_specs + [row, row, row],
        out_specs=[row] * 4,
        out_shape=[jax.ShapeDtypeStruct((L, R, C), F32)] * 4,
        compiler_params=_cparams("arbitrary", "arbitrary"),
    )(*recvs, w, m, v)


def _sum_parts(parts, *, name):
    _, R, C = parts.shape

    def body(p_ref, o_ref):
        o_ref[...] = _sum8(lambda s: p_ref[s])

    return pl.pallas_call(
        body, name=name, out_shape=jax.ShapeDtypeStruct((R, C), F32),
        in_specs=[pl.BlockSpec(memory_space=pltpu.VMEM)], out_specs=pl.BlockSpec(memory_space=pltpu.VMEM),
    )(parts)


def _adam(g, w, m, v, *, name):
    def body(g_ref, w_ref, m_ref, v_ref, d_ref, m2_ref, v2_ref):
        delta, m2, v2 = _adam_math(w_ref[...], g_ref[...], m_ref[...], v_ref[...])
        d_ref[...] = delta
        m2_ref[...] = m2
        v2_ref[...] = v2

    whole = pl.BlockSpec(memory_space=pltpu.VMEM)
    return pl.pallas_call(
        body, name=name, in_specs=[whole] * 4, out_specs=[whole] * 3,
        out_shape=[jax.ShapeDtypeStruct(g.shape, F32)] * 3,
    )(g, w, m, v)


SHARDED = ("w_in", "ffn_up", "p_pool", "p_attn", "p_ssd", "w_out", "ffn_down", "ssd_conv_w", "ffn_conv_w")
REPLICATED = ("norm_mix", "pool_mix", "pool_scale", "f_bias", "ssd_conv_b", "ssd_dt_bias", "ssd_a_log", "ssd_d",
              "ssd_norm", "norm_ffn", "ffn_conv_b", "norm_final")
WEIGHTS = ("norm_mix", "w_in", "pool_mix", "pool_scale", "f_bias", "ssd_conv_w", "ssd_conv_b", "ssd_dt_bias",
           "ssd_a_log", "ssd_d", "ssd_norm", "p_pool", "p_attn", "p_ssd", "w_out", "norm_ffn", "ffn_up",
           "ffn_conv_w", "ffn_conv_b", "ffn_down", "norm_final")


def _round_up(n, k):
    return -(-n // k) * k


def _pad_last(x, width):
    return jnp.pad(x, [(0, 0)] * (x.ndim - 1) + [(0, width - x.shape[-1])])


def _flat_rows(parts, rows):
    flat = jnp.concatenate([p.reshape(-1) for p in parts])
    return jnp.pad(flat, (0, rows * FLAT_W - flat.shape[0])).reshape(rows, FLAT_W)


def _lanes(v, lane0):
    return jnp.pad(v, (lane0, LANE - lane0 - v.shape[0]))[None]


def _cols_by_device(g):
    _, L, R, c = g.shape
    return jnp.moveaxis(g, 0, 2).reshape(L, R, N_DEV * c)


def _rows_by_device(g):
    _, L, r, C = g.shape
    return jnp.moveaxis(g, 0, 1).reshape(L, N_DEV * r, C)


def _gather_payload(W, l, shard_pad, up_pad):
    bf = lambda n: W[n][l].astype(BF16)
    return [_pad_last(bf("w_in"), shard_pad), _pad_last(bf("ffn_up"), up_pad), bf("p_pool"), bf("p_attn"),
            bf("p_ssd"), bf("w_out"), bf("ffn_down"), W["ssd_conv_w"][l], W["ffn_conv_w"][l]]


def _proj_params(g_win, W, l, shard):
    w_qkv, w_rest = _repack_w_in(g_win[:, None], shard, name=f"l{l}_repack_w_in")
    return dict(nm=W["norm_mix"][l][None], w_qkv=w_qkv[0], w_rest=w_rest[0], fb=_lanes(W["f_bias"][l], 0), shard=shard)


def _mixer_params(gathered, W, l, up_w, up_pad):
    g_up, g_pp, g_pa, g_ps, g_wo, g_dn, g_cw, g_fcw = [g[:, None] for g in gathered]
    half = N_DEV // 2
    w_down = _rows_by_device(g_dn)[0]
    w_down_p = jnp.pad(w_down.reshape(half, up_w, -1), ((0, 0), (0, up_pad - up_w), (0, 0))).reshape(half * up_pad, -1)
    fcb_p = _pad_last(W["ffn_conv_b"][l].reshape(N_DEV, up_w), up_pad).reshape(1, N_DEV * up_pad)
    return dict(
        mix=W["pool_mix"][l].astype(BF16), pscale=W["pool_scale"][l][None], cw=_cols_by_device(g_cw)[0],
        cb=W["ssd_conv_b"][l][None], dtb=_lanes(W["ssd_dt_bias"][l], DT_LANE0),
        alog=_lanes(W["ssd_a_log"][l], DT_LANE0), dsk=_lanes(W["ssd_d"][l], DT_LANE0), snw=W["ssd_norm"][l][None],
        p_pool=_cols_by_device(g_pp)[0], p_attn=_cols_by_device(g_pa)[0], p_ssd=_rows_by_device(g_ps)[0],
        w_out=_rows_by_device(g_wo)[0], nf=W["norm_ffn"][l][None], g_up=g_up, up_blocks=(N_DEV, up_w, up_pad),
        fcw=_cols_by_device(_pad_last(g_fcw, up_pad))[0], fcb=fcb_p, w_down=w_down_p)


def _layer_fwd(x, W, l, g_win, g_mixer, to_gather, dims):
    nm = lambda s: f"l{l}_{s}"
    P = _proj_params(g_win, W, l, dims[0])
    u = _rms_fwd(x, P["nm"], name=nm("rms_mix"))
    qkv = _mm(u, P["w_qkv"], out_dtype=BF16, name=nm("proj_qkv"))
    rest = _mm(u, P["w_rest"], name=nm("proj_rest"))
    ccol = _fox_pre(rest, REST_FDT_BLK, P["fb"], name=nm("fox_pre"))
    o, lrow, got = _attn_fwd(qkv, ccol, comm=("gather", to_gather) if to_gather else None, name=nm("attn"))
    if g_mixer is None:
        g_mixer, got = got[:len(SHARDED) - 1], got[len(SHARDED) - 1:]
    P.update(_mixer_params(g_mixer, W, l, *dims[1:]))
    d, ya0 = _pool_fwd(rest, REST_POOL_BLK, P["mix"], P["pscale"], name=nm("pool"))
    ya = _mm(ya0, P["p_pool"], name=nm("p_pool"))
    yb = _mm(o, P["p_attn"], name=nm("p_attn"))
    xc = _conv_silu_fwd(rest, REST_XBC0, SSD_CH, P["cw"], P["cb"], name=nm("ssd_conv"))
    y, hs = _ssd_fwd(xc, rest, REST_FDT_BLK, P["dtb"], P["alog"], P["dsk"], name=nm("ssd"))
    yc0 = _gate_norm_fwd(y, rest, REST_Z_BLK, P["snw"], name=nm("ssd_norm"))
    yc = _mm(yc0, P["p_ssd"], name=nm("p_ssd"))
    merged = _merge_fwd(rest, ya, yb, yc, name=nm("merge"))
    x1 = _mm(merged, P["w_out"], add=x, name=nm("w_out"))
    u2 = _rms_fwd(x1, P["nf"], name=nm("rms_ffn"))
    h0 = _mm_up_fwd(u2, P["g_up"], 0, name=nm("ffn_up"))
    a = _ffn_act_fwd(h0, P["fcw"], P["fcb"], name=nm("ffn_act"))
    x2 = _mm(a, P["w_down"], add=x1, name=nm("ffn_down"))
    saved = dict(x=x, u=u, qkv=qkv, rest=rest, d=d, ya0=ya0, ya=ya, ccol=ccol, o=o, lrow=lrow, yb=yb,
                 xc=xc, y=y, hs=hs, yc0=yc0, yc=yc, merged=merged, x1=x1, u2=u2, h0=h0, a=a)
    return x2, saved, P, got


EARLY_GRADS = ("ffn_up", "ffn_down", "ffn_conv_w", "p_pool", "p_attn", "p_ssd", "w_out")
LATE_GRADS = ("w_in", "ssd_conv_w")


def _layer_bwd(g, gb, sv, P, l):
    nm = lambda s: f"l{l}_{s}_bwd"
    rest = sv["rest"]
    G = {}
    up_blocks, up_w, up_pad = P["up_blocks"]
    half = up_blocks // 2
    da = _mm(gb, P["w_down"], tb=True, name=nm("ffn_down_dx"))
    dwd = _mm(sv["a"], gb, ta=True, out_dtype=GRAD_WIRE, name=nm("ffn_down_dw"))
    G["ffn_down"] = dwd.reshape(half, up_pad, -1)[:, :up_w].reshape(N_DEV, -1, dwd.shape[1])
    dh, dfcw, dfcb = _ffn_act_bwd(sv["h0"], P["fcw"], P["fcb"], da, name=nm("ffn_act"))
    du2 = _mm_up_dx(dh, P["g_up"], 0, name=nm("ffn_up_dx"))
    G["ffn_up"] = _mm_up_dw(sv["u2"], dh, up_blocks, up_w, name=nm("ffn_up_dw"))
    taps = dfcw.shape[1]
    G["ffn_conv_w"] = jnp.moveaxis(dfcw.reshape(2, taps, half, up_pad)[..., :up_w], 2, 1).reshape(N_DEV, taps, up_w)
    G["ffn_conv_b"] = dfcb.reshape(2, half, up_pad)[..., :up_w].reshape(-1)
    dx1, dx1b, dnf = _rms_bwd(sv["x1"], P["nf"], du2, g, name=nm("rms_ffn"))
    G["norm_ffn"] = dnf[0]
    dmerged = _mm(dx1b, P["w_out"], tb=True, name=nm("w_out_dx"))
    dwo = _mm(sv["merged"], dx1b, ta=True, out_dtype=GRAD_WIRE, name=nm("w_out_dw"))
    G["w_out"] = dwo.reshape(N_DEV, -1, dwo.shape[1])
    dgl, dya, dyb, dyc = _merge_bwd(rest, sv["ya"], sv["yb"], sv["yc"], dmerged, name=nm("merge"))
    col_blocks = lambda dw: jnp.moveaxis(dw.reshape(dw.shape[0], N_DEV, -1), 1, 0)
    G["p_pool"] = col_blocks(_mm(sv["ya0"], dya, ta=True, out_dtype=GRAD_WIRE, name=nm("p_pool_dw")))
    dya0 = _mm(dya, P["p_pool"], tb=True, name=nm("p_pool_dx"))
    G["p_attn"] = col_blocks(_mm(sv["o"], dyb, ta=True, out_dtype=GRAD_WIRE, name=nm("p_attn_dw")))
    do = _mm(dyb, P["p_attn"], tb=True, out_dtype=BF16, name=nm("p_attn_dx"))
    dps = _mm(sv["yc0"], dyc, ta=True, out_dtype=GRAD_WIRE, name=nm("p_ssd_dw"))
    G["p_ssd"] = dps.reshape(N_DEV, -1, dps.shape[1])
    dyc0 = _mm(dyc, P["p_ssd"], tb=True, name=nm("p_ssd_dx"))
    dpv, dmix, dpsc = _pool_bwd(dya0, sv["d"], P["mix"], P["pscale"], name=nm("pool"))
    G["pool_mix"], G["pool_scale"] = dmix, dpsc[0]
    dq, dk, dv, dcq, dck, got = _attn_bwd(sv["qkv"], sv["o"], do, sv["lrow"], sv["ccol"],
                                          comm=("exchange", [G[n] for n in EARLY_GRADS]), name=nm("attn"))
    received = {(n, l): r for n, r in zip(EARLY_GRADS, got)}
    dy, dz, dsnw = _gate_norm_bwd(sv["y"], rest, REST_Z_BLK, P["snw"], dyc0, name=nm("ssd_norm"))
    G["ssd_norm"] = dsnw[0]
    dxc, ddt, dpar = _ssd_bwd(sv["xc"], rest, REST_FDT_BLK, P["dtb"], P["alog"], P["dsk"], sv["hs"], dy, name=nm("ssd"))
    G["ssd_a_log"] = dpar[0, DT_LANE0:DT_LANE0 + SSD_HEADS]
    G["ssd_d"] = dpar[1, DT_LANE0:DT_LANE0 + SSD_HEADS]
    dxbc, dcw, dcb = _conv_silu_bwd(rest, REST_XBC0, SSD_CH, P["cw"], P["cb"], dxc, name=nm("ssd_conv"))
    G["ssd_conv_w"] = jnp.moveaxis(dcw.reshape(dcw.shape[0], N_DEV, -1), 1, 0)
    G["ssd_conv_b"] = dcb[0]
    dfdt, dfdtb = _fox_post(dcq, dck, rest, REST_FDT_BLK, P["fb"], ddt, name=nm("fox_post"))
    G["f_bias"] = dfdtb[0, :ATTN_HEADS]
    G["ssd_dt_bias"] = dfdtb[0, DT_LANE0:DT_LANE0 + SSD_HEADS]
    dqkv = jnp.concatenate([dq, dk, dv], axis=1)
    drest = jnp.concatenate([dgl, dz, dpv, dxbc, dfdt], axis=1)
    dwq = _mm(sv["u"], dqkv, ta=True, out_dtype=GRAD_WIRE, name=nm("proj_qkv_dw"))
    dwr = _mm(sv["u"], drest, ta=True, out_dtype=GRAD_WIRE, name=nm("proj_rest_dw"))
    G["w_in"] = _repack_dw_in([dwq], [dwr], P["shard"], name=nm("repack_dw_in"))[:, 0]
    du = _mm(dqkv, P["w_qkv"], tb=True, name=nm("proj_qkv_dx"))
    du, got = _mm(drest, P["w_rest"], tb=True, add=du, comm=("exchange", [G[n] for n in LATE_GRADS]),
                  name=nm("proj_rest_dx"))
    received.update({(n, l): r for n, r in zip(LATE_GRADS, got)})
    dx, dxb, dnm = _rms_bwd(sv["x"], P["nm"], du, dx1, name=nm("rms_mix"))
    G["norm_mix"] = dnm[0]
    return dx, dxb, G, received


def kernel(x, norm_mix, w_in, pool_mix, pool_scale, f_bias, ssd_conv_w, ssd_conv_b, ssd_dt_bias, ssd_a_log, ssd_d, ssd_norm, p_pool, p_attn, p_ssd, w_out, norm_ffn, ffn_up, ffn_conv_w, ffn_conv_b, ffn_down, norm_final, loss_target, m_norm_mix, m_w_in, m_pool_mix, m_pool_scale, m_f_bias, m_ssd_conv_w, m_ssd_conv_b, m_ssd_dt_bias, m_ssd_a_log, m_ssd_d, m_ssd_norm, m_p_pool, m_p_attn, m_p_ssd, m_w_out, m_norm_ffn, m_ffn_up, m_ffn_conv_w, m_ffn_conv_b, m_ffn_down, m_norm_final, v_norm_mix, v_w_in, v_pool_mix, v_pool_scale, v_f_bias, v_ssd_conv_w, v_ssd_conv_b, v_ssd_dt_bias, v_ssd_a_log, v_ssd_d, v_ssd_norm, v_p_pool, v_p_attn, v_p_ssd, v_w_out, v_norm_ffn, v_ffn_up, v_ffn_conv_w, v_ffn_conv_b, v_ffn_down, v_norm_final):
    args = dict(locals())
    W = {n: args[n] for n in WEIGHTS}
    M = {n: args["m_" + n] for n in WEIGHTS}
    V = {n: args["v_" + n] for n in WEIGHTS}
    L = norm_mix.shape[0]
    shard = w_in.shape[-1]
    up_w = ffn_up.shape[-1]
    up_pad = _round_up(up_w, LANE)

    payload = lambda l: _gather_payload(W, l, _round_up(shard, LANE), up_pad)
    first = payload(0)
    g_win = _all_gather(first[:1], name="gather_w_in_l0")[0]
    g_mixer, to_gather = None, first[1:]
    h, saves, params = x[0], [], []
    for l in range(L):
        to_gather = to_gather + (payload(l + 1) if l + 1 < L else [])
        h, sv, P, got = _layer_fwd(h, W, l, g_win, g_mixer, to_gather, (shard, up_w, up_pad))
        saves.append(sv)
        params.append(P)
        if l + 1 < L:
            g_win, g_mixer, to_gather = got[0], got[1:], []
    loss_part, g, gb, dnfin = _loss_head(h, norm_final[None], loss_target[0], name="loss_head")

    G, received = [None] * L, {}
    for l in reversed(range(L)):
        g, gb, G[l], got = _layer_bwd(g, gb, saves[l], params[l], l)
        received.update(got)

    rep_grads = {n: jnp.stack([G[l][n] for l in range(L)]) for n in REPLICATED if n != "norm_final"}
    rep_grads["norm_final"] = dnfin[0]
    rep_size = sum(W[n].size for n in REPLICATED)
    rep_rows = _round_up(-(-rep_size // (N_DEV * FLAT_W)), SUBLANE)
    rep_send = _flat_rows([rep_grads[n] for n in REPLICATED], N_DEV * rep_rows).reshape(N_DEV, rep_rows, FLAT_W)
    last = _exchange([rep_send], name="exchange_replicated_grads")

    out = {}
    for n in SHARDED:
        shape = W[n].shape
        as3 = lambda a: a.reshape(L, -1, shape[-1])
        res = _sum_adam([received[(n, l)] for l in range(L)], as3(W[n]), as3(M[n]), as3(V[n]), name="sum_adam_" + n)
        for key, arr in zip(("grad_", "delta_", "new_m_", "new_v_"), res):
            out[key + n] = arr.reshape(shape)

    rep_sum = _sum_parts(last[-1], name="sum_replicated")
    rep_g = _all_gather([rep_sum], name="gather_replicated_grads")[0].reshape(N_DEV * rep_rows, FLAT_W)
    flat_rep = lambda D: _flat_rows([D[n] for n in REPLICATED], N_DEV * rep_rows)
    d_rp, m_rp, v_rp = _adam(rep_g, flat_rep(W), flat_rep(M), flat_rep(V), name="adam_replicated")
    off = 0
    for n in REPLICATED:
        size = W[n].size
        for key, arr in (("grad_", rep_g), ("delta_", d_rp), ("new_m_", m_rp), ("new_v_", v_rp)):
            out[key + n] = arr.reshape(-1)[off:off + size].reshape(W[n].shape)
        off += size

    loss = lax.psum(loss_part[0, 0], ("x", "y", "c"))
    return (loss, g[None], *[out["grad_" + n] for n in WEIGHTS], *[out["delta_" + n] for n in WEIGHTS],
            *[out["new_m_" + n] for n in WEIGHTS], *[out["new_v_" + n] for n in WEIGHTS])
```

```python
import functools

import jax
import jax.numpy as jnp
from jax import lax
from jax.experimental import pallas as pl
from jax.experimental.pallas import tpu as pltpu

F32 = jnp.float32
BF16 = jnp.bfloat16

LANE = 128
SUBLANE = 8
VMEM_LIMIT = 56 * 1024 * 1024

NORM_EPS = 1e-6
POOL_WINDOWS = (2, 4, 8, 16)
POOL_GROUP_DIM = 128
ATTN_HEADS = 8
ATTN_HEAD_DIM = 64
SSD_HEAD_DIM = 64
SSD_HEADS = 16
SSD_GROUPS = 2
SSD_STATE = 128
SSD_CHUNK = 128

ADAM_LR = 0.001
ADAM_B1 = 0.9
ADAM_B2 = 0.999
ADAM_EPS = 1e-08
ADAM_WD = 0.01
ADAM_STEP = 10


def _cparams(*sem):
    return pltpu.CompilerParams(dimension_semantics=tuple(sem), vmem_limit_bytes=VMEM_LIMIT)


def _tile(n, pref):
    if n <= pref:
        return n
    assert n % LANE == 0, n
    q = n // LANE
    best = 1
    for d in range(1, pref // LANE + 1):
        if q % d == 0:
            best = d
    return best * LANE


GRAD_WIRE = BF16


def _mm_call(args, in_specs, out_spec, out_shape, grid, *, ta, tb, nk, acc_shape, has_add, keep, name, comm=None):
    dims = (((0 if ta else 1,), (1 if tb else 0,)), ((), ()))
    nc = len(comm[1]) if comm else 0
    n_in = 2 + has_add

    def body(*refs):
        a_ref, b_ref = refs[0], refs[1]
        o_ref = refs[n_in + nc]
        acc_ref = refs[n_in + 2 * nc + 1] if nk > 1 else None
        carried = (refs[n_in:n_in + nc], refs[n_in + nc + 1:n_in + 2 * nc + 1], refs[n_in + 2 * nc + 1 + (nk > 1):])
        i, j, k = pl.program_id(0), pl.program_id(1), pl.program_id(2)
        if comm:
            _carry_start(comm[0], *carried, jnp.logical_and(jnp.logical_and(i == 0, j == 0), k == 0))
        d = lax.dot_general(a_ref[...], b_ref[...], dims, preferred_element_type=F32)

        def finish(r):
            if has_add:
                r = r + refs[2][...]
            if keep is not None:
                r = r[:, :keep]
            o_ref[...] = r.astype(o_ref.dtype)

        if nk == 1:
            finish(d)
        else:
            @pl.when(k == 0)
            def _():
                acc_ref[...] = d

            @pl.when(jnp.logical_and(k > 0, k < nk - 1))
            def _():
                acc_ref[...] += d

            @pl.when(k == nk - 1)
            def _():
                finish(acc_ref[...] + d)

        if comm:
            _carry_wait(comm[0], *carried, jnp.logical_and(jnp.logical_and(i == grid[0] - 1, j == grid[1] - 1),
                                                           k == nk - 1))

    res = pl.pallas_call(
        body, name=name, grid=grid, in_specs=in_specs + [ANY] * nc, out_specs=[out_spec] + [ANY] * nc,
        out_shape=[out_shape] + (_a2a_out_shapes(*comm) if comm else []),
        scratch_shapes=([pltpu.VMEM(acc_shape, F32)] if nk > 1 else []) + (_a2a_sems(nc) if comm else []),
        compiler_params=_cparams(*(("arbitrary",) * 3 if comm else ("parallel", "parallel", "arbitrary"))),
    )(*args, *(comm[1] if comm else []))
    return (res[0], list(res[1:])) if comm else res[0]


MM_VMEM_BUDGET = 40 * 1024 * 1024
MM_MAX_ROWS = 2048


MM_MIN_ROWS_WHOLE_K = 256


def _mm_rows(M, tn, tk, nk, out_bytes, has_add):
    best = None
    for tm in range(LANE, min(M, MM_MAX_ROWS) + 1, LANE):
        if M % tm:
            continue
        blocks = 2 * (tm * tk * 2 + tk * tn * 2 + tm * tn * out_bytes + (tm * tn * 4 if has_add else 0))
        blocks += tm * tn * 4 * (2 if nk > 1 else 1)
        if blocks <= MM_VMEM_BUDGET:
            best = tm
    return M if best is None else best


def _mm_tiles(M, N, K, out_bytes, has_add, whole_k_ok):
    tn = _tile(N, 1024)
    if whole_k_ok and K > 2048 and M % MM_MIN_ROWS_WHOLE_K == 0:
        tm = _mm_rows(M, tn, K, 1, out_bytes, has_add)
        need = 2 * (tm * K * 2 + K * tn * 2 + tm * tn * out_bytes + (tm * tn * 4 if has_add else 0)) + tm * tn * 4
        if tm >= MM_MIN_ROWS_WHOLE_K and need <= MM_VMEM_BUDGET:
            return tm, tn, K
    tk = _tile(K, 2048)
    return _mm_rows(M, tn, tk, K // tk, out_bytes, has_add), tn, tk


def _mm(a, b, *, ta=False, tb=False, add=None, out_dtype=F32, comm=None, name):
    assert a.dtype == BF16 and b.dtype == BF16, (a.dtype, b.dtype)
    if ta:
        K, M = a.shape
    else:
        M, K = a.shape
    if tb:
        N, K2 = b.shape
    else:
        K2, N = b.shape
    assert K == K2, (a.shape, b.shape, ta, tb)
    tm, tn, tk = _mm_tiles(M, N, K, jnp.dtype(out_dtype).itemsize, add is not None, whole_k_ok=not ta)
    a_spec = pl.BlockSpec((tk, tm), lambda i, j, k: (k, i)) if ta else pl.BlockSpec((tm, tk), lambda i, j, k: (i, k))
    b_spec = pl.BlockSpec((tn, tk), lambda i, j, k: (j, k)) if tb else pl.BlockSpec((tk, tn), lambda i, j, k: (k, j))
    in_specs, args = [a_spec, b_spec], [a, b]
    if add is not None:
        in_specs.append(pl.BlockSpec((tm, tn), lambda i, j, k: (i, j)))
        args.append(add)
    return _mm_call(args, in_specs, pl.BlockSpec((tm, tn), lambda i, j, k: (i, j)),
                    jax.ShapeDtypeStruct((M, N), out_dtype), (M // tm, N // tn, K // tk),
                    ta=ta, tb=tb, nk=K // tk, acc_shape=(tm, tn), has_add=add is not None, keep=None, name=name,
                    comm=comm)


def _half_of(b, half):
    hi = jnp.where(b >= half, 1, 0)
    return hi, b - half * hi


def _mm_up_fwd(u, g_up, l, *, name):
    S, K = u.shape
    nblk, _, _, bw = g_up.shape
    tm = _mm_rows(S, bw, K, 1, 4, False)
    return _mm_call([u, g_up],
                    [pl.BlockSpec((tm, K), lambda i, j, k: (i, 0)),
                     pl.BlockSpec((None, None, K, bw), lambda i, j, k: (j, l, 0, 0))],
                    pl.BlockSpec((tm, bw), lambda i, j, k: (i, j)),
                    jax.ShapeDtypeStruct((S, nblk * bw), F32), (S // tm, nblk, 1),
                    ta=False, tb=False, nk=1, acc_shape=(tm, bw), has_add=False, keep=None, name=name)


def _mm_up_dx(dh, g_up, l, *, name):
    _, S, F = dh.shape
    nblk, _, N, bw = g_up.shape
    half = nblk // 2
    tm = min(MM_MIN_ROWS_WHOLE_K, S)

    def body(dh_ref, g_ref, o_ref):
        acc = None
        for j in range(nblk):
            a = dh_ref[j // half, :, (j % half) * bw:(j % half + 1) * bw]
            d = lax.dot_general(a, g_ref[j], (((1,), (1,)), ((), ())), preferred_element_type=F32)
            acc = d if acc is None else acc + d
        o_ref[...] = acc

    return pl.pallas_call(
        body, name=name, grid=(S // tm,),
        in_specs=[pl.BlockSpec((2, tm, F), lambda i: (0, i, 0)),
                  pl.BlockSpec((nblk, None, N, bw), lambda i: (0, l, 0, 0))],
        out_specs=pl.BlockSpec((tm, N), lambda i: (i, 0)),
        out_shape=jax.ShapeDtypeStruct((S, N), F32),
        compiler_params=_cparams("parallel"),
    )(dh, g_up)


def _mm_up_dw(u, dh, nblk, keep, *, name):
    S, M = u.shape
    half = nblk // 2
    bw = dh.shape[2] // half
    tk = _tile(S, 2048)
    tm = _mm_rows(M, bw, tk, S // tk, jnp.dtype(GRAD_WIRE).itemsize, False)
    return _mm_call([u, dh],
                    [pl.BlockSpec((tk, tm), lambda i, j, k: (k, i)),
                     pl.BlockSpec((None, tk, bw), lambda i, j, k: (_half_of(j, half)[0], k, _half_of(j, half)[1]))],
                    pl.BlockSpec((None, tm, keep), lambda i, j, k: (j, i, 0)),
                    jax.ShapeDtypeStruct((nblk, M, keep), GRAD_WIRE), (M // tm, nblk, S // tk),
                    ta=True, tb=False, nk=S // tk, acc_shape=(tm, bw), has_add=False, keep=keep, name=name)


ROW_TILE = 512
HALO = 8
POOL_HALO = 16


def _row_tile(S):
    return min(ROW_TILE, S)


def _acc_out(ref, val, first):
    @pl.when(first)
    def _():
        ref[...] = val

    @pl.when(jnp.logical_not(first))
    def _():
        ref[...] += val


def _rms_fwd(x, w, *, name):
    S, D = x.shape
    ts = _row_tile(S)

    def body(x_ref, w_ref, u_ref):
        xv = x_ref[...]
        r = lax.rsqrt(jnp.mean(xv * xv, axis=-1, keepdims=True) + NORM_EPS)
        u_ref[...] = (xv * r * w_ref[...]).astype(BF16)

    return pl.pallas_call(
        body, name=name, grid=(S // ts,),
        in_specs=[pl.BlockSpec((ts, D), lambda i: (i, 0)), pl.BlockSpec((1, D), lambda i: (0, 0))],
        out_specs=pl.BlockSpec((ts, D), lambda i: (i, 0)),
        out_shape=jax.ShapeDtypeStruct((S, D), BF16),
        compiler_params=_cparams("parallel"),
    )(x, w)


def _rms_bwd(x, w, du, g, *, name):
    S, D = x.shape
    ts = _row_tile(S)

    def body(x_ref, w_ref, du_ref, g_ref, dx_ref, dxb_ref, dw_ref):
        xv = x_ref[...]
        r = lax.rsqrt(jnp.mean(xv * xv, axis=-1, keepdims=True) + NORM_EPS)
        n = xv * r
        duv = du_ref[...]
        dn = duv * w_ref[...]
        dx = g_ref[...] + r * (dn - n * jnp.mean(dn * n, axis=-1, keepdims=True))
        dx_ref[...] = dx
        dxb_ref[...] = dx.astype(BF16)
        _acc_out(dw_ref, jnp.sum(duv * n, axis=0, keepdims=True), pl.program_id(0) == 0)

    row = pl.BlockSpec((ts, D), lambda i: (i, 0))
    vec = pl.BlockSpec((1, D), lambda i: (0, 0))
    return pl.pallas_call(
        body, name=name, grid=(S // ts,),
        in_specs=[row, vec, row, row],
        out_specs=[row, row, vec],
        out_shape=[jax.ShapeDtypeStruct((S, D), F32), jax.ShapeDtypeStruct((S, D), BF16),
                   jax.ShapeDtypeStruct((1, D), F32)],
        compiler_params=_cparams("arbitrary"),
    )(x, w, du, g)


def _loss_head(x, w, target, *, name):
    S, D = x.shape
    ts = _row_tile(S)

    def body(x_ref, w_ref, t_ref, loss_ref, dx_ref, dxb_ref, dw_ref):
        xv = x_ref[...]
        wv = w_ref[...]
        r = lax.rsqrt(jnp.mean(xv * xv, axis=-1, keepdims=True) + NORM_EPS)
        n = xv * r
        err = n * wv - t_ref[...]
        part = 0.5 * jnp.sum(jnp.mean(err * err, axis=-1, keepdims=True), axis=0, keepdims=True)
        dy = err * (1.0 / D)
        dn = dy * wv
        dx = r * (dn - n * jnp.mean(dn * n, axis=-1, keepdims=True))
        dx_ref[...] = dx
        dxb_ref[...] = dx.astype(BF16)
        first = pl.program_id(0) == 0
        _acc_out(dw_ref, jnp.sum(dy * n, axis=0, keepdims=True), first)
        _acc_out(loss_ref, jnp.broadcast_to(part, loss_ref.shape), first)

    row = pl.BlockSpec((ts, D), lambda i: (i, 0))
    vec = pl.BlockSpec((1, D), lambda i: (0, 0))
    return pl.pallas_call(
        body, name=name, grid=(S // ts,),
        in_specs=[row, vec, row],
        out_specs=[pl.BlockSpec((SUBLANE, LANE), lambda i: (0, 0)), row, row, vec],
        out_shape=[jax.ShapeDtypeStruct((SUBLANE, LANE), F32), jax.ShapeDtypeStruct((S, D), F32),
                   jax.ShapeDtypeStruct((S, D), BF16), jax.ShapeDtypeStruct((1, D), F32)],
        compiler_params=_cparams("arbitrary"),
    )(x, w, target)


def _sigmoid(x):
    return 1.0 / (1.0 + jnp.exp(-x))


def _merge_fwd(gl, ya, yb, yc, *, name):
    S, D = ya.shape
    ts = _row_tile(S)

    def body(ga_ref, gb_ref, gc_ref, ya_ref, yb_ref, yc_ref, o_ref):
        m = (_sigmoid(ga_ref[...]) * ya_ref[...] + _sigmoid(gb_ref[...]) * yb_ref[...]
             + _sigmoid(gc_ref[...]) * yc_ref[...])
        o_ref[...] = m.astype(BF16)

    row = pl.BlockSpec((ts, D), lambda i: (i, 0))
    gspec = [pl.BlockSpec((ts, D), functools.partial(lambda b, i: (i, b), b)) for b in range(3)]
    return pl.pallas_call(
        body, name=name, grid=(S // ts,),
        in_specs=gspec + [row, row, row],
        out_specs=row,
        out_shape=jax.ShapeDtypeStruct((S, D), BF16),
        compiler_params=_cparams("parallel"),
    )(gl, gl, gl, ya, yb, yc)


def _merge_bwd(gl, ya, yb, yc, dm, *, name):
    S, D = ya.shape
    ts = _row_tile(S)

    def body(ga_ref, gb_ref, gc_ref, ya_ref, yb_ref, yc_ref, dm_ref, dgl_ref, da_ref, db_ref, dc_ref):
        dmv = dm_ref[...]
        for b, (g_ref, y_ref, dy_ref) in enumerate(((ga_ref, ya_ref, da_ref), (gb_ref, yb_ref, db_ref),
                                                     (gc_ref, yc_ref, dc_ref))):
            s = _sigmoid(g_ref[...])
            dy_ref[...] = (dmv * s).astype(BF16)
            dgl_ref[:, b * D:(b + 1) * D] = (dmv * y_ref[...] * s * (1.0 - s)).astype(BF16)

    row = pl.BlockSpec((ts, D), lambda i: (i, 0))
    gspec = [pl.BlockSpec((ts, D), functools.partial(lambda b, i: (i, b), b)) for b in range(3)]
    return pl.pallas_call(
        body, name=name, grid=(S // ts,),
        in_specs=gspec + [row, row, row, row],
        out_specs=[pl.BlockSpec((ts, 3 * D), lambda i: (i, 0)), row, row, row],
        out_shape=[jax.ShapeDtypeStruct((S, 3 * D), BF16)] + [jax.ShapeDtypeStruct((S, D), BF16)] * 3,
        compiler_params=_cparams("parallel"),
    )(gl, gl, gl, ya, yb, yc, dm)


POOL_WIDTH = 512


def _pool_cnt(t, w):
    return jnp.minimum(t + 1, w).astype(F32)


def _pool_fwd(rest, col, mix, scale, *, name):
    S = rest.shape[0]
    W, G, H = POOL_WIDTH, POOL_GROUP_DIM, POOL_HALO
    ts = _row_tile(S)

    def body(v_ref, h_ref, mix_ref, sc_ref, d_ref, y_ref, ext_ref):
        i = pl.program_id(0)
        cur = v_ref[...]
        ext_ref[0:H, :] = jnp.where(i > 0, h_ref[...], 0.0)
        ext_ref[H:, :] = cur
        t = i * ts + lax.broadcasted_iota(jnp.int32, (ts, 1), 0)
        for g, w in enumerate(POOL_WINDOWS):
            cols = slice(g * G, (g + 1) * G)
            acc = cur[:, cols]
            for k in range(1, w):
                acc = acc + ext_ref[pl.ds(H - k, ts), cols]
            db = (acc / _pool_cnt(t, w) - cur[:, cols]).astype(BF16)
            d_ref[:, cols] = db
            y = jnp.dot(db, mix_ref[g], preferred_element_type=F32) * sc_ref[:, cols]
            y_ref[:, cols] = y.astype(BF16)

    row = pl.BlockSpec((ts, W), lambda i: (i, 0))
    return pl.pallas_call(
        body, name=name, grid=(S // ts,),
        in_specs=[pl.BlockSpec((ts, W), lambda i: (i, col)),
                  pl.BlockSpec((H, W), lambda i: (jnp.maximum(i * (ts // H) - 1, 0), col)),
                  pl.BlockSpec((len(POOL_WINDOWS), G, G), lambda i: (0, 0, 0)),
                  pl.BlockSpec((1, W), lambda i: (0, 0))],
        out_specs=[row, row],
        out_shape=[jax.ShapeDtypeStruct((S, W), BF16)] * 2,
        scratch_shapes=[pltpu.VMEM((H + ts, W), F32)],
        compiler_params=_cparams("parallel"),
    )(rest, rest, mix, scale)


def _pool_bwd(dy, d, mix, scale, *, name):
    S = dy.shape[0]
    W, G, H = POOL_WIDTH, POOL_GROUP_DIM, POOL_HALO
    ts = _row_tile(S)
    n = S // ts
    NT = (((1,), (1,)), ((), ()))
    TN = (((0,), (0,)), ((), ()))

    def body(dy_ref, dyn_ref, d_ref, mix_ref, sc_ref, dv_ref, dmix_ref, dsc_ref, ext_ref):
        i = pl.program_id(0)

        @pl.when(i == 0)
        def _():
            dmix_ref[...] = jnp.zeros_like(dmix_ref)
            dsc_ref[...] = jnp.zeros_like(dsc_ref)

        t = i * ts + lax.broadcasted_iota(jnp.int32, (ts, 1), 0)
        tn = (i + 1) * ts + lax.broadcasted_iota(jnp.int32, (H, 1), 0)
        for g, w in enumerate(POOL_WINDOWS):
            cols = slice(g * G, (g + 1) * G)
            sc = sc_ref[:, cols]
            dyv = dy_ref[:, cols]
            db = d_ref[:, cols]
            mg = mix_ref[g]
            yp = jnp.dot(db, mg, preferred_element_type=F32)
            dsc_ref[:, cols] += jnp.sum(dyv * yp, axis=0, keepdims=True)
            dyp = (dyv * sc).astype(BF16)
            dd = lax.dot_general(dyp, mg, NT, preferred_element_type=F32)
            dmix_ref[g] += lax.dot_general(db, dyp, TN, preferred_element_type=F32)
            dyn = (jnp.where(i < n - 1, dyn_ref[:, cols], 0.0) * sc).astype(BF16)
            ddn = lax.dot_general(dyn, mg, NT, preferred_element_type=F32)
            e = dd / _pool_cnt(t, w)
            ext_ref[0:ts, cols] = e
            ext_ref[ts:, cols] = ddn / _pool_cnt(tn, w)
            acc = e
            for k in range(1, w):
                acc = acc + ext_ref[pl.ds(k, ts), cols]
            dv_ref[:, cols] = (acc - dd).astype(BF16)

    row = pl.BlockSpec((ts, W), lambda i: (i, 0))
    return pl.pallas_call(
        body, name=name, grid=(n,),
        in_specs=[row,
                  pl.BlockSpec((H, W), lambda i: (jnp.minimum((i + 1) * (ts // H), S // H - 1), 0)),
                  row,
                  pl.BlockSpec((len(POOL_WINDOWS), G, G), lambda i: (0, 0, 0)),
                  pl.BlockSpec((1, W), lambda i: (0, 0))],
        out_specs=[row, pl.BlockSpec((len(POOL_WINDOWS), G, G), lambda i: (0, 0, 0)),
                   pl.BlockSpec((1, W), lambda i: (0, 0))],
        out_shape=[jax.ShapeDtypeStruct((S, W), BF16),
                   jax.ShapeDtypeStruct((len(POOL_WINDOWS), G, G), F32),
                   jax.ShapeDtypeStruct((1, W), F32)],
        scratch_shapes=[pltpu.VMEM((ts + H, W), F32)],
        compiler_params=_cparams("arbitrary"),
    )(dy, dy, d, mix, scale)


def _conv_taps(ext_ref, w_ref, b_ref, rows, first):
    K = w_ref.shape[0]
    pre = b_ref[...] + w_ref[K - 1:K, :] * ext_ref[pl.ds(first, rows), :]
    for k in range(K - 1):
        pre = pre + w_ref[k:k + 1, :] * ext_ref[pl.ds(first - (K - 1) + k, rows), :]
    return pre


CONV_ROW_TILE = 1024
ROW_CHUNK = 40


def _chunks(rows):
    ch = max(c for c in range(SUBLANE, ROW_CHUNK + 1, SUBLANE) if rows % c == 0)
    return [(r0, ch) for r0 in range(0, rows, ch)]


def _conv_bwd_taps(g_ref, ext_ref, w_ref, dx_store, ts):
    K = w_ref.shape[0]
    db, dws = 0.0, [0.0] * K
    fold = lambda v: jnp.sum(v.reshape(v.shape[0] // SUBLANE, SUBLANE, v.shape[1]), axis=0)
    for r0, ch in _chunks(ts):
        dx = w_ref[K - 1:K, :] * g_ref[pl.ds(r0, ch), :]
        for k in range(K - 1):
            dx = dx + w_ref[k:k + 1, :] * g_ref[pl.ds(r0 + K - 1 - k, ch), :]
        dx_store(r0, ch, dx.astype(BF16))
        gc = g_ref[pl.ds(r0, ch), :]
        db = db + fold(gc)
        for k in range(K):
            dws[k] = dws[k] + fold(gc * ext_ref[pl.ds(HALO - (K - 1) + k + r0, ch), :])
    total = lambda v: jnp.sum(v, axis=0, keepdims=True)
    return total(db), jnp.concatenate([total(d) for d in dws], axis=0)


def _prev_halo_spec(ts, tc, colfn):
    return pl.BlockSpec((HALO, tc), lambda *g: (jnp.maximum(g[-1] * (ts // HALO) - 1, 0), colfn(*g)))


def _conv_silu_fwd(rest, col0, C, w, b, *, name):
    S = rest.shape[0]
    ts, tc = min(CONV_ROW_TILE, S), 512
    assert col0 % tc == 0 and C % tc == 0
    cb = col0 // tc

    def body(x_ref, h_ref, w_ref, b_ref, o_ref, ext_ref):
        i = pl.program_id(1)
        ext_ref[0:HALO, :] = jnp.where(i > 0, h_ref[...], 0.0)
        ext_ref[HALO:, :] = x_ref[...]
        for r0, ch in _chunks(ts):
            pre = _conv_taps(ext_ref, w_ref, b_ref, ch, HALO + r0)
            o_ref[pl.ds(r0, ch), :] = pre * _sigmoid(pre)

    K = w.shape[0]
    return pl.pallas_call(
        body, name=name, grid=(C // tc, S // ts),
        in_specs=[pl.BlockSpec((ts, tc), lambda j, i: (i, cb + j)),
                  _prev_halo_spec(ts, tc, lambda j, i: cb + j),
                  pl.BlockSpec((K, tc), lambda j, i: (0, j)),
                  pl.BlockSpec((1, tc), lambda j, i: (0, j))],
        out_specs=pl.BlockSpec((ts, tc), lambda j, i: (i, j)),
        out_shape=jax.ShapeDtypeStruct((S, C), F32),
        scratch_shapes=[pltpu.VMEM((HALO + ts, tc), F32)],
        compiler_params=_cparams("parallel", "parallel"),
    )(rest, rest, w, b)


def _dsilu(pre):
    s = _sigmoid(pre)
    return s, s * (1.0 + pre * (1.0 - s))


def _conv_silu_bwd(rest, col0, C, w, b, dy, *, name):
    S = rest.shape[0]
    ts, tc = min(CONV_ROW_TILE, S), 512
    cb = col0 // tc
    n = S // ts
    K = w.shape[0]
    R = ts + HALO

    def body(x_ref, hp_ref, hn_ref, w_ref, b_ref, dy_ref, dyn_ref, dx_ref, dw_ref, db_ref, ext_ref, g_ref):
        i = pl.program_id(1)
        last = i == n - 1
        ext_ref[0:HALO, :] = jnp.where(i > 0, hp_ref[...], 0.0)
        ext_ref[HALO:HALO + ts, :] = x_ref[...]
        ext_ref[HALO + ts:, :] = jnp.where(last, 0.0, hn_ref[...])
        g_ref[0:ts, :] = dy_ref[...]
        g_ref[ts:, :] = jnp.where(last, 0.0, dyn_ref[...])
        for r0, ch in _chunks(R):
            pre = _conv_taps(ext_ref, w_ref, b_ref, ch, HALO + r0)
            g_ref[pl.ds(r0, ch), :] = g_ref[pl.ds(r0, ch), :] * _dsilu(pre)[1]

        def dx_store(r0, ch, v):
            dx_ref[pl.ds(r0, ch), :] = v

        db, dw = _conv_bwd_taps(g_ref, ext_ref, w_ref, dx_store, ts)
        _acc_out(db_ref, db, i == 0)
        _acc_out(dw_ref, dw, i == 0)

    nxt = lambda cf: pl.BlockSpec((HALO, tc), lambda j, i: (jnp.minimum((i + 1) * (ts // HALO), S // HALO - 1), cf(j)))
    return pl.pallas_call(
        body, name=name, grid=(C // tc, n),
        in_specs=[pl.BlockSpec((ts, tc), lambda j, i: (i, cb + j)),
                  _prev_halo_spec(ts, tc, lambda j, i: cb + j),
                  nxt(lambda j: cb + j),
                  pl.BlockSpec((K, tc), lambda j, i: (0, j)),
                  pl.BlockSpec((1, tc), lambda j, i: (0, j)),
                  pl.BlockSpec((ts, tc), lambda j, i: (i, j)),
                  nxt(lambda j: j)],
        out_specs=[pl.BlockSpec((ts, tc), lambda j, i: (i, j)),
                   pl.BlockSpec((K, tc), lambda j, i: (0, j)),
                   pl.BlockSpec((1, tc), lambda j, i: (0, j))],
        out_shape=[jax.ShapeDtypeStruct((S, C), BF16), jax.ShapeDtypeStruct((K, C), F32),
                   jax.ShapeDtypeStruct((1, C), F32)],
        scratch_shapes=[pltpu.VMEM((HALO + R, tc), F32), pltpu.VMEM((R, tc), F32)],
        compiler_params=_cparams("parallel", "arbitrary"),
    )(rest, rest, rest, w, b, dy, dy)


FFN_TC = 256


def _ffn_act_fwd(h0, w, b, *, name):
    S, F2 = h0.shape
    F = F2 // 2
    ts, tc = min(CONV_ROW_TILE, S), FFN_TC
    nb = F // tc
    K = w.shape[0]

    def body(xg_ref, hg_ref, xv_ref, hv_ref, wg_ref, wv_ref, bg_ref, bv_ref, o_ref, eg_ref, ev_ref):
        i = pl.program_id(1)
        for x_ref, h_ref, e_ref in ((xg_ref, hg_ref, eg_ref), (xv_ref, hv_ref, ev_ref)):
            e_ref[0:HALO, :] = jnp.where(i > 0, h_ref[...], 0.0)
            e_ref[HALO:, :] = x_ref[...]
        for r0, ch in _chunks(ts):
            pg = _conv_taps(eg_ref, wg_ref, bg_ref, ch, HALO + r0)
            pv = _conv_taps(ev_ref, wv_ref, bv_ref, ch, HALO + r0)
            o_ref[pl.ds(r0, ch), :] = (pg * _sigmoid(pg) * pv).astype(BF16)

    def half(off):
        return [pl.BlockSpec((ts, tc), lambda j, i: (i, off + j)), _prev_halo_spec(ts, tc, lambda j, i: off + j)]

    wspec = lambda off: pl.BlockSpec((K, tc), lambda j, i: (0, off + j))
    bspec = lambda off: pl.BlockSpec((1, tc), lambda j, i: (0, off + j))
    return pl.pallas_call(
        body, name=name, grid=(nb, S // ts),
        in_specs=half(0) + half(nb) + [wspec(0), wspec(nb), bspec(0), bspec(nb)],
        out_specs=pl.BlockSpec((ts, tc), lambda j, i: (i, j)),
        out_shape=jax.ShapeDtypeStruct((S, F), BF16),
        scratch_shapes=[pltpu.VMEM((HALO + ts, tc), F32)] * 2,
        compiler_params=_cparams("parallel", "parallel"),
    )(h0, h0, h0, h0, w, w, b, b)


def _ffn_act_bwd(h0, w, b, da, *, name):
    S, F2 = h0.shape
    F = F2 // 2
    ts, tc = min(CONV_ROW_TILE, S), FFN_TC
    nb = F // tc
    n = S // ts
    K = w.shape[0]
    R = ts + HALO

    def body(xg_ref, pg_ref, ng_ref, xv_ref, pv_ref, nv_ref, wg_ref, wv_ref, bg_ref, bv_ref, da_ref, dan_ref,
             dx_ref, dw_ref, db_ref, eg_ref, ev_ref, gg_ref, gv_ref):
        i = pl.program_id(1)
        last = i == n - 1
        for x_ref, p_ref, n_ref, e_ref in ((xg_ref, pg_ref, ng_ref, eg_ref), (xv_ref, pv_ref, nv_ref, ev_ref)):
            e_ref[0:HALO, :] = jnp.where(i > 0, p_ref[...], 0.0)
            e_ref[HALO:HALO + ts, :] = x_ref[...]
            e_ref[HALO + ts:, :] = jnp.where(last, 0.0, n_ref[...])
        gg_ref[0:ts, :] = da_ref[...]
        gg_ref[ts:, :] = jnp.where(last, 0.0, dan_ref[...])
        for r0, ch in _chunks(R):
            rows = pl.ds(r0, ch)
            dav = gg_ref[rows, :]
            pg = _conv_taps(eg_ref, wg_ref, bg_ref, ch, HALO + r0)
            pv = _conv_taps(ev_ref, wv_ref, bv_ref, ch, HALO + r0)
            s, ds = _dsilu(pg)
            gg_ref[rows, :] = dav * pv * ds
            gv_ref[rows, :] = dav * pg * s
        for h, (g_ref, e_ref, w_ref) in enumerate(((gg_ref, eg_ref, wg_ref), (gv_ref, ev_ref, wv_ref))):
            def dx_store(r0, ch, v):
                dx_ref[h, pl.ds(r0, ch), :] = v

            db, dw = _conv_bwd_taps(g_ref, e_ref, w_ref, dx_store, ts)
            _acc_out(db_ref.at[h], db, i == 0)
            _acc_out(dw_ref.at[h], dw, i == 0)

    nxt = lambda off: pl.BlockSpec((HALO, tc), lambda j, i: (jnp.minimum((i + 1) * (ts // HALO), S // HALO - 1), off + j))

    def half(off):
        return [pl.BlockSpec((ts, tc), lambda j, i: (i, off + j)), _prev_halo_spec(ts, tc, lambda j, i: off + j), nxt(off)]

    wspec = lambda off: pl.BlockSpec((K, tc), lambda j, i: (0, off + j))
    bspec = lambda off: pl.BlockSpec((1, tc), lambda j, i: (0, off + j))
    return pl.pallas_call(
        body, name=name, grid=(nb, n),
        in_specs=half(0) + half(nb) + [wspec(0), wspec(nb), bspec(0), bspec(nb),
                                       pl.BlockSpec((ts, tc), lambda j, i: (i, j)), nxt(0)],
        out_specs=[pl.BlockSpec((2, ts, tc), lambda j, i: (0, i, j)),
                   pl.BlockSpec((2, K, tc), lambda j, i: (0, 0, j)),
                   pl.BlockSpec((2, 1, tc), lambda j, i: (0, 0, j))],
        out_shape=[jax.ShapeDtypeStruct((2, S, F), BF16), jax.ShapeDtypeStruct((2, K, F), F32),
                   jax.ShapeDtypeStruct((2, 1, F), F32)],
        scratch_shapes=[pltpu.VMEM((HALO + R, tc), F32)] * 2 + [pltpu.VMEM((R, tc), F32)] * 2,
        compiler_params=_cparams("parallel", "arbitrary"),
    )(h0, h0, h0, h0, h0, h0, w, w, b, b, da, da)


CUM_TILE = 256
ATT_TILE = 512
ATT_TILE_FWD = 1024
NEG = -1e30
N_PAIR = ATTN_HEADS // 2
NT_DIMS = (((1,), (1,)), ((), ()))


def _split3(x):
    a = x.astype(BF16)
    r = x - a.astype(F32)
    b = r.astype(BF16)
    c = (r - b.astype(F32)).astype(BF16)
    return a, b, c


def _tri_dot(tri, x):
    return sum(jnp.dot(tri, p, preferred_element_type=F32) for p in _split3(x))


def _log_sigmoid(x):
    return jnp.minimum(x, 0.0) - jnp.log(1.0 + jnp.exp(-jnp.abs(x)))


def _col(v, idx, lane):
    return jnp.sum(jnp.where(lane == idx, v, 0.0), axis=1, keepdims=True)


def _fox_pre(rest, col, fb, *, name):
    S = rest.shape[0]
    ts = min(CUM_TILE, S)

    def body(f_ref, fb_ref, ccol_ref, carry_ref):
        i = pl.program_id(0)

        @pl.when(i == 0)
        def _():
            carry_ref[...] = jnp.zeros_like(carry_ref)

        lane = lax.broadcasted_iota(jnp.int32, (1, LANE), 1)
        logf = jnp.where(lane < ATTN_HEADS, _log_sigmoid(f_ref[...] + fb_ref[...]), 0.0)
        r = lax.broadcasted_iota(jnp.int32, (ts, ts), 0)
        c = lax.broadcasted_iota(jnp.int32, (ts, ts), 1)
        tri = jnp.where(c <= r, 1.0, 0.0).astype(BF16)
        cs = _tri_dot(tri, logf) + carry_ref[...]
        ccol_ref[...] = cs
        carry_ref[...] = cs[ts - 1:ts, :]

    return pl.pallas_call(
        body, name=name, grid=(S // ts,),
        in_specs=[pl.BlockSpec((ts, LANE), lambda i: (i, col)), pl.BlockSpec((1, LANE), lambda i: (0, 0))],
        out_specs=pl.BlockSpec((ts, LANE), lambda i: (i, 0)),
        out_shape=jax.ShapeDtypeStruct((S, LANE), F32),
        scratch_shapes=[pltpu.VMEM((1, LANE), F32)],
        compiler_params=_cparams("arbitrary"),
    )(rest, fb)


def _fox_post(dcq, dck, rest, col, fb, ddt, *, name):
    S = rest.shape[0]
    ts = min(CUM_TILE, S)
    n = S // ts

    def body(dcq_ref, dck_ref, f_ref, fb_ref, ddt_ref, o_ref, db_ref, carry_ref):
        i = pl.program_id(0)

        @pl.when(i == 0)
        def _():
            carry_ref[...] = jnp.zeros_like(carry_ref)

        lane = lax.broadcasted_iota(jnp.int32, (1, LANE), 1)
        dc = jnp.zeros((ts, LANE), F32)
        for h in range(ATTN_HEADS):
            d = dcq_ref[h // 2] - dck_ref[h // 2]
            dc = jnp.where(lane == h, _col(d, h % 2, lane), dc)
        r = lax.broadcasted_iota(jnp.int32, (ts, ts), 0)
        c = lax.broadcasted_iota(jnp.int32, (ts, ts), 1)
        tri = jnp.where(c >= r, 1.0, 0.0).astype(BF16)
        rc = _tri_dot(tri, dc) + carry_ref[...]
        carry_ref[...] = rc[0:1, :]
        df = rc * _sigmoid(-(f_ref[...] + fb_ref[...]))
        out = jnp.where(lane < ATTN_HEADS, df, ddt_ref[...])
        o_ref[...] = out.astype(BF16)
        _acc_out(db_ref, jnp.sum(out, axis=0, keepdims=True), i == 0)

    rev = lambda i: n - 1 - i
    return pl.pallas_call(
        body, name=name, grid=(n,),
        in_specs=[pl.BlockSpec((N_PAIR, ts, LANE), lambda i: (0, rev(i), 0)),
                  pl.BlockSpec((N_PAIR, ts, LANE), lambda i: (0, rev(i), 0)),
                  pl.BlockSpec((ts, LANE), lambda i: (rev(i), col)),
                  pl.BlockSpec((1, LANE), lambda i: (0, 0)),
                  pl.BlockSpec((ts, LANE), lambda i: (rev(i), 0))],
        out_specs=[pl.BlockSpec((ts, LANE), lambda i: (rev(i), 0)), pl.BlockSpec((1, LANE), lambda i: (0, 0))],
        out_shape=[jax.ShapeDtypeStruct((S, LANE), BF16), jax.ShapeDtypeStruct((1, LANE), F32)],
        scratch_shapes=[pltpu.VMEM((1, LANE), F32)],
        compiler_params=_cparams("arbitrary"),
    )(dcq, dck, rest, fb, ddt)


def _head_masks():
    lane = lax.broadcasted_iota(jnp.int32, (1, LANE), 1)
    return lane, (lane < ATTN_HEAD_DIM, lane >= ATTN_HEAD_DIM)


def _causal_mask(t):
    r = lax.broadcasted_iota(jnp.int32, (t, t), 0)
    c = lax.broadcasted_iota(jnp.int32, (t, t), 1)
    return r, c


AUG_ONE = 0
AUG_C = 3


def _free_lane0(hh):
    return ATTN_HEAD_DIM if hh == 0 else 0


def _augment(x, head_mask, lane, f0, c, key_side):
    terms = _split3(c)
    one = jnp.ones((), BF16)
    ones_at, terms_at = (AUG_ONE, AUG_C) if key_side else (AUG_C, AUG_ONE)
    out = jnp.where(head_mask, x, jnp.zeros((), BF16))
    for k in range(3):
        out = jnp.where(lane == f0 + ones_at + k, one, out)
        out = jnp.where(lane == f0 + terms_at + k, terms[k], out)
    return out


def _carry_start(kind, in_refs, out_refs, sems, first):
    @pl.when(first)
    def _():
        if kind == "gather":
            _gather_phase(0, in_refs, out_refs, sems)
        else:
            _a2a_start(in_refs, out_refs, sems)


def _carry_mid(kind, in_refs, out_refs, sems, mid):
    if kind == "gather":
        @pl.when(mid)
        def _():
            _gather_phase(1, in_refs, out_refs, sems)


def _carry_wait(kind, in_refs, out_refs, sems, last):
    @pl.when(last)
    def _():
        if kind == "gather":
            _gather_phase(2, in_refs, out_refs, sems)
        else:
            _a2a_wait(in_refs, out_refs, sems)


def _attn_fwd(qkv, ccol, *, comm=None, name):
    S = qkv.shape[0]
    t = min(ATT_TILE_FWD, S)
    n = S // t
    scale = ATTN_HEAD_DIM ** -0.5

    nc = len(comm[1]) if comm else 0

    def body(*refs):
        q_ref, k_ref, v_ref, cc_ref = refs[:4]
        o_ref, lrow_ref = refs[4 + nc:6 + nc]
        kaug_ref, vt_ref, m_ref, l_ref, acc_ref = refs[6 + 2 * nc:11 + 2 * nc]
        j, i = pl.program_id(0), pl.program_id(1)
        lane, masks = _head_masks()
        if comm:
            carried = (refs[4:4 + nc], refs[6 + nc:6 + 2 * nc], refs[11 + 2 * nc:])
            _carry_start(comm[0], *carried, jnp.logical_and(j == 0, i == 0))
            _carry_mid(comm[0], *carried, jnp.logical_and(j == N_PAIR - 1, i == 0))

        @pl.when(i == 0)
        def _():
            def fill(b, carry):
                off = pl.multiple_of(b * t, t)
                kblk = k_ref[pl.ds(off, t), :]
                ccb = cc_ref[pl.ds(off, t), :]
                for hh in range(2):
                    ck = _col(ccb, 2 * j + hh, lane)
                    kaug_ref[hh, pl.ds(off, t), :] = _augment(kblk, masks[hh], lane, _free_lane0(hh), -ck, True)
                vt_ref[:, pl.ds(off, t)] = v_ref[pl.ds(off, t), :].astype(F32).T.astype(BF16)
                return carry
            lax.fori_loop(0, n, fill, 0)

        q2 = q_ref[...] * scale
        ccq = cc_ref[pl.ds(pl.multiple_of(i * t, t), t), :]
        qaug = [_augment(q2, masks[hh], lane, _free_lane0(hh), _col(ccq, 2 * j + hh, lane), False) for hh in range(2)]
        m_ref[...] = jnp.full(m_ref.shape, NEG, F32)
        l_ref[...] = jnp.zeros_like(l_ref)
        acc_ref[...] = jnp.zeros_like(acc_ref)

        def step(kb, masked):
            off = pl.multiple_of(kb * t, t)
            vt = vt_ref[:, pl.ds(off, t)]
            for hh in range(2):
                st = lax.dot_general(kaug_ref[hh, pl.ds(off, t), :], qaug[hh], NT_DIMS, preferred_element_type=F32)
                if masked:
                    r, c = _causal_mask(t)
                    st = jnp.where(r <= c, st, NEG)
                m_old = m_ref[hh]
                m_new = jnp.maximum(m_old, jnp.max(st, axis=0, keepdims=True))
                p = jnp.exp(st - m_new)
                alpha = jnp.exp(m_old - m_new)
                l_ref[hh] = alpha * l_ref[hh] + jnp.sum(p, axis=0, keepdims=True)
                acc_ref[hh] = alpha * acc_ref[hh] + jnp.dot(vt, p.astype(BF16), preferred_element_type=F32)
                m_ref[hh] = m_new

        def loop_body(kb, carry):
            step(kb, False)
            return carry

        lax.fori_loop(0, i, loop_body, 0)
        step(i, True)
        drow = lax.broadcasted_iota(jnp.int32, (LANE, 1), 0)
        ot = jnp.where(drow < ATTN_HEAD_DIM, acc_ref[0] / l_ref[0], acc_ref[1] / l_ref[1])
        o_ref[...] = ot.T.astype(BF16)
        for hh in range(2):
            lrow_ref[hh] = m_ref[hh] + jnp.log(l_ref[hh])
        if comm:
            _carry_wait(comm[0], *carried, jnp.logical_and(j == N_PAIR - 1, i == n - 1))

    res = pl.pallas_call(
        body, name=name, grid=(N_PAIR, n),
        in_specs=[pl.BlockSpec((t, LANE), lambda j, i: (i, j)),
                  pl.BlockSpec((S, LANE), lambda j, i: (0, N_PAIR + j)),
                  pl.BlockSpec((S, LANE), lambda j, i: (0, 2 * N_PAIR + j)),
                  pl.BlockSpec((S, LANE), lambda j, i: (0, 0))] + [ANY] * nc,
        out_specs=[pl.BlockSpec((t, LANE), lambda j, i: (i, j)),
                   pl.BlockSpec((2, 1, t), lambda j, i: (j, 0, i))] + [ANY] * nc,
        out_shape=[jax.ShapeDtypeStruct((S, N_PAIR * LANE), BF16), jax.ShapeDtypeStruct((ATTN_HEADS, 1, S), F32)]
        + (_a2a_out_shapes(*comm) if comm else []),
        scratch_shapes=[pltpu.VMEM((2, S, LANE), BF16), pltpu.VMEM((LANE, S), BF16),
                        pltpu.VMEM((2, 1, t), F32), pltpu.VMEM((2, 1, t), F32), pltpu.VMEM((2, LANE, t), F32)]
        + (_a2a_sems(nc) if comm else []),
        compiler_params=_cparams("arbitrary" if comm else "parallel", "arbitrary"),
    )(qkv, qkv, qkv, ccol, *(comm[1] if comm else []))
    return res[0], res[1], list(res[2:])


def _attn_bwd(qkv, o, do, lrow, ccol, *, comm=None, name):
    S = qkv.shape[0]
    t = min(ATT_TILE, S)
    n = S // t
    scale = ATTN_HEAD_DIM ** -0.5

    nc = len(comm[1]) if comm else 0

    def body(*refs):
        q_ref, k_ref, v_ref, o_ref, do_ref, lrow_ref, cc_ref = refs[:7]
        dq_ref, dk_ref, dv_ref, dcq_ref, dck_ref = refs[7 + nc:12 + nc]
        qaug_ref, drow_ref, dqt_ref, dka_ref, dva_ref = refs[12 + 2 * nc:17 + 2 * nc]
        carried = (refs[7:7 + nc], refs[12 + nc:12 + 2 * nc], refs[17 + 2 * nc:])
        j, kb = pl.program_id(0), pl.program_id(1)
        lane, masks = _head_masks()
        if comm:
            _carry_start(comm[0], *carried, jnp.logical_and(j == 0, kb == 0))
        drows = lax.broadcasted_iota(jnp.int32, (LANE, 1), 0)

        @pl.when(kb == 0)
        def _():
            dqt_ref[...] = jnp.zeros_like(dqt_ref)

            def fill(b, carry):
                off = pl.multiple_of(b * t, t)
                q2 = q_ref[pl.ds(off, t), :] * scale
                ccb = cc_ref[pl.ds(off, t), :]
                prod_t = (do_ref[pl.ds(off, t), :].astype(F32) * o_ref[pl.ds(off, t), :].astype(F32)).T
                lo = jnp.sum(jnp.where(drows < ATTN_HEAD_DIM, prod_t, 0.0), axis=0, keepdims=True)
                drow_ref[0, :, pl.ds(off, t)] = lo
                drow_ref[1, :, pl.ds(off, t)] = jnp.sum(prod_t, axis=0, keepdims=True) - lo
                for hh in range(2):
                    cq = _col(ccb, 2 * j + hh, lane)
                    qaug_ref[hh, pl.ds(off, t), :] = _augment(q2, masks[hh], lane, _free_lane0(hh), cq, False)
                return carry
            lax.fori_loop(0, n, fill, 0)

        koff = pl.multiple_of(kb * t, t)
        kblk = k_ref[...]
        v2 = v_ref[...]
        cck = cc_ref[pl.ds(koff, t), :]
        kaug = [_augment(kblk, masks[hh], lane, _free_lane0(hh), -_col(cck, 2 * j + hh, lane), True) for hh in range(2)]
        kaug_t = [ka.astype(F32).T.astype(BF16) for ka in kaug]
        vh = [jnp.where(mk, v2, jnp.zeros((), BF16)) for mk in masks]
        dka_ref[...] = jnp.zeros_like(dka_ref)
        dva_ref[...] = jnp.zeros_like(dva_ref)

        def step(qb, masked):
            off = pl.multiple_of(qb * t, t)
            doblk = do_ref[pl.ds(off, t), :]
            for hh in range(2):
                qa = qaug_ref[hh, pl.ds(off, t), :]
                st = lax.dot_general(kaug[hh], qa, NT_DIMS, preferred_element_type=F32)
                if masked:
                    r, c = _causal_mask(t)
                    st = jnp.where(r <= c, st, NEG)
                pt = jnp.exp(st - lrow_ref[hh, :, pl.ds(off, t)])
                dpt = lax.dot_general(vh[hh], doblk, NT_DIMS, preferred_element_type=F32)
                dst = (pt * (dpt - drow_ref[hh, :, pl.ds(off, t)])).astype(BF16)
                dva_ref[hh] += jnp.dot(pt.astype(BF16), doblk, preferred_element_type=F32)
                dka_ref[hh] += jnp.dot(dst, qa, preferred_element_type=F32)
                dqt_ref[hh, :, pl.ds(off, t)] += jnp.dot(kaug_t[hh], dst, preferred_element_type=F32)

        step(kb, True)

        def loop_body(qb, carry):
            step(qb, False)
            return carry

        lax.fori_loop(kb + 1, n, loop_body, 0)
        dk_ref[...] = jnp.where(masks[0], dka_ref[0], dka_ref[1]).astype(BF16)
        dv_ref[...] = jnp.where(masks[0], dva_ref[0], dva_ref[1]).astype(BF16)
        dck = [_col(dka_ref[hh], _free_lane0(hh) + AUG_C, lane) for hh in range(2)]
        dck_ref[0] = jnp.where(lane == 0, dck[0], jnp.where(lane == 1, dck[1], 0.0))

        @pl.when(kb == n - 1)
        def _():
            def flush(b, carry):
                off = pl.multiple_of(b * t, t)
                d = [dqt_ref[hh, :, pl.ds(off, t)].T for hh in range(2)]
                dq_ref[pl.ds(off, t), :] = (jnp.where(masks[0], d[0], d[1]) * scale).astype(BF16)
                dcq = [_col(d[hh], _free_lane0(hh) + AUG_ONE, lane) for hh in range(2)]
                dcq_ref[0, pl.ds(off, t), :] = jnp.where(lane == 0, dcq[0], jnp.where(lane == 1, dcq[1], 0.0))
                return carry
            lax.fori_loop(0, n, flush, 0)

        if comm:
            _carry_wait(comm[0], *carried, jnp.logical_and(j == N_PAIR - 1, kb == n - 1))

    full = lambda cb: pl.BlockSpec((S, LANE), lambda j, kb: (0, cb(j)))
    kspec = lambda base: pl.BlockSpec((t, LANE), lambda j, kb: (kb, base + j))
    oblk = pl.BlockSpec((t, LANE), lambda j, kb: (kb, j))
    res = pl.pallas_call(
        body, name=name, grid=(N_PAIR, n),
        in_specs=[full(lambda j: j), kspec(N_PAIR), kspec(2 * N_PAIR), full(lambda j: j), full(lambda j: j),
                  pl.BlockSpec((2, 1, S), lambda j, kb: (j, 0, 0)), full(lambda j: 0)] + [ANY] * nc,
        out_specs=[full(lambda j: j), oblk, oblk,
                   pl.BlockSpec((1, S, LANE), lambda j, kb: (j, 0, 0)),
                   pl.BlockSpec((1, t, LANE), lambda j, kb: (j, kb, 0))] + [ANY] * nc,
        out_shape=[jax.ShapeDtypeStruct((S, N_PAIR * LANE), BF16)] * 3 + [jax.ShapeDtypeStruct((N_PAIR, S, LANE), F32)] * 2
        + (_a2a_out_shapes(*comm) if comm else []),
        scratch_shapes=[pltpu.VMEM((2, S, LANE), BF16), pltpu.VMEM((2, 1, S), F32), pltpu.VMEM((2, LANE, S), F32),
                        pltpu.VMEM((2, t, LANE), F32), pltpu.VMEM((2, t, LANE), F32)]
        + (_a2a_sems(nc) if comm else []),
        compiler_params=_cparams("arbitrary" if comm else "parallel", "arbitrary"),
    )(qkv, qkv, qkv, o, do, lrow, ccol, *(comm[1] if comm else []))
    return tuple(res[:5]) + (list(res[5:]),)


DT_LANE0 = ATTN_HEADS
SSD_PAIRS = SSD_HEADS // 2
SSD_X = SSD_HEADS * SSD_HEAD_DIM
SSD_B0 = SSD_X
SSD_C0 = SSD_X + SSD_GROUPS * SSD_STATE
SSD_CH = SSD_X + 2 * SSD_GROUPS * SSD_STATE
TN_DIMS = (((0,), (0,)), ((), ()))


def _softplus(x):
    return jnp.maximum(x, 0.0) + jnp.log(1.0 + jnp.exp(-jnp.abs(x)))


def _ssd_prep(fdt, dtb, alog):
    L = fdt.shape[0]
    lane = lax.broadcasted_iota(jnp.int32, (1, LANE), 1)
    hl = jnp.logical_and(lane >= DT_LANE0, lane < DT_LANE0 + SSD_HEADS)
    dtv = jnp.where(hl, _softplus(fdt + dtb), 0.0)
    A = jnp.where(hl, -jnp.exp(alog), 0.0)
    r = lax.broadcasted_iota(jnp.int32, (L, L), 0)
    c = lax.broadcasted_iota(jnp.int32, (L, L), 1)
    cs = _tri_dot(jnp.where(c <= r, 1.0, 0.0).astype(BF16), dtv * A)
    return lane, hl, dtv, A, cs, r, c


def _halves(lane, v0, v1):
    return jnp.where(lane < SSD_HEAD_DIM, v0, v1)


def _head_select(width):
    r = jnp.arange(LANE)[:, None]
    c = jnp.arange(SSD_HEADS * width)[None, :]
    return (r == DT_LANE0 + c // width).astype(BF16)


def _spread(x, sel, terms):
    return sum(jnp.dot(p, sel, preferred_element_type=F32) for p in _split3(x)[:terms])


def _ssd_fwd(xc, rest, col, dtb, alog, dskip, *, name):
    S = xc.shape[0]
    L = SSD_CHUNK
    nc = S // L

    def body(xc_ref, f_ref, dtb_ref, al_ref, dk_ref, selp_ref, y_ref, hs_ref, h_ref, cst_ref):
        i = pl.program_id(0)

        @pl.when(i == 0)
        def _():
            h_ref[...] = jnp.zeros_like(h_ref)

        lane, hl, dtv, A, cs, r, c = _ssd_prep(f_ref[...], dtb_ref[...], al_ref[...])
        cst_ref[...] = cs.T
        cs_last = cs[L - 1:L, :]
        cd = jnp.exp(cs_last)
        dkv = dk_ref[...]
        selp = selp_ref[...]
        dt_p, ecs_p, dec_p = (_spread(v, selp, 2) for v in (dtv, jnp.exp(cs), jnp.exp(cs_last - cs)))
        prow = lax.broadcasted_iota(jnp.int32, (LANE, 1), 0)
        for g in range(SSD_GROUPS):
            Bg = xc_ref[:, SSD_B0 + g * SSD_STATE:SSD_B0 + (g + 1) * SSD_STATE].astype(BF16)
            Cg = xc_ref[:, SSD_C0 + g * SSD_STATE:SSD_C0 + (g + 1) * SSD_STATE].astype(BF16)
            CB = lax.dot_general(Cg, Bg, NT_DIMS, preferred_element_type=F32)
            for pp in range(SSD_PAIRS // SSD_GROUPS):
                pi = g * (SSD_PAIRS // SSD_GROUPS) + pp
                hl0 = DT_LANE0 + 2 * pi
                pair = slice(pi * LANE, (pi + 1) * LANE)
                x2 = xc_ref[:, pair]
                xd = x2 * dt_p[:, pair]
                xdb = xd.astype(BF16)
                yd = jnp.zeros((L, LANE), F32)
                for hh in range(2):
                    seg = _col(cs, hl0 + hh, lane) - cst_ref[hl0 + hh:hl0 + hh + 1, :]
                    M = CB * jnp.exp(jnp.where(c <= r, seg, NEG))
                    yh = jnp.dot(M.astype(BF16), xdb, preferred_element_type=F32)
                    yd = jnp.where((lane >= SSD_HEAD_DIM) if hh else (lane < SSD_HEAD_DIM), yh, yd)
                hp = h_ref[pi]
                hs_ref[0, pi] = hp
                yo = lax.dot_general(Cg, hp.astype(BF16), NT_DIMS, preferred_element_type=F32) * ecs_p[:, pair]
                dsk = _halves(lane, _col(dkv, hl0, lane), _col(dkv, hl0 + 1, lane))
                y_ref[:, pair] = yd + yo + dsk * x2
                xw = (xd * dec_p[:, pair]).astype(BF16)
                st = lax.dot_general(xw, Bg, TN_DIMS, preferred_element_type=F32)
                cdp = jnp.where(prow < SSD_HEAD_DIM, _col(cd, hl0, lane), _col(cd, hl0 + 1, lane))
                h_ref[pi] = cdp * hp + st

    vec = pl.BlockSpec((1, LANE), lambda i: (0, 0))
    selp = _head_select(SSD_HEAD_DIM)
    return pl.pallas_call(
        body, name=name, grid=(nc,),
        in_specs=[pl.BlockSpec((L, SSD_CH), lambda i: (i, 0)), pl.BlockSpec((L, LANE), lambda i: (i, col)),
                  vec, vec, vec, pl.BlockSpec(selp.shape, lambda i: (0, 0))],
        out_specs=[pl.BlockSpec((L, SSD_X), lambda i: (i, 0)),
                   pl.BlockSpec((1, SSD_PAIRS, LANE, SSD_STATE), lambda i: (i, 0, 0, 0))],
        out_shape=[jax.ShapeDtypeStruct((S, SSD_X), F32),
                   jax.ShapeDtypeStruct((nc, SSD_PAIRS, LANE, SSD_STATE), F32)],
        scratch_shapes=[pltpu.VMEM((SSD_PAIRS, LANE, SSD_STATE), F32), pltpu.VMEM((LANE, L), F32)],
        compiler_params=_cparams("arbitrary"),
    )(xc, rest, dtb, alog, dskip, selp)


def _pair_sums(lane, v):
    lo = jnp.sum(jnp.where(lane < SSD_HEAD_DIM, v, 0.0), axis=1, keepdims=True)
    return lo, jnp.sum(v, axis=1, keepdims=True) - lo


def _ssd_bwd(xc, rest, col, dtb, alog, dskip, hs, dy, *, name):
    S = xc.shape[0]
    L = SSD_CHUNK
    nc = S // L
    PG = SSD_PAIRS // SSD_GROUPS

    def body(xc_ref, f_ref, dtb_ref, al_ref, dk_ref, hs_ref, dy_ref, selp_ref,
             dxc_ref, ddt_ref, dp_ref, dh_ref, cst_ref):
        i = pl.program_id(0)

        @pl.when(i == 0)
        def _():
            dh_ref[...] = jnp.zeros_like(dh_ref)
            dp_ref[...] = jnp.zeros_like(dp_ref)

        fv = f_ref[...] + dtb_ref[...]
        lane, hl, dtv, A, cs, r, c = _ssd_prep(f_ref[...], dtb_ref[...], al_ref[...])
        cst_ref[...] = cs.T
        cs_last = cs[L - 1:L, :]
        cd = jnp.exp(cs_last)
        dkv = dk_ref[...]
        selp = selp_ref[...]
        dt_p, ecs_p, dec_p = (_spread(v, selp, 2) for v in (dtv, jnp.exp(cs), jnp.exp(cs_last - cs)))
        prow = lax.broadcasted_iota(jnp.int32, (LANE, 1), 0)
        lrow = lax.broadcasted_iota(jnp.int32, (L, 1), 0)
        is_last = lrow == L - 1
        causal = c <= r
        dcs = jnp.zeros((L, LANE), F32)
        ddt = jnp.zeros((L, LANE), F32)
        dD = jnp.zeros((1, LANE), F32)
        for g in range(SSD_GROUPS):
            Bg = xc_ref[:, SSD_B0 + g * SSD_STATE:SSD_B0 + (g + 1) * SSD_STATE].astype(BF16)
            Cg = xc_ref[:, SSD_C0 + g * SSD_STATE:SSD_C0 + (g + 1) * SSD_STATE].astype(BF16)
            CB = lax.dot_general(Cg, Bg, NT_DIMS, preferred_element_type=F32)
            dCB = jnp.zeros((L, L), F32)
            dB = jnp.zeros((L, SSD_STATE), F32)
            dC = jnp.zeros((L, SSD_STATE), F32)
            for pp in range(PG):
                pi = g * PG + pp
                hl0 = DT_LANE0 + 2 * pi
                pair = slice(pi * LANE, (pi + 1) * LANE)
                x2 = xc_ref[:, pair]
                dy2 = dy_ref[:, pair]
                dtp, ecsp, decp = dt_p[:, pair], ecs_p[:, pair], dec_p[:, pair]
                xd = x2 * dtp
                xdb = xd.astype(BF16)
                dsk = _halves(lane, _col(dkv, hl0, lane), _col(dkv, hl0 + 1, lane))
                dx2 = dsk * dy2
                sD = _pair_sums(lane, dy2 * x2)
                hp = hs_ref[0, pi]
                hpb = hp.astype(BF16)
                yo = lax.dot_general(Cg, hpb, NT_DIMS, preferred_element_type=F32) * ecsp
                dW = (ecsp * dy2).astype(BF16)
                dC = dC + jnp.dot(dW, hpb, preferred_element_type=F32)
                dhp = lax.dot_general(dW, Cg, TN_DIMS, preferred_element_type=F32)
                sYo = _pair_sums(lane, dy2 * yo)
                dhn = dh_ref[pi]
                cdh = (_col(cd, hl0, lane), _col(cd, hl0 + 1, lane))
                dhp = dhp + jnp.where(prow < SSD_HEAD_DIM, cdh[0], cdh[1]) * dhn
                rs = jnp.sum(dhn * hp, axis=1, keepdims=True)
                lo = jnp.sum(jnp.where(prow < SSD_HEAD_DIM, rs, 0.0), axis=0, keepdims=True)
                dcd = (lo, jnp.sum(rs, axis=0, keepdims=True) - lo)
                dhnb = dhn.astype(BF16)
                G = lax.dot_general(Bg, dhnb, NT_DIMS, preferred_element_type=F32)
                dxd = decp * G
                sdd = _pair_sums(lane, xd * dxd)
                dB = dB + jnp.dot((xd * decp).astype(BF16), dhnb, preferred_element_type=F32)
                dh_ref[pi] = dhp
                for hh in range(2):
                    hmask = (lane >= SSD_HEAD_DIM) if hh else (lane < SSD_HEAD_DIM)
                    seg = _col(cs, hl0 + hh, lane) - cst_ref[hl0 + hh:hl0 + hh + 1, :]
                    Lm = jnp.exp(jnp.where(causal, seg, NEG))
                    M = CB * Lm
                    dyh = jnp.where(hmask, dy2, 0.0).astype(BF16)
                    dM = lax.dot_general(dyh, xdb, NT_DIMS, preferred_element_type=F32)
                    dxd = dxd + lax.dot_general(M.astype(BF16), dyh, TN_DIMS, preferred_element_type=F32)
                    Q = dM * M
                    dCB = dCB + dM * Lm
                    dd = sdd[hh]
                    end = dcd[hh] * cdh[hh] + jnp.sum(dd, axis=0, keepdims=True)
                    dcs_h = (sYo[hh] - dd + jnp.sum(Q, axis=1, keepdims=True)
                             - jnp.sum(Q.T, axis=1, keepdims=True) + jnp.where(is_last, end, 0.0))
                    dcs = jnp.where(lane == hl0 + hh, dcs_h, dcs)
                    dD = jnp.where(lane == hl0 + hh, jnp.sum(sD[hh], axis=0, keepdims=True), dD)
                sdt = _pair_sums(lane, dxd * x2)
                ddt = jnp.where(lane == hl0, sdt[0], jnp.where(lane == hl0 + 1, sdt[1], ddt))
                dxc_ref[:, pair] = dx2 + dxd * dtp
            dCBb = dCB.astype(BF16)
            dC = dC + jnp.dot(dCBb, Bg, preferred_element_type=F32)
            dB = dB + lax.dot_general(dCBb, Cg, TN_DIMS, preferred_element_type=F32)
            dxc_ref[:, SSD_B0 + g * SSD_STATE:SSD_B0 + (g + 1) * SSD_STATE] = dB
            dxc_ref[:, SSD_C0 + g * SSD_STATE:SSD_C0 + (g + 1) * SSD_STATE] = dC
        da = _tri_dot(jnp.where(c >= r, 1.0, 0.0).astype(BF16), dcs)
        ddtv = ddt + da * A
        ddt_ref[...] = jnp.where(hl, ddtv * _sigmoid(fv), 0.0)
        dal = jnp.sum(da * dtv, axis=0, keepdims=True) * A
        dp_ref[0:1, :] += dal
        dp_ref[1:2, :] += dD

    rev = lambda i: nc - 1 - i
    vec = pl.BlockSpec((1, LANE), lambda i: (0, 0))
    selp = _head_select(SSD_HEAD_DIM)
    return pl.pallas_call(
        body, name=name, grid=(nc,),
        in_specs=[pl.BlockSpec((L, SSD_CH), lambda i: (rev(i), 0)), pl.BlockSpec((L, LANE), lambda i: (rev(i), col)),
                  vec, vec, vec,
                  pl.BlockSpec((1, SSD_PAIRS, LANE, SSD_STATE), lambda i: (rev(i), 0, 0, 0)),
                  pl.BlockSpec((L, SSD_X), lambda i: (rev(i), 0)), pl.BlockSpec(selp.shape, lambda i: (0, 0))],
        out_specs=[pl.BlockSpec((L, SSD_CH), lambda i: (rev(i), 0)), pl.BlockSpec((L, LANE), lambda i: (rev(i), 0)),
                   pl.BlockSpec((SUBLANE, LANE), lambda i: (0, 0))],
        out_shape=[jax.ShapeDtypeStruct((S, SSD_CH), F32), jax.ShapeDtypeStruct((S, LANE), F32),
                   jax.ShapeDtypeStruct((SUBLANE, LANE), F32)],
        scratch_shapes=[pltpu.VMEM((SSD_PAIRS, LANE, SSD_STATE), F32), pltpu.VMEM((LANE, L), F32)],
        compiler_params=_cparams("arbitrary"),
    )(xc, rest, dtb, alog, dskip, hs, dy, selp)


def _gate_norm_fwd(y, rest, zcol, nw, *, name):
    S, W = y.shape
    ts = _row_tile(S)
    GW = W // SSD_GROUPS

    def body(y_ref, z_ref, nw_ref, o_ref):
        z = z_ref[...]
        t = y_ref[...] * (z * _sigmoid(z))
        for g in range(SSD_GROUPS):
            cols = slice(g * GW, (g + 1) * GW)
            tg = t[:, cols]
            rr = lax.rsqrt(jnp.mean(tg * tg, axis=-1, keepdims=True) + NORM_EPS)
            o_ref[:, cols] = (tg * rr * nw_ref[:, cols]).astype(BF16)

    row = pl.BlockSpec((ts, W), lambda i: (i, 0))
    return pl.pallas_call(
        body, name=name, grid=(S // ts,),
        in_specs=[row, pl.BlockSpec((ts, W), lambda i: (i, zcol)), pl.BlockSpec((1, W), lambda i: (0, 0))],
        out_specs=row,
        out_shape=jax.ShapeDtypeStruct((S, W), BF16),
        compiler_params=_cparams("parallel"),
    )(y, rest, nw)


def _gate_norm_bwd(y, rest, zcol, nw, do, *, name):
    S, W = y.shape
    ts = _row_tile(S)
    GW = W // SSD_GROUPS

    def body(y_ref, z_ref, nw_ref, do_ref, dy_ref, dz_ref, dnw_ref):
        z = z_ref[...]
        yv = y_ref[...]
        s, ds = _dsilu(z)
        sz = z * s
        t = yv * sz
        dov = do_ref[...]
        parts = []
        for g in range(SSD_GROUPS):
            cols = slice(g * GW, (g + 1) * GW)
            tg = t[:, cols]
            rr = lax.rsqrt(jnp.mean(tg * tg, axis=-1, keepdims=True) + NORM_EPS)
            n = tg * rr
            dog = dov[:, cols]
            dn = dog * nw_ref[:, cols]
            dt = rr * (dn - n * jnp.mean(dn * n, axis=-1, keepdims=True))
            dy_ref[:, cols] = dt * sz[:, cols]
            dz_ref[:, cols] = (dt * yv[:, cols] * ds[:, cols]).astype(BF16)
            parts.append(jnp.sum(dog * n, axis=0, keepdims=True))
        _acc_out(dnw_ref, jnp.concatenate(parts, axis=1), pl.program_id(0) == 0)

    row = pl.BlockSpec((ts, W), lambda i: (i, 0))
    vec = pl.BlockSpec((1, W), lambda i: (0, 0))
    return pl.pallas_call(
        body, name=name, grid=(S // ts,),
        in_specs=[row, pl.BlockSpec((ts, W), lambda i: (i, zcol)), vec, row],
        out_specs=[row, row, vec],
        out_shape=[jax.ShapeDtypeStruct((S, W), F32), jax.ShapeDtypeStruct((S, W), BF16),
                   jax.ShapeDtypeStruct((1, W), F32)],
        compiler_params=_cparams("arbitrary"),
    )(y, rest, nw, do)


N_DEV = 8
MESH = pl.DeviceIdType.MESH
ANY = pl.BlockSpec(memory_space=pl.ANY)


def _all_gather(xs, *, name):
    n = len(xs)

    def body(*refs):
        for phase in range(3):
            _gather_phase(phase, refs[:n], refs[n:2 * n], refs[2 * n:])

    return pl.pallas_call(
        body, name=name,
        out_shape=_a2a_out_shapes("gather", xs), in_specs=[ANY] * n, out_specs=[ANY] * n,
        scratch_shapes=_a2a_sems(n),
    )(*xs)


def _gather_phase(phase, x_refs, o_refs, sems):
    send_sems, recv_sems, local_sems = sems
    n = len(x_refs)
    px, py, pc = lax.axis_index("x"), lax.axis_index("y"), lax.axis_index("c")
    me, sibling = (px, py, pc), (px, py, 1 - pc)
    chips = [(1 - px, py), (px, 1 - py), (1 - px, 1 - py)]

    def copy(a, k, block, to, src=None):
        slot = o_refs[a].at[4 * block[0] + 2 * block[1] + block[2]]
        return pltpu.make_async_remote_copy(
            src_ref=slot if src is None else src, dst_ref=slot,
            send_sem=send_sems.at[a, k], recv_sem=recv_sems.at[a, k], device_id=to, device_id_type=MESH)

    def own(a):
        return ([copy(a, 0, me, sibling, src=x_refs[a])]
                + [copy(a, 1 + j, me, (*chip, pc), src=x_refs[a]) for j, chip in enumerate(chips)])

    mine = lambda a: pltpu.make_async_copy(x_refs[a], o_refs[a].at[4 * px + 2 * py + pc], local_sems.at[a])
    passed = lambda a, j: copy(a, 4 + j, (*chips[j], pc), sibling)
    if phase == 0:
        for a in range(n):
            mine(a).start()
            for cp in own(a):
                cp.start()
    elif phase == 1:
        for j, chip in enumerate(chips):
            for a in range(n):
                copy(a, 1 + j, (*chip, pc), me).wait_recv()
                passed(a, j).start()
    else:
        for a in range(n):
            copy(a, 0, sibling, me).wait_recv()
            for j, chip in enumerate(chips):
                copy(a, 4 + j, (*chip, 1 - pc), me).wait_recv()
            for cp in own(a) + [passed(a, j) for j in range(len(chips))]:
                cp.wait_send()
            mine(a).wait()


def _exchange(sends, *, name):
    n = len(sends)

    def body(*refs):
        _a2a_start(refs[:n], refs[n:2 * n], refs[2 * n:])
        _a2a_wait(refs[:n], refs[n:2 * n], refs[2 * n:])

    return pl.pallas_call(
        body, name=name,
        out_shape=_a2a_out_shapes("exchange", sends), in_specs=[ANY] * n, out_specs=[ANY] * n,
        scratch_shapes=_a2a_sems(n),
    )(*sends)


def _a2a_out_shapes(kind, arrays):
    if kind == "gather":
        return [jax.ShapeDtypeStruct((N_DEV,) + x.shape, x.dtype) for x in arrays]
    return [jax.ShapeDtypeStruct(x.shape, x.dtype) for x in arrays]


def _a2a_sems(n):
    return [pltpu.SemaphoreType.DMA((n, N_DEV - 1)), pltpu.SemaphoreType.DMA((n, N_DEV - 1)),
            pltpu.SemaphoreType.DMA((n,))]


def _a2a_copies(in_refs, out_refs, send_sems, recv_sems, local_sems, arrivals):
    px, py, pc = lax.axis_index("x"), lax.axis_index("y"), lax.axis_index("c")
    me = 4 * px + 2 * py + pc
    n = len(in_refs)
    src = lambda a, p: in_refs[a].at[p]
    local = [pltpu.make_async_copy(src(a, me), out_refs[a].at[me], local_sems.at[a]) for a in range(n)]
    sent, landing = [], []
    for k in range(1, N_DEV):
        qx = 1 - px if k & 4 else px
        qy = 1 - py if k & 2 else py
        qc = 1 - pc if k & 1 else pc
        peer = 4 * qx + 2 * qy + qc
        for a in range(n):
            def rdma(dst_slot):
                return pltpu.make_async_remote_copy(
                    src_ref=src(a, peer), dst_ref=out_refs[a].at[dst_slot],
                    send_sem=send_sems.at[a, k - 1], recv_sem=recv_sems.at[a, k - 1],
                    device_id=(qx, qy, qc), device_id_type=MESH)
            sent.append(rdma(me))
            if arrivals:
                landing.append(rdma(peer))
    return local, sent, landing


def _a2a_start(in_refs, out_refs, sems):
    local, sent, _ = _a2a_copies(in_refs, out_refs, *sems, arrivals=False)
    for cp in local + sent:
        cp.start()


def _a2a_wait(in_refs, out_refs, sems):
    local, sent, landing = _a2a_copies(in_refs, out_refs, *sems, arrivals=True)
    for cp in landing:
        cp.wait_recv()
    for cp in sent:
        cp.wait_send()
    for cp in local:
        cp.wait()


IN_POOL, IN_QKV, IN_F, IN_Z, IN_XBC, IN_DT, IN_GATE, IN_TOTAL = 0, 512, 2048, 2056, 3080, 4616, 4632, 7704
QKV_W = IN_F - IN_QKV
REST_Z_BLK, REST_POOL_BLK, REST_XBC0, REST_FDT_BLK = 3, 8, 4608, 48
REST_FDT0 = REST_FDT_BLK * LANE
REST_USED = REST_FDT0 + (IN_Z - IN_F) + (IN_GATE - IN_DT)
REST_W = REST_FDT0 + LANE


def _w_in_runs(shard):
    segs = [(0, IN_QKV, IN_F, 0), (1, IN_GATE, IN_TOTAL, 0), (1, IN_Z, IN_XBC, 3072), (1, IN_POOL, IN_QKV, 4096),
            (1, IN_XBC, IN_DT, REST_XBC0), (1, IN_F, IN_Z, REST_FDT0), (1, IN_DT, IN_GATE, REST_FDT0 + IN_Z - IN_F)]
    runs = []
    for dst, t0, t1, d0 in segs:
        for j in range(N_DEV):
            lo, hi = max(t0, shard * j), min(t1, shard * (j + 1))
            if lo < hi:
                runs.append((dst, d0 + lo - t0, j, lo - shard * j, hi - lo))
    return runs


def _repack_w_in(g, shard, *, name):
    _, L, K, P = g.shape
    tr = 256
    runs = _w_in_runs(shard)

    def body(g_ref, q_ref, r_ref):
        r_ref[:, REST_USED:] = jnp.zeros((tr, REST_W - REST_USED), g_ref.dtype)
        for dst, b, j, a, n in runs:
            (q_ref, r_ref)[dst][:, b:b + n] = g_ref[j, :, a:a + n]

    return pl.pallas_call(
        body, name=name, grid=(L, K // tr),
        in_specs=[pl.BlockSpec((N_DEV, None, tr, P), lambda l, i: (0, l, i, 0))],
        out_specs=[pl.BlockSpec((None, tr, QKV_W), lambda l, i: (l, i, 0)),
                   pl.BlockSpec((None, tr, REST_W), lambda l, i: (l, i, 0))],
        out_shape=[jax.ShapeDtypeStruct((L, K, QKV_W), g.dtype), jax.ShapeDtypeStruct((L, K, REST_W), g.dtype)],
        compiler_params=_cparams("parallel", "parallel"),
    )(g)


def _held_index(l, li, i, n):
    return jnp.where(l == li, i, jnp.where(l > li, n - 1, 0))


def _repack_dw_in(dwq, dwr, shard, *, name):
    L = len(dwq)
    K = dwq[0].shape[0]
    tr = 128
    nt = K // tr
    runs = _w_in_runs(shard)

    def body(*refs):
        srcs, o_ref = refs[:2 * L], refs[2 * L]
        l = pl.program_id(0)
        for li in range(L):
            @pl.when(l == li)
            def _():
                for dst, b, j, a, n in runs:
                    o_ref[j, :, a:a + n] = srcs[2 * li + dst][:, b:b + n]

    in_specs = []
    for li in range(L):
        hold = functools.partial(lambda li, l, i: (_held_index(l, li, i, nt), 0), li)
        in_specs += [pl.BlockSpec((tr, QKV_W), hold), pl.BlockSpec((tr, REST_W), hold)]
    args = [x for pair in zip(dwq, dwr) for x in pair]
    return pl.pallas_call(
        body, name=name, grid=(L, nt),
        in_specs=in_specs,
        out_specs=pl.BlockSpec((N_DEV, None, tr, shard), lambda l, i: (0, l, i, 0)),
        out_shape=jax.ShapeDtypeStruct((N_DEV, L, K, shard), dwq[0].dtype),
        compiler_params=_cparams("arbitrary", "arbitrary"),
    )(*args)


FLAT_W = 1024
ADAM_ROWS = 128


def _adam_math(w, g, m, v):
    m2 = ADAM_B1 * m + (1.0 - ADAM_B1) * g
    v2 = ADAM_B2 * v + (1.0 - ADAM_B2) * (g * g)
    m_hat = m2 / (1.0 - ADAM_B1 ** ADAM_STEP)
    v_hat = v2 / (1.0 - ADAM_B2 ** ADAM_STEP)
    delta = -ADAM_LR * (m_hat / (jnp.sqrt(v_hat) + ADAM_EPS) + ADAM_WD * w)
    return delta, m2, v2


def _row_div(R, align):
    best = R
    for t in range(align, min(R, ADAM_ROWS) + 1, align):
        if R % t == 0:
            best = t
    return best


def _sum8(ref_of):
    g = ref_of(0).astype(F32)
    for s in range(1, N_DEV):
        g = g + ref_of(s).astype(F32)
    return g


def _sum_adam(recvs, w, m, v, *, name):
    L, R, C = w.shape
    single = not isinstance(recvs, (list, tuple))
    recvs = [recvs] if single else list(recvs)
    tr = _row_div(R, SUBLANE * (4 // recvs[0].dtype.itemsize))
    nt = R // tr
    nr = len(recvs)

    def body(*refs):
        r_refs = refs[:nr]
        w_ref, m_ref, v_ref, g_ref, d_ref, m2_ref, v2_ref = refs[nr:]

        def run(ref_of):
            g = _sum8(ref_of)
            delta, m2, v2 = _adam_math(w_ref[...], g, m_ref[...], v_ref[...])
            g_ref[...] = g
            d_ref[...] = delta
            m2_ref[...] = m2
            v2_ref[...] = v2

        if single:
            run(lambda s: r_refs[0][s])
        else:
            l = pl.program_id(0)
            for li in range(L):
                @pl.when(l == li)
                def _():
                    run(lambda s: r_refs[li][s])

    if single:
        r_specs = [pl.BlockSpec((N_DEV, None, tr, C), lambda l, i: (0, l, i, 0))]
    else:
        r_specs = [pl.BlockSpec((N_DEV, tr, C), functools.partial(lambda li, l, i: (0, _held_index(l, li, i, nt), 0), li))
                   for li in range(L)]
    row = pl.BlockSpec((None, tr, C), lambda l, i: (l, i, 0))
    return pl.pallas_call(
        body, name=name, grid=(L, nt),
        in_specs=r_specs + [row, row, row],
        out_specs=[row] * 4,
        out_shape=[jax.ShapeDtypeStruct((L, R, C), F32)] * 4,
        compiler_params=_cparams("arbitrary", "arbitrary"),
    )(*recvs, w, m, v)


def _sum_parts(parts, *, name):
    _, R, C = parts.shape

    def body(p_ref, o_ref):
        o_ref[...] = _sum8(lambda s: p_ref[s])

    return pl.pallas_call(
        body, name=name, out_shape=jax.ShapeDtypeStruct((R, C), F32),
        in_specs=[pl.BlockSpec(memory_space=pltpu.VMEM)], out_specs=pl.BlockSpec(memory_space=pltpu.VMEM),
    )(parts)


def _adam(g, w, m, v, *, name):
    def body(g_ref, w_ref, m_ref, v_ref, d_ref, m2_ref, v2_ref):
        delta, m2, v2 = _adam_math(w_ref[...], g_ref[...], m_ref[...], v_ref[...])
        d_ref[...] = delta
        m2_ref[...] = m2
        v2_ref[...] = v2

    whole = pl.BlockSpec(memory_space=pltpu.VMEM)
    return pl.pallas_call(
        body, name=name, in_specs=[whole] * 4, out_specs=[whole] * 3,
        out_shape=[jax.ShapeDtypeStruct(g.shape, F32)] * 3,
    )(g, w, m, v)


SHARDED = ("w_in", "ffn_up", "p_pool", "p_attn", "p_ssd", "w_out", "ffn_down", "ssd_conv_w", "ffn_conv_w")
REPLICATED = ("norm_mix", "pool_mix", "pool_scale", "f_bias", "ssd_conv_b", "ssd_dt_bias", "ssd_a_log", "ssd_d",
              "ssd_norm", "norm_ffn", "ffn_conv_b", "norm_final")
WEIGHTS = ("norm_mix", "w_in", "pool_mix", "pool_scale", "f_bias", "ssd_conv_w", "ssd_conv_b", "ssd_dt_bias",
           "ssd_a_log", "ssd_d", "ssd_norm", "p_pool", "p_attn", "p_ssd", "w_out", "norm_ffn", "ffn_up",
           "ffn_conv_w", "ffn_conv_b", "ffn_down", "norm_final")


def _round_up(n, k):
    return -(-n // k) * k


def _pad_last(x, width):
    return jnp.pad(x, [(0, 0)] * (x.ndim - 1) + [(0, width - x.shape[-1])])


def _flat_rows(parts, rows):
    flat = jnp.concatenate([p.reshape(-1) for p in parts])
    return jnp.pad(flat, (0, rows * FLAT_W - flat.shape[0])).reshape(rows, FLAT_W)


def _lanes(v, lane0):
    return jnp.pad(v, (lane0, LANE - lane0 - v.shape[0]))[None]


def _cols_by_device(g):
    _, L, R, c = g.shape
    return jnp.moveaxis(g, 0, 2).reshape(L, R, N_DEV * c)


def _rows_by_device(g):
    _, L, r, C = g.shape
    return jnp.moveaxis(g, 0, 1).reshape(L, N_DEV * r, C)


def _gather_payload(W, l, shard_pad, up_pad):
    bf = lambda n: W[n][l].astype(BF16)
    return [_pad_last(bf("w_in"), shard_pad), _pad_last(bf("ffn_up"), up_pad), bf("p_pool"), bf("p_attn"),
            bf("p_ssd"), bf("w_out"), bf("ffn_down"), W["ssd_conv_w"][l], W["ffn_conv_w"][l]]


def _proj_params(g_win, W, l, shard):
    w_qkv, w_rest = _repack_w_in(g_win[:, None], shard, name=f"l{l}_repack_w_in")
    return dict(nm=W["norm_mix"][l][None], w_qkv=w_qkv[0], w_rest=w_rest[0], fb=_lanes(W["f_bias"][l], 0), shard=shard)


def _mixer_params(gathered, W, l, up_w, up_pad):
    g_up, g_pp, g_pa, g_ps, g_wo, g_dn, g_cw, g_fcw = [g[:, None] for g in gathered]
    half = N_DEV // 2
    w_down = _rows_by_device(g_dn)[0]
    w_down_p = jnp.pad(w_down.reshape(half, up_w, -1), ((0, 0), (0, up_pad - up_w), (0, 0))).reshape(half * up_pad, -1)
    fcb_p = _pad_last(W["ffn_conv_b"][l].reshape(N_DEV, up_w), up_pad).reshape(1, N_DEV * up_pad)
    return dict(
        mix=W["pool_mix"][l].astype(BF16), pscale=W["pool_scale"][l][None], cw=_cols_by_device(g_cw)[0],
        cb=W["ssd_conv_b"][l][None], dtb=_lanes(W["ssd_dt_bias"][l], DT_LANE0),
        alog=_lanes(W["ssd_a_log"][l], DT_LANE0), dsk=_lanes(W["ssd_d"][l], DT_LANE0), snw=W["ssd_norm"][l][None],
        p_pool=_cols_by_device(g_pp)[0], p_attn=_cols_by_device(g_pa)[0], p_ssd=_rows_by_device(g_ps)[0],
        w_out=_rows_by_device(g_wo)[0], nf=W["norm_ffn"][l][None], g_up=g_up, up_blocks=(N_DEV, up_w, up_pad),
        fcw=_cols_by_device(_pad_last(g_fcw, up_pad))[0], fcb=fcb_p, w_down=w_down_p)


def _layer_fwd(x, W, l, g_win, g_mixer, to_gather, dims):
    nm = lambda s: f"l{l}_{s}"
    P = _proj_params(g_win, W, l, dims[0])
    u = _rms_fwd(x, P["nm"], name=nm("rms_mix"))
    qkv = _mm(u, P["w_qkv"], out_dtype=BF16, name=nm("proj_qkv"))
    rest = _mm(u, P["w_rest"], name=nm("proj_rest"))
    ccol = _fox_pre(rest, REST_FDT_BLK, P["fb"], name=nm("fox_pre"))
    o, lrow, got = _attn_fwd(qkv, ccol, comm=("gather", to_gather) if to_gather else None, name=nm("attn"))
    if g_mixer is None:
        g_mixer, got = got[:len(SHARDED) - 1], got[len(SHARDED) - 1:]
    P.update(_mixer_params(g_mixer, W, l, *dims[1:]))
    d, ya0 = _pool_fwd(rest, REST_POOL_BLK, P["mix"], P["pscale"], name=nm("pool"))
    ya = _mm(ya0, P["p_pool"], name=nm("p_pool"))
    yb = _mm(o, P["p_attn"], name=nm("p_attn"))
    xc = _conv_silu_fwd(rest, REST_XBC0, SSD_CH, P["cw"], P["cb"], name=nm("ssd_conv"))
    y, hs = _ssd_fwd(xc, rest, REST_FDT_BLK, P["dtb"], P["alog"], P["dsk"], name=nm("ssd"))
    yc0 = _gate_norm_fwd(y, rest, REST_Z_BLK, P["snw"], name=nm("ssd_norm"))
    yc = _mm(yc0, P["p_ssd"], name=nm("p_ssd"))
    merged = _merge_fwd(rest, ya, yb, yc, name=nm("merge"))
    x1 = _mm(merged, P["w_out"], add=x, name=nm("w_out"))
    u2 = _rms_fwd(x1, P["nf"], name=nm("rms_ffn"))
    h0 = _mm_up_fwd(u2, P["g_up"], 0, name=nm("ffn_up"))
    a = _ffn_act_fwd(h0, P["fcw"], P["fcb"], name=nm("ffn_act"))
    x2 = _mm(a, P["w_down"], add=x1, name=nm("ffn_down"))
    saved = dict(x=x, u=u, qkv=qkv, rest=rest, d=d, ya0=ya0, ya=ya, ccol=ccol, o=o, lrow=lrow, yb=yb,
                 xc=xc, y=y, hs=hs, yc0=yc0, yc=yc, merged=merged, x1=x1, u2=u2, h0=h0, a=a)
    return x2, saved, P, got


EARLY_GRADS = ("ffn_up", "ffn_down", "ffn_conv_w", "p_pool", "p_attn", "p_ssd", "w_out")
LATE_GRADS = ("w_in", "ssd_conv_w")


def _layer_bwd(g, gb, sv, P, l):
    nm = lambda s: f"l{l}_{s}_bwd"
    rest = sv["rest"]
    G = {}
    up_blocks, up_w, up_pad = P["up_blocks"]
    half = up_blocks // 2
    da = _mm(gb, P["w_down"], tb=True, name=nm("ffn_down_dx"))
    dwd = _mm(sv["a"], gb, ta=True, out_dtype=GRAD_WIRE, name=nm("ffn_down_dw"))
    G["ffn_down"] = dwd.reshape(half, up_pad, -1)[:, :up_w].reshape(N_DEV, -1, dwd.shape[1])
    dh, dfcw, dfcb = _ffn_act_bwd(sv["h0"], P["fcw"], P["fcb"], da, name=nm("ffn_act"))
    du2 = _mm_up_dx(dh, P["g_up"], 0, name=nm("ffn_up_dx"))
    G["ffn_up"] = _mm_up_dw(sv["u2"], dh, up_blocks, up_w, name=nm("ffn_up_dw"))
    taps = dfcw.shape[1]
    G["ffn_conv_w"] = jnp.moveaxis(dfcw.reshape(2, taps, half, up_pad)[..., :up_w], 2, 1).reshape(N_DEV, taps, up_w)
    G["ffn_conv_b"] = dfcb.reshape(2, half, up_pad)[..., :up_w].reshape(-1)
    dx1, dx1b, dnf = _rms_bwd(sv["x1"], P["nf"], du2, g, name=nm("rms_ffn"))
    G["norm_ffn"] = dnf[0]
    dmerged = _mm(dx1b, P["w_out"], tb=True, name=nm("w_out_dx"))
    dwo = _mm(sv["merged"], dx1b, ta=True, out_dtype=GRAD_WIRE, name=nm("w_out_dw"))
    G["w_out"] = dwo.reshape(N_DEV, -1, dwo.shape[1])
    dgl, dya, dyb, dyc = _merge_bwd(rest, sv["ya"], sv["yb"], sv["yc"], dmerged, name=nm("merge"))
    col_blocks = lambda dw: jnp.moveaxis(dw.reshape(dw.shape[0], N_DEV, -1), 1, 0)
    G["p_pool"] = col_blocks(_mm(sv["ya0"], dya, ta=True, out_dtype=GRAD_WIRE, name=nm("p_pool_dw")))
    dya0 = _mm(dya, P["p_pool"], tb=True, name=nm("p_pool_dx"))
    G["p_attn"] = col_blocks(_mm(sv["o"], dyb, ta=True, out_dtype=GRAD_WIRE, name=nm("p_attn_dw")))
    do = _mm(dyb, P["p_attn"], tb=True, out_dtype=BF16, name=nm("p_attn_dx"))
    dps = _mm(sv["yc0"], dyc, ta=True, out_dtype=GRAD_WIRE, name=nm("p_ssd_dw"))
    G["p_ssd"] = dps.reshape(N_DEV, -1, dps.shape[1])
    dyc0 = _mm(dyc, P["p_ssd"], tb=True, name=nm("p_ssd_dx"))
    dpv, dmix, dpsc = _pool_bwd(dya0, sv["d"], P["mix"], P["pscale"], name=nm("pool"))
    G["pool_mix"], G["pool_scale"] = dmix, dpsc[0]
    dq, dk, dv, dcq, dck, got = _attn_bwd(sv["qkv"], sv["o"], do, sv["lrow"], sv["ccol"],
                                          comm=("exchange", [G[n] for n in EARLY_GRADS]), name=nm("attn"))
    received = {(n, l): r for n, r in zip(EARLY_GRADS, got)}
    dy, dz, dsnw = _gate_norm_bwd(sv["y"], rest, REST_Z_BLK, P["snw"], dyc0, name=nm("ssd_norm"))
    G["ssd_norm"] = dsnw[0]
    dxc, ddt, dpar = _ssd_bwd(sv["xc"], rest, REST_FDT_BLK, P["dtb"], P["alog"], P["dsk"], sv["hs"], dy, name=nm("ssd"))
    G["ssd_a_log"] = dpar[0, DT_LANE0:DT_LANE0 + SSD_HEADS]
    G["ssd_d"] = dpar[1, DT_LANE0:DT_LANE0 + SSD_HEADS]
    dxbc, dcw, dcb = _conv_silu_bwd(rest, REST_XBC0, SSD_CH, P["cw"], P["cb"], dxc, name=nm("ssd_conv"))
    G["ssd_conv_w"] = jnp.moveaxis(dcw.reshape(dcw.shape[0], N_DEV, -1), 1, 0)
    G["ssd_conv_b"] = dcb[0]
    dfdt, dfdtb = _fox_post(dcq, dck, rest, REST_FDT_BLK, P["fb"], ddt, name=nm("fox_post"))
    G["f_bias"] = dfdtb[0, :ATTN_HEADS]
    G["ssd_dt_bias"] = dfdtb[0, DT_LANE0:DT_LANE0 + SSD_HEADS]
    dqkv = jnp.concatenate([dq, dk, dv], axis=1)
    drest = jnp.concatenate([dgl, dz, dpv, dxbc, dfdt], axis=1)
    dwq = _mm(sv["u"], dqkv, ta=True, out_dtype=GRAD_WIRE, name=nm("proj_qkv_dw"))
    dwr = _mm(sv["u"], drest, ta=True, out_dtype=GRAD_WIRE, name=nm("proj_rest_dw"))
    G["w_in"] = _repack_dw_in([dwq], [dwr], P["shard"], name=nm("repack_dw_in"))[:, 0]
    du = _mm(dqkv, P["w_qkv"], tb=True, name=nm("proj_qkv_dx"))
    du, got = _mm(drest, P["w_rest"], tb=True, add=du, comm=("exchange", [G[n] for n in LATE_GRADS]),
                  name=nm("proj_rest_dx"))
    received.update({(n, l): r for n, r in zip(LATE_GRADS, got)})
    dx, dxb, dnm = _rms_bwd(sv["x"], P["nm"], du, dx1, name=nm("rms_mix"))
    G["norm_mix"] = dnm[0]
    return dx, dxb, G, received


def kernel(x, norm_mix, w_in, pool_mix, pool_scale, f_bias, ssd_conv_w, ssd_conv_b, ssd_dt_bias, ssd_a_log, ssd_d, ssd_norm, p_pool, p_attn, p_ssd, w_out, norm_ffn, ffn_up, ffn_conv_w, ffn_conv_b, ffn_down, norm_final, loss_target, m_norm_mix, m_w_in, m_pool_mix, m_pool_scale, m_f_bias, m_ssd_conv_w, m_ssd_conv_b, m_ssd_dt_bias, m_ssd_a_log, m_ssd_d, m_ssd_norm, m_p_pool, m_p_attn, m_p_ssd, m_w_out, m_norm_ffn, m_ffn_up, m_ffn_conv_w, m_ffn_conv_b, m_ffn_down, m_norm_final, v_norm_mix, v_w_in, v_pool_mix, v_pool_scale, v_f_bias, v_ssd_conv_w, v_ssd_conv_b, v_ssd_dt_bias, v_ssd_a_log, v_ssd_d, v_ssd_norm, v_p_pool, v_p_attn, v_p_ssd, v_w_out, v_norm_ffn, v_ffn_up, v_ffn_conv_w, v_ffn_conv_b, v_ffn_down, v_norm_final):
    args = dict(locals())
    W = {n: args[n] for n in WEIGHTS}
    M = {n: args["m_" + n] for n in WEIGHTS}
    V = {n: args["v_" + n] for n in WEIGHTS}
    L = norm_mix.shape[0]
    shard = w_in.shape[-1]
    up_w = ffn_up.shape[-1]
    up_pad = _round_up(up_w, LANE)

    payload = lambda l: _gather_payload(W, l, _round_up(shard, LANE), up_pad)
    first = payload(0)
    g_win = _all_gather(first[:1], name="gather_w_in_l0")[0]
    g_mixer, to_gather = None, first[1:]
    h, saves, params = x[0], [], []
    for l in range(L):
        to_gather = to_gather + (payload(l + 1) if l + 1 < L else [])
        h, sv, P, got = _layer_fwd(h, W, l, g_win, g_mixer, to_gather, (shard, up_w, up_pad))
        saves.append(sv)
        params.append(P)
        if l + 1 < L:
            g_win, g_mixer, to_gather = got[0], got[1:], []
    loss_part, g, gb, dnfin = _loss_head(h, norm_final[None], loss_target[0], name="loss_head")

    G, received = [None] * L, {}
    for l in reversed(range(L)):
        g, gb, G[l], got = _layer_bwd(g, gb, saves[l], params[l], l)
        received.update(got)

    rep_grads = {n: jnp.stack([G[l][n] for l in range(L)]) for n in REPLICATED if n != "norm_final"}
    rep_grads["norm_final"] = dnfin[0]
    rep_size = sum(W[n].size for n in REPLICATED)
    rep_rows = _round_up(-(-rep_size // (N_DEV * FLAT_W)), SUBLANE)
    rep_send = _flat_rows([rep_grads[n] for n in REPLICATED], N_DEV * rep_rows).reshape(N_DEV, rep_rows, FLAT_W)
    last = _exchange([rep_send], name="exchange_replicated_grads")

    out = {}
    for n in SHARDED:
        shape = W[n].shape
        as3 = lambda a: a.reshape(L, -1, shape[-1])
        res = _sum_adam([received[(n, l)] for l in range(L)], as3(W[n]), as3(M[n]), as3(V[n]), name="sum_adam_" + n)
        for key, arr in zip(("grad_", "delta_", "new_m_", "new_v_"), res):
            out[key + n] = arr.reshape(shape)

    rep_sum = _sum_parts(last[-1], name="sum_replicated")
    rep_g = _all_gather([rep_sum], name="gather_replicated_grads")[0].reshape(N_DEV * rep_rows, FLAT_W)
    flat_rep = lambda D: _flat_rows([D[n] for n in REPLICATED], N_DEV * rep_rows)
    d_rp, m_rp, v_rp = _adam(rep_g, flat_rep(W), flat_rep(M), flat_rep(V), name="adam_replicated")
    off = 0
    for n in REPLICATED:
        size = W[n].size
        for key, arr in (("grad_", rep_g), ("delta_", d_rp), ("new_m_", m_rp), ("new_v_", v_rp)):
            out[key + n] = arr.reshape(-1)[off:off + size].reshape(W[n].shape)
        off += size

    loss = lax.psum(loss_part[0, 0], ("x", "y", "c"))
    return (loss, g[None], *[out["grad_" + n] for n in WEIGHTS], *[out["delta_" + n] for n in WEIGHTS],
            *[out["new_m_" + n] for n in WEIGHTS], *[out["new_v_" + n] for n in WEIGHTS])
```

```python
import functools

import jax
import jax.numpy as jnp
from jax import lax
from jax.experimental import pallas as pl
from jax.experimental.pallas import tpu as pltpu

F32 = jnp.float32
BF16 = jnp.bfloat16

LANE = 128
SUBLANE = 8
VMEM_LIMIT = 56 * 1024 * 1024

NORM_EPS = 1e-6
POOL_WINDOWS = (2, 4, 8, 16)
POOL_GROUP_DIM = 128
ATTN_HEADS = 8
ATTN_HEAD_DIM = 64
SSD_HEAD_DIM = 64
SSD_HEADS = 16
SSD_GROUPS = 2
SSD_STATE = 128
SSD_CHUNK = 128

ADAM_LR = 0.001
ADAM_B1 = 0.9
ADAM_B2 = 0.999
ADAM_EPS = 1e-08
ADAM_WD = 0.01
ADAM_STEP = 10


def _cparams(*sem):
    return pltpu.CompilerParams(dimension_semantics=tuple(sem), vmem_limit_bytes=VMEM_LIMIT)


def _tile(n, pref):
    if n <= pref:
        return n
    assert n % LANE == 0, n
    q = n // LANE
    best = 1
    for d in range(1, pref // LANE + 1):
        if q % d == 0:
            best = d
    return best * LANE


GRAD_WIRE = BF16


def _mm_call(args, in_specs, out_spec, out_shape, grid, *, ta, tb, nk, acc_shape, has_add, keep, name, comm=None):
    dims = (((0 if ta else 1,), (1 if tb else 0,)), ((), ()))
    nc = len(comm[1]) if comm else 0
    n_in = 2 + has_add

    def body(*refs):
        a_ref, b_ref = refs[0], refs[1]
        o_ref = refs[n_in + nc]
        acc_ref = refs[n_in + 2 * nc + 1] if nk > 1 else None
        carried = (refs[n_in:n_in + nc], refs[n_in + nc + 1:n_in + 2 * nc + 1], refs[n_in + 2 * nc + 1 + (nk > 1):])
        i, j, k = pl.program_id(0), pl.program_id(1), pl.program_id(2)
        if comm:
            _carry_start(comm[0], *carried, jnp.logical_and(jnp.logical_and(i == 0, j == 0), k == 0))
        d = lax.dot_general(a_ref[...], b_ref[...], dims, preferred_element_type=F32)

        def finish(r):
            if has_add:
                r = r + refs[2][...]
            if keep is not None:
                r = r[:, :keep]
            o_ref[...] = r.astype(o_ref.dtype)

        if nk == 1:
            finish(d)
        else:
            @pl.when(k == 0)
            def _():
                acc_ref[...] = d

            @pl.when(jnp.logical_and(k > 0, k < nk - 1))
            def _():
                acc_ref[...] += d

            @pl.when(k == nk - 1)
            def _():
                finish(acc_ref[...] + d)

        if comm:
            _carry_wait(comm[0], *carried, jnp.logical_and(jnp.logical_and(i == grid[0] - 1, j == grid[1] - 1),
                                                           k == nk - 1))

    res = pl.pallas_call(
        body, name=name, grid=grid, in_specs=in_specs + [ANY] * nc, out_specs=[out_spec] + [ANY] * nc,
        out_shape=[out_shape] + (_a2a_out_shapes(*comm) if comm else []),
        scratch_shapes=([pltpu.VMEM(acc_shape, F32)] if nk > 1 else []) + (_a2a_sems(nc) if comm else []),
        compiler_params=_cparams(*(("arbitrary",) * 3 if comm else ("parallel", "parallel", "arbitrary"))),
    )(*args, *(comm[1] if comm else []))
    return (res[0], list(res[1:])) if comm else res[0]


MM_VMEM_BUDGET = 40 * 1024 * 1024
MM_MAX_ROWS = 2048


MM_MIN_ROWS_WHOLE_K = 256


def _mm_rows(M, tn, tk, nk, out_bytes, has_add):
    best = None
    for tm in range(LANE, min(M, MM_MAX_ROWS) + 1, LANE):
        if M % tm:
            continue
        blocks = 2 * (tm * tk * 2 + tk * tn * 2 + tm * tn * out_bytes + (tm * tn * 4 if has_add else 0))
        blocks += tm * tn * 4 * (2 if nk > 1 else 1)
        if blocks <= MM_VMEM_BUDGET:
            best = tm
    return M if best is None else best


def _mm_tiles(M, N, K, out_bytes, has_add, whole_k_ok):
    tn = _tile(N, 1024)
    if whole_k_ok and K > 2048 and M % MM_MIN_ROWS_WHOLE_K == 0:
        tm = _mm_rows(M, tn, K, 1, out_bytes, has_add)
        need = 2 * (tm * K * 2 + K * tn * 2 + tm * tn * out_bytes + (tm * tn * 4 if has_add else 0)) + tm * tn * 4
        if tm >= MM_MIN_ROWS_WHOLE_K and need <= MM_VMEM_BUDGET:
            return tm, tn, K
    tk = _tile(K, 2048)
    return _mm_rows(M, tn, tk, K // tk, out_bytes, has_add), tn, tk


def _mm(a, b, *, ta=False, tb=False, add=None, out_dtype=F32, comm=None, name):
    assert a.dtype == BF16 and b.dtype == BF16, (a.dtype, b.dtype)
    if ta:
        K, M = a.shape
    else:
        M, K = a.shape
    if tb:
        N, K2 = b.shape
    else:
        K2, N = b.shape
    assert K == K2, (a.shape, b.shape, ta, tb)
    tm, tn, tk = _mm_tiles(M, N, K, jnp.dtype(out_dtype).itemsize, add is not None, whole_k_ok=not ta)
    a_spec = pl.BlockSpec((tk, tm), lambda i, j, k: (k, i)) if ta else pl.BlockSpec((tm, tk), lambda i, j, k: (i, k))
    b_spec = pl.BlockSpec((tn, tk), lambda i, j, k: (j, k)) if tb else pl.BlockSpec((tk, tn), lambda i, j, k: (k, j))
    in_specs, args = [a_spec, b_spec], [a, b]
    if add is not None:
        in_specs.append(pl.BlockSpec((tm, tn), lambda i, j, k: (i, j)))
        args.append(add)
    return _mm_call(args, in_specs, pl.BlockSpec((tm, tn), lambda i, j, k: (i, j)),
                    jax.ShapeDtypeStruct((M, N), out_dtype), (M // tm, N // tn, K // tk),
                    ta=ta, tb=tb, nk=K // tk, acc_shape=(tm, tn), has_add=add is not None, keep=None, name=name,
                    comm=comm)


def _half_of(b, half):
    hi = jnp.where(b >= half, 1, 0)
    return hi, b - half * hi


def _mm_up_fwd(u, g_up, l, *, name):
    S, K = u.shape
    nblk, _, _, bw = g_up.shape
    tm = _mm_rows(S, bw, K, 1, 4, False)
    return _mm_call([u, g_up],
                    [pl.BlockSpec((tm, K), lambda i, j, k: (i, 0)),
                     pl.BlockSpec((None, None, K, bw), lambda i, j, k: (j, l, 0, 0))],
                    pl.BlockSpec((tm, bw), lambda i, j, k: (i, j)),
                    jax.ShapeDtypeStruct((S, nblk * bw), F32), (S // tm, nblk, 1),
                    ta=False, tb=False, nk=1, acc_shape=(tm, bw), has_add=False, keep=None, name=name)


def _mm_up_dx(dh, g_up, l, *, name):
    _, S, F = dh.shape
    nblk, _, N, bw = g_up.shape
    half = nblk // 2
    tm = min(MM_MIN_ROWS_WHOLE_K, S)

    def body(dh_ref, g_ref, o_ref):
        acc = None
        for j in range(nblk):
            a = dh_ref[j // half, :, (j % half) * bw:(j % half + 1) * bw]
            d = lax.dot_general(a, g_ref[j], (((1,), (1,)), ((), ())), preferred_element_type=F32)
            acc = d if acc is None else acc + d
        o_ref[...] = acc

    return pl.pallas_call(
        body, name=name, grid=(S // tm,),
        in_specs=[pl.BlockSpec((2, tm, F), lambda i: (0, i, 0)),
                  pl.BlockSpec((nblk, None, N, bw), lambda i: (0, l, 0, 0))],
        out_specs=pl.BlockSpec((tm, N), lambda i: (i, 0)),
        out_shape=jax.ShapeDtypeStruct((S, N), F32),
        compiler_params=_cparams("parallel"),
    )(dh, g_up)


def _mm_up_dw(u, dh, nblk, keep, *, name):
    S, M = u.shape
    half = nblk // 2
    bw = dh.shape[2] // half
    tk = _tile(S, 2048)
    tm = _mm_rows(M, bw, tk, S // tk, jnp.dtype(GRAD_WIRE).itemsize, False)
    return _mm_call([u, dh],
                    [pl.BlockSpec((tk, tm), lambda i, j, k: (k, i)),
                     pl.BlockSpec((None, tk, bw), lambda i, j, k: (_half_of(j, half)[0], k, _half_of(j, half)[1]))],
                    pl.BlockSpec((None, tm, keep), lambda i, j, k: (j, i, 0)),
                    jax.ShapeDtypeStruct((nblk, M, keep), GRAD_WIRE), (M // tm, nblk, S // tk),
                    ta=True, tb=False, nk=S // tk, acc_shape=(tm, bw), has_add=False, keep=keep, name=name)


ROW_TILE = 512
HALO = 8
POOL_HALO = 16


def _row_tile(S):
    return min(ROW_TILE, S)


def _acc_out(ref, val, first):
    @pl.when(first)
    def _():
        ref[...] = val

    @pl.when(jnp.logical_not(first))
    def _():
        ref[...] += val


def _rms_fwd(x, w, *, name):
    S, D = x.shape
    ts = _row_tile(S)

    def body(x_ref, w_ref, u_ref):
        xv = x_ref[...]
        r = lax.rsqrt(jnp.mean(xv * xv, axis=-1, keepdims=True) + NORM_EPS)
        u_ref[...] = (xv * r * w_ref[...]).astype(BF16)

    return pl.pallas_call(
        body, name=name, grid=(S // ts,),
        in_specs=[pl.BlockSpec((ts, D), lambda i: (i, 0)), pl.BlockSpec((1, D), lambda i: (0, 0))],
        out_specs=pl.BlockSpec((ts, D), lambda i: (i, 0)),
        out_shape=jax.ShapeDtypeStruct((S, D), BF16),
        compiler_params=_cparams("parallel"),
    )(x, w)


def _rms_bwd(x, w, du, g, *, name):
    S, D = x.shape
    ts = _row_tile(S)

    def body(x_ref, w_ref, du_ref, g_ref, dx_ref, dxb_ref, dw_ref):
        xv = x_ref[...]
        r = lax.rsqrt(jnp.mean(xv * xv, axis=-1, keepdims=True) + NORM_EPS)
        n = xv * r
        duv = du_ref[...]
        dn = duv * w_ref[...]
        dx = g_ref[...] + r * (dn - n * jnp.mean(dn * n, axis=-1, keepdims=True))
        dx_ref[...] = dx
        dxb_ref[...] = dx.astype(BF16)
        _acc_out(dw_ref, jnp.sum(duv * n, axis=0, keepdims=True), pl.program_id(0) == 0)

    row = pl.BlockSpec((ts, D), lambda i: (i, 0))
    vec = pl.BlockSpec((1, D), lambda i: (0, 0))
    return pl.pallas_call(
        body, name=name, grid=(S // ts,),
        in_specs=[row, vec, row, row],
        out_specs=[row, row, vec],
        out_shape=[jax.ShapeDtypeStruct((S, D), F32), jax.ShapeDtypeStruct((S, D), BF16),
                   jax.ShapeDtypeStruct((1, D), F32)],
        compiler_params=_cparams("arbitrary"),
    )(x, w, du, g)


def _loss_head(x, w, target, *, name):
    S, D = x.shape
    ts = _row_tile(S)

    def body(x_ref, w_ref, t_ref, loss_ref, dx_ref, dxb_ref, dw_ref):
        xv = x_ref[...]
        wv = w_ref[...]
        r = lax.rsqrt(jnp.mean(xv * xv, axis=-1, keepdims=True) + NORM_EPS)
        n = xv * r
        err = n * wv - t_ref[...]
        part = 0.5 * jnp.sum(jnp.mean(err * err, axis=-1, keepdims=True), axis=0, keepdims=True)
        dy = err * (1.0 / D)
        dn = dy * wv
        dx = r * (dn - n * jnp.mean(dn * n, axis=-1, keepdims=True))
        dx_ref[...] = dx
        dxb_ref[...] = dx.astype(BF16)
        first = pl.program_id(0) == 0
        _acc_out(dw_ref, jnp.sum(dy * n, axis=0, keepdims=True), first)
        _acc_out(loss_ref, jnp.broadcast_to(part, loss_ref.shape), first)

    row = pl.BlockSpec((ts, D), lambda i: (i, 0))
    vec = pl.BlockSpec((1, D), lambda i: (0, 0))
    return pl.pallas_call(
        body, name=name, grid=(S // ts,),
        in_specs=[row, vec, row],
        out_specs=[pl.BlockSpec((SUBLANE, LANE), lambda i: (0, 0)), row, row, vec],
        out_shape=[jax.ShapeDtypeStruct((SUBLANE, LANE), F32), jax.ShapeDtypeStruct((S, D), F32),
                   jax.ShapeDtypeStruct((S, D), BF16), jax.ShapeDtypeStruct((1, D), F32)],
        compiler_params=_cparams("arbitrary"),
    )(x, w, target)


def _sigmoid(x):
    return 1.0 / (1.0 + jnp.exp(-x))


def _merge_fwd(gl, ya, yb, yc, *, name):
    S, D = ya.shape
    ts = _row_tile(S)

    def body(ga_ref, gb_ref, gc_ref, ya_ref, yb_ref, yc_ref, o_ref):
        m = (_sigmoid(ga_ref[...]) * ya_ref[...] + _sigmoid(gb_ref[...]) * yb_ref[...]
             + _sigmoid(gc_ref[...]) * yc_ref[...])
        o_ref[...] = m.astype(BF16)

    row = pl.BlockSpec((ts, D), lambda i: (i, 0))
    gspec = [pl.BlockSpec((ts, D), functools.partial(lambda b, i: (i, b), b)) for b in range(3)]
    return pl.pallas_call(
        body, name=name, grid=(S // ts,),
        in_specs=gspec + [row, row, row],
        out_specs=row,
        out_shape=jax.ShapeDtypeStruct((S, D), BF16),
        compiler_params=_cparams("parallel"),
    )(gl, gl, gl, ya, yb, yc)


def _merge_bwd(gl, ya, yb, yc, dm, *, name):
    S, D = ya.shape
    ts = _row_tile(S)

    def body(ga_ref, gb_ref, gc_ref, ya_ref, yb_ref, yc_ref, dm_ref, dgl_ref, da_ref, db_ref, dc_ref):
        dmv = dm_ref[...]
        for b, (g_ref, y_ref, dy_ref) in enumerate(((ga_ref, ya_ref, da_ref), (gb_ref, yb_ref, db_ref),
                                                     (gc_ref, yc_ref, dc_ref))):
            s = _sigmoid(g_ref[...])
            dy_ref[...] = (dmv * s).astype(BF16)
            dgl_ref[:, b * D:(b + 1) * D] = (dmv * y_ref[...] * s * (1.0 - s)).astype(BF16)

    row = pl.BlockSpec((ts, D), lambda i: (i, 0))
    gspec = [pl.BlockSpec((ts, D), functools.partial(lambda b, i: (i, b), b)) for b in range(3)]
    return pl.pallas_call(
        body, name=name, grid=(S // ts,),
        in_specs=gspec + [row, row, row, row],
        out_specs=[pl.BlockSpec((ts, 3 * D), lambda i: (i, 0)), row, row, row],
        out_shape=[jax.ShapeDtypeStruct((S, 3 * D), BF16)] + [jax.ShapeDtypeStruct((S, D), BF16)] * 3,
        compiler_params=_cparams("parallel"),
    )(gl, gl, gl, ya, yb, yc, dm)


POOL_WIDTH = 512


def _pool_cnt(t, w):
    return jnp.minimum(t + 1, w).astype(F32)


def _pool_fwd(rest, col, mix, scale, *, name):
    S = rest.shape[0]
    W, G, H = POOL_WIDTH, POOL_GROUP_DIM, POOL_HALO
    ts = _row_tile(S)

    def body(v_ref, h_ref, mix_ref, sc_ref, d_ref, y_ref, ext_ref):
        i = pl.program_id(0)
        cur = v_ref[...]
        ext_ref[0:H, :] = jnp.where(i > 0, h_ref[...], 0.0)
        ext_ref[H:, :] = cur
        t = i * ts + lax.broadcasted_iota(jnp.int32, (ts, 1), 0)
        for g, w in enumerate(POOL_WINDOWS):
            cols = slice(g * G, (g + 1) * G)
            acc = cur[:, cols]
            for k in range(1, w):
                acc = acc + ext_ref[pl.ds(H - k, ts), cols]
            db = (acc / _pool_cnt(t, w) - cur[:, cols]).astype(BF16)
            d_ref[:, cols] = db
            y = jnp.dot(db, mix_ref[g], preferred_element_type=F32) * sc_ref[:, cols]
            y_ref[:, cols] = y.astype(BF16)

    row = pl.BlockSpec((ts, W), lambda i: (i, 0))
    return pl.pallas_call(
        body, name=name, grid=(S // ts,),
        in_specs=[pl.BlockSpec((ts, W), lambda i: (i, col)),
                  pl.BlockSpec((H, W), lambda i: (jnp.maximum(i * (ts // H) - 1, 0), col)),
                  pl.BlockSpec((len(POOL_WINDOWS), G, G), lambda i: (0, 0, 0)),
                  pl.BlockSpec((1, W), lambda i: (0, 0))],
        out_specs=[row, row],
        out_shape=[jax.ShapeDtypeStruct((S, W), BF16)] * 2,
        scratch_shapes=[pltpu.VMEM((H + ts, W), F32)],
        compiler_params=_cparams("parallel"),
    )(rest, rest, mix, scale)


def _pool_bwd(dy, d, mix, scale, *, name):
    S = dy.shape[0]
    W, G, H = POOL_WIDTH, POOL_GROUP_DIM, POOL_HALO
    ts = _row_tile(S)
    n = S // ts
    NT = (((1,), (1,)), ((), ()))
    TN = (((0,), (0,)), ((), ()))

    def body(dy_ref, dyn_ref, d_ref, mix_ref, sc_ref, dv_ref, dmix_ref, dsc_ref, ext_ref):
        i = pl.program_id(0)

        @pl.when(i == 0)
        def _():
            dmix_ref[...] = jnp.zeros_like(dmix_ref)
            dsc_ref[...] = jnp.zeros_like(dsc_ref)

        t = i * ts + lax.broadcasted_iota(jnp.int32, (ts, 1), 0)
        tn = (i + 1) * ts + lax.broadcasted_iota(jnp.int32, (H, 1), 0)
        for g, w in enumerate(POOL_WINDOWS):
            cols = slice(g * G, (g + 1) * G)
            sc = sc_ref[:, cols]
            dyv = dy_ref[:, cols]
            db = d_ref[:, cols]
            mg = mix_ref[g]
            yp = jnp.dot(db, mg, preferred_element_type=F32)
            dsc_ref[:, cols] += jnp.sum(dyv * yp, axis=0, keepdims=True)
            dyp = (dyv * sc).astype(BF16)
            dd = lax.dot_general(dyp, mg, NT, preferred_element_type=F32)
            dmix_ref[g] += lax.dot_general(db, dyp, TN, preferred_element_type=F32)
            dyn = (jnp.where(i < n - 1, dyn_ref[:, cols], 0.0) * sc).astype(BF16)
            ddn = lax.dot_general(dyn, mg, NT, preferred_element_type=F32)
            e = dd / _pool_cnt(t, w)
            ext_ref[0:ts, cols] = e
            ext_ref[ts:, cols] = ddn / _pool_cnt(tn, w)
            acc = e
            for k in range(1, w):
                acc = acc + ext_ref[pl.ds(k, ts), cols]
            dv_ref[:, cols] = (acc - dd).astype(BF16)

    row = pl.BlockSpec((ts, W), lambda i: (i, 0))
    return pl.pallas_call(
        body, name=name, grid=(n,),
        in_specs=[row,
                  pl.BlockSpec((H, W), lambda i: (jnp.minimum((i + 1) * (ts // H), S // H - 1), 0)),
                  row,
                  pl.BlockSpec((len(POOL_WINDOWS), G, G), lambda i: (0, 0, 0)),
                  pl.BlockSpec((1, W), lambda i: (0, 0))],
        out_specs=[row, pl.BlockSpec((len(POOL_WINDOWS), G, G), lambda i: (0, 0, 0)),
                   pl.BlockSpec((1, W), lambda i: (0, 0))],
        out_shape=[jax.ShapeDtypeStruct((S, W), BF16),
                   jax.ShapeDtypeStruct((len(POOL_WINDOWS), G, G), F32),
                   jax.ShapeDtypeStruct((1, W), F32)],
        scratch_shapes=[pltpu.VMEM((ts + H, W), F32)],
        compiler_params=_cparams("arbitrary"),
    )(dy, dy, d, mix, scale)


def _conv_taps(ext_ref, w_ref, b_ref, rows, first):
    K = w_ref.shape[0]
    pre = b_ref[...] + w_ref[K - 1:K, :] * ext_ref[pl.ds(first, rows), :]
    for k in range(K - 1):
        pre = pre + w_ref[k:k + 1, :] * ext_ref[pl.ds(first - (K - 1) + k, rows), :]
    return pre


CONV_ROW_TILE = 1024
ROW_CHUNK = 40


def _chunks(rows):
    ch = max(c for c in range(SUBLANE, ROW_CHUNK + 1, SUBLANE) if rows % c == 0)
    return [(r0, ch) for r0 in range(0, rows, ch)]


def _conv_bwd_taps(g_ref, ext_ref, w_ref, dx_store, ts):
    K = w_ref.shape[0]
    db, dws = 0.0, [0.0] * K
    fold = lambda v: jnp.sum(v.reshape(v.shape[0] // SUBLANE, SUBLANE, v.shape[1]), axis=0)
    for r0, ch in _chunks(ts):
        dx = w_ref[K - 1:K, :] * g_ref[pl.ds(r0, ch), :]
        for k in range(K - 1):
            dx = dx + w_ref[k:k + 1, :] * g_ref[pl.ds(r0 + K - 1 - k, ch), :]
        dx_store(r0, ch, dx.astype(BF16))
        gc = g_ref[pl.ds(r0, ch), :]
        db = db + fold(gc)
        for k in range(K):
            dws[k] = dws[k] + fold(gc * ext_ref[pl.ds(HALO - (K - 1) + k + r0, ch), :])
    total = lambda v: jnp.sum(v, axis=0, keepdims=True)
    return total(db), jnp.concatenate([total(d) for d in dws], axis=0)


def _prev_halo_spec(ts, tc, colfn):
    return pl.BlockSpec((HALO, tc), lambda *g: (jnp.maximum(g[-1] * (ts // HALO) - 1, 0), colfn(*g)))


def _conv_silu_fwd(rest, col0, C, w, b, *, name):
    S = rest.shape[0]
    ts, tc = min(CONV_ROW_TILE, S), 512
    assert col0 % tc == 0 and C % tc == 0
    cb = col0 // tc

    def body(x_ref, h_ref, w_ref, b_ref, o_ref, ext_ref):
        i = pl.program_id(1)
        ext_ref[0:HALO, :] = jnp.where(i > 0, h_ref[...], 0.0)
        ext_ref[HALO:, :] = x_ref[...]
        for r0, ch in _chunks(ts):
            pre = _conv_taps(ext_ref, w_ref, b_ref, ch, HALO + r0)
            o_ref[pl.ds(r0, ch), :] = pre * _sigmoid(pre)

    K = w.shape[0]
    return pl.pallas_call(
        body, name=name, grid=(C // tc, S // ts),
        in_specs=[pl.BlockSpec((ts, tc), lambda j, i: (i, cb + j)),
                  _prev_halo_spec(ts, tc, lambda j, i: cb + j),
                  pl.BlockSpec((K, tc), lambda j, i: (0, j)),
                  pl.BlockSpec((1, tc), lambda j, i: (0, j))],
        out_specs=pl.BlockSpec((ts, tc), lambda j, i: (i, j)),
        out_shape=jax.ShapeDtypeStruct((S, C), F32),
        scratch_shapes=[pltpu.VMEM((HALO + ts, tc), F32)],
        compiler_params=_cparams("parallel", "parallel"),
    )(rest, rest, w, b)


def _dsilu(pre):
    s = _sigmoid(pre)
    return s, s * (1.0 + pre * (1.0 - s))


def _conv_silu_bwd(rest, col0, C, w, b, dy, *, name):
    S = rest.shape[0]
    ts, tc = min(CONV_ROW_TILE, S), 512
    cb = col0 // tc
    n = S // ts
    K = w.shape[0]
    R = ts + HALO

    def body(x_ref, hp_ref, hn_ref, w_ref, b_ref, dy_ref, dyn_ref, dx_ref, dw_ref, db_ref, ext_ref, g_ref):
        i = pl.program_id(1)
        last = i == n - 1
        ext_ref[0:HALO, :] = jnp.where(i > 0, hp_ref[...], 0.0)
        ext_ref[HALO:HALO + ts, :] = x_ref[...]
        ext_ref[HALO + ts:, :] = jnp.where(last, 0.0, hn_ref[...])
        g_ref[0:ts, :] = dy_ref[...]
        g_ref[ts:, :] = jnp.where(last, 0.0, dyn_ref[...])
        for r0, ch in _chunks(R):
            pre = _conv_taps(ext_ref, w_ref, b_ref, ch, HALO + r0)
            g_ref[pl.ds(r0, ch), :] = g_ref[pl.ds(r0, ch), :] * _dsilu(pre)[1]

        def dx_store(r0, ch, v):
            dx_ref[pl.ds(r0, ch), :] = v

        db, dw = _conv_bwd_taps(g_ref, ext_ref, w_ref, dx_store, ts)
        _acc_out(db_ref, db, i == 0)
        _acc_out(dw_ref, dw, i == 0)

    nxt = lambda cf: pl.BlockSpec((HALO, tc), lambda j, i: (jnp.minimum((i + 1) * (ts // HALO), S // HALO - 1), cf(j)))
    return pl.pallas_call(
        body, name=name, grid=(C // tc, n),
        in_specs=[pl.BlockSpec((ts, tc), lambda j, i: (i, cb + j)),
                  _prev_halo_spec(ts, tc, lambda j, i: cb + j),
                  nxt(lambda j: cb + j),
                  pl.BlockSpec((K, tc), lambda j, i: (0, j)),
                  pl.BlockSpec((1, tc), lambda j, i: (0, j)),
                  pl.BlockSpec((ts, tc), lambda j, i: (i, j)),
                  nxt(lambda j: j)],
        out_specs=[pl.BlockSpec((ts, tc), lambda j, i: (i, j)),
                   pl.BlockSpec((K, tc), lambda j, i: (0, j)),
                   pl.BlockSpec((1, tc), lambda j, i: (0, j))],
        out_shape=[jax.ShapeDtypeStruct((S, C), BF16), jax.ShapeDtypeStruct((K, C), F32),
                   jax.ShapeDtypeStruct((1, C), F32)],
        scratch_shapes=[pltpu.VMEM((HALO + R, tc), F32), pltpu.VMEM((R, tc), F32)],
        compiler_params=_cparams("parallel", "arbitrary"),
    )(rest, rest, rest, w, b, dy, dy)


FFN_TC = 256


def _ffn_act_fwd(h0, w, b, *, name):
    S, F2 = h0.shape
    F = F2 // 2
    ts, tc = min(CONV_ROW_TILE, S), FFN_TC
    nb = F // tc
    K = w.shape[0]

    def body(xg_ref, hg_ref, xv_ref, hv_ref, wg_ref, wv_ref, bg_ref, bv_ref, o_ref, eg_ref, ev_ref):
        i = pl.program_id(1)
        for x_ref, h_ref, e_ref in ((xg_ref, hg_ref, eg_ref), (xv_ref, hv_ref, ev_ref)):
            e_ref[0:HALO, :] = jnp.where(i > 0, h_ref[...], 0.0)
            e_ref[HALO:, :] = x_ref[...]
        for r0, ch in _chunks(ts):
            pg = _conv_taps(eg_ref, wg_ref, bg_ref, ch, HALO + r0)
            pv = _conv_taps(ev_ref, wv_ref, bv_ref, ch, HALO + r0)
            o_ref[pl.ds(r0, ch), :] = (pg * _sigmoid(pg) * pv).astype(BF16)

    def half(off):
        return [pl.BlockSpec((ts, tc), lambda j, i: (i, off + j)), _prev_halo_spec(ts, tc, lambda j, i: off + j)]

    wspec = lambda off: pl.BlockSpec((K, tc), lambda j, i: (0, off + j))
    bspec = lambda off: pl.BlockSpec((1, tc), lambda j, i: (0, off + j))
    return pl.pallas_call(
        body, name=name, grid=(nb, S // ts),
        in_specs=half(0) + half(nb) + [wspec(0), wspec(nb), bspec(0), bspec(nb)],
        out_specs=pl.BlockSpec((ts, tc), lambda j, i: (i, j)),
        out_shape=jax.ShapeDtypeStruct((S, F), BF16),
        scratch_shapes=[pltpu.VMEM((HALO + ts, tc), F32)] * 2,
        compiler_params=_cparams("parallel", "parallel"),
    )(h0, h0, h0, h0, w, w, b, b)


def _ffn_act_bwd(h0, w, b, da, *, name):
    S, F2 = h0.shape
    F = F2 // 2
    ts, tc = min(CONV_ROW_TILE, S), FFN_TC
    nb = F // tc
    n = S // ts
    K = w.shape[0]
    R = ts + HALO

    def body(xg_ref, pg_ref, ng_ref, xv_ref, pv_ref, nv_ref, wg_ref, wv_ref, bg_ref, bv_ref, da_ref, dan_ref,
             dx_ref, dw_ref, db_ref, eg_ref, ev_ref, gg_ref, gv_ref):
        i = pl.program_id(1)
        last = i == n - 1
        for x_ref, p_ref, n_ref, e_ref in ((xg_ref, pg_ref, ng_ref, eg_ref), (xv_ref, pv_ref, nv_ref, ev_ref)):
            e_ref[0:HALO, :] = jnp.where(i > 0, p_ref[...], 0.0)
            e_ref[HALO:HALO + ts, :] = x_ref[...]
            e_ref[HALO + ts:, :] = jnp.where(last, 0.0, n_ref[...])
        gg_ref[0:ts, :] = da_ref[...]
        gg_ref[ts:, :] = jnp.where(last, 0.0, dan_ref[...])
        for r0, ch in _chunks(R):
            rows = pl.ds(r0, ch)
            dav = gg_ref[rows, :]
            pg = _conv_taps(eg_ref, wg_ref, bg_ref, ch, HALO + r0)
            pv = _conv_taps(ev_ref, wv_ref, bv_ref, ch, HALO + r0)
            s, ds = _dsilu(pg)
            gg_ref[rows, :] = dav * pv * ds
            gv_ref[rows, :] = dav * pg * s
        for h, (g_ref, e_ref, w_ref) in enumerate(((gg_ref, eg_ref, wg_ref), (gv_ref, ev_ref, wv_ref))):
            def dx_store(r0, ch, v):
                dx_ref[h, pl.ds(r0, ch), :] = v

            db, dw = _conv_bwd_taps(g_ref, e_ref, w_ref, dx_store, ts)
            _acc_out(db_ref.at[h], db, i == 0)
            _acc_out(dw_ref.at[h], dw, i == 0)

    nxt = lambda off: pl.BlockSpec((HALO, tc), lambda j, i: (jnp.minimum((i + 1) * (ts // HALO), S // HALO - 1), off + j))

    def half(off):
        return [pl.BlockSpec((ts, tc), lambda j, i: (i, off + j)), _prev_halo_spec(ts, tc, lambda j, i: off + j), nxt(off)]

    wspec = lambda off: pl.BlockSpec((K, tc), lambda j, i: (0, off + j))
    bspec = lambda off: pl.BlockSpec((1, tc), lambda j, i: (0, off + j))
    return pl.pallas_call(
        body, name=name, grid=(nb, n),
        in_specs=half(0) + half(nb) + [wspec(0), wspec(nb), bspec(0), bspec(nb),
                                       pl.BlockSpec((ts, tc), lambda j, i: (i, j)), nxt(0)],
        out_specs=[pl.BlockSpec((2, ts, tc), lambda j, i: (0, i, j)),
                   pl.BlockSpec((2, K, tc), lambda j, i: (0, 0, j)),
                   pl.BlockSpec((2, 1, tc), lambda j, i: (0, 0, j))],
        out_shape=[jax.ShapeDtypeStruct((2, S, F), BF16), jax.ShapeDtypeStruct((2, K, F), F32),
                   jax.ShapeDtypeStruct((2, 1, F), F32)],
        scratch_shapes=[pltpu.VMEM((HALO + R, tc), F32)] * 2 + [pltpu.VMEM((R, tc), F32)] * 2,
        compiler_params=_cparams("parallel", "arbitrary"),
    )(h0, h0, h0, h0, h0, h0, w, w, b, b, da, da)


CUM_TILE = 256
ATT_TILE = 512
ATT_TILE_FWD = 1024
NEG = -1e30
N_PAIR = ATTN_HEADS // 2
NT_DIMS = (((1,), (1,)), ((), ()))


def _split3(x):
    a = x.astype(BF16)
    r = x - a.astype(F32)
    b = r.astype(BF16)
    c = (r - b.astype(F32)).astype(BF16)
    return a, b, c


def _tri_dot(tri, x):
    return sum(jnp.dot(tri, p, preferred_element_type=F32) for p in _split3(x))


def _log_sigmoid(x):
    return jnp.minimum(x, 0.0) - jnp.log(1.0 + jnp.exp(-jnp.abs(x)))


def _col(v, idx, lane):
    return jnp.sum(jnp.where(lane == idx, v, 0.0), axis=1, keepdims=True)


def _fox_pre(rest, col, fb, *, name):
    S = rest.shape[0]
    ts = min(CUM_TILE, S)

    def body(f_ref, fb_ref, ccol_ref, carry_ref):
        i = pl.program_id(0)

        @pl.when(i == 0)
        def _():
            carry_ref[...] = jnp.zeros_like(carry_ref)

        lane = lax.broadcasted_iota(jnp.int32, (1, LANE), 1)
        logf = jnp.where(lane < ATTN_HEADS, _log_sigmoid(f_ref[...] + fb_ref[...]), 0.0)
        r = lax.broadcasted_iota(jnp.int32, (ts, ts), 0)
        c = lax.broadcasted_iota(jnp.int32, (ts, ts), 1)
        tri = jnp.where(c <= r, 1.0, 0.0).astype(BF16)
        cs = _tri_dot(tri, logf) + carry_ref[...]
        ccol_ref[...] = cs
        carry_ref[...] = cs[ts - 1:ts, :]

    return pl.pallas_call(
        body, name=name, grid=(S // ts,),
        in_specs=[pl.BlockSpec((ts, LANE), lambda i: (i, col)), pl.BlockSpec((1, LANE), lambda i: (0, 0))],
        out_specs=pl.BlockSpec((ts, LANE), lambda i: (i, 0)),
        out_shape=jax.ShapeDtypeStruct((S, LANE), F32),
        scratch_shapes=[pltpu.VMEM((1, LANE), F32)],
        compiler_params=_cparams("arbitrary"),
    )(rest, fb)


def _fox_post(dcq, dck, rest, col, fb, ddt, *, name):
    S = rest.shape[0]
    ts = min(CUM_TILE, S)
    n = S // ts

    def body(dcq_ref, dck_ref, f_ref, fb_ref, ddt_ref, o_ref, db_ref, carry_ref):
        i = pl.program_id(0)

        @pl.when(i == 0)
        def _():
            carry_ref[...] = jnp.zeros_like(carry_ref)

        lane = lax.broadcasted_iota(jnp.int32, (1, LANE), 1)
        dc = jnp.zeros((ts, LANE), F32)
        for h in range(ATTN_HEADS):
            d = dcq_ref[h // 2] - dck_ref[h // 2]
            dc = jnp.where(lane == h, _col(d, h % 2, lane), dc)
        r = lax.broadcasted_iota(jnp.int32, (ts, ts), 0)
        c = lax.broadcasted_iota(jnp.int32, (ts, ts), 1)
        tri = jnp.where(c >= r, 1.0, 0.0).astype(BF16)
        rc = _tri_dot(tri, dc) + carry_ref[...]
        carry_ref[...] = rc[0:1, :]
        df = rc * _sigmoid(-(f_ref[...] + fb_ref[...]))
        out = jnp.where(lane < ATTN_HEADS, df, ddt_ref[...])
        o_ref[...] = out.astype(BF16)
        _acc_out(db_ref, jnp.sum(out, axis=0, keepdims=True), i == 0)

    rev = lambda i: n - 1 - i
    return pl.pallas_call(
        body, name=name, grid=(n,),
        in_specs=[pl.BlockSpec((N_PAIR, ts, LANE), lambda i: (0, rev(i), 0)),
                  pl.BlockSpec((N_PAIR, ts, LANE), lambda i: (0, rev(i), 0)),
                  pl.BlockSpec((ts, LANE), lambda i: (rev(i), col)),
                  pl.BlockSpec((1, LANE), lambda i: (0, 0)),
                  pl.BlockSpec((ts, LANE), lambda i: (rev(i), 0))],
        out_specs=[pl.BlockSpec((ts, LANE), lambda i: (rev(i), 0)), pl.BlockSpec((1, LANE), lambda i: (0, 0))],
        out_shape=[jax.ShapeDtypeStruct((S, LANE), BF16), jax.ShapeDtypeStruct((1, LANE), F32)],
        scratch_shapes=[pltpu.VMEM((1, LANE), F32)],
        compiler_params=_cparams("arbitrary"),
    )(dcq, dck, rest, fb, ddt)


def _head_masks():
    lane = lax.broadcasted_iota(jnp.int32, (1, LANE), 1)
    return lane, (lane < ATTN_HEAD_DIM, lane >= ATTN_HEAD_DIM)


def _causal_mask(t):
    r = lax.broadcasted_iota(jnp.int32, (t, t), 0)
    c = lax.broadcasted_iota(jnp.int32, (t, t), 1)
    return r, c


AUG_ONE = 0
AUG_C = 3


def _free_lane0(hh):
    return ATTN_HEAD_DIM if hh == 0 else 0


def _augment(x, head_mask, lane, f0, c, key_side):
    terms = _split3(c)
    one = jnp.ones((), BF16)
    ones_at, terms_at = (AUG_ONE, AUG_C) if key_side else (AUG_C, AUG_ONE)
    out = jnp.where(head_mask, x, jnp.zeros((), BF16))
    for k in range(3):
        out = jnp.where(lane == f0 + ones_at + k, one, out)
        out = jnp.where(lane == f0 + terms_at + k, terms[k], out)
    return out


def _carry_start(kind, in_refs, out_refs, sems, first):
    @pl.when(first)
    def _():
        if kind == "gather":
            _gather_phase(0, in_refs, out_refs, sems)
        else:
            _a2a_start(in_refs, out_refs, sems)


def _carry_mid(kind, in_refs, out_refs, sems, mid):
    if kind == "gather":
        @pl.when(mid)
        def _():
            _gather_phase(1, in_refs, out_refs, sems)


def _carry_wait(kind, in_refs, out_refs, sems, last):
    @pl.when(last)
    def _():
        if kind == "gather":
            _gather_phase(2, in_refs, out_refs, sems)
        else:
            _a2a_wait(in_refs, out_refs, sems)


def _attn_fwd(qkv, ccol, *, comm=None, name):
    S = qkv.shape[0]
    t = min(ATT_TILE_FWD, S)
    n = S // t
    scale = ATTN_HEAD_DIM ** -0.5

    nc = len(comm[1]) if comm else 0

    def body(*refs):
        q_ref, k_ref, v_ref, cc_ref = refs[:4]
        o_ref, lrow_ref = refs[4 + nc:6 + nc]
        kaug_ref, vt_ref, m_ref, l_ref, acc_ref = refs[6 + 2 * nc:11 + 2 * nc]
        j, i = pl.program_id(0), pl.program_id(1)
        lane, masks = _head_masks()
        if comm:
            carried = (refs[4:4 + nc], refs[6 + nc:6 + 2 * nc], refs[11 + 2 * nc:])
            _carry_start(comm[0], *carried, jnp.logical_and(j == 0, i == 0))
            _carry_mid(comm[0], *carried, jnp.logical_and(j == N_PAIR - 1, i == 0))

        @pl.when(i == 0)
        def _():
            def fill(b, carry):
                off = pl.multiple_of(b * t, t)
                kblk = k_ref[pl.ds(off, t), :]
                ccb = cc_ref[pl.ds(off, t), :]
                for hh in range(2):
                    ck = _col(ccb, 2 * j + hh, lane)
                    kaug_ref[hh, pl.ds(off, t), :] = _augment(kblk, masks[hh], lane, _free_lane0(hh), -ck, True)
                vt_ref[:, pl.ds(off, t)] = v_ref[pl.ds(off, t), :].astype(F32).T.astype(BF16)
                return carry
            lax.fori_loop(0, n, fill, 0)

        q2 = q_ref[...] * scale
        ccq = cc_ref[pl.ds(pl.multiple_of(i * t, t), t), :]
        qaug = [_augment(q2, masks[hh], lane, _free_lane0(hh), _col(ccq, 2 * j + hh, lane), False) for hh in range(2)]
        m_ref[...] = jnp.full(m_ref.shape, NEG, F32)
        l_ref[...] = jnp.zeros_like(l_ref)
        acc_ref[...] = jnp.zeros_like(acc_ref)

        def step(kb, masked):
            off = pl.multiple_of(kb * t, t)
            vt = vt_ref[:, pl.ds(off, t)]
            for hh in range(2):
                st = lax.dot_general(kaug_ref[hh, pl.ds(off, t), :], qaug[hh], NT_DIMS, preferred_element_type=F32)
                if masked:
                    r, c = _causal_mask(t)
                    st = jnp.where(r <= c, st, NEG)
                m_old = m_ref[hh]
                m_new = jnp.maximum(m_old, jnp.max(st, axis=0, keepdims=True))
                p = jnp.exp(st - m_new)
                alpha = jnp.exp(m_old - m_new)
                l_ref[hh] = alpha * l_ref[hh] + jnp.sum(p, axis=0, keepdims=True)
                acc_ref[hh] = alpha * acc_ref[hh] + jnp.dot(vt, p.astype(BF16), preferred_element_type=F32)
                m_ref[hh] = m_new

        def loop_body(kb, carry):
            step(kb, False)
            return carry

        lax.fori_loop(0, i, loop_body, 0)
        step(i, True)
        drow = lax.broadcasted_iota(jnp.int32, (LANE, 1), 0)
        ot = jnp.where(drow < ATTN_HEAD_DIM, acc_ref[0] / l_ref[0], acc_ref[1] / l_ref[1])
        o_ref[...] = ot.T.astype(BF16)
        for hh in range(2):
            lrow_ref[hh] = m_ref[hh] + jnp.log(l_ref[hh])
        if comm:
            _carry_wait(comm[0], *carried, jnp.logical_and(j == N_PAIR - 1, i == n - 1))

    res = pl.pallas_call(
        body, name=name, grid=(N_PAIR, n),
        in_specs=[pl.BlockSpec((t, LANE), lambda j, i: (i, j)),
                  pl.BlockSpec((S, LANE), lambda j, i: (0, N_PAIR + j)),
                  pl.BlockSpec((S, LANE), lambda j, i: (0, 2 * N_PAIR + j)),
                  pl.BlockSpec((S, LANE), lambda j, i: (0, 0))] + [ANY] * nc,
        out_specs=[pl.BlockSpec((t, LANE), lambda j, i: (i, j)),
                   pl.BlockSpec((2, 1, t), lambda j, i: (j, 0, i))] + [ANY] * nc,
        out_shape=[jax.ShapeDtypeStruct((S, N_PAIR * LANE), BF16), jax.ShapeDtypeStruct((ATTN_HEADS, 1, S), F32)]
        + (_a2a_out_shapes(*comm) if comm else []),
        scratch_shapes=[pltpu.VMEM((2, S, LANE), BF16), pltpu.VMEM((LANE, S), BF16),
                        pltpu.VMEM((2, 1, t), F32), pltpu.VMEM((2, 1, t), F32), pltpu.VMEM((2, LANE, t), F32)]
        + (_a2a_sems(nc) if comm else []),
        compiler_params=_cparams("arbitrary" if comm else "parallel", "arbitrary"),
    )(qkv, qkv, qkv, ccol, *(comm[1] if comm else []))
    return res[0], res[1], list(res[2:])


def _attn_bwd(qkv, o, do, lrow, ccol, *, comm=None, name):
    S = qkv.shape[0]
    t = min(ATT_TILE, S)
    n = S // t
    scale = ATTN_HEAD_DIM ** -0.5

    nc = len(comm[1]) if comm else 0

    def body(*refs):
        q_ref, k_ref, v_ref, o_ref, do_ref, lrow_ref, cc_ref = refs[:7]
        dq_ref, dk_ref, dv_ref, dcq_ref, dck_ref = refs[7 + nc:12 + nc]
        qaug_ref, drow_ref, dqt_ref, dka_ref, dva_ref = refs[12 + 2 * nc:17 + 2 * nc]
        carried = (refs[7:7 + nc], refs[12 + nc:12 + 2 * nc], refs[17 + 2 * nc:])
        j, kb = pl.program_id(0), pl.program_id(1)
        lane, masks = _head_masks()
        if comm:
            _carry_start(comm[0], *carried, jnp.logical_and(j == 0, kb == 0))
        drows = lax.broadcasted_iota(jnp.int32, (LANE, 1), 0)

        @pl.when(kb == 0)
        def _():
            dqt_ref[...] = jnp.zeros_like(dqt_ref)

            def fill(b, carry):
                off = pl.multiple_of(b * t, t)
                q2 = q_ref[pl.ds(off, t), :] * scale
                ccb = cc_ref[pl.ds(off, t), :]
                prod_t = (do_ref[pl.ds(off, t), :].astype(F32) * o_ref[pl.ds(off, t), :].astype(F32)).T
                lo = jnp.sum(jnp.where(drows < ATTN_HEAD_DIM, prod_t, 0.0), axis=0, keepdims=True)
                drow_ref[0, :, pl.ds(off, t)] = lo
                drow_ref[1, :, pl.ds(off, t)] = jnp.sum(prod_t, axis=0, keepdims=True) - lo
                for hh in range(2):
                    cq = _col(ccb, 2 * j + hh, lane)
                    qaug_ref[hh, pl.ds(off, t), :] = _augment(q2, masks[hh], lane, _free_lane0(hh), cq, False)
                return carry
            lax.fori_loop(0, n, fill, 0)

        koff = pl.multiple_of(kb * t, t)
        kblk = k_ref[...]
        v2 = v_ref[...]
        cck = cc_ref[pl.ds(koff, t), :]
        kaug = [_augment(kblk, masks[hh], lane, _free_lane0(hh), -_col(cck, 2 * j + hh, lane), True) for hh in range(2)]
        kaug_t = [ka.astype(F32).T.astype(BF16) for ka in kaug]
        vh = [jnp.where(mk, v2, jnp.zeros((), BF16)) for mk in masks]
        dka_ref[...] = jnp.zeros_like(dka_ref)
        dva_ref[...] = jnp.zeros_like(dva_ref)

        def step(qb, masked):
            off = pl.multiple_of(qb * t, t)
            doblk = do_ref[pl.ds(off, t), :]
            for hh in range(2):
                qa = qaug_ref[hh, pl.ds(off, t), :]
                st = lax.dot_general(kaug[hh], qa, NT_DIMS, preferred_element_type=F32)
                if masked:
                    r, c = _causal_mask(t)
                    st = jnp.where(r <= c, st, NEG)
                pt = jnp.exp(st - lrow_ref[hh, :, pl.ds(off, t)])
                dpt = lax.dot_general(vh[hh], doblk, NT_DIMS, preferred_element_type=F32)
                dst = (pt * (dpt - drow_ref[hh, :, pl.ds(off, t)])).astype(BF16)
                dva_ref[hh] += jnp.dot(pt.astype(BF16), doblk, preferred_element_type=F32)
                dka_ref[hh] += jnp.dot(dst, qa, preferred_element_type=F32)
                dqt_ref[hh, :, pl.ds(off, t)] += jnp.dot(kaug_t[hh], dst, preferred_element_type=F32)

        step(kb, True)

        def loop_body(qb, carry):
            step(qb, False)
            return carry

        lax.fori_loop(kb + 1, n, loop_body, 0)
        dk_ref[...] = jnp.where(masks[0], dka_ref[0], dka_ref[1]).astype(BF16)
        dv_ref[...] = jnp.where(masks[0], dva_ref[0], dva_ref[1]).astype(BF16)
        dck = [_col(dka_ref[hh], _free_lane0(hh) + AUG_C, lane) for hh in range(2)]
        dck_ref[0] = jnp.where(lane == 0, dck[0], jnp.where(lane == 1, dck[1], 0.0))

        @pl.when(kb == n - 1)
        def _():
            def flush(b, carry):
                off = pl.multiple_of(b * t, t)
                d = [dqt_ref[hh, :, pl.ds(off, t)].T for hh in range(2)]
                dq_ref[pl.ds(off, t), :] = (jnp.where(masks[0], d[0], d[1]) * scale).astype(BF16)
                dcq = [_col(d[hh], _free_lane0(hh) + AUG_ONE, lane) for hh in range(2)]
                dcq_ref[0, pl.ds(off, t), :] = jnp.where(lane == 0, dcq[0], jnp.where(lane == 1, dcq[1], 0.0))
                return carry
            lax.fori_loop(0, n, flush, 0)

        if comm:
            _carry_wait(comm[0], *carried, jnp.logical_and(j == N_PAIR - 1, kb == n - 1))

    full = lambda cb: pl.BlockSpec((S, LANE), lambda j, kb: (0, cb(j)))
    kspec = lambda base: pl.BlockSpec((t, LANE), lambda j, kb: (kb, base + j))
    oblk = pl.BlockSpec((t, LANE), lambda j, kb: (kb, j))
    res = pl.pallas_call(
        body, name=name, grid=(N_PAIR, n),
        in_specs=[full(lambda j: j), kspec(N_PAIR), kspec(2 * N_PAIR), full(lambda j: j), full(lambda j: j),
                  pl.BlockSpec((2, 1, S), lambda j, kb: (j, 0, 0)), full(lambda j: 0)] + [ANY] * nc,
        out_specs=[full(lambda j: j), oblk, oblk,
                   pl.BlockSpec((1, S, LANE), lambda j, kb: (j, 0, 0)),
                   pl.BlockSpec((1, t, LANE), lambda j, kb: (j, kb, 0))] + [ANY] * nc,
        out_shape=[jax.ShapeDtypeStruct((S, N_PAIR * LANE), BF16)] * 3 + [jax.ShapeDtypeStruct((N_PAIR, S, LANE), F32)] * 2
        + (_a2a_out_shapes(*comm) if comm else []),
        scratch_shapes=[pltpu.VMEM((2, S, LANE), BF16), pltpu.VMEM((2, 1, S), F32), pltpu.VMEM((2, LANE, S), F32),
                        pltpu.VMEM((2, t, LANE), F32), pltpu.VMEM((2, t, LANE), F32)]
        + (_a2a_sems(nc) if comm else []),
        compiler_params=_cparams("arbitrary" if comm else "parallel", "arbitrary"),
    )(qkv, qkv, qkv, o, do, lrow, ccol, *(comm[1] if comm else []))
    return tuple(res[:5]) + (list(res[5:]),)


DT_LANE0 = ATTN_HEADS
SSD_PAIRS = SSD_HEADS // 2
SSD_X = SSD_HEADS * SSD_HEAD_DIM
SSD_B0 = SSD_X
SSD_C0 = SSD_X + SSD_GROUPS * SSD_STATE
SSD_CH = SSD_X + 2 * SSD_GROUPS * SSD_STATE
TN_DIMS = (((0,), (0,)), ((), ()))


def _softplus(x):
    return jnp.maximum(x, 0.0) + jnp.log(1.0 + jnp.exp(-jnp.abs(x)))


def _ssd_prep(fdt, dtb, alog):
    L = fdt.shape[0]
    lane = lax.broadcasted_iota(jnp.int32, (1, LANE), 1)
    hl = jnp.logical_and(lane >= DT_LANE0, lane < DT_LANE0 + SSD_HEADS)
    dtv = jnp.where(hl, _softplus(fdt + dtb), 0.0)
    A = jnp.where(hl, -jnp.exp(alog), 0.0)
    r = lax.broadcasted_iota(jnp.int32, (L, L), 0)
    c = lax.broadcasted_iota(jnp.int32, (L, L), 1)
    cs = _tri_dot(jnp.where(c <= r, 1.0, 0.0).astype(BF16), dtv * A)
    return lane, hl, dtv, A, cs, r, c


def _halves(lane, v0, v1):
    return jnp.where(lane < SSD_HEAD_DIM, v0, v1)


def _head_select(width):
    r = jnp.arange(LANE)[:, None]
    c = jnp.arange(SSD_HEADS * width)[None, :]
    return (r == DT_LANE0 + c // width).astype(BF16)


def _spread(x, sel, terms):
    return sum(jnp.dot(p, sel, preferred_element_type=F32) for p in _split3(x)[:terms])


def _ssd_fwd(xc, rest, col, dtb, alog, dskip, *, name):
    S = xc.shape[0]
    L = SSD_CHUNK
    nc = S // L

    def body(xc_ref, f_ref, dtb_ref, al_ref, dk_ref, selp_ref, y_ref, hs_ref, h_ref, cst_ref):
        i = pl.program_id(0)

        @pl.when(i == 0)
        def _():
            h_ref[...] = jnp.zeros_like(h_ref)

        lane, hl, dtv, A, cs, r, c = _ssd_prep(f_ref[...], dtb_ref[...], al_ref[...])
        cst_ref[...] = cs.T
        cs_last = cs[L - 1:L, :]
        cd = jnp.exp(cs_last)
        dkv = dk_ref[...]
        selp = selp_ref[...]
        dt_p, ecs_p, dec_p = (_spread(v, selp, 2) for v in (dtv, jnp.exp(cs), jnp.exp(cs_last - cs)))
        prow = lax.broadcasted_iota(jnp.int32, (LANE, 1), 0)
        for g in range(SSD_GROUPS):
            Bg = xc_ref[:, SSD_B0 + g * SSD_STATE:SSD_B0 + (g + 1) * SSD_STATE].astype(BF16)
            Cg = xc_ref[:, SSD_C0 + g * SSD_STATE:SSD_C0 + (g + 1) * SSD_STATE].astype(BF16)
            CB = lax.dot_general(Cg, Bg, NT_DIMS, preferred_element_type=F32)
            for pp in range(SSD_PAIRS // SSD_GROUPS):
                pi = g * (SSD_PAIRS // SSD_GROUPS) + pp
                hl0 = DT_LANE0 + 2 * pi
                pair = slice(pi * LANE, (pi + 1) * LANE)
                x2 = xc_ref[:, pair]
                xd = x2 * dt_p[:, pair]
                xdb = xd.astype(BF16)
                yd = jnp.zeros((L, LANE), F32)
                for hh in range(2):
                    seg = _col(cs, hl0 + hh, lane) - cst_ref[hl0 + hh:hl0 + hh + 1, :]
                    M = CB * jnp.exp(jnp.where(c <= r, seg, NEG))
                    yh = jnp.dot(M.astype(BF16), xdb, preferred_element_type=F32)
                    yd = jnp.where((lane >= SSD_HEAD_DIM) if hh else (lane < SSD_HEAD_DIM), yh, yd)
                hp = h_ref[pi]
                hs_ref[0, pi] = hp
                yo = lax.dot_general(Cg, hp.astype(BF16), NT_DIMS, preferred_element_type=F32) * ecs_p[:, pair]
                dsk = _halves(lane, _col(dkv, hl0, lane), _col(dkv, hl0 + 1, lane))
                y_ref[:, pair] = yd + yo + dsk * x2
                xw = (xd * dec_p[:, pair]).astype(BF16)
                st = lax.dot_general(xw, Bg, TN_DIMS, preferred_element_type=F32)
                cdp = jnp.where(prow < SSD_HEAD_DIM, _col(cd, hl0, lane), _col(cd, hl0 + 1, lane))
                h_ref[pi] = cdp * hp + st

    vec = pl.BlockSpec((1, LANE), lambda i: (0, 0))
    selp = _head_select(SSD_HEAD_DIM)
    return pl.pallas_call(
        body, name=name, grid=(nc,),
        in_specs=[pl.BlockSpec((L, SSD_CH), lambda i: (i, 0)), pl.BlockSpec((L, LANE), lambda i: (i, col)),
                  vec, vec, vec, pl.BlockSpec(selp.shape, lambda i: (0, 0))],
        out_specs=[pl.BlockSpec((L, SSD_X), lambda i: (i, 0)),
                   pl.BlockSpec((1, SSD_PAIRS, LANE, SSD_STATE), lambda i: (i, 0, 0, 0))],
        out_shape=[jax.ShapeDtypeStruct((S, SSD_X), F32),
                   jax.ShapeDtypeStruct((nc, SSD_PAIRS, LANE, SSD_STATE), F32)],
        scratch_shapes=[pltpu.VMEM((SSD_PAIRS, LANE, SSD_STATE), F32), pltpu.VMEM((LANE, L), F32)],
        compiler_params=_cparams("arbitrary"),
    )(xc, rest, dtb, alog, dskip, selp)


def _pair_sums(lane, v):
    lo = jnp.sum(jnp.where(lane < SSD_HEAD_DIM, v, 0.0), axis=1, keepdims=True)
    return lo, jnp.sum(v, axis=1, keepdims=True) - lo


def _ssd_bwd(xc, rest, col, dtb, alog, dskip, hs, dy, *, name):
    S = xc.shape[0]
    L = SSD_CHUNK
    nc = S // L
    PG = SSD_PAIRS // SSD_GROUPS

    def body(xc_ref, f_ref, dtb_ref, al_ref, dk_ref, hs_ref, dy_ref, selp_ref,
             dxc_ref, ddt_ref, dp_ref, dh_ref, cst_ref):
        i = pl.program_id(0)

        @pl.when(i == 0)
        def _():
            dh_ref[...] = jnp.zeros_like(dh_ref)
            dp_ref[...] = jnp.zeros_like(dp_ref)

        fv = f_ref[...] + dtb_ref[...]
        lane, hl, dtv, A, cs, r, c = _ssd_prep(f_ref[...], dtb_ref[...], al_ref[...])
        cst_ref[...] = cs.T
        cs_last = cs[L - 1:L, :]
        cd = jnp.exp(cs_last)
        dkv = dk_ref[...]
        selp = selp_ref[...]
        dt_p, ecs_p, dec_p = (_spread(v, selp, 2) for v in (dtv, jnp.exp(cs), jnp.exp(cs_last - cs)))
        prow = lax.broadcasted_iota(jnp.int32, (LANE, 1), 0)
        lrow = lax.broadcasted_iota(jnp.int32, (L, 1), 0)
        is_last = lrow == L - 1
        causal = c <= r
        dcs = jnp.zeros((L, LANE), F32)
        ddt = jnp.zeros((L, LANE), F32)
        dD = jnp.zeros((1, LANE), F32)
        for g in range(SSD_GROUPS):
            Bg = xc_ref[:, SSD_B0 + g * SSD_STATE:SSD_B0 + (g + 1) * SSD_STATE].astype(BF16)
            Cg = xc_ref[:, SSD_C0 + g * SSD_STATE:SSD_C0 + (g + 1) * SSD_STATE].astype(BF16)
            CB = lax.dot_general(Cg, Bg, NT_DIMS, preferred_element_type=F32)
            dCB = jnp.zeros((L, L), F32)
            dB = jnp.zeros((L, SSD_STATE), F32)
            dC = jnp.zeros((L, SSD_STATE), F32)
            for pp in range(PG):
                pi = g * PG + pp
                hl0 = DT_LANE0 + 2 * pi
                pair = slice(pi * LANE, (pi + 1) * LANE)
                x2 = xc_ref[:, pair]
                dy2 = dy_ref[:, pair]
                dtp, ecsp, decp = dt_p[:, pair], ecs_p[:, pair], dec_p[:, pair]
                xd = x2 * dtp
                xdb = xd.astype(BF16)
                dsk = _halves(lane, _col(dkv, hl0, lane), _col(dkv, hl0 + 1, lane))
                dx2 = dsk * dy2
                sD = _pair_sums(lane, dy2 * x2)
                hp = hs_ref[0, pi]
                hpb = hp.astype(BF16)
                yo = lax.dot_general(Cg, hpb, NT_DIMS, preferred_element_type=F32) * ecsp
                dW = (ecsp * dy2).astype(BF16)
                dC = dC + jnp.dot(dW, hpb, preferred_element_type=F32)
                dhp = lax.dot_general(dW, Cg, TN_DIMS, preferred_element_type=F32)
                sYo = _pair_sums(lane, dy2 * yo)
                dhn = dh_ref[pi]
                cdh = (_col(cd, hl0, lane), _col(cd, hl0 + 1, lane))
                dhp = dhp + jnp.where(prow < SSD_HEAD_DIM, cdh[0], cdh[1]) * dhn
                rs = jnp.sum(dhn * hp, axis=1, keepdims=True)
                lo = jnp.sum(jnp.where(prow < SSD_HEAD_DIM, rs, 0.0), axis=0, keepdims=True)
                dcd = (lo, jnp.sum(rs, axis=0, keepdims=True) - lo)
                dhnb = dhn.astype(BF16)
                G = lax.dot_general(Bg, dhnb, NT_DIMS, preferred_element_type=F32)
                dxd = decp * G
                sdd = _pair_sums(lane, xd * dxd)
                dB = dB + jnp.dot((xd * decp).astype(BF16), dhnb, preferred_element_type=F32)
                dh_ref[pi] = dhp
                for hh in range(2):
                    hmask = (lane >= SSD_HEAD_DIM) if hh else (lane < SSD_HEAD_DIM)
                    seg = _col(cs, hl0 + hh, lane) - cst_ref[hl0 + hh:hl0 + hh + 1, :]
                    Lm = jnp.exp(jnp.where(causal, seg, NEG))
                    M = CB * Lm
                    dyh = jnp.where(hmask, dy2, 0.0).astype(BF16)
                    dM = lax.dot_general(dyh, xdb, NT_DIMS, preferred_element_type=F32)
                    dxd = dxd + lax.dot_general(M.astype(BF16), dyh, TN_DIMS, preferred_element_type=F32)
                    Q = dM * M
                    dCB = dCB + dM * Lm
                    dd = sdd[hh]
                    end = dcd[hh] * cdh[hh] + jnp.sum(dd, axis=0, keepdims=True)
                    dcs_h = (sYo[hh] - dd + jnp.sum(Q, axis=1, keepdims=True)
                             - jnp.sum(Q.T, axis=1, keepdims=True) + jnp.where(is_last, end, 0.0))
                    dcs = jnp.where(lane == hl0 + hh, dcs_h, dcs)
                    dD = jnp.where(lane == hl0 + hh, jnp.sum(sD[hh], axis=0, keepdims=True), dD)
                sdt = _pair_sums(lane, dxd * x2)
                ddt = jnp.where(lane == hl0, sdt[0], jnp.where(lane == hl0 + 1, sdt[1], ddt))
                dxc_ref[:, pair] = dx2 + dxd * dtp
            dCBb = dCB.astype(BF16)
            dC = dC + jnp.dot(dCBb, Bg, preferred_element_type=F32)
            dB = dB + lax.dot_general(dCBb, Cg, TN_DIMS, preferred_element_type=F32)
            dxc_ref[:, SSD_B0 + g * SSD_STATE:SSD_B0 + (g + 1) * SSD_STATE] = dB
            dxc_ref[:, SSD_C0 + g * SSD_STATE:SSD_C0 + (g + 1) * SSD_STATE] = dC
        da = _tri_dot(jnp.where(c >= r, 1.0, 0.0).astype(BF16), dcs)
        ddtv = ddt + da * A
        ddt_ref[...] = jnp.where(hl, ddtv * _sigmoid(fv), 0.0)
        dal = jnp.sum(da * dtv, axis=0, keepdims=True) * A
        dp_ref[0:1, :] += dal
        dp_ref[1:2, :] += dD

    rev = lambda i: nc - 1 - i
    vec = pl.BlockSpec((1, LANE), lambda i: (0, 0))
    selp = _head_select(SSD_HEAD_DIM)
    return pl.pallas_call(
        body, name=name, grid=(nc,),
        in_specs=[pl.BlockSpec((L, SSD_CH), lambda i: (rev(i), 0)), pl.BlockSpec((L, LANE), lambda i: (rev(i), col)),
                  vec, vec, vec,
                  pl.BlockSpec((1, SSD_PAIRS, LANE, SSD_STATE), lambda i: (rev(i), 0, 0, 0)),
                  pl.BlockSpec((L, SSD_X), lambda i: (rev(i), 0)), pl.BlockSpec(selp.shape, lambda i: (0, 0))],
        out_specs=[pl.BlockSpec((L, SSD_CH), lambda i: (rev(i), 0)), pl.BlockSpec((L, LANE), lambda i: (rev(i), 0)),
                   pl.BlockSpec((SUBLANE, LANE), lambda i: (0, 0))],
        out_shape=[jax.ShapeDtypeStruct((S, SSD_CH), F32), jax.ShapeDtypeStruct((S, LANE), F32),
                   jax.ShapeDtypeStruct((SUBLANE, LANE), F32)],
        scratch_shapes=[pltpu.VMEM((SSD_PAIRS, LANE, SSD_STATE), F32), pltpu.VMEM((LANE, L), F32)],
        compiler_params=_cparams("arbitrary"),
    )(xc, rest, dtb, alog, dskip, hs, dy, selp)


def _gate_norm_fwd(y, rest, zcol, nw, *, name):
    S, W = y.shape
    ts = _row_tile(S)
    GW = W // SSD_GROUPS

    def body(y_ref, z_ref, nw_ref, o_ref):
        z = z_ref[...]
        t = y_ref[...] * (z * _sigmoid(z))
        for g in range(SSD_GROUPS):
            cols = slice(g * GW, (g + 1) * GW)
            tg = t[:, cols]
            rr = lax.rsqrt(jnp.mean(tg * tg, axis=-1, keepdims=True) + NORM_EPS)
            o_ref[:, cols] = (tg * rr * nw_ref[:, cols]).astype(BF16)

    row = pl.BlockSpec((ts, W), lambda i: (i, 0))
    return pl.pallas_call(
        body, name=name, grid=(S // ts,),
        in_specs=[row, pl.BlockSpec((ts, W), lambda i: (i, zcol)), pl.BlockSpec((1, W), lambda i: (0, 0))],
        out_specs=row,
        out_shape=jax.ShapeDtypeStruct((S, W), BF16),
        compiler_params=_cparams("parallel"),
    )(y, rest, nw)


def _gate_norm_bwd(y, rest, zcol, nw, do, *, name):
    S, W = y.shape
    ts = _row_tile(S)
    GW = W // SSD_GROUPS

    def body(y_ref, z_ref, nw_ref, do_ref, dy_ref, dz_ref, dnw_ref):
        z = z_ref[...]
        yv = y_ref[...]
        s, ds = _dsilu(z)
        sz = z * s
        t = yv * sz
        dov = do_ref[...]
        parts = []
        for g in range(SSD_GROUPS):
            cols = slice(g * GW, (g + 1) * GW)
            tg = t[:, cols]
            rr = lax.rsqrt(jnp.mean(tg * tg, axis=-1, keepdims=True) + NORM_EPS)
            n = tg * rr
            dog = dov[:, cols]
            dn = dog * nw_ref[:, cols]
            dt = rr * (dn - n * jnp.mean(dn * n, axis=-1, keepdims=True))
            dy_ref[:, cols] = dt * sz[:, cols]
            dz_ref[:, cols] = (dt * yv[:, cols] * ds[:, cols]).astype(BF16)
            parts.append(jnp.sum(dog * n, axis=0, keepdims=True))
        _acc_out(dnw_ref, jnp.concatenate(parts, axis=1), pl.program_id(0) == 0)

    row = pl.BlockSpec((ts, W), lambda i: (i, 0))
    vec = pl.BlockSpec((1, W), lambda i: (0, 0))
    return pl.pallas_call(
        body, name=name, grid=(S // ts,),
        in_specs=[row, pl.BlockSpec((ts, W), lambda i: (i, zcol)), vec, row],
        out_specs=[row, row, vec],
        out_shape=[jax.ShapeDtypeStruct((S, W), F32), jax.ShapeDtypeStruct((S, W), BF16),
                   jax.ShapeDtypeStruct((1, W), F32)],
        compiler_params=_cparams("arbitrary"),
    )(y, rest, nw, do)


N_DEV = 8
MESH = pl.DeviceIdType.MESH
ANY = pl.BlockSpec(memory_space=pl.ANY)


def _all_gather(xs, *, name):
    n = len(xs)

    def body(*refs):
        for phase in range(3):
            _gather_phase(phase, refs[:n], refs[n:2 * n], refs[2 * n:])

    return pl.pallas_call(
        body, name=name,
        out_shape=_a2a_out_shapes("gather", xs), in_specs=[ANY] * n, out_specs=[ANY] * n,
        scratch_shapes=_a2a_sems(n),
    )(*xs)


def _gather_phase(phase, x_refs, o_refs, sems):
    send_sems, recv_sems, local_sems = sems
    n = len(x_refs)
    px, py, pc = lax.axis_index("x"), lax.axis_index("y"), lax.axis_index("c")
    me, sibling = (px, py, pc), (px, py, 1 - pc)
    chips = [(1 - px, py), (px, 1 - py), (1 - px, 1 - py)]

    def copy(a, k, block, to, src=None):
        slot = o_refs[a].at[4 * block[0] + 2 * block[1] + block[2]]
        return pltpu.make_async_remote_copy(
            src_ref=slot if src is None else src, dst_ref=slot,
            send_sem=send_sems.at[a, k], recv_sem=recv_sems.at[a, k], device_id=to, device_id_type=MESH)

    def own(a):
        return ([copy(a, 0, me, sibling, src=x_refs[a])]
                + [copy(a, 1 + j, me, (*chip, pc), src=x_refs[a]) for j, chip in enumerate(chips)])

    mine = lambda a: pltpu.make_async_copy(x_refs[a], o_refs[a].at[4 * px + 2 * py + pc], local_sems.at[a])
    passed = lambda a, j: copy(a, 4 + j, (*chips[j], pc), sibling)
    if phase == 0:
        for a in range(n):
            mine(a).start()
            for cp in own(a):
                cp.start()
    elif phase == 1:
        for j, chip in enumerate(chips):
            for a in range(n):
                copy(a, 1 + j, (*chip, pc), me).wait_recv()
                passed(a, j).start()
    else:
        for a in range(n):
            copy(a, 0, sibling, me).wait_recv()
            for j, chip in enumerate(chips):
                copy(a, 4 + j, (*chip, 1 - pc), me).wait_recv()
            for cp in own(a) + [passed(a, j) for j in range(len(chips))]:
                cp.wait_send()
            mine(a).wait()


def _exchange(sends, *, name):
    n = len(sends)

    def body(*refs):
        _a2a_start(refs[:n], refs[n:2 * n], refs[2 * n:])
        _a2a_wait(refs[:n], refs[n:2 * n], refs[2 * n:])

    return pl.pallas_call(
        body, name=name,
        out_shape=_a2a_out_shapes("exchange", sends), in_specs=[ANY] * n, out_specs=[ANY] * n,
        scratch_shapes=_a2a_sems(n),
    )(*sends)


def _a2a_out_shapes(kind, arrays):
    if kind == "gather":
        return [jax.ShapeDtypeStruct((N_DEV,) + x.shape, x.dtype) for x in arrays]
    return [jax.ShapeDtypeStruct(x.shape, x.dtype) for x in arrays]


def _a2a_sems(n):
    return [pltpu.SemaphoreType.DMA((n, N_DEV - 1)), pltpu.SemaphoreType.DMA((n, N_DEV - 1)),
            pltpu.SemaphoreType.DMA((n,))]


def _a2a_copies(in_refs, out_refs, send_sems, recv_sems, local_sems, arrivals):
    px, py, pc = lax.axis_index("x"), lax.axis_index("y"), lax.axis_index("c")
    me = 4 * px + 2 * py + pc
    n = len(in_refs)
    src = lambda a, p: in_refs[a].at[p]
    local = [pltpu.make_async_copy(src(a, me), out_refs[a].at[me], local_sems.at[a]) for a in range(n)]
    sent, landing = [], []
    for k in range(1, N_DEV):
        qx = 1 - px if k & 4 else px
        qy = 1 - py if k & 2 else py
        qc = 1 - pc if k & 1 else pc
        peer = 4 * qx + 2 * qy + qc
        for a in range(n):
            def rdma(dst_slot):
                return pltpu.make_async_remote_copy(
                    src_ref=src(a, peer), dst_ref=out_refs[a].at[dst_slot],
                    send_sem=send_sems.at[a, k - 1], recv_sem=recv_sems.at[a, k - 1],
                    device_id=(qx, qy, qc), device_id_type=MESH)
            sent.append(rdma(me))
            if arrivals:
                landing.append(rdma(peer))
    return local, sent, landing


def _a2a_start(in_refs, out_refs, sems):
    local, sent, _ = _a2a_copies(in_refs, out_refs, *sems, arrivals=False)
    for cp in local + sent:
        cp.start()


def _a2a_wait(in_refs, out_refs, sems):
    local, sent, landing = _a2a_copies(in_refs, out_refs, *sems, arrivals=True)
    for cp in landing:
        cp.wait_recv()
    for cp in sent:
        cp.wait_send()
    for cp in local:
        cp.wait()


IN_POOL, IN_QKV, IN_F, IN_Z, IN_XBC, IN_DT, IN_GATE, IN_TOTAL = 0, 512, 2048, 2056, 3080, 4616, 4632, 7704
QKV_W = IN_F - IN_QKV
REST_Z_BLK, REST_POOL_BLK, REST_XBC0, REST_FDT_BLK = 3, 8, 4608, 48
REST_FDT0 = REST_FDT_BLK * LANE
REST_USED = REST_FDT0 + (IN_Z - IN_F) + (IN_GATE - IN_DT)
REST_W = REST_FDT0 + LANE


def _w_in_runs(shard):
    segs = [(0, IN_QKV, IN_F, 0), (1, IN_GATE, IN_TOTAL, 0), (1, IN_Z, IN_XBC, 3072), (1, IN_POOL, IN_QKV, 4096),
            (1, IN_XBC, IN_DT, REST_XBC0), (1, IN_F, IN_Z, REST_FDT0), (1, IN_DT, IN_GATE, REST_FDT0 + IN_Z - IN_F)]
    runs = []
    for dst, t0, t1, d0 in segs:
        for j in range(N_DEV):
            lo, hi = max(t0, shard * j), min(t1, shard * (j + 1))
            if lo < hi:
                runs.append((dst, d0 + lo - t0, j, lo - shard * j, hi - lo))
    return runs


def _repack_w_in(g, shard, *, name):
    _, L, K, P = g.shape
    tr = 256
    runs = _w_in_runs(shard)

    def body(g_ref, q_ref, r_ref):
        r_ref[:, REST_USED:] = jnp.zeros((tr, REST_W - REST_USED), g_ref.dtype)
        for dst, b, j, a, n in runs:
            (q_ref, r_ref)[dst][:, b:b + n] = g_ref[j, :, a:a + n]

    return pl.pallas_call(
        body, name=name, grid=(L, K // tr),
        in_specs=[pl.BlockSpec((N_DEV, None, tr, P), lambda l, i: (0, l, i, 0))],
        out_specs=[pl.BlockSpec((None, tr, QKV_W), lambda l, i: (l, i, 0)),
                   pl.BlockSpec((None, tr, REST_W), lambda l, i: (l, i, 0))],
        out_shape=[jax.ShapeDtypeStruct((L, K, QKV_W), g.dtype), jax.ShapeDtypeStruct((L, K, REST_W), g.dtype)],
        compiler_params=_cparams("parallel", "parallel"),
    )(g)


def _held_index(l, li, i, n):
    return jnp.where(l == li, i, jnp.where(l > li, n - 1, 0))


def _repack_dw_in(dwq, dwr, shard, *, name):
    L = len(dwq)
    K = dwq[0].shape[0]
    tr = 128
    nt = K // tr
    runs = _w_in_runs(shard)

    def body(*refs):
        srcs, o_ref = refs[:2 * L], refs[2 * L]
        l = pl.program_id(0)
        for li in range(L):
            @pl.when(l == li)
            def _():
                for dst, b, j, a, n in runs:
                    o_ref[j, :, a:a + n] = srcs[2 * li + dst][:, b:b + n]

    in_specs = []
    for li in range(L):
        hold = functools.partial(lambda li, l, i: (_held_index(l, li, i, nt), 0), li)
        in_specs += [pl.BlockSpec((tr, QKV_W), hold), pl.BlockSpec((tr, REST_W), hold)]
    args = [x for pair in zip(dwq, dwr) for x in pair]
    return pl.pallas_call(
        body, name=name, grid=(L, nt),
        in_specs=in_specs,
        out_specs=pl.BlockSpec((N_DEV, None, tr, shard), lambda l, i: (0, l, i, 0)),
        out_shape=jax.ShapeDtypeStruct((N_DEV, L, K, shard), dwq[0].dtype),
        compiler_params=_cparams("arbitrary", "arbitrary"),
    )(*args)


FLAT_W = 1024
ADAM_ROWS = 128


def _adam_math(w, g, m, v):
    m2 = ADAM_B1 * m + (1.0 - ADAM_B1) * g
    v2 = ADAM_B2 * v + (1.0 - ADAM_B2) * (g * g)
    m_hat = m2 / (1.0 - ADAM_B1 ** ADAM_STEP)
    v_hat = v2 / (1.0 - ADAM_B2 ** ADAM_STEP)
    delta = -ADAM_LR * (m_hat / (jnp.sqrt(v_hat) + ADAM_EPS) + ADAM_WD * w)
    return delta, m2, v2


def _row_div(R, align):
    best = R
    for t in range(align, min(R, ADAM_ROWS) + 1, align):
        if R % t == 0:
            best = t
    return best


def _sum8(ref_of):
    g = ref_of(0).astype(F32)
    for s in range(1, N_DEV):
        g = g + ref_of(s).astype(F32)
    return g


def _sum_adam(recvs, w, m, v, *, name):
    L, R, C = w.shape
    single = not isinstance(recvs, (list, tuple))
    recvs = [recvs] if single else list(recvs)
    tr = _row_div(R, SUBLANE * (4 // recvs[0].dtype.itemsize))
    nt = R // tr
    nr = len(recvs)

    def body(*refs):
        r_refs = refs[:nr]
        w_ref, m_ref, v_ref, g_ref, d_ref, m2_ref, v2_ref = refs[nr:]

        def run(ref_of):
            g = _sum8(ref_of)
            delta, m2, v2 = _adam_math(w_ref[...], g, m_ref[...], v_ref[...])
            g_ref[...] = g
            d_ref[...] = delta
            m2_ref[...] = m2
            v2_ref[...] = v2

        if single:
            run(lambda s: r_refs[0][s])
        else:
            l = pl.program_id(0)
            for li in range(L):
                @pl.when(l == li)
                def _():
                    run(lambda s: r_refs[li][s])

    if single:
        r_specs = [pl.BlockSpec((N_DEV, None, tr, C), lambda l, i: (0, l, i, 0))]
    else:
        r_specs = [pl.BlockSpec((N_DEV, tr, C), functools.partial(lambda li, l, i: (0, _held_index(l, li, i, nt), 0), li))
                   for li in range(L)]
    row = pl.BlockSpec((None, tr, C), lambda l, i: (l, i, 0))
    return pl.pallas_call(
        body, name=name, grid=(L, nt),
        in_specs=r_specs + [row, row, row],
        out_specs=[row] * 4,
        out_shape=[jax.ShapeDtypeStruct((L, R, C), F32)] * 4,
        compiler_params=_cparams("arbitrary", "arbitrary"),
    )(*recvs, w, m, v)


def _sum_parts(parts, *, name):
    _, R, C = parts.shape

    def body(p_ref, o_ref):
        o_ref[...] = _sum8(lambda s: p_ref[s])

    return pl.pallas_call(
        body, name=name, out_shape=jax.ShapeDtypeStruct((R, C), F32),
        in_specs=[pl.BlockSpec(memory_space=pltpu.VMEM)], out_specs=pl.BlockSpec(memory_space=pltpu.VMEM),
    )(parts)


def _adam(g, w, m, v, *, name):
    def body(g_ref, w_ref, m_ref, v_ref, d_ref, m2_ref, v2_ref):
        delta, m2, v2 = _adam_math(w_ref[...], g_ref[...], m_ref[...], v_ref[...])
        d_ref[...] = delta
        m2_ref[...] = m2
        v2_ref[...] = v2

    whole = pl.BlockSpec(memory_space=pltpu.VMEM)
    return pl.pallas_call(
        body, name=name, in_specs=[whole] * 4, out_specs=[whole] * 3,
        out_shape=[jax.ShapeDtypeStruct(g.shape, F32)] * 3,
    )(g, w, m, v)


SHARDED = ("w_in", "ffn_up", "p_pool", "p_attn", "p_ssd", "w_out", "ffn_down", "ssd_conv_w", "ffn_conv_w")
REPLICATED = ("norm_mix", "pool_mix", "pool_scale", "f_bias", "ssd_conv_b", "ssd_dt_bias", "ssd_a_log", "ssd_d",
              "ssd_norm", "norm_ffn", "ffn_conv_b", "norm_final")
WEIGHTS = ("norm_mix", "w_in", "pool_mix", "pool_scale", "f_bias", "ssd_conv_w", "ssd_conv_b", "ssd_dt_bias",
           "ssd_a_log", "ssd_d", "ssd_norm", "p_pool", "p_attn", "p_ssd", "w_out", "norm_ffn", "ffn_up",
           "ffn_conv_w", "ffn_conv_b", "ffn_down", "norm_final")


def _round_up(n, k):
    return -(-n // k) * k


def _pad_last(x, width):
    return jnp.pad(x, [(0, 0)] * (x.ndim - 1) + [(0, width - x.shape[-1])])


def _flat_rows(parts, rows):
    flat = jnp.concatenate([p.reshape(-1) for p in parts])
    return jnp.pad(flat, (0, rows * FLAT_W - flat.shape[0])).reshape(rows, FLAT_W)


def _lanes(v, lane0):
    return jnp.pad(v, (lane0, LANE - lane0 - v.shape[0]))[None]


def _cols_by_device(g):
    _, L, R, c = g.shape
    return jnp.moveaxis(g, 0, 2).reshape(L, R, N_DEV * c)


def _rows_by_device(g):
    _, L, r, C = g.shape
    return jnp.moveaxis(g, 0, 1).reshape(L, N_DEV * r, C)


def _gather_payload(W, l, shard_pad, up_pad):
    bf = lambda n: W[n][l].astype(BF16)
    return [_pad_last(bf("w_in"), shard_pad), _pad_last(bf("ffn_up"), up_pad), bf("p_pool"), bf("p_attn"),
            bf("p_ssd"), bf("w_out"), bf("ffn_down"), W["ssd_conv_w"][l], W["ffn_conv_w"][l]]


def _proj_params(g_win, W, l, shard):
    w_qkv, w_rest = _repack_w_in(g_win[:, None], shard, name=f"l{l}_repack_w_in")
    return dict(nm=W["norm_mix"][l][None], w_qkv=w_qkv[0], w_rest=w_rest[0], fb=_lanes(W["f_bias"][l], 0), shard=shard)


def _mixer_params(gathered, W, l, up_w, up_pad):
    g_up, g_pp, g_pa, g_ps, g_wo, g_dn, g_cw, g_fcw = [g[:, None] for g in gathered]
    half = N_DEV // 2
    w_down = _rows_by_device(g_dn)[0]
    w_down_p = jnp.pad(w_down.reshape(half, up_w, -1), ((0, 0), (0, up_pad - up_w), (0, 0))).reshape(half * up_pad, -1)
    fcb_p = _pad_last(W["ffn_conv_b"][l].reshape(N_DEV, up_w), up_pad).reshape(1, N_DEV * up_pad)
    return dict(
        mix=W["pool_mix"][l].astype(BF16), pscale=W["pool_scale"][l][None], cw=_cols_by_device(g_cw)[0],
        cb=W["ssd_conv_b"][l][None], dtb=_lanes(W["ssd_dt_bias"][l], DT_LANE0),
        alog=_lanes(W["ssd_a_log"][l], DT_LANE0), dsk=_lanes(W["ssd_d"][l], DT_LANE0), snw=W["ssd_norm"][l][None],
        p_pool=_cols_by_device(g_pp)[0], p_attn=_cols_by_device(g_pa)[0], p_ssd=_rows_by_device(g_ps)[0],
        w_out=_rows_by_device(g_wo)[0], nf=W["norm_ffn"][l][None], g_up=g_up, up_blocks=(N_DEV, up_w, up_pad),
        fcw=_cols_by_device(_pad_last(g_fcw, up_pad))[0], fcb=fcb_p, w_down=w_down_p)


def _layer_fwd(x, W, l, g_win, g_mixer, to_gather, dims):
    nm = lambda s: f"l{l}_{s}"
    P = _proj_params(g_win, W, l, dims[0])
    u = _rms_fwd(x, P["nm"], name=nm("rms_mix"))
    qkv = _mm(u, P["w_qkv"], out_dtype=BF16, name=nm("proj_qkv"))
    rest = _mm(u, P["w_rest"], name=nm("proj_rest"))
    ccol = _fox_pre(rest, REST_FDT_BLK, P["fb"], name=nm("fox_pre"))
    o, lrow, got = _attn_fwd(qkv, ccol, comm=("gather", to_gather) if to_gather else None, name=nm("attn"))
    if g_mixer is None:
        g_mixer, got = got[:len(SHARDED) - 1], got[len(SHARDED) - 1:]
    P.update(_mixer_params(g_mixer, W, l, *dims[1:]))
    d, ya0 = _pool_fwd(rest, REST_POOL_BLK, P["mix"], P["pscale"], name=nm("pool"))
    ya = _mm(ya0, P["p_pool"], name=nm("p_pool"))
    yb = _mm(o, P["p_attn"], name=nm("p_attn"))
    xc = _conv_silu_fwd(rest, REST_XBC0, SSD_CH, P["cw"], P["cb"], name=nm("ssd_conv"))
    y, hs = _ssd_fwd(xc, rest, REST_FDT_BLK, P["dtb"], P["alog"], P["dsk"], name=nm("ssd"))
    yc0 = _gate_norm_fwd(y, rest, REST_Z_BLK, P["snw"], name=nm("ssd_norm"))
    yc = _mm(yc0, P["p_ssd"], name=nm("p_ssd"))
    merged = _merge_fwd(rest, ya, yb, yc, name=nm("merge"))
    x1 = _mm(merged, P["w_out"], add=x, name=nm("w_out"))
    u2 = _rms_fwd(x1, P["nf"], name=nm("rms_ffn"))
    h0 = _mm_up_fwd(u2, P["g_up"], 0, name=nm("ffn_up"))
    a = _ffn_act_fwd(h0, P["fcw"], P["fcb"], name=nm("ffn_act"))
    x2 = _mm(a, P["w_down"], add=x1, name=nm("ffn_down"))
    saved = dict(x=x, u=u, qkv=qkv, rest=rest, d=d, ya0=ya0, ya=ya, ccol=ccol, o=o, lrow=lrow, yb=yb,
                 xc=xc, y=y, hs=hs, yc0=yc0, yc=yc, merged=merged, x1=x1, u2=u2, h0=h0, a=a)
    return x2, saved, P, got


EARLY_GRADS = ("ffn_up", "ffn_down", "ffn_conv_w", "p_pool", "p_attn", "p_ssd", "w_out")
LATE_GRADS = ("w_in", "ssd_conv_w")


def _layer_bwd(g, gb, sv, P, l, pending, last):
    nm = lambda s: f"l{l}_{s}_bwd"
    rest = sv["rest"]
    G = {}
    up_blocks, up_w, up_pad = P["up_blocks"]
    half = up_blocks // 2
    da = _mm(gb, P["w_down"], tb=True, name=nm("ffn_down_dx"))
    dwd = _mm(sv["a"], gb, ta=True, out_dtype=GRAD_WIRE, name=nm("ffn_down_dw"))
    G["ffn_down"] = dwd.reshape(half, up_pad, -1)[:, :up_w].reshape(N_DEV, -1, dwd.shape[1])
    dh, dfcw, dfcb = _ffn_act_bwd(sv["h0"], P["fcw"], P["fcb"], da, name=nm("ffn_act"))
    du2 = _mm_up_dx(dh, P["g_up"], 0, name=nm("ffn_up_dx"))
    G["ffn_up"] = _mm_up_dw(sv["u2"], dh, up_blocks, up_w, name=nm("ffn_up_dw"))
    taps = dfcw.shape[1]
    G["ffn_conv_w"] = jnp.moveaxis(dfcw.reshape(2, taps, half, up_pad)[..., :up_w], 2, 1).reshape(N_DEV, taps, up_w)
    G["ffn_conv_b"] = dfcb.reshape(2, half, up_pad)[..., :up_w].reshape(-1)
    dx1, dx1b, dnf = _rms_bwd(sv["x1"], P["nf"], du2, g, name=nm("rms_ffn"))
    G["norm_ffn"] = dnf[0]
    dmerged = _mm(dx1b, P["w_out"], tb=True, name=nm("w_out_dx"))
    dwo = _mm(sv["merged"], dx1b, ta=True, out_dtype=GRAD_WIRE, name=nm("w_out_dw"))
    G["w_out"] = dwo.reshape(N_DEV, -1, dwo.shape[1])
    dgl, dya, dyb, dyc = _merge_bwd(rest, sv["ya"], sv["yb"], sv["yc"], dmerged, name=nm("merge"))
    col_blocks = lambda dw: jnp.moveaxis(dw.reshape(dw.shape[0], N_DEV, -1), 1, 0)
    G["p_pool"] = col_blocks(_mm(sv["ya0"], dya, ta=True, out_dtype=GRAD_WIRE, name=nm("p_pool_dw")))
    dya0 = _mm(dya, P["p_pool"], tb=True, name=nm("p_pool_dx"))
    G["p_attn"] = col_blocks(_mm(sv["o"], dyb, ta=True, out_dtype=GRAD_WIRE, name=nm("p_attn_dw")))
    do = _mm(dyb, P["p_attn"], tb=True, out_dtype=BF16, name=nm("p_attn_dx"))
    dps = _mm(sv["yc0"], dyc, ta=True, out_dtype=GRAD_WIRE, name=nm("p_ssd_dw"))
    G["p_ssd"] = dps.reshape(N_DEV, -1, dps.shape[1])
    dyc0 = _mm(dyc, P["p_ssd"], tb=True, name=nm("p_ssd_dx"))
    dpv, dmix, dpsc = _pool_bwd(dya0, sv["d"], P["mix"], P["pscale"], name=nm("pool"))
    G["pool_mix"], G["pool_scale"] = dmix, dpsc[0]
    carried = [((n, l), G[n]) for n in EARLY_GRADS] + pending
    dq, dk, dv, dcq, dck, got = _attn_bwd(sv["qkv"], sv["o"], do, sv["lrow"], sv["ccol"],
                                          comm=("exchange", [a for _, a in carried]), name=nm("attn"))
    received = {key: r for (key, _), r in zip(carried, got)}
    dy, dz, dsnw = _gate_norm_bwd(sv["y"], rest, REST_Z_BLK, P["snw"], dyc0, name=nm("ssd_norm"))
    G["ssd_norm"] = dsnw[0]
    dxc, ddt, dpar = _ssd_bwd(sv["xc"], rest, REST_FDT_BLK, P["dtb"], P["alog"], P["dsk"], sv["hs"], dy, name=nm("ssd"))
    G["ssd_a_log"] = dpar[0, DT_LANE0:DT_LANE0 + SSD_HEADS]
    G["ssd_d"] = dpar[1, DT_LANE0:DT_LANE0 + SSD_HEADS]
    dxbc, dcw, dcb = _conv_silu_bwd(rest, REST_XBC0, SSD_CH, P["cw"], P["cb"], dxc, name=nm("ssd_conv"))
    G["ssd_conv_w"] = jnp.moveaxis(dcw.reshape(dcw.shape[0], N_DEV, -1), 1, 0)
    G["ssd_conv_b"] = dcb[0]
    dfdt, dfdtb = _fox_post(dcq, dck, rest, REST_FDT_BLK, P["fb"], ddt, name=nm("fox_post"))
    G["f_bias"] = dfdtb[0, :ATTN_HEADS]
    G["ssd_dt_bias"] = dfdtb[0, DT_LANE0:DT_LANE0 + SSD_HEADS]
    dqkv = jnp.concatenate([dq, dk, dv], axis=1)
    drest = jnp.concatenate([dgl, dz, dpv, dxbc, dfdt], axis=1)
    dwq = _mm(sv["u"], dqkv, ta=True, out_dtype=GRAD_WIRE, name=nm("proj_qkv_dw"))
    dwr = _mm(sv["u"], drest, ta=True, out_dtype=GRAD_WIRE, name=nm("proj_rest_dw"))
    G["w_in"] = _repack_dw_in([dwq], [dwr], P["shard"], name=nm("repack_dw_in"))[:, 0]
    du = _mm(dqkv, P["w_qkv"], tb=True, name=nm("proj_qkv_dx"))
    late = [((n, l), G[n]) for n in LATE_GRADS]
    if last:
        du, got = _mm(drest, P["w_rest"], tb=True, add=du, comm=("exchange", [a for _, a in late]),
                      name=nm("proj_rest_dx"))
        received.update({key: r for (key, _), r in zip(late, got)})
        late = []
    else:
        du = _mm(drest, P["w_rest"], tb=True, add=du, name=nm("proj_rest_dx"))
    dx, dxb, dnm = _rms_bwd(sv["x"], P["nm"], du, dx1, name=nm("rms_mix"))
    G["norm_mix"] = dnm[0]
    return dx, dxb, G, received, late


def kernel(x, norm_mix, w_in, pool_mix, pool_scale, f_bias, ssd_conv_w, ssd_conv_b, ssd_dt_bias, ssd_a_log, ssd_d, ssd_norm, p_pool, p_attn, p_ssd, w_out, norm_ffn, ffn_up, ffn_conv_w, ffn_conv_b, ffn_down, norm_final, loss_target, m_norm_mix, m_w_in, m_pool_mix, m_pool_scale, m_f_bias, m_ssd_conv_w, m_ssd_conv_b, m_ssd_dt_bias, m_ssd_a_log, m_ssd_d, m_ssd_norm, m_p_pool, m_p_attn, m_p_ssd, m_w_out, m_norm_ffn, m_ffn_up, m_ffn_conv_w, m_ffn_conv_b, m_ffn_down, m_norm_final, v_norm_mix, v_w_in, v_pool_mix, v_pool_scale, v_f_bias, v_ssd_conv_w, v_ssd_conv_b, v_ssd_dt_bias, v_ssd_a_log, v_ssd_d, v_ssd_norm, v_p_pool, v_p_attn, v_p_ssd, v_w_out, v_norm_ffn, v_ffn_up, v_ffn_conv_w, v_ffn_conv_b, v_ffn_down, v_norm_final):
    args = dict(locals())
    W = {n: args[n] for n in WEIGHTS}
    M = {n: args["m_" + n] for n in WEIGHTS}
    V = {n: args["v_" + n] for n in WEIGHTS}
    L = norm_mix.shape[0]
    shard = w_in.shape[-1]
    up_w = ffn_up.shape[-1]
    up_pad = _round_up(up_w, LANE)

    payload = lambda l: _gather_payload(W, l, _round_up(shard, LANE), up_pad)
    first = payload(0)
    g_win = _all_gather(first[:1], name="gather_w_in_l0")[0]
    g_mixer, to_gather = None, first[1:]
    h, saves, params = x[0], [], []
    for l in range(L):
        to_gather = to_gather + (payload(l + 1) if l + 1 < L else [])
        h, sv, P, got = _layer_fwd(h, W, l, g_win, g_mixer, to_gather, (shard, up_w, up_pad))
        saves.append(sv)
        params.append(P)
        if l + 1 < L:
            g_win, g_mixer, to_gather = got[0], got[1:], []
    loss_part, g, gb, dnfin = _loss_head(h, norm_final[None], loss_target[0], name="loss_head")

    G, received, pending = [None] * L, {}, []
    for l in reversed(range(L)):
        g, gb, G[l], got, pending = _layer_bwd(g, gb, saves[l], params[l], l, pending, last=l == 0)
        received.update(got)

    rep_grads = {n: jnp.stack([G[l][n] for l in range(L)]) for n in REPLICATED if n != "norm_final"}
    rep_grads["norm_final"] = dnfin[0]
    rep_size = sum(W[n].size for n in REPLICATED)
    rep_rows = _round_up(-(-rep_size // (N_DEV * FLAT_W)), SUBLANE)
    rep_send = _flat_rows([rep_grads[n] for n in REPLICATED], N_DEV * rep_rows).reshape(N_DEV, rep_rows, FLAT_W)
    last = _exchange([rep_send], name="exchange_replicated_grads")

    out = {}
    for n in SHARDED:
        shape = W[n].shape
        as3 = lambda a: a.reshape(L, -1, shape[-1])
        res = _sum_adam([received[(n, l)] for l in range(L)], as3(W[n]), as3(M[n]), as3(V[n]), name="sum_adam_" + n)
        for key, arr in zip(("grad_", "delta_", "new_m_", "new_v_"), res):
            out[key + n] = arr.reshape(shape)

    rep_sum = _sum_parts(last[-1], name="sum_replicated")
    rep_g = _all_gather([rep_sum], name="gather_replicated_grads")[0].reshape(N_DEV * rep_rows, FLAT_W)
    flat_rep = lambda D: _flat_rows([D[n] for n in REPLICATED], N_DEV * rep_rows)
    d_rp, m_rp, v_rp = _adam(rep_g, flat_rep(W), flat_rep(M), flat_rep(V), name="adam_replicated")
    off = 0
    for n in REPLICATED:
        size = W[n].size
        for key, arr in (("grad_", rep_g), ("delta_", d_rp), ("new_m_", m_rp), ("new_v_", v_rp)):
            out[key + n] = arr.reshape(-1)[off:off + size].reshape(W[n].shape)
        off += size

    loss = lax.psum(loss_part[0, 0], ("x", "y", "c"))
    return (loss, g[None], *[out["grad_" + n] for n in WEIGHTS], *[out["delta_" + n] for n in WEIGHTS],
            *[out["new_m_" + n] for n in WEIGHTS], *[out["new_v_" + n] for n in WEIGHTS])
```

```python
import functools

import jax
import jax.numpy as jnp
from jax import lax
from jax.experimental import pallas as pl
from jax.experimental.pallas import tpu as pltpu

F32 = jnp.float32
BF16 = jnp.bfloat16

LANE = 128
SUBLANE = 8
VMEM_LIMIT = 56 * 1024 * 1024

NORM_EPS = 1e-6
POOL_WINDOWS = (2, 4, 8, 16)
POOL_GROUP_DIM = 128
ATTN_HEADS = 8
ATTN_HEAD_DIM = 64
SSD_HEAD_DIM = 64
SSD_HEADS = 16
SSD_GROUPS = 2
SSD_STATE = 128
SSD_CHUNK = 128

ADAM_LR = 0.001
ADAM_B1 = 0.9
ADAM_B2 = 0.999
ADAM_EPS = 1e-08
ADAM_WD = 0.01
ADAM_STEP = 10


def _cparams(*sem):
    return pltpu.CompilerParams(dimension_semantics=tuple(sem), vmem_limit_bytes=VMEM_LIMIT)


def _tile(n, pref):
    if n <= pref:
        return n
    assert n % LANE == 0, n
    q = n // LANE
    best = 1
    for d in range(1, pref // LANE + 1):
        if q % d == 0:
            best = d
    return best * LANE


GRAD_WIRE = BF16


def _mm_call(args, in_specs, out_spec, out_shape, grid, *, ta, tb, nk, acc_shape, has_add, keep, name, comm=None):
    dims = (((0 if ta else 1,), (1 if tb else 0,)), ((), ()))
    nc = len(comm[1]) if comm else 0
    n_in = 2 + has_add

    def body(*refs):
        a_ref, b_ref = refs[0], refs[1]
        o_ref = refs[n_in + nc]
        acc_ref = refs[n_in + 2 * nc + 1] if nk > 1 else None
        carried = (refs[n_in:n_in + nc], refs[n_in + nc + 1:n_in + 2 * nc + 1], refs[n_in + 2 * nc + 1 + (nk > 1):])
        i, j, k = pl.program_id(0), pl.program_id(1), pl.program_id(2)
        if comm:
            _carry_start(comm[0], *carried, jnp.logical_and(jnp.logical_and(i == 0, j == 0), k == 0))
        d = lax.dot_general(a_ref[...], b_ref[...], dims, preferred_element_type=F32)

        def finish(r):
            if has_add:
                r = r + refs[2][...]
            if keep is not None:
                r = r[:, :keep]
            o_ref[...] = r.astype(o_ref.dtype)

        if nk == 1:
            finish(d)
        else:
            @pl.when(k == 0)
            def _():
                acc_ref[...] = d

            @pl.when(jnp.logical_and(k > 0, k < nk - 1))
            def _():
                acc_ref[...] += d

            @pl.when(k == nk - 1)
            def _():
                finish(acc_ref[...] + d)

        if comm:
            _carry_wait(comm[0], *carried, jnp.logical_and(jnp.logical_and(i == grid[0] - 1, j == grid[1] - 1),
                                                           k == nk - 1))

    res = pl.pallas_call(
        body, name=name, grid=grid, in_specs=in_specs + [ANY] * nc, out_specs=[out_spec] + [ANY] * nc,
        out_shape=[out_shape] + (_a2a_out_shapes(*comm) if comm else []),
        scratch_shapes=([pltpu.VMEM(acc_shape, F32)] if nk > 1 else []) + (_a2a_sems(nc) if comm else []),
        compiler_params=_cparams(*(("arbitrary",) * 3 if comm else ("parallel", "parallel", "arbitrary"))),
    )(*args, *(comm[1] if comm else []))
    return (res[0], list(res[1:])) if comm else res[0]


MM_VMEM_BUDGET = 40 * 1024 * 1024
MM_MAX_ROWS = 2048


MM_MIN_ROWS_WHOLE_K = 256
MM_K_STEP = 4096


def _mm_rows(M, tn, tk, nk, out_bytes, has_add):
    best = None
    for tm in range(LANE, min(M, MM_MAX_ROWS) + 1, LANE):
        if M % tm:
            continue
        blocks = 2 * (tm * tk * 2 + tk * tn * 2 + tm * tn * out_bytes + (tm * tn * 4 if has_add else 0))
        blocks += tm * tn * 4 * (2 if nk > 1 else 1)
        if blocks <= MM_VMEM_BUDGET:
            best = tm
    return M if best is None else best


def _mm_tiles(M, N, K, out_bytes, has_add, whole_k_ok):
    tn = _tile(N, 1024)
    if whole_k_ok and K > 2048 and M % MM_MIN_ROWS_WHOLE_K == 0:
        tm = _mm_rows(M, tn, K, 1, out_bytes, has_add)
        need = 2 * (tm * K * 2 + K * tn * 2 + tm * tn * out_bytes + (tm * tn * 4 if has_add else 0)) + tm * tn * 4
        if tm >= MM_MIN_ROWS_WHOLE_K and need <= MM_VMEM_BUDGET:
            return tm, tn, K
    tk = _tile(K, MM_K_STEP)
    return _mm_rows(M, tn, tk, K // tk, out_bytes, has_add), tn, tk


def _mm(a, b, *, ta=False, tb=False, add=None, out_dtype=F32, comm=None, name):
    assert a.dtype == BF16 and b.dtype == BF16, (a.dtype, b.dtype)
    if ta:
        K, M = a.shape
    else:
        M, K = a.shape
    if tb:
        N, K2 = b.shape
    else:
        K2, N = b.shape
    assert K == K2, (a.shape, b.shape, ta, tb)
    tm, tn, tk = _mm_tiles(M, N, K, jnp.dtype(out_dtype).itemsize, add is not None, whole_k_ok=not ta)
    a_spec = pl.BlockSpec((tk, tm), lambda i, j, k: (k, i)) if ta else pl.BlockSpec((tm, tk), lambda i, j, k: (i, k))
    b_spec = pl.BlockSpec((tn, tk), lambda i, j, k: (j, k)) if tb else pl.BlockSpec((tk, tn), lambda i, j, k: (k, j))
    in_specs, args = [a_spec, b_spec], [a, b]
    if add is not None:
        in_specs.append(pl.BlockSpec((tm, tn), lambda i, j, k: (i, j)))
        args.append(add)
    return _mm_call(args, in_specs, pl.BlockSpec((tm, tn), lambda i, j, k: (i, j)),
                    jax.ShapeDtypeStruct((M, N), out_dtype), (M // tm, N // tn, K // tk),
                    ta=ta, tb=tb, nk=K // tk, acc_shape=(tm, tn), has_add=add is not None, keep=None, name=name,
                    comm=comm)


def _half_of(b, half):
    hi = jnp.where(b >= half, 1, 0)
    return hi, b - half * hi


def _mm_up_fwd(u, g_up, l, *, name):
    S, K = u.shape
    nblk, _, _, bw = g_up.shape
    tm = _mm_rows(S, bw, K, 1, 4, False)
    return _mm_call([u, g_up],
                    [pl.BlockSpec((tm, K), lambda i, j, k: (i, 0)),
                     pl.BlockSpec((None, None, K, bw), lambda i, j, k: (j, l, 0, 0))],
                    pl.BlockSpec((tm, bw), lambda i, j, k: (i, j)),
                    jax.ShapeDtypeStruct((S, nblk * bw), F32), (S // tm, nblk, 1),
                    ta=False, tb=False, nk=1, acc_shape=(tm, bw), has_add=False, keep=None, name=name)


def _mm_up_dx(dh, g_up, l, *, name):
    _, S, F = dh.shape
    nblk, _, N, bw = g_up.shape
    half = nblk // 2
    tm = min(MM_MIN_ROWS_WHOLE_K, S)

    def body(dh_ref, g_ref, o_ref):
        acc = None
        for j in range(nblk):
            a = dh_ref[j // half, :, (j % half) * bw:(j % half + 1) * bw]
            d = lax.dot_general(a, g_ref[j], (((1,), (1,)), ((), ())), preferred_element_type=F32)
            acc = d if acc is None else acc + d
        o_ref[...] = acc

    return pl.pallas_call(
        body, name=name, grid=(S // tm,),
        in_specs=[pl.BlockSpec((2, tm, F), lambda i: (0, i, 0)),
                  pl.BlockSpec((nblk, None, N, bw), lambda i: (0, l, 0, 0))],
        out_specs=pl.BlockSpec((tm, N), lambda i: (i, 0)),
        out_shape=jax.ShapeDtypeStruct((S, N), F32),
        compiler_params=_cparams("parallel"),
    )(dh, g_up)


def _mm_up_dw(u, dh, nblk, keep, *, name):
    S, M = u.shape
    half = nblk // 2
    bw = dh.shape[2] // half
    tk = _tile(S, MM_K_STEP)
    tm = _mm_rows(M, bw, tk, S // tk, jnp.dtype(GRAD_WIRE).itemsize, False)
    return _mm_call([u, dh],
                    [pl.BlockSpec((tk, tm), lambda i, j, k: (k, i)),
                     pl.BlockSpec((None, tk, bw), lambda i, j, k: (_half_of(j, half)[0], k, _half_of(j, half)[1]))],
                    pl.BlockSpec((None, tm, keep), lambda i, j, k: (j, i, 0)),
                    jax.ShapeDtypeStruct((nblk, M, keep), GRAD_WIRE), (M // tm, nblk, S // tk),
                    ta=True, tb=False, nk=S // tk, acc_shape=(tm, bw), has_add=False, keep=keep, name=name)


ROW_TILE = 512
HALO = 8
POOL_HALO = 16


def _row_tile(S):
    return min(ROW_TILE, S)


def _acc_out(ref, val, first):
    @pl.when(first)
    def _():
        ref[...] = val

    @pl.when(jnp.logical_not(first))
    def _():
        ref[...] += val


def _rms_fwd(x, w, *, name):
    S, D = x.shape
    ts = _row_tile(S)

    def body(x_ref, w_ref, u_ref):
        xv = x_ref[...]
        r = lax.rsqrt(jnp.mean(xv * xv, axis=-1, keepdims=True) + NORM_EPS)
        u_ref[...] = (xv * r * w_ref[...]).astype(BF16)

    return pl.pallas_call(
        body, name=name, grid=(S // ts,),
        in_specs=[pl.BlockSpec((ts, D), lambda i: (i, 0)), pl.BlockSpec((1, D), lambda i: (0, 0))],
        out_specs=pl.BlockSpec((ts, D), lambda i: (i, 0)),
        out_shape=jax.ShapeDtypeStruct((S, D), BF16),
        compiler_params=_cparams("parallel"),
    )(x, w)


def _rms_bwd(x, w, du, g, *, name):
    S, D = x.shape
    ts = _row_tile(S)

    def body(x_ref, w_ref, du_ref, g_ref, dx_ref, dxb_ref, dw_ref):
        xv = x_ref[...]
        r = lax.rsqrt(jnp.mean(xv * xv, axis=-1, keepdims=True) + NORM_EPS)
        n = xv * r
        duv = du_ref[...]
        dn = duv * w_ref[...]
        dx = g_ref[...] + r * (dn - n * jnp.mean(dn * n, axis=-1, keepdims=True))
        dx_ref[...] = dx
        dxb_ref[...] = dx.astype(BF16)
        _acc_out(dw_ref, jnp.sum(duv * n, axis=0, keepdims=True), pl.program_id(0) == 0)

    row = pl.BlockSpec((ts, D), lambda i: (i, 0))
    vec = pl.BlockSpec((1, D), lambda i: (0, 0))
    return pl.pallas_call(
        body, name=name, grid=(S // ts,),
        in_specs=[row, vec, row, row],
        out_specs=[row, row, vec],
        out_shape=[jax.ShapeDtypeStruct((S, D), F32), jax.ShapeDtypeStruct((S, D), BF16),
                   jax.ShapeDtypeStruct((1, D), F32)],
        compiler_params=_cparams("arbitrary"),
    )(x, w, du, g)


def _loss_head(x, w, target, *, name):
    S, D = x.shape
    ts = _row_tile(S)

    def body(x_ref, w_ref, t_ref, loss_ref, dx_ref, dxb_ref, dw_ref):
        xv = x_ref[...]
        wv = w_ref[...]
        r = lax.rsqrt(jnp.mean(xv * xv, axis=-1, keepdims=True) + NORM_EPS)
        n = xv * r
        err = n * wv - t_ref[...]
        part = 0.5 * jnp.sum(jnp.mean(err * err, axis=-1, keepdims=True), axis=0, keepdims=True)
        dy = err * (1.0 / D)
        dn = dy * wv
        dx = r * (dn - n * jnp.mean(dn * n, axis=-1, keepdims=True))
        dx_ref[...] = dx
        dxb_ref[...] = dx.astype(BF16)
        first = pl.program_id(0) == 0
        _acc_out(dw_ref, jnp.sum(dy * n, axis=0, keepdims=True), first)
        _acc_out(loss_ref, jnp.broadcast_to(part, loss_ref.shape), first)

    row = pl.BlockSpec((ts, D), lambda i: (i, 0))
    vec = pl.BlockSpec((1, D), lambda i: (0, 0))
    return pl.pallas_call(
        body, name=name, grid=(S // ts,),
        in_specs=[row, vec, row],
        out_specs=[pl.BlockSpec((SUBLANE, LANE), lambda i: (0, 0)), row, row, vec],
        out_shape=[jax.ShapeDtypeStruct((SUBLANE, LANE), F32), jax.ShapeDtypeStruct((S, D), F32),
                   jax.ShapeDtypeStruct((S, D), BF16), jax.ShapeDtypeStruct((1, D), F32)],
        compiler_params=_cparams("arbitrary"),
    )(x, w, target)


def _sigmoid(x):
    return 1.0 / (1.0 + jnp.exp(-x))


def _merge_fwd(gl, ya, yb, yc, *, name):
    S, D = ya.shape
    ts = _row_tile(S)

    def body(ga_ref, gb_ref, gc_ref, ya_ref, yb_ref, yc_ref, o_ref):
        m = (_sigmoid(ga_ref[...]) * ya_ref[...] + _sigmoid(gb_ref[...]) * yb_ref[...]
             + _sigmoid(gc_ref[...]) * yc_ref[...])
        o_ref[...] = m.astype(BF16)

    row = pl.BlockSpec((ts, D), lambda i: (i, 0))
    gspec = [pl.BlockSpec((ts, D), functools.partial(lambda b, i: (i, b), b)) for b in range(3)]
    return pl.pallas_call(
        body, name=name, grid=(S // ts,),
        in_specs=gspec + [row, row, row],
        out_specs=row,
        out_shape=jax.ShapeDtypeStruct((S, D), BF16),
        compiler_params=_cparams("parallel"),
    )(gl, gl, gl, ya, yb, yc)


def _merge_bwd(gl, ya, yb, yc, dm, *, name):
    S, D = ya.shape
    ts = _row_tile(S)

    def body(ga_ref, gb_ref, gc_ref, ya_ref, yb_ref, yc_ref, dm_ref, dgl_ref, da_ref, db_ref, dc_ref):
        dmv = dm_ref[...]
        for b, (g_ref, y_ref, dy_ref) in enumerate(((ga_ref, ya_ref, da_ref), (gb_ref, yb_ref, db_ref),
                                                     (gc_ref, yc_ref, dc_ref))):
            s = _sigmoid(g_ref[...])
            dy_ref[...] = (dmv * s).astype(BF16)
            dgl_ref[:, b * D:(b + 1) * D] = (dmv * y_ref[...] * s * (1.0 - s)).astype(BF16)

    row = pl.BlockSpec((ts, D), lambda i: (i, 0))
    gspec = [pl.BlockSpec((ts, D), functools.partial(lambda b, i: (i, b), b)) for b in range(3)]
    return pl.pallas_call(
        body, name=name, grid=(S // ts,),
        in_specs=gspec + [row, row, row, row],
        out_specs=[pl.BlockSpec((ts, 3 * D), lambda i: (i, 0)), row, row, row],
        out_shape=[jax.ShapeDtypeStruct((S, 3 * D), BF16)] + [jax.ShapeDtypeStruct((S, D), BF16)] * 3,
        compiler_params=_cparams("parallel"),
    )(gl, gl, gl, ya, yb, yc, dm)


POOL_WIDTH = 512


def _pool_cnt(t, w):
    return jnp.minimum(t + 1, w).astype(F32)


def _pool_fwd(rest, col, mix, scale, *, name):
    S = rest.shape[0]
    W, G, H = POOL_WIDTH, POOL_GROUP_DIM, POOL_HALO
    ts = _row_tile(S)

    def body(v_ref, h_ref, mix_ref, sc_ref, d_ref, y_ref, ext_ref):
        i = pl.program_id(0)
        cur = v_ref[...]
        ext_ref[0:H, :] = jnp.where(i > 0, h_ref[...], 0.0)
        ext_ref[H:, :] = cur
        t = i * ts + lax.broadcasted_iota(jnp.int32, (ts, 1), 0)
        for g, w in enumerate(POOL_WINDOWS):
            cols = slice(g * G, (g + 1) * G)
            acc = cur[:, cols]
            for k in range(1, w):
                acc = acc + ext_ref[pl.ds(H - k, ts), cols]
            db = (acc / _pool_cnt(t, w) - cur[:, cols]).astype(BF16)
            d_ref[:, cols] = db
            y = jnp.dot(db, mix_ref[g], preferred_element_type=F32) * sc_ref[:, cols]
            y_ref[:, cols] = y.astype(BF16)

    row = pl.BlockSpec((ts, W), lambda i: (i, 0))
    return pl.pallas_call(
        body, name=name, grid=(S // ts,),
        in_specs=[pl.BlockSpec((ts, W), lambda i: (i, col)),
                  pl.BlockSpec((H, W), lambda i: (jnp.maximum(i * (ts // H) - 1, 0), col)),
                  pl.BlockSpec((len(POOL_WINDOWS), G, G), lambda i: (0, 0, 0)),
                  pl.BlockSpec((1, W), lambda i: (0, 0))],
        out_specs=[row, row],
        out_shape=[jax.ShapeDtypeStruct((S, W), BF16)] * 2,
        scratch_shapes=[pltpu.VMEM((H + ts, W), F32)],
        compiler_params=_cparams("parallel"),
    )(rest, rest, mix, scale)


def _pool_bwd(dy, d, mix, scale, *, name):
    S = dy.shape[0]
    W, G, H = POOL_WIDTH, POOL_GROUP_DIM, POOL_HALO
    ts = _row_tile(S)
    n = S // ts
    NT = (((1,), (1,)), ((), ()))
    TN = (((0,), (0,)), ((), ()))

    def body(dy_ref, dyn_ref, d_ref, mix_ref, sc_ref, dv_ref, dmix_ref, dsc_ref, ext_ref):
        i = pl.program_id(0)

        @pl.when(i == 0)
        def _():
            dmix_ref[...] = jnp.zeros_like(dmix_ref)
            dsc_ref[...] = jnp.zeros_like(dsc_ref)

        t = i * ts + lax.broadcasted_iota(jnp.int32, (ts, 1), 0)
        tn = (i + 1) * ts + lax.broadcasted_iota(jnp.int32, (H, 1), 0)
        for g, w in enumerate(POOL_WINDOWS):
            cols = slice(g * G, (g + 1) * G)
            sc = sc_ref[:, cols]
            dyv = dy_ref[:, cols]
            db = d_ref[:, cols]
            mg = mix_ref[g]
            yp = jnp.dot(db, mg, preferred_element_type=F32)
            dsc_ref[:, cols] += jnp.sum(dyv * yp, axis=0, keepdims=True)
            dyp = (dyv * sc).astype(BF16)
            dd = lax.dot_general(dyp, mg, NT, preferred_element_type=F32)
            dmix_ref[g] += lax.dot_general(db, dyp, TN, preferred_element_type=F32)
            dyn = (jnp.where(i < n - 1, dyn_ref[:, cols], 0.0) * sc).astype(BF16)
            ddn = lax.dot_general(dyn, mg, NT, preferred_element_type=F32)
            e = dd / _pool_cnt(t, w)
            ext_ref[0:ts, cols] = e
            ext_ref[ts:, cols] = ddn / _pool_cnt(tn, w)
            acc = e
            for k in range(1, w):
                acc = acc + ext_ref[pl.ds(k, ts), cols]
            dv_ref[:, cols] = (acc - dd).astype(BF16)

    row = pl.BlockSpec((ts, W), lambda i: (i, 0))
    return pl.pallas_call(
        body, name=name, grid=(n,),
        in_specs=[row,
                  pl.BlockSpec((H, W), lambda i: (jnp.minimum((i + 1) * (ts // H), S // H - 1), 0)),
                  row,
                  pl.BlockSpec((len(POOL_WINDOWS), G, G), lambda i: (0, 0, 0)),
                  pl.BlockSpec((1, W), lambda i: (0, 0))],
        out_specs=[row, pl.BlockSpec((len(POOL_WINDOWS), G, G), lambda i: (0, 0, 0)),
                   pl.BlockSpec((1, W), lambda i: (0, 0))],
        out_shape=[jax.ShapeDtypeStruct((S, W), BF16),
                   jax.ShapeDtypeStruct((len(POOL_WINDOWS), G, G), F32),
                   jax.ShapeDtypeStruct((1, W), F32)],
        scratch_shapes=[pltpu.VMEM((ts + H, W), F32)],
        compiler_params=_cparams("arbitrary"),
    )(dy, dy, d, mix, scale)


def _conv_taps(ext_ref, w_ref, b_ref, rows, first):
    K = w_ref.shape[0]
    pre = b_ref[...] + w_ref[K - 1:K, :] * ext_ref[pl.ds(first, rows), :]
    for k in range(K - 1):
        pre = pre + w_ref[k:k + 1, :] * ext_ref[pl.ds(first - (K - 1) + k, rows), :]
    return pre


CONV_ROW_TILE = 1024
ROW_CHUNK = 40


def _chunks(rows):
    ch = max(c for c in range(SUBLANE, ROW_CHUNK + 1, SUBLANE) if rows % c == 0)
    return [(r0, ch) for r0 in range(0, rows, ch)]


def _conv_bwd_taps(g_ref, ext_ref, w_ref, dx_store, ts):
    K = w_ref.shape[0]
    db, dws = 0.0, [0.0] * K
    fold = lambda v: jnp.sum(v.reshape(v.shape[0] // SUBLANE, SUBLANE, v.shape[1]), axis=0)
    for r0, ch in _chunks(ts):
        dx = w_ref[K - 1:K, :] * g_ref[pl.ds(r0, ch), :]
        for k in range(K - 1):
            dx = dx + w_ref[k:k + 1, :] * g_ref[pl.ds(r0 + K - 1 - k, ch), :]
        dx_store(r0, ch, dx.astype(BF16))
        gc = g_ref[pl.ds(r0, ch), :]
        db = db + fold(gc)
        for k in range(K):
            dws[k] = dws[k] + fold(gc * ext_ref[pl.ds(HALO - (K - 1) + k + r0, ch), :])
    total = lambda v: jnp.sum(v, axis=0, keepdims=True)
    return total(db), jnp.concatenate([total(d) for d in dws], axis=0)


def _prev_halo_spec(ts, tc, colfn):
    return pl.BlockSpec((HALO, tc), lambda *g: (jnp.maximum(g[-1] * (ts // HALO) - 1, 0), colfn(*g)))


def _conv_silu_fwd(rest, col0, C, w, b, *, name):
    S = rest.shape[0]
    ts, tc = min(CONV_ROW_TILE, S), 512
    assert col0 % tc == 0 and C % tc == 0
    cb = col0 // tc

    def body(x_ref, h_ref, w_ref, b_ref, o_ref, ext_ref):
        i = pl.program_id(1)
        ext_ref[0:HALO, :] = jnp.where(i > 0, h_ref[...], 0.0)
        ext_ref[HALO:, :] = x_ref[...]
        for r0, ch in _chunks(ts):
            pre = _conv_taps(ext_ref, w_ref, b_ref, ch, HALO + r0)
            o_ref[pl.ds(r0, ch), :] = pre * _sigmoid(pre)

    K = w.shape[0]
    return pl.pallas_call(
        body, name=name, grid=(C // tc, S // ts),
        in_specs=[pl.BlockSpec((ts, tc), lambda j, i: (i, cb + j)),
                  _prev_halo_spec(ts, tc, lambda j, i: cb + j),
                  pl.BlockSpec((K, tc), lambda j, i: (0, j)),
                  pl.BlockSpec((1, tc), lambda j, i: (0, j))],
        out_specs=pl.BlockSpec((ts, tc), lambda j, i: (i, j)),
        out_shape=jax.ShapeDtypeStruct((S, C), F32),
        scratch_shapes=[pltpu.VMEM((HALO + ts, tc), F32)],
        compiler_params=_cparams("parallel", "parallel"),
    )(rest, rest, w, b)


def _dsilu(pre):
    s = _sigmoid(pre)
    return s, s * (1.0 + pre * (1.0 - s))


def _conv_silu_bwd(rest, col0, C, w, b, dy, *, name):
    S = rest.shape[0]
    ts, tc = min(CONV_ROW_TILE, S), 512
    cb = col0 // tc
    n = S // ts
    K = w.shape[0]
    R = ts + HALO

    def body(x_ref, hp_ref, hn_ref, w_ref, b_ref, dy_ref, dyn_ref, dx_ref, dw_ref, db_ref, ext_ref, g_ref):
        i = pl.program_id(1)
        last = i == n - 1
        ext_ref[0:HALO, :] = jnp.where(i > 0, hp_ref[...], 0.0)
        ext_ref[HALO:HALO + ts, :] = x_ref[...]
        ext_ref[HALO + ts:, :] = jnp.where(last, 0.0, hn_ref[...])
        g_ref[0:ts, :] = dy_ref[...]
        g_ref[ts:, :] = jnp.where(last, 0.0, dyn_ref[...])
        for r0, ch in _chunks(R):
            pre = _conv_taps(ext_ref, w_ref, b_ref, ch, HALO + r0)
            g_ref[pl.ds(r0, ch), :] = g_ref[pl.ds(r0, ch), :] * _dsilu(pre)[1]

        def dx_store(r0, ch, v):
            dx_ref[pl.ds(r0, ch), :] = v

        db, dw = _conv_bwd_taps(g_ref, ext_ref, w_ref, dx_store, ts)
        _acc_out(db_ref, db, i == 0)
        _acc_out(dw_ref, dw, i == 0)

    nxt = lambda cf: pl.BlockSpec((HALO, tc), lambda j, i: (jnp.minimum((i + 1) * (ts // HALO), S // HALO - 1), cf(j)))
    return pl.pallas_call(
        body, name=name, grid=(C // tc, n),
        in_specs=[pl.BlockSpec((ts, tc), lambda j, i: (i, cb + j)),
                  _prev_halo_spec(ts, tc, lambda j, i: cb + j),
                  nxt(lambda j: cb + j),
                  pl.BlockSpec((K, tc), lambda j, i: (0, j)),
                  pl.BlockSpec((1, tc), lambda j, i: (0, j)),
                  pl.BlockSpec((ts, tc), lambda j, i: (i, j)),
                  nxt(lambda j: j)],
        out_specs=[pl.BlockSpec((ts, tc), lambda j, i: (i, j)),
                   pl.BlockSpec((K, tc), lambda j, i: (0, j)),
                   pl.BlockSpec((1, tc), lambda j, i: (0, j))],
        out_shape=[jax.ShapeDtypeStruct((S, C), BF16), jax.ShapeDtypeStruct((K, C), F32),
                   jax.ShapeDtypeStruct((1, C), F32)],
        scratch_shapes=[pltpu.VMEM((HALO + R, tc), F32), pltpu.VMEM((R, tc), F32)],
        compiler_params=_cparams("parallel", "arbitrary"),
    )(rest, rest, rest, w, b, dy, dy)


FFN_TC = 256


def _ffn_act_fwd(h0, w, b, *, name):
    S, F2 = h0.shape
    F = F2 // 2
    ts, tc = min(CONV_ROW_TILE, S), FFN_TC
    nb = F // tc
    K = w.shape[0]

    def body(xg_ref, hg_ref, xv_ref, hv_ref, wg_ref, wv_ref, bg_ref, bv_ref, o_ref, eg_ref, ev_ref):
        i = pl.program_id(1)
        for x_ref, h_ref, e_ref in ((xg_ref, hg_ref, eg_ref), (xv_ref, hv_ref, ev_ref)):
            e_ref[0:HALO, :] = jnp.where(i > 0, h_ref[...], 0.0)
            e_ref[HALO:, :] = x_ref[...]
        for r0, ch in _chunks(ts):
            pg = _conv_taps(eg_ref, wg_ref, bg_ref, ch, HALO + r0)
            pv = _conv_taps(ev_ref, wv_ref, bv_ref, ch, HALO + r0)
            o_ref[pl.ds(r0, ch), :] = (pg * _sigmoid(pg) * pv).astype(BF16)

    def half(off):
        return [pl.BlockSpec((ts, tc), lambda j, i: (i, off + j)), _prev_halo_spec(ts, tc, lambda j, i: off + j)]

    wspec = lambda off: pl.BlockSpec((K, tc), lambda j, i: (0, off + j))
    bspec = lambda off: pl.BlockSpec((1, tc), lambda j, i: (0, off + j))
    return pl.pallas_call(
        body, name=name, grid=(nb, S // ts),
        in_specs=half(0) + half(nb) + [wspec(0), wspec(nb), bspec(0), bspec(nb)],
        out_specs=pl.BlockSpec((ts, tc), lambda j, i: (i, j)),
        out_shape=jax.ShapeDtypeStruct((S, F), BF16),
        scratch_shapes=[pltpu.VMEM((HALO + ts, tc), F32)] * 2,
        compiler_params=_cparams("parallel", "parallel"),
    )(h0, h0, h0, h0, w, w, b, b)


def _ffn_act_bwd(h0, w, b, da, *, name):
    S, F2 = h0.shape
    F = F2 // 2
    ts, tc = min(CONV_ROW_TILE, S), FFN_TC
    nb = F // tc
    n = S // ts
    K = w.shape[0]
    R = ts + HALO

    def body(xg_ref, pg_ref, ng_ref, xv_ref, pv_ref, nv_ref, wg_ref, wv_ref, bg_ref, bv_ref, da_ref, dan_ref,
             dx_ref, dw_ref, db_ref, eg_ref, ev_ref, gg_ref, gv_ref):
        i = pl.program_id(1)
        last = i == n - 1
        for x_ref, p_ref, n_ref, e_ref in ((xg_ref, pg_ref, ng_ref, eg_ref), (xv_ref, pv_ref, nv_ref, ev_ref)):
            e_ref[0:HALO, :] = jnp.where(i > 0, p_ref[...], 0.0)
            e_ref[HALO:HALO + ts, :] = x_ref[...]
            e_ref[HALO + ts:, :] = jnp.where(last, 0.0, n_ref[...])
        gg_ref[0:ts, :] = da_ref[...]
        gg_ref[ts:, :] = jnp.where(last, 0.0, dan_ref[...])
        for r0, ch in _chunks(R):
            rows = pl.ds(r0, ch)
            dav = gg_ref[rows, :]
            pg = _conv_taps(eg_ref, wg_ref, bg_ref, ch, HALO + r0)
            pv = _conv_taps(ev_ref, wv_ref, bv_ref, ch, HALO + r0)
            s, ds = _dsilu(pg)
            gg_ref[rows, :] = dav * pv * ds
            gv_ref[rows, :] = dav * pg * s
        for h, (g_ref, e_ref, w_ref) in enumerate(((gg_ref, eg_ref, wg_ref), (gv_ref, ev_ref, wv_ref))):
            def dx_store(r0, ch, v):
                dx_ref[h, pl.ds(r0, ch), :] = v

            db, dw = _conv_bwd_taps(g_ref, e_ref, w_ref, dx_store, ts)
            _acc_out(db_ref.at[h], db, i == 0)
            _acc_out(dw_ref.at[h], dw, i == 0)

    nxt = lambda off: pl.BlockSpec((HALO, tc), lambda j, i: (jnp.minimum((i + 1) * (ts // HALO), S // HALO - 1), off + j))

    def half(off):
        return [pl.BlockSpec((ts, tc), lambda j, i: (i, off + j)), _prev_halo_spec(ts, tc, lambda j, i: off + j), nxt(off)]

    wspec = lambda off: pl.BlockSpec((K, tc), lambda j, i: (0, off + j))
    bspec = lambda off: pl.BlockSpec((1, tc), lambda j, i: (0, off + j))
    return pl.pallas_call(
        body, name=name, grid=(nb, n),
        in_specs=half(0) + half(nb) + [wspec(0), wspec(nb), bspec(0), bspec(nb),
                                       pl.BlockSpec((ts, tc), lambda j, i: (i, j)), nxt(0)],
        out_specs=[pl.BlockSpec((2, ts, tc), lambda j, i: (0, i, j)),
                   pl.BlockSpec((2, K, tc), lambda j, i: (0, 0, j)),
                   pl.BlockSpec((2, 1, tc), lambda j, i: (0, 0, j))],
        out_shape=[jax.ShapeDtypeStruct((2, S, F), BF16), jax.ShapeDtypeStruct((2, K, F), F32),
                   jax.ShapeDtypeStruct((2, 1, F), F32)],
        scratch_shapes=[pltpu.VMEM((HALO + R, tc), F32)] * 2 + [pltpu.VMEM((R, tc), F32)] * 2,
        compiler_params=_cparams("parallel", "arbitrary"),
    )(h0, h0, h0, h0, h0, h0, w, w, b, b, da, da)


CUM_TILE = 256
ATT_TILE = 512
ATT_TILE_FWD = 1024
NEG = -1e30
N_PAIR = ATTN_HEADS // 2
NT_DIMS = (((1,), (1,)), ((), ()))


def _split3(x):
    a = x.astype(BF16)
    r = x - a.astype(F32)
    b = r.astype(BF16)
    c = (r - b.astype(F32)).astype(BF16)
    return a, b, c


def _tri_dot(tri, x):
    return sum(jnp.dot(tri, p, preferred_element_type=F32) for p in _split3(x))


def _log_sigmoid(x):
    return jnp.minimum(x, 0.0) - jnp.log(1.0 + jnp.exp(-jnp.abs(x)))


def _col(v, idx, lane):
    return jnp.sum(jnp.where(lane == idx, v, 0.0), axis=1, keepdims=True)


def _fox_pre(rest, col, fb, *, name):
    S = rest.shape[0]
    ts = min(CUM_TILE, S)

    def body(f_ref, fb_ref, ccol_ref, carry_ref):
        i = pl.program_id(0)

        @pl.when(i == 0)
        def _():
            carry_ref[...] = jnp.zeros_like(carry_ref)

        lane = lax.broadcasted_iota(jnp.int32, (1, LANE), 1)
        logf = jnp.where(lane < ATTN_HEADS, _log_sigmoid(f_ref[...] + fb_ref[...]), 0.0)
        r = lax.broadcasted_iota(jnp.int32, (ts, ts), 0)
        c = lax.broadcasted_iota(jnp.int32, (ts, ts), 1)
        tri = jnp.where(c <= r, 1.0, 0.0).astype(BF16)
        cs = _tri_dot(tri, logf) + carry_ref[...]
        ccol_ref[...] = cs
        carry_ref[...] = cs[ts - 1:ts, :]

    return pl.pallas_call(
        body, name=name, grid=(S // ts,),
        in_specs=[pl.BlockSpec((ts, LANE), lambda i: (i, col)), pl.BlockSpec((1, LANE), lambda i: (0, 0))],
        out_specs=pl.BlockSpec((ts, LANE), lambda i: (i, 0)),
        out_shape=jax.ShapeDtypeStruct((S, LANE), F32),
        scratch_shapes=[pltpu.VMEM((1, LANE), F32)],
        compiler_params=_cparams("arbitrary"),
    )(rest, fb)


def _fox_post(dcq, dck, rest, col, fb, ddt, *, name):
    S = rest.shape[0]
    ts = min(CUM_TILE, S)
    n = S // ts

    def body(dcq_ref, dck_ref, f_ref, fb_ref, ddt_ref, o_ref, db_ref, carry_ref):
        i = pl.program_id(0)

        @pl.when(i == 0)
        def _():
            carry_ref[...] = jnp.zeros_like(carry_ref)

        lane = lax.broadcasted_iota(jnp.int32, (1, LANE), 1)
        dc = jnp.zeros((ts, LANE), F32)
        for h in range(ATTN_HEADS):
            d = dcq_ref[h // 2] - dck_ref[h // 2]
            dc = jnp.where(lane == h, _col(d, h % 2, lane), dc)
        r = lax.broadcasted_iota(jnp.int32, (ts, ts), 0)
        c = lax.broadcasted_iota(jnp.int32, (ts, ts), 1)
        tri = jnp.where(c >= r, 1.0, 0.0).astype(BF16)
        rc = _tri_dot(tri, dc) + carry_ref[...]
        carry_ref[...] = rc[0:1, :]
        df = rc * _sigmoid(-(f_ref[...] + fb_ref[...]))
        out = jnp.where(lane < ATTN_HEADS, df, ddt_ref[...])
        o_ref[...] = out.astype(BF16)
        _acc_out(db_ref, jnp.sum(out, axis=0, keepdims=True), i == 0)

    rev = lambda i: n - 1 - i
    return pl.pallas_call(
        body, name=name, grid=(n,),
        in_specs=[pl.BlockSpec((N_PAIR, ts, LANE), lambda i: (0, rev(i), 0)),
                  pl.BlockSpec((N_PAIR, ts, LANE), lambda i: (0, rev(i), 0)),
                  pl.BlockSpec((ts, LANE), lambda i: (rev(i), col)),
                  pl.BlockSpec((1, LANE), lambda i: (0, 0)),
                  pl.BlockSpec((ts, LANE), lambda i: (rev(i), 0))],
        out_specs=[pl.BlockSpec((ts, LANE), lambda i: (rev(i), 0)), pl.BlockSpec((1, LANE), lambda i: (0, 0))],
        out_shape=[jax.ShapeDtypeStruct((S, LANE), BF16), jax.ShapeDtypeStruct((1, LANE), F32)],
        scratch_shapes=[pltpu.VMEM((1, LANE), F32)],
        compiler_params=_cparams("arbitrary"),
    )(dcq, dck, rest, fb, ddt)


def _head_masks():
    lane = lax.broadcasted_iota(jnp.int32, (1, LANE), 1)
    return lane, (lane < ATTN_HEAD_DIM, lane >= ATTN_HEAD_DIM)


def _causal_mask(t):
    r = lax.broadcasted_iota(jnp.int32, (t, t), 0)
    c = lax.broadcasted_iota(jnp.int32, (t, t), 1)
    return r, c


AUG_ONE = 0
AUG_C = 3


def _free_lane0(hh):
    return ATTN_HEAD_DIM if hh == 0 else 0


def _augment(x, head_mask, lane, f0, c, key_side):
    terms = _split3(c)
    one = jnp.ones((), BF16)
    ones_at, terms_at = (AUG_ONE, AUG_C) if key_side else (AUG_C, AUG_ONE)
    out = jnp.where(head_mask, x, jnp.zeros((), BF16))
    for k in range(3):
        out = jnp.where(lane == f0 + ones_at + k, one, out)
        out = jnp.where(lane == f0 + terms_at + k, terms[k], out)
    return out


def _carry_start(kind, in_refs, out_refs, sems, first):
    @pl.when(first)
    def _():
        if kind == "gather":
            _gather_phase(0, in_refs, out_refs, sems)
        else:
            _a2a_start(in_refs, out_refs, sems)


def _carry_mid(kind, in_refs, out_refs, sems, mid):
    if kind == "gather":
        @pl.when(mid)
        def _():
            _gather_phase(1, in_refs, out_refs, sems)


def _carry_wait(kind, in_refs, out_refs, sems, last):
    @pl.when(last)
    def _():
        if kind == "gather":
            _gather_phase(2, in_refs, out_refs, sems)
        else:
            _a2a_wait(in_refs, out_refs, sems)


def _attn_fwd(qkv, ccol, *, comm=None, name):
    S = qkv.shape[0]
    t = min(ATT_TILE_FWD, S)
    n = S // t
    scale = ATTN_HEAD_DIM ** -0.5

    nc = len(comm[1]) if comm else 0

    def body(*refs):
        q_ref, k_ref, v_ref, cc_ref = refs[:4]
        o_ref, lrow_ref = refs[4 + nc:6 + nc]
        kaug_ref, vt_ref, m_ref, l_ref, acc_ref = refs[6 + 2 * nc:11 + 2 * nc]
        j, i = pl.program_id(0), pl.program_id(1)
        lane, masks = _head_masks()
        if comm:
            carried = (refs[4:4 + nc], refs[6 + nc:6 + 2 * nc], refs[11 + 2 * nc:])
            _carry_start(comm[0], *carried, jnp.logical_and(j == 0, i == 0))
            _carry_mid(comm[0], *carried, jnp.logical_and(j == N_PAIR - 1, i == 0))

        @pl.when(i == 0)
        def _():
            def fill(b, carry):
                off = pl.multiple_of(b * t, t)
                kblk = k_ref[pl.ds(off, t), :]
                ccb = cc_ref[pl.ds(off, t), :]
                for hh in range(2):
                    ck = _col(ccb, 2 * j + hh, lane)
                    kaug_ref[hh, pl.ds(off, t), :] = _augment(kblk, masks[hh], lane, _free_lane0(hh), -ck, True)
                vt_ref[:, pl.ds(off, t)] = v_ref[pl.ds(off, t), :].astype(F32).T.astype(BF16)
                return carry
            lax.fori_loop(0, n, fill, 0)

        q2 = q_ref[...] * scale
        ccq = cc_ref[pl.ds(pl.multiple_of(i * t, t), t), :]
        qaug = [_augment(q2, masks[hh], lane, _free_lane0(hh), _col(ccq, 2 * j + hh, lane), False) for hh in range(2)]
        m_ref[...] = jnp.full(m_ref.shape, NEG, F32)
        l_ref[...] = jnp.zeros_like(l_ref)
        acc_ref[...] = jnp.zeros_like(acc_ref)

        def step(kb, masked):
            off = pl.multiple_of(kb * t, t)
            vt = vt_ref[:, pl.ds(off, t)]
            for hh in range(2):
                st = lax.dot_general(kaug_ref[hh, pl.ds(off, t), :], qaug[hh], NT_DIMS, preferred_element_type=F32)
                if masked:
                    r, c = _causal_mask(t)
                    st = jnp.where(r <= c, st, NEG)
                m_old = m_ref[hh]
                m_new = jnp.maximum(m_old, jnp.max(st, axis=0, keepdims=True))
                p = jnp.exp(st - m_new)
                alpha = jnp.exp(m_old - m_new)
                l_ref[hh] = alpha * l_ref[hh] + jnp.sum(p, axis=0, keepdims=True)
                acc_ref[hh] = alpha * acc_ref[hh] + jnp.dot(vt, p.astype(BF16), preferred_element_type=F32)
                m_ref[hh] = m_new

        def loop_body(kb, carry):
            step(kb, False)
            return carry

        lax.fori_loop(0, i, loop_body, 0)
        step(i, True)
        drow = lax.broadcasted_iota(jnp.int32, (LANE, 1), 0)
        ot = jnp.where(drow < ATTN_HEAD_DIM, acc_ref[0] / l_ref[0], acc_ref[1] / l_ref[1])
        o_ref[...] = ot.T.astype(BF16)
        for hh in range(2):
            lrow_ref[hh] = m_ref[hh] + jnp.log(l_ref[hh])
        if comm:
            _carry_wait(comm[0], *carried, jnp.logical_and(j == N_PAIR - 1, i == n - 1))

    res = pl.pallas_call(
        body, name=name, grid=(N_PAIR, n),
        in_specs=[pl.BlockSpec((t, LANE), lambda j, i: (i, j)),
                  pl.BlockSpec((S, LANE), lambda j, i: (0, N_PAIR + j)),
                  pl.BlockSpec((S, LANE), lambda j, i: (0, 2 * N_PAIR + j)),
                  pl.BlockSpec((S, LANE), lambda j, i: (0, 0))] + [ANY] * nc,
        out_specs=[pl.BlockSpec((t, LANE), lambda j, i: (i, j)),
                   pl.BlockSpec((2, 1, t), lambda j, i: (j, 0, i))] + [ANY] * nc,
        out_shape=[jax.ShapeDtypeStruct((S, N_PAIR * LANE), BF16), jax.ShapeDtypeStruct((ATTN_HEADS, 1, S), F32)]
        + (_a2a_out_shapes(*comm) if comm else []),
        scratch_shapes=[pltpu.VMEM((2, S, LANE), BF16), pltpu.VMEM((LANE, S), BF16),
                        pltpu.VMEM((2, 1, t), F32), pltpu.VMEM((2, 1, t), F32), pltpu.VMEM((2, LANE, t), F32)]
        + (_a2a_sems(nc) if comm else []),
        compiler_params=_cparams("arbitrary" if comm else "parallel", "arbitrary"),
    )(qkv, qkv, qkv, ccol, *(comm[1] if comm else []))
    return res[0], res[1], list(res[2:])


def _attn_bwd(qkv, o, do, lrow, ccol, *, comm=None, name):
    S = qkv.shape[0]
    t = min(ATT_TILE, S)
    n = S // t
    scale = ATTN_HEAD_DIM ** -0.5

    nc = len(comm[1]) if comm else 0

    def body(*refs):
        q_ref, k_ref, v_ref, o_ref, do_ref, lrow_ref, cc_ref = refs[:7]
        dq_ref, dk_ref, dv_ref, dcq_ref, dck_ref = refs[7 + nc:12 + nc]
        qaug_ref, drow_ref, dqt_ref, dka_ref, dva_ref = refs[12 + 2 * nc:17 + 2 * nc]
        carried = (refs[7:7 + nc], refs[12 + nc:12 + 2 * nc], refs[17 + 2 * nc:])
        j, kb = pl.program_id(0), pl.program_id(1)
        lane, masks = _head_masks()
        if comm:
            _carry_start(comm[0], *carried, jnp.logical_and(j == 0, kb == 0))
        drows = lax.broadcasted_iota(jnp.int32, (LANE, 1), 0)

        @pl.when(kb == 0)
        def _():
            dqt_ref[...] = jnp.zeros_like(dqt_ref)

            def fill(b, carry):
                off = pl.multiple_of(b * t, t)
                q2 = q_ref[pl.ds(off, t), :] * scale
                ccb = cc_ref[pl.ds(off, t), :]
                prod_t = (do_ref[pl.ds(off, t), :].astype(F32) * o_ref[pl.ds(off, t), :].astype(F32)).T
                lo = jnp.sum(jnp.where(drows < ATTN_HEAD_DIM, prod_t, 0.0), axis=0, keepdims=True)
                drow_ref[0, :, pl.ds(off, t)] = lo
                drow_ref[1, :, pl.ds(off, t)] = jnp.sum(prod_t, axis=0, keepdims=True) - lo
                for hh in range(2):
                    cq = _col(ccb, 2 * j + hh, lane)
                    qaug_ref[hh, pl.ds(off, t), :] = _augment(q2, masks[hh], lane, _free_lane0(hh), cq, False)
                return carry
            lax.fori_loop(0, n, fill, 0)

        koff = pl.multiple_of(kb * t, t)
        kblk = k_ref[...]
        v2 = v_ref[...]
        cck = cc_ref[pl.ds(koff, t), :]
        kaug = [_augment(kblk, masks[hh], lane, _free_lane0(hh), -_col(cck, 2 * j + hh, lane), True) for hh in range(2)]
        kaug_t = [ka.astype(F32).T.astype(BF16) for ka in kaug]
        vh = [jnp.where(mk, v2, jnp.zeros((), BF16)) for mk in masks]
        dka_ref[...] = jnp.zeros_like(dka_ref)
        dva_ref[...] = jnp.zeros_like(dva_ref)

        def step(qb, masked):
            off = pl.multiple_of(qb * t, t)
            doblk = do_ref[pl.ds(off, t), :]
            for hh in range(2):
                qa = qaug_ref[hh, pl.ds(off, t), :]
                st = lax.dot_general(kaug[hh], qa, NT_DIMS, preferred_element_type=F32)
                if masked:
                    r, c = _causal_mask(t)
                    st = jnp.where(r <= c, st, NEG)
                pt = jnp.exp(st - lrow_ref[hh, :, pl.ds(off, t)])
                dpt = lax.dot_general(vh[hh], doblk, NT_DIMS, preferred_element_type=F32)
                dst = (pt * (dpt - drow_ref[hh, :, pl.ds(off, t)])).astype(BF16)
                dva_ref[hh] += jnp.dot(pt.astype(BF16), doblk, preferred_element_type=F32)
                dka_ref[hh] += jnp.dot(dst, qa, preferred_element_type=F32)
                dqt_ref[hh, :, pl.ds(off, t)] += jnp.dot(kaug_t[hh], dst, preferred_element_type=F32)

        step(kb, True)

        def loop_body(qb, carry):
            step(qb, False)
            return carry

        lax.fori_loop(kb + 1, n, loop_body, 0)
        dk_ref[...] = jnp.where(masks[0], dka_ref[0], dka_ref[1]).astype(BF16)
        dv_ref[...] = jnp.where(masks[0], dva_ref[0], dva_ref[1]).astype(BF16)
        dck = [_col(dka_ref[hh], _free_lane0(hh) + AUG_C, lane) for hh in range(2)]
        dck_ref[0] = jnp.where(lane == 0, dck[0], jnp.where(lane == 1, dck[1], 0.0))

        @pl.when(kb == n - 1)
        def _():
            def flush(b, carry):
                off = pl.multiple_of(b * t, t)
                d = [dqt_ref[hh, :, pl.ds(off, t)].T for hh in range(2)]
                dq_ref[pl.ds(off, t), :] = (jnp.where(masks[0], d[0], d[1]) * scale).astype(BF16)
                dcq = [_col(d[hh], _free_lane0(hh) + AUG_ONE, lane) for hh in range(2)]
                dcq_ref[0, pl.ds(off, t), :] = jnp.where(lane == 0, dcq[0], jnp.where(lane == 1, dcq[1], 0.0))
                return carry
            lax.fori_loop(0, n, flush, 0)

        if comm:
            _carry_wait(comm[0], *carried, jnp.logical_and(j == N_PAIR - 1, kb == n - 1))

    full = lambda cb: pl.BlockSpec((S, LANE), lambda j, kb: (0, cb(j)))
    kspec = lambda base: pl.BlockSpec((t, LANE), lambda j, kb: (kb, base + j))
    oblk = pl.BlockSpec((t, LANE), lambda j, kb: (kb, j))
    res = pl.pallas_call(
        body, name=name, grid=(N_PAIR, n),
        in_specs=[full(lambda j: j), kspec(N_PAIR), kspec(2 * N_PAIR), full(lambda j: j), full(lambda j: j),
                  pl.BlockSpec((2, 1, S), lambda j, kb: (j, 0, 0)), full(lambda j: 0)] + [ANY] * nc,
        out_specs=[full(lambda j: j), oblk, oblk,
                   pl.BlockSpec((1, S, LANE), lambda j, kb: (j, 0, 0)),
                   pl.BlockSpec((1, t, LANE), lambda j, kb: (j, kb, 0))] + [ANY] * nc,
        out_shape=[jax.ShapeDtypeStruct((S, N_PAIR * LANE), BF16)] * 3 + [jax.ShapeDtypeStruct((N_PAIR, S, LANE), F32)] * 2
        + (_a2a_out_shapes(*comm) if comm else []),
        scratch_shapes=[pltpu.VMEM((2, S, LANE), BF16), pltpu.VMEM((2, 1, S), F32), pltpu.VMEM((2, LANE, S), F32),
                        pltpu.VMEM((2, t, LANE), F32), pltpu.VMEM((2, t, LANE), F32)]
        + (_a2a_sems(nc) if comm else []),
        compiler_params=_cparams("arbitrary" if comm else "parallel", "arbitrary"),
    )(qkv, qkv, qkv, o, do, lrow, ccol, *(comm[1] if comm else []))
    return tuple(res[:5]) + (list(res[5:]),)


DT_LANE0 = ATTN_HEADS
SSD_PAIRS = SSD_HEADS // 2
SSD_X = SSD_HEADS * SSD_HEAD_DIM
SSD_B0 = SSD_X
SSD_C0 = SSD_X + SSD_GROUPS * SSD_STATE
SSD_CH = SSD_X + 2 * SSD_GROUPS * SSD_STATE
TN_DIMS = (((0,), (0,)), ((), ()))


def _softplus(x):
    return jnp.maximum(x, 0.0) + jnp.log(1.0 + jnp.exp(-jnp.abs(x)))


def _ssd_prep(fdt, dtb, alog):
    L = fdt.shape[0]
    lane = lax.broadcasted_iota(jnp.int32, (1, LANE), 1)
    hl = jnp.logical_and(lane >= DT_LANE0, lane < DT_LANE0 + SSD_HEADS)
    dtv = jnp.where(hl, _softplus(fdt + dtb), 0.0)
    A = jnp.where(hl, -jnp.exp(alog), 0.0)
    r = lax.broadcasted_iota(jnp.int32, (L, L), 0)
    c = lax.broadcasted_iota(jnp.int32, (L, L), 1)
    cs = _tri_dot(jnp.where(c <= r, 1.0, 0.0).astype(BF16), dtv * A)
    return lane, hl, dtv, A, cs, r, c


def _halves(lane, v0, v1):
    return jnp.where(lane < SSD_HEAD_DIM, v0, v1)


def _head_select(width):
    r = jnp.arange(LANE)[:, None]
    c = jnp.arange(SSD_HEADS * width)[None, :]
    return (r == DT_LANE0 + c // width).astype(BF16)


def _spread(x, sel, terms):
    return sum(jnp.dot(p, sel, preferred_element_type=F32) for p in _split3(x)[:terms])


def _ssd_fwd(xc, rest, col, dtb, alog, dskip, *, name):
    S = xc.shape[0]
    L = SSD_CHUNK
    nc = S // L

    def body(xc_ref, f_ref, dtb_ref, al_ref, dk_ref, selp_ref, y_ref, hs_ref, h_ref, cst_ref):
        i = pl.program_id(0)

        @pl.when(i == 0)
        def _():
            h_ref[...] = jnp.zeros_like(h_ref)

        lane, hl, dtv, A, cs, r, c = _ssd_prep(f_ref[...], dtb_ref[...], al_ref[...])
        cst_ref[...] = cs.T
        cs_last = cs[L - 1:L, :]
        cd = jnp.exp(cs_last)
        dkv = dk_ref[...]
        selp = selp_ref[...]
        dt_p, ecs_p, dec_p = (_spread(v, selp, 2) for v in (dtv, jnp.exp(cs), jnp.exp(cs_last - cs)))
        prow = lax.broadcasted_iota(jnp.int32, (LANE, 1), 0)
        for g in range(SSD_GROUPS):
            Bg = xc_ref[:, SSD_B0 + g * SSD_STATE:SSD_B0 + (g + 1) * SSD_STATE].astype(BF16)
            Cg = xc_ref[:, SSD_C0 + g * SSD_STATE:SSD_C0 + (g + 1) * SSD_STATE].astype(BF16)
            CB = lax.dot_general(Cg, Bg, NT_DIMS, preferred_element_type=F32)
            for pp in range(SSD_PAIRS // SSD_GROUPS):
                pi = g * (SSD_PAIRS // SSD_GROUPS) + pp
                hl0 = DT_LANE0 + 2 * pi
                pair = slice(pi * LANE, (pi + 1) * LANE)
                x2 = xc_ref[:, pair]
                xd = x2 * dt_p[:, pair]
                xdb = xd.astype(BF16)
                yd = jnp.zeros((L, LANE), F32)
                for hh in range(2):
                    seg = _col(cs, hl0 + hh, lane) - cst_ref[hl0 + hh:hl0 + hh + 1, :]
                    M = CB * jnp.exp(jnp.where(c <= r, seg, NEG))
                    yh = jnp.dot(M.astype(BF16), xdb, preferred_element_type=F32)
                    yd = jnp.where((lane >= SSD_HEAD_DIM) if hh else (lane < SSD_HEAD_DIM), yh, yd)
                hp = h_ref[pi]
                hs_ref[0, pi] = hp
                yo = lax.dot_general(Cg, hp.astype(BF16), NT_DIMS, preferred_element_type=F32) * ecs_p[:, pair]
                dsk = _halves(lane, _col(dkv, hl0, lane), _col(dkv, hl0 + 1, lane))
                y_ref[:, pair] = yd + yo + dsk * x2
                xw = (xd * dec_p[:, pair]).astype(BF16)
                st = lax.dot_general(xw, Bg, TN_DIMS, preferred_element_type=F32)
                cdp = jnp.where(prow < SSD_HEAD_DIM, _col(cd, hl0, lane), _col(cd, hl0 + 1, lane))
                h_ref[pi] = cdp * hp + st

    vec = pl.BlockSpec((1, LANE), lambda i: (0, 0))
    selp = _head_select(SSD_HEAD_DIM)
    return pl.pallas_call(
        body, name=name, grid=(nc,),
        in_specs=[pl.BlockSpec((L, SSD_CH), lambda i: (i, 0)), pl.BlockSpec((L, LANE), lambda i: (i, col)),
                  vec, vec, vec, pl.BlockSpec(selp.shape, lambda i: (0, 0))],
        out_specs=[pl.BlockSpec((L, SSD_X), lambda i: (i, 0)),
                   pl.BlockSpec((1, SSD_PAIRS, LANE, SSD_STATE), lambda i: (i, 0, 0, 0))],
        out_shape=[jax.ShapeDtypeStruct((S, SSD_X), F32),
                   jax.ShapeDtypeStruct((nc, SSD_PAIRS, LANE, SSD_STATE), F32)],
        scratch_shapes=[pltpu.VMEM((SSD_PAIRS, LANE, SSD_STATE), F32), pltpu.VMEM((LANE, L), F32)],
        compiler_params=_cparams("arbitrary"),
    )(xc, rest, dtb, alog, dskip, selp)


def _pair_sums(lane, v):
    lo = jnp.sum(jnp.where(lane < SSD_HEAD_DIM, v, 0.0), axis=1, keepdims=True)
    return lo, jnp.sum(v, axis=1, keepdims=True) - lo


def _ssd_bwd(xc, rest, col, dtb, alog, dskip, hs, dy, *, name):
    S = xc.shape[0]
    L = SSD_CHUNK
    nc = S // L
    PG = SSD_PAIRS // SSD_GROUPS

    def body(xc_ref, f_ref, dtb_ref, al_ref, dk_ref, hs_ref, dy_ref, selp_ref,
             dxc_ref, ddt_ref, dp_ref, dh_ref, cst_ref):
        i = pl.program_id(0)

        @pl.when(i == 0)
        def _():
            dh_ref[...] = jnp.zeros_like(dh_ref)
            dp_ref[...] = jnp.zeros_like(dp_ref)

        fv = f_ref[...] + dtb_ref[...]
        lane, hl, dtv, A, cs, r, c = _ssd_prep(f_ref[...], dtb_ref[...], al_ref[...])
        cst_ref[...] = cs.T
        cs_last = cs[L - 1:L, :]
        cd = jnp.exp(cs_last)
        dkv = dk_ref[...]
        selp = selp_ref[...]
        dt_p, ecs_p, dec_p = (_spread(v, selp, 2) for v in (dtv, jnp.exp(cs), jnp.exp(cs_last - cs)))
        prow = lax.broadcasted_iota(jnp.int32, (LANE, 1), 0)
        lrow = lax.broadcasted_iota(jnp.int32, (L, 1), 0)
        is_last = lrow == L - 1
        causal = c <= r
        dcs = jnp.zeros((L, LANE), F32)
        ddt = jnp.zeros((L, LANE), F32)
        dD = jnp.zeros((1, LANE), F32)
        for g in range(SSD_GROUPS):
            Bg = xc_ref[:, SSD_B0 + g * SSD_STATE:SSD_B0 + (g + 1) * SSD_STATE].astype(BF16)
            Cg = xc_ref[:, SSD_C0 + g * SSD_STATE:SSD_C0 + (g + 1) * SSD_STATE].astype(BF16)
            CB = lax.dot_general(Cg, Bg, NT_DIMS, preferred_element_type=F32)
            dCB = jnp.zeros((L, L), F32)
            dB = jnp.zeros((L, SSD_STATE), F32)
            dC = jnp.zeros((L, SSD_STATE), F32)
            for pp in range(PG):
                pi = g * PG + pp
                hl0 = DT_LANE0 + 2 * pi
                pair = slice(pi * LANE, (pi + 1) * LANE)
                x2 = xc_ref[:, pair]
                dy2 = dy_ref[:, pair]
                dtp, ecsp, decp = dt_p[:, pair], ecs_p[:, pair], dec_p[:, pair]
                xd = x2 * dtp
                xdb = xd.astype(BF16)
                dsk = _halves(lane, _col(dkv, hl0, lane), _col(dkv, hl0 + 1, lane))
                dx2 = dsk * dy2
                sD = _pair_sums(lane, dy2 * x2)
                hp = hs_ref[0, pi]
                hpb = hp.astype(BF16)
                yo = lax.dot_general(Cg, hpb, NT_DIMS, preferred_element_type=F32) * ecsp
                dW = (ecsp * dy2).astype(BF16)
                dC = dC + jnp.dot(dW, hpb, preferred_element_type=F32)
                dhp = lax.dot_general(dW, Cg, TN_DIMS, preferred_element_type=F32)
                sYo = _pair_sums(lane, dy2 * yo)
                dhn = dh_ref[pi]
                cdh = (_col(cd, hl0, lane), _col(cd, hl0 + 1, lane))
                dhp = dhp + jnp.where(prow < SSD_HEAD_DIM, cdh[0], cdh[1]) * dhn
                rs = jnp.sum(dhn * hp, axis=1, keepdims=True)
                lo = jnp.sum(jnp.where(prow < SSD_HEAD_DIM, rs, 0.0), axis=0, keepdims=True)
                dcd = (lo, jnp.sum(rs, axis=0, keepdims=True) - lo)
                dhnb = dhn.astype(BF16)
                G = lax.dot_general(Bg, dhnb, NT_DIMS, preferred_element_type=F32)
                dxd = decp * G
                sdd = _pair_sums(lane, xd * dxd)
                dB = dB + jnp.dot((xd * decp).astype(BF16), dhnb, preferred_element_type=F32)
                dh_ref[pi] = dhp
                for hh in range(2):
                    hmask = (lane >= SSD_HEAD_DIM) if hh else (lane < SSD_HEAD_DIM)
                    seg = _col(cs, hl0 + hh, lane) - cst_ref[hl0 + hh:hl0 + hh + 1, :]
                    Lm = jnp.exp(jnp.where(causal, seg, NEG))
                    M = CB * Lm
                    dyh = jnp.where(hmask, dy2, 0.0).astype(BF16)
                    dM = lax.dot_general(dyh, xdb, NT_DIMS, preferred_element_type=F32)
                    dxd = dxd + lax.dot_general(M.astype(BF16), dyh, TN_DIMS, preferred_element_type=F32)
                    Q = dM * M
                    dCB = dCB + dM * Lm
                    dd = sdd[hh]
                    end = dcd[hh] * cdh[hh] + jnp.sum(dd, axis=0, keepdims=True)
                    dcs_h = (sYo[hh] - dd + jnp.sum(Q, axis=1, keepdims=True)
                             - jnp.sum(Q.T, axis=1, keepdims=True) + jnp.where(is_last, end, 0.0))
                    dcs = jnp.where(lane == hl0 + hh, dcs_h, dcs)
                    dD = jnp.where(lane == hl0 + hh, jnp.sum(sD[hh], axis=0, keepdims=True), dD)
                sdt = _pair_sums(lane, dxd * x2)
                ddt = jnp.where(lane == hl0, sdt[0], jnp.where(lane == hl0 + 1, sdt[1], ddt))
                dxc_ref[:, pair] = dx2 + dxd * dtp
            dCBb = dCB.astype(BF16)
            dC = dC + jnp.dot(dCBb, Bg, preferred_element_type=F32)
            dB = dB + lax.dot_general(dCBb, Cg, TN_DIMS, preferred_element_type=F32)
            dxc_ref[:, SSD_B0 + g * SSD_STATE:SSD_B0 + (g + 1) * SSD_STATE] = dB
            dxc_ref[:, SSD_C0 + g * SSD_STATE:SSD_C0 + (g + 1) * SSD_STATE] = dC
        da = _tri_dot(jnp.where(c >= r, 1.0, 0.0).astype(BF16), dcs)
        ddtv = ddt + da * A
        ddt_ref[...] = jnp.where(hl, ddtv * _sigmoid(fv), 0.0)
        dal = jnp.sum(da * dtv, axis=0, keepdims=True) * A
        dp_ref[0:1, :] += dal
        dp_ref[1:2, :] += dD

    rev = lambda i: nc - 1 - i
    vec = pl.BlockSpec((1, LANE), lambda i: (0, 0))
    selp = _head_select(SSD_HEAD_DIM)
    return pl.pallas_call(
        body, name=name, grid=(nc,),
        in_specs=[pl.BlockSpec((L, SSD_CH), lambda i: (rev(i), 0)), pl.BlockSpec((L, LANE), lambda i: (rev(i), col)),
                  vec, vec, vec,
                  pl.BlockSpec((1, SSD_PAIRS, LANE, SSD_STATE), lambda i: (rev(i), 0, 0, 0)),
                  pl.BlockSpec((L, SSD_X), lambda i: (rev(i), 0)), pl.BlockSpec(selp.shape, lambda i: (0, 0))],
        out_specs=[pl.BlockSpec((L, SSD_CH), lambda i: (rev(i), 0)), pl.BlockSpec((L, LANE), lambda i: (rev(i), 0)),
                   pl.BlockSpec((SUBLANE, LANE), lambda i: (0, 0))],
        out_shape=[jax.ShapeDtypeStruct((S, SSD_CH), F32), jax.ShapeDtypeStruct((S, LANE), F32),
                   jax.ShapeDtypeStruct((SUBLANE, LANE), F32)],
        scratch_shapes=[pltpu.VMEM((SSD_PAIRS, LANE, SSD_STATE), F32), pltpu.VMEM((LANE, L), F32)],
        compiler_params=_cparams("arbitrary"),
    )(xc, rest, dtb, alog, dskip, hs, dy, selp)


def _gate_norm_fwd(y, rest, zcol, nw, *, name):
    S, W = y.shape
    ts = _row_tile(S)
    GW = W // SSD_GROUPS

    def body(y_ref, z_ref, nw_ref, o_ref):
        z = z_ref[...]
        t = y_ref[...] * (z * _sigmoid(z))
        for g in range(SSD_GROUPS):
            cols = slice(g * GW, (g + 1) * GW)
            tg = t[:, cols]
            rr = lax.rsqrt(jnp.mean(tg * tg, axis=-1, keepdims=True) + NORM_EPS)
            o_ref[:, cols] = (tg * rr * nw_ref[:, cols]).astype(BF16)

    row = pl.BlockSpec((ts, W), lambda i: (i, 0))
    return pl.pallas_call(
        body, name=name, grid=(S // ts,),
        in_specs=[row, pl.BlockSpec((ts, W), lambda i: (i, zcol)), pl.BlockSpec((1, W), lambda i: (0, 0))],
        out_specs=row,
        out_shape=jax.ShapeDtypeStruct((S, W), BF16),
        compiler_params=_cparams("parallel"),
    )(y, rest, nw)


def _gate_norm_bwd(y, rest, zcol, nw, do, *, name):
    S, W = y.shape
    ts = _row_tile(S)
    GW = W // SSD_GROUPS

    def body(y_ref, z_ref, nw_ref, do_ref, dy_ref, dz_ref, dnw_ref):
        z = z_ref[...]
        yv = y_ref[...]
        s, ds = _dsilu(z)
        sz = z * s
        t = yv * sz
        dov = do_ref[...]
        parts = []
        for g in range(SSD_GROUPS):
            cols = slice(g * GW, (g + 1) * GW)
            tg = t[:, cols]
            rr = lax.rsqrt(jnp.mean(tg * tg, axis=-1, keepdims=True) + NORM_EPS)
            n = tg * rr
            dog = dov[:, cols]
            dn = dog * nw_ref[:, cols]
            dt = rr * (dn - n * jnp.mean(dn * n, axis=-1, keepdims=True))
            dy_ref[:, cols] = dt * sz[:, cols]
            dz_ref[:, cols] = (dt * yv[:, cols] * ds[:, cols]).astype(BF16)
            parts.append(jnp.sum(dog * n, axis=0, keepdims=True))
        _acc_out(dnw_ref, jnp.concatenate(parts, axis=1), pl.program_id(0) == 0)

    row = pl.BlockSpec((ts, W), lambda i: (i, 0))
    vec = pl.BlockSpec((1, W), lambda i: (0, 0))
    return pl.pallas_call(
        body, name=name, grid=(S // ts,),
        in_specs=[row, pl.BlockSpec((ts, W), lambda i: (i, zcol)), vec, row],
        out_specs=[row, row, vec],
        out_shape=[jax.ShapeDtypeStruct((S, W), F32), jax.ShapeDtypeStruct((S, W), BF16),
                   jax.ShapeDtypeStruct((1, W), F32)],
        compiler_params=_cparams("arbitrary"),
    )(y, rest, nw, do)


N_DEV = 8
MESH = pl.DeviceIdType.MESH
ANY = pl.BlockSpec(memory_space=pl.ANY)


def _all_gather(xs, *, name):
    n = len(xs)

    def body(*refs):
        for phase in range(3):
            _gather_phase(phase, refs[:n], refs[n:2 * n], refs[2 * n:])

    return pl.pallas_call(
        body, name=name,
        out_shape=_a2a_out_shapes("gather", xs), in_specs=[ANY] * n, out_specs=[ANY] * n,
        scratch_shapes=_a2a_sems(n),
    )(*xs)


def _gather_phase(phase, x_refs, o_refs, sems):
    send_sems, recv_sems, local_sems = sems
    n = len(x_refs)
    px, py, pc = lax.axis_index("x"), lax.axis_index("y"), lax.axis_index("c")
    me, sibling = (px, py, pc), (px, py, 1 - pc)
    chips = [(1 - px, py), (px, 1 - py), (1 - px, 1 - py)]

    def copy(a, k, block, to, src=None):
        slot = o_refs[a].at[4 * block[0] + 2 * block[1] + block[2]]
        return pltpu.make_async_remote_copy(
            src_ref=slot if src is None else src, dst_ref=slot,
            send_sem=send_sems.at[a, k], recv_sem=recv_sems.at[a, k], device_id=to, device_id_type=MESH)

    def own(a):
        return ([copy(a, 0, me, sibling, src=x_refs[a])]
                + [copy(a, 1 + j, me, (*chip, pc), src=x_refs[a]) for j, chip in enumerate(chips)])

    mine = lambda a: pltpu.make_async_copy(x_refs[a], o_refs[a].at[4 * px + 2 * py + pc], local_sems.at[a])
    passed = lambda a, j: copy(a, 4 + j, (*chips[j], pc), sibling)
    if phase == 0:
        for a in range(n):
            mine(a).start()
            for cp in own(a):
                cp.start()
    elif phase == 1:
        for j, chip in enumerate(chips):
            for a in range(n):
                copy(a, 1 + j, (*chip, pc), me).wait_recv()
                passed(a, j).start()
    else:
        for a in range(n):
            copy(a, 0, sibling, me).wait_recv()
            for j, chip in enumerate(chips):
                copy(a, 4 + j, (*chip, 1 - pc), me).wait_recv()
            for cp in own(a) + [passed(a, j) for j in range(len(chips))]:
                cp.wait_send()
            mine(a).wait()


def _exchange(sends, *, name):
    n = len(sends)

    def body(*refs):
        _a2a_start(refs[:n], refs[n:2 * n], refs[2 * n:])
        _a2a_wait(refs[:n], refs[n:2 * n], refs[2 * n:])

    return pl.pallas_call(
        body, name=name,
        out_shape=_a2a_out_shapes("exchange", sends), in_specs=[ANY] * n, out_specs=[ANY] * n,
        scratch_shapes=_a2a_sems(n),
    )(*sends)


def _a2a_out_shapes(kind, arrays):
    if kind == "gather":
        return [jax.ShapeDtypeStruct((N_DEV,) + x.shape, x.dtype) for x in arrays]
    return [jax.ShapeDtypeStruct(x.shape, x.dtype) for x in arrays]


def _a2a_sems(n):
    return [pltpu.SemaphoreType.DMA((n, N_DEV - 1)), pltpu.SemaphoreType.DMA((n, N_DEV - 1)),
            pltpu.SemaphoreType.DMA((n,))]


def _a2a_copies(in_refs, out_refs, send_sems, recv_sems, local_sems, arrivals):
    px, py, pc = lax.axis_index("x"), lax.axis_index("y"), lax.axis_index("c")
    me = 4 * px + 2 * py + pc
    n = len(in_refs)
    src = lambda a, p: in_refs[a].at[p]
    local = [pltpu.make_async_copy(src(a, me), out_refs[a].at[me], local_sems.at[a]) for a in range(n)]
    sent, landing = [], []
    for k in range(1, N_DEV):
        qx = 1 - px if k & 4 else px
        qy = 1 - py if k & 2 else py
        qc = 1 - pc if k & 1 else pc
        peer = 4 * qx + 2 * qy + qc
        for a in range(n):
            def rdma(dst_slot):
                return pltpu.make_async_remote_copy(
                    src_ref=src(a, peer), dst_ref=out_refs[a].at[dst_slot],
                    send_sem=send_sems.at[a, k - 1], recv_sem=recv_sems.at[a, k - 1],
                    device_id=(qx, qy, qc), device_id_type=MESH)
            sent.append(rdma(me))
            if arrivals:
                landing.append(rdma(peer))
    return local, sent, landing


def _a2a_start(in_refs, out_refs, sems):
    local, sent, _ = _a2a_copies(in_refs, out_refs, *sems, arrivals=False)
    for cp in local + sent:
        cp.start()


def _a2a_wait(in_refs, out_refs, sems):
    local, sent, landing = _a2a_copies(in_refs, out_refs, *sems, arrivals=True)
    for cp in landing:
        cp.wait_recv()
    for cp in sent:
        cp.wait_send()
    for cp in local:
        cp.wait()


IN_POOL, IN_QKV, IN_F, IN_Z, IN_XBC, IN_DT, IN_GATE, IN_TOTAL = 0, 512, 2048, 2056, 3080, 4616, 4632, 7704
QKV_W = IN_F - IN_QKV
REST_Z_BLK, REST_POOL_BLK, REST_XBC0, REST_FDT_BLK = 3, 8, 4608, 48
REST_FDT0 = REST_FDT_BLK * LANE
REST_USED = REST_FDT0 + (IN_Z - IN_F) + (IN_GATE - IN_DT)
REST_W = REST_FDT0 + LANE


def _w_in_runs(shard):
    segs = [(0, IN_QKV, IN_F, 0), (1, IN_GATE, IN_TOTAL, 0), (1, IN_Z, IN_XBC, 3072), (1, IN_POOL, IN_QKV, 4096),
            (1, IN_XBC, IN_DT, REST_XBC0), (1, IN_F, IN_Z, REST_FDT0), (1, IN_DT, IN_GATE, REST_FDT0 + IN_Z - IN_F)]
    runs = []
    for dst, t0, t1, d0 in segs:
        for j in range(N_DEV):
            lo, hi = max(t0, shard * j), min(t1, shard * (j + 1))
            if lo < hi:
                runs.append((dst, d0 + lo - t0, j, lo - shard * j, hi - lo))
    return runs


def _repack_w_in(g, shard, *, name):
    _, L, K, P = g.shape
    tr = 256
    runs = _w_in_runs(shard)

    def body(g_ref, q_ref, r_ref):
        r_ref[:, REST_USED:] = jnp.zeros((tr, REST_W - REST_USED), g_ref.dtype)
        for dst, b, j, a, n in runs:
            (q_ref, r_ref)[dst][:, b:b + n] = g_ref[j, :, a:a + n]

    return pl.pallas_call(
        body, name=name, grid=(L, K // tr),
        in_specs=[pl.BlockSpec((N_DEV, None, tr, P), lambda l, i: (0, l, i, 0))],
        out_specs=[pl.BlockSpec((None, tr, QKV_W), lambda l, i: (l, i, 0)),
                   pl.BlockSpec((None, tr, REST_W), lambda l, i: (l, i, 0))],
        out_shape=[jax.ShapeDtypeStruct((L, K, QKV_W), g.dtype), jax.ShapeDtypeStruct((L, K, REST_W), g.dtype)],
        compiler_params=_cparams("parallel", "parallel"),
    )(g)


def _held_index(l, li, i, n):
    return jnp.where(l == li, i, jnp.where(l > li, n - 1, 0))


def _repack_dw_in(dwq, dwr, shard, *, name):
    L = len(dwq)
    K = dwq[0].shape[0]
    tr = 128
    nt = K // tr
    runs = _w_in_runs(shard)

    def body(*refs):
        srcs, o_ref = refs[:2 * L], refs[2 * L]
        l = pl.program_id(0)
        for li in range(L):
            @pl.when(l == li)
            def _():
                for dst, b, j, a, n in runs:
                    o_ref[j, :, a:a + n] = srcs[2 * li + dst][:, b:b + n]

    in_specs = []
    for li in range(L):
        hold = functools.partial(lambda li, l, i: (_held_index(l, li, i, nt), 0), li)
        in_specs += [pl.BlockSpec((tr, QKV_W), hold), pl.BlockSpec((tr, REST_W), hold)]
    args = [x for pair in zip(dwq, dwr) for x in pair]
    return pl.pallas_call(
        body, name=name, grid=(L, nt),
        in_specs=in_specs,
        out_specs=pl.BlockSpec((N_DEV, None, tr, shard), lambda l, i: (0, l, i, 0)),
        out_shape=jax.ShapeDtypeStruct((N_DEV, L, K, shard), dwq[0].dtype),
        compiler_params=_cparams("arbitrary", "arbitrary"),
    )(*args)


FLAT_W = 1024
ADAM_ROWS = 128


def _adam_math(w, g, m, v):
    m2 = ADAM_B1 * m + (1.0 - ADAM_B1) * g
    v2 = ADAM_B2 * v + (1.0 - ADAM_B2) * (g * g)
    m_hat = m2 / (1.0 - ADAM_B1 ** ADAM_STEP)
    v_hat = v2 / (1.0 - ADAM_B2 ** ADAM_STEP)
    delta = -ADAM_LR * (m_hat / (jnp.sqrt(v_hat) + ADAM_EPS) + ADAM_WD * w)
    return delta, m2, v2


def _row_div(R, align):
    best = R
    for t in range(align, min(R, ADAM_ROWS) + 1, align):
        if R % t == 0:
            best = t
    return best


def _sum8(ref_of):
    g = ref_of(0).astype(F32)
    for s in range(1, N_DEV):
        g = g + ref_of(s).astype(F32)
    return g


def _sum_adam(recvs, w, m, v, *, name):
    L, R, C = w.shape
    single = not isinstance(recvs, (list, tuple))
    recvs = [recvs] if single else list(recvs)
    tr = _row_div(R, SUBLANE * (4 // recvs[0].dtype.itemsize))
    nt = R // tr
    nr = len(recvs)

    def body(*refs):
        r_refs = refs[:nr]
        w_ref, m_ref, v_ref, g_ref, d_ref, m2_ref, v2_ref = refs[nr:]

        def run(ref_of):
            g = _sum8(ref_of)
            delta, m2, v2 = _adam_math(w_ref[...], g, m_ref[...], v_ref[...])
            g_ref[...] = g
            d_ref[...] = delta
            m2_ref[...] = m2
            v2_ref[...] = v2

        if single:
            run(lambda s: r_refs[0][s])
        else:
            l = pl.program_id(0)
            for li in range(L):
                @pl.when(l == li)
                def _():
                    run(lambda s: r_refs[li][s])

    if single:
        r_specs = [pl.BlockSpec((N_DEV, None, tr, C), lambda l, i: (0, l, i, 0))]
    else:
        r_specs = [pl.BlockSpec((N_DEV, tr, C), functools.partial(lambda li, l, i: (0, _held_index(l, li, i, nt), 0), li))
                   for li in range(L)]
    row = pl.BlockSpec((None, tr, C), lambda l, i: (l, i, 0))
    return pl.pallas_call(
        body, name=name, grid=(L, nt),
        in_specs=r_specs + [row, row, row],
        out_specs=[row] * 4,
        out_shape=[jax.ShapeDtypeStruct((L, R, C), F32)] * 4,
        compiler_params=_cparams("arbitrary", "arbitrary"),
    )(*recvs, w, m, v)


def _sum_parts(parts, *, name):
    _, R, C = parts.shape

    def body(p_ref, o_ref):
        o_ref[...] = _sum8(lambda s: p_ref[s])

    return pl.pallas_call(
        body, name=name, out_shape=jax.ShapeDtypeStruct((R, C), F32),
        in_specs=[pl.BlockSpec(memory_space=pltpu.VMEM)], out_specs=pl.BlockSpec(memory_space=pltpu.VMEM),
    )(parts)


def _adam(g, w, m, v, *, name):
    def body(g_ref, w_ref, m_ref, v_ref, d_ref, m2_ref, v2_ref):
        delta, m2, v2 = _adam_math(w_ref[...], g_ref[...], m_ref[...], v_ref[...])
        d_ref[...] = delta
        m2_ref[...] = m2
        v2_ref[...] = v2

    whole = pl.BlockSpec(memory_space=pltpu.VMEM)
    return pl.pallas_call(
        body, name=name, in_specs=[whole] * 4, out_specs=[whole] * 3,
        out_shape=[jax.ShapeDtypeStruct(g.shape, F32)] * 3,
    )(g, w, m, v)


SHARDED = ("w_in", "ffn_up", "p_pool", "p_attn", "p_ssd", "w_out", "ffn_down", "ssd_conv_w", "ffn_conv_w")
REPLICATED = ("norm_mix", "pool_mix", "pool_scale", "f_bias", "ssd_conv_b", "ssd_dt_bias", "ssd_a_log", "ssd_d",
              "ssd_norm", "norm_ffn", "ffn_conv_b", "norm_final")
WEIGHTS = ("norm_mix", "w_in", "pool_mix", "pool_scale", "f_bias", "ssd_conv_w", "ssd_conv_b", "ssd_dt_bias",
           "ssd_a_log", "ssd_d", "ssd_norm", "p_pool", "p_attn", "p_ssd", "w_out", "norm_ffn", "ffn_up",
           "ffn_conv_w", "ffn_conv_b", "ffn_down", "norm_final")


def _round_up(n, k):
    return -(-n // k) * k


def _pad_last(x, width):
    return jnp.pad(x, [(0, 0)] * (x.ndim - 1) + [(0, width - x.shape[-1])])


def _flat_rows(parts, rows):
    flat = jnp.concatenate([p.reshape(-1) for p in parts])
    return jnp.pad(flat, (0, rows * FLAT_W - flat.shape[0])).reshape(rows, FLAT_W)


def _lanes(v, lane0):
    return jnp.pad(v, (lane0, LANE - lane0 - v.shape[0]))[None]


def _cols_by_device(g):
    _, L, R, c = g.shape
    return jnp.moveaxis(g, 0, 2).reshape(L, R, N_DEV * c)


def _rows_by_device(g):
    _, L, r, C = g.shape
    return jnp.moveaxis(g, 0, 1).reshape(L, N_DEV * r, C)


def _gather_payload(W, l, shard_pad, up_pad):
    bf = lambda n: W[n][l].astype(BF16)
    return [_pad_last(bf("w_in"), shard_pad), _pad_last(bf("ffn_up"), up_pad), bf("p_pool"), bf("p_attn"),
            bf("p_ssd"), bf("w_out"), bf("ffn_down"), W["ssd_conv_w"][l], W["ffn_conv_w"][l]]


def _proj_params(g_win, W, l, shard):
    w_qkv, w_rest = _repack_w_in(g_win[:, None], shard, name=f"l{l}_repack_w_in")
    return dict(nm=W["norm_mix"][l][None], w_qkv=w_qkv[0], w_rest=w_rest[0], fb=_lanes(W["f_bias"][l], 0), shard=shard)


def _mixer_params(gathered, W, l, up_w, up_pad):
    g_up, g_pp, g_pa, g_ps, g_wo, g_dn, g_cw, g_fcw = [g[:, None] for g in gathered]
    half = N_DEV // 2
    w_down = _rows_by_device(g_dn)[0]
    w_down_p = jnp.pad(w_down.reshape(half, up_w, -1), ((0, 0), (0, up_pad - up_w), (0, 0))).reshape(half * up_pad, -1)
    fcb_p = _pad_last(W["ffn_conv_b"][l].reshape(N_DEV, up_w), up_pad).reshape(1, N_DEV * up_pad)
    return dict(
        mix=W["pool_mix"][l].astype(BF16), pscale=W["pool_scale"][l][None], cw=_cols_by_device(g_cw)[0],
        cb=W["ssd_conv_b"][l][None], dtb=_lanes(W["ssd_dt_bias"][l], DT_LANE0),
        alog=_lanes(W["ssd_a_log"][l], DT_LANE0), dsk=_lanes(W["ssd_d"][l], DT_LANE0), snw=W["ssd_norm"][l][None],
        p_pool=_cols_by_device(g_pp)[0], p_attn=_cols_by_device(g_pa)[0], p_ssd=_rows_by_device(g_ps)[0],
        w_out=_rows_by_device(g_wo)[0], nf=W["norm_ffn"][l][None], g_up=g_up, up_blocks=(N_DEV, up_w, up_pad),
        fcw=_cols_by_device(_pad_last(g_fcw, up_pad))[0], fcb=fcb_p, w_down=w_down_p)


def _layer_fwd(x, W, l, g_win, g_mixer, to_gather, dims):
    nm = lambda s: f"l{l}_{s}"
    P = _proj_params(g_win, W, l, dims[0])
    u = _rms_fwd(x, P["nm"], name=nm("rms_mix"))
    qkv = _mm(u, P["w_qkv"], out_dtype=BF16, name=nm("proj_qkv"))
    rest = _mm(u, P["w_rest"], name=nm("proj_rest"))
    ccol = _fox_pre(rest, REST_FDT_BLK, P["fb"], name=nm("fox_pre"))
    o, lrow, got = _attn_fwd(qkv, ccol, comm=("gather", to_gather) if to_gather else None, name=nm("attn"))
    if g_mixer is None:
        g_mixer, got = got[:len(SHARDED) - 1], got[len(SHARDED) - 1:]
    P.update(_mixer_params(g_mixer, W, l, *dims[1:]))
    d, ya0 = _pool_fwd(rest, REST_POOL_BLK, P["mix"], P["pscale"], name=nm("pool"))
    ya = _mm(ya0, P["p_pool"], name=nm("p_pool"))
    yb = _mm(o, P["p_attn"], name=nm("p_attn"))
    xc = _conv_silu_fwd(rest, REST_XBC0, SSD_CH, P["cw"], P["cb"], name=nm("ssd_conv"))
    y, hs = _ssd_fwd(xc, rest, REST_FDT_BLK, P["dtb"], P["alog"], P["dsk"], name=nm("ssd"))
    yc0 = _gate_norm_fwd(y, rest, REST_Z_BLK, P["snw"], name=nm("ssd_norm"))
    yc = _mm(yc0, P["p_ssd"], name=nm("p_ssd"))
    merged = _merge_fwd(rest, ya, yb, yc, name=nm("merge"))
    x1 = _mm(merged, P["w_out"], add=x, name=nm("w_out"))
    u2 = _rms_fwd(x1, P["nf"], name=nm("rms_ffn"))
    h0 = _mm_up_fwd(u2, P["g_up"], 0, name=nm("ffn_up"))
    a = _ffn_act_fwd(h0, P["fcw"], P["fcb"], name=nm("ffn_act"))
    x2 = _mm(a, P["w_down"], add=x1, name=nm("ffn_down"))
    saved = dict(x=x, u=u, qkv=qkv, rest=rest, d=d, ya0=ya0, ya=ya, ccol=ccol, o=o, lrow=lrow, yb=yb,
                 xc=xc, y=y, hs=hs, yc0=yc0, yc=yc, merged=merged, x1=x1, u2=u2, h0=h0, a=a)
    return x2, saved, P, got


EARLY_GRADS = ("ffn_up", "ffn_down", "ffn_conv_w", "p_pool", "p_attn", "p_ssd", "w_out")
LATE_GRADS = ("w_in", "ssd_conv_w")


def _layer_bwd(g, gb, sv, P, l, pending, last):
    nm = lambda s: f"l{l}_{s}_bwd"
    rest = sv["rest"]
    G = {}
    up_blocks, up_w, up_pad = P["up_blocks"]
    half = up_blocks // 2
    da = _mm(gb, P["w_down"], tb=True, name=nm("ffn_down_dx"))
    dwd = _mm(sv["a"], gb, ta=True, out_dtype=GRAD_WIRE, name=nm("ffn_down_dw"))
    G["ffn_down"] = dwd.reshape(half, up_pad, -1)[:, :up_w].reshape(N_DEV, -1, dwd.shape[1])
    dh, dfcw, dfcb = _ffn_act_bwd(sv["h0"], P["fcw"], P["fcb"], da, name=nm("ffn_act"))
    du2 = _mm_up_dx(dh, P["g_up"], 0, name=nm("ffn_up_dx"))
    G["ffn_up"] = _mm_up_dw(sv["u2"], dh, up_blocks, up_w, name=nm("ffn_up_dw"))
    taps = dfcw.shape[1]
    G["ffn_conv_w"] = jnp.moveaxis(dfcw.reshape(2, taps, half, up_pad)[..., :up_w], 2, 1).reshape(N_DEV, taps, up_w)
    G["ffn_conv_b"] = dfcb.reshape(2, half, up_pad)[..., :up_w].reshape(-1)
    dx1, dx1b, dnf = _rms_bwd(sv["x1"], P["nf"], du2, g, name=nm("rms_ffn"))
    G["norm_ffn"] = dnf[0]
    dmerged = _mm(dx1b, P["w_out"], tb=True, name=nm("w_out_dx"))
    dwo = _mm(sv["merged"], dx1b, ta=True, out_dtype=GRAD_WIRE, name=nm("w_out_dw"))
    G["w_out"] = dwo.reshape(N_DEV, -1, dwo.shape[1])
    dgl, dya, dyb, dyc = _merge_bwd(rest, sv["ya"], sv["yb"], sv["yc"], dmerged, name=nm("merge"))
    col_blocks = lambda dw: jnp.moveaxis(dw.reshape(dw.shape[0], N_DEV, -1), 1, 0)
    G["p_pool"] = col_blocks(_mm(sv["ya0"], dya, ta=True, out_dtype=GRAD_WIRE, name=nm("p_pool_dw")))
    dya0 = _mm(dya, P["p_pool"], tb=True, name=nm("p_pool_dx"))
    G["p_attn"] = col_blocks(_mm(sv["o"], dyb, ta=True, out_dtype=GRAD_WIRE, name=nm("p_attn_dw")))
    do = _mm(dyb, P["p_attn"], tb=True, out_dtype=BF16, name=nm("p_attn_dx"))
    dps = _mm(sv["yc0"], dyc, ta=True, out_dtype=GRAD_WIRE, name=nm("p_ssd_dw"))
    G["p_ssd"] = dps.reshape(N_DEV, -1, dps.shape[1])
    dyc0 = _mm(dyc, P["p_ssd"], tb=True, name=nm("p_ssd_dx"))
    dpv, dmix, dpsc = _pool_bwd(dya0, sv["d"], P["mix"], P["pscale"], name=nm("pool"))
    G["pool_mix"], G["pool_scale"] = dmix, dpsc[0]
    carried = [((n, l), G[n]) for n in EARLY_GRADS] + pending
    dq, dk, dv, dcq, dck, got = _attn_bwd(sv["qkv"], sv["o"], do, sv["lrow"], sv["ccol"],
                                          comm=("exchange", [a for _, a in carried]), name=nm("attn"))
    received = {key: r for (key, _), r in zip(carried, got)}
    dy, dz, dsnw = _gate_norm_bwd(sv["y"], rest, REST_Z_BLK, P["snw"], dyc0, name=nm("ssd_norm"))
    G["ssd_norm"] = dsnw[0]
    dxc, ddt, dpar = _ssd_bwd(sv["xc"], rest, REST_FDT_BLK, P["dtb"], P["alog"], P["dsk"], sv["hs"], dy, name=nm("ssd"))
    G["ssd_a_log"] = dpar[0, DT_LANE0:DT_LANE0 + SSD_HEADS]
    G["ssd_d"] = dpar[1, DT_LANE0:DT_LANE0 + SSD_HEADS]
    dxbc, dcw, dcb = _conv_silu_bwd(rest, REST_XBC0, SSD_CH, P["cw"], P["cb"], dxc, name=nm("ssd_conv"))
    G["ssd_conv_w"] = jnp.moveaxis(dcw.reshape(dcw.shape[0], N_DEV, -1), 1, 0)
    G["ssd_conv_b"] = dcb[0]
    dfdt, dfdtb = _fox_post(dcq, dck, rest, REST_FDT_BLK, P["fb"], ddt, name=nm("fox_post"))
    G["f_bias"] = dfdtb[0, :ATTN_HEADS]
    G["ssd_dt_bias"] = dfdtb[0, DT_LANE0:DT_LANE0 + SSD_HEADS]
    dqkv = jnp.concatenate([dq, dk, dv], axis=1)
    drest = jnp.concatenate([dgl, dz, dpv, dxbc, dfdt], axis=1)
    dwq = _mm(sv["u"], dqkv, ta=True, out_dtype=GRAD_WIRE, name=nm("proj_qkv_dw"))
    dwr = _mm(sv["u"], drest, ta=True, out_dtype=GRAD_WIRE, name=nm("proj_rest_dw"))
    G["w_in"] = _repack_dw_in([dwq], [dwr], P["shard"], name=nm("repack_dw_in"))[:, 0]
    du = _mm(dqkv, P["w_qkv"], tb=True, name=nm("proj_qkv_dx"))
    late = [((n, l), G[n]) for n in LATE_GRADS]
    if last:
        du, got = _mm(drest, P["w_rest"], tb=True, add=du, comm=("exchange", [a for _, a in late]),
                      name=nm("proj_rest_dx"))
        received.update({key: r for (key, _), r in zip(late, got)})
        late = []
    else:
        du = _mm(drest, P["w_rest"], tb=True, add=du, name=nm("proj_rest_dx"))
    dx, dxb, dnm = _rms_bwd(sv["x"], P["nm"], du, dx1, name=nm("rms_mix"))
    G["norm_mix"] = dnm[0]
    return dx, dxb, G, received, late


def kernel(x, norm_mix, w_in, pool_mix, pool_scale, f_bias, ssd_conv_w, ssd_conv_b, ssd_dt_bias, ssd_a_log, ssd_d, ssd_norm, p_pool, p_attn, p_ssd, w_out, norm_ffn, ffn_up, ffn_conv_w, ffn_conv_b, ffn_down, norm_final, loss_target, m_norm_mix, m_w_in, m_pool_mix, m_pool_scale, m_f_bias, m_ssd_conv_w, m_ssd_conv_b, m_ssd_dt_bias, m_ssd_a_log, m_ssd_d, m_ssd_norm, m_p_pool, m_p_attn, m_p_ssd, m_w_out, m_norm_ffn, m_ffn_up, m_ffn_conv_w, m_ffn_conv_b, m_ffn_down, m_norm_final, v_norm_mix, v_w_in, v_pool_mix, v_pool_scale, v_f_bias, v_ssd_conv_w, v_ssd_conv_b, v_ssd_dt_bias, v_ssd_a_log, v_ssd_d, v_ssd_norm, v_p_pool, v_p_attn, v_p_ssd, v_w_out, v_norm_ffn, v_ffn_up, v_ffn_conv_w, v_ffn_conv_b, v_ffn_down, v_norm_final):
    args = dict(locals())
    W = {n: args[n] for n in WEIGHTS}
    M = {n: args["m_" + n] for n in WEIGHTS}
    V = {n: args["v_" + n] for n in WEIGHTS}
    L = norm_mix.shape[0]
    shard = w_in.shape[-1]
    up_w = ffn_up.shape[-1]
    up_pad = _round_up(up_w, LANE)

    payload = lambda l: _gather_payload(W, l, _round_up(shard, LANE), up_pad)
    first = payload(0)
    g_win = _all_gather(first[:1], name="gather_w_in_l0")[0]
    g_mixer, to_gather = None, first[1:]
    h, saves, params = x[0], [], []
    for l in range(L):
        to_gather = to_gather + (payload(l + 1) if l + 1 < L else [])
        h, sv, P, got = _layer_fwd(h, W, l, g_win, g_mixer, to_gather, (shard, up_w, up_pad))
        saves.append(sv)
        params.append(P)
        if l + 1 < L:
            g_win, g_mixer, to_gather = got[0], got[1:], []
    loss_part, g, gb, dnfin = _loss_head(h, norm_final[None], loss_target[0], name="loss_head")

    G, received, pending = [None] * L, {}, []
    for l in reversed(range(L)):
        g, gb, G[l], got, pending = _layer_bwd(g, gb, saves[l], params[l], l, pending, last=l == 0)
        received.update(got)

    rep_grads = {n: jnp.stack([G[l][n] for l in range(L)]) for n in REPLICATED if n != "norm_final"}
    rep_grads["norm_final"] = dnfin[0]
    rep_size = sum(W[n].size for n in REPLICATED)
    rep_rows = _round_up(-(-rep_size // (N_DEV * FLAT_W)), SUBLANE)
    rep_send = _flat_rows([rep_grads[n] for n in REPLICATED], N_DEV * rep_rows).reshape(N_DEV, rep_rows, FLAT_W)
    last = _exchange([rep_send], name="exchange_replicated_grads")

    out = {}
    for n in SHARDED:
        shape = W[n].shape
        as3 = lambda a: a.reshape(L, -1, shape[-1])
        res = _sum_adam([received[(n, l)] for l in range(L)], as3(W[n]), as3(M[n]), as3(V[n]), name="sum_adam_" + n)
        for key, arr in zip(("grad_", "delta_", "new_m_", "new_v_"), res):
            out[key + n] = arr.reshape(shape)

    rep_sum = _sum_parts(last[-1], name="sum_replicated")
    rep_g = _all_gather([rep_sum], name="gather_replicated_grads")[0].reshape(N_DEV * rep_rows, FLAT_W)
    flat_rep = lambda D: _flat_rows([D[n] for n in REPLICATED], N_DEV * rep_rows)
    d_rp, m_rp, v_rp = _adam(rep_g, flat_rep(W), flat_rep(M), flat_rep(V), name="adam_replicated")
    off = 0
    for n in REPLICATED:
        size = W[n].size
        for key, arr in (("grad_", rep_g), ("delta_", d_rp), ("new_m_", m_rp), ("new_v_", v_rp)):
            out[key + n] = arr.reshape(-1)[off:off + size].reshape(W[n].shape)
        off += size

    loss = lax.psum(loss_part[0, 0], ("x", "y", "c"))
    return (loss, g[None], *[out["grad_" + n] for n in WEIGHTS], *[out["delta_" + n] for n in WEIGHTS],
            *[out["new_m_" + n] for n in WEIGHTS], *[out["new_v_" + n] for n in WEIGHTS])
```

```python
import functools

import jax
import jax.numpy as jnp
from jax import lax
from jax.experimental import pallas as pl
from jax.experimental.pallas import tpu as pltpu

F32 = jnp.float32
BF16 = jnp.bfloat16

LANE = 128
SUBLANE = 8
VMEM_LIMIT = 56 * 1024 * 1024

NORM_EPS = 1e-6
POOL_WINDOWS = (2, 4, 8, 16)
POOL_GROUP_DIM = 128
ATTN_HEADS = 8
ATTN_HEAD_DIM = 64
SSD_HEAD_DIM = 64
SSD_HEADS = 16
SSD_GROUPS = 2
SSD_STATE = 128
SSD_CHUNK = 128

ADAM_LR = 0.001
ADAM_B1 = 0.9
ADAM_B2 = 0.999
ADAM_EPS = 1e-08
ADAM_WD = 0.01
ADAM_STEP = 10


def _cparams(*sem):
    return pltpu.CompilerParams(dimension_semantics=tuple(sem), vmem_limit_bytes=VMEM_LIMIT)


def _tile(n, pref):
    if n <= pref:
        return n
    assert n % LANE == 0, n
    q = n // LANE
    best = 1
    for d in range(1, pref // LANE + 1):
        if q % d == 0:
            best = d
    return best * LANE


GRAD_WIRE = BF16


def _mm_call(args, in_specs, out_spec, out_shape, grid, *, ta, tb, nk, acc_shape, has_add, keep, name, comm=None):
    dims = (((0 if ta else 1,), (1 if tb else 0,)), ((), ()))
    nc = len(comm[1]) if comm else 0
    n_in = 2 + has_add

    def body(*refs):
        a_ref, b_ref = refs[0], refs[1]
        o_ref = refs[n_in + nc]
        acc_ref = refs[n_in + 2 * nc + 1] if nk > 1 else None
        carried = (refs[n_in:n_in + nc], refs[n_in + nc + 1:n_in + 2 * nc + 1], refs[n_in + 2 * nc + 1 + (nk > 1):])
        i, j, k = pl.program_id(0), pl.program_id(1), pl.program_id(2)
        if comm:
            _carry_start(comm[0], *carried, jnp.logical_and(jnp.logical_and(i == 0, j == 0), k == 0))
        d = lax.dot_general(a_ref[...], b_ref[...], dims, preferred_element_type=F32)

        def finish(r):
            if has_add:
                r = r + refs[2][...]
            if keep is not None:
                r = r[:, :keep]
            o_ref[...] = r.astype(o_ref.dtype)

        if nk == 1:
            finish(d)
        else:
            @pl.when(k == 0)
            def _():
                acc_ref[...] = d

            @pl.when(jnp.logical_and(k > 0, k < nk - 1))
            def _():
                acc_ref[...] += d

            @pl.when(k == nk - 1)
            def _():
                finish(acc_ref[...] + d)

        if comm:
            _carry_wait(comm[0], *carried, jnp.logical_and(jnp.logical_and(i == grid[0] - 1, j == grid[1] - 1),
                                                           k == nk - 1))

    res = pl.pallas_call(
        body, name=name, grid=grid, in_specs=in_specs + [ANY] * nc, out_specs=[out_spec] + [ANY] * nc,
        out_shape=[out_shape] + (_a2a_out_shapes(*comm) if comm else []),
        scratch_shapes=([pltpu.VMEM(acc_shape, F32)] if nk > 1 else []) + (_a2a_sems(nc) if comm else []),
        compiler_params=_cparams(*(("arbitrary",) * 3 if comm else ("parallel", "parallel", "arbitrary"))),
    )(*args, *(comm[1] if comm else []))
    return (res[0], list(res[1:])) if comm else res[0]


MM_VMEM_BUDGET = 40 * 1024 * 1024
MM_MAX_ROWS = 2048


MM_MIN_ROWS_WHOLE_K = 256


def _mm_rows(M, tn, tk, nk, out_bytes, has_add):
    best = None
    for tm in range(LANE, min(M, MM_MAX_ROWS) + 1, LANE):
        if M % tm:
            continue
        blocks = 2 * (tm * tk * 2 + tk * tn * 2 + tm * tn * out_bytes + (tm * tn * 4 if has_add else 0))
        blocks += tm * tn * 4 * (2 if nk > 1 else 1)
        if blocks <= MM_VMEM_BUDGET:
            best = tm
    return M if best is None else best


def _mm_tiles(M, N, K, out_bytes, has_add, whole_k_ok):
    tn = _tile(N, 1024)
    if whole_k_ok and K > 2048 and M % MM_MIN_ROWS_WHOLE_K == 0:
        tm = _mm_rows(M, tn, K, 1, out_bytes, has_add)
        need = 2 * (tm * K * 2 + K * tn * 2 + tm * tn * out_bytes + (tm * tn * 4 if has_add else 0)) + tm * tn * 4
        if tm >= MM_MIN_ROWS_WHOLE_K and need <= MM_VMEM_BUDGET:
            return tm, tn, K
    tk = _tile(K, 2048)
    return _mm_rows(M, tn, tk, K // tk, out_bytes, has_add), tn, tk


def _mm(a, b, *, ta=False, tb=False, add=None, out_dtype=F32, comm=None, name):
    assert a.dtype == BF16 and b.dtype == BF16, (a.dtype, b.dtype)
    if ta:
        K, M = a.shape
    else:
        M, K = a.shape
    if tb:
        N, K2 = b.shape
    else:
        K2, N = b.shape
    assert K == K2, (a.shape, b.shape, ta, tb)
    tm, tn, tk = _mm_tiles(M, N, K, jnp.dtype(out_dtype).itemsize, add is not None, whole_k_ok=not ta)
    a_spec = pl.BlockSpec((tk, tm), lambda i, j, k: (k, i)) if ta else pl.BlockSpec((tm, tk), lambda i, j, k: (i, k))
    b_spec = pl.BlockSpec((tn, tk), lambda i, j, k: (j, k)) if tb else pl.BlockSpec((tk, tn), lambda i, j, k: (k, j))
    in_specs, args = [a_spec, b_spec], [a, b]
    if add is not None:
        in_specs.append(pl.BlockSpec((tm, tn), lambda i, j, k: (i, j)))
        args.append(add)
    return _mm_call(args, in_specs, pl.BlockSpec((tm, tn), lambda i, j, k: (i, j)),
                    jax.ShapeDtypeStruct((M, N), out_dtype), (M // tm, N // tn, K // tk),
                    ta=ta, tb=tb, nk=K // tk, acc_shape=(tm, tn), has_add=add is not None, keep=None, name=name,
                    comm=comm)


def _half_of(b, half):
    hi = jnp.where(b >= half, 1, 0)
    return hi, b - half * hi


def _mm_up_fwd(u, g_up, l, *, name):
    S, K = u.shape
    nblk, _, _, bw = g_up.shape
    tm = _mm_rows(S, bw, K, 1, 4, False)
    return _mm_call([u, g_up],
                    [pl.BlockSpec((tm, K), lambda i, j, k: (i, 0)),
                     pl.BlockSpec((None, None, K, bw), lambda i, j, k: (j, l, 0, 0))],
                    pl.BlockSpec((tm, bw), lambda i, j, k: (i, j)),
                    jax.ShapeDtypeStruct((S, nblk * bw), F32), (S // tm, nblk, 1),
                    ta=False, tb=False, nk=1, acc_shape=(tm, bw), has_add=False, keep=None, name=name)


def _mm_up_dx(dh, g_up, l, *, name):
    _, S, F = dh.shape
    nblk, _, N, bw = g_up.shape
    half = nblk // 2
    tm = min(MM_MIN_ROWS_WHOLE_K, S)

    def body(dh_ref, g_ref, o_ref):
        acc = None
        for j in range(nblk):
            a = dh_ref[j // half, :, (j % half) * bw:(j % half + 1) * bw]
            d = lax.dot_general(a, g_ref[j], (((1,), (1,)), ((), ())), preferred_element_type=F32)
            acc = d if acc is None else acc + d
        o_ref[...] = acc

    return pl.pallas_call(
        body, name=name, grid=(S // tm,),
        in_specs=[pl.BlockSpec((2, tm, F), lambda i: (0, i, 0)),
                  pl.BlockSpec((nblk, None, N, bw), lambda i: (0, l, 0, 0))],
        out_specs=pl.BlockSpec((tm, N), lambda i: (i, 0)),
        out_shape=jax.ShapeDtypeStruct((S, N), F32),
        compiler_params=_cparams("parallel"),
    )(dh, g_up)


def _mm_up_dw(u, dh, nblk, keep, *, name):
    S, M = u.shape
    half = nblk // 2
    bw = dh.shape[2] // half
    tk = _tile(S, 2048)
    tm = _mm_rows(M, bw, tk, S // tk, jnp.dtype(GRAD_WIRE).itemsize, False)
    return _mm_call([u, dh],
                    [pl.BlockSpec((tk, tm), lambda i, j, k: (k, i)),
                     pl.BlockSpec((None, tk, bw), lambda i, j, k: (_half_of(j, half)[0], k, _half_of(j, half)[1]))],
                    pl.BlockSpec((None, tm, keep), lambda i, j, k: (j, i, 0)),
                    jax.ShapeDtypeStruct((nblk, M, keep), GRAD_WIRE), (M // tm, nblk, S // tk),
                    ta=True, tb=False, nk=S // tk, acc_shape=(tm, bw), has_add=False, keep=keep, name=name)


ROW_TILE = 512
HALO = 8
POOL_HALO = 16


def _row_tile(S):
    return min(ROW_TILE, S)


def _acc_out(ref, val, first):
    @pl.when(first)
    def _():
        ref[...] = val

    @pl.when(jnp.logical_not(first))
    def _():
        ref[...] += val


def _rms_fwd(x, w, *, name):
    S, D = x.shape
    ts = _row_tile(S)

    def body(x_ref, w_ref, u_ref):
        xv = x_ref[...]
        r = lax.rsqrt(jnp.mean(xv * xv, axis=-1, keepdims=True) + NORM_EPS)
        u_ref[...] = (xv * r * w_ref[...]).astype(BF16)

    return pl.pallas_call(
        body, name=name, grid=(S // ts,),
        in_specs=[pl.BlockSpec((ts, D), lambda i: (i, 0)), pl.BlockSpec((1, D), lambda i: (0, 0))],
        out_specs=pl.BlockSpec((ts, D), lambda i: (i, 0)),
        out_shape=jax.ShapeDtypeStruct((S, D), BF16),
        compiler_params=_cparams("parallel"),
    )(x, w)


def _rms_bwd(x, w, du, g, *, name):
    S, D = x.shape
    ts = _row_tile(S)

    def body(x_ref, w_ref, du_ref, g_ref, dx_ref, dxb_ref, dw_ref):
        xv = x_ref[...]
        r = lax.rsqrt(jnp.mean(xv * xv, axis=-1, keepdims=True) + NORM_EPS)
        n = xv * r
        duv = du_ref[...]
        dn = duv * w_ref[...]
        dx = g_ref[...] + r * (dn - n * jnp.mean(dn * n, axis=-1, keepdims=True))
        dx_ref[...] = dx
        dxb_ref[...] = dx.astype(BF16)
        _acc_out(dw_ref, jnp.sum(duv * n, axis=0, keepdims=True), pl.program_id(0) == 0)

    row = pl.BlockSpec((ts, D), lambda i: (i, 0))
    vec = pl.BlockSpec((1, D), lambda i: (0, 0))
    return pl.pallas_call(
        body, name=name, grid=(S // ts,),
        in_specs=[row, vec, row, row],
        out_specs=[row, row, vec],
        out_shape=[jax.ShapeDtypeStruct((S, D), F32), jax.ShapeDtypeStruct((S, D), BF16),
                   jax.ShapeDtypeStruct((1, D), F32)],
        compiler_params=_cparams("arbitrary"),
    )(x, w, du, g)


def _loss_head(x, w, target, *, name):
    S, D = x.shape
    ts = _row_tile(S)

    def body(x_ref, w_ref, t_ref, loss_ref, dx_ref, dxb_ref, dw_ref):
        xv = x_ref[...]
        wv = w_ref[...]
        r = lax.rsqrt(jnp.mean(xv * xv, axis=-1, keepdims=True) + NORM_EPS)
        n = xv * r
        err = n * wv - t_ref[...]
        part = 0.5 * jnp.sum(jnp.mean(err * err, axis=-1, keepdims=True), axis=0, keepdims=True)
        dy = err * (1.0 / D)
        dn = dy * wv
        dx = r * (dn - n * jnp.mean(dn * n, axis=-1, keepdims=True))
        dx_ref[...] = dx
        dxb_ref[...] = dx.astype(BF16)
        first = pl.program_id(0) == 0
        _acc_out(dw_ref, jnp.sum(dy * n, axis=0, keepdims=True), first)
        _acc_out(loss_ref, jnp.broadcast_to(part, loss_ref.shape), first)

    row = pl.BlockSpec((ts, D), lambda i: (i, 0))
    vec = pl.BlockSpec((1, D), lambda i: (0, 0))
    return pl.pallas_call(
        body, name=name, grid=(S // ts,),
        in_specs=[row, vec, row],
        out_specs=[pl.BlockSpec((SUBLANE, LANE), lambda i: (0, 0)), row, row, vec],
        out_shape=[jax.ShapeDtypeStruct((SUBLANE, LANE), F32), jax.ShapeDtypeStruct((S, D), F32),
                   jax.ShapeDtypeStruct((S, D), BF16), jax.ShapeDtypeStruct((1, D), F32)],
        compiler_params=_cparams("arbitrary"),
    )(x, w, target)


def _sigmoid(x):
    return 1.0 / (1.0 + jnp.exp(-x))


def _merge_fwd(gl, ya, yb, yc, *, name):
    S, D = ya.shape
    ts = _row_tile(S)

    def body(ga_ref, gb_ref, gc_ref, ya_ref, yb_ref, yc_ref, o_ref):
        m = (_sigmoid(ga_ref[...]) * ya_ref[...] + _sigmoid(gb_ref[...]) * yb_ref[...]
             + _sigmoid(gc_ref[...]) * yc_ref[...])
        o_ref[...] = m.astype(BF16)

    row = pl.BlockSpec((ts, D), lambda i: (i, 0))
    gspec = [pl.BlockSpec((ts, D), functools.partial(lambda b, i: (i, b), b)) for b in range(3)]
    return pl.pallas_call(
        body, name=name, grid=(S // ts,),
        in_specs=gspec + [row, row, row],
        out_specs=row,
        out_shape=jax.ShapeDtypeStruct((S, D), BF16),
        compiler_params=_cparams("parallel"),
    )(gl, gl, gl, ya, yb, yc)


def _merge_bwd(gl, ya, yb, yc, dm, *, name):
    S, D = ya.shape
    ts = _row_tile(S)

    def body(ga_ref, gb_ref, gc_ref, ya_ref, yb_ref, yc_ref, dm_ref, dgl_ref, da_ref, db_ref, dc_ref):
        dmv = dm_ref[...]
        for b, (g_ref, y_ref, dy_ref) in enumerate(((ga_ref, ya_ref, da_ref), (gb_ref, yb_ref, db_ref),
                                                     (gc_ref, yc_ref, dc_ref))):
            s = _sigmoid(g_ref[...])
            dy_ref[...] = (dmv * s).astype(BF16)
            dgl_ref[:, b * D:(b + 1) * D] = (dmv * y_ref[...] * s * (1.0 - s)).astype(BF16)

    row = pl.BlockSpec((ts, D), lambda i: (i, 0))
    gspec = [pl.BlockSpec((ts, D), functools.partial(lambda b, i: (i, b), b)) for b in range(3)]
    return pl.pallas_call(
        body, name=name, grid=(S // ts,),
        in_specs=gspec + [row, row, row, row],
        out_specs=[pl.BlockSpec((ts, 3 * D), lambda i: (i, 0)), row, row, row],
        out_shape=[jax.ShapeDtypeStruct((S, 3 * D), BF16)] + [jax.ShapeDtypeStruct((S, D), BF16)] * 3,
        compiler_params=_cparams("parallel"),
    )(gl, gl, gl, ya, yb, yc, dm)


POOL_WIDTH = 512


def _pool_cnt(t, w):
    return jnp.minimum(t + 1, w).astype(F32)


def _pool_fwd(rest, col, mix, scale, *, name):
    S = rest.shape[0]
    W, G, H = POOL_WIDTH, POOL_GROUP_DIM, POOL_HALO
    ts = _row_tile(S)

    def body(v_ref, h_ref, mix_ref, sc_ref, d_ref, y_ref, ext_ref):
        i = pl.program_id(0)
        cur = v_ref[...]
        ext_ref[0:H, :] = jnp.where(i > 0, h_ref[...], 0.0)
        ext_ref[H:, :] = cur
        t = i * ts + lax.broadcasted_iota(jnp.int32, (ts, 1), 0)
        for g, w in enumerate(POOL_WINDOWS):
            cols = slice(g * G, (g + 1) * G)
            acc = cur[:, cols]
            for k in range(1, w):
                acc = acc + ext_ref[pl.ds(H - k, ts), cols]
            db = (acc / _pool_cnt(t, w) - cur[:, cols]).astype(BF16)
            d_ref[:, cols] = db
            y = jnp.dot(db, mix_ref[g], preferred_element_type=F32) * sc_ref[:, cols]
            y_ref[:, cols] = y.astype(BF16)

    row = pl.BlockSpec((ts, W), lambda i: (i, 0))
    return pl.pallas_call(
        body, name=name, grid=(S // ts,),
        in_specs=[pl.BlockSpec((ts, W), lambda i: (i, col)),
                  pl.BlockSpec((H, W), lambda i: (jnp.maximum(i * (ts // H) - 1, 0), col)),
                  pl.BlockSpec((len(POOL_WINDOWS), G, G), lambda i: (0, 0, 0)),
                  pl.BlockSpec((1, W), lambda i: (0, 0))],
        out_specs=[row, row],
        out_shape=[jax.ShapeDtypeStruct((S, W), BF16)] * 2,
        scratch_shapes=[pltpu.VMEM((H + ts, W), F32)],
        compiler_params=_cparams("parallel"),
    )(rest, rest, mix, scale)


def _pool_bwd(dy, d, mix, scale, *, name):
    S = dy.shape[0]
    W, G, H = POOL_WIDTH, POOL_GROUP_DIM, POOL_HALO
    ts = _row_tile(S)
    n = S // ts
    NT = (((1,), (1,)), ((), ()))
    TN = (((0,), (0,)), ((), ()))

    def body(dy_ref, dyn_ref, d_ref, mix_ref, sc_ref, dv_ref, dmix_ref, dsc_ref, ext_ref):
        i = pl.program_id(0)

        @pl.when(i == 0)
        def _():
            dmix_ref[...] = jnp.zeros_like(dmix_ref)
            dsc_ref[...] = jnp.zeros_like(dsc_ref)

        t = i * ts + lax.broadcasted_iota(jnp.int32, (ts, 1), 0)
        tn = (i + 1) * ts + lax.broadcasted_iota(jnp.int32, (H, 1), 0)
        for g, w in enumerate(POOL_WINDOWS):
            cols = slice(g * G, (g + 1) * G)
            sc = sc_ref[:, cols]
            dyv = dy_ref[:, cols]
            db = d_ref[:, cols]
            mg = mix_ref[g]
            yp = jnp.dot(db, mg, preferred_element_type=F32)
            dsc_ref[:, cols] += jnp.sum(dyv * yp, axis=0, keepdims=True)
            dyp = (dyv * sc).astype(BF16)
            dd = lax.dot_general(dyp, mg, NT, preferred_element_type=F32)
            dmix_ref[g] += lax.dot_general(db, dyp, TN, preferred_element_type=F32)
            dyn = (jnp.where(i < n - 1, dyn_ref[:, cols], 0.0) * sc).astype(BF16)
            ddn = lax.dot_general(dyn, mg, NT, preferred_element_type=F32)
            e = dd / _pool_cnt(t, w)
            ext_ref[0:ts, cols] = e
            ext_ref[ts:, cols] = ddn / _pool_cnt(tn, w)
            acc = e
            for k in range(1, w):
                acc = acc + ext_ref[pl.ds(k, ts), cols]
            dv_ref[:, cols] = (acc - dd).astype(BF16)

    row = pl.BlockSpec((ts, W), lambda i: (i, 0))
    return pl.pallas_call(
        body, name=name, grid=(n,),
        in_specs=[row,
                  pl.BlockSpec((H, W), lambda i: (jnp.minimum((i + 1) * (ts // H), S // H - 1), 0)),
                  row,
                  pl.BlockSpec((len(POOL_WINDOWS), G, G), lambda i: (0, 0, 0)),
                  pl.BlockSpec((1, W), lambda i: (0, 0))],
        out_specs=[row, pl.BlockSpec((len(POOL_WINDOWS), G, G), lambda i: (0, 0, 0)),
                   pl.BlockSpec((1, W), lambda i: (0, 0))],
        out_shape=[jax.ShapeDtypeStruct((S, W), BF16),
                   jax.ShapeDtypeStruct((len(POOL_WINDOWS), G, G), F32),
                   jax.ShapeDtypeStruct((1, W), F32)],
        scratch_shapes=[pltpu.VMEM((ts + H, W), F32)],
        compiler_params=_cparams("arbitrary"),
    )(dy, dy, d, mix, scale)


def _conv_taps(ext_ref, w_ref, b_ref, rows, first):
    K = w_ref.shape[0]
    pre = b_ref[...] + w_ref[K - 1:K, :] * ext_ref[pl.ds(first, rows), :]
    for k in range(K - 1):
        pre = pre + w_ref[k:k + 1, :] * ext_ref[pl.ds(first - (K - 1) + k, rows), :]
    return pre


CONV_ROW_TILE = 1024
ROW_CHUNK = 40


def _chunks(rows):
    ch = max(c for c in range(SUBLANE, ROW_CHUNK + 1, SUBLANE) if rows % c == 0)
    return [(r0, ch) for r0 in range(0, rows, ch)]


def _conv_bwd_taps(g_ref, ext_ref, w_ref, dx_store, ts):
    K = w_ref.shape[0]
    db, dws = 0.0, [0.0] * K
    fold = lambda v: jnp.sum(v.reshape(v.shape[0] // SUBLANE, SUBLANE, v.shape[1]), axis=0)
    for r0, ch in _chunks(ts):
        dx = w_ref[K - 1:K, :] * g_ref[pl.ds(r0, ch), :]
        for k in range(K - 1):
            dx = dx + w_ref[k:k + 1, :] * g_ref[pl.ds(r0 + K - 1 - k, ch), :]
        dx_store(r0, ch, dx.astype(BF16))
        gc = g_ref[pl.ds(r0, ch), :]
        db = db + fold(gc)
        for k in range(K):
            dws[k] = dws[k] + fold(gc * ext_ref[pl.ds(HALO - (K - 1) + k + r0, ch), :])
    total = lambda v: jnp.sum(v, axis=0, keepdims=True)
    return total(db), jnp.concatenate([total(d) for d in dws], axis=0)


def _prev_halo_spec(ts, tc, colfn):
    return pl.BlockSpec((HALO, tc), lambda *g: (jnp.maximum(g[-1] * (ts // HALO) - 1, 0), colfn(*g)))


def _conv_silu_fwd(rest, col0, C, w, b, *, name):
    S = rest.shape[0]
    ts, tc = min(CONV_ROW_TILE, S), 512
    assert col0 % tc == 0 and C % tc == 0
    cb = col0 // tc

    def body(x_ref, h_ref, w_ref, b_ref, o_ref, ext_ref):
        i = pl.program_id(1)
        ext_ref[0:HALO, :] = jnp.where(i > 0, h_ref[...], 0.0)
        ext_ref[HALO:, :] = x_ref[...]
        for r0, ch in _chunks(ts):
            pre = _conv_taps(ext_ref, w_ref, b_ref, ch, HALO + r0)
            o_ref[pl.ds(r0, ch), :] = pre * _sigmoid(pre)

    K = w.shape[0]
    return pl.pallas_call(
        body, name=name, grid=(C // tc, S // ts),
        in_specs=[pl.BlockSpec((ts, tc), lambda j, i: (i, cb + j)),
                  _prev_halo_spec(ts, tc, lambda j, i: cb + j),
                  pl.BlockSpec((K, tc), lambda j, i: (0, j)),
                  pl.BlockSpec((1, tc), lambda j, i: (0, j))],
        out_specs=pl.BlockSpec((ts, tc), lambda j, i: (i, j)),
        out_shape=jax.ShapeDtypeStruct((S, C), F32),
        scratch_shapes=[pltpu.VMEM((HALO + ts, tc), F32)],
        compiler_params=_cparams("parallel", "parallel"),
    )(rest, rest, w, b)


def _dsilu(pre):
    s = _sigmoid(pre)
    return s, s * (1.0 + pre * (1.0 - s))


def _conv_silu_bwd(rest, col0, C, w, b, dy, *, name):
    S = rest.shape[0]
    ts, tc = min(CONV_ROW_TILE, S), 512
    cb = col0 // tc
    n = S // ts
    K = w.shape[0]
    R = ts + HALO

    def body(x_ref, hp_ref, hn_ref, w_ref, b_ref, dy_ref, dyn_ref, dx_ref, dw_ref, db_ref, ext_ref, g_ref):
        i = pl.program_id(1)
        last = i == n - 1
        ext_ref[0:HALO, :] = jnp.where(i > 0, hp_ref[...], 0.0)
        ext_ref[HALO:HALO + ts, :] = x_ref[...]
        ext_ref[HALO + ts:, :] = jnp.where(last, 0.0, hn_ref[...])
        g_ref[0:ts, :] = dy_ref[...]
        g_ref[ts:, :] = jnp.where(last, 0.0, dyn_ref[...])
        for r0, ch in _chunks(R):
            pre = _conv_taps(ext_ref, w_ref, b_ref, ch, HALO + r0)
            g_ref[pl.ds(r0, ch), :] = g_ref[pl.ds(r0, ch), :] * _dsilu(pre)[1]

        def dx_store(r0, ch, v):
            dx_ref[pl.ds(r0, ch), :] = v

        db, dw = _conv_bwd_taps(g_ref, ext_ref, w_ref, dx_store, ts)
        _acc_out(db_ref, db, i == 0)
        _acc_out(dw_ref, dw, i == 0)

    nxt = lambda cf: pl.BlockSpec((HALO, tc), lambda j, i: (jnp.minimum((i + 1) * (ts // HALO), S // HALO - 1), cf(j)))
    return pl.pallas_call(
        body, name=name, grid=(C // tc, n),
        in_specs=[pl.BlockSpec((ts, tc), lambda j, i: (i, cb + j)),
                  _prev_halo_spec(ts, tc, lambda j, i: cb + j),
                  nxt(lambda j: cb + j),
                  pl.BlockSpec((K, tc), lambda j, i: (0, j)),
                  pl.BlockSpec((1, tc), lambda j, i: (0, j)),
                  pl.BlockSpec((ts, tc), lambda j, i: (i, j)),
                  nxt(lambda j: j)],
        out_specs=[pl.BlockSpec((ts, tc), lambda j, i: (i, j)),
                   pl.BlockSpec((K, tc), lambda j, i: (0, j)),
                   pl.BlockSpec((1, tc), lambda j, i: (0, j))],
        out_shape=[jax.ShapeDtypeStruct((S, C), BF16), jax.ShapeDtypeStruct((K, C), F32),
                   jax.ShapeDtypeStruct((1, C), F32)],
        scratch_shapes=[pltpu.VMEM((HALO + R, tc), F32), pltpu.VMEM((R, tc), F32)],
        compiler_params=_cparams("parallel", "arbitrary"),
    )(rest, rest, rest, w, b, dy, dy)


FFN_TC = 512


def _ffn_act_fwd(h0, w, b, *, name):
    S, F2 = h0.shape
    F = F2 // 2
    ts, tc = min(CONV_ROW_TILE, S), FFN_TC
    nb = F // tc
    K = w.shape[0]

    def body(xg_ref, hg_ref, xv_ref, hv_ref, wg_ref, wv_ref, bg_ref, bv_ref, o_ref, eg_ref, ev_ref):
        i = pl.program_id(1)
        for x_ref, h_ref, e_ref in ((xg_ref, hg_ref, eg_ref), (xv_ref, hv_ref, ev_ref)):
            e_ref[0:HALO, :] = jnp.where(i > 0, h_ref[...], 0.0)
            e_ref[HALO:, :] = x_ref[...]
        for r0, ch in _chunks(ts):
            pg = _conv_taps(eg_ref, wg_ref, bg_ref, ch, HALO + r0)
            pv = _conv_taps(ev_ref, wv_ref, bv_ref, ch, HALO + r0)
            o_ref[pl.ds(r0, ch), :] = (pg * _sigmoid(pg) * pv).astype(BF16)

    def half(off):
        return [pl.BlockSpec((ts, tc), lambda j, i: (i, off + j)), _prev_halo_spec(ts, tc, lambda j, i: off + j)]

    wspec = lambda off: pl.BlockSpec((K, tc), lambda j, i: (0, off + j))
    bspec = lambda off: pl.BlockSpec((1, tc), lambda j, i: (0, off + j))
    return pl.pallas_call(
        body, name=name, grid=(nb, S // ts),
        in_specs=half(0) + half(nb) + [wspec(0), wspec(nb), bspec(0), bspec(nb)],
        out_specs=pl.BlockSpec((ts, tc), lambda j, i: (i, j)),
        out_shape=jax.ShapeDtypeStruct((S, F), BF16),
        scratch_shapes=[pltpu.VMEM((HALO + ts, tc), F32)] * 2,
        compiler_params=_cparams("parallel", "parallel"),
    )(h0, h0, h0, h0, w, w, b, b)


def _ffn_act_bwd(h0, w, b, da, *, name):
    S, F2 = h0.shape
    F = F2 // 2
    ts, tc = min(CONV_ROW_TILE, S), FFN_TC
    nb = F // tc
    n = S // ts
    K = w.shape[0]
    R = ts + HALO

    def body(xg_ref, pg_ref, ng_ref, xv_ref, pv_ref, nv_ref, wg_ref, wv_ref, bg_ref, bv_ref, da_ref, dan_ref,
             dx_ref, dw_ref, db_ref, eg_ref, ev_ref, gg_ref, gv_ref):
        i = pl.program_id(1)
        last = i == n - 1
        for x_ref, p_ref, n_ref, e_ref in ((xg_ref, pg_ref, ng_ref, eg_ref), (xv_ref, pv_ref, nv_ref, ev_ref)):
            e_ref[0:HALO, :] = jnp.where(i > 0, p_ref[...], 0.0)
            e_ref[HALO:HALO + ts, :] = x_ref[...]
            e_ref[HALO + ts:, :] = jnp.where(last, 0.0, n_ref[...])
        gg_ref[0:ts, :] = da_ref[...]
        gg_ref[ts:, :] = jnp.where(last, 0.0, dan_ref[...])
        for r0, ch in _chunks(R):
            rows = pl.ds(r0, ch)
            dav = gg_ref[rows, :]
            pg = _conv_taps(eg_ref, wg_ref, bg_ref, ch, HALO + r0)
            pv = _conv_taps(ev_ref, wv_ref, bv_ref, ch, HALO + r0)
            s, ds = _dsilu(pg)
            gg_ref[rows, :] = dav * pv * ds
            gv_ref[rows, :] = dav * pg * s
        for h, (g_ref, e_ref, w_ref) in enumerate(((gg_ref, eg_ref, wg_ref), (gv_ref, ev_ref, wv_ref))):
            def dx_store(r0, ch, v):
                dx_ref[h, pl.ds(r0, ch), :] = v

            db, dw = _conv_bwd_taps(g_ref, e_ref, w_ref, dx_store, ts)
            _acc_out(db_ref.at[h], db, i == 0)
            _acc_out(dw_ref.at[h], dw, i == 0)

    nxt = lambda off: pl.BlockSpec((HALO, tc), lambda j, i: (jnp.minimum((i + 1) * (ts // HALO), S // HALO - 1), off + j))

    def half(off):
        return [pl.BlockSpec((ts, tc), lambda j, i: (i, off + j)), _prev_halo_spec(ts, tc, lambda j, i: off + j), nxt(off)]

    wspec = lambda off: pl.BlockSpec((K, tc), lambda j, i: (0, off + j))
    bspec = lambda off: pl.BlockSpec((1, tc), lambda j, i: (0, off + j))
    return pl.pallas_call(
        body, name=name, grid=(nb, n),
        in_specs=half(0) + half(nb) + [wspec(0), wspec(nb), bspec(0), bspec(nb),
                                       pl.BlockSpec((ts, tc), lambda j, i: (i, j)), nxt(0)],
        out_specs=[pl.BlockSpec((2, ts, tc), lambda j, i: (0, i, j)),
                   pl.BlockSpec((2, K, tc), lambda j, i: (0, 0, j)),
                   pl.BlockSpec((2, 1, tc), lambda j, i: (0, 0, j))],
        out_shape=[jax.ShapeDtypeStruct((2, S, F), BF16), jax.ShapeDtypeStruct((2, K, F), F32),
                   jax.ShapeDtypeStruct((2, 1, F), F32)],
        scratch_shapes=[pltpu.VMEM((HALO + R, tc), F32)] * 2 + [pltpu.VMEM((R, tc), F32)] * 2,
        compiler_params=_cparams("parallel", "arbitrary"),
    )(h0, h0, h0, h0, h0, h0, w, w, b, b, da, da)


CUM_TILE = 256
ATT_TILE = 512
ATT_TILE_FWD = 1024
NEG = -1e30
N_PAIR = ATTN_HEADS // 2
NT_DIMS = (((1,), (1,)), ((), ()))


def _split3(x):
    a = x.astype(BF16)
    r = x - a.astype(F32)
    b = r.astype(BF16)
    c = (r - b.astype(F32)).astype(BF16)
    return a, b, c


def _tri_dot(tri, x):
    return sum(jnp.dot(tri, p, preferred_element_type=F32) for p in _split3(x))


def _log_sigmoid(x):
    return jnp.minimum(x, 0.0) - jnp.log(1.0 + jnp.exp(-jnp.abs(x)))


def _col(v, idx, lane):
    return jnp.sum(jnp.where(lane == idx, v, 0.0), axis=1, keepdims=True)


def _fox_pre(rest, col, fb, *, name):
    S = rest.shape[0]
    ts = min(CUM_TILE, S)

    def body(f_ref, fb_ref, ccol_ref, carry_ref):
        i = pl.program_id(0)

        @pl.when(i == 0)
        def _():
            carry_ref[...] = jnp.zeros_like(carry_ref)

        lane = lax.broadcasted_iota(jnp.int32, (1, LANE), 1)
        logf = jnp.where(lane < ATTN_HEADS, _log_sigmoid(f_ref[...] + fb_ref[...]), 0.0)
        r = lax.broadcasted_iota(jnp.int32, (ts, ts), 0)
        c = lax.broadcasted_iota(jnp.int32, (ts, ts), 1)
        tri = jnp.where(c <= r, 1.0, 0.0).astype(BF16)
        cs = _tri_dot(tri, logf) + carry_ref[...]
        ccol_ref[...] = cs
        carry_ref[...] = cs[ts - 1:ts, :]

    return pl.pallas_call(
        body, name=name, grid=(S // ts,),
        in_specs=[pl.BlockSpec((ts, LANE), lambda i: (i, col)), pl.BlockSpec((1, LANE), lambda i: (0, 0))],
        out_specs=pl.BlockSpec((ts, LANE), lambda i: (i, 0)),
        out_shape=jax.ShapeDtypeStruct((S, LANE), F32),
        scratch_shapes=[pltpu.VMEM((1, LANE), F32)],
        compiler_params=_cparams("arbitrary"),
    )(rest, fb)


def _fox_post(dcq, dck, rest, col, fb, ddt, *, name):
    S = rest.shape[0]
    ts = min(CUM_TILE, S)
    n = S // ts

    def body(dcq_ref, dck_ref, f_ref, fb_ref, ddt_ref, o_ref, db_ref, carry_ref):
        i = pl.program_id(0)

        @pl.when(i == 0)
        def _():
            carry_ref[...] = jnp.zeros_like(carry_ref)

        lane = lax.broadcasted_iota(jnp.int32, (1, LANE), 1)
        dc = jnp.zeros((ts, LANE), F32)
        for h in range(ATTN_HEADS):
            d = dcq_ref[h // 2] - dck_ref[h // 2]
            dc = jnp.where(lane == h, _col(d, h % 2, lane), dc)
        r = lax.broadcasted_iota(jnp.int32, (ts, ts), 0)
        c = lax.broadcasted_iota(jnp.int32, (ts, ts), 1)
        tri = jnp.where(c >= r, 1.0, 0.0).astype(BF16)
        rc = _tri_dot(tri, dc) + carry_ref[...]
        carry_ref[...] = rc[0:1, :]
        df = rc * _sigmoid(-(f_ref[...] + fb_ref[...]))
        out = jnp.where(lane < ATTN_HEADS, df, ddt_ref[...])
        o_ref[...] = out.astype(BF16)
        _acc_out(db_ref, jnp.sum(out, axis=0, keepdims=True), i == 0)

    rev = lambda i: n - 1 - i
    return pl.pallas_call(
        body, name=name, grid=(n,),
        in_specs=[pl.BlockSpec((N_PAIR, ts, LANE), lambda i: (0, rev(i), 0)),
                  pl.BlockSpec((N_PAIR, ts, LANE), lambda i: (0, rev(i), 0)),
                  pl.BlockSpec((ts, LANE), lambda i: (rev(i), col)),
                  pl.BlockSpec((1, LANE), lambda i: (0, 0)),
                  pl.BlockSpec((ts, LANE), lambda i: (rev(i), 0))],
        out_specs=[pl.BlockSpec((ts, LANE), lambda i: (rev(i), 0)), pl.BlockSpec((1, LANE), lambda i: (0, 0))],
        out_shape=[jax.ShapeDtypeStruct((S, LANE), BF16), jax.ShapeDtypeStruct((1, LANE), F32)],
        scratch_shapes=[pltpu.VMEM((1, LANE), F32)],
        compiler_params=_cparams("arbitrary"),
    )(dcq, dck, rest, fb, ddt)


def _head_masks():
    lane = lax.broadcasted_iota(jnp.int32, (1, LANE), 1)
    return lane, (lane < ATTN_HEAD_DIM, lane >= ATTN_HEAD_DIM)


def _causal_mask(t):
    r = lax.broadcasted_iota(jnp.int32, (t, t), 0)
    c = lax.broadcasted_iota(jnp.int32, (t, t), 1)
    return r, c


AUG_ONE = 0
AUG_C = 3


def _free_lane0(hh):
    return ATTN_HEAD_DIM if hh == 0 else 0


def _augment(x, head_mask, lane, f0, c, key_side):
    terms = _split3(c)
    one = jnp.ones((), BF16)
    ones_at, terms_at = (AUG_ONE, AUG_C) if key_side else (AUG_C, AUG_ONE)
    out = jnp.where(head_mask, x, jnp.zeros((), BF16))
    for k in range(3):
        out = jnp.where(lane == f0 + ones_at + k, one, out)
        out = jnp.where(lane == f0 + terms_at + k, terms[k], out)
    return out


def _carry_start(kind, in_refs, out_refs, sems, first):
    @pl.when(first)
    def _():
        if kind == "gather":
            _gather_phase(0, in_refs, out_refs, sems)
        else:
            _a2a_start(in_refs, out_refs, sems)


def _carry_mid(kind, in_refs, out_refs, sems, mid):
    if kind == "gather":
        @pl.when(mid)
        def _():
            _gather_phase(1, in_refs, out_refs, sems)


def _carry_wait(kind, in_refs, out_refs, sems, last):
    @pl.when(last)
    def _():
        if kind == "gather":
            _gather_phase(2, in_refs, out_refs, sems)
        else:
            _a2a_wait(in_refs, out_refs, sems)


def _attn_fwd(qkv, ccol, *, comm=None, name):
    S = qkv.shape[0]
    t = min(ATT_TILE_FWD, S)
    n = S // t
    scale = ATTN_HEAD_DIM ** -0.5

    nc = len(comm[1]) if comm else 0

    def body(*refs):
        q_ref, k_ref, v_ref, cc_ref = refs[:4]
        o_ref, lrow_ref = refs[4 + nc:6 + nc]
        kaug_ref, vt_ref, m_ref, l_ref, acc_ref = refs[6 + 2 * nc:11 + 2 * nc]
        j, i = pl.program_id(0), pl.program_id(1)
        lane, masks = _head_masks()
        if comm:
            carried = (refs[4:4 + nc], refs[6 + nc:6 + 2 * nc], refs[11 + 2 * nc:])
            _carry_start(comm[0], *carried, jnp.logical_and(j == 0, i == 0))
            _carry_mid(comm[0], *carried, jnp.logical_and(j == N_PAIR - 1, i == 0))

        @pl.when(i == 0)
        def _():
            def fill(b, carry):
                off = pl.multiple_of(b * t, t)
                kblk = k_ref[pl.ds(off, t), :]
                ccb = cc_ref[pl.ds(off, t), :]
                for hh in range(2):
                    ck = _col(ccb, 2 * j + hh, lane)
                    kaug_ref[hh, pl.ds(off, t), :] = _augment(kblk, masks[hh], lane, _free_lane0(hh), -ck, True)
                vt_ref[:, pl.ds(off, t)] = v_ref[pl.ds(off, t), :].astype(F32).T.astype(BF16)
                return carry
            lax.fori_loop(0, n, fill, 0)

        q2 = q_ref[...] * scale
        ccq = cc_ref[pl.ds(pl.multiple_of(i * t, t), t), :]
        qaug = [_augment(q2, masks[hh], lane, _free_lane0(hh), _col(ccq, 2 * j + hh, lane), False) for hh in range(2)]
        m_ref[...] = jnp.full(m_ref.shape, NEG, F32)
        l_ref[...] = jnp.zeros_like(l_ref)
        acc_ref[...] = jnp.zeros_like(acc_ref)

        def step(kb, masked):
            off = pl.multiple_of(kb * t, t)
            vt = vt_ref[:, pl.ds(off, t)]
            for hh in range(2):
                st = lax.dot_general(kaug_ref[hh, pl.ds(off, t), :], qaug[hh], NT_DIMS, preferred_element_type=F32)
                if masked:
                    r, c = _causal_mask(t)
                    st = jnp.where(r <= c, st, NEG)
                m_old = m_ref[hh]
                m_new = jnp.maximum(m_old, jnp.max(st, axis=0, keepdims=True))
                p = jnp.exp(st - m_new)
                alpha = jnp.exp(m_old - m_new)
                l_ref[hh] = alpha * l_ref[hh] + jnp.sum(p, axis=0, keepdims=True)
                acc_ref[hh] = alpha * acc_ref[hh] + jnp.dot(vt, p.astype(BF16), preferred_element_type=F32)
                m_ref[hh] = m_new

        def loop_body(kb, carry):
            step(kb, False)
            return carry

        lax.fori_loop(0, i, loop_body, 0)
        step(i, True)
        drow = lax.broadcasted_iota(jnp.int32, (LANE, 1), 0)
        ot = jnp.where(drow < ATTN_HEAD_DIM, acc_ref[0] / l_ref[0], acc_ref[1] / l_ref[1])
        o_ref[...] = ot.T.astype(BF16)
        for hh in range(2):
            lrow_ref[hh] = m_ref[hh] + jnp.log(l_ref[hh])
        if comm:
            _carry_wait(comm[0], *carried, jnp.logical_and(j == N_PAIR - 1, i == n - 1))

    res = pl.pallas_call(
        body, name=name, grid=(N_PAIR, n),
        in_specs=[pl.BlockSpec((t, LANE), lambda j, i: (i, j)),
                  pl.BlockSpec((S, LANE), lambda j, i: (0, N_PAIR + j)),
                  pl.BlockSpec((S, LANE), lambda j, i: (0, 2 * N_PAIR + j)),
                  pl.BlockSpec((S, LANE), lambda j, i: (0, 0))] + [ANY] * nc,
        out_specs=[pl.BlockSpec((t, LANE), lambda j, i: (i, j)),
                   pl.BlockSpec((2, 1, t), lambda j, i: (j, 0, i))] + [ANY] * nc,
        out_shape=[jax.ShapeDtypeStruct((S, N_PAIR * LANE), BF16), jax.ShapeDtypeStruct((ATTN_HEADS, 1, S), F32)]
        + (_a2a_out_shapes(*comm) if comm else []),
        scratch_shapes=[pltpu.VMEM((2, S, LANE), BF16), pltpu.VMEM((LANE, S), BF16),
                        pltpu.VMEM((2, 1, t), F32), pltpu.VMEM((2, 1, t), F32), pltpu.VMEM((2, LANE, t), F32)]
        + (_a2a_sems(nc) if comm else []),
        compiler_params=_cparams("arbitrary" if comm else "parallel", "arbitrary"),
    )(qkv, qkv, qkv, ccol, *(comm[1] if comm else []))
    return res[0], res[1], list(res[2:])


def _attn_bwd(qkv, o, do, lrow, ccol, *, comm=None, name):
    S = qkv.shape[0]
    t = min(ATT_TILE, S)
    n = S // t
    scale = ATTN_HEAD_DIM ** -0.5

    nc = len(comm[1]) if comm else 0

    def body(*refs):
        q_ref, k_ref, v_ref, o_ref, do_ref, lrow_ref, cc_ref = refs[:7]
        dq_ref, dk_ref, dv_ref, dcq_ref, dck_ref = refs[7 + nc:12 + nc]
        qaug_ref, drow_ref, dqt_ref, dka_ref, dva_ref = refs[12 + 2 * nc:17 + 2 * nc]
        carried = (refs[7:7 + nc], refs[12 + nc:12 + 2 * nc], refs[17 + 2 * nc:])
        j, kb = pl.program_id(0), pl.program_id(1)
        lane, masks = _head_masks()
        if comm:
            _carry_start(comm[0], *carried, jnp.logical_and(j == 0, kb == 0))
        drows = lax.broadcasted_iota(jnp.int32, (LANE, 1), 0)

        @pl.when(kb == 0)
        def _():
            dqt_ref[...] = jnp.zeros_like(dqt_ref)

            def fill(b, carry):
                off = pl.multiple_of(b * t, t)
                q2 = q_ref[pl.ds(off, t), :] * scale
                ccb = cc_ref[pl.ds(off, t), :]
                prod_t = (do_ref[pl.ds(off, t), :].astype(F32) * o_ref[pl.ds(off, t), :].astype(F32)).T
                lo = jnp.sum(jnp.where(drows < ATTN_HEAD_DIM, prod_t, 0.0), axis=0, keepdims=True)
                drow_ref[0, :, pl.ds(off, t)] = lo
                drow_ref[1, :, pl.ds(off, t)] = jnp.sum(prod_t, axis=0, keepdims=True) - lo
                for hh in range(2):
                    cq = _col(ccb, 2 * j + hh, lane)
                    qaug_ref[hh, pl.ds(off, t), :] = _augment(q2, masks[hh], lane, _free_lane0(hh), cq, False)
                return carry
            lax.fori_loop(0, n, fill, 0)

        koff = pl.multiple_of(kb * t, t)
        kblk = k_ref[...]
        v2 = v_ref[...]
        cck = cc_ref[pl.ds(koff, t), :]
        kaug = [_augment(kblk, masks[hh], lane, _free_lane0(hh), -_col(cck, 2 * j + hh, lane), True) for hh in range(2)]
        kaug_t = [ka.astype(F32).T.astype(BF16) for ka in kaug]
        vh = [jnp.where(mk, v2, jnp.zeros((), BF16)) for mk in masks]
        dka_ref[...] = jnp.zeros_like(dka_ref)
        dva_ref[...] = jnp.zeros_like(dva_ref)

        def step(qb, masked):
            off = pl.multiple_of(qb * t, t)
            doblk = do_ref[pl.ds(off, t), :]
            for hh in range(2):
                qa = qaug_ref[hh, pl.ds(off, t), :]
                st = lax.dot_general(kaug[hh], qa, NT_DIMS, preferred_element_type=F32)
                if masked:
                    r, c = _causal_mask(t)
                    st = jnp.where(r <= c, st, NEG)
                pt = jnp.exp(st - lrow_ref[hh, :, pl.ds(off, t)])
                dpt = lax.dot_general(vh[hh], doblk, NT_DIMS, preferred_element_type=F32)
                dst = (pt * (dpt - drow_ref[hh, :, pl.ds(off, t)])).astype(BF16)
                dva_ref[hh] += jnp.dot(pt.astype(BF16), doblk, preferred_element_type=F32)
                dka_ref[hh] += jnp.dot(dst, qa, preferred_element_type=F32)
                dqt_ref[hh, :, pl.ds(off, t)] += jnp.dot(kaug_t[hh], dst, preferred_element_type=F32)

        step(kb, True)

        def loop_body(qb, carry):
            step(qb, False)
            return carry

        lax.fori_loop(kb + 1, n, loop_body, 0)
        dk_ref[...] = jnp.where(masks[0], dka_ref[0], dka_ref[1]).astype(BF16)
        dv_ref[...] = jnp.where(masks[0], dva_ref[0], dva_ref[1]).astype(BF16)
        dck = [_col(dka_ref[hh], _free_lane0(hh) + AUG_C, lane) for hh in range(2)]
        dck_ref[0] = jnp.where(lane == 0, dck[0], jnp.where(lane == 1, dck[1], 0.0))

        @pl.when(kb == n - 1)
        def _():
            def flush(b, carry):
                off = pl.multiple_of(b * t, t)
                d = [dqt_ref[hh, :, pl.ds(off, t)].T for hh in range(2)]
                dq_ref[pl.ds(off, t), :] = (jnp.where(masks[0], d[0], d[1]) * scale).astype(BF16)
                dcq = [_col(d[hh], _free_lane0(hh) + AUG_ONE, lane) for hh in range(2)]
                dcq_ref[0, pl.ds(off, t), :] = jnp.where(lane == 0, dcq[0], jnp.where(lane == 1, dcq[1], 0.0))
                return carry
            lax.fori_loop(0, n, flush, 0)

        if comm:
            _carry_wait(comm[0], *carried, jnp.logical_and(j == N_PAIR - 1, kb == n - 1))

    full = lambda cb: pl.BlockSpec((S, LANE), lambda j, kb: (0, cb(j)))
    kspec = lambda base: pl.BlockSpec((t, LANE), lambda j, kb: (kb, base + j))
    oblk = pl.BlockSpec((t, LANE), lambda j, kb: (kb, j))
    res = pl.pallas_call(
        body, name=name, grid=(N_PAIR, n),
        in_specs=[full(lambda j: j), kspec(N_PAIR), kspec(2 * N_PAIR), full(lambda j: j), full(lambda j: j),
                  pl.BlockSpec((2, 1, S), lambda j, kb: (j, 0, 0)), full(lambda j: 0)] + [ANY] * nc,
        out_specs=[full(lambda j: j), oblk, oblk,
                   pl.BlockSpec((1, S, LANE), lambda j, kb: (j, 0, 0)),
                   pl.BlockSpec((1, t, LANE), lambda j, kb: (j, kb, 0))] + [ANY] * nc,
        out_shape=[jax.ShapeDtypeStruct((S, N_PAIR * LANE), BF16)] * 3 + [jax.ShapeDtypeStruct((N_PAIR, S, LANE), F32)] * 2
        + (_a2a_out_shapes(*comm) if comm else []),
        scratch_shapes=[pltpu.VMEM((2, S, LANE), BF16), pltpu.VMEM((2, 1, S), F32), pltpu.VMEM((2, LANE, S), F32),
                        pltpu.VMEM((2, t, LANE), F32), pltpu.VMEM((2, t, LANE), F32)]
        + (_a2a_sems(nc) if comm else []),
        compiler_params=_cparams("arbitrary" if comm else "parallel", "arbitrary"),
    )(qkv, qkv, qkv, o, do, lrow, ccol, *(comm[1] if comm else []))
    return tuple(res[:5]) + (list(res[5:]),)


DT_LANE0 = ATTN_HEADS
SSD_PAIRS = SSD_HEADS // 2
SSD_X = SSD_HEADS * SSD_HEAD_DIM
SSD_B0 = SSD_X
SSD_C0 = SSD_X + SSD_GROUPS * SSD_STATE
SSD_CH = SSD_X + 2 * SSD_GROUPS * SSD_STATE
TN_DIMS = (((0,), (0,)), ((), ()))


def _softplus(x):
    return jnp.maximum(x, 0.0) + jnp.log(1.0 + jnp.exp(-jnp.abs(x)))


def _ssd_prep(fdt, dtb, alog):
    L = fdt.shape[0]
    lane = lax.broadcasted_iota(jnp.int32, (1, LANE), 1)
    hl = jnp.logical_and(lane >= DT_LANE0, lane < DT_LANE0 + SSD_HEADS)
    dtv = jnp.where(hl, _softplus(fdt + dtb), 0.0)
    A = jnp.where(hl, -jnp.exp(alog), 0.0)
    r = lax.broadcasted_iota(jnp.int32, (L, L), 0)
    c = lax.broadcasted_iota(jnp.int32, (L, L), 1)
    cs = _tri_dot(jnp.where(c <= r, 1.0, 0.0).astype(BF16), dtv * A)
    return lane, hl, dtv, A, cs, r, c


def _halves(lane, v0, v1):
    return jnp.where(lane < SSD_HEAD_DIM, v0, v1)


def _head_select(width):
    r = jnp.arange(LANE)[:, None]
    c = jnp.arange(SSD_HEADS * width)[None, :]
    return (r == DT_LANE0 + c // width).astype(BF16)


def _spread(x, sel, terms):
    return sum(jnp.dot(p, sel, preferred_element_type=F32) for p in _split3(x)[:terms])


def _ssd_fwd(xc, rest, col, dtb, alog, dskip, *, name):
    S = xc.shape[0]
    L = SSD_CHUNK
    nc = S // L

    def body(xc_ref, f_ref, dtb_ref, al_ref, dk_ref, selp_ref, y_ref, hs_ref, h_ref, cst_ref):
        i = pl.program_id(0)

        @pl.when(i == 0)
        def _():
            h_ref[...] = jnp.zeros_like(h_ref)

        lane, hl, dtv, A, cs, r, c = _ssd_prep(f_ref[...], dtb_ref[...], al_ref[...])
        cst_ref[...] = cs.T
        cs_last = cs[L - 1:L, :]
        cd = jnp.exp(cs_last)
        dkv = dk_ref[...]
        selp = selp_ref[...]
        dt_p, ecs_p, dec_p = (_spread(v, selp, 2) for v in (dtv, jnp.exp(cs), jnp.exp(cs_last - cs)))
        prow = lax.broadcasted_iota(jnp.int32, (LANE, 1), 0)
        for g in range(SSD_GROUPS):
            Bg = xc_ref[:, SSD_B0 + g * SSD_STATE:SSD_B0 + (g + 1) * SSD_STATE].astype(BF16)
            Cg = xc_ref[:, SSD_C0 + g * SSD_STATE:SSD_C0 + (g + 1) * SSD_STATE].astype(BF16)
            CB = lax.dot_general(Cg, Bg, NT_DIMS, preferred_element_type=F32)
            for pp in range(SSD_PAIRS // SSD_GROUPS):
                pi = g * (SSD_PAIRS // SSD_GROUPS) + pp
                hl0 = DT_LANE0 + 2 * pi
                pair = slice(pi * LANE, (pi + 1) * LANE)
                x2 = xc_ref[:, pair]
                xd = x2 * dt_p[:, pair]
                xdb = xd.astype(BF16)
                yd = jnp.zeros((L, LANE), F32)
                for hh in range(2):
                    seg = _col(cs, hl0 + hh, lane) - cst_ref[hl0 + hh:hl0 + hh + 1, :]
                    M = CB * jnp.exp(jnp.where(c <= r, seg, NEG))
                    yh = jnp.dot(M.astype(BF16), xdb, preferred_element_type=F32)
                    yd = jnp.where((lane >= SSD_HEAD_DIM) if hh else (lane < SSD_HEAD_DIM), yh, yd)
                hp = h_ref[pi]
                hs_ref[0, pi] = hp
                yo = lax.dot_general(Cg, hp.astype(BF16), NT_DIMS, preferred_element_type=F32) * ecs_p[:, pair]
                dsk = _halves(lane, _col(dkv, hl0, lane), _col(dkv, hl0 + 1, lane))
                y_ref[:, pair] = yd + yo + dsk * x2
                xw = (xd * dec_p[:, pair]).astype(BF16)
                st = lax.dot_general(xw, Bg, TN_DIMS, preferred_element_type=F32)
                cdp = jnp.where(prow < SSD_HEAD_DIM, _col(cd, hl0, lane), _col(cd, hl0 + 1, lane))
                h_ref[pi] = cdp * hp + st

    vec = pl.BlockSpec((1, LANE), lambda i: (0, 0))
    selp = _head_select(SSD_HEAD_DIM)
    return pl.pallas_call(
        body, name=name, grid=(nc,),
        in_specs=[pl.BlockSpec((L, SSD_CH), lambda i: (i, 0)), pl.BlockSpec((L, LANE), lambda i: (i, col)),
                  vec, vec, vec, pl.BlockSpec(selp.shape, lambda i: (0, 0))],
        out_specs=[pl.BlockSpec((L, SSD_X), lambda i: (i, 0)),
                   pl.BlockSpec((1, SSD_PAIRS, LANE, SSD_STATE), lambda i: (i, 0, 0, 0))],
        out_shape=[jax.ShapeDtypeStruct((S, SSD_X), F32),
                   jax.ShapeDtypeStruct((nc, SSD_PAIRS, LANE, SSD_STATE), F32)],
        scratch_shapes=[pltpu.VMEM((SSD_PAIRS, LANE, SSD_STATE), F32), pltpu.VMEM((LANE, L), F32)],
        compiler_params=_cparams("arbitrary"),
    )(xc, rest, dtb, alog, dskip, selp)


def _pair_sums(lane, v):
    lo = jnp.sum(jnp.where(lane < SSD_HEAD_DIM, v, 0.0), axis=1, keepdims=True)
    return lo, jnp.sum(v, axis=1, keepdims=True) - lo


def _ssd_bwd(xc, rest, col, dtb, alog, dskip, hs, dy, *, name):
    S = xc.shape[0]
    L = SSD_CHUNK
    nc = S // L
    PG = SSD_PAIRS // SSD_GROUPS

    def body(xc_ref, f_ref, dtb_ref, al_ref, dk_ref, hs_ref, dy_ref, selp_ref,
             dxc_ref, ddt_ref, dp_ref, dh_ref, cst_ref):
        i = pl.program_id(0)

        @pl.when(i == 0)
        def _():
            dh_ref[...] = jnp.zeros_like(dh_ref)
            dp_ref[...] = jnp.zeros_like(dp_ref)

        fv = f_ref[...] + dtb_ref[...]
        lane, hl, dtv, A, cs, r, c = _ssd_prep(f_ref[...], dtb_ref[...], al_ref[...])
        cst_ref[...] = cs.T
        cs_last = cs[L - 1:L, :]
        cd = jnp.exp(cs_last)
        dkv = dk_ref[...]
        selp = selp_ref[...]
        dt_p, ecs_p, dec_p = (_spread(v, selp, 2) for v in (dtv, jnp.exp(cs), jnp.exp(cs_last - cs)))
        prow = lax.broadcasted_iota(jnp.int32, (LANE, 1), 0)
        lrow = lax.broadcasted_iota(jnp.int32, (L, 1), 0)
        is_last = lrow == L - 1
        causal = c <= r
        dcs = jnp.zeros((L, LANE), F32)
        ddt = jnp.zeros((L, LANE), F32)
        dD = jnp.zeros((1, LANE), F32)
        for g in range(SSD_GROUPS):
            Bg = xc_ref[:, SSD_B0 + g * SSD_STATE:SSD_B0 + (g + 1) * SSD_STATE].astype(BF16)
            Cg = xc_ref[:, SSD_C0 + g * SSD_STATE:SSD_C0 + (g + 1) * SSD_STATE].astype(BF16)
            CB = lax.dot_general(Cg, Bg, NT_DIMS, preferred_element_type=F32)
            dCB = jnp.zeros((L, L), F32)
            dB = jnp.zeros((L, SSD_STATE), F32)
            dC = jnp.zeros((L, SSD_STATE), F32)
            for pp in range(PG):
                pi = g * PG + pp
                hl0 = DT_LANE0 + 2 * pi
                pair = slice(pi * LANE, (pi + 1) * LANE)
                x2 = xc_ref[:, pair]
                dy2 = dy_ref[:, pair]
                dtp, ecsp, decp = dt_p[:, pair], ecs_p[:, pair], dec_p[:, pair]
                xd = x2 * dtp
                xdb = xd.astype(BF16)
                dsk = _halves(lane, _col(dkv, hl0, lane), _col(dkv, hl0 + 1, lane))
                dx2 = dsk * dy2
                sD = _pair_sums(lane, dy2 * x2)
                hp = hs_ref[0, pi]
                hpb = hp.astype(BF16)
                yo = lax.dot_general(Cg, hpb, NT_DIMS, preferred_element_type=F32) * ecsp
                dW = (ecsp * dy2).astype(BF16)
                dC = dC + jnp.dot(dW, hpb, preferred_element_type=F32)
                dhp = lax.dot_general(dW, Cg, TN_DIMS, preferred_element_type=F32)
                sYo = _pair_sums(lane, dy2 * yo)
                dhn = dh_ref[pi]
                cdh = (_col(cd, hl0, lane), _col(cd, hl0 + 1, lane))
                dhp = dhp + jnp.where(prow < SSD_HEAD_DIM, cdh[0], cdh[1]) * dhn
                rs = jnp.sum(dhn * hp, axis=1, keepdims=True)
                lo = jnp.sum(jnp.where(prow < SSD_HEAD_DIM, rs, 0.0), axis=0, keepdims=True)
                dcd = (lo, jnp.sum(rs, axis=0, keepdims=True) - lo)
                dhnb = dhn.astype(BF16)
                G = lax.dot_general(Bg, dhnb, NT_DIMS, preferred_element_type=F32)
                dxd = decp * G
                sdd = _pair_sums(lane, xd * dxd)
                dB = dB + jnp.dot((xd * decp).astype(BF16), dhnb, preferred_element_type=F32)
                dh_ref[pi] = dhp
                for hh in range(2):
                    hmask = (lane >= SSD_HEAD_DIM) if hh else (lane < SSD_HEAD_DIM)
                    seg = _col(cs, hl0 + hh, lane) - cst_ref[hl0 + hh:hl0 + hh + 1, :]
                    Lm = jnp.exp(jnp.where(causal, seg, NEG))
                    M = CB * Lm
                    dyh = jnp.where(hmask, dy2, 0.0).astype(BF16)
                    dM = lax.dot_general(dyh, xdb, NT_DIMS, preferred_element_type=F32)
                    dxd = dxd + lax.dot_general(M.astype(BF16), dyh, TN_DIMS, preferred_element_type=F32)
                    Q = dM * M
                    dCB = dCB + dM * Lm
                    dd = sdd[hh]
                    end = dcd[hh] * cdh[hh] + jnp.sum(dd, axis=0, keepdims=True)
                    dcs_h = (sYo[hh] - dd + jnp.sum(Q, axis=1, keepdims=True)
                             - jnp.sum(Q.T, axis=1, keepdims=True) + jnp.where(is_last, end, 0.0))
                    dcs = jnp.where(lane == hl0 + hh, dcs_h, dcs)
                    dD = jnp.where(lane == hl0 + hh, jnp.sum(sD[hh], axis=0, keepdims=True), dD)
                sdt = _pair_sums(lane, dxd * x2)
                ddt = jnp.where(lane == hl0, sdt[0], jnp.where(lane == hl0 + 1, sdt[1], ddt))
                dxc_ref[:, pair] = dx2 + dxd * dtp
            dCBb = dCB.astype(BF16)
            dC = dC + jnp.dot(dCBb, Bg, preferred_element_type=F32)
            dB = dB + lax.dot_general(dCBb, Cg, TN_DIMS, preferred_element_type=F32)
            dxc_ref[:, SSD_B0 + g * SSD_STATE:SSD_B0 + (g + 1) * SSD_STATE] = dB
            dxc_ref[:, SSD_C0 + g * SSD_STATE:SSD_C0 + (g + 1) * SSD_STATE] = dC
        da = _tri_dot(jnp.where(c >= r, 1.0, 0.0).astype(BF16), dcs)
        ddtv = ddt + da * A
        ddt_ref[...] = jnp.where(hl, ddtv * _sigmoid(fv), 0.0)
        dal = jnp.sum(da * dtv, axis=0, keepdims=True) * A
        dp_ref[0:1, :] += dal
        dp_ref[1:2, :] += dD

    rev = lambda i: nc - 1 - i
    vec = pl.BlockSpec((1, LANE), lambda i: (0, 0))
    selp = _head_select(SSD_HEAD_DIM)
    return pl.pallas_call(
        body, name=name, grid=(nc,),
        in_specs=[pl.BlockSpec((L, SSD_CH), lambda i: (rev(i), 0)), pl.BlockSpec((L, LANE), lambda i: (rev(i), col)),
                  vec, vec, vec,
                  pl.BlockSpec((1, SSD_PAIRS, LANE, SSD_STATE), lambda i: (rev(i), 0, 0, 0)),
                  pl.BlockSpec((L, SSD_X), lambda i: (rev(i), 0)), pl.BlockSpec(selp.shape, lambda i: (0, 0))],
        out_specs=[pl.BlockSpec((L, SSD_CH), lambda i: (rev(i), 0)), pl.BlockSpec((L, LANE), lambda i: (rev(i), 0)),
                   pl.BlockSpec((SUBLANE, LANE), lambda i: (0, 0))],
        out_shape=[jax.ShapeDtypeStruct((S, SSD_CH), F32), jax.ShapeDtypeStruct((S, LANE), F32),
                   jax.ShapeDtypeStruct((SUBLANE, LANE), F32)],
        scratch_shapes=[pltpu.VMEM((SSD_PAIRS, LANE, SSD_STATE), F32), pltpu.VMEM((LANE, L), F32)],
        compiler_params=_cparams("arbitrary"),
    )(xc, rest, dtb, alog, dskip, hs, dy, selp)


def _gate_norm_fwd(y, rest, zcol, nw, *, name):
    S, W = y.shape
    ts = _row_tile(S)
    GW = W // SSD_GROUPS

    def body(y_ref, z_ref, nw_ref, o_ref):
        z = z_ref[...]
        t = y_ref[...] * (z * _sigmoid(z))
        for g in range(SSD_GROUPS):
            cols = slice(g * GW, (g + 1) * GW)
            tg = t[:, cols]
            rr = lax.rsqrt(jnp.mean(tg * tg, axis=-1, keepdims=True) + NORM_EPS)
            o_ref[:, cols] = (tg * rr * nw_ref[:, cols]).astype(BF16)

    row = pl.BlockSpec((ts, W), lambda i: (i, 0))
    return pl.pallas_call(
        body, name=name, grid=(S // ts,),
        in_specs=[row, pl.BlockSpec((ts, W), lambda i: (i, zcol)), pl.BlockSpec((1, W), lambda i: (0, 0))],
        out_specs=row,
        out_shape=jax.ShapeDtypeStruct((S, W), BF16),
        compiler_params=_cparams("parallel"),
    )(y, rest, nw)


def _gate_norm_bwd(y, rest, zcol, nw, do, *, name):
    S, W = y.shape
    ts = _row_tile(S)
    GW = W // SSD_GROUPS

    def body(y_ref, z_ref, nw_ref, do_ref, dy_ref, dz_ref, dnw_ref):
        z = z_ref[...]
        yv = y_ref[...]
        s, ds = _dsilu(z)
        sz = z * s
        t = yv * sz
        dov = do_ref[...]
        parts = []
        for g in range(SSD_GROUPS):
            cols = slice(g * GW, (g + 1) * GW)
            tg = t[:, cols]
            rr = lax.rsqrt(jnp.mean(tg * tg, axis=-1, keepdims=True) + NORM_EPS)
            n = tg * rr
            dog = dov[:, cols]
            dn = dog * nw_ref[:, cols]
            dt = rr * (dn - n * jnp.mean(dn * n, axis=-1, keepdims=True))
            dy_ref[:, cols] = dt * sz[:, cols]
            dz_ref[:, cols] = (dt * yv[:, cols] * ds[:, cols]).astype(BF16)
            parts.append(jnp.sum(dog * n, axis=0, keepdims=True))
        _acc_out(dnw_ref, jnp.concatenate(parts, axis=1), pl.program_id(0) == 0)

    row = pl.BlockSpec((ts, W), lambda i: (i, 0))
    vec = pl.BlockSpec((1, W), lambda i: (0, 0))
    return pl.pallas_call(
        body, name=name, grid=(S // ts,),
        in_specs=[row, pl.BlockSpec((ts, W), lambda i: (i, zcol)), vec, row],
        out_specs=[row, row, vec],
        out_shape=[jax.ShapeDtypeStruct((S, W), F32), jax.ShapeDtypeStruct((S, W), BF16),
                   jax.ShapeDtypeStruct((1, W), F32)],
        compiler_params=_cparams("arbitrary"),
    )(y, rest, nw, do)


N_DEV = 8
MESH = pl.DeviceIdType.MESH
ANY = pl.BlockSpec(memory_space=pl.ANY)


def _all_gather(xs, *, name):
    n = len(xs)

    def body(*refs):
        for phase in range(3):
            _gather_phase(phase, refs[:n], refs[n:2 * n], refs[2 * n:])

    return pl.pallas_call(
        body, name=name,
        out_shape=_a2a_out_shapes("gather", xs), in_specs=[ANY] * n, out_specs=[ANY] * n,
        scratch_shapes=_a2a_sems(n),
    )(*xs)


def _gather_phase(phase, x_refs, o_refs, sems):
    send_sems, recv_sems, local_sems = sems
    n = len(x_refs)
    px, py, pc = lax.axis_index("x"), lax.axis_index("y"), lax.axis_index("c")
    me, sibling = (px, py, pc), (px, py, 1 - pc)
    chips = [(1 - px, py), (px, 1 - py), (1 - px, 1 - py)]

    def copy(a, k, block, to, src=None):
        slot = o_refs[a].at[4 * block[0] + 2 * block[1] + block[2]]
        return pltpu.make_async_remote_copy(
            src_ref=slot if src is None else src, dst_ref=slot,
            send_sem=send_sems.at[a, k], recv_sem=recv_sems.at[a, k], device_id=to, device_id_type=MESH)

    def own(a):
        return ([copy(a, 0, me, sibling, src=x_refs[a])]
                + [copy(a, 1 + j, me, (*chip, pc), src=x_refs[a]) for j, chip in enumerate(chips)])

    mine = lambda a: pltpu.make_async_copy(x_refs[a], o_refs[a].at[4 * px + 2 * py + pc], local_sems.at[a])
    passed = lambda a, j: copy(a, 4 + j, (*chips[j], pc), sibling)
    if phase == 0:
        for a in range(n):
            mine(a).start()
            for cp in own(a):
                cp.start()
    elif phase == 1:
        for j, chip in enumerate(chips):
            for a in range(n):
                copy(a, 1 + j, (*chip, pc), me).wait_recv()
                passed(a, j).start()
    else:
        for a in range(n):
            copy(a, 0, sibling, me).wait_recv()
            for j, chip in enumerate(chips):
                copy(a, 4 + j, (*chip, 1 - pc), me).wait_recv()
            for cp in own(a) + [passed(a, j) for j in range(len(chips))]:
                cp.wait_send()
            mine(a).wait()


def _exchange(sends, *, name):
    n = len(sends)

    def body(*refs):
        _a2a_start(refs[:n], refs[n:2 * n], refs[2 * n:])
        _a2a_wait(refs[:n], refs[n:2 * n], refs[2 * n:])

    return pl.pallas_call(
        body, name=name,
        out_shape=_a2a_out_shapes("exchange", sends), in_specs=[ANY] * n, out_specs=[ANY] * n,
        scratch_shapes=_a2a_sems(n),
    )(*sends)


def _a2a_out_shapes(kind, arrays):
    if kind == "gather":
        return [jax.ShapeDtypeStruct((N_DEV,) + x.shape, x.dtype) for x in arrays]
    return [jax.ShapeDtypeStruct(x.shape, x.dtype) for x in arrays]


def _a2a_sems(n):
    return [pltpu.SemaphoreType.DMA((n, N_DEV - 1)), pltpu.SemaphoreType.DMA((n, N_DEV - 1)),
            pltpu.SemaphoreType.DMA((n,))]


def _a2a_copies(in_refs, out_refs, send_sems, recv_sems, local_sems, arrivals):
    px, py, pc = lax.axis_index("x"), lax.axis_index("y"), lax.axis_index("c")
    me = 4 * px + 2 * py + pc
    n = len(in_refs)
    src = lambda a, p: in_refs[a].at[p]
    local = [pltpu.make_async_copy(src(a, me), out_refs[a].at[me], local_sems.at[a]) for a in range(n)]
    sent, landing = [], []
    for k in range(1, N_DEV):
        qx = 1 - px if k & 4 else px
        qy = 1 - py if k & 2 else py
        qc = 1 - pc if k & 1 else pc
        peer = 4 * qx + 2 * qy + qc
        for a in range(n):
            def rdma(dst_slot):
                return pltpu.make_async_remote_copy(
                    src_ref=src(a, peer), dst_ref=out_refs[a].at[dst_slot],
                    send_sem=send_sems.at[a, k - 1], recv_sem=recv_sems.at[a, k - 1],
                    device_id=(qx, qy, qc), device_id_type=MESH)
            sent.append(rdma(me))
            if arrivals:
                landing.append(rdma(peer))
    return local, sent, landing


def _a2a_start(in_refs, out_refs, sems):
    local, sent, _ = _a2a_copies(in_refs, out_refs, *sems, arrivals=False)
    for cp in local + sent:
        cp.start()


def _a2a_wait(in_refs, out_refs, sems):
    local, sent, landing = _a2a_copies(in_refs, out_refs, *sems, arrivals=True)
    for cp in landing:
        cp.wait_recv()
    for cp in sent:
        cp.wait_send()
    for cp in local:
        cp.wait()


IN_POOL, IN_QKV, IN_F, IN_Z, IN_XBC, IN_DT, IN_GATE, IN_TOTAL = 0, 512, 2048, 2056, 3080, 4616, 4632, 7704
QKV_W = IN_F - IN_QKV
REST_Z_BLK, REST_POOL_BLK, REST_XBC0, REST_FDT_BLK = 3, 8, 4608, 48
REST_FDT0 = REST_FDT_BLK * LANE
REST_USED = REST_FDT0 + (IN_Z - IN_F) + (IN_GATE - IN_DT)
REST_W = REST_FDT0 + LANE


def _w_in_runs(shard):
    segs = [(0, IN_QKV, IN_F, 0), (1, IN_GATE, IN_TOTAL, 0), (1, IN_Z, IN_XBC, 3072), (1, IN_POOL, IN_QKV, 4096),
            (1, IN_XBC, IN_DT, REST_XBC0), (1, IN_F, IN_Z, REST_FDT0), (1, IN_DT, IN_GATE, REST_FDT0 + IN_Z - IN_F)]
    runs = []
    for dst, t0, t1, d0 in segs:
        for j in range(N_DEV):
            lo, hi = max(t0, shard * j), min(t1, shard * (j + 1))
            if lo < hi:
                runs.append((dst, d0 + lo - t0, j, lo - shard * j, hi - lo))
    return runs


def _repack_w_in(g, shard, *, name):
    _, L, K, P = g.shape
    tr = 256
    runs = _w_in_runs(shard)

    def body(g_ref, q_ref, r_ref):
        r_ref[:, REST_USED:] = jnp.zeros((tr, REST_W - REST_USED), g_ref.dtype)
        for dst, b, j, a, n in runs:
            (q_ref, r_ref)[dst][:, b:b + n] = g_ref[j, :, a:a + n]

    return pl.pallas_call(
        body, name=name, grid=(L, K // tr),
        in_specs=[pl.BlockSpec((N_DEV, None, tr, P), lambda l, i: (0, l, i, 0))],
        out_specs=[pl.BlockSpec((None, tr, QKV_W), lambda l, i: (l, i, 0)),
                   pl.BlockSpec((None, tr, REST_W), lambda l, i: (l, i, 0))],
        out_shape=[jax.ShapeDtypeStruct((L, K, QKV_W), g.dtype), jax.ShapeDtypeStruct((L, K, REST_W), g.dtype)],
        compiler_params=_cparams("parallel", "parallel"),
    )(g)


def _held_index(l, li, i, n):
    return jnp.where(l == li, i, jnp.where(l > li, n - 1, 0))


def _repack_dw_in(dwq, dwr, shard, *, name):
    L = len(dwq)
    K = dwq[0].shape[0]
    tr = 128
    nt = K // tr
    runs = _w_in_runs(shard)

    def body(*refs):
        srcs, o_ref = refs[:2 * L], refs[2 * L]
        l = pl.program_id(0)
        for li in range(L):
            @pl.when(l == li)
            def _():
                for dst, b, j, a, n in runs:
                    o_ref[j, :, a:a + n] = srcs[2 * li + dst][:, b:b + n]

    in_specs = []
    for li in range(L):
        hold = functools.partial(lambda li, l, i: (_held_index(l, li, i, nt), 0), li)
        in_specs += [pl.BlockSpec((tr, QKV_W), hold), pl.BlockSpec((tr, REST_W), hold)]
    args = [x for pair in zip(dwq, dwr) for x in pair]
    return pl.pallas_call(
        body, name=name, grid=(L, nt),
        in_specs=in_specs,
        out_specs=pl.BlockSpec((N_DEV, None, tr, shard), lambda l, i: (0, l, i, 0)),
        out_shape=jax.ShapeDtypeStruct((N_DEV, L, K, shard), dwq[0].dtype),
        compiler_params=_cparams("arbitrary", "arbitrary"),
    )(*args)


FLAT_W = 1024
ADAM_ROWS = 128


def _adam_math(w, g, m, v):
    m2 = ADAM_B1 * m + (1.0 - ADAM_B1) * g
    v2 = ADAM_B2 * v + (1.0 - ADAM_B2) * (g * g)
    m_hat = m2 / (1.0 - ADAM_B1 ** ADAM_STEP)
    v_hat = v2 / (1.0 - ADAM_B2 ** ADAM_STEP)
    delta = -ADAM_LR * (m_hat / (jnp.sqrt(v_hat) + ADAM_EPS) + ADAM_WD * w)
    return delta, m2, v2


def _row_div(R, align):
    best = R
    for t in range(align, min(R, ADAM_ROWS) + 1, align):
        if R % t == 0:
            best = t
    return best


def _sum8(ref_of):
    g = ref_of(0).astype(F32)
    for s in range(1, N_DEV):
        g = g + ref_of(s).astype(F32)
    return g


def _sum_adam(recvs, w, m, v, *, name):
    L, R, C = w.shape
    single = not isinstance(recvs, (list, tuple))
    recvs = [recvs] if single else list(recvs)
    tr = _row_div(R, SUBLANE * (4 // recvs[0].dtype.itemsize))
    nt = R // tr
    nr = len(recvs)

    def body(*refs):
        r_refs = refs[:nr]
        w_ref, m_ref, v_ref, g_ref, d_ref, m2_ref, v2_ref = refs[nr:]

        def run(ref_of):
            g = _sum8(ref_of)
            delta, m2, v2 = _adam_math(w_ref[...], g, m_ref[...], v_ref[...])
            g_ref[...] = g
            d_ref[...] = delta
            m2_ref[...] = m2
            v2_ref[...] = v2

        if single:
            run(lambda s: r_refs[0][s])
        else:
            l = pl.program_id(0)
            for li in range(L):
                @pl.when(l == li)
                def _():
                    run(lambda s: r_refs[li][s])

    if single:
        r_specs = [pl.BlockSpec((N_DEV, None, tr, C), lambda l, i: (0, l, i, 0))]
    else:
        r_specs = [pl.BlockSpec((N_DEV, tr, C), functools.partial(lambda li, l, i: (0, _held_index(l, li, i, nt), 0), li))
                   for li in range(L)]
    row = pl.BlockSpec((None, tr, C), lambda l, i: (l, i, 0))
    return pl.pallas_call(
        body, name=name, grid=(L, nt),
        in_specs=r_specs + [row, row, row],
        out_specs=[row] * 4,
        out_shape=[jax.ShapeDtypeStruct((L, R, C), F32)] * 4,
        compiler_params=_cparams("arbitrary", "arbitrary"),
    )(*recvs, w, m, v)


def _sum_parts(parts, *, name):
    _, R, C = parts.shape

    def body(p_ref, o_ref):
        o_ref[...] = _sum8(lambda s: p_ref[s])

    return pl.pallas_call(
        body, name=name, out_shape=jax.ShapeDtypeStruct((R, C), F32),
        in_specs=[pl.BlockSpec(memory_space=pltpu.VMEM)], out_specs=pl.BlockSpec(memory_space=pltpu.VMEM),
    )(parts)


def _adam(g, w, m, v, *, name):
    def body(g_ref, w_ref, m_ref, v_ref, d_ref, m2_ref, v2_ref):
        delta, m2, v2 = _adam_math(w_ref[...], g_ref[...], m_ref[...], v_ref[...])
        d_ref[...] = delta
        m2_ref[...] = m2
        v2_ref[...] = v2

    whole = pl.BlockSpec(memory_space=pltpu.VMEM)
    return pl.pallas_call(
        body, name=name, in_specs=[whole] * 4, out_specs=[whole] * 3,
        out_shape=[jax.ShapeDtypeStruct(g.shape, F32)] * 3,
    )(g, w, m, v)


SHARDED = ("w_in", "ffn_up", "p_pool", "p_attn", "p_ssd", "w_out", "ffn_down", "ssd_conv_w", "ffn_conv_w")
REPLICATED = ("norm_mix", "pool_mix", "pool_scale", "f_bias", "ssd_conv_b", "ssd_dt_bias", "ssd_a_log", "ssd_d",
              "ssd_norm", "norm_ffn", "ffn_conv_b", "norm_final")
WEIGHTS = ("norm_mix", "w_in", "pool_mix", "pool_scale", "f_bias", "ssd_conv_w", "ssd_conv_b", "ssd_dt_bias",
           "ssd_a_log", "ssd_d", "ssd_norm", "p_pool", "p_attn", "p_ssd", "w_out", "norm_ffn", "ffn_up",
           "ffn_conv_w", "ffn_conv_b", "ffn_down", "norm_final")


def _round_up(n, k):
    return -(-n // k) * k


def _pad_last(x, width):
    return jnp.pad(x, [(0, 0)] * (x.ndim - 1) + [(0, width - x.shape[-1])])


def _flat_rows(parts, rows):
    flat = jnp.concatenate([p.reshape(-1) for p in parts])
    return jnp.pad(flat, (0, rows * FLAT_W - flat.shape[0])).reshape(rows, FLAT_W)


def _lanes(v, lane0):
    return jnp.pad(v, (lane0, LANE - lane0 - v.shape[0]))[None]


def _cols_by_device(g):
    _, L, R, c = g.shape
    return jnp.moveaxis(g, 0, 2).reshape(L, R, N_DEV * c)


def _rows_by_device(g):
    _, L, r, C = g.shape
    return jnp.moveaxis(g, 0, 1).reshape(L, N_DEV * r, C)


def _gather_payload(W, l, shard_pad, up_pad):
    bf = lambda n: W[n][l].astype(BF16)
    return [_pad_last(bf("w_in"), shard_pad), _pad_last(bf("ffn_up"), up_pad), bf("p_pool"), bf("p_attn"),
            bf("p_ssd"), bf("w_out"), bf("ffn_down"), W["ssd_conv_w"][l], W["ffn_conv_w"][l]]


def _proj_params(g_win, W, l, shard):
    w_qkv, w_rest = _repack_w_in(g_win[:, None], shard, name=f"l{l}_repack_w_in")
    return dict(nm=W["norm_mix"][l][None], w_qkv=w_qkv[0], w_rest=w_rest[0], fb=_lanes(W["f_bias"][l], 0), shard=shard)


def _mixer_params(gathered, W, l, up_w, up_pad):
    g_up, g_pp, g_pa, g_ps, g_wo, g_dn, g_cw, g_fcw = [g[:, None] for g in gathered]
    half = N_DEV // 2
    w_down = _rows_by_device(g_dn)[0]
    w_down_p = jnp.pad(w_down.reshape(half, up_w, -1), ((0, 0), (0, up_pad - up_w), (0, 0))).reshape(half * up_pad, -1)
    fcb_p = _pad_last(W["ffn_conv_b"][l].reshape(N_DEV, up_w), up_pad).reshape(1, N_DEV * up_pad)
    return dict(
        mix=W["pool_mix"][l].astype(BF16), pscale=W["pool_scale"][l][None], cw=_cols_by_device(g_cw)[0],
        cb=W["ssd_conv_b"][l][None], dtb=_lanes(W["ssd_dt_bias"][l], DT_LANE0),
        alog=_lanes(W["ssd_a_log"][l], DT_LANE0), dsk=_lanes(W["ssd_d"][l], DT_LANE0), snw=W["ssd_norm"][l][None],
        p_pool=_cols_by_device(g_pp)[0], p_attn=_cols_by_device(g_pa)[0], p_ssd=_rows_by_device(g_ps)[0],
        w_out=_rows_by_device(g_wo)[0], nf=W["norm_ffn"][l][None], g_up=g_up, up_blocks=(N_DEV, up_w, up_pad),
        fcw=_cols_by_device(_pad_last(g_fcw, up_pad))[0], fcb=fcb_p, w_down=w_down_p)


def _layer_fwd(x, W, l, g_win, g_mixer, to_gather, dims):
    nm = lambda s: f"l{l}_{s}"
    P = _proj_params(g_win, W, l, dims[0])
    u = _rms_fwd(x, P["nm"], name=nm("rms_mix"))
    qkv = _mm(u, P["w_qkv"], out_dtype=BF16, name=nm("proj_qkv"))
    rest = _mm(u, P["w_rest"], name=nm("proj_rest"))
    ccol = _fox_pre(rest, REST_FDT_BLK, P["fb"], name=nm("fox_pre"))
    o, lrow, got = _attn_fwd(qkv, ccol, comm=("gather", to_gather) if to_gather else None, name=nm("attn"))
    if g_mixer is None:
        g_mixer, got = got[:len(SHARDED) - 1], got[len(SHARDED) - 1:]
    P.update(_mixer_params(g_mixer, W, l, *dims[1:]))
    d, ya0 = _pool_fwd(rest, REST_POOL_BLK, P["mix"], P["pscale"], name=nm("pool"))
    ya = _mm(ya0, P["p_pool"], name=nm("p_pool"))
    yb = _mm(o, P["p_attn"], name=nm("p_attn"))
    xc = _conv_silu_fwd(rest, REST_XBC0, SSD_CH, P["cw"], P["cb"], name=nm("ssd_conv"))
    y, hs = _ssd_fwd(xc, rest, REST_FDT_BLK, P["dtb"], P["alog"], P["dsk"], name=nm("ssd"))
    yc0 = _gate_norm_fwd(y, rest, REST_Z_BLK, P["snw"], name=nm("ssd_norm"))
    yc = _mm(yc0, P["p_ssd"], name=nm("p_ssd"))
    merged = _merge_fwd(rest, ya, yb, yc, name=nm("merge"))
    x1 = _mm(merged, P["w_out"], add=x, name=nm("w_out"))
    u2 = _rms_fwd(x1, P["nf"], name=nm("rms_ffn"))
    h0 = _mm_up_fwd(u2, P["g_up"], 0, name=nm("ffn_up"))
    a = _ffn_act_fwd(h0, P["fcw"], P["fcb"], name=nm("ffn_act"))
    x2 = _mm(a, P["w_down"], add=x1, name=nm("ffn_down"))
    saved = dict(x=x, u=u, qkv=qkv, rest=rest, d=d, ya0=ya0, ya=ya, ccol=ccol, o=o, lrow=lrow, yb=yb,
                 xc=xc, y=y, hs=hs, yc0=yc0, yc=yc, merged=merged, x1=x1, u2=u2, h0=h0, a=a)
    return x2, saved, P, got


EARLY_GRADS = ("ffn_up", "ffn_down", "ffn_conv_w", "p_pool", "p_attn", "p_ssd", "w_out")
LATE_GRADS = ("w_in", "ssd_conv_w")


def _layer_bwd(g, gb, sv, P, l, pending, last):
    nm = lambda s: f"l{l}_{s}_bwd"
    rest = sv["rest"]
    G = {}
    up_blocks, up_w, up_pad = P["up_blocks"]
    half = up_blocks // 2
    da = _mm(gb, P["w_down"], tb=True, name=nm("ffn_down_dx"))
    dwd = _mm(sv["a"], gb, ta=True, out_dtype=GRAD_WIRE, name=nm("ffn_down_dw"))
    G["ffn_down"] = dwd.reshape(half, up_pad, -1)[:, :up_w].reshape(N_DEV, -1, dwd.shape[1])
    dh, dfcw, dfcb = _ffn_act_bwd(sv["h0"], P["fcw"], P["fcb"], da, name=nm("ffn_act"))
    du2 = _mm_up_dx(dh, P["g_up"], 0, name=nm("ffn_up_dx"))
    G["ffn_up"] = _mm_up_dw(sv["u2"], dh, up_blocks, up_w, name=nm("ffn_up_dw"))
    taps = dfcw.shape[1]
    G["ffn_conv_w"] = jnp.moveaxis(dfcw.reshape(2, taps, half, up_pad)[..., :up_w], 2, 1).reshape(N_DEV, taps, up_w)
    G["ffn_conv_b"] = dfcb.reshape(2, half, up_pad)[..., :up_w].reshape(-1)
    dx1, dx1b, dnf = _rms_bwd(sv["x1"], P["nf"], du2, g, name=nm("rms_ffn"))
    G["norm_ffn"] = dnf[0]
    dmerged = _mm(dx1b, P["w_out"], tb=True, name=nm("w_out_dx"))
    dwo = _mm(sv["merged"], dx1b, ta=True, out_dtype=GRAD_WIRE, name=nm("w_out_dw"))
    G["w_out"] = dwo.reshape(N_DEV, -1, dwo.shape[1])
    dgl, dya, dyb, dyc = _merge_bwd(rest, sv["ya"], sv["yb"], sv["yc"], dmerged, name=nm("merge"))
    col_blocks = lambda dw: jnp.moveaxis(dw.reshape(dw.shape[0], N_DEV, -1), 1, 0)
    G["p_pool"] = col_blocks(_mm(sv["ya0"], dya, ta=True, out_dtype=GRAD_WIRE, name=nm("p_pool_dw")))
    dya0 = _mm(dya, P["p_pool"], tb=True, name=nm("p_pool_dx"))
    G["p_attn"] = col_blocks(_mm(sv["o"], dyb, ta=True, out_dtype=GRAD_WIRE, name=nm("p_attn_dw")))
    do = _mm(dyb, P["p_attn"], tb=True, out_dtype=BF16, name=nm("p_attn_dx"))
    dps = _mm(sv["yc0"], dyc, ta=True, out_dtype=GRAD_WIRE, name=nm("p_ssd_dw"))
    G["p_ssd"] = dps.reshape(N_DEV, -1, dps.shape[1])
    dyc0 = _mm(dyc, P["p_ssd"], tb=True, name=nm("p_ssd_dx"))
    dpv, dmix, dpsc = _pool_bwd(dya0, sv["d"], P["mix"], P["pscale"], name=nm("pool"))
    G["pool_mix"], G["pool_scale"] = dmix, dpsc[0]
    carried = [((n, l), G[n]) for n in EARLY_GRADS] + pending
    dq, dk, dv, dcq, dck, got = _attn_bwd(sv["qkv"], sv["o"], do, sv["lrow"], sv["ccol"],
                                          comm=("exchange", [a for _, a in carried]), name=nm("attn"))
    received = {key: r for (key, _), r in zip(carried, got)}
    dy, dz, dsnw = _gate_norm_bwd(sv["y"], rest, REST_Z_BLK, P["snw"], dyc0, name=nm("ssd_norm"))
    G["ssd_norm"] = dsnw[0]
    dxc, ddt, dpar = _ssd_bwd(sv["xc"], rest, REST_FDT_BLK, P["dtb"], P["alog"], P["dsk"], sv["hs"], dy, name=nm("ssd"))
    G["ssd_a_log"] = dpar[0, DT_LANE0:DT_LANE0 + SSD_HEADS]
    G["ssd_d"] = dpar[1, DT_LANE0:DT_LANE0 + SSD_HEADS]
    dxbc, dcw, dcb = _conv_silu_bwd(rest, REST_XBC0, SSD_CH, P["cw"], P["cb"], dxc, name=nm("ssd_conv"))
    G["ssd_conv_w"] = jnp.moveaxis(dcw.reshape(dcw.shape[0], N_DEV, -1), 1, 0)
    G["ssd_conv_b"] = dcb[0]
    dfdt, dfdtb = _fox_post(dcq, dck, rest, REST_FDT_BLK, P["fb"], ddt, name=nm("fox_post"))
    G["f_bias"] = dfdtb[0, :ATTN_HEADS]
    G["ssd_dt_bias"] = dfdtb[0, DT_LANE0:DT_LANE0 + SSD_HEADS]
    dqkv = jnp.concatenate([dq, dk, dv], axis=1)
    drest = jnp.concatenate([dgl, dz, dpv, dxbc, dfdt], axis=1)
    dwq = _mm(sv["u"], dqkv, ta=True, out_dtype=GRAD_WIRE, name=nm("proj_qkv_dw"))
    dwr = _mm(sv["u"], drest, ta=True, out_dtype=GRAD_WIRE, name=nm("proj_rest_dw"))
    G["w_in"] = _repack_dw_in([dwq], [dwr], P["shard"], name=nm("repack_dw_in"))[:, 0]
    du = _mm(dqkv, P["w_qkv"], tb=True, name=nm("proj_qkv_dx"))
    late = [((n, l), G[n]) for n in LATE_GRADS]
    if last:
        du, got = _mm(drest, P["w_rest"], tb=True, add=du, comm=("exchange", [a for _, a in late]),
                      name=nm("proj_rest_dx"))
        received.update({key: r for (key, _), r in zip(late, got)})
        late = []
    else:
        du = _mm(drest, P["w_rest"], tb=True, add=du, name=nm("proj_rest_dx"))
    dx, dxb, dnm = _rms_bwd(sv["x"], P["nm"], du, dx1, name=nm("rms_mix"))
    G["norm_mix"] = dnm[0]
    return dx, dxb, G, received, late


def kernel(x, norm_mix, w_in, pool_mix, pool_scale, f_bias, ssd_conv_w, ssd_conv_b, ssd_dt_bias, ssd_a_log, ssd_d, ssd_norm, p_pool, p_attn, p_ssd, w_out, norm_ffn, ffn_up, ffn_conv_w, ffn_conv_b, ffn_down, norm_final, loss_target, m_norm_mix, m_w_in, m_pool_mix, m_pool_scale, m_f_bias, m_ssd_conv_w, m_ssd_conv_b, m_ssd_dt_bias, m_ssd_a_log, m_ssd_d, m_ssd_norm, m_p_pool, m_p_attn, m_p_ssd, m_w_out, m_norm_ffn, m_ffn_up, m_ffn_conv_w, m_ffn_conv_b, m_ffn_down, m_norm_final, v_norm_mix, v_w_in, v_pool_mix, v_pool_scale, v_f_bias, v_ssd_conv_w, v_ssd_conv_b, v_ssd_dt_bias, v_ssd_a_log, v_ssd_d, v_ssd_norm, v_p_pool, v_p_attn, v_p_ssd, v_w_out, v_norm_ffn, v_ffn_up, v_ffn_conv_w, v_ffn_conv_b, v_ffn_down, v_norm_final):
    args = dict(locals())
    W = {n: args[n] for n in WEIGHTS}
    M = {n: args["m_" + n] for n in WEIGHTS}
    V = {n: args["v_" + n] for n in WEIGHTS}
    L = norm_mix.shape[0]
    shard = w_in.shape[-1]
    up_w = ffn_up.shape[-1]
    up_pad = _round_up(up_w, LANE)

    payload = lambda l: _gather_payload(W, l, _round_up(shard, LANE), up_pad)
    first = payload(0)
    g_win = _all_gather(first[:1], name="gather_w_in_l0")[0]
    g_mixer, to_gather = None, first[1:]
    h, saves, params = x[0], [], []
    for l in range(L):
        to_gather = to_gather + (payload(l + 1) if l + 1 < L else [])
        h, sv, P, got = _layer_fwd(h, W, l, g_win, g_mixer, to_gather, (shard, up_w, up_pad))
        saves.append(sv)
        params.append(P)
        if l + 1 < L:
            g_win, g_mixer, to_gather = got[0], got[1:], []
    loss_part, g, gb, dnfin = _loss_head(h, norm_final[None], loss_target[0], name="loss_head")

    G, received, pending = [None] * L, {}, []
    for l in reversed(range(L)):
        g, gb, G[l], got, pending = _layer_bwd(g, gb, saves[l], params[l], l, pending, last=l == 0)
        received.update(got)

    rep_grads = {n: jnp.stack([G[l][n] for l in range(L)]) for n in REPLICATED if n != "norm_final"}
    rep_grads["norm_final"] = dnfin[0]
    rep_size = sum(W[n].size for n in REPLICATED)
    rep_rows = _round_up(-(-rep_size // (N_DEV * FLAT_W)), SUBLANE)
    rep_send = _flat_rows([rep_grads[n] for n in REPLICATED], N_DEV * rep_rows).reshape(N_DEV, rep_rows, FLAT_W)
    last = _exchange([rep_send], name="exchange_replicated_grads")

    out = {}
    for n in SHARDED:
        shape = W[n].shape
        as3 = lambda a: a.reshape(L, -1, shape[-1])
        res = _sum_adam([received[(n, l)] for l in range(L)], as3(W[n]), as3(M[n]), as3(V[n]), name="sum_adam_" + n)
        for key, arr in zip(("grad_", "delta_", "new_m_", "new_v_"), res):
            out[key + n] = arr.reshape(shape)

    rep_sum = _sum_parts(last[-1], name="sum_replicated")
    rep_g = _all_gather([rep_sum], name="gather_replicated_grads")[0].reshape(N_DEV * rep_rows, FLAT_W)
    flat_rep = lambda D: _flat_rows([D[n] for n in REPLICATED], N_DEV * rep_rows)
    d_rp, m_rp, v_rp = _adam(rep_g, flat_rep(W), flat_rep(M), flat_rep(V), name="adam_replicated")
    off = 0
    for n in REPLICATED:
        size = W[n].size
        for key, arr in (("grad_", rep_g), ("delta_", d_rp), ("new_m_", m_rp), ("new_v_", v_rp)):
            out[key + n] = arr.reshape(-1)[off:off + size].reshape(W[n].shape)
        off += size

    loss = lax.psum(loss_part[0, 0], ("x", "y", "c"))
    return (loss, g[None], *[out["grad_" + n] for n in WEIGHTS], *[out["delta_" + n] for n in WEIGHTS],
            *[out["new_m_" + n] for n in WEIGHTS], *[out["new_v_" + n] for n in WEIGHTS])
```

```python
import functools

import jax
import jax.numpy as jnp
from jax import lax
from jax.experimental import pallas as pl
from jax.experimental.pallas import tpu as pltpu

F32 = jnp.float32
BF16 = jnp.bfloat16

LANE = 128
SUBLANE = 8
VMEM_LIMIT = 56 * 1024 * 1024

NORM_EPS = 1e-6
POOL_WINDOWS = (2, 4, 8, 16)
POOL_GROUP_DIM = 128
ATTN_HEADS = 8
ATTN_HEAD_DIM = 64
SSD_HEAD_DIM = 64
SSD_HEADS = 16
SSD_GROUPS = 2
SSD_STATE = 128
SSD_CHUNK = 128

ADAM_LR = 0.001
ADAM_B1 = 0.9
ADAM_B2 = 0.999
ADAM_EPS = 1e-08
ADAM_WD = 0.01
ADAM_STEP = 10


def _cparams(*sem):
    return pltpu.CompilerParams(dimension_semantics=tuple(sem), vmem_limit_bytes=VMEM_LIMIT)


def _tile(n, pref):
    if n <= pref:
        return n
    assert n % LANE == 0, n
    q = n // LANE
    best = 1
    for d in range(1, pref // LANE + 1):
        if q % d == 0:
            best = d
    return best * LANE


GRAD_WIRE = BF16


def _mm_call(args, in_specs, out_spec, out_shape, grid, *, ta, tb, nk, acc_shape, has_add, keep, name, comm=None):
    dims = (((0 if ta else 1,), (1 if tb else 0,)), ((), ()))
    nc = len(comm[1]) if comm else 0
    n_in = 2 + has_add

    def body(*refs):
        a_ref, b_ref = refs[0], refs[1]
        o_ref = refs[n_in + nc]
        acc_ref = refs[n_in + 2 * nc + 1] if nk > 1 else None
        carried = (refs[n_in:n_in + nc], refs[n_in + nc + 1:n_in + 2 * nc + 1], refs[n_in + 2 * nc + 1 + (nk > 1):])
        i, j, k = pl.program_id(0), pl.program_id(1), pl.program_id(2)
        if comm:
            _carry_start(comm[0], *carried, jnp.logical_and(jnp.logical_and(i == 0, j == 0), k == 0))
        d = lax.dot_general(a_ref[...], b_ref[...], dims, preferred_element_type=F32)

        def finish(r):
            if has_add:
                r = r + refs[2][...]
            if keep is not None:
                r = r[:, :keep]
            o_ref[...] = r.astype(o_ref.dtype)

        if nk == 1:
            finish(d)
        else:
            @pl.when(k == 0)
            def _():
                acc_ref[...] = d

            @pl.when(jnp.logical_and(k > 0, k < nk - 1))
            def _():
                acc_ref[...] += d

            @pl.when(k == nk - 1)
            def _():
                finish(acc_ref[...] + d)

        if comm:
            _carry_wait(comm[0], *carried, jnp.logical_and(jnp.logical_and(i == grid[0] - 1, j == grid[1] - 1),
                                                           k == nk - 1))

    res = pl.pallas_call(
        body, name=name, grid=grid, in_specs=in_specs + [ANY] * nc, out_specs=[out_spec] + [ANY] * nc,
        out_shape=[out_shape] + (_a2a_out_shapes(*comm) if comm else []),
        scratch_shapes=([pltpu.VMEM(acc_shape, F32)] if nk > 1 else []) + (_a2a_sems(nc) if comm else []),
        compiler_params=_cparams(*(("arbitrary",) * 3 if comm else ("parallel", "parallel", "arbitrary"))),
    )(*args, *(comm[1] if comm else []))
    return (res[0], list(res[1:])) if comm else res[0]


MM_VMEM_BUDGET = 40 * 1024 * 1024
MM_MAX_ROWS = 2048


MM_MIN_ROWS_WHOLE_K = 256


def _mm_rows(M, tn, tk, nk, out_bytes, has_add):
    best = None
    for tm in range(LANE, min(M, MM_MAX_ROWS) + 1, LANE):
        if M % tm:
            continue
        blocks = 2 * (tm * tk * 2 + tk * tn * 2 + tm * tn * out_bytes + (tm * tn * 4 if has_add else 0))
        blocks += tm * tn * 4 * (2 if nk > 1 else 1)
        if blocks <= MM_VMEM_BUDGET:
            best = tm
    return M if best is None else best


def _mm_tiles(M, N, K, out_bytes, has_add, whole_k_ok):
    tn = _tile(N, 1024)
    if whole_k_ok and K > 2048 and M % MM_MIN_ROWS_WHOLE_K == 0:
        tm = _mm_rows(M, tn, K, 1, out_bytes, has_add)
        need = 2 * (tm * K * 2 + K * tn * 2 + tm * tn * out_bytes + (tm * tn * 4 if has_add else 0)) + tm * tn * 4
        if tm >= MM_MIN_ROWS_WHOLE_K and need <= MM_VMEM_BUDGET:
            return tm, tn, K
    tk = _tile(K, 2048)
    return _mm_rows(M, tn, tk, K // tk, out_bytes, has_add), tn, tk


def _mm(a, b, *, ta=False, tb=False, add=None, out_dtype=F32, comm=None, name):
    assert a.dtype == BF16 and b.dtype == BF16, (a.dtype, b.dtype)
    if ta:
        K, M = a.shape
    else:
        M, K = a.shape
    if tb:
        N, K2 = b.shape
    else:
        K2, N = b.shape
    assert K == K2, (a.shape, b.shape, ta, tb)
    tm, tn, tk = _mm_tiles(M, N, K, jnp.dtype(out_dtype).itemsize, add is not None, whole_k_ok=not ta)
    a_spec = pl.BlockSpec((tk, tm), lambda i, j, k: (k, i)) if ta else pl.BlockSpec((tm, tk), lambda i, j, k: (i, k))
    b_spec = pl.BlockSpec((tn, tk), lambda i, j, k: (j, k)) if tb else pl.BlockSpec((tk, tn), lambda i, j, k: (k, j))
    in_specs, args = [a_spec, b_spec], [a, b]
    if add is not None:
        in_specs.append(pl.BlockSpec((tm, tn), lambda i, j, k: (i, j)))
        args.append(add)
    return _mm_call(args, in_specs, pl.BlockSpec((tm, tn), lambda i, j, k: (i, j)),
                    jax.ShapeDtypeStruct((M, N), out_dtype), (M // tm, N // tn, K // tk),
                    ta=ta, tb=tb, nk=K // tk, acc_shape=(tm, tn), has_add=add is not None, keep=None, name=name,
                    comm=comm)


def _half_of(b, half):
    hi = jnp.where(b >= half, 1, 0)
    return hi, b - half * hi


def _mm_up_fwd(u, g_up, l, *, name):
    S, K = u.shape
    nblk, _, _, bw = g_up.shape
    tm = _mm_rows(S, bw, K, 1, 4, False)
    return _mm_call([u, g_up],
                    [pl.BlockSpec((tm, K), lambda i, j, k: (i, 0)),
                     pl.BlockSpec((None, None, K, bw), lambda i, j, k: (j, l, 0, 0))],
                    pl.BlockSpec((tm, bw), lambda i, j, k: (i, j)),
                    jax.ShapeDtypeStruct((S, nblk * bw), F32), (S // tm, nblk, 1),
                    ta=False, tb=False, nk=1, acc_shape=(tm, bw), has_add=False, keep=None, name=name)


def _mm_up_dx(dh, g_up, l, *, name):
    _, S, F = dh.shape
    nblk, _, N, bw = g_up.shape
    half = nblk // 2
    tm = min(MM_MIN_ROWS_WHOLE_K, S)

    def body(dh_ref, g_ref, o_ref):
        acc = None
        for j in range(nblk):
            a = dh_ref[j // half, :, (j % half) * bw:(j % half + 1) * bw]
            d = lax.dot_general(a, g_ref[j], (((1,), (1,)), ((), ())), preferred_element_type=F32)
            acc = d if acc is None else acc + d
        o_ref[...] = acc

    return pl.pallas_call(
        body, name=name, grid=(S // tm,),
        in_specs=[pl.BlockSpec((2, tm, F), lambda i: (0, i, 0)),
                  pl.BlockSpec((nblk, None, N, bw), lambda i: (0, l, 0, 0))],
        out_specs=pl.BlockSpec((tm, N), lambda i: (i, 0)),
        out_shape=jax.ShapeDtypeStruct((S, N), F32),
        compiler_params=_cparams("parallel"),
    )(dh, g_up)


def _mm_up_dw(u, dh, nblk, keep, *, name):
    S, M = u.shape
    half = nblk // 2
    bw = dh.shape[2] // half
    tk = _tile(S, 2048)
    tm = _mm_rows(M, bw, tk, S // tk, jnp.dtype(GRAD_WIRE).itemsize, False)
    return _mm_call([u, dh],
                    [pl.BlockSpec((tk, tm), lambda i, j, k: (k, i)),
                     pl.BlockSpec((None, tk, bw), lambda i, j, k: (_half_of(j, half)[0], k, _half_of(j, half)[1]))],
                    pl.BlockSpec((None, tm, keep), lambda i, j, k: (j, i, 0)),
                    jax.ShapeDtypeStruct((nblk, M, keep), GRAD_WIRE), (M // tm, nblk, S // tk),
                    ta=True, tb=False, nk=S // tk, acc_shape=(tm, bw), has_add=False, keep=keep, name=name)


ROW_TILE = 512
HALO = 8
POOL_HALO = 16


def _row_tile(S):
    return min(ROW_TILE, S)


def _acc_out(ref, val, first):
    @pl.when(first)
    def _():
        ref[...] = val

    @pl.when(jnp.logical_not(first))
    def _():
        ref[...] += val


def _rms_fwd(x, w, *, name):
    S, D = x.shape
    ts = _row_tile(S)

    def body(x_ref, w_ref, u_ref):
        xv = x_ref[...]
        r = lax.rsqrt(jnp.mean(xv * xv, axis=-1, keepdims=True) + NORM_EPS)
        u_ref[...] = (xv * r * w_ref[...]).astype(BF16)

    return pl.pallas_call(
        body, name=name, grid=(S // ts,),
        in_specs=[pl.BlockSpec((ts, D), lambda i: (i, 0)), pl.BlockSpec((1, D), lambda i: (0, 0))],
        out_specs=pl.BlockSpec((ts, D), lambda i: (i, 0)),
        out_shape=jax.ShapeDtypeStruct((S, D), BF16),
        compiler_params=_cparams("parallel"),
    )(x, w)


def _rms_bwd(x, w, du, g, *, name):
    S, D = x.shape
    ts = _row_tile(S)

    def body(x_ref, w_ref, du_ref, g_ref, dx_ref, dxb_ref, dw_ref):
        xv = x_ref[...]
        r = lax.rsqrt(jnp.mean(xv * xv, axis=-1, keepdims=True) + NORM_EPS)
        n = xv * r
        duv = du_ref[...]
        dn = duv * w_ref[...]
        dx = g_ref[...] + r * (dn - n * jnp.mean(dn * n, axis=-1, keepdims=True))
        dx_ref[...] = dx
        dxb_ref[...] = dx.astype(BF16)
        _acc_out(dw_ref, jnp.sum(duv * n, axis=0, keepdims=True), pl.program_id(0) == 0)

    row = pl.BlockSpec((ts, D), lambda i: (i, 0))
    vec = pl.BlockSpec((1, D), lambda i: (0, 0))
    return pl.pallas_call(
        body, name=name, grid=(S // ts,),
        in_specs=[row, vec, row, row],
        out_specs=[row, row, vec],
        out_shape=[jax.ShapeDtypeStruct((S, D), F32), jax.ShapeDtypeStruct((S, D), BF16),
                   jax.ShapeDtypeStruct((1, D), F32)],
        compiler_params=_cparams("arbitrary"),
    )(x, w, du, g)


def _loss_head(x, w, target, *, name):
    S, D = x.shape
    ts = _row_tile(S)

    def body(x_ref, w_ref, t_ref, loss_ref, dx_ref, dxb_ref, dw_ref):
        xv = x_ref[...]
        wv = w_ref[...]
        r = lax.rsqrt(jnp.mean(xv * xv, axis=-1, keepdims=True) + NORM_EPS)
        n = xv * r
        err = n * wv - t_ref[...]
        part = 0.5 * jnp.sum(jnp.mean(err * err, axis=-1, keepdims=True), axis=0, keepdims=True)
        dy = err * (1.0 / D)
        dn = dy * wv
        dx = r * (dn - n * jnp.mean(dn * n, axis=-1, keepdims=True))
        dx_ref[...] = dx
        dxb_ref[...] = dx.astype(BF16)
        first = pl.program_id(0) == 0
        _acc_out(dw_ref, jnp.sum(dy * n, axis=0, keepdims=True), first)
        _acc_out(loss_ref, jnp.broadcast_to(part, loss_ref.shape), first)

    row = pl.BlockSpec((ts, D), lambda i: (i, 0))
    vec = pl.BlockSpec((1, D), lambda i: (0, 0))
    return pl.pallas_call(
        body, name=name, grid=(S // ts,),
        in_specs=[row, vec, row],
        out_specs=[pl.BlockSpec((SUBLANE, LANE), lambda i: (0, 0)), row, row, vec],
        out_shape=[jax.ShapeDtypeStruct((SUBLANE, LANE), F32), jax.ShapeDtypeStruct((S, D), F32),
                   jax.ShapeDtypeStruct((S, D), BF16), jax.ShapeDtypeStruct((1, D), F32)],
        compiler_params=_cparams("arbitrary"),
    )(x, w, target)


def _sigmoid(x):
    return 1.0 / (1.0 + jnp.exp(-x))


def _merge_fwd(gl, ya, yb, yc, *, name):
    S, D = ya.shape
    ts = _row_tile(S)

    def body(ga_ref, gb_ref, gc_ref, ya_ref, yb_ref, yc_ref, o_ref):
        m = (_sigmoid(ga_ref[...]) * ya_ref[...] + _sigmoid(gb_ref[...]) * yb_ref[...]
             + _sigmoid(gc_ref[...]) * yc_ref[...])
        o_ref[...] = m.astype(BF16)

    row = pl.BlockSpec((ts, D), lambda i: (i, 0))
    gspec = [pl.BlockSpec((ts, D), functools.partial(lambda b, i: (i, b), b)) for b in range(3)]
    return pl.pallas_call(
        body, name=name, grid=(S // ts,),
        in_specs=gspec + [row, row, row],
        out_specs=row,
        out_shape=jax.ShapeDtypeStruct((S, D), BF16),
        compiler_params=_cparams("parallel"),
    )(gl, gl, gl, ya, yb, yc)


def _merge_bwd(gl, ya, yb, yc, dm, *, name):
    S, D = ya.shape
    ts = _row_tile(S)

    def body(ga_ref, gb_ref, gc_ref, ya_ref, yb_ref, yc_ref, dm_ref, dgl_ref, da_ref, db_ref, dc_ref):
        dmv = dm_ref[...]
        for b, (g_ref, y_ref, dy_ref) in enumerate(((ga_ref, ya_ref, da_ref), (gb_ref, yb_ref, db_ref),
                                                     (gc_ref, yc_ref, dc_ref))):
            s = _sigmoid(g_ref[...])
            dy_ref[...] = (dmv * s).astype(BF16)
            dgl_ref[:, b * D:(b + 1) * D] = (dmv * y_ref[...] * s * (1.0 - s)).astype(BF16)

    row = pl.BlockSpec((ts, D), lambda i: (i, 0))
    gspec = [pl.BlockSpec((ts, D), functools.partial(lambda b, i: (i, b), b)) for b in range(3)]
    return pl.pallas_call(
        body, name=name, grid=(S // ts,),
        in_specs=gspec + [row, row, row, row],
        out_specs=[pl.BlockSpec((ts, 3 * D), lambda i: (i, 0)), row, row, row],
        out_shape=[jax.ShapeDtypeStruct((S, 3 * D), BF16)] + [jax.ShapeDtypeStruct((S, D), BF16)] * 3,
        compiler_params=_cparams("parallel"),
    )(gl, gl, gl, ya, yb, yc, dm)


POOL_WIDTH = 512


def _pool_cnt(t, w):
    return jnp.minimum(t + 1, w).astype(F32)


def _pool_fwd(rest, col, mix, scale, *, name):
    S = rest.shape[0]
    W, G, H = POOL_WIDTH, POOL_GROUP_DIM, POOL_HALO
    ts = _row_tile(S)

    def body(v_ref, h_ref, mix_ref, sc_ref, d_ref, y_ref, ext_ref):
        i = pl.program_id(0)
        cur = v_ref[...]
        ext_ref[0:H, :] = jnp.where(i > 0, h_ref[...], 0.0)
        ext_ref[H:, :] = cur
        t = i * ts + lax.broadcasted_iota(jnp.int32, (ts, 1), 0)
        for g, w in enumerate(POOL_WINDOWS):
            cols = slice(g * G, (g + 1) * G)
            acc = cur[:, cols]
            for k in range(1, w):
                acc = acc + ext_ref[pl.ds(H - k, ts), cols]
            db = (acc / _pool_cnt(t, w) - cur[:, cols]).astype(BF16)
            d_ref[:, cols] = db
            y = jnp.dot(db, mix_ref[g], preferred_element_type=F32) * sc_ref[:, cols]
            y_ref[:, cols] = y.astype(BF16)

    row = pl.BlockSpec((ts, W), lambda i: (i, 0))
    return pl.pallas_call(
        body, name=name, grid=(S // ts,),
        in_specs=[pl.BlockSpec((ts, W), lambda i: (i, col)),
                  pl.BlockSpec((H, W), lambda i: (jnp.maximum(i * (ts // H) - 1, 0), col)),
                  pl.BlockSpec((len(POOL_WINDOWS), G, G), lambda i: (0, 0, 0)),
                  pl.BlockSpec((1, W), lambda i: (0, 0))],
        out_specs=[row, row],
        out_shape=[jax.ShapeDtypeStruct((S, W), BF16)] * 2,
        scratch_shapes=[pltpu.VMEM((H + ts, W), F32)],
        compiler_params=_cparams("parallel"),
    )(rest, rest, mix, scale)


def _pool_bwd(dy, d, mix, scale, *, name):
    S = dy.shape[0]
    W, G, H = POOL_WIDTH, POOL_GROUP_DIM, POOL_HALO
    ts = _row_tile(S)
    n = S // ts
    NT = (((1,), (1,)), ((), ()))
    TN = (((0,), (0,)), ((), ()))

    def body(dy_ref, dyn_ref, d_ref, mix_ref, sc_ref, dv_ref, dmix_ref, dsc_ref, ext_ref):
        i = pl.program_id(0)

        @pl.when(i == 0)
        def _():
            dmix_ref[...] = jnp.zeros_like(dmix_ref)
            dsc_ref[...] = jnp.zeros_like(dsc_ref)

        t = i * ts + lax.broadcasted_iota(jnp.int32, (ts, 1), 0)
        tn = (i + 1) * ts + lax.broadcasted_iota(jnp.int32, (H, 1), 0)
        for g, w in enumerate(POOL_WINDOWS):
            cols = slice(g * G, (g + 1) * G)
            sc = sc_ref[:, cols]
            dyv = dy_ref[:, cols]
            db = d_ref[:, cols]
            mg = mix_ref[g]
            yp = jnp.dot(db, mg, preferred_element_type=F32)
            dsc_ref[:, cols] += jnp.sum(dyv * yp, axis=0, keepdims=True)
            dyp = (dyv * sc).astype(BF16)
            dd = lax.dot_general(dyp, mg, NT, preferred_element_type=F32)
            dmix_ref[g] += lax.dot_general(db, dyp, TN, preferred_element_type=F32)
            dyn = (jnp.where(i < n - 1, dyn_ref[:, cols], 0.0) * sc).astype(BF16)
            ddn = lax.dot_general(dyn, mg, NT, preferred_element_type=F32)
            e = dd / _pool_cnt(t, w)
            ext_ref[0:ts, cols] = e
            ext_ref[ts:, cols] = ddn / _pool_cnt(tn, w)
            acc = e
            for k in range(1, w):
                acc = acc + ext_ref[pl.ds(k, ts), cols]
            dv_ref[:, cols] = (acc - dd).astype(BF16)

    row = pl.BlockSpec((ts, W), lambda i: (i, 0))
    return pl.pallas_call(
        body, name=name, grid=(n,),
        in_specs=[row,
                  pl.BlockSpec((H, W), lambda i: (jnp.minimum((i + 1) * (ts // H), S // H - 1), 0)),
                  row,
                  pl.BlockSpec((len(POOL_WINDOWS), G, G), lambda i: (0, 0, 0)),
                  pl.BlockSpec((1, W), lambda i: (0, 0))],
        out_specs=[row, pl.BlockSpec((len(POOL_WINDOWS), G, G), lambda i: (0, 0, 0)),
                   pl.BlockSpec((1, W), lambda i: (0, 0))],
        out_shape=[jax.ShapeDtypeStruct((S, W), BF16),
                   jax.ShapeDtypeStruct((len(POOL_WINDOWS), G, G), F32),
                   jax.ShapeDtypeStruct((1, W), F32)],
        scratch_shapes=[pltpu.VMEM((ts + H, W), F32)],
        compiler_params=_cparams("arbitrary"),
    )(dy, dy, d, mix, scale)


def _conv_taps(ext_ref, w_ref, b_ref, rows, first):
    K = w_ref.shape[0]
    pre = b_ref[...] + w_ref[K - 1:K, :] * ext_ref[pl.ds(first, rows), :]
    for k in range(K - 1):
        pre = pre + w_ref[k:k + 1, :] * ext_ref[pl.ds(first - (K - 1) + k, rows), :]
    return pre


CONV_ROW_TILE = 1024
ROW_CHUNK = 40


def _chunks(rows):
    ch = max(c for c in range(SUBLANE, ROW_CHUNK + 1, SUBLANE) if rows % c == 0)
    return [(r0, ch) for r0 in range(0, rows, ch)]


def _conv_bwd_taps(g_ref, ext_ref, w_ref, dx_store, ts):
    K = w_ref.shape[0]
    db, dws = 0.0, [0.0] * K
    fold = lambda v: jnp.sum(v.reshape(v.shape[0] // SUBLANE, SUBLANE, v.shape[1]), axis=0)
    for r0, ch in _chunks(ts):
        dx = w_ref[K - 1:K, :] * g_ref[pl.ds(r0, ch), :]
        for k in range(K - 1):
            dx = dx + w_ref[k:k + 1, :] * g_ref[pl.ds(r0 + K - 1 - k, ch), :]
        dx_store(r0, ch, dx.astype(BF16))
        gc = g_ref[pl.ds(r0, ch), :]
        db = db + fold(gc)
        for k in range(K):
            dws[k] = dws[k] + fold(gc * ext_ref[pl.ds(HALO - (K - 1) + k + r0, ch), :])
    total = lambda v: jnp.sum(v, axis=0, keepdims=True)
    return total(db), jnp.concatenate([total(d) for d in dws], axis=0)


def _prev_halo_spec(ts, tc, colfn):
    return pl.BlockSpec((HALO, tc), lambda *g: (jnp.maximum(g[-1] * (ts // HALO) - 1, 0), colfn(*g)))


def _conv_silu_fwd(rest, col0, C, w, b, *, name):
    S = rest.shape[0]
    ts, tc = min(CONV_ROW_TILE, S), 512
    assert col0 % tc == 0 and C % tc == 0
    cb = col0 // tc

    def body(x_ref, h_ref, w_ref, b_ref, o_ref, ext_ref):
        i = pl.program_id(1)
        ext_ref[0:HALO, :] = jnp.where(i > 0, h_ref[...], 0.0)
        ext_ref[HALO:, :] = x_ref[...]
        for r0, ch in _chunks(ts):
            pre = _conv_taps(ext_ref, w_ref, b_ref, ch, HALO + r0)
            o_ref[pl.ds(r0, ch), :] = pre * _sigmoid(pre)

    K = w.shape[0]
    return pl.pallas_call(
        body, name=name, grid=(C // tc, S // ts),
        in_specs=[pl.BlockSpec((ts, tc), lambda j, i: (i, cb + j)),
                  _prev_halo_spec(ts, tc, lambda j, i: cb + j),
                  pl.BlockSpec((K, tc), lambda j, i: (0, j)),
                  pl.BlockSpec((1, tc), lambda j, i: (0, j))],
        out_specs=pl.BlockSpec((ts, tc), lambda j, i: (i, j)),
        out_shape=jax.ShapeDtypeStruct((S, C), F32),
        scratch_shapes=[pltpu.VMEM((HALO + ts, tc), F32)],
        compiler_params=_cparams("parallel", "parallel"),
    )(rest, rest, w, b)


def _dsilu(pre):
    s = _sigmoid(pre)
    return s, s * (1.0 + pre * (1.0 - s))


def _conv_silu_bwd(rest, col0, C, w, b, dy, *, name):
    S = rest.shape[0]
    ts, tc = min(CONV_ROW_TILE, S), 512
    cb = col0 // tc
    n = S // ts
    K = w.shape[0]
    R = ts + HALO

    def body(x_ref, hp_ref, hn_ref, w_ref, b_ref, dy_ref, dyn_ref, dx_ref, dw_ref, db_ref, ext_ref, g_ref):
        i = pl.program_id(1)
        last = i == n - 1
        ext_ref[0:HALO, :] = jnp.where(i > 0, hp_ref[...], 0.0)
        ext_ref[HALO:HALO + ts, :] = x_ref[...]
        ext_ref[HALO + ts:, :] = jnp.where(last, 0.0, hn_ref[...])
        g_ref[0:ts, :] = dy_ref[...]
        g_ref[ts:, :] = jnp.where(last, 0.0, dyn_ref[...])
        for r0, ch in _chunks(R):
            pre = _conv_taps(ext_ref, w_ref, b_ref, ch, HALO + r0)
            g_ref[pl.ds(r0, ch), :] = g_ref[pl.ds(r0, ch), :] * _dsilu(pre)[1]

        def dx_store(r0, ch, v):
            dx_ref[pl.ds(r0, ch), :] = v

        db, dw = _conv_bwd_taps(g_ref, ext_ref, w_ref, dx_store, ts)
        _acc_out(db_ref, db, i == 0)
        _acc_out(dw_ref, dw, i == 0)

    nxt = lambda cf: pl.BlockSpec((HALO, tc), lambda j, i: (jnp.minimum((i + 1) * (ts // HALO), S // HALO - 1), cf(j)))
    return pl.pallas_call(
        body, name=name, grid=(C // tc, n),
        in_specs=[pl.BlockSpec((ts, tc), lambda j, i: (i, cb + j)),
                  _prev_halo_spec(ts, tc, lambda j, i: cb + j),
                  nxt(lambda j: cb + j),
                  pl.BlockSpec((K, tc), lambda j, i: (0, j)),
                  pl.BlockSpec((1, tc), lambda j, i: (0, j)),
                  pl.BlockSpec((ts, tc), lambda j, i: (i, j)),
                  nxt(lambda j: j)],
        out_specs=[pl.BlockSpec((ts, tc), lambda j, i: (i, j)),
                   pl.BlockSpec((K, tc), lambda j, i: (0, j)),
                   pl.BlockSpec((1, tc), lambda j, i: (0, j))],
        out_shape=[jax.ShapeDtypeStruct((S, C), BF16), jax.ShapeDtypeStruct((K, C), F32),
                   jax.ShapeDtypeStruct((1, C), F32)],
        scratch_shapes=[pltpu.VMEM((HALO + R, tc), F32), pltpu.VMEM((R, tc), F32)],
        compiler_params=_cparams("parallel", "arbitrary"),
    )(rest, rest, rest, w, b, dy, dy)


FFN_TC = 512


def _ffn_act_fwd(h0, w, b, *, name):
    S, F2 = h0.shape
    F = F2 // 2
    ts, tc = min(CONV_ROW_TILE, S), FFN_TC
    nb = F // tc
    K = w.shape[0]

    def body(xg_ref, hg_ref, xv_ref, hv_ref, wg_ref, wv_ref, bg_ref, bv_ref, o_ref, eg_ref, ev_ref):
        i = pl.program_id(1)
        for x_ref, h_ref, e_ref in ((xg_ref, hg_ref, eg_ref), (xv_ref, hv_ref, ev_ref)):
            e_ref[0:HALO, :] = jnp.where(i > 0, h_ref[...], 0.0)
            e_ref[HALO:, :] = x_ref[...]
        for r0, ch in _chunks(ts):
            pg = _conv_taps(eg_ref, wg_ref, bg_ref, ch, HALO + r0)
            pv = _conv_taps(ev_ref, wv_ref, bv_ref, ch, HALO + r0)
            o_ref[pl.ds(r0, ch), :] = (pg * _sigmoid(pg) * pv).astype(BF16)

    def half(off):
        return [pl.BlockSpec((ts, tc), lambda j, i: (i, off + j)), _prev_halo_spec(ts, tc, lambda j, i: off + j)]

    wspec = lambda off: pl.BlockSpec((K, tc), lambda j, i: (0, off + j))
    bspec = lambda off: pl.BlockSpec((1, tc), lambda j, i: (0, off + j))
    return pl.pallas_call(
        body, name=name, grid=(nb, S // ts),
        in_specs=half(0) + half(nb) + [wspec(0), wspec(nb), bspec(0), bspec(nb)],
        out_specs=pl.BlockSpec((ts, tc), lambda j, i: (i, j)),
        out_shape=jax.ShapeDtypeStruct((S, F), BF16),
        scratch_shapes=[pltpu.VMEM((HALO + ts, tc), F32)] * 2,
        compiler_params=_cparams("parallel", "parallel"),
    )(h0, h0, h0, h0, w, w, b, b)


def _ffn_act_bwd(h0, w, b, da, *, name):
    S, F2 = h0.shape
    F = F2 // 2
    ts, tc = min(CONV_ROW_TILE, S), FFN_TC
    nb = F // tc
    n = S // ts
    K = w.shape[0]
    R = ts + HALO

    def body(xg_ref, pg_ref, ng_ref, xv_ref, pv_ref, nv_ref, wg_ref, wv_ref, bg_ref, bv_ref, da_ref, dan_ref,
             dx_ref, dw_ref, db_ref, eg_ref, ev_ref, gg_ref, gv_ref):
        i = pl.program_id(1)
        last = i == n - 1
        for x_ref, p_ref, n_ref, e_ref in ((xg_ref, pg_ref, ng_ref, eg_ref), (xv_ref, pv_ref, nv_ref, ev_ref)):
            e_ref[0:HALO, :] = jnp.where(i > 0, p_ref[...], 0.0)
            e_ref[HALO:HALO + ts, :] = x_ref[...]
            e_ref[HALO + ts:, :] = jnp.where(last, 0.0, n_ref[...])
        gg_ref[0:ts, :] = da_ref[...]
        gg_ref[ts:, :] = jnp.where(last, 0.0, dan_ref[...])
        for r0, ch in _chunks(R):
            rows = pl.ds(r0, ch)
            dav = gg_ref[rows, :]
            pg = _conv_taps(eg_ref, wg_ref, bg_ref, ch, HALO + r0)
            pv = _conv_taps(ev_ref, wv_ref, bv_ref, ch, HALO + r0)
            s, ds = _dsilu(pg)
            gg_ref[rows, :] = dav * pv * ds
            gv_ref[rows, :] = dav * pg * s
        for h, (g_ref, e_ref, w_ref) in enumerate(((gg_ref, eg_ref, wg_ref), (gv_ref, ev_ref, wv_ref))):
            def dx_store(r0, ch, v):
                dx_ref[h, pl.ds(r0, ch), :] = v

            db, dw = _conv_bwd_taps(g_ref, e_ref, w_ref, dx_store, ts)
            _acc_out(db_ref.at[h], db, i == 0)
            _acc_out(dw_ref.at[h], dw, i == 0)

    nxt = lambda off: pl.BlockSpec((HALO, tc), lambda j, i: (jnp.minimum((i + 1) * (ts // HALO), S // HALO - 1), off + j))

    def half(off):
        return [pl.BlockSpec((ts, tc), lambda j, i: (i, off + j)), _prev_halo_spec(ts, tc, lambda j, i: off + j), nxt(off)]

    wspec = lambda off: pl.BlockSpec((K, tc), lambda j, i: (0, off + j))
    bspec = lambda off: pl.BlockSpec((1, tc), lambda j, i: (0, off + j))
    return pl.pallas_call(
        body, name=name, grid=(nb, n),
        in_specs=half(0) + half(nb) + [wspec(0), wspec(nb), bspec(0), bspec(nb),
                                       pl.BlockSpec((ts, tc), lambda j, i: (i, j)), nxt(0)],
        out_specs=[pl.BlockSpec((2, ts, tc), lambda j, i: (0, i, j)),
                   pl.BlockSpec((2, K, tc), lambda j, i: (0, 0, j)),
                   pl.BlockSpec((2, 1, tc), lambda j, i: (0, 0, j))],
        out_shape=[jax.ShapeDtypeStruct((2, S, F), BF16), jax.ShapeDtypeStruct((2, K, F), F32),
                   jax.ShapeDtypeStruct((2, 1, F), F32)],
        scratch_shapes=[pltpu.VMEM((HALO + R, tc), F32)] * 2 + [pltpu.VMEM((R, tc), F32)] * 2,
        compiler_params=_cparams("parallel", "arbitrary"),
    )(h0, h0, h0, h0, h0, h0, w, w, b, b, da, da)


CUM_TILE = 256
ATT_TILE = 512
ATT_TILE_FWD = 1024
ATT_KEY_SPLIT = 1
ATT_QUERY_SPLIT = 2
NEG = -1e30
N_PAIR = ATTN_HEADS // 2
NT_DIMS = (((1,), (1,)), ((), ()))


def _split3(x):
    a = x.astype(BF16)
    r = x - a.astype(F32)
    b = r.astype(BF16)
    c = (r - b.astype(F32)).astype(BF16)
    return a, b, c


def _tri_dot(tri, x):
    return sum(jnp.dot(tri, p, preferred_element_type=F32) for p in _split3(x))


def _log_sigmoid(x):
    return jnp.minimum(x, 0.0) - jnp.log(1.0 + jnp.exp(-jnp.abs(x)))


def _col(v, idx, lane):
    return jnp.sum(jnp.where(lane == idx, v, 0.0), axis=1, keepdims=True)


def _fox_pre(rest, col, fb, *, name):
    S = rest.shape[0]
    ts = min(CUM_TILE, S)

    def body(f_ref, fb_ref, ccol_ref, carry_ref):
        i = pl.program_id(0)

        @pl.when(i == 0)
        def _():
            carry_ref[...] = jnp.zeros_like(carry_ref)

        lane = lax.broadcasted_iota(jnp.int32, (1, LANE), 1)
        logf = jnp.where(lane < ATTN_HEADS, _log_sigmoid(f_ref[...] + fb_ref[...]), 0.0)
        r = lax.broadcasted_iota(jnp.int32, (ts, ts), 0)
        c = lax.broadcasted_iota(jnp.int32, (ts, ts), 1)
        tri = jnp.where(c <= r, 1.0, 0.0).astype(BF16)
        cs = _tri_dot(tri, logf) + carry_ref[...]
        ccol_ref[...] = cs
        carry_ref[...] = cs[ts - 1:ts, :]

    return pl.pallas_call(
        body, name=name, grid=(S // ts,),
        in_specs=[pl.BlockSpec((ts, LANE), lambda i: (i, col)), pl.BlockSpec((1, LANE), lambda i: (0, 0))],
        out_specs=pl.BlockSpec((ts, LANE), lambda i: (i, 0)),
        out_shape=jax.ShapeDtypeStruct((S, LANE), F32),
        scratch_shapes=[pltpu.VMEM((1, LANE), F32)],
        compiler_params=_cparams("arbitrary"),
    )(rest, fb)


def _fox_post(dcq, dck, rest, col, fb, ddt, *, name):
    S = rest.shape[0]
    ts = min(CUM_TILE, S)
    n = S // ts

    def body(dcq_ref, dck_ref, f_ref, fb_ref, ddt_ref, o_ref, db_ref, carry_ref):
        i = pl.program_id(0)

        @pl.when(i == 0)
        def _():
            carry_ref[...] = jnp.zeros_like(carry_ref)

        lane = lax.broadcasted_iota(jnp.int32, (1, LANE), 1)
        dc = jnp.zeros((ts, LANE), F32)
        for h in range(ATTN_HEADS):
            d = dcq_ref[h // 2] - dck_ref[h // 2]
            dc = jnp.where(lane == h, _col(d, h % 2, lane), dc)
        r = lax.broadcasted_iota(jnp.int32, (ts, ts), 0)
        c = lax.broadcasted_iota(jnp.int32, (ts, ts), 1)
        tri = jnp.where(c >= r, 1.0, 0.0).astype(BF16)
        rc = _tri_dot(tri, dc) + carry_ref[...]
        carry_ref[...] = rc[0:1, :]
        df = rc * _sigmoid(-(f_ref[...] + fb_ref[...]))
        out = jnp.where(lane < ATTN_HEADS, df, ddt_ref[...])
        o_ref[...] = out.astype(BF16)
        _acc_out(db_ref, jnp.sum(out, axis=0, keepdims=True), i == 0)

    rev = lambda i: n - 1 - i
    return pl.pallas_call(
        body, name=name, grid=(n,),
        in_specs=[pl.BlockSpec((N_PAIR, ts, LANE), lambda i: (0, rev(i), 0)),
                  pl.BlockSpec((N_PAIR, ts, LANE), lambda i: (0, rev(i), 0)),
                  pl.BlockSpec((ts, LANE), lambda i: (rev(i), col)),
                  pl.BlockSpec((1, LANE), lambda i: (0, 0)),
                  pl.BlockSpec((ts, LANE), lambda i: (rev(i), 0))],
        out_specs=[pl.BlockSpec((ts, LANE), lambda i: (rev(i), 0)), pl.BlockSpec((1, LANE), lambda i: (0, 0))],
        out_shape=[jax.ShapeDtypeStruct((S, LANE), BF16), jax.ShapeDtypeStruct((1, LANE), F32)],
        scratch_shapes=[pltpu.VMEM((1, LANE), F32)],
        compiler_params=_cparams("arbitrary"),
    )(dcq, dck, rest, fb, ddt)


def _head_masks():
    lane = lax.broadcasted_iota(jnp.int32, (1, LANE), 1)
    return lane, (lane < ATTN_HEAD_DIM, lane >= ATTN_HEAD_DIM)


def _causal_mask(t):
    r = lax.broadcasted_iota(jnp.int32, (t, t), 0)
    c = lax.broadcasted_iota(jnp.int32, (t, t), 1)
    return r, c


AUG_ONE = 0
AUG_C = 3


def _free_lane0(hh):
    return ATTN_HEAD_DIM if hh == 0 else 0


def _augment(x, head_mask, lane, f0, c, key_side):
    terms = _split3(c)
    one = jnp.ones((), BF16)
    ones_at, terms_at = (AUG_ONE, AUG_C) if key_side else (AUG_C, AUG_ONE)
    out = jnp.where(head_mask, x, jnp.zeros((), BF16))
    for k in range(3):
        out = jnp.where(lane == f0 + ones_at + k, one, out)
        out = jnp.where(lane == f0 + terms_at + k, terms[k], out)
    return out


def _carry_start(kind, in_refs, out_refs, sems, first):
    @pl.when(first)
    def _():
        if kind == "gather":
            _gather_phase(0, in_refs, out_refs, sems)
        else:
            _a2a_start(in_refs, out_refs, sems)


def _carry_mid(kind, in_refs, out_refs, sems, mid):
    if kind == "gather":
        @pl.when(mid)
        def _():
            _gather_phase(1, in_refs, out_refs, sems)


def _carry_wait(kind, in_refs, out_refs, sems, last):
    @pl.when(last)
    def _():
        if kind == "gather":
            _gather_phase(2, in_refs, out_refs, sems)
        else:
            _a2a_wait(in_refs, out_refs, sems)


def _attn_fwd(qkv, ccol, *, comm=None, name):
    S = qkv.shape[0]
    t = min(ATT_TILE_FWD, S)
    n = S // t
    scale = ATTN_HEAD_DIM ** -0.5

    nc = len(comm[1]) if comm else 0

    def body(*refs):
        q_ref, k_ref, v_ref, cc_ref = refs[:4]
        o_ref, lrow_ref = refs[4 + nc:6 + nc]
        kaug_ref, vt_ref, m_ref, l_ref, acc_ref = refs[6 + 2 * nc:11 + 2 * nc]
        j, i = pl.program_id(0), pl.program_id(1)
        lane, masks = _head_masks()
        if comm:
            carried = (refs[4:4 + nc], refs[6 + nc:6 + 2 * nc], refs[11 + 2 * nc:])
            _carry_start(comm[0], *carried, jnp.logical_and(j == 0, i == 0))
            _carry_mid(comm[0], *carried, jnp.logical_and(j == N_PAIR - 1, i == 0))

        @pl.when(i == 0)
        def _():
            def fill(b, carry):
                off = pl.multiple_of(b * t, t)
                kblk = k_ref[pl.ds(off, t), :]
                ccb = cc_ref[pl.ds(off, t), :]
                for hh in range(2):
                    ck = _col(ccb, 2 * j + hh, lane)
                    kaug_ref[hh, pl.ds(off, t), :] = _augment(kblk, masks[hh], lane, _free_lane0(hh), -ck, True)
                vt_ref[:, pl.ds(off, t)] = v_ref[pl.ds(off, t), :].astype(F32).T.astype(BF16)
                return carry
            lax.fori_loop(0, n, fill, 0)

        q2 = q_ref[...] * scale
        ccq = cc_ref[pl.ds(pl.multiple_of(i * t, t), t), :]
        qaug = [_augment(q2, masks[hh], lane, _free_lane0(hh), _col(ccq, 2 * j + hh, lane), False) for hh in range(2)]
        m_ref[...] = jnp.full(m_ref.shape, NEG, F32)
        l_ref[...] = jnp.zeros_like(l_ref)
        acc_ref[...] = jnp.zeros_like(acc_ref)

        th = t // ATT_KEY_SPLIT

        def step(kb, masked):
            for part in range(ATT_KEY_SPLIT):
                sub_step(pl.multiple_of(kb * t + part * th, th), part * th if masked else None)

        def sub_step(off, mask_shift):
            vt = vt_ref[:, pl.ds(off, th)]
            for hh in range(2):
                st = lax.dot_general(kaug_ref[hh, pl.ds(off, th), :], qaug[hh], NT_DIMS, preferred_element_type=F32)
                if mask_shift is not None:
                    r = lax.broadcasted_iota(jnp.int32, (th, t), 0) + mask_shift
                    c = lax.broadcasted_iota(jnp.int32, (th, t), 1)
                    st = jnp.where(r <= c, st, NEG)
                m_old = m_ref[hh]
                m_new = jnp.maximum(m_old, jnp.max(st, axis=0, keepdims=True))
                p = jnp.exp(st - m_new)
                alpha = jnp.exp(m_old - m_new)
                l_ref[hh] = alpha * l_ref[hh] + jnp.sum(p, axis=0, keepdims=True)
                acc_ref[hh] = alpha * acc_ref[hh] + jnp.dot(vt, p.astype(BF16), preferred_element_type=F32)
                m_ref[hh] = m_new

        def loop_body(kb, carry):
            step(kb, False)
            return carry

        lax.fori_loop(0, i, loop_body, 0)
        step(i, True)
        drow = lax.broadcasted_iota(jnp.int32, (LANE, 1), 0)
        ot = jnp.where(drow < ATTN_HEAD_DIM, acc_ref[0] / l_ref[0], acc_ref[1] / l_ref[1])
        o_ref[...] = ot.T.astype(BF16)
        for hh in range(2):
            lrow_ref[hh] = m_ref[hh] + jnp.log(l_ref[hh])
        if comm:
            _carry_wait(comm[0], *carried, jnp.logical_and(j == N_PAIR - 1, i == n - 1))

    res = pl.pallas_call(
        body, name=name, grid=(N_PAIR, n),
        in_specs=[pl.BlockSpec((t, LANE), lambda j, i: (i, j)),
                  pl.BlockSpec((S, LANE), lambda j, i: (0, N_PAIR + j)),
                  pl.BlockSpec((S, LANE), lambda j, i: (0, 2 * N_PAIR + j)),
                  pl.BlockSpec((S, LANE), lambda j, i: (0, 0))] + [ANY] * nc,
        out_specs=[pl.BlockSpec((t, LANE), lambda j, i: (i, j)),
                   pl.BlockSpec((2, 1, t), lambda j, i: (j, 0, i))] + [ANY] * nc,
        out_shape=[jax.ShapeDtypeStruct((S, N_PAIR * LANE), BF16), jax.ShapeDtypeStruct((ATTN_HEADS, 1, S), F32)]
        + (_a2a_out_shapes(*comm) if comm else []),
        scratch_shapes=[pltpu.VMEM((2, S, LANE), BF16), pltpu.VMEM((LANE, S), BF16),
                        pltpu.VMEM((2, 1, t), F32), pltpu.VMEM((2, 1, t), F32), pltpu.VMEM((2, LANE, t), F32)]
        + (_a2a_sems(nc) if comm else []),
        compiler_params=_cparams("arbitrary" if comm else "parallel", "arbitrary"),
    )(qkv, qkv, qkv, ccol, *(comm[1] if comm else []))
    return res[0], res[1], list(res[2:])


def _attn_bwd(qkv, o, do, lrow, ccol, *, comm=None, name):
    S = qkv.shape[0]
    t = min(ATT_TILE, S)
    n = S // t
    scale = ATTN_HEAD_DIM ** -0.5

    nc = len(comm[1]) if comm else 0

    def body(*refs):
        q_ref, k_ref, v_ref, o_ref, do_ref, lrow_ref, cc_ref = refs[:7]
        dq_ref, dk_ref, dv_ref, dcq_ref, dck_ref = refs[7 + nc:12 + nc]
        qaug_ref, drow_ref, dqt_ref, dka_ref, dva_ref = refs[12 + 2 * nc:17 + 2 * nc]
        carried = (refs[7:7 + nc], refs[12 + nc:12 + 2 * nc], refs[17 + 2 * nc:])
        j, kb = pl.program_id(0), pl.program_id(1)
        lane, masks = _head_masks()
        if comm:
            _carry_start(comm[0], *carried, jnp.logical_and(j == 0, kb == 0))
        drows = lax.broadcasted_iota(jnp.int32, (LANE, 1), 0)

        @pl.when(kb == 0)
        def _():
            dqt_ref[...] = jnp.zeros_like(dqt_ref)

            def fill(b, carry):
                off = pl.multiple_of(b * t, t)
                q2 = q_ref[pl.ds(off, t), :] * scale
                ccb = cc_ref[pl.ds(off, t), :]
                prod_t = (do_ref[pl.ds(off, t), :].astype(F32) * o_ref[pl.ds(off, t), :].astype(F32)).T
                lo = jnp.sum(jnp.where(drows < ATTN_HEAD_DIM, prod_t, 0.0), axis=0, keepdims=True)
                drow_ref[0, :, pl.ds(off, t)] = lo
                drow_ref[1, :, pl.ds(off, t)] = jnp.sum(prod_t, axis=0, keepdims=True) - lo
                for hh in range(2):
                    cq = _col(ccb, 2 * j + hh, lane)
                    qaug_ref[hh, pl.ds(off, t), :] = _augment(q2, masks[hh], lane, _free_lane0(hh), cq, False)
                return carry
            lax.fori_loop(0, n, fill, 0)

        koff = pl.multiple_of(kb * t, t)
        kblk = k_ref[...]
        v2 = v_ref[...]
        cck = cc_ref[pl.ds(koff, t), :]
        kaug = [_augment(kblk, masks[hh], lane, _free_lane0(hh), -_col(cck, 2 * j + hh, lane), True) for hh in range(2)]
        kaug_t = [ka.astype(F32).T.astype(BF16) for ka in kaug]
        vh = [jnp.where(mk, v2, jnp.zeros((), BF16)) for mk in masks]
        dka_ref[...] = jnp.zeros_like(dka_ref)
        dva_ref[...] = jnp.zeros_like(dva_ref)

        tq = t // ATT_QUERY_SPLIT

        def step(qb, masked):
            for part in range(ATT_QUERY_SPLIT):
                sub_step(pl.multiple_of(qb * t + part * tq, tq), part * tq if masked else None)

        def sub_step(off, mask_shift):
            doblk = do_ref[pl.ds(off, tq), :]
            for hh in range(2):
                qa = qaug_ref[hh, pl.ds(off, tq), :]
                st = lax.dot_general(kaug[hh], qa, NT_DIMS, preferred_element_type=F32)
                if mask_shift is not None:
                    r = lax.broadcasted_iota(jnp.int32, (t, tq), 0)
                    c = lax.broadcasted_iota(jnp.int32, (t, tq), 1) + mask_shift
                    st = jnp.where(r <= c, st, NEG)
                pt = jnp.exp(st - lrow_ref[hh, :, pl.ds(off, tq)])
                dpt = lax.dot_general(vh[hh], doblk, NT_DIMS, preferred_element_type=F32)
                dst = (pt * (dpt - drow_ref[hh, :, pl.ds(off, tq)])).astype(BF16)
                dva_ref[hh] += jnp.dot(pt.astype(BF16), doblk, preferred_element_type=F32)
                dka_ref[hh] += jnp.dot(dst, qa, preferred_element_type=F32)
                dqt_ref[hh, :, pl.ds(off, tq)] += jnp.dot(kaug_t[hh], dst, preferred_element_type=F32)

        step(kb, True)

        def loop_body(qb, carry):
            step(qb, False)
            return carry

        lax.fori_loop(kb + 1, n, loop_body, 0)
        dk_ref[...] = jnp.where(masks[0], dka_ref[0], dka_ref[1]).astype(BF16)
        dv_ref[...] = jnp.where(masks[0], dva_ref[0], dva_ref[1]).astype(BF16)
        dck = [_col(dka_ref[hh], _free_lane0(hh) + AUG_C, lane) for hh in range(2)]
        dck_ref[0] = jnp.where(lane == 0, dck[0], jnp.where(lane == 1, dck[1], 0.0))

        @pl.when(kb == n - 1)
        def _():
            def flush(b, carry):
                off = pl.multiple_of(b * t, t)
                d = [dqt_ref[hh, :, pl.ds(off, t)].T for hh in range(2)]
                dq_ref[pl.ds(off, t), :] = (jnp.where(masks[0], d[0], d[1]) * scale).astype(BF16)
                dcq = [_col(d[hh], _free_lane0(hh) + AUG_ONE, lane) for hh in range(2)]
                dcq_ref[0, pl.ds(off, t), :] = jnp.where(lane == 0, dcq[0], jnp.where(lane == 1, dcq[1], 0.0))
                return carry
            lax.fori_loop(0, n, flush, 0)

        if comm:
            _carry_wait(comm[0], *carried, jnp.logical_and(j == N_PAIR - 1, kb == n - 1))

    full = lambda cb: pl.BlockSpec((S, LANE), lambda j, kb: (0, cb(j)))
    kspec = lambda base: pl.BlockSpec((t, LANE), lambda j, kb: (kb, base + j))
    oblk = pl.BlockSpec((t, LANE), lambda j, kb: (kb, j))
    res = pl.pallas_call(
        body, name=name, grid=(N_PAIR, n),
        in_specs=[full(lambda j: j), kspec(N_PAIR), kspec(2 * N_PAIR), full(lambda j: j), full(lambda j: j),
                  pl.BlockSpec((2, 1, S), lambda j, kb: (j, 0, 0)), full(lambda j: 0)] + [ANY] * nc,
        out_specs=[full(lambda j: j), oblk, oblk,
                   pl.BlockSpec((1, S, LANE), lambda j, kb: (j, 0, 0)),
                   pl.BlockSpec((1, t, LANE), lambda j, kb: (j, kb, 0))] + [ANY] * nc,
        out_shape=[jax.ShapeDtypeStruct((S, N_PAIR * LANE), BF16)] * 3 + [jax.ShapeDtypeStruct((N_PAIR, S, LANE), F32)] * 2
        + (_a2a_out_shapes(*comm) if comm else []),
        scratch_shapes=[pltpu.VMEM((2, S, LANE), BF16), pltpu.VMEM((2, 1, S), F32), pltpu.VMEM((2, LANE, S), F32),
                        pltpu.VMEM((2, t, LANE), F32), pltpu.VMEM((2, t, LANE), F32)]
        + (_a2a_sems(nc) if comm else []),
        compiler_params=_cparams("arbitrary" if comm else "parallel", "arbitrary"),
    )(qkv, qkv, qkv, o, do, lrow, ccol, *(comm[1] if comm else []))
    return tuple(res[:5]) + (list(res[5:]),)


DT_LANE0 = ATTN_HEADS
SSD_PAIRS = SSD_HEADS // 2
SSD_X = SSD_HEADS * SSD_HEAD_DIM
SSD_B0 = SSD_X
SSD_C0 = SSD_X + SSD_GROUPS * SSD_STATE
SSD_CH = SSD_X + 2 * SSD_GROUPS * SSD_STATE
TN_DIMS = (((0,), (0,)), ((), ()))


def _softplus(x):
    return jnp.maximum(x, 0.0) + jnp.log(1.0 + jnp.exp(-jnp.abs(x)))


def _ssd_prep(fdt, dtb, alog):
    L = fdt.shape[0]
    lane = lax.broadcasted_iota(jnp.int32, (1, LANE), 1)
    hl = jnp.logical_and(lane >= DT_LANE0, lane < DT_LANE0 + SSD_HEADS)
    dtv = jnp.where(hl, _softplus(fdt + dtb), 0.0)
    A = jnp.where(hl, -jnp.exp(alog), 0.0)
    r = lax.broadcasted_iota(jnp.int32, (L, L), 0)
    c = lax.broadcasted_iota(jnp.int32, (L, L), 1)
    cs = _tri_dot(jnp.where(c <= r, 1.0, 0.0).astype(BF16), dtv * A)
    return lane, hl, dtv, A, cs, r, c


def _halves(lane, v0, v1):
    return jnp.where(lane < SSD_HEAD_DIM, v0, v1)


def _head_select(width):
    r = jnp.arange(LANE)[:, None]
    c = jnp.arange(SSD_HEADS * width)[None, :]
    return (r == DT_LANE0 + c // width).astype(BF16)


def _spread(x, sel, terms):
    return sum(jnp.dot(p, sel, preferred_element_type=F32) for p in _split3(x)[:terms])


def _ssd_fwd(xc, rest, col, dtb, alog, dskip, *, name):
    S = xc.shape[0]
    L = SSD_CHUNK
    nc = S // L

    def body(xc_ref, f_ref, dtb_ref, al_ref, dk_ref, selp_ref, y_ref, hs_ref, h_ref, cst_ref):
        i = pl.program_id(0)

        @pl.when(i == 0)
        def _():
            h_ref[...] = jnp.zeros_like(h_ref)

        lane, hl, dtv, A, cs, r, c = _ssd_prep(f_ref[...], dtb_ref[...], al_ref[...])
        cst_ref[...] = cs.T
        cs_last = cs[L - 1:L, :]
        cd = jnp.exp(cs_last)
        dkv = dk_ref[...]
        selp = selp_ref[...]
        dt_p, ecs_p, dec_p = (_spread(v, selp, 2) for v in (dtv, jnp.exp(cs), jnp.exp(cs_last - cs)))
        prow = lax.broadcasted_iota(jnp.int32, (LANE, 1), 0)
        for g in range(SSD_GROUPS):
            Bg = xc_ref[:, SSD_B0 + g * SSD_STATE:SSD_B0 + (g + 1) * SSD_STATE].astype(BF16)
            Cg = xc_ref[:, SSD_C0 + g * SSD_STATE:SSD_C0 + (g + 1) * SSD_STATE].astype(BF16)
            CB = lax.dot_general(Cg, Bg, NT_DIMS, preferred_element_type=F32)
            for pp in range(SSD_PAIRS // SSD_GROUPS):
                pi = g * (SSD_PAIRS // SSD_GROUPS) + pp
                hl0 = DT_LANE0 + 2 * pi
                pair = slice(pi * LANE, (pi + 1) * LANE)
                x2 = xc_ref[:, pair]
                xd = x2 * dt_p[:, pair]
                xdb = xd.astype(BF16)
                yd = jnp.zeros((L, LANE), F32)
                for hh in range(2):
                    seg = _col(cs, hl0 + hh, lane) - cst_ref[hl0 + hh:hl0 + hh + 1, :]
                    M = CB * jnp.exp(jnp.where(c <= r, seg, NEG))
                    yh = jnp.dot(M.astype(BF16), xdb, preferred_element_type=F32)
                    yd = jnp.where((lane >= SSD_HEAD_DIM) if hh else (lane < SSD_HEAD_DIM), yh, yd)
                hp = h_ref[pi]
                hs_ref[0, pi] = hp
                yo = lax.dot_general(Cg, hp.astype(BF16), NT_DIMS, preferred_element_type=F32) * ecs_p[:, pair]
                dsk = _halves(lane, _col(dkv, hl0, lane), _col(dkv, hl0 + 1, lane))
                y_ref[:, pair] = yd + yo + dsk * x2
                xw = (xd * dec_p[:, pair]).astype(BF16)
                st = lax.dot_general(xw, Bg, TN_DIMS, preferred_element_type=F32)
                cdp = jnp.where(prow < SSD_HEAD_DIM, _col(cd, hl0, lane), _col(cd, hl0 + 1, lane))
                h_ref[pi] = cdp * hp + st

    vec = pl.BlockSpec((1, LANE), lambda i: (0, 0))
    selp = _head_select(SSD_HEAD_DIM)
    return pl.pallas_call(
        body, name=name, grid=(nc,),
        in_specs=[pl.BlockSpec((L, SSD_CH), lambda i: (i, 0)), pl.BlockSpec((L, LANE), lambda i: (i, col)),
                  vec, vec, vec, pl.BlockSpec(selp.shape, lambda i: (0, 0))],
        out_specs=[pl.BlockSpec((L, SSD_X), lambda i: (i, 0)),
                   pl.BlockSpec((1, SSD_PAIRS, LANE, SSD_STATE), lambda i: (i, 0, 0, 0))],
        out_shape=[jax.ShapeDtypeStruct((S, SSD_X), F32),
                   jax.ShapeDtypeStruct((nc, SSD_PAIRS, LANE, SSD_STATE), F32)],
        scratch_shapes=[pltpu.VMEM((SSD_PAIRS, LANE, SSD_STATE), F32), pltpu.VMEM((LANE, L), F32)],
        compiler_params=_cparams("arbitrary"),
    )(xc, rest, dtb, alog, dskip, selp)


def _pair_sums(lane, v):
    lo = jnp.sum(jnp.where(lane < SSD_HEAD_DIM, v, 0.0), axis=1, keepdims=True)
    return lo, jnp.sum(v, axis=1, keepdims=True) - lo


def _ssd_bwd(xc, rest, col, dtb, alog, dskip, hs, dy, *, name):
    S = xc.shape[0]
    L = SSD_CHUNK
    nc = S // L
    PG = SSD_PAIRS // SSD_GROUPS

    def body(xc_ref, f_ref, dtb_ref, al_ref, dk_ref, hs_ref, dy_ref, selp_ref,
             dxc_ref, ddt_ref, dp_ref, dh_ref, cst_ref):
        i = pl.program_id(0)

        @pl.when(i == 0)
        def _():
            dh_ref[...] = jnp.zeros_like(dh_ref)
            dp_ref[...] = jnp.zeros_like(dp_ref)

        fv = f_ref[...] + dtb_ref[...]
        lane, hl, dtv, A, cs, r, c = _ssd_prep(f_ref[...], dtb_ref[...], al_ref[...])
        cst_ref[...] = cs.T
        cs_last = cs[L - 1:L, :]
        cd = jnp.exp(cs_last)
        dkv = dk_ref[...]
        selp = selp_ref[...]
        dt_p, ecs_p, dec_p = (_spread(v, selp, 2) for v in (dtv, jnp.exp(cs), jnp.exp(cs_last - cs)))
        prow = lax.broadcasted_iota(jnp.int32, (LANE, 1), 0)
        lrow = lax.broadcasted_iota(jnp.int32, (L, 1), 0)
        is_last = lrow == L - 1
        causal = c <= r
        dcs = jnp.zeros((L, LANE), F32)
        ddt = jnp.zeros((L, LANE), F32)
        dD = jnp.zeros((1, LANE), F32)
        for g in range(SSD_GROUPS):
            Bg = xc_ref[:, SSD_B0 + g * SSD_STATE:SSD_B0 + (g + 1) * SSD_STATE].astype(BF16)
            Cg = xc_ref[:, SSD_C0 + g * SSD_STATE:SSD_C0 + (g + 1) * SSD_STATE].astype(BF16)
            CB = lax.dot_general(Cg, Bg, NT_DIMS, preferred_element_type=F32)
            dCB = jnp.zeros((L, L), F32)
            dB = jnp.zeros((L, SSD_STATE), F32)
            dC = jnp.zeros((L, SSD_STATE), F32)
            for pp in range(PG):
                pi = g * PG + pp
                hl0 = DT_LANE0 + 2 * pi
                pair = slice(pi * LANE, (pi + 1) * LANE)
                x2 = xc_ref[:, pair]
                dy2 = dy_ref[:, pair]
                dtp, ecsp, decp = dt_p[:, pair], ecs_p[:, pair], dec_p[:, pair]
                xd = x2 * dtp
                xdb = xd.astype(BF16)
                dsk = _halves(lane, _col(dkv, hl0, lane), _col(dkv, hl0 + 1, lane))
                dx2 = dsk * dy2
                sD = _pair_sums(lane, dy2 * x2)
                hp = hs_ref[0, pi]
                hpb = hp.astype(BF16)
                yo = lax.dot_general(Cg, hpb, NT_DIMS, preferred_element_type=F32) * ecsp
                dW = (ecsp * dy2).astype(BF16)
                dC = dC + jnp.dot(dW, hpb, preferred_element_type=F32)
                dhp = lax.dot_general(dW, Cg, TN_DIMS, preferred_element_type=F32)
                sYo = _pair_sums(lane, dy2 * yo)
                dhn = dh_ref[pi]
                cdh = (_col(cd, hl0, lane), _col(cd, hl0 + 1, lane))
                dhp = dhp + jnp.where(prow < SSD_HEAD_DIM, cdh[0], cdh[1]) * dhn
                rs = jnp.sum(dhn * hp, axis=1, keepdims=True)
                lo = jnp.sum(jnp.where(prow < SSD_HEAD_DIM, rs, 0.0), axis=0, keepdims=True)
                dcd = (lo, jnp.sum(rs, axis=0, keepdims=True) - lo)
                dhnb = dhn.astype(BF16)
                G = lax.dot_general(Bg, dhnb, NT_DIMS, preferred_element_type=F32)
                dxd = decp * G
                sdd = _pair_sums(lane, xd * dxd)
                dB = dB + jnp.dot((xd * decp).astype(BF16), dhnb, preferred_element_type=F32)
                dh_ref[pi] = dhp
                for hh in range(2):
                    hmask = (lane >= SSD_HEAD_DIM) if hh else (lane < SSD_HEAD_DIM)
                    seg = _col(cs, hl0 + hh, lane) - cst_ref[hl0 + hh:hl0 + hh + 1, :]
                    Lm = jnp.exp(jnp.where(causal, seg, NEG))
                    M = CB * Lm
                    dyh = jnp.where(hmask, dy2, 0.0).astype(BF16)
                    dM = lax.dot_general(dyh, xdb, NT_DIMS, preferred_element_type=F32)
                    dxd = dxd + lax.dot_general(M.astype(BF16), dyh, TN_DIMS, preferred_element_type=F32)
                    Q = dM * M
                    dCB = dCB + dM * Lm
                    dd = sdd[hh]
                    end = dcd[hh] * cdh[hh] + jnp.sum(dd, axis=0, keepdims=True)
                    dcs_h = (sYo[hh] - dd + jnp.sum(Q, axis=1, keepdims=True)
                             - jnp.sum(Q.T, axis=1, keepdims=True) + jnp.where(is_last, end, 0.0))
                    dcs = jnp.where(lane == hl0 + hh, dcs_h, dcs)
                    dD = jnp.where(lane == hl0 + hh, jnp.sum(sD[hh], axis=0, keepdims=True), dD)
                sdt = _pair_sums(lane, dxd * x2)
                ddt = jnp.where(lane == hl0, sdt[0], jnp.where(lane == hl0 + 1, sdt[1], ddt))
                dxc_ref[:, pair] = dx2 + dxd * dtp
            dCBb = dCB.astype(BF16)
            dC = dC + jnp.dot(dCBb, Bg, preferred_element_type=F32)
            dB = dB + lax.dot_general(dCBb, Cg, TN_DIMS, preferred_element_type=F32)
            dxc_ref[:, SSD_B0 + g * SSD_STATE:SSD_B0 + (g + 1) * SSD_STATE] = dB
            dxc_ref[:, SSD_C0 + g * SSD_STATE:SSD_C0 + (g + 1) * SSD_STATE] = dC
        da = _tri_dot(jnp.where(c >= r, 1.0, 0.0).astype(BF16), dcs)
        ddtv = ddt + da * A
        ddt_ref[...] = jnp.where(hl, ddtv * _sigmoid(fv), 0.0)
        dal = jnp.sum(da * dtv, axis=0, keepdims=True) * A
        dp_ref[0:1, :] += dal
        dp_ref[1:2, :] += dD

    rev = lambda i: nc - 1 - i
    vec = pl.BlockSpec((1, LANE), lambda i: (0, 0))
    selp = _head_select(SSD_HEAD_DIM)
    return pl.pallas_call(
        body, name=name, grid=(nc,),
        in_specs=[pl.BlockSpec((L, SSD_CH), lambda i: (rev(i), 0)), pl.BlockSpec((L, LANE), lambda i: (rev(i), col)),
                  vec, vec, vec,
                  pl.BlockSpec((1, SSD_PAIRS, LANE, SSD_STATE), lambda i: (rev(i), 0, 0, 0)),
                  pl.BlockSpec((L, SSD_X), lambda i: (rev(i), 0)), pl.BlockSpec(selp.shape, lambda i: (0, 0))],
        out_specs=[pl.BlockSpec((L, SSD_CH), lambda i: (rev(i), 0)), pl.BlockSpec((L, LANE), lambda i: (rev(i), 0)),
                   pl.BlockSpec((SUBLANE, LANE), lambda i: (0, 0))],
        out_shape=[jax.ShapeDtypeStruct((S, SSD_CH), F32), jax.ShapeDtypeStruct((S, LANE), F32),
                   jax.ShapeDtypeStruct((SUBLANE, LANE), F32)],
        scratch_shapes=[pltpu.VMEM((SSD_PAIRS, LANE, SSD_STATE), F32), pltpu.VMEM((LANE, L), F32)],
        compiler_params=_cparams("arbitrary"),
    )(xc, rest, dtb, alog, dskip, hs, dy, selp)


def _gate_norm_fwd(y, rest, zcol, nw, *, name):
    S, W = y.shape
    ts = _row_tile(S)
    GW = W // SSD_GROUPS

    def body(y_ref, z_ref, nw_ref, o_ref):
        z = z_ref[...]
        t = y_ref[...] * (z * _sigmoid(z))
        for g in range(SSD_GROUPS):
            cols = slice(g * GW, (g + 1) * GW)
            tg = t[:, cols]
            rr = lax.rsqrt(jnp.mean(tg * tg, axis=-1, keepdims=True) + NORM_EPS)
            o_ref[:, cols] = (tg * rr * nw_ref[:, cols]).astype(BF16)

    row = pl.BlockSpec((ts, W), lambda i: (i, 0))
    return pl.pallas_call(
        body, name=name, grid=(S // ts,),
        in_specs=[row, pl.BlockSpec((ts, W), lambda i: (i, zcol)), pl.BlockSpec((1, W), lambda i: (0, 0))],
        out_specs=row,
        out_shape=jax.ShapeDtypeStruct((S, W), BF16),
        compiler_params=_cparams("parallel"),
    )(y, rest, nw)


def _gate_norm_bwd(y, rest, zcol, nw, do, *, name):
    S, W = y.shape
    ts = _row_tile(S)
    GW = W // SSD_GROUPS

    def body(y_ref, z_ref, nw_ref, do_ref, dy_ref, dz_ref, dnw_ref):
        z = z_ref[...]
        yv = y_ref[...]
        s, ds = _dsilu(z)
        sz = z * s
        t = yv * sz
        dov = do_ref[...]
        parts = []
        for g in range(SSD_GROUPS):
            cols = slice(g * GW, (g + 1) * GW)
            tg = t[:, cols]
            rr = lax.rsqrt(jnp.mean(tg * tg, axis=-1, keepdims=True) + NORM_EPS)
            n = tg * rr
            dog = dov[:, cols]
            dn = dog * nw_ref[:, cols]
            dt = rr * (dn - n * jnp.mean(dn * n, axis=-1, keepdims=True))
            dy_ref[:, cols] = dt * sz[:, cols]
            dz_ref[:, cols] = (dt * yv[:, cols] * ds[:, cols]).astype(BF16)
            parts.append(jnp.sum(dog * n, axis=0, keepdims=True))
        _acc_out(dnw_ref, jnp.concatenate(parts, axis=1), pl.program_id(0) == 0)

    row = pl.BlockSpec((ts, W), lambda i: (i, 0))
    vec = pl.BlockSpec((1, W), lambda i: (0, 0))
    return pl.pallas_call(
        body, name=name, grid=(S // ts,),
        in_specs=[row, pl.BlockSpec((ts, W), lambda i: (i, zcol)), vec, row],
        out_specs=[row, row, vec],
        out_shape=[jax.ShapeDtypeStruct((S, W), F32), jax.ShapeDtypeStruct((S, W), BF16),
                   jax.ShapeDtypeStruct((1, W), F32)],
        compiler_params=_cparams("arbitrary"),
    )(y, rest, nw, do)


N_DEV = 8
MESH = pl.DeviceIdType.MESH
ANY = pl.BlockSpec(memory_space=pl.ANY)


def _all_gather(xs, *, name):
    n = len(xs)

    def body(*refs):
        for phase in range(3):
            _gather_phase(phase, refs[:n], refs[n:2 * n], refs[2 * n:])

    return pl.pallas_call(
        body, name=name,
        out_shape=_a2a_out_shapes("gather", xs), in_specs=[ANY] * n, out_specs=[ANY] * n,
        scratch_shapes=_a2a_sems(n),
    )(*xs)


def _gather_phase(phase, x_refs, o_refs, sems):
    send_sems, recv_sems, local_sems = sems
    n = len(x_refs)
    px, py, pc = lax.axis_index("x"), lax.axis_index("y"), lax.axis_index("c")
    me, sibling = (px, py, pc), (px, py, 1 - pc)
    chips = [(1 - px, py), (px, 1 - py), (1 - px, 1 - py)]

    def copy(a, k, block, to, src=None):
        slot = o_refs[a].at[4 * block[0] + 2 * block[1] + block[2]]
        return pltpu.make_async_remote_copy(
            src_ref=slot if src is None else src, dst_ref=slot,
            send_sem=send_sems.at[a, k], recv_sem=recv_sems.at[a, k], device_id=to, device_id_type=MESH)

    def own(a):
        return ([copy(a, 0, me, sibling, src=x_refs[a])]
                + [copy(a, 1 + j, me, (*chip, pc), src=x_refs[a]) for j, chip in enumerate(chips)])

    mine = lambda a: pltpu.make_async_copy(x_refs[a], o_refs[a].at[4 * px + 2 * py + pc], local_sems.at[a])
    passed = lambda a, j: copy(a, 4 + j, (*chips[j], pc), sibling)
    if phase == 0:
        for a in range(n):
            mine(a).start()
            for cp in own(a):
                cp.start()
    elif phase == 1:
        for j, chip in enumerate(chips):
            for a in range(n):
                copy(a, 1 + j, (*chip, pc), me).wait_recv()
                passed(a, j).start()
    else:
        for a in range(n):
            copy(a, 0, sibling, me).wait_recv()
            for j, chip in enumerate(chips):
                copy(a, 4 + j, (*chip, 1 - pc), me).wait_recv()
            for cp in own(a) + [passed(a, j) for j in range(len(chips))]:
                cp.wait_send()
            mine(a).wait()


def _exchange(sends, *, name):
    n = len(sends)

    def body(*refs):
        _a2a_start(refs[:n], refs[n:2 * n], refs[2 * n:])
        _a2a_wait(refs[:n], refs[n:2 * n], refs[2 * n:])

    return pl.pallas_call(
        body, name=name,
        out_shape=_a2a_out_shapes("exchange", sends), in_specs=[ANY] * n, out_specs=[ANY] * n,
        scratch_shapes=_a2a_sems(n),
    )(*sends)


def _a2a_out_shapes(kind, arrays):
    if kind == "gather":
        return [jax.ShapeDtypeStruct((N_DEV,) + x.shape, x.dtype) for x in arrays]
    return [jax.ShapeDtypeStruct(x.shape, x.dtype) for x in arrays]


def _a2a_sems(n):
    return [pltpu.SemaphoreType.DMA((n, N_DEV - 1)), pltpu.SemaphoreType.DMA((n, N_DEV - 1)),
            pltpu.SemaphoreType.DMA((n,))]


def _a2a_copies(in_refs, out_refs, send_sems, recv_sems, local_sems, arrivals):
    px, py, pc = lax.axis_index("x"), lax.axis_index("y"), lax.axis_index("c")
    me = 4 * px + 2 * py + pc
    n = len(in_refs)
    src = lambda a, p: in_refs[a].at[p]
    local = [pltpu.make_async_copy(src(a, me), out_refs[a].at[me], local_sems.at[a]) for a in range(n)]
    sent, landing = [], []
    for k in range(1, N_DEV):
        qx = 1 - px if k & 4 else px
        qy = 1 - py if k & 2 else py
        qc = 1 - pc if k & 1 else pc
        peer = 4 * qx + 2 * qy + qc
        for a in range(n):
            def rdma(dst_slot):
                return pltpu.make_async_remote_copy(
                    src_ref=src(a, peer), dst_ref=out_refs[a].at[dst_slot],
                    send_sem=send_sems.at[a, k - 1], recv_sem=recv_sems.at[a, k - 1],
                    device_id=(qx, qy, qc), device_id_type=MESH)
            sent.append(rdma(me))
            if arrivals:
                landing.append(rdma(peer))
    return local, sent, landing


def _a2a_start(in_refs, out_refs, sems):
    local, sent, _ = _a2a_copies(in_refs, out_refs, *sems, arrivals=False)
    for cp in local + sent:
        cp.start()


def _a2a_wait(in_refs, out_refs, sems):
    local, sent, landing = _a2a_copies(in_refs, out_refs, *sems, arrivals=True)
    for cp in landing:
        cp.wait_recv()
    for cp in sent:
        cp.wait_send()
    for cp in local:
        cp.wait()


IN_POOL, IN_QKV, IN_F, IN_Z, IN_XBC, IN_DT, IN_GATE, IN_TOTAL = 0, 512, 2048, 2056, 3080, 4616, 4632, 7704
QKV_W = IN_F - IN_QKV
REST_Z_BLK, REST_POOL_BLK, REST_XBC0, REST_FDT_BLK = 3, 8, 4608, 48
REST_FDT0 = REST_FDT_BLK * LANE
REST_USED = REST_FDT0 + (IN_Z - IN_F) + (IN_GATE - IN_DT)
REST_W = REST_FDT0 + LANE


def _w_in_runs(shard):
    segs = [(0, IN_QKV, IN_F, 0), (1, IN_GATE, IN_TOTAL, 0), (1, IN_Z, IN_XBC, 3072), (1, IN_POOL, IN_QKV, 4096),
            (1, IN_XBC, IN_DT, REST_XBC0), (1, IN_F, IN_Z, REST_FDT0), (1, IN_DT, IN_GATE, REST_FDT0 + IN_Z - IN_F)]
    runs = []
    for dst, t0, t1, d0 in segs:
        for j in range(N_DEV):
            lo, hi = max(t0, shard * j), min(t1, shard * (j + 1))
            if lo < hi:
                runs.append((dst, d0 + lo - t0, j, lo - shard * j, hi - lo))
    return runs


def _repack_w_in(g, shard, *, name):
    _, L, K, P = g.shape
    tr = 256
    runs = _w_in_runs(shard)

    def body(g_ref, q_ref, r_ref):
        r_ref[:, REST_USED:] = jnp.zeros((tr, REST_W - REST_USED), g_ref.dtype)
        for dst, b, j, a, n in runs:
            (q_ref, r_ref)[dst][:, b:b + n] = g_ref[j, :, a:a + n]

    return pl.pallas_call(
        body, name=name, grid=(L, K // tr),
        in_specs=[pl.BlockSpec((N_DEV, None, tr, P), lambda l, i: (0, l, i, 0))],
        out_specs=[pl.BlockSpec((None, tr, QKV_W), lambda l, i: (l, i, 0)),
                   pl.BlockSpec((None, tr, REST_W), lambda l, i: (l, i, 0))],
        out_shape=[jax.ShapeDtypeStruct((L, K, QKV_W), g.dtype), jax.ShapeDtypeStruct((L, K, REST_W), g.dtype)],
        compiler_params=_cparams("parallel", "parallel"),
    )(g)


def _held_index(l, li, i, n):
    return jnp.where(l == li, i, jnp.where(l > li, n - 1, 0))


def _repack_dw_in(dwq, dwr, shard, *, name):
    L = len(dwq)
    K = dwq[0].shape[0]
    tr = 128
    nt = K // tr
    runs = _w_in_runs(shard)

    def body(*refs):
        srcs, o_ref = refs[:2 * L], refs[2 * L]
        l = pl.program_id(0)
        for li in range(L):
            @pl.when(l == li)
            def _():
                for dst, b, j, a, n in runs:
                    o_ref[j, :, a:a + n] = srcs[2 * li + dst][:, b:b + n]

    in_specs = []
    for li in range(L):
        hold = functools.partial(lambda li, l, i: (_held_index(l, li, i, nt), 0), li)
        in_specs += [pl.BlockSpec((tr, QKV_W), hold), pl.BlockSpec((tr, REST_W), hold)]
    args = [x for pair in zip(dwq, dwr) for x in pair]
    return pl.pallas_call(
        body, name=name, grid=(L, nt),
        in_specs=in_specs,
        out_specs=pl.BlockSpec((N_DEV, None, tr, shard), lambda l, i: (0, l, i, 0)),
        out_shape=jax.ShapeDtypeStruct((N_DEV, L, K, shard), dwq[0].dtype),
        compiler_params=_cparams("arbitrary", "arbitrary"),
    )(*args)


FLAT_W = 1024
ADAM_ROWS = 128


def _adam_math(w, g, m, v):
    m2 = ADAM_B1 * m + (1.0 - ADAM_B1) * g
    v2 = ADAM_B2 * v + (1.0 - ADAM_B2) * (g * g)
    m_hat = m2 / (1.0 - ADAM_B1 ** ADAM_STEP)
    v_hat = v2 / (1.0 - ADAM_B2 ** ADAM_STEP)
    delta = -ADAM_LR * (m_hat / (jnp.sqrt(v_hat) + ADAM_EPS) + ADAM_WD * w)
    return delta, m2, v2


def _row_div(R, align):
    best = R
    for t in range(align, min(R, ADAM_ROWS) + 1, align):
        if R % t == 0:
            best = t
    return best


def _sum8(ref_of):
    g = ref_of(0).astype(F32)
    for s in range(1, N_DEV):
        g = g + ref_of(s).astype(F32)
    return g


def _sum_adam(recvs, w, m, v, *, name):
    L, R, C = w.shape
    single = not isinstance(recvs, (list, tuple))
    recvs = [recvs] if single else list(recvs)
    tr = _row_div(R, SUBLANE * (4 // recvs[0].dtype.itemsize))
    nt = R // tr
    nr = len(recvs)

    def body(*refs):
        r_refs = refs[:nr]
        w_ref, m_ref, v_ref, g_ref, d_ref, m2_ref, v2_ref = refs[nr:]

        def run(ref_of):
            g = _sum8(ref_of)
            delta, m2, v2 = _adam_math(w_ref[...], g, m_ref[...], v_ref[...])
            g_ref[...] = g
            d_ref[...] = delta
            m2_ref[...] = m2
            v2_ref[...] = v2

        if single:
            run(lambda s: r_refs[0][s])
        else:
            l = pl.program_id(0)
            for li in range(L):
                @pl.when(l == li)
                def _():
                    run(lambda s: r_refs[li][s])

    if single:
        r_specs = [pl.BlockSpec((N_DEV, None, tr, C), lambda l, i: (0, l, i, 0))]
    else:
        r_specs = [pl.BlockSpec((N_DEV, tr, C), functools.partial(lambda li, l, i: (0, _held_index(l, li, i, nt), 0), li))
                   for li in range(L)]
    row = pl.BlockSpec((None, tr, C), lambda l, i: (l, i, 0))
    return pl.pallas_call(
        body, name=name, grid=(L, nt),
        in_specs=r_specs + [row, row, row],
        out_specs=[row] * 4,
        out_shape=[jax.ShapeDtypeStruct((L, R, C), F32)] * 4,
        compiler_params=_cparams("arbitrary", "arbitrary"),
    )(*recvs, w, m, v)


def _sum_parts(parts, *, name):
    _, R, C = parts.shape

    def body(p_ref, o_ref):
        o_ref[...] = _sum8(lambda s: p_ref[s])

    return pl.pallas_call(
        body, name=name, out_shape=jax.ShapeDtypeStruct((R, C), F32),
        in_specs=[pl.BlockSpec(memory_space=pltpu.VMEM)], out_specs=pl.BlockSpec(memory_space=pltpu.VMEM),
    )(parts)


def _adam(g, w, m, v, *, name):
    def body(g_ref, w_ref, m_ref, v_ref, d_ref, m2_ref, v2_ref):
        delta, m2, v2 = _adam_math(w_ref[...], g_ref[...], m_ref[...], v_ref[...])
        d_ref[...] = delta
        m2_ref[...] = m2
        v2_ref[...] = v2

    whole = pl.BlockSpec(memory_space=pltpu.VMEM)
    return pl.pallas_call(
        body, name=name, in_specs=[whole] * 4, out_specs=[whole] * 3,
        out_shape=[jax.ShapeDtypeStruct(g.shape, F32)] * 3,
    )(g, w, m, v)


SHARDED = ("w_in", "ffn_up", "p_pool", "p_attn", "p_ssd", "w_out", "ffn_down", "ssd_conv_w", "ffn_conv_w")
REPLICATED = ("norm_mix", "pool_mix", "pool_scale", "f_bias", "ssd_conv_b", "ssd_dt_bias", "ssd_a_log", "ssd_d",
              "ssd_norm", "norm_ffn", "ffn_conv_b", "norm_final")
WEIGHTS = ("norm_mix", "w_in", "pool_mix", "pool_scale", "f_bias", "ssd_conv_w", "ssd_conv_b", "ssd_dt_bias",
           "ssd_a_log", "ssd_d", "ssd_norm", "p_pool", "p_attn", "p_ssd", "w_out", "norm_ffn", "ffn_up",
           "ffn_conv_w", "ffn_conv_b", "ffn_down", "norm_final")


def _round_up(n, k):
    return -(-n // k) * k


def _pad_last(x, width):
    return jnp.pad(x, [(0, 0)] * (x.ndim - 1) + [(0, width - x.shape[-1])])


def _flat_rows(parts, rows):
    flat = jnp.concatenate([p.reshape(-1) for p in parts])
    return jnp.pad(flat, (0, rows * FLAT_W - flat.shape[0])).reshape(rows, FLAT_W)


def _lanes(v, lane0):
    return jnp.pad(v, (lane0, LANE - lane0 - v.shape[0]))[None]


def _cols_by_device(g):
    _, L, R, c = g.shape
    return jnp.moveaxis(g, 0, 2).reshape(L, R, N_DEV * c)


def _rows_by_device(g):
    _, L, r, C = g.shape
    return jnp.moveaxis(g, 0, 1).reshape(L, N_DEV * r, C)


def _gather_payload(W, l, shard_pad, up_pad):
    bf = lambda n: W[n][l].astype(BF16)
    return [_pad_last(bf("w_in"), shard_pad), _pad_last(bf("ffn_up"), up_pad), bf("p_pool"), bf("p_attn"),
            bf("p_ssd"), bf("w_out"), bf("ffn_down"), W["ssd_conv_w"][l], W["ffn_conv_w"][l]]


def _proj_params(g_win, W, l, shard):
    w_qkv, w_rest = _repack_w_in(g_win[:, None], shard, name=f"l{l}_repack_w_in")
    return dict(nm=W["norm_mix"][l][None], w_qkv=w_qkv[0], w_rest=w_rest[0], fb=_lanes(W["f_bias"][l], 0), shard=shard)


def _mixer_params(gathered, W, l, up_w, up_pad):
    g_up, g_pp, g_pa, g_ps, g_wo, g_dn, g_cw, g_fcw = [g[:, None] for g in gathered]
    half = N_DEV // 2
    w_down = _rows_by_device(g_dn)[0]
    w_down_p = jnp.pad(w_down.reshape(half, up_w, -1), ((0, 0), (0, up_pad - up_w), (0, 0))).reshape(half * up_pad, -1)
    fcb_p = _pad_last(W["ffn_conv_b"][l].reshape(N_DEV, up_w), up_pad).reshape(1, N_DEV * up_pad)
    return dict(
        mix=W["pool_mix"][l].astype(BF16), pscale=W["pool_scale"][l][None], cw=_cols_by_device(g_cw)[0],
        cb=W["ssd_conv_b"][l][None], dtb=_lanes(W["ssd_dt_bias"][l], DT_LANE0),
        alog=_lanes(W["ssd_a_log"][l], DT_LANE0), dsk=_lanes(W["ssd_d"][l], DT_LANE0), snw=W["ssd_norm"][l][None],
        p_pool=_cols_by_device(g_pp)[0], p_attn=_cols_by_device(g_pa)[0], p_ssd=_rows_by_device(g_ps)[0],
        w_out=_rows_by_device(g_wo)[0], nf=W["norm_ffn"][l][None], g_up=g_up, up_blocks=(N_DEV, up_w, up_pad),
        fcw=_cols_by_device(_pad_last(g_fcw, up_pad))[0], fcb=fcb_p, w_down=w_down_p)


def _layer_fwd(x, W, l, g_win, g_mixer, to_gather, dims):
    nm = lambda s: f"l{l}_{s}"
    P = _proj_params(g_win, W, l, dims[0])
    u = _rms_fwd(x, P["nm"], name=nm("rms_mix"))
    qkv = _mm(u, P["w_qkv"], out_dtype=BF16, name=nm("proj_qkv"))
    rest = _mm(u, P["w_rest"], name=nm("proj_rest"))
    ccol = _fox_pre(rest, REST_FDT_BLK, P["fb"], name=nm("fox_pre"))
    o, lrow, got = _attn_fwd(qkv, ccol, comm=("gather", to_gather) if to_gather else None, name=nm("attn"))
    if g_mixer is None:
        g_mixer, got = got[:len(SHARDED) - 1], got[len(SHARDED) - 1:]
    P.update(_mixer_params(g_mixer, W, l, *dims[1:]))
    d, ya0 = _pool_fwd(rest, REST_POOL_BLK, P["mix"], P["pscale"], name=nm("pool"))
    ya = _mm(ya0, P["p_pool"], name=nm("p_pool"))
    yb = _mm(o, P["p_attn"], name=nm("p_attn"))
    xc = _conv_silu_fwd(rest, REST_XBC0, SSD_CH, P["cw"], P["cb"], name=nm("ssd_conv"))
    y, hs = _ssd_fwd(xc, rest, REST_FDT_BLK, P["dtb"], P["alog"], P["dsk"], name=nm("ssd"))
    yc0 = _gate_norm_fwd(y, rest, REST_Z_BLK, P["snw"], name=nm("ssd_norm"))
    yc = _mm(yc0, P["p_ssd"], name=nm("p_ssd"))
    merged = _merge_fwd(rest, ya, yb, yc, name=nm("merge"))
    x1 = _mm(merged, P["w_out"], add=x, name=nm("w_out"))
    u2 = _rms_fwd(x1, P["nf"], name=nm("rms_ffn"))
    h0 = _mm_up_fwd(u2, P["g_up"], 0, name=nm("ffn_up"))
    a = _ffn_act_fwd(h0, P["fcw"], P["fcb"], name=nm("ffn_act"))
    x2 = _mm(a, P["w_down"], add=x1, name=nm("ffn_down"))
    saved = dict(x=x, u=u, qkv=qkv, rest=rest, d=d, ya0=ya0, ya=ya, ccol=ccol, o=o, lrow=lrow, yb=yb,
                 xc=xc, y=y, hs=hs, yc0=yc0, yc=yc, merged=merged, x1=x1, u2=u2, h0=h0, a=a)
    return x2, saved, P, got


EARLY_GRADS = ("ffn_up", "ffn_down", "ffn_conv_w", "p_pool", "p_attn", "p_ssd", "w_out")
LATE_GRADS = ("w_in", "ssd_conv_w")


def _layer_bwd(g, gb, sv, P, l, pending, last):
    nm = lambda s: f"l{l}_{s}_bwd"
    rest = sv["rest"]
    G = {}
    up_blocks, up_w, up_pad = P["up_blocks"]
    half = up_blocks // 2
    da = _mm(gb, P["w_down"], tb=True, name=nm("ffn_down_dx"))
    dwd = _mm(sv["a"], gb, ta=True, out_dtype=GRAD_WIRE, name=nm("ffn_down_dw"))
    G["ffn_down"] = dwd.reshape(half, up_pad, -1)[:, :up_w].reshape(N_DEV, -1, dwd.shape[1])
    dh, dfcw, dfcb = _ffn_act_bwd(sv["h0"], P["fcw"], P["fcb"], da, name=nm("ffn_act"))
    du2 = _mm_up_dx(dh, P["g_up"], 0, name=nm("ffn_up_dx"))
    G["ffn_up"] = _mm_up_dw(sv["u2"], dh, up_blocks, up_w, name=nm("ffn_up_dw"))
    taps = dfcw.shape[1]
    G["ffn_conv_w"] = jnp.moveaxis(dfcw.reshape(2, taps, half, up_pad)[..., :up_w], 2, 1).reshape(N_DEV, taps, up_w)
    G["ffn_conv_b"] = dfcb.reshape(2, half, up_pad)[..., :up_w].reshape(-1)
    dx1, dx1b, dnf = _rms_bwd(sv["x1"], P["nf"], du2, g, name=nm("rms_ffn"))
    G["norm_ffn"] = dnf[0]
    dmerged = _mm(dx1b, P["w_out"], tb=True, name=nm("w_out_dx"))
    dwo = _mm(sv["merged"], dx1b, ta=True, out_dtype=GRAD_WIRE, name=nm("w_out_dw"))
    G["w_out"] = dwo.reshape(N_DEV, -1, dwo.shape[1])
    dgl, dya, dyb, dyc = _merge_bwd(rest, sv["ya"], sv["yb"], sv["yc"], dmerged, name=nm("merge"))
    col_blocks = lambda dw: jnp.moveaxis(dw.reshape(dw.shape[0], N_DEV, -1), 1, 0)
    G["p_pool"] = col_blocks(_mm(sv["ya0"], dya, ta=True, out_dtype=GRAD_WIRE, name=nm("p_pool_dw")))
    dya0 = _mm(dya, P["p_pool"], tb=True, name=nm("p_pool_dx"))
    G["p_attn"] = col_blocks(_mm(sv["o"], dyb, ta=True, out_dtype=GRAD_WIRE, name=nm("p_attn_dw")))
    do = _mm(dyb, P["p_attn"], tb=True, out_dtype=BF16, name=nm("p_attn_dx"))
    dps = _mm(sv["yc0"], dyc, ta=True, out_dtype=GRAD_WIRE, name=nm("p_ssd_dw"))
    G["p_ssd"] = dps.reshape(N_DEV, -1, dps.shape[1])
    dyc0 = _mm(dyc, P["p_ssd"], tb=True, name=nm("p_ssd_dx"))
    dpv, dmix, dpsc = _pool_bwd(dya0, sv["d"], P["mix"], P["pscale"], name=nm("pool"))
    G["pool_mix"], G["pool_scale"] = dmix, dpsc[0]
    carried = [((n, l), G[n]) for n in EARLY_GRADS] + pending
    dq, dk, dv, dcq, dck, got = _attn_bwd(sv["qkv"], sv["o"], do, sv["lrow"], sv["ccol"],
                                          comm=("exchange", [a for _, a in carried]), name=nm("attn"))
    received = {key: r for (key, _), r in zip(carried, got)}
    dy, dz, dsnw = _gate_norm_bwd(sv["y"], rest, REST_Z_BLK, P["snw"], dyc0, name=nm("ssd_norm"))
    G["ssd_norm"] = dsnw[0]
    dxc, ddt, dpar = _ssd_bwd(sv["xc"], rest, REST_FDT_BLK, P["dtb"], P["alog"], P["dsk"], sv["hs"], dy, name=nm("ssd"))
    G["ssd_a_log"] = dpar[0, DT_LANE0:DT_LANE0 + SSD_HEADS]
    G["ssd_d"] = dpar[1, DT_LANE0:DT_LANE0 + SSD_HEADS]
    dxbc, dcw, dcb = _conv_silu_bwd(rest, REST_XBC0, SSD_CH, P["cw"], P["cb"], dxc, name=nm("ssd_conv"))
    G["ssd_conv_w"] = jnp.moveaxis(dcw.reshape(dcw.shape[0], N_DEV, -1), 1, 0)
    G["ssd_conv_b"] = dcb[0]
    dfdt, dfdtb = _fox_post(dcq, dck, rest, REST_FDT_BLK, P["fb"], ddt, name=nm("fox_post"))
    G["f_bias"] = dfdtb[0, :ATTN_HEADS]
    G["ssd_dt_bias"] = dfdtb[0, DT_LANE0:DT_LANE0 + SSD_HEADS]
    dqkv = jnp.concatenate([dq, dk, dv], axis=1)
    drest = jnp.concatenate([dgl, dz, dpv, dxbc, dfdt], axis=1)
    dwq = _mm(sv["u"], dqkv, ta=True, out_dtype=GRAD_WIRE, name=nm("proj_qkv_dw"))
    dwr = _mm(sv["u"], drest, ta=True, out_dtype=GRAD_WIRE, name=nm("proj_rest_dw"))
    G["w_in"] = _repack_dw_in([dwq], [dwr], P["shard"], name=nm("repack_dw_in"))[:, 0]
    du = _mm(dqkv, P["w_qkv"], tb=True, name=nm("proj_qkv_dx"))
    late = [((n, l), G[n]) for n in LATE_GRADS]
    if last:
        du, got = _mm(drest, P["w_rest"], tb=True, add=du, comm=("exchange", [a for _, a in late]),
                      name=nm("proj_rest_dx"))
        received.update({key: r for (key, _), r in zip(late, got)})
        late = []
    else:
        du = _mm(drest, P["w_rest"], tb=True, add=du, name=nm("proj_rest_dx"))
    dx, dxb, dnm = _rms_bwd(sv["x"], P["nm"], du, dx1, name=nm("rms_mix"))
    G["norm_mix"] = dnm[0]
    return dx, dxb, G, received, late


def kernel(x, norm_mix, w_in, pool_mix, pool_scale, f_bias, ssd_conv_w, ssd_conv_b, ssd_dt_bias, ssd_a_log, ssd_d, ssd_norm, p_pool, p_attn, p_ssd, w_out, norm_ffn, ffn_up, ffn_conv_w, ffn_conv_b, ffn_down, norm_final, loss_target, m_norm_mix, m_w_in, m_pool_mix, m_pool_scale, m_f_bias, m_ssd_conv_w, m_ssd_conv_b, m_ssd_dt_bias, m_ssd_a_log, m_ssd_d, m_ssd_norm, m_p_pool, m_p_attn, m_p_ssd, m_w_out, m_norm_ffn, m_ffn_up, m_ffn_conv_w, m_ffn_conv_b, m_ffn_down, m_norm_final, v_norm_mix, v_w_in, v_pool_mix, v_pool_scale, v_f_bias, v_ssd_conv_w, v_ssd_conv_b, v_ssd_dt_bias, v_ssd_a_log, v_ssd_d, v_ssd_norm, v_p_pool, v_p_attn, v_p_ssd, v_w_out, v_norm_ffn, v_ffn_up, v_ffn_conv_w, v_ffn_conv_b, v_ffn_down, v_norm_final):
    args = dict(locals())
    W = {n: args[n] for n in WEIGHTS}
    M = {n: args["m_" + n] for n in WEIGHTS}
    V = {n: args["v_" + n] for n in WEIGHTS}
    L = norm_mix.shape[0]
    shard = w_in.shape[-1]
    up_w = ffn_up.shape[-1]
    up_pad = _round_up(up_w, LANE)

    payload = lambda l: _gather_payload(W, l, _round_up(shard, LANE), up_pad)
    first = payload(0)
    g_win = _all_gather(first[:1], name="gather_w_in_l0")[0]
    g_mixer, to_gather = None, first[1:]
    h, saves, params = x[0], [], []
    for l in range(L):
        to_gather = to_gather + (payload(l + 1) if l + 1 < L else [])
        h, sv, P, got = _layer_fwd(h, W, l, g_win, g_mixer, to_gather, (shard, up_w, up_pad))
        saves.append(sv)
        params.append(P)
        if l + 1 < L:
            g_win, g_mixer, to_gather = got[0], got[1:], []
    loss_part, g, gb, dnfin = _loss_head(h, norm_final[None], loss_target[0], name="loss_head")

    G, received, pending = [None] * L, {}, []
    for l in reversed(range(L)):
        g, gb, G[l], got, pending = _layer_bwd(g, gb, saves[l], params[l], l, pending, last=l == 0)
        received.update(got)

    rep_grads = {n: jnp.stack([G[l][n] for l in range(L)]) for n in REPLICATED if n != "norm_final"}
    rep_grads["norm_final"] = dnfin[0]
    rep_size = sum(W[n].size for n in REPLICATED)
    rep_rows = _round_up(-(-rep_size // (N_DEV * FLAT_W)), SUBLANE)
    rep_send = _flat_rows([rep_grads[n] for n in REPLICATED], N_DEV * rep_rows).reshape(N_DEV, rep_rows, FLAT_W)
    last = _exchange([rep_send], name="exchange_replicated_grads")

    out = {}
    for n in SHARDED:
        shape = W[n].shape
        as3 = lambda a: a.reshape(L, -1, shape[-1])
        res = _sum_adam([received[(n, l)] for l in range(L)], as3(W[n]), as3(M[n]), as3(V[n]), name="sum_adam_" + n)
        for key, arr in zip(("grad_", "delta_", "new_m_", "new_v_"), res):
            out[key + n] = arr.reshape(shape)

    rep_sum = _sum_parts(last[-1], name="sum_replicated")
    rep_g = _all_gather([rep_sum], name="gather_replicated_grads")[0].reshape(N_DEV * rep_rows, FLAT_W)
    flat_rep = lambda D: _flat_rows([D[n] for n in REPLICATED], N_DEV * rep_rows)
    d_rp, m_rp, v_rp = _adam(rep_g, flat_rep(W), flat_rep(M), flat_rep(V), name="adam_replicated")
    off = 0
    for n in REPLICATED:
        size = W[n].size
        for key, arr in (("grad_", rep_g), ("delta_", d_rp), ("new_m_", m_rp), ("new_v_", v_rp)):
            out[key + n] = arr.reshape(-1)[off:off + size].reshape(W[n].shape)
        off += size

    loss = lax.psum(loss_part[0, 0], ("x", "y", "c"))
    return (loss, g[None], *[out["grad_" + n] for n in WEIGHTS], *[out["delta_" + n] for n in WEIGHTS],
            *[out["new_m_" + n] for n in WEIGHTS], *[out["new_v_" + n] for n in WEIGHTS])
```
